```python
import jax, jax.numpy as jnp
from jax import lax
import numpy as np

D_MODEL = 1024
BATCH = 8
SEQ = 8192
DEPTH = 4

CHUNK = 64
EPS = 1e-5
CONV_WIDTH = D_MODEL
CONV_GROUPS = 16
SHORT_K = 3
SSD_HEAD_DIM = 64
SSD_HEADS = D_MODEL // SSD_HEAD_DIM
SSD_INNER = SSD_HEADS * SSD_HEAD_DIM
SSD_GROUPS = 2
SSD_STATE = 128
SSD_CONV_K = 4
SSD_CONV_DIM = SSD_INNER + 2 * SSD_GROUPS * SSD_STATE
MIX_WIDTH = CONV_WIDTH + SSD_INNER
D_FF = 4 * D_MODEL
IN_COLS = 3 * CONV_WIDTH + SSD_INNER + SSD_CONV_DIM + SSD_HEADS

kernel_name = "hybrid_shortconv_ssd_trunk"


def rmsnorm(x, w):
    xf = x.astype(jnp.float32)
    y = xf * lax.rsqrt(jnp.mean(xf * xf, axis=-1, keepdims=True) + EPS)
    return (y * w.astype(jnp.float32)).astype(x.dtype)


def causal_dwconv(u, w, b=None):
    k, c = w.shape
    y = lax.conv_general_dilated(
        u, w[:, None, :], window_strides=(1,), padding=[(k - 1, 0)],
        dimension_numbers=("NWC", "WIO", "NWC"), feature_group_count=c)
    if b is not None:
        y = y + b
    return y


def short_conv_mixer(u_b, u_c, u_h, conv_w):
    return u_b * causal_dwconv(u_c * u_h, conv_w)


def ssd_scan(xs, dt, a_head, bm, cm):
    f32 = jnp.float32
    b, t, h, p = xs.shape
    g, n = bm.shape[2], bm.shape[3]
    r = h // g
    nc = t // CHUNK
    x_c = (xs.astype(f32) * dt[..., None]).reshape(b, nc, CHUNK, g, r, p)
    a_c = (dt * a_head).reshape(b, nc, CHUNK, g, r)
    b_c = bm.astype(f32).reshape(b, nc, CHUNK, g, n)
    c_c = cm.astype(f32).reshape(b, nc, CHUNK, g, n)
    a_cum = jnp.cumsum(a_c, axis=2)
    causal = jnp.tril(jnp.ones((CHUNK, CHUNK), dtype=bool))
    seg = a_cum[:, :, :, None] - a_cum[:, :, None, :]
    decay = jnp.exp(jnp.where(causal[None, None, :, :, None, None], seg, -jnp.inf))
    scores = jnp.einsum("bclgn,bcsgn->bclsg", c_c, b_c)
    y_diag = jnp.einsum("bclsgr,bcsgrp->bclgrp", scores[..., None] * decay, x_c)
    decay_end = jnp.exp(a_cum[:, :, -1:] - a_cum)
    states = jnp.einsum("bclgn,bclgrp->bcgrpn", b_c, x_c * decay_end[..., None])
    chunk_decay = jnp.exp(a_cum[:, :, -1])

    def step(hs, inp):
        s_c, d_c = inp
        return hs * d_c[..., None, None] + s_c, hs

    h0 = jnp.zeros((b, g, r, p, n), dtype=f32)
    _, prev = lax.scan(step, h0, (jnp.moveaxis(states, 1, 0), jnp.moveaxis(chunk_decay, 1, 0)))
    prev = jnp.moveaxis(prev, 0, 1)
    y_off = jnp.einsum("bclgn,bcgrpn->bclgrp", c_c, prev) * jnp.exp(a_cum)[..., None]
    return (y_diag + y_off).reshape(b, t, h, p)


def ssd_mixer(z, xbc, dt_raw, conv_w, conv_b, dt_bias, a_log, d_skip, norm_w):
    b, t, _ = z.shape
    xbc = jax.nn.silu(causal_dwconv(xbc, conv_w, conv_b))
    xs, bm, cm = jnp.split(xbc, [SSD_INNER, SSD_INNER + SSD_GROUPS * SSD_STATE], axis=-1)
    xs = xs.reshape(b, t, SSD_HEADS, SSD_HEAD_DIM)
    bm = bm.reshape(b, t, SSD_GROUPS, SSD_STATE)
    cm = cm.reshape(b, t, SSD_GROUPS, SSD_STATE)
    dt = jax.nn.softplus(dt_raw.astype(jnp.float32) + dt_bias.astype(jnp.float32))
    a_head = -jnp.exp(a_log.astype(jnp.float32))
    y = ssd_scan(xs, dt, a_head, bm, cm)
    y = y + d_skip.astype(jnp.float32)[:, None] * xs.astype(jnp.float32)
    y = y.reshape(b, t, SSD_INNER).astype(z.dtype)
    gated = (y * jax.nn.silu(z)).reshape(b, t, SSD_GROUPS, SSD_INNER // SSD_GROUPS)
    gated = rmsnorm(gated, norm_w.reshape(SSD_GROUPS, SSD_INNER // SSD_GROUPS))
    return gated.reshape(b, t, SSD_INNER)


SPLITS = list(np.cumsum([CONV_WIDTH, CONV_WIDTH, CONV_WIDTH, SSD_INNER, SSD_CONV_DIM]))


def hybrid_layer(x, norm_mix_w, w_in, short_conv_w, ssd_conv_w, ssd_conv_b, dt_bias,
                 a_log, d_skip, ssd_norm_w, w_out, norm_mlp_w, w_up, w_down):
    h = rmsnorm(x, norm_mix_w)
    proj = jnp.einsum("btd,dc->btc", h, w_in)
    u_b, u_c, u_h, z, xbc, dt_raw = jnp.split(proj, SPLITS, axis=-1)
    y_a = short_conv_mixer(u_b, u_c, u_h, short_conv_w)
    y_b = ssd_mixer(z, xbc, dt_raw, ssd_conv_w, ssd_conv_b, dt_bias, a_log, d_skip, ssd_norm_w)
    y = jnp.concatenate([y_a, y_b], axis=-1)
    x = x + jnp.einsum("btc,cd->btd", y, w_out)
    h = rmsnorm(x, norm_mlp_w)
    hid = jnp.square(jax.nn.relu(jnp.einsum("btd,df->btf", h, w_up)))
    return x + jnp.einsum("btf,fd->btd", hid, w_down)


def _fwd_setup_inputs(seed: int = 0) -> dict:
    key = jax.random.key(seed)
    ks = jax.random.split(key, 16)
    f32 = jnp.float32
    nrm = lambda k, shape, s: jax.random.normal(k, shape, f32) * s
    gain = lambda k, shape: 1.0 + 0.02 * jax.random.normal(k, shape, f32)
    dt0 = jnp.exp(jax.random.uniform(ks[6], (DEPTH, SSD_HEADS), f32, math_log(1e-3), math_log(1e-1)))
    dt_bias = dt0 + jnp.log(-jnp.expm1(-dt0))
    return {
        "x": jax.random.normal(ks[0], (BATCH, SEQ, D_MODEL), f32),
        "norm_mix_w": gain(ks[1], (DEPTH, D_MODEL)),
        "w_in": nrm(ks[2], (DEPTH, D_MODEL, IN_COLS), D_MODEL ** -0.5),
        "short_conv_w": nrm(ks[3], (DEPTH, SHORT_K, CONV_WIDTH), SHORT_K ** -0.5),
        "ssd_conv_w": nrm(ks[4], (DEPTH, SSD_CONV_K, SSD_CONV_DIM), SSD_CONV_K ** -0.5),
        "ssd_conv_b": nrm(ks[5], (DEPTH, SSD_CONV_DIM), 0.02),
        "dt_bias": dt_bias,
        "a_log": jnp.log(jax.random.uniform(ks[7], (DEPTH, SSD_HEADS), f32, 1.0, 16.0)),
        "d_skip": gain(ks[8], (DEPTH, SSD_HEADS)),
        "ssd_norm_w": gain(ks[9], (DEPTH, SSD_INNER)),
        "w_out": nrm(ks[10], (DEPTH, MIX_WIDTH, D_MODEL), MIX_WIDTH ** -0.5),
        "norm_mlp_w": gain(ks[11], (DEPTH, D_MODEL)),
        "w_up": nrm(ks[12], (DEPTH, D_MODEL, D_FF), D_MODEL ** -0.5),
        "w_down": nrm(ks[13], (DEPTH, D_FF, D_MODEL), D_FF ** -0.5),
        "final_norm_w": gain(ks[14], (D_MODEL,)),
    }


def math_log(v):
    return float(np.log(v))


def _fwd_reference(x, norm_mix_w, w_in, short_conv_w, ssd_conv_w, ssd_conv_b, dt_bias, a_log,
              d_skip, ssd_norm_w, w_out, norm_mlp_w, w_up, w_down, final_norm_w):
    for i in range(DEPTH):
        x = hybrid_layer(x, norm_mix_w[i], w_in[i], short_conv_w[i], ssd_conv_w[i],
                         ssd_conv_b[i], dt_bias[i], a_log[i], d_skip[i], ssd_norm_w[i],
                         w_out[i], norm_mlp_w[i], w_up[i], w_down[i])
    return rmsnorm(x, final_norm_w)


import jax as _jax
import jax.numpy as _jnp

TWIN_FORMAT = 'train_step'
FWD_PARAMS = ['x', 'norm_mix_w', 'w_in', 'short_conv_w', 'ssd_conv_w', 'ssd_conv_b', 'dt_bias', 'a_log', 'd_skip', 'ssd_norm_w', 'w_out', 'norm_mlp_w', 'w_up', 'w_down', 'final_norm_w']
TWIN_WEIGHTS = ['norm_mix_w', 'w_in', 'short_conv_w', 'ssd_conv_w', 'ssd_conv_b', 'dt_bias', 'a_log', 'd_skip', 'ssd_norm_w', 'w_out', 'norm_mlp_w', 'w_up', 'w_down', 'final_norm_w']
TWIN_DIFF_INPUT = 'x'
TWIN_INPUTS = ['x', 'norm_mix_w', 'w_in', 'short_conv_w', 'ssd_conv_w', 'ssd_conv_b', 'dt_bias', 'a_log', 'd_skip', 'ssd_norm_w', 'w_out', 'norm_mlp_w', 'w_up', 'w_down', 'final_norm_w', 'loss_target', 'm_norm_mix_w', 'm_w_in', 'm_short_conv_w', 'm_ssd_conv_w', 'm_ssd_conv_b', 'm_dt_bias', 'm_a_log', 'm_d_skip', 'm_ssd_norm_w', 'm_w_out', 'm_norm_mlp_w', 'm_w_up', 'm_w_down', 'm_final_norm_w', 'v_norm_mix_w', 'v_w_in', 'v_short_conv_w', 'v_ssd_conv_w', 'v_ssd_conv_b', 'v_dt_bias', 'v_a_log', 'v_d_skip', 'v_ssd_norm_w', 'v_w_out', 'v_norm_mlp_w', 'v_w_up', 'v_w_down', 'v_final_norm_w']
TWIN_OUTPUTS = ['loss', 'grad_x', 'grad_norm_mix_w', 'grad_w_in', 'grad_short_conv_w', 'grad_ssd_conv_w', 'grad_ssd_conv_b', 'grad_dt_bias', 'grad_a_log', 'grad_d_skip', 'grad_ssd_norm_w', 'grad_w_out', 'grad_norm_mlp_w', 'grad_w_up', 'grad_w_down', 'grad_final_norm_w', 'delta_norm_mix_w', 'delta_w_in', 'delta_short_conv_w', 'delta_ssd_conv_w', 'delta_ssd_conv_b', 'delta_dt_bias', 'delta_a_log', 'delta_d_skip', 'delta_ssd_norm_w', 'delta_w_out', 'delta_norm_mlp_w', 'delta_w_up', 'delta_w_down', 'delta_final_norm_w', 'new_m_norm_mix_w', 'new_m_w_in', 'new_m_short_conv_w', 'new_m_ssd_conv_w', 'new_m_ssd_conv_b', 'new_m_dt_bias', 'new_m_a_log', 'new_m_d_skip', 'new_m_ssd_norm_w', 'new_m_w_out', 'new_m_norm_mlp_w', 'new_m_w_up', 'new_m_w_down', 'new_m_final_norm_w', 'new_v_norm_mix_w', 'new_v_w_in', 'new_v_short_conv_w', 'new_v_ssd_conv_w', 'new_v_ssd_conv_b', 'new_v_dt_bias', 'new_v_a_log', 'new_v_d_skip', 'new_v_ssd_norm_w', 'new_v_w_out', 'new_v_norm_mlp_w', 'new_v_w_up', 'new_v_w_down', 'new_v_final_norm_w']
TWIN_LEAF_KINDS = {'loss': 'loss', 'grad_x': 'grad_x', 'grad_norm_mix_w': 'grad_w', 'grad_w_in': 'grad_w', 'grad_short_conv_w': 'grad_w', 'grad_ssd_conv_w': 'grad_w', 'grad_ssd_conv_b': 'grad_w', 'grad_dt_bias': 'grad_w', 'grad_a_log': 'grad_w', 'grad_d_skip': 'grad_w', 'grad_ssd_norm_w': 'grad_w', 'grad_w_out': 'grad_w', 'grad_norm_mlp_w': 'grad_w', 'grad_w_up': 'grad_w', 'grad_w_down': 'grad_w', 'grad_final_norm_w': 'grad_w', 'delta_norm_mix_w': 'delta_w', 'delta_w_in': 'delta_w', 'delta_short_conv_w': 'delta_w', 'delta_ssd_conv_w': 'delta_w', 'delta_ssd_conv_b': 'delta_w', 'delta_dt_bias': 'delta_w', 'delta_a_log': 'delta_w', 'delta_d_skip': 'delta_w', 'delta_ssd_norm_w': 'delta_w', 'delta_w_out': 'delta_w', 'delta_norm_mlp_w': 'delta_w', 'delta_w_up': 'delta_w', 'delta_w_down': 'delta_w', 'delta_final_norm_w': 'delta_w', 'new_m_norm_mix_w': 'new_m', 'new_m_w_in': 'new_m', 'new_m_short_conv_w': 'new_m', 'new_m_ssd_conv_w': 'new_m', 'new_m_ssd_conv_b': 'new_m', 'new_m_dt_bias': 'new_m', 'new_m_a_log': 'new_m', 'new_m_d_skip': 'new_m', 'new_m_ssd_norm_w': 'new_m', 'new_m_w_out': 'new_m', 'new_m_norm_mlp_w': 'new_m', 'new_m_w_up': 'new_m', 'new_m_w_down': 'new_m', 'new_m_final_norm_w': 'new_m', 'new_v_norm_mix_w': 'new_v', 'new_v_w_in': 'new_v', 'new_v_short_conv_w': 'new_v', 'new_v_ssd_conv_w': 'new_v', 'new_v_ssd_conv_b': 'new_v', 'new_v_dt_bias': 'new_v', 'new_v_a_log': 'new_v', 'new_v_d_skip': 'new_v', 'new_v_ssd_norm_w': 'new_v', 'new_v_w_out': 'new_v', 'new_v_norm_mlp_w': 'new_v', 'new_v_w_up': 'new_v', 'new_v_w_down': 'new_v', 'new_v_final_norm_w': 'new_v'}


def _forward(args):
    return _fwd_reference(*[args[k] for k in FWD_PARAMS])


def _output_shape():
    def fwd():
        inp = _fwd_setup_inputs(0)
        return _fwd_reference(*[inp[k] for k in FWD_PARAMS])
    out = _jax.eval_shape(fwd)
    return out.shape, out.dtype

N_MICROBATCH = 1
ADAM_LR = 0.001
ADAM_B1 = 0.9
ADAM_B2 = 0.999
ADAM_EPS = 1e-08
ADAM_WD = 0.01
ADAM_STEP = 10
PER_EXAMPLE_BATCH_AXIS = {'x': 0, 'loss_target': 0}
SHARED_INPUTS = []
_WEIGHT_DTYPES = {'norm_mix_w': _jnp.float32, 'w_in': _jnp.float32, 'short_conv_w': _jnp.float32, 'ssd_conv_w': _jnp.float32, 'ssd_conv_b': _jnp.float32, 'dt_bias': _jnp.float32, 'a_log': _jnp.float32, 'd_skip': _jnp.float32, 'ssd_norm_w': _jnp.float32, 'w_out': _jnp.float32, 'norm_mlp_w': _jnp.float32, 'w_up': _jnp.float32, 'w_down': _jnp.float32, 'final_norm_w': _jnp.float32}
MOMENT_SCALE = {'norm_mix_w': 2.995554e-01, 'w_in': 1.233117e-01, 'short_conv_w': 1.274932e-01, 'ssd_conv_w': 1.094421e-01, 'ssd_conv_b': 1.516415e-01, 'dt_bias': 4.029215e-01, 'a_log': 9.462588e-01, 'd_skip': 4.660455e-01, 'ssd_norm_w': 1.280568e-01, 'w_out': 1.783356e-01, 'norm_mlp_w': 1.910068e-01, 'w_up': 9.603974e-02, 'w_down': 1.882485e-01, 'final_norm_w': 6.497476e+01}


def _to_microbatches(a, axis):
    t = _jnp.moveaxis(a, axis, 0)
    t = t.reshape((N_MICROBATCH, t.shape[0] // N_MICROBATCH) + t.shape[1:])
    return _jnp.moveaxis(t, 1, axis + 1)


def setup_inputs(seed: int = 0) -> dict:
    inp = _fwd_setup_inputs(seed)
    key = _jax.random.fold_in(_jax.random.key(seed), 7919)
    shape, _ = _output_shape()
    out = dict(inp)
    out["loss_target"] = _jax.random.normal(_jax.random.fold_in(key, 0), shape, _jnp.float32)
    for i, name in enumerate(TWIN_WEIGHTS):
        w = inp[name].astype(_jnp.float32)
        if MOMENT_SCALE is None:
            s = _jnp.sqrt(_jnp.mean(_jnp.square(w)) + 1e-30)
        else:
            s = MOMENT_SCALE[name]
        km, kv = _jax.random.split(_jax.random.fold_in(key, i + 1))
        out[name] = w
        out["m_" + name] = s * _jax.random.normal(km, w.shape, _jnp.float32)
        out["v_" + name] = (s * s) * _jax.random.uniform(kv, w.shape, _jnp.float32, 0.5, 1.5)
    if N_MICROBATCH > 1:
        for name, axis in PER_EXAMPLE_BATCH_AXIS.items():
            out[name] = _to_microbatches(out[name], axis)
    return {'x': out['x'], 'norm_mix_w': out['norm_mix_w'], 'w_in': out['w_in'], 'short_conv_w': out['short_conv_w'], 'ssd_conv_w': out['ssd_conv_w'], 'ssd_conv_b': out['ssd_conv_b'], 'dt_bias': out['dt_bias'], 'a_log': out['a_log'], 'd_skip': out['d_skip'], 'ssd_norm_w': out['ssd_norm_w'], 'w_out': out['w_out'], 'norm_mlp_w': out['norm_mlp_w'], 'w_up': out['w_up'], 'w_down': out['w_down'], 'final_norm_w': out['final_norm_w'], 'loss_target': out['loss_target'], 'm_norm_mix_w': out['m_norm_mix_w'], 'm_w_in': out['m_w_in'], 'm_short_conv_w': out['m_short_conv_w'], 'm_ssd_conv_w': out['m_ssd_conv_w'], 'm_ssd_conv_b': out['m_ssd_conv_b'], 'm_dt_bias': out['m_dt_bias'], 'm_a_log': out['m_a_log'], 'm_d_skip': out['m_d_skip'], 'm_ssd_norm_w': out['m_ssd_norm_w'], 'm_w_out': out['m_w_out'], 'm_norm_mlp_w': out['m_norm_mlp_w'], 'm_w_up': out['m_w_up'], 'm_w_down': out['m_w_down'], 'm_final_norm_w': out['m_final_norm_w'], 'v_norm_mix_w': out['v_norm_mix_w'], 'v_w_in': out['v_w_in'], 'v_short_conv_w': out['v_short_conv_w'], 'v_ssd_conv_w': out['v_ssd_conv_w'], 'v_ssd_conv_b': out['v_ssd_conv_b'], 'v_dt_bias': out['v_dt_bias'], 'v_a_log': out['v_a_log'], 'v_d_skip': out['v_d_skip'], 'v_ssd_norm_w': out['v_ssd_norm_w'], 'v_w_out': out['v_w_out'], 'v_norm_mlp_w': out['v_norm_mlp_w'], 'v_w_up': out['v_w_up'], 'v_w_down': out['v_w_down'], 'v_final_norm_w': out['v_final_norm_w']}


def _loss(weights, diff, rest, loss_target):
    with _jax.named_scope("forward"):
        args = {**rest, TWIN_DIFF_INPUT: diff, **{k: w.astype(_WEIGHT_DTYPES[k]) for k, w in weights.items()}}
        y = _forward(args)
    with _jax.named_scope("loss_head"):
        err = _jnp.square(y.astype(_jnp.float32) - loss_target)
        return 0.5 * _jnp.sum(_jnp.mean(err, axis=-1)) if err.ndim else 0.5 * err


def _adamw(w, g, m, v):
    m = ADAM_B1 * m + (1.0 - ADAM_B1) * g
    v = ADAM_B2 * v + (1.0 - ADAM_B2) * _jnp.square(g)
    m_hat = m / (1.0 - ADAM_B1 ** ADAM_STEP)
    v_hat = v / (1.0 - ADAM_B2 ** ADAM_STEP)
    delta = -ADAM_LR * (m_hat / (_jnp.sqrt(v_hat) + ADAM_EPS) + ADAM_WD * w)
    return delta, m, v


def reference(x, norm_mix_w, w_in, short_conv_w, ssd_conv_w, ssd_conv_b, dt_bias, a_log, d_skip, ssd_norm_w, w_out, norm_mlp_w, w_up, w_down, final_norm_w, loss_target, m_norm_mix_w, m_w_in, m_short_conv_w, m_ssd_conv_w, m_ssd_conv_b, m_dt_bias, m_a_log, m_d_skip, m_ssd_norm_w, m_w_out, m_norm_mlp_w, m_w_up, m_w_down, m_final_norm_w, v_norm_mix_w, v_w_in, v_short_conv_w, v_ssd_conv_w, v_ssd_conv_b, v_dt_bias, v_a_log, v_d_skip, v_ssd_norm_w, v_w_out, v_norm_mlp_w, v_w_up, v_w_down, v_final_norm_w):
    given = dict(x=x, norm_mix_w=norm_mix_w, w_in=w_in, short_conv_w=short_conv_w, ssd_conv_w=ssd_conv_w, ssd_conv_b=ssd_conv_b, dt_bias=dt_bias, a_log=a_log, d_skip=d_skip, ssd_norm_w=ssd_norm_w, w_out=w_out, norm_mlp_w=norm_mlp_w, w_up=w_up, w_down=w_down, final_norm_w=final_norm_w, loss_target=loss_target, m_norm_mix_w=m_norm_mix_w, m_w_in=m_w_in, m_short_conv_w=m_short_conv_w, m_ssd_conv_w=m_ssd_conv_w, m_ssd_conv_b=m_ssd_conv_b, m_dt_bias=m_dt_bias, m_a_log=m_a_log, m_d_skip=m_d_skip, m_ssd_norm_w=m_ssd_norm_w, m_w_out=m_w_out, m_norm_mlp_w=m_norm_mlp_w, m_w_up=m_w_up, m_w_down=m_w_down, m_final_norm_w=m_final_norm_w, v_norm_mix_w=v_norm_mix_w, v_w_in=v_w_in, v_short_conv_w=v_short_conv_w, v_ssd_conv_w=v_ssd_conv_w, v_ssd_conv_b=v_ssd_conv_b, v_dt_bias=v_dt_bias, v_a_log=v_a_log, v_d_skip=v_d_skip, v_ssd_norm_w=v_ssd_norm_w, v_w_out=v_w_out, v_norm_mlp_w=v_norm_mlp_w, v_w_up=v_w_up, v_w_down=v_w_down, v_final_norm_w=v_final_norm_w)
    weights = {n: given[n] for n in TWIN_WEIGHTS}
    shared = {n: given[n] for n in SHARED_INPUTS}
    per_example = {n: given[n] for n in ['x']}
    grad_fn = _jax.value_and_grad(_loss, argnums=(0, 1))

    def one_microbatch(ex, loss_target):
        ex = dict(ex)
        diff = ex.pop(TWIN_DIFF_INPUT)
        return grad_fn(weights, diff, {**shared, **ex}, loss_target)

    if N_MICROBATCH == 1:
        loss, (grad_w, grad_x) = one_microbatch(per_example, given["loss_target"])
    else:
        def body(carry, xs):
            loss_sum, grad_sum = carry
            l_k, (gw_k, gx_k) = one_microbatch(xs[0], xs[1])
            with _jax.named_scope("update"):
                return (loss_sum + l_k, _jax.tree.map(_jnp.add, grad_sum, gw_k)), gx_k

        init = (_jnp.zeros((), _jnp.float32), _jax.tree.map(_jnp.zeros_like, weights))
        (loss, grad_w), grad_x = _jax.lax.scan(body, init, (per_example, given["loss_target"]))
    with _jax.named_scope("update"):
        delta_w, new_m, new_v = {}, {}, {}
        for n in TWIN_WEIGHTS:
            delta_w[n], new_m[n], new_v[n] = _adamw(weights[n], grad_w[n], given["m_" + n], given["v_" + n])
    return (loss, grad_x, *[grad_w[n] for n in TWIN_WEIGHTS], *[delta_w[n] for n in TWIN_WEIGHTS],
            *[new_m[n] for n in TWIN_WEIGHTS], *[new_v[n] for n in TWIN_WEIGHTS])
```

```python
import functools

import numpy as np
import jax
import jax.numpy as jnp
from jax import lax
from jax.experimental import pallas as pl
from jax.experimental.pallas import tpu as pltpu

F32 = jnp.float32
BF16 = jnp.bfloat16
SDS = jax.ShapeDtypeStruct

N_DEV = 8
DEPTH = 4
D = 1024
NIN = 5648
NINP = 5760
DFF = 4096
MIX = 2048
NHEAD = 16
HDIM = 64
NSTATE = 128
CHUNK = 64
XBC = 1536
EPS = 1e-5
LANES = 128

C_UB, C_UC, C_UH, C_Z, C_XS, C_BC, C_DT = 0, 1024, 2048, 3072, 4096, 5120, 5632

ADAM_LR = 0.001
ADAM_B1 = 0.9
ADAM_B2 = 0.999
ADAM_EPS = 1e-08
ADAM_WD = 0.01
ADAM_STEP = 10

VMEM_LIMIT = 56 * 1024 * 1024
MESH = pl.DeviceIdType.MESH


def _cparams(sem):
    return pltpu.CompilerParams(dimension_semantics=sem, vmem_limit_bytes=VMEM_LIMIT)


def _nt(a, b):
    return lax.dot_general(a, b, (((1,), (1,)), ((), ())), preferred_element_type=F32)


def _tn(a, b):
    return lax.dot_general(a, b, (((0,), (0,)), ((), ())), preferred_element_type=F32)


def _nn(a, b):
    return jnp.dot(a, b, preferred_element_type=F32)


def _sigmoid(v):
    return 1.0 / (1.0 + jnp.exp(-v))


def _split3(v):
    v1 = v.astype(BF16)
    r1 = v - v1.astype(F32)
    v2 = r1.astype(BF16)
    v3 = (r1 - v2.astype(F32)).astype(BF16)
    return v1, v2, v3


def _expand(v, eh):
    v1, v2, v3 = _split3(v)
    return _nn(v1, eh) + _nn(v2, eh) + _nn(v3, eh)


def _head_reduce(v, eht):
    v1, v2, v3 = _split3(v)
    return _nn(v1, eht) + _nn(v2, eht) + _nn(v3, eht)


def _head_matrices():
    eh = np.zeros((LANES, D), np.float32)
    for h in range(NHEAD):
        eh[h, h * HDIM:(h + 1) * HDIM] = 1.0
    return jnp.asarray(eh, BF16), jnp.asarray(eh.T.copy(), BF16)


def norm_matmul(x, nw, w, w_spec, n_tiles, tn, name):
    t_len = x.shape[0]
    tm = min(512, t_len)

    def body(x_ref, nw_ref, w_ref, o_ref, h_ref, hs):
        @pl.when(pl.program_id(1) == 0)
        def _():
            xv = x_ref[...]
            r = lax.rsqrt(jnp.mean(xv * xv, axis=-1, keepdims=True) + EPS)
            hv = (xv * r * nw_ref[...]).astype(BF16)
            hs[...] = hv
            h_ref[...] = hv
        o_ref[...] = _nn(hs[...], w_ref[...]).astype(o_ref.dtype)

    return pl.pallas_call(
        body, grid=(t_len // tm, n_tiles),
        in_specs=[pl.BlockSpec((tm, D), lambda i, j: (i, 0)),
                  pl.BlockSpec((1, D), lambda i, j: (0, 0)),
                  w_spec],
        out_specs=[pl.BlockSpec((tm, tn), lambda i, j: (i, j)),
                   pl.BlockSpec((tm, D), lambda i, j: (i, 0))],
        out_shape=[SDS((t_len, n_tiles * tn), BF16), SDS((t_len, D), BF16)],
        scratch_shapes=[pltpu.VMEM((tm, D), BF16)],
        compiler_params=_cparams(("parallel", "arbitrary")), name=name)(x, nw, w)


def matmul_residual(a, w, res, relu2, name):
    t_len, k_len = a.shape
    tm = min(512, t_len)

    def body(a_ref, w_ref, res_ref, o_ref):
        av = a_ref[...]
        if relu2:
            af = jnp.maximum(av.astype(F32), 0.0)
            av = (af * af).astype(BF16)
        o_ref[...] = res_ref[...] + _nn(av, w_ref[...])

    return pl.pallas_call(
        body, grid=(t_len // tm,),
        in_specs=[pl.BlockSpec((tm, k_len), lambda i: (i, 0)),
                  pl.BlockSpec((k_len, D), lambda i: (0, 0)),
                  pl.BlockSpec((tm, D), lambda i: (i, 0))],
        out_specs=pl.BlockSpec((tm, D), lambda i: (i, 0)),
        out_shape=SDS((t_len, D), F32),
        compiler_params=_cparams(("parallel",)), name=name)(a, w, res)


def matmul_nt_act(dy, w, u, name):
    t_len = dy.shape[0]
    n_len = w.shape[0]
    tm = min(512, t_len)
    tn = 1024

    def body(dy_ref, w_ref, *rest):
        if u is None:
            (o_ref,) = rest
        else:
            u_ref, o_ref = rest
        p = _nt(dy_ref[...], w_ref[...])
        if u is not None:
            p = p * (2.0 * jnp.maximum(u_ref[...].astype(F32), 0.0))
        o_ref[...] = p.astype(o_ref.dtype)

    in_specs = [pl.BlockSpec((tm, D), lambda i, j: (i, 0)),
                pl.BlockSpec((tn, D), lambda i, j: (j, 0))]
    args = [dy, w]
    if u is not None:
        in_specs.append(pl.BlockSpec((tm, tn), lambda i, j: (i, j)))
        args.append(u)
    return pl.pallas_call(
        body, grid=(t_len // tm, n_len // tn),
        in_specs=in_specs,
        out_specs=pl.BlockSpec((tm, tn), lambda i, j: (i, j)),
        out_shape=SDS((t_len, n_len), BF16),
        compiler_params=_cparams(("parallel", "parallel")), name=name)(*args)


def matmul_tn(a, b, a_spec, b_spec, o_spec, o_shape, n_out, relu2, name):
    t_len = a.shape[0]
    tt = min(512, t_len)
    nt = t_len // tt

    def body(a_ref, b_ref, o_ref, acc):
        t = pl.program_id(1)
        av = a_ref[...]
        if relu2:
            af = jnp.maximum(av.astype(F32), 0.0)
            av = (af * af).astype(BF16)
        p = _tn(av, b_ref[...])

        @pl.when(t == 0)
        def _():
            acc[...] = p

        @pl.when(t > 0)
        def _():
            acc[...] += p

        @pl.when(t == nt - 1)
        def _():
            o_ref[...] = acc[...].astype(o_ref.dtype)

    blk = tuple(s for s in o_spec.block_shape if s is not None)
    return pl.pallas_call(
        body, grid=(n_out, nt),
        in_specs=[a_spec(tt), b_spec(tt)],
        out_specs=o_spec, out_shape=o_shape,
        scratch_shapes=[pltpu.VMEM(blk, F32)],
        compiler_params=_cparams(("parallel", "arbitrary")), name=name)(a, b)


def matmul_nt_norm_bwd(dy, w, w_spec, tk, nk, x, nw, dres, name):
    t_len = x.shape[0]
    tm = min(512, t_len)
    nt = t_len // tm

    def body(dy_ref, w_ref, x_ref, nw_ref, dres_ref, dx_ref, dxb_ref, dnw_ref, acc):
        i, k = pl.program_id(0), pl.program_id(1)
        p = _nt(dy_ref[...], w_ref[...])

        @pl.when(k == 0)
        def _():
            acc[...] = p

        @pl.when(k > 0)
        def _():
            acc[...] += p

        @pl.when(jnp.logical_and(i == 0, k == 0))
        def _():
            dnw_ref[...] = jnp.zeros_like(dnw_ref)

        @pl.when(k == nk - 1)
        def _():
            dh = acc[...]
            xv = x_ref[...]
            r = lax.rsqrt(jnp.mean(xv * xv, axis=-1, keepdims=True) + EPS)
            xh = xv * r
            dnw_ref[0:1, :] += jnp.sum(dh * xh, axis=0, keepdims=True)
            g = dh * nw_ref[...]
            dx = dres_ref[...] + r * (g - xh * jnp.mean(g * xh, axis=-1, keepdims=True))
            dx_ref[...] = dx
            dxb_ref[...] = dx.astype(BF16)

    return pl.pallas_call(
        body, grid=(nt, nk),
        in_specs=[pl.BlockSpec((tm, tk), lambda i, k: (i, k)),
                  w_spec,
                  pl.BlockSpec((tm, D), lambda i, k: (i, 0)),
                  pl.BlockSpec((1, D), lambda i, k: (0, 0)),
                  pl.BlockSpec((tm, D), lambda i, k: (i, 0))],
        out_specs=[pl.BlockSpec((tm, D), lambda i, k: (i, 0)),
                   pl.BlockSpec((tm, D), lambda i, k: (i, 0)),
                   pl.BlockSpec((8, D), lambda i, k: (0, 0))],
        out_shape=[SDS((t_len, D), F32), SDS((t_len, D), BF16), SDS((8, D), F32)],
        scratch_shapes=[pltpu.VMEM((tm, D), F32)],
        compiler_params=_cparams(("arbitrary", "arbitrary")), name=name)(dy, w, x, nw, dres)


def loss_head(x, fw, tgt):
    t_len = x.shape[0]
    tm = min(512, t_len)

    def body(x_ref, fw_ref, t_ref, loss_ref, dx_ref, dxb_ref, dfw_ref):
        @pl.when(pl.program_id(0) == 0)
        def _():
            loss_ref[...] = jnp.zeros_like(loss_ref)
            dfw_ref[...] = jnp.zeros_like(dfw_ref)
        xv = x_ref[...]
        r = lax.rsqrt(jnp.mean(xv * xv, axis=-1, keepdims=True) + EPS)
        xh = xv * r
        w = fw_ref[...]
        e = xh * w - t_ref[...]
        row = jnp.sum(e * e, axis=-1, keepdims=True) * (1.0 / D)
        loss_ref[...] += 0.5 * jnp.sum(row, axis=0, keepdims=True)
        dyf = e * (1.0 / D)
        dfw_ref[0:1, :] += jnp.sum(dyf * xh, axis=0, keepdims=True)
        g = dyf * w
        dx = r * (g - xh * jnp.mean(g * xh, axis=-1, keepdims=True))
        dx_ref[...] = dx
        dxb_ref[...] = dx.astype(BF16)

    return pl.pallas_call(
        body, grid=(t_len // tm,),
        in_specs=[pl.BlockSpec((tm, D), lambda i: (i, 0)),
                  pl.BlockSpec((1, D), lambda i: (0, 0)),
                  pl.BlockSpec((tm, D), lambda i: (i, 0))],
        out_specs=[pl.BlockSpec((8, LANES), lambda i: (0, 0)),
                   pl.BlockSpec((tm, D), lambda i: (i, 0)),
                   pl.BlockSpec((tm, D), lambda i: (i, 0)),
                   pl.BlockSpec((8, D), lambda i: (0, 0))],
        out_shape=[SDS((8, LANES), F32), SDS((t_len, D), F32), SDS((t_len, D), BF16), SDS((8, D), F32)],
        compiler_params=_cparams(("arbitrary",)), name="loss_head")(x, fw, tgt)


def _shift_dn(x, halo, j):
    if j == 0:
        return x
    xr = pltpu.roll(x, j, 0)
    hr = pltpu.roll(halo, j, 0)
    row = lax.broadcasted_iota(jnp.int32, hr.shape, 0)
    top = jnp.where(row < j, hr, xr[0:8])
    return jnp.concatenate([top, xr[8:]], axis=0)


def _shift_up(x, nxt, j):
    if j == 0:
        return x
    n = x.shape[0]
    xr = pltpu.roll(x, n - j, 0)
    hr = pltpu.roll(nxt, 8 - j, 0)
    row = lax.broadcasted_iota(jnp.int32, hr.shape, 0)
    bot = jnp.where(row >= 8 - j, hr, xr[n - 8:n])
    return jnp.concatenate([xr[:n - 8], bot], axis=0)


def _conv_fwd(x, halo, w_ref, kw):
    acc = None
    for k in range(kw):
        term = w_ref[k:k + 1, :] * _shift_dn(x, halo, kw - 1 - k)
        acc = term if acc is None else acc + term
    return acc


def _chunk_cumsum(a, pos):
    for sh in (1, 2, 4, 8, 16, 32):
        a = a + jnp.where(pos >= sh, pltpu.roll(a, sh, 0), 0.0)
    return a


def _chunk_rcumsum(a, pos):
    n = a.shape[0]
    for sh in (1, 2, 4, 8, 16, 32):
        a = a + jnp.where(pos < CHUNK - sh, pltpu.roll(a, n - sh, 0), 0.0)
    return a


def _softplus(v):
    return jnp.maximum(v, 0.0) + jnp.log(1.0 + jnp.exp(-jnp.abs(v)))


def _silu(v):
    return v * _sigmoid(v)


def _dsilu(v):
    s = _sigmoid(v)
    return s * (1.0 + v * (1.0 - s))


def _lane_masks(width=D):
    lane = lax.broadcasted_iota(jnp.int32, (CHUNK, width), 1) & (HDIM - 1)
    row = lax.broadcasted_iota(jnp.int32, (CHUNK, width), 0)
    return lane == row, lane <= row


def _rep_matrix():
    lane = lax.broadcasted_iota(jnp.int32, (CHUNK, 512), 1) & (HDIM - 1)
    row = lax.broadcasted_iota(jnp.int32, (CHUNK, 512), 0)
    return jnp.where(lane == row, 1.0, 0.0).astype(BF16)


def _blockdiag(xp):
    lane = lax.broadcasted_iota(jnp.int32, xp.shape, 1)
    zero = jnp.zeros_like(xp)
    return jnp.concatenate([jnp.where(lane < HDIM, xp, zero), jnp.where(lane >= HDIM, xp, zero)], axis=0)


def _mixer_views(tt):
    r8 = tt // 8

    def main(width, col):
        return pl.BlockSpec((tt, width), lambda i, c=col // width: (i, c))

    def halo(width, col):
        return pl.BlockSpec((8, width), lambda i, c=col // width: (jnp.maximum(i * r8 - 1, 0), c))

    return main, halo


def mixer_fwd(proj, prm, tt):
    t_len = proj.shape[0]
    nblk = t_len // tt
    nc = tt // CHUNK
    main, halo = _mixer_views(tt)

    def body(ub_ref, uc_ref, uh_ref, z_ref, xr_ref, bcr_ref, dtr_ref, uch_ref, uhh_ref, xrh_ref, bcrh_ref,
             scw_ref, cwx_ref, cwbc_ref, cbx_ref, cbbc_ref, dtb_ref, alog_ref, dsk_ref, nrm_ref, eh_ref,
             y_ref, st_ref, hs, xs_s, bc_s, dtx_s, cumx_s, yssd_s):
        i = pl.program_id(0)
        first = i == 0

        @pl.when(first)
        def _():
            hs[...] = jnp.zeros_like(hs)

        keep = jnp.where(first, 0.0, 1.0)
        v = uc_ref[...].astype(F32) * uh_ref[...].astype(F32)
        vh = uch_ref[...].astype(F32) * uhh_ref[...].astype(F32) * keep
        y_ref[:, 0:D] = (ub_ref[...].astype(F32) * _conv_fwd(v, vh, scw_ref, 3)).astype(BF16)

        xs = _silu(_conv_fwd(xr_ref[...].astype(F32), xrh_ref[...].astype(F32) * keep, cwx_ref, 4) + cbx_ref[...])
        xs_s[...] = xs
        bc_s[...] = _silu(_conv_fwd(bcr_ref[...].astype(F32), bcrh_ref[...].astype(F32) * keep, cwbc_ref, 4)
                          + cbbc_ref[...])
        dt = _softplus(dtr_ref[...].astype(F32) + dtb_ref[...])
        a_neg = -jnp.exp(alog_ref[...])
        pos = lax.broadcasted_iota(jnp.int32, (tt, LANES), 0) & (CHUNK - 1)
        cum = _chunk_cumsum(dt * a_neg, pos)
        eh = eh_ref[...]
        dtx_s[...] = _expand(dt, eh)
        cumx_s[...] = _expand(cum, eh)
        irep, causal = _lane_masks()
        rep = _rep_matrix()

        def chunk(c, carry):
            r0 = pl.multiple_of(c * CHUNK, CHUNK)
            rows = pl.ds(r0, CHUNK)
            cumx = cumx_s[rows, :]
            cum_l = cumx[CHUNK - 1:CHUNK, :]
            xd = xs_s[rows, :] * dtx_s[rows, :]
            xf = xd * jnp.exp(cum_l - cumx)
            ex = jnp.exp(cumx)
            e_l = jnp.exp(cum_l)
            rvec = jnp.sum(jnp.where(irep, cumx, 0.0), axis=0, keepdims=True)
            lam = jnp.where(causal, jnp.exp(jnp.where(causal, cumx - rvec, 0.0)), 0.0)
            bc = bc_s[rows, :]
            for g in range(2):
                gs = slice(g * 512, (g + 1) * 512)
                bg = bc[:, g * NSTATE:(g + 1) * NSTATE].astype(BF16)
                cg = bc[:, 256 + g * NSTATE:256 + (g + 1) * NSTATE].astype(BF16)
                s_rep = _nn(_nt(cg, bg).astype(BF16), rep)
                m_g = (s_rep * lam[:, gs]).astype(BF16)
                h_g = hs[:, gs]
                h_b = h_g.astype(BF16)
                st_ref[c, :, gs] = h_b
                yo = _nn(cg, h_b) * ex[:, gs]
                xd_b = xd[:, gs].astype(BF16)
                for hp in range(4):
                    ps = slice(hp * LANES, (hp + 1) * LANES)
                    yd = _nn(m_g[:, ps], _blockdiag(xd_b[:, ps]))
                    yssd_s[rows, g * 512 + hp * LANES:g * 512 + (hp + 1) * LANES] = yd + yo[:, ps]
                hs[:, gs] = h_g * e_l[:, gs] + _tn(bg, xf[:, gs].astype(BF16))
            return carry

        lax.fori_loop(0, nc, chunk, 0)

        ys = yssd_s[...] + dsk_ref[...] * xs_s[...]
        gt = ys * _silu(z_ref[...].astype(F32))
        for g in range(2):
            gs = slice(g * 512, (g + 1) * 512)
            gg = gt[:, gs]
            rn = lax.rsqrt(jnp.mean(gg * gg, axis=-1, keepdims=True) + EPS)
            y_ref[:, D + g * 512:D + (g + 1) * 512] = (gg * rn * nrm_ref[:, gs]).astype(BF16)

    def const(shape):
        return pl.BlockSpec(shape, lambda i: (0, 0))

    in_specs = [main(D, C_UB), main(D, C_UC), main(D, C_UH), main(D, C_Z), main(D, C_XS), main(512, C_BC),
                main(LANES, C_DT), halo(D, C_UC), halo(D, C_UH), halo(D, C_XS), halo(512, C_BC),
                const((8, D)), const((8, D)), const((8, 512)), const((1, D)), const((1, 512)),
                const((1, LANES)), const((1, LANES)), const((1, D)), const((1, D)), const((LANES, D))]
    return pl.pallas_call(
        body, grid=(nblk,),
        in_specs=in_specs,
        out_specs=[pl.BlockSpec((tt, MIX), lambda i: (i, 0)),
                   pl.BlockSpec((nc, NSTATE, D), lambda i: (i, 0, 0))],
        out_shape=[SDS((t_len, MIX), BF16), SDS((t_len // CHUNK, NSTATE, D), BF16)],
        scratch_shapes=[pltpu.VMEM((NSTATE, D), F32), pltpu.VMEM((tt, D), F32), pltpu.VMEM((tt, 512), F32),
                        pltpu.VMEM((tt, D), F32), pltpu.VMEM((tt, D), F32), pltpu.VMEM((tt, D), F32)],
        compiler_params=_cparams(("arbitrary",)), name="mixer_fwd")(
            *([proj] * 11), prm["scw"], prm["cwx"], prm["cwbc"], prm["cbx"], prm["cbbc"], prm["dtb"],
            prm["alog"], prm["dskx"], prm["nrm"], prm["eh"])


def mixer_bwd(proj, dy, states, prm, tt):
    t_len = proj.shape[0]
    nblk = t_len // tt
    nc = tt // CHUNK
    r8 = tt // 8

    def rev(i):
        return nblk - 1 - i

    def main(width, col):
        return pl.BlockSpec((tt, width), lambda i, c=col // width: (rev(i), c))

    def halo(width, col):
        return pl.BlockSpec((8, width), lambda i, c=col // width: (jnp.maximum(rev(i) * r8 - 1, 0), c))

    def body(ub_ref, uc_ref, uh_ref, z_ref, xr_ref, bcr_ref, dtr_ref, uch_ref, uhh_ref, xrh_ref, bcrh_ref,
             dy_ref, st_ref,
             scw_ref, cwx_ref, cwbc_ref, cbx_ref, cbbc_ref, dtb_ref, alog_ref, dsk_ref, nrm_ref, eh_ref, eht_ref,
             dp_ref, gscw_ref, gcwx_ref, gcwbc_ref, gvec_ref, gdt_ref,
             dhs, xs_s, bc_s, dtx_s, cumx_s, dys_s, dxs_s, dbc_s, red_s, ddtx_s, nx_cv, nx_px, nx_pbc):
        i = pl.program_id(0)
        blk = rev(i)

        @pl.when(i == 0)
        def _():
            dhs[...] = jnp.zeros_like(dhs)
            nx_cv[...] = jnp.zeros_like(nx_cv)
            nx_px[...] = jnp.zeros_like(nx_px)
            nx_pbc[...] = jnp.zeros_like(nx_pbc)
            gscw_ref[...] = jnp.zeros_like(gscw_ref)
            gcwx_ref[...] = jnp.zeros_like(gcwx_ref)
            gcwbc_ref[...] = jnp.zeros_like(gcwbc_ref)
            gvec_ref[...] = jnp.zeros_like(gvec_ref)
            gdt_ref[...] = jnp.zeros_like(gdt_ref)

        keep = jnp.where(blk == 0, 0.0, 1.0)

        ub = ub_ref[...].astype(F32)
        uc = uc_ref[...].astype(F32)
        uh = uh_ref[...].astype(F32)
        v = uc * uh
        vh = uch_ref[...].astype(F32) * uhh_ref[...].astype(F32) * keep
        dya = dy_ref[:, 0:D].astype(F32)
        dp_ref[:, C_UB:C_UB + D] = (dya * _conv_fwd(v, vh, scw_ref, 3)).astype(BF16)
        dcv = dya * ub
        nxt = nx_cv[...]
        dv = None
        for k in range(3):
            gscw_ref[k:k + 1, :] += jnp.sum(dcv * _shift_dn(v, vh, 2 - k), axis=0, keepdims=True)
            term = scw_ref[k:k + 1, :] * _shift_up(dcv, nxt, 2 - k)
            dv = term if dv is None else dv + term
        nx_cv[...] = dcv[0:8]
        dp_ref[:, C_UC:C_UC + D] = (dv * uh).astype(BF16)
        dp_ref[:, C_UH:C_UH + D] = (dv * uc).astype(BF16)

        xraw = xr_ref[...].astype(F32)
        xrh = xrh_ref[...].astype(F32) * keep
        bcraw = bcr_ref[...].astype(F32)
        bcrh = bcrh_ref[...].astype(F32) * keep
        pre_x = _conv_fwd(xraw, xrh, cwx_ref, 4) + cbx_ref[...]
        pre_bc = _conv_fwd(bcraw, bcrh, cwbc_ref, 4) + cbbc_ref[...]
        xs = _silu(pre_x)
        xs_s[...] = xs
        bc_s[...] = _silu(pre_bc)
        dt_pre = dtr_ref[...].astype(F32) + dtb_ref[...]
        dt = _softplus(dt_pre)
        a_neg = -jnp.exp(alog_ref[...])
        pos = lax.broadcasted_iota(jnp.int32, (tt, LANES), 0) & (CHUNK - 1)
        cum = _chunk_cumsum(dt * a_neg, pos)
        eh = eh_ref[...]
        eht = eht_ref[...]
        dtx_s[...] = _expand(dt, eh)
        cumx_s[...] = _expand(cum, eh)

        irep, causal = _lane_masks()
        irep_g, _ = _lane_masks(512)
        rep = _rep_matrix()
        row64 = lax.broadcasted_iota(jnp.int32, (CHUNK, 512), 0)
        lane128 = lax.broadcasted_iota(jnp.int32, (CHUNK, LANES), 1)

        def fwd_chunk(c, carry):
            r0 = pl.multiple_of(c * CHUNK, CHUNK)
            rows = pl.ds(r0, CHUNK)
            cumx = cumx_s[rows, :]
            xd = xs_s[rows, :] * dtx_s[rows, :]
            ex = jnp.exp(cumx)
            rvec = jnp.sum(jnp.where(irep, cumx, 0.0), axis=0, keepdims=True)
            lam = jnp.where(causal, jnp.exp(jnp.where(causal, cumx - rvec, 0.0)), 0.0)
            bc = bc_s[rows, :]
            for g in range(2):
                gs = slice(g * 512, (g + 1) * 512)
                bg = bc[:, g * NSTATE:(g + 1) * NSTATE].astype(BF16)
                cg = bc[:, 256 + g * NSTATE:256 + (g + 1) * NSTATE].astype(BF16)
                s_rep = _nn(_nt(cg, bg).astype(BF16), rep)
                m_g = (s_rep * lam[:, gs]).astype(BF16)
                yo = _nn(cg, st_ref[c, :, gs]) * ex[:, gs]
                xd_b = xd[:, gs].astype(BF16)
                for hp in range(4):
                    ps = slice(hp * LANES, (hp + 1) * LANES)
                    yd = _nn(m_g[:, ps], _blockdiag(xd_b[:, ps]))
                    dys_s[rows, g * 512 + hp * LANES:g * 512 + (hp + 1) * LANES] = yd + yo[:, ps]
            return carry

        lax.fori_loop(0, nc, fwd_chunk, 0)

        z = z_ref[...].astype(F32)
        sz = _silu(z)
        ys = dys_s[...] + dsk_ref[...] * xs
        gt = ys * sz
        dyb = dy_ref[:, D:MIX].astype(F32)
        for g in range(2):
            gs = slice(g * 512, (g + 1) * 512)
            gg = gt[:, gs]
            rn = lax.rsqrt(jnp.mean(gg * gg, axis=-1, keepdims=True) + EPS)
            gvec_ref[0:1, gs] += jnp.sum(dyb[:, gs] * gg * rn, axis=0, keepdims=True)
            dgn = dyb[:, gs] * nrm_ref[:, gs]
            dgt = rn * (dgn - gg * (rn * rn) * jnp.mean(dgn * gg, axis=-1, keepdims=True))
            dys = dgt * sz[:, gs]
            dys_s[:, gs] = dys
            dp_ref[:, C_Z + g * 512:C_Z + (g + 1) * 512] = (dgt * ys[:, gs] * _dsilu(z[:, gs])).astype(BF16)
        dys_all = dys_s[...]
        gvec_ref[1:2, :] += jnp.sum(dys_all * xs, axis=0, keepdims=True)

        def bwd_chunk(cc, carry):
            c = nc - 1 - cc
            r0 = pl.multiple_of(c * CHUNK, CHUNK)
            rows = pl.ds(r0, CHUNK)
            cumx = cumx_s[rows, :]
            cum_l = cumx[CHUNK - 1:CHUNK, :]
            xs_c = xs_s[rows, :]
            dtx = dtx_s[rows, :]
            xd = xs_c * dtx
            f = jnp.exp(cum_l - cumx)
            xf = xd * f
            ex = jnp.exp(cumx)
            e_l = jnp.exp(cum_l)
            rvec = jnp.sum(jnp.where(irep, cumx, 0.0), axis=0, keepdims=True)
            lam = jnp.where(causal, jnp.exp(jnp.where(causal, cumx - rvec, 0.0)), 0.0)
            bc = bc_s[rows, :]
            dyc = dys_s[rows, :]
            for g in range(2):
                gs = slice(g * 512, (g + 1) * 512)
                bg = bc[:, g * NSTATE:(g + 1) * NSTATE].astype(BF16)
                cg = bc[:, 256 + g * NSTATE:256 + (g + 1) * NSTATE].astype(BF16)
                h0 = st_ref[c, :, gs]
                dh = dhs[:, gs]
                dh_b = dh.astype(BF16)
                xf_g = xf[:, gs]
                dxf = _nn(bg, dh_b)
                db = _nt(xf_g.astype(BF16), dh_b)
                s_rep = _nn(_nt(cg, bg).astype(BF16), rep)
                lam_g = lam[:, gs]
                m_g = s_rep * lam_g
                m_b = m_g.astype(BF16)
                ex_g = ex[:, gs]
                dy_g = dyc[:, gs]
                yo = _nn(cg, h0) * ex_g
                dg_b = (dy_g * ex_g).astype(BF16)
                dc = _nt(dg_b, h0)
                el_g = e_l[:, gs]
                dee = jnp.sum(dh * h0.astype(F32), axis=0, keepdims=True) * el_g
                dhs[:, gs] = dh * el_g + _tn(cg, dg_b)
                xd_b = xd[:, gs].astype(BF16)
                dy_b = dy_g.astype(BF16)
                dm_parts, dxd_parts = [], []
                for hp in range(4):
                    ps = slice(hp * LANES, (hp + 1) * LANES)
                    bd = _blockdiag(xd_b[:, ps])
                    dm_parts.append(_nt(dy_b[:, ps], bd))
                    t2 = _tn(m_b[:, ps], dy_b[:, ps])
                    dxd_parts.append(jnp.where(lane128 < HDIM, t2[0:CHUNK], t2[CHUNK:2 * CHUNK]))
                dm = jnp.concatenate(dm_parts, axis=1)
                dxd = jnp.concatenate(dxd_parts, axis=1) + dxf * f[:, gs]
                dseg = dm * m_g
                ds_b = _nt((dm * lam_g).astype(BF16), rep).astype(BF16)
                dc = dc + _nn(ds_b, bg)
                db = db + _tn(ds_b, cg)
                colsum = jnp.sum(dseg, axis=0, keepdims=True)
                dxfxf = dxf * xf_g
                red = dseg - jnp.where(irep_g, colsum, 0.0) + dy_g * yo - dxfxf
                last = jnp.sum(dxfxf, axis=0, keepdims=True) + dee
                red = red + jnp.where(row64 == CHUNK - 1, last, 0.0)
                red_s[rows, gs] = red
                ddtx_s[rows, gs] = dxd * xs_c[:, gs]
                dxs_s[rows, gs] = dxd * dtx[:, gs] + dsk_ref[:, gs] * dy_g
                dbc_s[rows, g * NSTATE:(g + 1) * NSTATE] = db
                dbc_s[rows, 256 + g * NSTATE:256 + (g + 1) * NSTATE] = dc
            return carry

        lax.fori_loop(0, nc, bwd_chunk, 0)

        dcum = _head_reduce(red_s[...], eht)
        da = _chunk_rcumsum(dcum, pos)
        ddt = _head_reduce(ddtx_s[...], eht) + da * a_neg
        gdt_ref[1:2, :] += jnp.sum(da * dt, axis=0, keepdims=True) * a_neg
        ddt_raw = ddt * _sigmoid(dt_pre)
        lane_t = lax.broadcasted_iota(jnp.int32, (tt, LANES), 1)
        ddt_raw = jnp.where(lane_t < NHEAD, ddt_raw, 0.0)
        gdt_ref[0:1, :] += jnp.sum(ddt_raw, axis=0, keepdims=True)
        dp_ref[:, C_DT:C_DT + LANES] = ddt_raw.astype(BF16)

        dpx = dxs_s[...] * _dsilu(pre_x)
        dpbc = dbc_s[...] * _dsilu(pre_bc)
        gvec_ref[2:3, :] += jnp.sum(dpx, axis=0, keepdims=True)
        gcwbc_ref[4:5, :] += jnp.sum(dpbc, axis=0, keepdims=True)
        nxt_x = nx_px[...]
        nxt_bc = nx_pbc[...]
        dxr, dbcr = None, None
        for k in range(4):
            gcwx_ref[k:k + 1, :] += jnp.sum(dpx * _shift_dn(xraw, xrh, 3 - k), axis=0, keepdims=True)
            gcwbc_ref[k:k + 1, :] += jnp.sum(dpbc * _shift_dn(bcraw, bcrh, 3 - k), axis=0, keepdims=True)
            tx = cwx_ref[k:k + 1, :] * _shift_up(dpx, nxt_x, 3 - k)
            tb = cwbc_ref[k:k + 1, :] * _shift_up(dpbc, nxt_bc, 3 - k)
            dxr = tx if dxr is None else dxr + tx
            dbcr = tb if dbcr is None else dbcr + tb
        nx_px[...] = dpx[0:8]
        nx_pbc[...] = dpbc[0:8]
        dp_ref[:, C_XS:C_XS + D] = dxr.astype(BF16)
        dp_ref[:, C_BC:C_BC + 512] = dbcr.astype(BF16)

        @pl.when(i == nblk - 1)
        def _():
            gdt_ref[2:3, :] = _head_reduce(gvec_ref[1:2, :] * jnp.ones((8, 1), F32), eht)[0:1, :]

    def const(shape):
        return pl.BlockSpec(shape, lambda i: (0, 0))

    in_specs = [main(D, C_UB), main(D, C_UC), main(D, C_UH), main(D, C_Z), main(D, C_XS), main(512, C_BC),
                main(LANES, C_DT), halo(D, C_UC), halo(D, C_UH), halo(D, C_XS), halo(512, C_BC),
                pl.BlockSpec((tt, MIX), lambda i: (rev(i), 0)),
                pl.BlockSpec((nc, NSTATE, D), lambda i: (rev(i), 0, 0)),
                const((8, D)), const((8, D)), const((8, 512)), const((1, D)), const((1, 512)),
                const((1, LANES)), const((1, LANES)), const((1, D)), const((1, D)), const((LANES, D)),
                const((D, LANES))]
    return pl.pallas_call(
        body, grid=(nblk,),
        in_specs=in_specs,
        out_specs=[pl.BlockSpec((tt, NINP), lambda i: (rev(i), 0)),
                   const((8, D)), const((8, D)), const((8, 512)), const((8, D)), const((8, LANES))],
        out_shape=[SDS((t_len, NINP), BF16), SDS((8, D), F32), SDS((8, D), F32), SDS((8, 512), F32),
                   SDS((8, D), F32), SDS((8, LANES), F32)],
        scratch_shapes=[pltpu.VMEM((NSTATE, D), F32),
                        pltpu.VMEM((tt, D), F32), pltpu.VMEM((tt, 512), F32),
                        pltpu.VMEM((tt, D), F32), pltpu.VMEM((tt, D), F32),
                        pltpu.VMEM((tt, D), F32), pltpu.VMEM((tt, D), F32),
                        pltpu.VMEM((tt, 512), F32),
                        pltpu.VMEM((tt, D), F32), pltpu.VMEM((tt, D), F32),
                        pltpu.VMEM((8, D), F32), pltpu.VMEM((8, D), F32), pltpu.VMEM((8, 512), F32)],
        compiler_params=_cparams(("arbitrary",)), name="mixer_bwd")(
            *([proj] * 11), dy, states, prm["scw"], prm["cwx"], prm["cwbc"], prm["cbx"], prm["cbbc"], prm["dtb"],
            prm["alog"], prm["dskx"], prm["nrm"], prm["eh"], prm["eht"])


TN_IN = 1920


def layer_fwd(x, lw, prm, tt):
    proj, h1 = norm_matmul(x, lw["nw1"], lw["win"], pl.BlockSpec((D, TN_IN), lambda i, j: (0, j)),
                           NINP // TN_IN, TN_IN, "in_proj")
    y, st = mixer_fwd(proj, prm, tt)
    x1 = matmul_residual(y, lw["wout"], x, False, "out_proj")
    u, h2 = norm_matmul(x1, lw["nw2"], lw["wup"], pl.BlockSpec((None, D, 512), lambda i, j: (j, 0, 0)),
                        N_DEV, 512, "up_proj")
    x2 = matmul_residual(u, lw["wdn"], x1, True, "down_proj")
    return x2, (x, h1, proj, st, y, x1, h2, u)


def layer_bwd(dx2, dx2b, lw, prm, saved, tt):
    x, h1, proj, st, y, x1, h2, u = saved
    du = matmul_nt_act(dx2b, lw["wdn"], u, "mlp_bwd_du")
    g_wdn = matmul_tn(
        u, dx2b,
        lambda t_: pl.BlockSpec((t_, 512), lambda n, t: (t, n)),
        lambda t_: pl.BlockSpec((t_, D), lambda n, t: (t, 0)),
        pl.BlockSpec((512, D), lambda n, t: (n, 0)), SDS((DFF, D), BF16), N_DEV, True, "dw_down")
    dx1, dx1b, g_nw2 = matmul_nt_norm_bwd(
        du, lw["wup"], pl.BlockSpec((None, D, 512), lambda i, k: (k, 0, 0)), 512, N_DEV, x1, lw["nw2"], dx2,
        "mlp_bwd_dx")
    g_wup = matmul_tn(
        h2, du,
        lambda t_: pl.BlockSpec((t_, D), lambda n, t: (t, 0)),
        lambda t_: pl.BlockSpec((t_, 512), lambda n, t: (t, n)),
        pl.BlockSpec((None, D, 512), lambda n, t: (n, 0, 0)), SDS((N_DEV, D, 512), BF16), N_DEV, False, "dw_up")
    dy = matmul_nt_act(dx1b, lw["wout"], None, "out_bwd_dy")
    g_wout = matmul_tn(
        y, dx1b,
        lambda t_: pl.BlockSpec((t_, 512), lambda n, t: (t, n)),
        lambda t_: pl.BlockSpec((t_, D), lambda n, t: (t, 0)),
        pl.BlockSpec((512, D), lambda n, t: (n, 0)), SDS((MIX, D), BF16), MIX // 512, False, "dw_out")
    dproj, gscw, gcwx, gcwbc, gvec, gdt = mixer_bwd(proj, dy, st, prm, tt)
    dx0, dx0b, g_nw1 = matmul_nt_norm_bwd(
        dproj, lw["win"], pl.BlockSpec((D, TN_IN), lambda i, k: (0, k)), TN_IN, NINP // TN_IN, x, lw["nw1"], dx1,
        "in_bwd_dx")
    g_win = matmul_tn(
        h1, dproj,
        lambda t_: pl.BlockSpec((t_, D), lambda n, t: (t, 0)),
        lambda t_: pl.BlockSpec((t_, TN_IN), lambda n, t: (t, n)),
        pl.BlockSpec((D, TN_IN), lambda n, t: (0, n)), SDS((D, NINP), BF16), NINP // TN_IN, False, "dw_in")
    grads = {
        "win": g_win, "wout": g_wout, "wup": g_wup, "wdn": g_wdn,
        "scw": gscw[0:3], "cw": jnp.concatenate([gcwx[0:4], gcwbc[0:4]], axis=1),
        "cb": jnp.concatenate([gvec[2], gcwbc[4]], axis=0),
        "dtb": gdt[0, :NHEAD], "alog": gdt[1, :NHEAD], "dsk": gdt[2, :NHEAD],
        "nrm": gvec[0], "nw1": g_nw1[0], "nw2": g_nw2[0],
    }
    return dx0, dx0b, grads


def layer_params(i, norm_mix_w, win, wout, wup, wdn, scw, cw, ssd_conv_b, dt_bias, a_log, d_skip, ssd_norm_w,
                 norm_mlp_w, eh, eht):
    def rows8(a):
        return jnp.pad(a, ((0, 8 - a.shape[0]), (0, 0)))

    def lanes128(a):
        return jnp.pad(a, (0, LANES - a.shape[0]))[None, :]

    lw = {"win": win[i], "wout": wout[i], "wup": wup[i], "wdn": wdn[i],
          "nw1": norm_mix_w[i][None, :], "nw2": norm_mlp_w[i][None, :]}
    prm = {"scw": rows8(scw[i]), "cwx": rows8(cw[i][:, :D]), "cwbc": rows8(cw[i][:, D:]),
           "cbx": ssd_conv_b[i][None, :D], "cbbc": ssd_conv_b[i][None, D:],
           "dtb": lanes128(dt_bias[i]), "alog": lanes128(a_log[i]),
           "dskx": jnp.repeat(d_skip[i], HDIM)[None, :], "nrm": ssd_norm_w[i][None, :], "eh": eh, "eht": eht}
    return lw, prm


def _flip(v, bit):
    return 1 - v if bit else v


def all_gather(arrs, name):
    n = len(arrs)

    def body(*refs):
        ins, outs = refs[:n], refs[n:2 * n]
        send_sems, recv_sems, local_sems = refs[2 * n:]
        x, y, c = lax.axis_index("x"), lax.axis_index("y"), lax.axis_index("c")
        sibling = (x, y, 1 - c)
        chips = [(1 - x, y), (x, 1 - y), (1 - x, 1 - y)]

        def idx(px, py, pc):
            return 4 * px + 2 * py + pc

        def copy(a, k, block, to, src=None):
            dst = outs[a].at[idx(*block)]
            return pltpu.make_async_remote_copy(
                src_ref=dst if src is None else src, dst_ref=dst,
                send_sem=send_sems.at[a, k], recv_sem=recv_sems.at[a, k], device_id=to, device_id_type=MESH)

        me = (x, y, c)
        mine = [pltpu.make_async_copy(ins[a], outs[a].at[idx(*me)], local_sems.at[a]) for a in range(n)]
        for cp in mine:
            cp.start()
        first = []
        for a in range(n):
            first.append(copy(a, 0, me, sibling, src=ins[a]))
            first += [copy(a, 1 + j, me, (*chip, c), src=ins[a]) for j, chip in enumerate(chips)]
        for cp in first:
            cp.start()
        passed = []
        for j, chip in enumerate(chips):
            for a in range(n):
                copy(a, 1 + j, (*chip, c), me).wait_recv()
                cp = copy(a, 4 + j, (*chip, c), sibling)
                cp.start()
                passed.append(cp)
        for a in range(n):
            copy(a, 0, sibling, me).wait_recv()
            for j, chip in enumerate(chips):
                copy(a, 4 + j, (*chip, 1 - c), me).wait_recv()
        for cp in first + passed:
            cp.wait_send()
        for cp in mine:
            cp.wait()

    any_spec = pl.BlockSpec(memory_space=pl.ANY)
    return pl.pallas_call(
        body, in_specs=[any_spec] * n, out_specs=[any_spec] * n,
        out_shape=[SDS((N_DEV,) + a.shape, a.dtype) for a in arrs],
        scratch_shapes=[pltpu.SemaphoreType.DMA((n, 7)), pltpu.SemaphoreType.DMA((n, 7)),
                        pltpu.SemaphoreType.DMA((n,))],
        name=name)(*arrs)


def all_to_all(arrs, name):
    n = len(arrs)

    def body(*refs):
        ins, outs = refs[:n], refs[n:2 * n]
        send_sems, recv_sems, local_sems = refs[2 * n:]
        x, y, c = lax.axis_index("x"), lax.axis_index("y"), lax.axis_index("c")
        me = 4 * x + 2 * y + c
        mine = [pltpu.make_async_copy(ins[a].at[me], outs[a].at[me], local_sems.at[a]) for a in range(n)]
        for cp in mine:
            cp.start()
        sends = []
        for mask in range(1, N_DEV):
            px, py, pc = _flip(x, mask & 4), _flip(y, mask & 2), _flip(c, mask & 1)
            peer = 4 * px + 2 * py + pc
            for a in range(n):
                cp = pltpu.make_async_remote_copy(
                    src_ref=ins[a].at[peer], dst_ref=outs[a].at[me],
                    send_sem=send_sems.at[a, mask - 1], recv_sem=recv_sems.at[a, mask - 1],
                    device_id=(px, py, pc), device_id_type=MESH)
                cp.start()
                sends.append((cp, a, mask, peer))
        for cp, a, mask, peer in sends:
            pltpu.make_async_remote_copy(
                src_ref=ins[a].at[peer], dst_ref=outs[a].at[peer],
                send_sem=send_sems.at[a, mask - 1], recv_sem=recv_sems.at[a, mask - 1],
                device_id=(x, y, c), device_id_type=MESH).wait_recv()
        for cp, _, _, _ in sends:
            cp.wait_send()
        for cp in mine:
            cp.wait()

    any_spec = pl.BlockSpec(memory_space=pl.ANY)
    return pl.pallas_call(
        body, in_specs=[any_spec] * n, out_specs=[any_spec] * n,
        out_shape=[SDS(a.shape, a.dtype) for a in arrs],
        scratch_shapes=[pltpu.SemaphoreType.DMA((n, 7)), pltpu.SemaphoreType.DMA((n, 7)),
                        pltpu.SemaphoreType.DMA((n,))],
        name=name)(*arrs)


def adamw(w, slots, m, v, name):
    r_len, c_len = w.shape
    br = r_len if r_len <= 512 else 512
    assert r_len % br == 0

    def body(w_ref, s_ref, m_ref, v_ref, g_ref, d_ref, nm_ref, nv_ref):
        g = s_ref[0].astype(F32)
        for k in range(1, N_DEV):
            g = g + s_ref[k].astype(F32)
        mn = ADAM_B1 * m_ref[...] + (1.0 - ADAM_B1) * g
        vn = ADAM_B2 * v_ref[...] + (1.0 - ADAM_B2) * jnp.square(g)
        m_hat = mn / (1.0 - ADAM_B1 ** ADAM_STEP)
        v_hat = vn / (1.0 - ADAM_B2 ** ADAM_STEP)
        g_ref[...] = g
        d_ref[...] = -ADAM_LR * (m_hat / (jnp.sqrt(v_hat) + ADAM_EPS) + ADAM_WD * w_ref[...])
        nm_ref[...] = mn
        nv_ref[...] = vn

    spec = pl.BlockSpec((br, c_len), lambda i: (i, 0))
    return pl.pallas_call(
        body, grid=(r_len // br,),
        in_specs=[spec, pl.BlockSpec((N_DEV, br, c_len), lambda i: (0, i, 0)), spec, spec],
        out_specs=[spec] * 4, out_shape=[SDS((r_len, c_len), F32)] * 4,
        compiler_params=_cparams(("parallel",)), name=name)(w, slots, m, v)


def _adamw_nd(w, slots, m, v, name):
    shp = w.shape
    r = int(np.prod(shp[:-1]))
    outs = adamw(w.reshape(r, shp[-1]), slots.reshape(N_DEV, r, shp[-1]), m.reshape(r, shp[-1]),
                 v.reshape(r, shp[-1]), name)
    return [o.reshape(shp) for o in outs]


SMALL = [("norm_mix_w", DEPTH * D), ("ssd_conv_b", DEPTH * XBC), ("dt_bias", DEPTH * NHEAD),
         ("a_log", DEPTH * NHEAD), ("d_skip", DEPTH * NHEAD), ("ssd_norm_w", DEPTH * D),
         ("norm_mlp_w", DEPTH * D), ("final_norm_w", D)]
SMALL_LEN = sum(s for _, s in SMALL)
SMALL_ROWS = -(-SMALL_LEN // LANES)


def _pack_small(parts):
    flat = jnp.concatenate([parts[k].reshape(-1) for k, _ in SMALL])
    return jnp.pad(flat, (0, SMALL_ROWS * LANES - SMALL_LEN)).reshape(SMALL_ROWS, LANES)


def _unpack_small(packed, shapes):
    flat = packed.reshape(-1)
    out, off = {}, 0
    for k, s in SMALL:
        out[k] = flat[off:off + s].reshape(shapes[k])
        off += s
    return out


def kernel(x, norm_mix_w, w_in, short_conv_w, ssd_conv_w, ssd_conv_b, dt_bias, a_log, d_skip, ssd_norm_w, w_out, norm_mlp_w, w_up, w_down, final_norm_w, loss_target, m_norm_mix_w, m_w_in, m_short_conv_w, m_ssd_conv_w, m_ssd_conv_b, m_dt_bias, m_a_log, m_d_skip, m_ssd_norm_w, m_w_out, m_norm_mlp_w, m_w_up, m_w_down, m_final_norm_w, v_norm_mix_w, v_w_in, v_short_conv_w, v_ssd_conv_w, v_ssd_conv_b, v_dt_bias, v_a_log, v_d_skip, v_ssd_norm_w, v_w_out, v_norm_mlp_w, v_w_up, v_w_down, v_final_norm_w):
    xs = x[0]
    t_len = xs.shape[0]
    tt = min(256, t_len)
    eh, eht = _head_matrices()

    g_in, g_out, g_up, g_dn, g_sc, g_cw = all_gather(
        [w_in.astype(BF16), w_out.astype(BF16), w_up.astype(BF16), w_down.astype(BF16), short_conv_w, ssd_conv_w],
        "gather_weights")
    win = jnp.pad(g_in.transpose(1, 2, 0, 3).reshape(DEPTH, D, NIN), ((0, 0), (0, 0), (0, NINP - NIN)))
    wout = g_out.transpose(1, 0, 2, 3).reshape(DEPTH, MIX, D)
    wup = g_up.transpose(1, 0, 2, 3)
    wdn = g_dn.transpose(1, 0, 2, 3).reshape(DEPTH, DFF, D)
    scw = g_sc.transpose(1, 2, 0, 3).reshape(DEPTH, 3, D)
    cw = g_cw.transpose(1, 2, 0, 3).reshape(DEPTH, 4, XBC)

    layers = [layer_params(i, norm_mix_w, win, wout, wup, wdn, scw, cw, ssd_conv_b, dt_bias, a_log, d_skip,
                           ssd_norm_w, norm_mlp_w, eh, eht) for i in range(DEPTH)]
    act = xs
    saved = []
    for lw, prm in layers:
        act, sv = layer_fwd(act, lw, prm, tt)
        saved.append(sv)
    loss_acc, dx, dxb, g_fw = loss_head(act, final_norm_w[None, :], loss_target[0])

    grads = [None] * DEPTH
    for i in reversed(range(DEPTH)):
        lw, prm = layers[i]
        dx, dxb, grads[i] = layer_bwd(dx, dxb, lw, prm, saved[i], tt)

    def stack(k):
        return jnp.stack([g[k] for g in grads])

    p_in = stack("win")[:, :, :NIN].reshape(DEPTH, D, N_DEV, NIN // N_DEV).transpose(2, 0, 1, 3)
    p_out = stack("wout").reshape(DEPTH, N_DEV, MIX // N_DEV, D).transpose(1, 0, 2, 3)
    p_up = stack("wup").transpose(1, 0, 2, 3)
    p_dn = stack("wdn").reshape(DEPTH, N_DEV, DFF // N_DEV, D).transpose(1, 0, 2, 3)
    p_sc = stack("scw").reshape(DEPTH, 3, N_DEV, D // N_DEV).transpose(2, 0, 1, 3)
    p_cw = stack("cw").reshape(DEPTH, 4, N_DEV, XBC // N_DEV).transpose(2, 0, 1, 3)
    small = _pack_small({"norm_mix_w": stack("nw1"), "ssd_conv_b": stack("cb"), "dt_bias": stack("dtb"),
                         "a_log": stack("alog"), "d_skip": stack("dsk"), "ssd_norm_w": stack("nrm"),
                         "norm_mlp_w": stack("nw2"), "final_norm_w": g_fw[0]})
    r_in, r_out, r_up, r_dn, r_sc, r_cw = all_to_all([p_in, p_out, p_up, p_dn, p_sc, p_cw], "exchange_grads")
    (r_small,) = all_gather([small], "gather_small_grads")

    res = {}
    res["w_in"] = _adamw_nd(w_in, r_in, m_w_in, v_w_in, "adamw_w_in")
    res["w_out"] = _adamw_nd(w_out, r_out, m_w_out, v_w_out, "adamw_w_out")
    res["w_up"] = _adamw_nd(w_up, r_up, m_w_up, v_w_up, "adamw_w_up")
    res["w_down"] = _adamw_nd(w_down, r_dn, m_w_down, v_w_down, "adamw_w_down")
    res["short_conv_w"] = _adamw_nd(short_conv_w, r_sc, m_short_conv_w, v_short_conv_w, "adamw_short_conv")
    res["ssd_conv_w"] = _adamw_nd(ssd_conv_w, r_cw, m_ssd_conv_w, v_ssd_conv_w, "adamw_ssd_conv")
    small_w = {"norm_mix_w": norm_mix_w, "ssd_conv_b": ssd_conv_b, "dt_bias": dt_bias, "a_log": a_log,
               "d_skip": d_skip, "ssd_norm_w": ssd_norm_w, "norm_mlp_w": norm_mlp_w, "final_norm_w": final_norm_w}
    small_m = {"norm_mix_w": m_norm_mix_w, "ssd_conv_b": m_ssd_conv_b, "dt_bias": m_dt_bias, "a_log": m_a_log,
               "d_skip": m_d_skip, "ssd_norm_w": m_ssd_norm_w, "norm_mlp_w": m_norm_mlp_w,
               "final_norm_w": m_final_norm_w}
    small_v = {"norm_mix_w": v_norm_mix_w, "ssd_conv_b": v_ssd_conv_b, "dt_bias": v_dt_bias, "a_log": v_a_log,
               "d_skip": v_d_skip, "ssd_norm_w": v_ssd_norm_w, "norm_mlp_w": v_norm_mlp_w,
               "final_norm_w": v_final_norm_w}
    shapes = {k: a.shape for k, a in small_w.items()}
    packed = adamw(_pack_small(small_w), r_small, _pack_small(small_m), _pack_small(small_v), "adamw_small")
    unpacked = [_unpack_small(p, shapes) for p in packed]
    for k in small_w:
        res[k] = [u[k] for u in unpacked]

    loss = lax.psum(loss_acc[0, 0], ("x", "y", "c"))
    order = ["norm_mix_w", "w_in", "short_conv_w", "ssd_conv_w", "ssd_conv_b", "dt_bias", "a_log", "d_skip",
             "ssd_norm_w", "w_out", "norm_mlp_w", "w_up", "w_down", "final_norm_w"]
    out = [loss, dx[None]]
    for part in range(4):
        out += [res[k][part] for k in order]
    return tuple(out)
```

```python
import functools

import numpy as np
import jax
import jax.numpy as jnp
from jax import lax
from jax.experimental import pallas as pl
from jax.experimental.pallas import tpu as pltpu

F32 = jnp.float32
BF16 = jnp.bfloat16
SDS = jax.ShapeDtypeStruct

N_DEV = 8
DEPTH = 4
D = 1024
NIN = 5648
NINP = 5760
DFF = 4096
MIX = 2048
NHEAD = 16
HDIM = 64
NSTATE = 128
CHUNK = 64
XBC = 1536
EPS = 1e-5
LANES = 128

C_UB, C_UC, C_UH, C_Z, C_XS, C_BC, C_DT = 0, 1024, 2048, 3072, 4096, 5120, 5632

ADAM_LR = 0.001
ADAM_B1 = 0.9
ADAM_B2 = 0.999
ADAM_EPS = 1e-08
ADAM_WD = 0.01
ADAM_STEP = 10

VMEM_LIMIT = 56 * 1024 * 1024
MESH = pl.DeviceIdType.MESH


def _cparams(sem):
    return pltpu.CompilerParams(dimension_semantics=sem, vmem_limit_bytes=VMEM_LIMIT)


def _nt(a, b):
    return lax.dot_general(a, b, (((1,), (1,)), ((), ())), preferred_element_type=F32)


def _tn(a, b):
    return lax.dot_general(a, b, (((0,), (0,)), ((), ())), preferred_element_type=F32)


def _nn(a, b):
    return jnp.dot(a, b, preferred_element_type=F32)


def _sigmoid(v):
    return 1.0 / (1.0 + jnp.exp(-v))


def _split3(v):
    v1 = v.astype(BF16)
    r1 = v - v1.astype(F32)
    v2 = r1.astype(BF16)
    v3 = (r1 - v2.astype(F32)).astype(BF16)
    return v1, v2, v3


def _expand(v, eh):
    v1, v2, v3 = _split3(v)
    return _nn(v1, eh) + _nn(v2, eh) + _nn(v3, eh)


def _head_reduce(v, eht):
    v1, v2, v3 = _split3(v)
    return _nn(v1, eht) + _nn(v2, eht) + _nn(v3, eht)


def _head_matrices():
    eh = np.zeros((LANES, D), np.float32)
    for h in range(NHEAD):
        eh[h, h * HDIM:(h + 1) * HDIM] = 1.0
    return jnp.asarray(eh, BF16), jnp.asarray(eh.T.copy(), BF16)


def _resident(shape):
    return pl.BlockSpec(shape, lambda *_: (0,) * len(shape), pipeline_mode=pl.Buffered(1))


def _col_chunks(n, step):
    return [(c, min(c + step, n)) for c in range(0, n, step)]


def norm_matmul(x, nw, w, name):
    t_len = x.shape[0]
    n_len = w.shape[1]
    tm = min(512, t_len)
    chunks = _col_chunks(n_len, 1536)

    def body(x_ref, nw_ref, w_ref, o_ref, h_ref):
        xv = x_ref[...]
        r = lax.rsqrt(jnp.mean(xv * xv, axis=-1, keepdims=True) + EPS)
        hv = (xv * r * nw_ref[...]).astype(BF16)
        h_ref[...] = hv
        for c0, c1 in chunks:
            o_ref[:, c0:c1] = _nn(hv, w_ref[:, c0:c1]).astype(o_ref.dtype)

    return pl.pallas_call(
        body, grid=(t_len // tm,),
        in_specs=[pl.BlockSpec((tm, D), lambda i: (i, 0)), _resident((1, D)), _resident((D, n_len))],
        out_specs=[pl.BlockSpec((tm, n_len), lambda i: (i, 0)),
                   pl.BlockSpec((tm, D), lambda i: (i, 0))],
        out_shape=[SDS((t_len, n_len), BF16), SDS((t_len, D), BF16)],
        compiler_params=_cparams(("parallel",)), name=name)(x, nw, w)


def matmul_residual(a, w, res, relu2, name):
    t_len, k_len = a.shape
    tm = min(512, t_len)

    def body(a_ref, w_ref, res_ref, o_ref):
        av = a_ref[...]
        if relu2:
            af = jnp.maximum(av.astype(F32), 0.0)
            av = (af * af).astype(BF16)
        o_ref[...] = res_ref[...] + _nn(av, w_ref[...])

    return pl.pallas_call(
        body, grid=(t_len // tm,),
        in_specs=[pl.BlockSpec((tm, k_len), lambda i: (i, 0)),
                  _resident((k_len, D)),
                  pl.BlockSpec((tm, D), lambda i: (i, 0))],
        out_specs=pl.BlockSpec((tm, D), lambda i: (i, 0)),
        out_shape=SDS((t_len, D), F32),
        compiler_params=_cparams(("parallel",)), name=name)(a, w, res)


def matmul_nt_act(dy, w, u, name):
    t_len = dy.shape[0]
    n_len = w.shape[0]
    tm = min(512, t_len)
    chunks = _col_chunks(n_len, 1024)

    def body(dy_ref, w_ref, *rest):
        if u is None:
            (o_ref,) = rest
        else:
            u_ref, o_ref = rest
        dyv = dy_ref[...]
        for c0, c1 in chunks:
            p = _nt(dyv, w_ref[c0:c1, :])
            if u is not None:
                p = p * (2.0 * jnp.maximum(u_ref[:, c0:c1].astype(F32), 0.0))
            o_ref[:, c0:c1] = p.astype(o_ref.dtype)

    in_specs = [pl.BlockSpec((tm, D), lambda i: (i, 0)), _resident((n_len, D))]
    args = [dy, w]
    if u is not None:
        in_specs.append(pl.BlockSpec((tm, n_len), lambda i: (i, 0)))
        args.append(u)
    return pl.pallas_call(
        body, grid=(t_len // tm,),
        in_specs=in_specs,
        out_specs=pl.BlockSpec((tm, n_len), lambda i: (i, 0)),
        out_shape=SDS((t_len, n_len), BF16),
        compiler_params=_cparams(("parallel",)), name=name)(*args)


def matmul_tn(a, b, a_spec, b_spec, o_spec, o_shape, n_out, relu2, name):
    t_len = a.shape[0]
    tt = min(2048, t_len)
    nt = t_len // tt

    def body(a_ref, b_ref, o_ref, acc):
        t = pl.program_id(1)
        av = a_ref[...]
        if relu2:
            af = jnp.maximum(av.astype(F32), 0.0)
            av = (af * af).astype(BF16)
        p = _tn(av, b_ref[...])

        @pl.when(t == 0)
        def _():
            acc[...] = p

        @pl.when(t > 0)
        def _():
            acc[...] += p

        @pl.when(t == nt - 1)
        def _():
            o_ref[...] = acc[...].astype(o_ref.dtype)

    blk = tuple(s for s in o_spec.block_shape if s is not None)
    return pl.pallas_call(
        body, grid=(n_out, nt),
        in_specs=[a_spec(tt), b_spec(tt)],
        out_specs=o_spec, out_shape=o_shape,
        scratch_shapes=[pltpu.VMEM(blk, F32)],
        compiler_params=_cparams(("parallel", "arbitrary")), name=name)(a, b)


def matmul_nt_norm_bwd(dy, w, x, nw, dres, name):
    t_len = x.shape[0]
    k_len = w.shape[1]
    tm = min(512, t_len)

    def body(dy_ref, w_ref, x_ref, nw_ref, dres_ref, dx_ref, dxb_ref, dnw_ref):
        @pl.when(pl.program_id(0) == 0)
        def _():
            dnw_ref[...] = jnp.zeros_like(dnw_ref)

        dh = _nt(dy_ref[...], w_ref[...])
        xv = x_ref[...]
        r = lax.rsqrt(jnp.mean(xv * xv, axis=-1, keepdims=True) + EPS)
        xh = xv * r
        dnw_ref[0:1, :] += jnp.sum(dh * xh, axis=0, keepdims=True)
        g = dh * nw_ref[...]
        dx = dres_ref[...] + r * (g - xh * jnp.mean(g * xh, axis=-1, keepdims=True))
        dx_ref[...] = dx
        dxb_ref[...] = dx.astype(BF16)

    return pl.pallas_call(
        body, grid=(t_len // tm,),
        in_specs=[pl.BlockSpec((tm, k_len), lambda i: (i, 0)),
                  _resident((D, k_len)),
                  pl.BlockSpec((tm, D), lambda i: (i, 0)),
                  _resident((1, D)),
                  pl.BlockSpec((tm, D), lambda i: (i, 0))],
        out_specs=[pl.BlockSpec((tm, D), lambda i: (i, 0)),
                   pl.BlockSpec((tm, D), lambda i: (i, 0)),
                   pl.BlockSpec((8, D), lambda i: (0, 0))],
        out_shape=[SDS((t_len, D), F32), SDS((t_len, D), BF16), SDS((8, D), F32)],
        compiler_params=_cparams(("arbitrary",)), name=name)(dy, w, x, nw, dres)


def loss_head(x, fw, tgt):
    t_len = x.shape[0]
    tm = min(512, t_len)

    def body(x_ref, fw_ref, t_ref, loss_ref, dx_ref, dxb_ref, dfw_ref):
        @pl.when(pl.program_id(0) == 0)
        def _():
            loss_ref[...] = jnp.zeros_like(loss_ref)
            dfw_ref[...] = jnp.zeros_like(dfw_ref)
        xv = x_ref[...]
        r = lax.rsqrt(jnp.mean(xv * xv, axis=-1, keepdims=True) + EPS)
        xh = xv * r
        w = fw_ref[...]
        e = xh * w - t_ref[...]
        row = jnp.sum(e * e, axis=-1, keepdims=True) * (1.0 / D)
        loss_ref[...] += 0.5 * jnp.sum(row, axis=0, keepdims=True)
        dyf = e * (1.0 / D)
        dfw_ref[0:1, :] += jnp.sum(dyf * xh, axis=0, keepdims=True)
        g = dyf * w
        dx = r * (g - xh * jnp.mean(g * xh, axis=-1, keepdims=True))
        dx_ref[...] = dx
        dxb_ref[...] = dx.astype(BF16)

    return pl.pallas_call(
        body, grid=(t_len // tm,),
        in_specs=[pl.BlockSpec((tm, D), lambda i: (i, 0)),
                  pl.BlockSpec((1, D), lambda i: (0, 0)),
                  pl.BlockSpec((tm, D), lambda i: (i, 0))],
        out_specs=[pl.BlockSpec((8, LANES), lambda i: (0, 0)),
                   pl.BlockSpec((tm, D), lambda i: (i, 0)),
                   pl.BlockSpec((tm, D), lambda i: (i, 0)),
                   pl.BlockSpec((8, D), lambda i: (0, 0))],
        out_shape=[SDS((8, LANES), F32), SDS((t_len, D), F32), SDS((t_len, D), BF16), SDS((8, D), F32)],
        compiler_params=_cparams(("arbitrary",)), name="loss_head")(x, fw, tgt)


def _shift_dn(x, halo, j):
    if j == 0:
        return x
    xr = pltpu.roll(x, j, 0)
    hr = pltpu.roll(halo, j, 0)
    row = lax.broadcasted_iota(jnp.int32, hr.shape, 0)
    top = jnp.where(row < j, hr, xr[0:8])
    return jnp.concatenate([top, xr[8:]], axis=0)


def _shift_up(x, nxt, j):
    if j == 0:
        return x
    n = x.shape[0]
    xr = pltpu.roll(x, n - j, 0)
    hr = pltpu.roll(nxt, 8 - j, 0)
    row = lax.broadcasted_iota(jnp.int32, hr.shape, 0)
    bot = jnp.where(row >= 8 - j, hr, xr[n - 8:n])
    return jnp.concatenate([xr[:n - 8], bot], axis=0)


def _conv_fwd(x, halo, w_ref, kw):
    acc = None
    for k in range(kw):
        term = w_ref[k:k + 1, :] * _shift_dn(x, halo, kw - 1 - k)
        acc = term if acc is None else acc + term
    return acc


def _chunk_cumsum(a, pos):
    for sh in (1, 2, 4, 8, 16, 32):
        a = a + jnp.where(pos >= sh, pltpu.roll(a, sh, 0), 0.0)
    return a


def _chunk_rcumsum(a, pos):
    n = a.shape[0]
    for sh in (1, 2, 4, 8, 16, 32):
        a = a + jnp.where(pos < CHUNK - sh, pltpu.roll(a, n - sh, 0), 0.0)
    return a


def _softplus(v):
    return jnp.maximum(v, 0.0) + jnp.log(1.0 + jnp.exp(-jnp.abs(v)))


def _silu(v):
    return v * _sigmoid(v)


def _dsilu(v):
    s = _sigmoid(v)
    return s * (1.0 + v * (1.0 - s))


def _lane_masks(width=D):
    lane = lax.broadcasted_iota(jnp.int32, (CHUNK, width), 1) & (HDIM - 1)
    row = lax.broadcasted_iota(jnp.int32, (CHUNK, width), 0)
    return lane == row, lane <= row


def _rep_matrix():
    lane = lax.broadcasted_iota(jnp.int32, (CHUNK, 512), 1) & (HDIM - 1)
    row = lax.broadcasted_iota(jnp.int32, (CHUNK, 512), 0)
    return jnp.where(lane == row, 1.0, 0.0).astype(BF16)


def _blockdiag(xp):
    lane = lax.broadcasted_iota(jnp.int32, xp.shape, 1)
    zero = jnp.zeros_like(xp)
    return jnp.concatenate([jnp.where(lane < HDIM, xp, zero), jnp.where(lane >= HDIM, xp, zero)], axis=0)


def _mixer_views(tt):
    r8 = tt // 8

    def main(width, col):
        return pl.BlockSpec((tt, width), lambda i, c=col // width: (i, c))

    def halo(width, col):
        return pl.BlockSpec((8, width), lambda i, c=col // width: (jnp.maximum(i * r8 - 1, 0), c))

    return main, halo


def mixer_fwd(proj, prm, tt):
    t_len = proj.shape[0]
    nblk = t_len // tt
    nc = tt // CHUNK
    main, halo = _mixer_views(tt)

    def body(ub_ref, uc_ref, uh_ref, z_ref, xr_ref, bcr_ref, dtr_ref, uch_ref, uhh_ref, xrh_ref, bcrh_ref,
             scw_ref, cwx_ref, cwbc_ref, cbx_ref, cbbc_ref, dtb_ref, alog_ref, dsk_ref, nrm_ref, eh_ref,
             y_ref, st_ref, hs, xs_s, bc_s, dtx_s, cumx_s, yssd_s):
        i = pl.program_id(0)
        first = i == 0

        @pl.when(first)
        def _():
            hs[...] = jnp.zeros_like(hs)

        keep = jnp.where(first, 0.0, 1.0)
        v = uc_ref[...].astype(F32) * uh_ref[...].astype(F32)
        vh = uch_ref[...].astype(F32) * uhh_ref[...].astype(F32) * keep
        y_ref[:, 0:D] = (ub_ref[...].astype(F32) * _conv_fwd(v, vh, scw_ref, 3)).astype(BF16)

        xs = _silu(_conv_fwd(xr_ref[...].astype(F32), xrh_ref[...].astype(F32) * keep, cwx_ref, 4) + cbx_ref[...])
        xs_s[...] = xs
        bc_s[...] = _silu(_conv_fwd(bcr_ref[...].astype(F32), bcrh_ref[...].astype(F32) * keep, cwbc_ref, 4)
                          + cbbc_ref[...])
        dt = _softplus(dtr_ref[...].astype(F32) + dtb_ref[...])
        a_neg = -jnp.exp(alog_ref[...])
        pos = lax.broadcasted_iota(jnp.int32, (tt, LANES), 0) & (CHUNK - 1)
        cum = _chunk_cumsum(dt * a_neg, pos)
        eh = eh_ref[...]
        dtx_s[...] = _expand(dt, eh)
        cumx_s[...] = _expand(cum, eh)
        irep, causal = _lane_masks()
        rep = _rep_matrix()

        def chunk(c, carry):
            r0 = pl.multiple_of(c * CHUNK, CHUNK)
            rows = pl.ds(r0, CHUNK)
            cumx = cumx_s[rows, :]
            cum_l = cumx[CHUNK - 1:CHUNK, :]
            xd = xs_s[rows, :] * dtx_s[rows, :]
            xf = xd * jnp.exp(cum_l - cumx)
            ex = jnp.exp(cumx)
            e_l = jnp.exp(cum_l)
            rvec = jnp.sum(jnp.where(irep, cumx, 0.0), axis=0, keepdims=True)
            lam = jnp.where(causal, jnp.exp(jnp.where(causal, cumx - rvec, 0.0)), 0.0)
            bc = bc_s[rows, :]
            for g in range(2):
                gs = slice(g * 512, (g + 1) * 512)
                bg = bc[:, g * NSTATE:(g + 1) * NSTATE].astype(BF16)
                cg = bc[:, 256 + g * NSTATE:256 + (g + 1) * NSTATE].astype(BF16)
                s_rep = _nn(_nt(cg, bg).astype(BF16), rep)
                m_g = (s_rep * lam[:, gs]).astype(BF16)
                h_g = hs[:, gs]
                h_b = h_g.astype(BF16)
                st_ref[c, :, gs] = h_b
                yo = _nn(cg, h_b) * ex[:, gs]
                xd_b = xd[:, gs].astype(BF16)
                for hp in range(4):
                    ps = slice(hp * LANES, (hp + 1) * LANES)
                    yd = _nn(m_g[:, ps], _blockdiag(xd_b[:, ps]))
                    yssd_s[rows, g * 512 + hp * LANES:g * 512 + (hp + 1) * LANES] = yd + yo[:, ps]
                hs[:, gs] = h_g * e_l[:, gs] + _tn(bg, xf[:, gs].astype(BF16))
            return carry

        lax.fori_loop(0, nc, chunk, 0)

        ys = yssd_s[...] + dsk_ref[...] * xs_s[...]
        gt = ys * _silu(z_ref[...].astype(F32))
        for g in range(2):
            gs = slice(g * 512, (g + 1) * 512)
            gg = gt[:, gs]
            rn = lax.rsqrt(jnp.mean(gg * gg, axis=-1, keepdims=True) + EPS)
            y_ref[:, D + g * 512:D + (g + 1) * 512] = (gg * rn * nrm_ref[:, gs]).astype(BF16)

    def const(shape):
        return pl.BlockSpec(shape, lambda i: (0, 0))

    in_specs = [main(D, C_UB), main(D, C_UC), main(D, C_UH), main(D, C_Z), main(D, C_XS), main(512, C_BC),
                main(LANES, C_DT), halo(D, C_UC), halo(D, C_UH), halo(D, C_XS), halo(512, C_BC),
                const((8, D)), const((8, D)), const((8, 512)), const((1, D)), const((1, 512)),
                const((1, LANES)), const((1, LANES)), const((1, D)), const((1, D)), const((LANES, D))]
    return pl.pallas_call(
        body, grid=(nblk,),
        in_specs=in_specs,
        out_specs=[pl.BlockSpec((tt, MIX), lambda i: (i, 0)),
                   pl.BlockSpec((nc, NSTATE, D), lambda i: (i, 0, 0))],
        out_shape=[SDS((t_len, MIX), BF16), SDS((t_len // CHUNK, NSTATE, D), BF16)],
        scratch_shapes=[pltpu.VMEM((NSTATE, D), F32), pltpu.VMEM((tt, D), F32), pltpu.VMEM((tt, 512), F32),
                        pltpu.VMEM((tt, D), F32), pltpu.VMEM((tt, D), F32), pltpu.VMEM((tt, D), F32)],
        compiler_params=_cparams(("arbitrary",)), name="mixer_fwd")(
            *([proj] * 11), prm["scw"], prm["cwx"], prm["cwbc"], prm["cbx"], prm["cbbc"], prm["dtb"],
            prm["alog"], prm["dskx"], prm["nrm"], prm["eh"])


def mixer_bwd(proj, dy, states, prm, tt):
    t_len = proj.shape[0]
    nblk = t_len // tt
    nc = tt // CHUNK
    r8 = tt // 8

    def rev(i):
        return nblk - 1 - i

    def main(width, col):
        return pl.BlockSpec((tt, width), lambda i, c=col // width: (rev(i), c))

    def halo(width, col):
        return pl.BlockSpec((8, width), lambda i, c=col // width: (jnp.maximum(rev(i) * r8 - 1, 0), c))

    def body(ub_ref, uc_ref, uh_ref, z_ref, xr_ref, bcr_ref, dtr_ref, uch_ref, uhh_ref, xrh_ref, bcrh_ref,
             dy_ref, st_ref,
             scw_ref, cwx_ref, cwbc_ref, cbx_ref, cbbc_ref, dtb_ref, alog_ref, dsk_ref, nrm_ref, eh_ref, eht_ref,
             dp_ref, gscw_ref, gcwx_ref, gcwbc_ref, gvec_ref, gdt_ref,
             dhs, xs_s, bc_s, dtx_s, cumx_s, dys_s, dxs_s, dbc_s, red_s, ddtx_s, nx_cv, nx_px, nx_pbc):
        i = pl.program_id(0)
        blk = rev(i)

        @pl.when(i == 0)
        def _():
            dhs[...] = jnp.zeros_like(dhs)
            nx_cv[...] = jnp.zeros_like(nx_cv)
            nx_px[...] = jnp.zeros_like(nx_px)
            nx_pbc[...] = jnp.zeros_like(nx_pbc)
            gscw_ref[...] = jnp.zeros_like(gscw_ref)
            gcwx_ref[...] = jnp.zeros_like(gcwx_ref)
            gcwbc_ref[...] = jnp.zeros_like(gcwbc_ref)
            gvec_ref[...] = jnp.zeros_like(gvec_ref)
            gdt_ref[...] = jnp.zeros_like(gdt_ref)

        keep = jnp.where(blk == 0, 0.0, 1.0)

        ub = ub_ref[...].astype(F32)
        uc = uc_ref[...].astype(F32)
        uh = uh_ref[...].astype(F32)
        v = uc * uh
        vh = uch_ref[...].astype(F32) * uhh_ref[...].astype(F32) * keep
        dya = dy_ref[:, 0:D].astype(F32)
        dp_ref[:, C_UB:C_UB + D] = (dya * _conv_fwd(v, vh, scw_ref, 3)).astype(BF16)
        dcv = dya * ub
        nxt = nx_cv[...]
        dv = None
        for k in range(3):
            gscw_ref[k:k + 1, :] += jnp.sum(dcv * _shift_dn(v, vh, 2 - k), axis=0, keepdims=True)
            term = scw_ref[k:k + 1, :] * _shift_up(dcv, nxt, 2 - k)
            dv = term if dv is None else dv + term
        nx_cv[...] = dcv[0:8]
        dp_ref[:, C_UC:C_UC + D] = (dv * uh).astype(BF16)
        dp_ref[:, C_UH:C_UH + D] = (dv * uc).astype(BF16)

        xraw = xr_ref[...].astype(F32)
        xrh = xrh_ref[...].astype(F32) * keep
        bcraw = bcr_ref[...].astype(F32)
        bcrh = bcrh_ref[...].astype(F32) * keep
        pre_x = _conv_fwd(xraw, xrh, cwx_ref, 4) + cbx_ref[...]
        pre_bc = _conv_fwd(bcraw, bcrh, cwbc_ref, 4) + cbbc_ref[...]
        xs = _silu(pre_x)
        xs_s[...] = xs
        bc_s[...] = _silu(pre_bc)
        dt_pre = dtr_ref[...].astype(F32) + dtb_ref[...]
        dt = _softplus(dt_pre)
        a_neg = -jnp.exp(alog_ref[...])
        pos = lax.broadcasted_iota(jnp.int32, (tt, LANES), 0) & (CHUNK - 1)
        cum = _chunk_cumsum(dt * a_neg, pos)
        eh = eh_ref[...]
        eht = eht_ref[...]
        dtx_s[...] = _expand(dt, eh)
        cumx_s[...] = _expand(cum, eh)

        irep, causal = _lane_masks()
        irep_g, _ = _lane_masks(512)
        rep = _rep_matrix()
        row64 = lax.broadcasted_iota(jnp.int32, (CHUNK, 512), 0)
        lane128 = lax.broadcasted_iota(jnp.int32, (CHUNK, LANES), 1)

        def fwd_chunk(c, carry):
            r0 = pl.multiple_of(c * CHUNK, CHUNK)
            rows = pl.ds(r0, CHUNK)
            cumx = cumx_s[rows, :]
            xd = xs_s[rows, :] * dtx_s[rows, :]
            ex = jnp.exp(cumx)
            rvec = jnp.sum(jnp.where(irep, cumx, 0.0), axis=0, keepdims=True)
            lam = jnp.where(causal, jnp.exp(jnp.where(causal, cumx - rvec, 0.0)), 0.0)
            bc = bc_s[rows, :]
            for g in range(2):
                gs = slice(g * 512, (g + 1) * 512)
                bg = bc[:, g * NSTATE:(g + 1) * NSTATE].astype(BF16)
                cg = bc[:, 256 + g * NSTATE:256 + (g + 1) * NSTATE].astype(BF16)
                s_rep = _nn(_nt(cg, bg).astype(BF16), rep)
                m_g = (s_rep * lam[:, gs]).astype(BF16)
                yo = _nn(cg, st_ref[c, :, gs]) * ex[:, gs]
                xd_b = xd[:, gs].astype(BF16)
                for hp in range(4):
                    ps = slice(hp * LANES, (hp + 1) * LANES)
                    yd = _nn(m_g[:, ps], _blockdiag(xd_b[:, ps]))
                    dys_s[rows, g * 512 + hp * LANES:g * 512 + (hp + 1) * LANES] = yd + yo[:, ps]
            return carry

        lax.fori_loop(0, nc, fwd_chunk, 0)

        z = z_ref[...].astype(F32)
        sz = _silu(z)
        ys = dys_s[...] + dsk_ref[...] * xs
        gt = ys * sz
        dyb = dy_ref[:, D:MIX].astype(F32)
        for g in range(2):
            gs = slice(g * 512, (g + 1) * 512)
            gg = gt[:, gs]
            rn = lax.rsqrt(jnp.mean(gg * gg, axis=-1, keepdims=True) + EPS)
            gvec_ref[0:1, gs] += jnp.sum(dyb[:, gs] * gg * rn, axis=0, keepdims=True)
            dgn = dyb[:, gs] * nrm_ref[:, gs]
            dgt = rn * (dgn - gg * (rn * rn) * jnp.mean(dgn * gg, axis=-1, keepdims=True))
            dys = dgt * sz[:, gs]
            dys_s[:, gs] = dys
            dp_ref[:, C_Z + g * 512:C_Z + (g + 1) * 512] = (dgt * ys[:, gs] * _dsilu(z[:, gs])).astype(BF16)
        dys_all = dys_s[...]
        gvec_ref[1:2, :] += jnp.sum(dys_all * xs, axis=0, keepdims=True)

        def bwd_chunk(cc, carry):
            c = nc - 1 - cc
            r0 = pl.multiple_of(c * CHUNK, CHUNK)
            rows = pl.ds(r0, CHUNK)
            cumx = cumx_s[rows, :]
            cum_l = cumx[CHUNK - 1:CHUNK, :]
            xs_c = xs_s[rows, :]
            dtx = dtx_s[rows, :]
            xd = xs_c * dtx
            f = jnp.exp(cum_l - cumx)
            xf = xd * f
            ex = jnp.exp(cumx)
            e_l = jnp.exp(cum_l)
            rvec = jnp.sum(jnp.where(irep, cumx, 0.0), axis=0, keepdims=True)
            lam = jnp.where(causal, jnp.exp(jnp.where(causal, cumx - rvec, 0.0)), 0.0)
            bc = bc_s[rows, :]
            dyc = dys_s[rows, :]
            for g in range(2):
                gs = slice(g * 512, (g + 1) * 512)
                bg = bc[:, g * NSTATE:(g + 1) * NSTATE].astype(BF16)
                cg = bc[:, 256 + g * NSTATE:256 + (g + 1) * NSTATE].astype(BF16)
                h0 = st_ref[c, :, gs]
                dh = dhs[:, gs]
                dh_b = dh.astype(BF16)
                xf_g = xf[:, gs]
                dxf = _nn(bg, dh_b)
                db = _nt(xf_g.astype(BF16), dh_b)
                s_rep = _nn(_nt(cg, bg).astype(BF16), rep)
                lam_g = lam[:, gs]
                m_g = s_rep * lam_g
                m_b = m_g.astype(BF16)
                ex_g = ex[:, gs]
                dy_g = dyc[:, gs]
                yo = _nn(cg, h0) * ex_g
                dg_b = (dy_g * ex_g).astype(BF16)
                dc = _nt(dg_b, h0)
                el_g = e_l[:, gs]
                dee = jnp.sum(dh * h0.astype(F32), axis=0, keepdims=True) * el_g
                dhs[:, gs] = dh * el_g + _tn(cg, dg_b)
                xd_b = xd[:, gs].astype(BF16)
                dy_b = dy_g.astype(BF16)
                dm_parts, dxd_parts = [], []
                for hp in range(4):
                    ps = slice(hp * LANES, (hp + 1) * LANES)
                    bd = _blockdiag(xd_b[:, ps])
                    dm_parts.append(_nt(dy_b[:, ps], bd))
                    t2 = _tn(m_b[:, ps], dy_b[:, ps])
                    dxd_parts.append(jnp.where(lane128 < HDIM, t2[0:CHUNK], t2[CHUNK:2 * CHUNK]))
                dm = jnp.concatenate(dm_parts, axis=1)
                dxd = jnp.concatenate(dxd_parts, axis=1) + dxf * f[:, gs]
                dseg = dm * m_g
                ds_b = _nt((dm * lam_g).astype(BF16), rep).astype(BF16)
                dc = dc + _nn(ds_b, bg)
                db = db + _tn(ds_b, cg)
                colsum = jnp.sum(dseg, axis=0, keepdims=True)
                dxfxf = dxf * xf_g
                red = dseg - jnp.where(irep_g, colsum, 0.0) + dy_g * yo - dxfxf
                last = jnp.sum(dxfxf, axis=0, keepdims=True) + dee
                red = red + jnp.where(row64 == CHUNK - 1, last, 0.0)
                red_s[rows, gs] = red
                ddtx_s[rows, gs] = dxd * xs_c[:, gs]
                dxs_s[rows, gs] = dxd * dtx[:, gs] + dsk_ref[:, gs] * dy_g
                dbc_s[rows, g * NSTATE:(g + 1) * NSTATE] = db
                dbc_s[rows, 256 + g * NSTATE:256 + (g + 1) * NSTATE] = dc
            return carry

        lax.fori_loop(0, nc, bwd_chunk, 0)

        dcum = _head_reduce(red_s[...], eht)
        da = _chunk_rcumsum(dcum, pos)
        ddt = _head_reduce(ddtx_s[...], eht) + da * a_neg
        gdt_ref[1:2, :] += jnp.sum(da * dt, axis=0, keepdims=True) * a_neg
        ddt_raw = ddt * _sigmoid(dt_pre)
        lane_t = lax.broadcasted_iota(jnp.int32, (tt, LANES), 1)
        ddt_raw = jnp.where(lane_t < NHEAD, ddt_raw, 0.0)
        gdt_ref[0:1, :] += jnp.sum(ddt_raw, axis=0, keepdims=True)
        dp_ref[:, C_DT:C_DT + LANES] = ddt_raw.astype(BF16)

        dpx = dxs_s[...] * _dsilu(pre_x)
        dpbc = dbc_s[...] * _dsilu(pre_bc)
        gvec_ref[2:3, :] += jnp.sum(dpx, axis=0, keepdims=True)
        gcwbc_ref[4:5, :] += jnp.sum(dpbc, axis=0, keepdims=True)
        nxt_x = nx_px[...]
        nxt_bc = nx_pbc[...]
        dxr, dbcr = None, None
        for k in range(4):
            gcwx_ref[k:k + 1, :] += jnp.sum(dpx * _shift_dn(xraw, xrh, 3 - k), axis=0, keepdims=True)
            gcwbc_ref[k:k + 1, :] += jnp.sum(dpbc * _shift_dn(bcraw, bcrh, 3 - k), axis=0, keepdims=True)
            tx = cwx_ref[k:k + 1, :] * _shift_up(dpx, nxt_x, 3 - k)
            tb = cwbc_ref[k:k + 1, :] * _shift_up(dpbc, nxt_bc, 3 - k)
            dxr = tx if dxr is None else dxr + tx
            dbcr = tb if dbcr is None else dbcr + tb
        nx_px[...] = dpx[0:8]
        nx_pbc[...] = dpbc[0:8]
        dp_ref[:, C_XS:C_XS + D] = dxr.astype(BF16)
        dp_ref[:, C_BC:C_BC + 512] = dbcr.astype(BF16)

        @pl.when(i == nblk - 1)
        def _():
            gdt_ref[2:3, :] = _head_reduce(gvec_ref[1:2, :] * jnp.ones((8, 1), F32), eht)[0:1, :]

    def const(shape):
        return pl.BlockSpec(shape, lambda i: (0, 0))

    in_specs = [main(D, C_UB), main(D, C_UC), main(D, C_UH), main(D, C_Z), main(D, C_XS), main(512, C_BC),
                main(LANES, C_DT), halo(D, C_UC), halo(D, C_UH), halo(D, C_XS), halo(512, C_BC),
                pl.BlockSpec((tt, MIX), lambda i: (rev(i), 0)),
                pl.BlockSpec((nc, NSTATE, D), lambda i: (rev(i), 0, 0)),
                const((8, D)), const((8, D)), const((8, 512)), const((1, D)), const((1, 512)),
                const((1, LANES)), const((1, LANES)), const((1, D)), const((1, D)), const((LANES, D)),
                const((D, LANES))]
    return pl.pallas_call(
        body, grid=(nblk,),
        in_specs=in_specs,
        out_specs=[pl.BlockSpec((tt, NINP), lambda i: (rev(i), 0)),
                   const((8, D)), const((8, D)), const((8, 512)), const((8, D)), const((8, LANES))],
        out_shape=[SDS((t_len, NINP), BF16), SDS((8, D), F32), SDS((8, D), F32), SDS((8, 512), F32),
                   SDS((8, D), F32), SDS((8, LANES), F32)],
        scratch_shapes=[pltpu.VMEM((NSTATE, D), F32),
                        pltpu.VMEM((tt, D), F32), pltpu.VMEM((tt, 512), F32),
                        pltpu.VMEM((tt, D), F32), pltpu.VMEM((tt, D), F32),
                        pltpu.VMEM((tt, D), F32), pltpu.VMEM((tt, D), F32),
                        pltpu.VMEM((tt, 512), F32),
                        pltpu.VMEM((tt, D), F32), pltpu.VMEM((tt, D), F32),
                        pltpu.VMEM((8, D), F32), pltpu.VMEM((8, D), F32), pltpu.VMEM((8, 512), F32)],
        compiler_params=_cparams(("arbitrary",)), name="mixer_bwd")(
            *([proj] * 11), dy, states, prm["scw"], prm["cwx"], prm["cwbc"], prm["cbx"], prm["cbbc"], prm["dtb"],
            prm["alog"], prm["dskx"], prm["nrm"], prm["eh"], prm["eht"])


TN_IN = 1920


def layer_fwd(x, lw, prm, tt):
    proj, h1 = norm_matmul(x, lw["nw1"], lw["win"], "in_proj")
    y, st = mixer_fwd(proj, prm, tt)
    x1 = matmul_residual(y, lw["wout"], x, False, "out_proj")
    u, h2 = norm_matmul(x1, lw["nw2"], lw["wup"], "up_proj")
    x2 = matmul_residual(u, lw["wdn"], x1, True, "down_proj")
    return x2, (x, h1, proj, st, y, x1, h2, u)


def _dw(a, b, a_cols, b_cols, relu2, name):
    m_len, n_len = a.shape[1], b.shape[1]
    n_a, n_b = m_len // a_cols, n_len // b_cols
    assert n_a == 1 or n_b == 1
    if n_b == 1:
        return matmul_tn(
            a, b,
            lambda t_: pl.BlockSpec((t_, a_cols), lambda n, t: (t, n)),
            lambda t_: pl.BlockSpec((t_, n_len), lambda n, t: (t, 0)),
            pl.BlockSpec((a_cols, n_len), lambda n, t: (n, 0)), SDS((m_len, n_len), BF16), n_a, relu2, name)
    return matmul_tn(
        a, b,
        lambda t_: pl.BlockSpec((t_, m_len), lambda n, t: (t, 0)),
        lambda t_: pl.BlockSpec((t_, b_cols), lambda n, t: (t, n)),
        pl.BlockSpec((m_len, b_cols), lambda n, t: (0, n)), SDS((m_len, n_len), BF16), n_b, relu2, name)


def layer_bwd(dx2, dx2b, lw, prm, saved, tt):
    x, h1, proj, st, y, x1, h2, u = saved
    du = matmul_nt_act(dx2b, lw["wdn"], u, "mlp_bwd_du")
    g_wdn = _dw(u, dx2b, 1024, D, True, "dw_down")
    dx1, dx1b, g_nw2 = matmul_nt_norm_bwd(du, lw["wup"], x1, lw["nw2"], dx2, "mlp_bwd_dx")
    g_wup = _dw(h2, du, D, 1024, False, "dw_up")
    dy = matmul_nt_act(dx1b, lw["wout"], None, "out_bwd_dy")
    g_wout = _dw(y, dx1b, 1024, D, False, "dw_out")
    dproj, gscw, gcwx, gcwbc, gvec, gdt = mixer_bwd(proj, dy, st, prm, tt)
    dx0, dx0b, g_nw1 = matmul_nt_norm_bwd(dproj, lw["win"], x, lw["nw1"], dx1, "in_bwd_dx")
    g_win = _dw(h1, dproj, D, TN_IN, False, "dw_in")
    grads = {
        "win": g_win, "wout": g_wout, "wup": g_wup, "wdn": g_wdn,
        "scw": gscw[0:3], "cw": jnp.concatenate([gcwx[0:4], gcwbc[0:4]], axis=1),
        "cb": jnp.concatenate([gvec[2], gcwbc[4]], axis=0),
        "dtb": gdt[0, :NHEAD], "alog": gdt[1, :NHEAD], "dsk": gdt[2, :NHEAD],
        "nrm": gvec[0], "nw1": g_nw1[0], "nw2": g_nw2[0],
    }
    return dx0, dx0b, grads


def layer_params(i, norm_mix_w, win, wout, wup, wdn, scw, cw, ssd_conv_b, dt_bias, a_log, d_skip, ssd_norm_w,
                 norm_mlp_w, eh, eht):
    def rows8(a):
        return jnp.pad(a, ((0, 8 - a.shape[0]), (0, 0)))

    def lanes128(a):
        return jnp.pad(a, (0, LANES - a.shape[0]))[None, :]

    lw = {"win": win[i], "wout": wout[i], "wup": wup[i], "wdn": wdn[i],
          "nw1": norm_mix_w[i][None, :], "nw2": norm_mlp_w[i][None, :]}
    prm = {"scw": rows8(scw[i]), "cwx": rows8(cw[i][:, :D]), "cwbc": rows8(cw[i][:, D:]),
           "cbx": ssd_conv_b[i][None, :D], "cbbc": ssd_conv_b[i][None, D:],
           "dtb": lanes128(dt_bias[i]), "alog": lanes128(a_log[i]),
           "dskx": jnp.repeat(d_skip[i], HDIM)[None, :], "nrm": ssd_norm_w[i][None, :], "eh": eh, "eht": eht}
    return lw, prm


def _flip(v, bit):
    return 1 - v if bit else v


def all_gather(arrs, name):
    n = len(arrs)

    def body(*refs):
        ins, outs = refs[:n], refs[n:2 * n]
        send_sems, recv_sems, local_sems = refs[2 * n:]
        x, y, c = lax.axis_index("x"), lax.axis_index("y"), lax.axis_index("c")
        sibling = (x, y, 1 - c)
        chips = [(1 - x, y), (x, 1 - y), (1 - x, 1 - y)]

        def idx(px, py, pc):
            return 4 * px + 2 * py + pc

        def copy(a, k, block, to, src=None):
            dst = outs[a].at[idx(*block)]
            return pltpu.make_async_remote_copy(
                src_ref=dst if src is None else src, dst_ref=dst,
                send_sem=send_sems.at[a, k], recv_sem=recv_sems.at[a, k], device_id=to, device_id_type=MESH)

        me = (x, y, c)
        mine = [pltpu.make_async_copy(ins[a], outs[a].at[idx(*me)], local_sems.at[a]) for a in range(n)]
        for cp in mine:
            cp.start()
        first = []
        for a in range(n):
            first.append(copy(a, 0, me, sibling, src=ins[a]))
            first += [copy(a, 1 + j, me, (*chip, c), src=ins[a]) for j, chip in enumerate(chips)]
        for cp in first:
            cp.start()
        passed = []
        for j, chip in enumerate(chips):
            for a in range(n):
                copy(a, 1 + j, (*chip, c), me).wait_recv()
                cp = copy(a, 4 + j, (*chip, c), sibling)
                cp.start()
                passed.append(cp)
        for a in range(n):
            copy(a, 0, sibling, me).wait_recv()
            for j, chip in enumerate(chips):
                copy(a, 4 + j, (*chip, 1 - c), me).wait_recv()
        for cp in first + passed:
            cp.wait_send()
        for cp in mine:
            cp.wait()

    any_spec = pl.BlockSpec(memory_space=pl.ANY)
    return pl.pallas_call(
        body, in_specs=[any_spec] * n, out_specs=[any_spec] * n,
        out_shape=[SDS((N_DEV,) + a.shape, a.dtype) for a in arrs],
        scratch_shapes=[pltpu.SemaphoreType.DMA((n, 7)), pltpu.SemaphoreType.DMA((n, 7)),
                        pltpu.SemaphoreType.DMA((n,))],
        name=name)(*arrs)


def all_to_all(arrs, name):
    n = len(arrs)

    def body(*refs):
        ins, outs = refs[:n], refs[n:2 * n]
        send_sems, recv_sems, local_sems = refs[2 * n:]
        x, y, c = lax.axis_index("x"), lax.axis_index("y"), lax.axis_index("c")
        me = 4 * x + 2 * y + c
        mine = [pltpu.make_async_copy(ins[a].at[me], outs[a].at[me], local_sems.at[a]) for a in range(n)]
        for cp in mine:
            cp.start()
        sends = []
        for mask in range(1, N_DEV):
            px, py, pc = _flip(x, mask & 4), _flip(y, mask & 2), _flip(c, mask & 1)
            peer = 4 * px + 2 * py + pc
            for a in range(n):
                cp = pltpu.make_async_remote_copy(
                    src_ref=ins[a].at[peer], dst_ref=outs[a].at[me],
                    send_sem=send_sems.at[a, mask - 1], recv_sem=recv_sems.at[a, mask - 1],
                    device_id=(px, py, pc), device_id_type=MESH)
                cp.start()
                sends.append((cp, a, mask, peer))
        for cp, a, mask, peer in sends:
            pltpu.make_async_remote_copy(
                src_ref=ins[a].at[peer], dst_ref=outs[a].at[peer],
                send_sem=send_sems.at[a, mask - 1], recv_sem=recv_sems.at[a, mask - 1],
                device_id=(x, y, c), device_id_type=MESH).wait_recv()
        for cp, _, _, _ in sends:
            cp.wait_send()
        for cp in mine:
            cp.wait()

    any_spec = pl.BlockSpec(memory_space=pl.ANY)
    return pl.pallas_call(
        body, in_specs=[any_spec] * n, out_specs=[any_spec] * n,
        out_shape=[SDS(a.shape, a.dtype) for a in arrs],
        scratch_shapes=[pltpu.SemaphoreType.DMA((n, 7)), pltpu.SemaphoreType.DMA((n, 7)),
                        pltpu.SemaphoreType.DMA((n,))],
        name=name)(*arrs)


def adamw(w, slots, m, v, name):
    r_len, c_len = w.shape
    br = r_len if r_len <= 512 else 512
    assert r_len % br == 0

    def body(w_ref, s_ref, m_ref, v_ref, g_ref, d_ref, nm_ref, nv_ref):
        g = s_ref[0].astype(F32)
        for k in range(1, N_DEV):
            g = g + s_ref[k].astype(F32)
        mn = ADAM_B1 * m_ref[...] + (1.0 - ADAM_B1) * g
        vn = ADAM_B2 * v_ref[...] + (1.0 - ADAM_B2) * jnp.square(g)
        m_hat = mn / (1.0 - ADAM_B1 ** ADAM_STEP)
        v_hat = vn / (1.0 - ADAM_B2 ** ADAM_STEP)
        g_ref[...] = g
        d_ref[...] = -ADAM_LR * (m_hat / (jnp.sqrt(v_hat) + ADAM_EPS) + ADAM_WD * w_ref[...])
        nm_ref[...] = mn
        nv_ref[...] = vn

    spec = pl.BlockSpec((br, c_len), lambda i: (i, 0))
    return pl.pallas_call(
        body, grid=(r_len // br,),
        in_specs=[spec, pl.BlockSpec((N_DEV, br, c_len), lambda i: (0, i, 0)), spec, spec],
        out_specs=[spec] * 4, out_shape=[SDS((r_len, c_len), F32)] * 4,
        compiler_params=_cparams(("parallel",)), name=name)(w, slots, m, v)


def _adamw_nd(w, slots, m, v, name):
    shp = w.shape
    r = int(np.prod(shp[:-1]))
    outs = adamw(w.reshape(r, shp[-1]), slots.reshape(N_DEV, r, shp[-1]), m.reshape(r, shp[-1]),
                 v.reshape(r, shp[-1]), name)
    return [o.reshape(shp) for o in outs]


SMALL = [("norm_mix_w", DEPTH * D), ("ssd_conv_b", DEPTH * XBC), ("dt_bias", DEPTH * NHEAD),
         ("a_log", DEPTH * NHEAD), ("d_skip", DEPTH * NHEAD), ("ssd_norm_w", DEPTH * D),
         ("norm_mlp_w", DEPTH * D), ("final_norm_w", D)]
SMALL_LEN = sum(s for _, s in SMALL)
SMALL_ROWS = -(-SMALL_LEN // LANES)


def _pack_small(parts):
    flat = jnp.concatenate([parts[k].reshape(-1) for k, _ in SMALL])
    return jnp.pad(flat, (0, SMALL_ROWS * LANES - SMALL_LEN)).reshape(SMALL_ROWS, LANES)


def _unpack_small(packed, shapes):
    flat = packed.reshape(-1)
    out, off = {}, 0
    for k, s in SMALL:
        out[k] = flat[off:off + s].reshape(shapes[k])
        off += s
    return out


def kernel(x, norm_mix_w, w_in, short_conv_w, ssd_conv_w, ssd_conv_b, dt_bias, a_log, d_skip, ssd_norm_w, w_out, norm_mlp_w, w_up, w_down, final_norm_w, loss_target, m_norm_mix_w, m_w_in, m_short_conv_w, m_ssd_conv_w, m_ssd_conv_b, m_dt_bias, m_a_log, m_d_skip, m_ssd_norm_w, m_w_out, m_norm_mlp_w, m_w_up, m_w_down, m_final_norm_w, v_norm_mix_w, v_w_in, v_short_conv_w, v_ssd_conv_w, v_ssd_conv_b, v_dt_bias, v_a_log, v_d_skip, v_ssd_norm_w, v_w_out, v_norm_mlp_w, v_w_up, v_w_down, v_final_norm_w):
    xs = x[0]
    t_len = xs.shape[0]
    tt = min(256, t_len)
    eh, eht = _head_matrices()

    g_in, g_out, g_up, g_dn, g_sc, g_cw = all_gather(
        [w_in.astype(BF16), w_out.astype(BF16), w_up.astype(BF16), w_down.astype(BF16), short_conv_w, ssd_conv_w],
        "gather_weights")
    win = jnp.pad(g_in.transpose(1, 2, 0, 3).reshape(DEPTH, D, NIN), ((0, 0), (0, 0), (0, NINP - NIN)))
    wout = g_out.transpose(1, 0, 2, 3).reshape(DEPTH, MIX, D)
    wup = g_up.transpose(1, 2, 0, 3).reshape(DEPTH, D, DFF)
    wdn = g_dn.transpose(1, 0, 2, 3).reshape(DEPTH, DFF, D)
    scw = g_sc.transpose(1, 2, 0, 3).reshape(DEPTH, 3, D)
    cw = g_cw.transpose(1, 2, 0, 3).reshape(DEPTH, 4, XBC)

    layers = [layer_params(i, norm_mix_w, win, wout, wup, wdn, scw, cw, ssd_conv_b, dt_bias, a_log, d_skip,
                           ssd_norm_w, norm_mlp_w, eh, eht) for i in range(DEPTH)]
    act = xs
    saved = []
    for lw, prm in layers:
        act, sv = layer_fwd(act, lw, prm, tt)
        saved.append(sv)
    loss_acc, dx, dxb, g_fw = loss_head(act, final_norm_w[None, :], loss_target[0])

    grads = [None] * DEPTH
    for i in reversed(range(DEPTH)):
        lw, prm = layers[i]
        dx, dxb, grads[i] = layer_bwd(dx, dxb, lw, prm, saved[i], tt)

    def stack(k):
        return jnp.stack([g[k] for g in grads])

    p_in = stack("win")[:, :, :NIN].reshape(DEPTH, D, N_DEV, NIN // N_DEV).transpose(2, 0, 1, 3)
    p_out = stack("wout").reshape(DEPTH, N_DEV, MIX // N_DEV, D).transpose(1, 0, 2, 3)
    p_up = stack("wup").reshape(DEPTH, D, N_DEV, DFF // N_DEV).transpose(2, 0, 1, 3)
    p_dn = stack("wdn").reshape(DEPTH, N_DEV, DFF // N_DEV, D).transpose(1, 0, 2, 3)
    p_sc = stack("scw").reshape(DEPTH, 3, N_DEV, D // N_DEV).transpose(2, 0, 1, 3)
    p_cw = stack("cw").reshape(DEPTH, 4, N_DEV, XBC // N_DEV).transpose(2, 0, 1, 3)
    small = _pack_small({"norm_mix_w": stack("nw1"), "ssd_conv_b": stack("cb"), "dt_bias": stack("dtb"),
                         "a_log": stack("alog"), "d_skip": stack("dsk"), "ssd_norm_w": stack("nrm"),
                         "norm_mlp_w": stack("nw2"), "final_norm_w": g_fw[0]})
    r_in, r_out, r_up, r_dn, r_sc, r_cw = all_to_all([p_in, p_out, p_up, p_dn, p_sc, p_cw], "exchange_grads")
    (r_small,) = all_gather([small], "gather_small_grads")

    res = {}
    res["w_in"] = _adamw_nd(w_in, r_in, m_w_in, v_w_in, "adamw_w_in")
    res["w_out"] = _adamw_nd(w_out, r_out, m_w_out, v_w_out, "adamw_w_out")
    res["w_up"] = _adamw_nd(w_up, r_up, m_w_up, v_w_up, "adamw_w_up")
    res["w_down"] = _adamw_nd(w_down, r_dn, m_w_down, v_w_down, "adamw_w_down")
    res["short_conv_w"] = _adamw_nd(short_conv_w, r_sc, m_short_conv_w, v_short_conv_w, "adamw_short_conv")
    res["ssd_conv_w"] = _adamw_nd(ssd_conv_w, r_cw, m_ssd_conv_w, v_ssd_conv_w, "adamw_ssd_conv")
    small_w = {"norm_mix_w": norm_mix_w, "ssd_conv_b": ssd_conv_b, "dt_bias": dt_bias, "a_log": a_log,
               "d_skip": d_skip, "ssd_norm_w": ssd_norm_w, "norm_mlp_w": norm_mlp_w, "final_norm_w": final_norm_w}
    small_m = {"norm_mix_w": m_norm_mix_w, "ssd_conv_b": m_ssd_conv_b, "dt_bias": m_dt_bias, "a_log": m_a_log,
               "d_skip": m_d_skip, "ssd_norm_w": m_ssd_norm_w, "norm_mlp_w": m_norm_mlp_w,
               "final_norm_w": m_final_norm_w}
    small_v = {"norm_mix_w": v_norm_mix_w, "ssd_conv_b": v_ssd_conv_b, "dt_bias": v_dt_bias, "a_log": v_a_log,
               "d_skip": v_d_skip, "ssd_norm_w": v_ssd_norm_w, "norm_mlp_w": v_norm_mlp_w,
               "final_norm_w": v_final_norm_w}
    shapes = {k: a.shape for k, a in small_w.items()}
    packed = adamw(_pack_small(small_w), r_small, _pack_small(small_m), _pack_small(small_v), "adamw_small")
    unpacked = [_unpack_small(p, shapes) for p in packed]
    for k in small_w:
        res[k] = [u[k] for u in unpacked]

    loss = lax.psum(loss_acc[0, 0], ("x", "y", "c"))
    order = ["norm_mix_w", "w_in", "short_conv_w", "ssd_conv_w", "ssd_conv_b", "dt_bias", "a_log", "d_skip",
             "ssd_norm_w", "w_out", "norm_mlp_w", "w_up", "w_down", "final_norm_w"]
    out = [loss, dx[None]]
    for part in range(4):
        out += [res[k][part] for k in order]
    return tuple(out)
```

```python
import functools

import numpy as np
import jax
import jax.numpy as jnp
from jax import lax
from jax.experimental import pallas as pl
from jax.experimental.pallas import tpu as pltpu

F32 = jnp.float32
BF16 = jnp.bfloat16
SDS = jax.ShapeDtypeStruct

N_DEV = 8
DEPTH = 4
D = 1024
NIN = 5648
NINP = 5760
DFF = 4096
MIX = 2048
NHEAD = 16
HDIM = 64
NSTATE = 128
CHUNK = 64
XBC = 1536
EPS = 1e-5
LANES = 128

C_UB, C_UC, C_UH, C_Z, C_XS, C_BC, C_DT = 0, 1024, 2048, 3072, 4096, 5120, 5632

ADAM_LR = 0.001
ADAM_B1 = 0.9
ADAM_B2 = 0.999
ADAM_EPS = 1e-08
ADAM_WD = 0.01
ADAM_STEP = 10

VMEM_LIMIT = 56 * 1024 * 1024
MESH = pl.DeviceIdType.MESH


def _cparams(sem):
    return pltpu.CompilerParams(dimension_semantics=sem, vmem_limit_bytes=VMEM_LIMIT)


def _nt(a, b):
    return lax.dot_general(a, b, (((1,), (1,)), ((), ())), preferred_element_type=F32)


def _tn(a, b):
    return lax.dot_general(a, b, (((0,), (0,)), ((), ())), preferred_element_type=F32)


def _nn(a, b):
    return jnp.dot(a, b, preferred_element_type=F32)


def _sigmoid(v):
    return 1.0 / (1.0 + jnp.exp(-v))


def _split3(v):
    v1 = v.astype(BF16)
    r1 = v - v1.astype(F32)
    v2 = r1.astype(BF16)
    v3 = (r1 - v2.astype(F32)).astype(BF16)
    return v1, v2, v3


def _expand(v, eh):
    v1, v2, v3 = _split3(v)
    return _nn(v1, eh) + _nn(v2, eh) + _nn(v3, eh)


def _head_reduce(v, eht):
    v1, v2, v3 = _split3(v)
    return _nn(v1, eht) + _nn(v2, eht) + _nn(v3, eht)


def _head_matrices():
    eh = np.zeros((LANES, D), np.float32)
    for h in range(NHEAD):
        eh[h, h * HDIM:(h + 1) * HDIM] = 1.0
    return jnp.asarray(eh, BF16), jnp.asarray(eh.T.copy(), BF16)


def _resident(shape):
    return pl.BlockSpec(shape, lambda *_: (0,) * len(shape), pipeline_mode=pl.Buffered(1))


def _col_chunks(n, step):
    return [(c, min(c + step, n)) for c in range(0, n, step)]


def norm_matmul(x, nw, w, name):
    t_len = x.shape[0]
    blocked = w.ndim == 3
    n_len = w.shape[0] * w.shape[2] if blocked else w.shape[1]
    tm = min(512, t_len)
    chunks = _col_chunks(n_len, n_len // N_DEV if blocked else 1536)

    def body(x_ref, nw_ref, w_ref, o_ref, h_ref):
        xv = x_ref[...]
        r = lax.rsqrt(jnp.mean(xv * xv, axis=-1, keepdims=True) + EPS)
        hv = (xv * r * nw_ref[...]).astype(BF16)
        h_ref[...] = hv
        for j, (c0, c1) in enumerate(chunks):
            wj = w_ref[j] if blocked else w_ref[:, c0:c1]
            o_ref[:, c0:c1] = _nn(hv, wj).astype(o_ref.dtype)

    return pl.pallas_call(
        body, grid=(t_len // tm,),
        in_specs=[pl.BlockSpec((tm, D), lambda i: (i, 0)), _resident((1, D)), _resident(w.shape)],
        out_specs=[pl.BlockSpec((tm, n_len), lambda i: (i, 0)),
                   pl.BlockSpec((tm, D), lambda i: (i, 0))],
        out_shape=[SDS((t_len, n_len), BF16), SDS((t_len, D), BF16)],
        compiler_params=_cparams(("parallel",)), name=name)(x, nw, w)


def matmul_residual(a, w, res, relu2, name):
    t_len, k_len = a.shape
    tm = min(512, t_len)

    def body(a_ref, w_ref, res_ref, o_ref):
        av = a_ref[...]
        if relu2:
            af = jnp.maximum(av.astype(F32), 0.0)
            av = (af * af).astype(BF16)
        o_ref[...] = res_ref[...] + _nn(av, w_ref[...])

    return pl.pallas_call(
        body, grid=(t_len // tm,),
        in_specs=[pl.BlockSpec((tm, k_len), lambda i: (i, 0)),
                  _resident((k_len, D)),
                  pl.BlockSpec((tm, D), lambda i: (i, 0))],
        out_specs=pl.BlockSpec((tm, D), lambda i: (i, 0)),
        out_shape=SDS((t_len, D), F32),
        compiler_params=_cparams(("parallel",)), name=name)(a, w, res)


def matmul_nt_act(dy, w, u, name):
    t_len = dy.shape[0]
    n_len = w.shape[0]
    tm = min(512, t_len)
    chunks = _col_chunks(n_len, 1024)

    def body(dy_ref, w_ref, *rest):
        if u is None:
            (o_ref,) = rest
        else:
            u_ref, o_ref = rest
        dyv = dy_ref[...]
        for c0, c1 in chunks:
            p = _nt(dyv, w_ref[c0:c1, :])
            if u is not None:
                p = p * (2.0 * jnp.maximum(u_ref[:, c0:c1].astype(F32), 0.0))
            o_ref[:, c0:c1] = p.astype(o_ref.dtype)

    in_specs = [pl.BlockSpec((tm, D), lambda i: (i, 0)), _resident((n_len, D))]
    args = [dy, w]
    if u is not None:
        in_specs.append(pl.BlockSpec((tm, n_len), lambda i: (i, 0)))
        args.append(u)
    return pl.pallas_call(
        body, grid=(t_len // tm,),
        in_specs=in_specs,
        out_specs=pl.BlockSpec((tm, n_len), lambda i: (i, 0)),
        out_shape=SDS((t_len, n_len), BF16),
        compiler_params=_cparams(("parallel",)), name=name)(*args)


def matmul_tn(a, b, a_spec, b_spec, o_spec, o_shape, n_out, relu2, name):
    t_len = a.shape[0]
    tt = min(2048, t_len)
    nt = t_len // tt

    def body(a_ref, b_ref, o_ref, acc):
        t = pl.program_id(1)
        av = a_ref[...]
        if relu2:
            af = jnp.maximum(av.astype(F32), 0.0)
            av = (af * af).astype(BF16)
        p = _tn(av, b_ref[...])

        @pl.when(t == 0)
        def _():
            acc[...] = p

        @pl.when(t > 0)
        def _():
            acc[...] += p

        @pl.when(t == nt - 1)
        def _():
            if len(blk) == 3:
                for j in range(blk[0]):
                    o_ref[j] = acc[:, j * blk[2]:(j + 1) * blk[2]].astype(o_ref.dtype)
            else:
                o_ref[...] = acc[...].astype(o_ref.dtype)

    blk = tuple(o_spec.block_shape)
    acc_shape = (blk[1], blk[0] * blk[2]) if len(blk) == 3 else blk
    return pl.pallas_call(
        body, grid=(n_out, nt),
        in_specs=[a_spec(tt), b_spec(tt)],
        out_specs=o_spec, out_shape=o_shape,
        scratch_shapes=[pltpu.VMEM(acc_shape, F32)],
        compiler_params=_cparams(("parallel", "arbitrary")), name=name)(a, b)


def matmul_nt_norm_bwd(dy, w, x, nw, dres, name):
    t_len = x.shape[0]
    blocked = w.ndim == 3
    k_len = dy.shape[1]
    kb = k_len // N_DEV
    tm = min(512, t_len)

    def body(dy_ref, w_ref, x_ref, nw_ref, dres_ref, dx_ref, dxb_ref, dnw_ref):
        @pl.when(pl.program_id(0) == 0)
        def _():
            dnw_ref[...] = jnp.zeros_like(dnw_ref)

        if blocked:
            dh = _nt(dy_ref[:, 0:kb], w_ref[0])
            for j in range(1, N_DEV):
                dh = dh + _nt(dy_ref[:, j * kb:(j + 1) * kb], w_ref[j])
        else:
            dh = _nt(dy_ref[...], w_ref[...])
        xv = x_ref[...]
        r = lax.rsqrt(jnp.mean(xv * xv, axis=-1, keepdims=True) + EPS)
        xh = xv * r
        dnw_ref[0:1, :] += jnp.sum(dh * xh, axis=0, keepdims=True)
        g = dh * nw_ref[...]
        dx = dres_ref[...] + r * (g - xh * jnp.mean(g * xh, axis=-1, keepdims=True))
        dx_ref[...] = dx
        dxb_ref[...] = dx.astype(BF16)

    return pl.pallas_call(
        body, grid=(t_len // tm,),
        in_specs=[pl.BlockSpec((tm, k_len), lambda i: (i, 0)),
                  _resident(w.shape),
                  pl.BlockSpec((tm, D), lambda i: (i, 0)),
                  _resident((1, D)),
                  pl.BlockSpec((tm, D), lambda i: (i, 0))],
        out_specs=[pl.BlockSpec((tm, D), lambda i: (i, 0)),
                   pl.BlockSpec((tm, D), lambda i: (i, 0)),
                   pl.BlockSpec((8, D), lambda i: (0, 0))],
        out_shape=[SDS((t_len, D), F32), SDS((t_len, D), BF16), SDS((8, D), F32)],
        compiler_params=_cparams(("arbitrary",)), name=name)(dy, w, x, nw, dres)


def loss_head(x, fw, tgt):
    t_len = x.shape[0]
    tm = min(512, t_len)

    def body(x_ref, fw_ref, t_ref, loss_ref, dx_ref, dxb_ref, dfw_ref):
        @pl.when(pl.program_id(0) == 0)
        def _():
            loss_ref[...] = jnp.zeros_like(loss_ref)
            dfw_ref[...] = jnp.zeros_like(dfw_ref)
        xv = x_ref[...]
        r = lax.rsqrt(jnp.mean(xv * xv, axis=-1, keepdims=True) + EPS)
        xh = xv * r
        w = fw_ref[...]
        e = xh * w - t_ref[...]
        row = jnp.sum(e * e, axis=-1, keepdims=True) * (1.0 / D)
        loss_ref[...] += 0.5 * jnp.sum(row, axis=0, keepdims=True)
        dyf = e * (1.0 / D)
        dfw_ref[0:1, :] += jnp.sum(dyf * xh, axis=0, keepdims=True)
        g = dyf * w
        dx = r * (g - xh * jnp.mean(g * xh, axis=-1, keepdims=True))
        dx_ref[...] = dx
        dxb_ref[...] = dx.astype(BF16)

    return pl.pallas_call(
        body, grid=(t_len // tm,),
        in_specs=[pl.BlockSpec((tm, D), lambda i: (i, 0)),
                  pl.BlockSpec((1, D), lambda i: (0, 0)),
                  pl.BlockSpec((tm, D), lambda i: (i, 0))],
        out_specs=[pl.BlockSpec((8, LANES), lambda i: (0, 0)),
                   pl.BlockSpec((tm, D), lambda i: (i, 0)),
                   pl.BlockSpec((tm, D), lambda i: (i, 0)),
                   pl.BlockSpec((8, D), lambda i: (0, 0))],
        out_shape=[SDS((8, LANES), F32), SDS((t_len, D), F32), SDS((t_len, D), BF16), SDS((8, D), F32)],
        compiler_params=_cparams(("arbitrary",)), name="loss_head")(x, fw, tgt)


def _shift_dn(x, halo, j):
    if j == 0:
        return x
    xr = pltpu.roll(x, j, 0)
    hr = pltpu.roll(halo, j, 0)
    row = lax.broadcasted_iota(jnp.int32, hr.shape, 0)
    top = jnp.where(row < j, hr, xr[0:8])
    return jnp.concatenate([top, xr[8:]], axis=0)


def _shift_up(x, nxt, j):
    if j == 0:
        return x
    n = x.shape[0]
    xr = pltpu.roll(x, n - j, 0)
    hr = pltpu.roll(nxt, 8 - j, 0)
    row = lax.broadcasted_iota(jnp.int32, hr.shape, 0)
    bot = jnp.where(row >= 8 - j, hr, xr[n - 8:n])
    return jnp.concatenate([xr[:n - 8], bot], axis=0)


def _conv_fwd(x, halo, w_ref, kw):
    acc = None
    for k in range(kw):
        term = w_ref[k:k + 1, :] * _shift_dn(x, halo, kw - 1 - k)
        acc = term if acc is None else acc + term
    return acc


def _chunk_cumsum(a, pos):
    for sh in (1, 2, 4, 8, 16, 32):
        a = a + jnp.where(pos >= sh, pltpu.roll(a, sh, 0), 0.0)
    return a


def _chunk_rcumsum(a, pos):
    n = a.shape[0]
    for sh in (1, 2, 4, 8, 16, 32):
        a = a + jnp.where(pos < CHUNK - sh, pltpu.roll(a, n - sh, 0), 0.0)
    return a


def _softplus(v):
    return jnp.maximum(v, 0.0) + jnp.log(1.0 + jnp.exp(-jnp.abs(v)))


def _silu(v):
    return v * _sigmoid(v)


def _dsilu(v):
    s = _sigmoid(v)
    return s * (1.0 + v * (1.0 - s))


def _lane_masks(width=D):
    lane = lax.broadcasted_iota(jnp.int32, (CHUNK, width), 1) & (HDIM - 1)
    row = lax.broadcasted_iota(jnp.int32, (CHUNK, width), 0)
    return lane == row, lane <= row


def _rep_matrix():
    lane = lax.broadcasted_iota(jnp.int32, (CHUNK, 512), 1) & (HDIM - 1)
    row = lax.broadcasted_iota(jnp.int32, (CHUNK, 512), 0)
    return jnp.where(lane == row, 1.0, 0.0).astype(BF16)


def _blockdiag(xp):
    lane = lax.broadcasted_iota(jnp.int32, xp.shape, 1)
    zero = jnp.zeros_like(xp)
    return jnp.concatenate([jnp.where(lane < HDIM, xp, zero), jnp.where(lane >= HDIM, xp, zero)], axis=0)


def _mixer_views(tt):
    r8 = tt // 8

    def main(width, col):
        return pl.BlockSpec((tt, width), lambda i, c=col // width: (i, c))

    def halo(width, col):
        return pl.BlockSpec((8, width), lambda i, c=col // width: (jnp.maximum(i * r8 - 1, 0), c))

    return main, halo


def mixer_fwd(proj, prm, tt):
    t_len = proj.shape[0]
    nblk = t_len // tt
    nc = tt // CHUNK
    main, halo = _mixer_views(tt)

    def body(ub_ref, uc_ref, uh_ref, z_ref, xr_ref, bcr_ref, dtr_ref, uch_ref, uhh_ref, xrh_ref, bcrh_ref,
             scw_ref, cwx_ref, cwbc_ref, cbx_ref, cbbc_ref, dtb_ref, alog_ref, dsk_ref, nrm_ref, eh_ref,
             y_ref, st_ref, hs, xs_s, bc_s, dtx_s, cumx_s, yssd_s):
        i = pl.program_id(0)
        first = i == 0

        @pl.when(first)
        def _():
            hs[...] = jnp.zeros_like(hs)

        keep = jnp.where(first, 0.0, 1.0)
        v = uc_ref[...].astype(F32) * uh_ref[...].astype(F32)
        vh = uch_ref[...].astype(F32) * uhh_ref[...].astype(F32) * keep
        y_ref[:, 0:D] = (ub_ref[...].astype(F32) * _conv_fwd(v, vh, scw_ref, 3)).astype(BF16)

        xs = _silu(_conv_fwd(xr_ref[...].astype(F32), xrh_ref[...].astype(F32) * keep, cwx_ref, 4) + cbx_ref[...])
        xs_s[...] = xs
        bc_s[...] = _silu(_conv_fwd(bcr_ref[...].astype(F32), bcrh_ref[...].astype(F32) * keep, cwbc_ref, 4)
                          + cbbc_ref[...])
        dt = _softplus(dtr_ref[...].astype(F32) + dtb_ref[...])
        a_neg = -jnp.exp(alog_ref[...])
        pos = lax.broadcasted_iota(jnp.int32, (tt, LANES), 0) & (CHUNK - 1)
        cum = _chunk_cumsum(dt * a_neg, pos)
        eh = eh_ref[...]
        dtx_s[...] = _expand(dt, eh)
        cumx_s[...] = _expand(cum, eh)
        irep, causal = _lane_masks()
        rep = _rep_matrix()

        def chunk(c, carry):
            r0 = pl.multiple_of(c * CHUNK, CHUNK)
            rows = pl.ds(r0, CHUNK)
            cumx = cumx_s[rows, :]
            cum_l = cumx[CHUNK - 1:CHUNK, :]
            xd = xs_s[rows, :] * dtx_s[rows, :]
            xf = xd * jnp.exp(cum_l - cumx)
            ex = jnp.exp(cumx)
            e_l = jnp.exp(cum_l)
            rvec = jnp.sum(jnp.where(irep, cumx, 0.0), axis=0, keepdims=True)
            lam = jnp.where(causal, jnp.exp(jnp.where(causal, cumx - rvec, 0.0)), 0.0)
            bc = bc_s[rows, :]
            for g in range(2):
                gs = slice(g * 512, (g + 1) * 512)
                bg = bc[:, g * NSTATE:(g + 1) * NSTATE].astype(BF16)
                cg = bc[:, 256 + g * NSTATE:256 + (g + 1) * NSTATE].astype(BF16)
                s_rep = _nn(_nt(cg, bg).astype(BF16), rep)
                m_g = (s_rep * lam[:, gs]).astype(BF16)
                h_g = hs[:, gs]
                h_b = h_g.astype(BF16)
                st_ref[c, :, gs] = h_b
                yo = _nn(cg, h_b) * ex[:, gs]
                xd_b = xd[:, gs].astype(BF16)
                for hp in range(4):
                    ps = slice(hp * LANES, (hp + 1) * LANES)
                    yd = _nn(m_g[:, ps], _blockdiag(xd_b[:, ps]))
                    yssd_s[rows, g * 512 + hp * LANES:g * 512 + (hp + 1) * LANES] = yd + yo[:, ps]
                hs[:, gs] = h_g * e_l[:, gs] + _tn(bg, xf[:, gs].astype(BF16))
            return carry

        lax.fori_loop(0, nc, chunk, 0)

        ys = yssd_s[...] + dsk_ref[...] * xs_s[...]
        gt = ys * _silu(z_ref[...].astype(F32))
        for g in range(2):
            gs = slice(g * 512, (g + 1) * 512)
            gg = gt[:, gs]
            rn = lax.rsqrt(jnp.mean(gg * gg, axis=-1, keepdims=True) + EPS)
            y_ref[:, D + g * 512:D + (g + 1) * 512] = (gg * rn * nrm_ref[:, gs]).astype(BF16)

    def const(shape):
        return pl.BlockSpec(shape, lambda i: (0, 0))

    in_specs = [main(D, C_UB), main(D, C_UC), main(D, C_UH), main(D, C_Z), main(D, C_XS), main(512, C_BC),
                main(LANES, C_DT), halo(D, C_UC), halo(D, C_UH), halo(D, C_XS), halo(512, C_BC),
                const((8, D)), const((8, D)), const((8, 512)), const((1, D)), const((1, 512)),
                const((1, LANES)), const((1, LANES)), const((1, D)), const((1, D)), const((LANES, D))]
    return pl.pallas_call(
        body, grid=(nblk,),
        in_specs=in_specs,
        out_specs=[pl.BlockSpec((tt, MIX), lambda i: (i, 0)),
                   pl.BlockSpec((nc, NSTATE, D), lambda i: (i, 0, 0))],
        out_shape=[SDS((t_len, MIX), BF16), SDS((t_len // CHUNK, NSTATE, D), BF16)],
        scratch_shapes=[pltpu.VMEM((NSTATE, D), F32), pltpu.VMEM((tt, D), F32), pltpu.VMEM((tt, 512), F32),
                        pltpu.VMEM((tt, D), F32), pltpu.VMEM((tt, D), F32), pltpu.VMEM((tt, D), F32)],
        compiler_params=_cparams(("arbitrary",)), name="mixer_fwd")(
            *([proj] * 11), prm["scw"], prm["cwx"], prm["cwbc"], prm["cbx"], prm["cbbc"], prm["dtb"],
            prm["alog"], prm["dskx"], prm["nrm"], prm["eh"])


def mixer_bwd(proj, dy, states, prm, tt):
    t_len = proj.shape[0]
    nblk = t_len // tt
    nc = tt // CHUNK
    r8 = tt // 8

    def rev(i):
        return nblk - 1 - i

    def main(width, col):
        return pl.BlockSpec((tt, width), lambda i, c=col // width: (rev(i), c))

    def halo(width, col):
        return pl.BlockSpec((8, width), lambda i, c=col // width: (jnp.maximum(rev(i) * r8 - 1, 0), c))

    def body(ub_ref, uc_ref, uh_ref, z_ref, xr_ref, bcr_ref, dtr_ref, uch_ref, uhh_ref, xrh_ref, bcrh_ref,
             dy_ref, st_ref,
             scw_ref, cwx_ref, cwbc_ref, cbx_ref, cbbc_ref, dtb_ref, alog_ref, dsk_ref, nrm_ref, eh_ref, eht_ref,
             dp_ref, gscw_ref, gcwx_ref, gcwbc_ref, gvec_ref, gdt_ref,
             dhs, xs_s, bc_s, dtx_s, cumx_s, dys_s, dxs_s, dbc_s, red_s, ddtx_s, nx_cv, nx_px, nx_pbc):
        i = pl.program_id(0)
        blk = rev(i)

        @pl.when(i == 0)
        def _():
            dhs[...] = jnp.zeros_like(dhs)
            nx_cv[...] = jnp.zeros_like(nx_cv)
            nx_px[...] = jnp.zeros_like(nx_px)
            nx_pbc[...] = jnp.zeros_like(nx_pbc)
            gscw_ref[...] = jnp.zeros_like(gscw_ref)
            gcwx_ref[...] = jnp.zeros_like(gcwx_ref)
            gcwbc_ref[...] = jnp.zeros_like(gcwbc_ref)
            gvec_ref[...] = jnp.zeros_like(gvec_ref)
            gdt_ref[...] = jnp.zeros_like(gdt_ref)

        keep = jnp.where(blk == 0, 0.0, 1.0)

        ub = ub_ref[...].astype(F32)
        uc = uc_ref[...].astype(F32)
        uh = uh_ref[...].astype(F32)
        v = uc * uh
        vh = uch_ref[...].astype(F32) * uhh_ref[...].astype(F32) * keep
        dya = dy_ref[:, 0:D].astype(F32)
        dp_ref[:, C_UB:C_UB + D] = (dya * _conv_fwd(v, vh, scw_ref, 3)).astype(BF16)
        dcv = dya * ub
        nxt = nx_cv[...]
        dv = None
        for k in range(3):
            gscw_ref[k:k + 1, :] += jnp.sum(dcv * _shift_dn(v, vh, 2 - k), axis=0, keepdims=True)
            term = scw_ref[k:k + 1, :] * _shift_up(dcv, nxt, 2 - k)
            dv = term if dv is None else dv + term
        nx_cv[...] = dcv[0:8]
        dp_ref[:, C_UC:C_UC + D] = (dv * uh).astype(BF16)
        dp_ref[:, C_UH:C_UH + D] = (dv * uc).astype(BF16)

        xraw = xr_ref[...].astype(F32)
        xrh = xrh_ref[...].astype(F32) * keep
        bcraw = bcr_ref[...].astype(F32)
        bcrh = bcrh_ref[...].astype(F32) * keep
        pre_x = _conv_fwd(xraw, xrh, cwx_ref, 4) + cbx_ref[...]
        pre_bc = _conv_fwd(bcraw, bcrh, cwbc_ref, 4) + cbbc_ref[...]
        xs = _silu(pre_x)
        xs_s[...] = xs
        bc_s[...] = _silu(pre_bc)
        dt_pre = dtr_ref[...].astype(F32) + dtb_ref[...]
        dt = _softplus(dt_pre)
        a_neg = -jnp.exp(alog_ref[...])
        pos = lax.broadcasted_iota(jnp.int32, (tt, LANES), 0) & (CHUNK - 1)
        cum = _chunk_cumsum(dt * a_neg, pos)
        eh = eh_ref[...]
        eht = eht_ref[...]
        dtx_s[...] = _expand(dt, eh)
        cumx_s[...] = _expand(cum, eh)

        irep, causal = _lane_masks()
        irep_g, _ = _lane_masks(512)
        rep = _rep_matrix()
        row64 = lax.broadcasted_iota(jnp.int32, (CHUNK, 512), 0)
        lane128 = lax.broadcasted_iota(jnp.int32, (CHUNK, LANES), 1)

        def fwd_chunk(c, carry):
            r0 = pl.multiple_of(c * CHUNK, CHUNK)
            rows = pl.ds(r0, CHUNK)
            cumx = cumx_s[rows, :]
            xd = xs_s[rows, :] * dtx_s[rows, :]
            ex = jnp.exp(cumx)
            rvec = jnp.sum(jnp.where(irep, cumx, 0.0), axis=0, keepdims=True)
            lam = jnp.where(causal, jnp.exp(jnp.where(causal, cumx - rvec, 0.0)), 0.0)
            bc = bc_s[rows, :]
            for g in range(2):
                gs = slice(g * 512, (g + 1) * 512)
                bg = bc[:, g * NSTATE:(g + 1) * NSTATE].astype(BF16)
                cg = bc[:, 256 + g * NSTATE:256 + (g + 1) * NSTATE].astype(BF16)
                s_rep = _nn(_nt(cg, bg).astype(BF16), rep)
                m_g = (s_rep * lam[:, gs]).astype(BF16)
                yo = _nn(cg, st_ref[c, :, gs]) * ex[:, gs]
                xd_b = xd[:, gs].astype(BF16)
                for hp in range(4):
                    ps = slice(hp * LANES, (hp + 1) * LANES)
                    yd = _nn(m_g[:, ps], _blockdiag(xd_b[:, ps]))
                    dys_s[rows, g * 512 + hp * LANES:g * 512 + (hp + 1) * LANES] = yd + yo[:, ps]
            return carry

        lax.fori_loop(0, nc, fwd_chunk, 0)

        z = z_ref[...].astype(F32)
        sz = _silu(z)
        ys = dys_s[...] + dsk_ref[...] * xs
        gt = ys * sz
        dyb = dy_ref[:, D:MIX].astype(F32)
        for g in range(2):
            gs = slice(g * 512, (g + 1) * 512)
            gg = gt[:, gs]
            rn = lax.rsqrt(jnp.mean(gg * gg, axis=-1, keepdims=True) + EPS)
            gvec_ref[0:1, gs] += jnp.sum(dyb[:, gs] * gg * rn, axis=0, keepdims=True)
            dgn = dyb[:, gs] * nrm_ref[:, gs]
            dgt = rn * (dgn - gg * (rn * rn) * jnp.mean(dgn * gg, axis=-1, keepdims=True))
            dys = dgt * sz[:, gs]
            dys_s[:, gs] = dys
            dp_ref[:, C_Z + g * 512:C_Z + (g + 1) * 512] = (dgt * ys[:, gs] * _dsilu(z[:, gs])).astype(BF16)
        dys_all = dys_s[...]
        gvec_ref[1:2, :] += jnp.sum(dys_all * xs, axis=0, keepdims=True)

        def bwd_chunk(cc, carry):
            c = nc - 1 - cc
            r0 = pl.multiple_of(c * CHUNK, CHUNK)
            rows = pl.ds(r0, CHUNK)
            cumx = cumx_s[rows, :]
            cum_l = cumx[CHUNK - 1:CHUNK, :]
            xs_c = xs_s[rows, :]
            dtx = dtx_s[rows, :]
            xd = xs_c * dtx
            f = jnp.exp(cum_l - cumx)
            xf = xd * f
            ex = jnp.exp(cumx)
            e_l = jnp.exp(cum_l)
            rvec = jnp.sum(jnp.where(irep, cumx, 0.0), axis=0, keepdims=True)
            lam = jnp.where(causal, jnp.exp(jnp.where(causal, cumx - rvec, 0.0)), 0.0)
            bc = bc_s[rows, :]
            dyc = dys_s[rows, :]
            for g in range(2):
                gs = slice(g * 512, (g + 1) * 512)
                bg = bc[:, g * NSTATE:(g + 1) * NSTATE].astype(BF16)
                cg = bc[:, 256 + g * NSTATE:256 + (g + 1) * NSTATE].astype(BF16)
                h0 = st_ref[c, :, gs]
                dh = dhs[:, gs]
                dh_b = dh.astype(BF16)
                xf_g = xf[:, gs]
                dxf = _nn(bg, dh_b)
                db = _nt(xf_g.astype(BF16), dh_b)
                s_rep = _nn(_nt(cg, bg).astype(BF16), rep)
                lam_g = lam[:, gs]
                m_g = s_rep * lam_g
                m_b = m_g.astype(BF16)
                ex_g = ex[:, gs]
                dy_g = dyc[:, gs]
                yo = _nn(cg, h0) * ex_g
                dg_b = (dy_g * ex_g).astype(BF16)
                dc = _nt(dg_b, h0)
                el_g = e_l[:, gs]
                dee = jnp.sum(dh * h0.astype(F32), axis=0, keepdims=True) * el_g
                dhs[:, gs] = dh * el_g + _tn(cg, dg_b)
                xd_b = xd[:, gs].astype(BF16)
                dy_b = dy_g.astype(BF16)
                dm_parts, dxd_parts = [], []
                for hp in range(4):
                    ps = slice(hp * LANES, (hp + 1) * LANES)
                    bd = _blockdiag(xd_b[:, ps])
                    dm_parts.append(_nt(dy_b[:, ps], bd))
                    t2 = _tn(m_b[:, ps], dy_b[:, ps])
                    dxd_parts.append(jnp.where(lane128 < HDIM, t2[0:CHUNK], t2[CHUNK:2 * CHUNK]))
                dm = jnp.concatenate(dm_parts, axis=1)
                dxd = jnp.concatenate(dxd_parts, axis=1) + dxf * f[:, gs]
                dseg = dm * m_g
                ds_b = _nt((dm * lam_g).astype(BF16), rep).astype(BF16)
                dc = dc + _nn(ds_b, bg)
                db = db + _tn(ds_b, cg)
                colsum = jnp.sum(dseg, axis=0, keepdims=True)
                dxfxf = dxf * xf_g
                red = dseg - jnp.where(irep_g, colsum, 0.0) + dy_g * yo - dxfxf
                last = jnp.sum(dxfxf, axis=0, keepdims=True) + dee
                red = red + jnp.where(row64 == CHUNK - 1, last, 0.0)
                red_s[rows, gs] = red
                ddtx_s[rows, gs] = dxd * xs_c[:, gs]
                dxs_s[rows, gs] = dxd * dtx[:, gs] + dsk_ref[:, gs] * dy_g
                dbc_s[rows, g * NSTATE:(g + 1) * NSTATE] = db
                dbc_s[rows, 256 + g * NSTATE:256 + (g + 1) * NSTATE] = dc
            return carry

        lax.fori_loop(0, nc, bwd_chunk, 0)

        dcum = _head_reduce(red_s[...], eht)
        da = _chunk_rcumsum(dcum, pos)
        ddt = _head_reduce(ddtx_s[...], eht) + da * a_neg
        gdt_ref[1:2, :] += jnp.sum(da * dt, axis=0, keepdims=True) * a_neg
        ddt_raw = ddt * _sigmoid(dt_pre)
        lane_t = lax.broadcasted_iota(jnp.int32, (tt, LANES), 1)
        ddt_raw = jnp.where(lane_t < NHEAD, ddt_raw, 0.0)
        gdt_ref[0:1, :] += jnp.sum(ddt_raw, axis=0, keepdims=True)
        dp_ref[:, C_DT:C_DT + LANES] = ddt_raw.astype(BF16)

        dpx = dxs_s[...] * _dsilu(pre_x)
        dpbc = dbc_s[...] * _dsilu(pre_bc)
        gvec_ref[2:3, :] += jnp.sum(dpx, axis=0, keepdims=True)
        gcwbc_ref[4:5, :] += jnp.sum(dpbc, axis=0, keepdims=True)
        nxt_x = nx_px[...]
        nxt_bc = nx_pbc[...]
        dxr, dbcr = None, None
        for k in range(4):
            gcwx_ref[k:k + 1, :] += jnp.sum(dpx * _shift_dn(xraw, xrh, 3 - k), axis=0, keepdims=True)
            gcwbc_ref[k:k + 1, :] += jnp.sum(dpbc * _shift_dn(bcraw, bcrh, 3 - k), axis=0, keepdims=True)
            tx = cwx_ref[k:k + 1, :] * _shift_up(dpx, nxt_x, 3 - k)
            tb = cwbc_ref[k:k + 1, :] * _shift_up(dpbc, nxt_bc, 3 - k)
            dxr = tx if dxr is None else dxr + tx
            dbcr = tb if dbcr is None else dbcr + tb
        nx_px[...] = dpx[0:8]
        nx_pbc[...] = dpbc[0:8]
        dp_ref[:, C_XS:C_XS + D] = dxr.astype(BF16)
        dp_ref[:, C_BC:C_BC + 512] = dbcr.astype(BF16)

        @pl.when(i == nblk - 1)
        def _():
            gdt_ref[2:3, :] = _head_reduce(gvec_ref[1:2, :] * jnp.ones((8, 1), F32), eht)[0:1, :]

    def const(shape):
        return pl.BlockSpec(shape, lambda i: (0, 0))

    in_specs = [main(D, C_UB), main(D, C_UC), main(D, C_UH), main(D, C_Z), main(D, C_XS), main(512, C_BC),
                main(LANES, C_DT), halo(D, C_UC), halo(D, C_UH), halo(D, C_XS), halo(512, C_BC),
                pl.BlockSpec((tt, MIX), lambda i: (rev(i), 0)),
                pl.BlockSpec((nc, NSTATE, D), lambda i: (rev(i), 0, 0)),
                const((8, D)), const((8, D)), const((8, 512)), const((1, D)), const((1, 512)),
                const((1, LANES)), const((1, LANES)), const((1, D)), const((1, D)), const((LANES, D)),
                const((D, LANES))]
    return pl.pallas_call(
        body, grid=(nblk,),
        in_specs=in_specs,
        out_specs=[pl.BlockSpec((tt, NINP), lambda i: (rev(i), 0)),
                   const((8, D)), const((8, D)), const((8, 512)), const((8, D)), const((8, LANES))],
        out_shape=[SDS((t_len, NINP), BF16), SDS((8, D), F32), SDS((8, D), F32), SDS((8, 512), F32),
                   SDS((8, D), F32), SDS((8, LANES), F32)],
        scratch_shapes=[pltpu.VMEM((NSTATE, D), F32),
                        pltpu.VMEM((tt, D), F32), pltpu.VMEM((tt, 512), F32),
                        pltpu.VMEM((tt, D), F32), pltpu.VMEM((tt, D), F32),
                        pltpu.VMEM((tt, D), F32), pltpu.VMEM((tt, D), F32),
                        pltpu.VMEM((tt, 512), F32),
                        pltpu.VMEM((tt, D), F32), pltpu.VMEM((tt, D), F32),
                        pltpu.VMEM((8, D), F32), pltpu.VMEM((8, D), F32), pltpu.VMEM((8, 512), F32)],
        compiler_params=_cparams(("arbitrary",)), name="mixer_bwd")(
            *([proj] * 11), dy, states, prm["scw"], prm["cwx"], prm["cwbc"], prm["cbx"], prm["cbbc"], prm["dtb"],
            prm["alog"], prm["dskx"], prm["nrm"], prm["eh"], prm["eht"])


TN_IN = 1920


def layer_fwd(x, lw, prm, tt):
    proj, h1 = norm_matmul(x, lw["nw1"], lw["win"], "in_proj")
    y, st = mixer_fwd(proj, prm, tt)
    x1 = matmul_residual(y, lw["wout"], x, False, "out_proj")
    u, h2 = norm_matmul(x1, lw["nw2"], lw["wup"], "up_proj")
    x2 = matmul_residual(u, lw["wdn"], x1, True, "down_proj")
    return x2, (x, h1, proj, st, y, x1, h2, u)


def _dw(a, b, a_cols, b_cols, relu2, name):
    m_len, n_len = a.shape[1], b.shape[1]
    n_a, n_b = m_len // a_cols, n_len // b_cols
    assert n_a == 1 or n_b == 1
    if n_b == 1:
        return matmul_tn(
            a, b,
            lambda t_: pl.BlockSpec((t_, a_cols), lambda n, t: (t, n)),
            lambda t_: pl.BlockSpec((t_, n_len), lambda n, t: (t, 0)),
            pl.BlockSpec((a_cols, n_len), lambda n, t: (n, 0)), SDS((m_len, n_len), BF16), n_a, relu2, name)
    return matmul_tn(
        a, b,
        lambda t_: pl.BlockSpec((t_, m_len), lambda n, t: (t, 0)),
        lambda t_: pl.BlockSpec((t_, b_cols), lambda n, t: (t, n)),
        pl.BlockSpec((m_len, b_cols), lambda n, t: (0, n)), SDS((m_len, n_len), BF16), n_b, relu2, name)


def layer_bwd(dx2, dx2b, lw, prm, saved, tt):
    x, h1, proj, st, y, x1, h2, u = saved
    du = matmul_nt_act(dx2b, lw["wdn"], u, "mlp_bwd_du")
    g_wdn = _dw(u, dx2b, 1024, D, True, "dw_down")
    dx1, dx1b, g_nw2 = matmul_nt_norm_bwd(du, lw["wup"], x1, lw["nw2"], dx2, "mlp_bwd_dx")
    cb = DFF // N_DEV
    g_wup = matmul_tn(
        h2, du,
        lambda t_: pl.BlockSpec((t_, D), lambda n, t: (t, 0)),
        lambda t_: pl.BlockSpec((t_, 2 * cb), lambda n, t: (t, n)),
        pl.BlockSpec((2, D, cb), lambda n, t: (n, 0, 0)), SDS((N_DEV, D, cb), BF16), N_DEV // 2, False, "dw_up")
    dy = matmul_nt_act(dx1b, lw["wout"], None, "out_bwd_dy")
    g_wout = _dw(y, dx1b, 1024, D, False, "dw_out")
    dproj, gscw, gcwx, gcwbc, gvec, gdt = mixer_bwd(proj, dy, st, prm, tt)
    dx0, dx0b, g_nw1 = matmul_nt_norm_bwd(dproj, lw["win"], x, lw["nw1"], dx1, "in_bwd_dx")
    g_win = _dw(h1, dproj, D, TN_IN, False, "dw_in")
    grads = {
        "win": g_win, "wout": g_wout, "wup": g_wup, "wdn": g_wdn,
        "scw": gscw[0:3], "cw": jnp.concatenate([gcwx[0:4], gcwbc[0:4]], axis=1),
        "cb": jnp.concatenate([gvec[2], gcwbc[4]], axis=0),
        "dtb": gdt[0, :NHEAD], "alog": gdt[1, :NHEAD], "dsk": gdt[2, :NHEAD],
        "nrm": gvec[0], "nw1": g_nw1[0], "nw2": g_nw2[0],
    }
    return dx0, dx0b, grads


def layer_params(i, norm_mix_w, win, wout, wup, wdn, scw, cw, ssd_conv_b, dt_bias, a_log, d_skip, ssd_norm_w,
                 norm_mlp_w, eh, eht):
    def rows8(a):
        return jnp.pad(a, ((0, 8 - a.shape[0]), (0, 0)))

    def lanes128(a):
        return jnp.pad(a, (0, LANES - a.shape[0]))[None, :]

    lw = {"win": win[i], "wout": wout[i], "wup": wup[i], "wdn": wdn[i],
          "nw1": norm_mix_w[i][None, :], "nw2": norm_mlp_w[i][None, :]}
    prm = {"scw": rows8(scw[i]), "cwx": rows8(cw[i][:, :D]), "cwbc": rows8(cw[i][:, D:]),
           "cbx": ssd_conv_b[i][None, :D], "cbbc": ssd_conv_b[i][None, D:],
           "dtb": lanes128(dt_bias[i]), "alog": lanes128(a_log[i]),
           "dskx": jnp.repeat(d_skip[i], HDIM)[None, :], "nrm": ssd_norm_w[i][None, :], "eh": eh, "eht": eht}
    return lw, prm


def _flip(v, bit):
    return 1 - v if bit else v


def all_gather(arrs, name):
    n = len(arrs)

    def body(*refs):
        ins, outs = refs[:n], refs[n:2 * n]
        send_sems, recv_sems, local_sems = refs[2 * n:]
        x, y, c = lax.axis_index("x"), lax.axis_index("y"), lax.axis_index("c")
        sibling = (x, y, 1 - c)
        chips = [(1 - x, y), (x, 1 - y), (1 - x, 1 - y)]

        def idx(px, py, pc):
            return 4 * px + 2 * py + pc

        def copy(a, k, block, to, src=None):
            dst = outs[a].at[idx(*block)]
            return pltpu.make_async_remote_copy(
                src_ref=dst if src is None else src, dst_ref=dst,
                send_sem=send_sems.at[a, k], recv_sem=recv_sems.at[a, k], device_id=to, device_id_type=MESH)

        me = (x, y, c)
        mine = [pltpu.make_async_copy(ins[a], outs[a].at[idx(*me)], local_sems.at[a]) for a in range(n)]
        for cp in mine:
            cp.start()
        first = []
        for a in range(n):
            first.append(copy(a, 0, me, sibling, src=ins[a]))
            first += [copy(a, 1 + j, me, (*chip, c), src=ins[a]) for j, chip in enumerate(chips)]
        for cp in first:
            cp.start()
        passed = []
        for j, chip in enumerate(chips):
            for a in range(n):
                copy(a, 1 + j, (*chip, c), me).wait_recv()
                cp = copy(a, 4 + j, (*chip, c), sibling)
                cp.start()
                passed.append(cp)
        for a in range(n):
            copy(a, 0, sibling, me).wait_recv()
            for j, chip in enumerate(chips):
                copy(a, 4 + j, (*chip, 1 - c), me).wait_recv()
        for cp in first + passed:
            cp.wait_send()
        for cp in mine:
            cp.wait()

    any_spec = pl.BlockSpec(memory_space=pl.ANY)
    return pl.pallas_call(
        body, in_specs=[any_spec] * n, out_specs=[any_spec] * n,
        out_shape=[SDS((N_DEV,) + a.shape, a.dtype) for a in arrs],
        scratch_shapes=[pltpu.SemaphoreType.DMA((n, 7)), pltpu.SemaphoreType.DMA((n, 7)),
                        pltpu.SemaphoreType.DMA((n,))],
        name=name)(*arrs)


HBM_SPEC = pl.BlockSpec(memory_space=pltpu.HBM)
SEM_SPEC = pl.BlockSpec(memory_space=pltpu.SEMAPHORE)
SIDE_EFFECT = pltpu.SideEffectType.DATAFLOW_SIDE_EFFECTING
N_PEER = N_DEV - 1


def _peer(mask):
    x, y, c = lax.axis_index("x"), lax.axis_index("y"), lax.axis_index("c")
    return _flip(x, mask & 4), _flip(y, mask & 2), _flip(c, mask & 1)


def exchange_start(srcs, per_peer, name):
    n = len(srcs)
    lands = [SDS((N_DEV,) + (a.shape[1:] if per_peer else a.shape), a.dtype) for a in srcs]

    def body(*refs):
        src_refs, land_refs = refs[:n], refs[n:2 * n]
        send_sems, recv_sems = refs[2 * n], refs[2 * n + 1]
        token = refs[-1]
        x, y, c = lax.axis_index("x"), lax.axis_index("y"), lax.axis_index("c")
        me = 4 * x + 2 * y + c
        for mask in range(1, N_DEV):
            px, py, pc = _peer(mask)
            for a in range(n):
                part = src_refs[a].at[4 * px + 2 * py + pc] if per_peer else src_refs[a]
                pltpu.make_async_remote_copy(
                    src_ref=part, dst_ref=land_refs[a].at[me], send_sem=send_sems.at[a * N_PEER + mask - 1],
                    recv_sem=recv_sems.at[a * N_PEER + mask - 1], device_id=(px, py, pc),
                    device_id_type=MESH).start()
        token[...] = jnp.zeros_like(token)

    out = pl.pallas_call(
        body, name=name,
        out_shape=(pltpu.SemaphoreType.DMA((n * N_PEER,)), pltpu.SemaphoreType.DMA((n * N_PEER,)),
                   *[pltpu.HBM(a.shape, a.dtype) for a in srcs], *[pltpu.HBM(l.shape, l.dtype) for l in lands],
                   SDS((8, LANES), F32)),
        in_specs=(HBM_SPEC,) * (2 * n),
        out_specs=(SEM_SPEC, SEM_SPEC) + (HBM_SPEC,) * (2 * n) + (pl.BlockSpec(memory_space=pltpu.VMEM),),
        input_output_aliases={k: 2 + k for k in range(2 * n)},
        compiler_params=pltpu.CompilerParams(has_side_effects=SIDE_EFFECT),
    )(*[pltpu.with_memory_space_constraint(a, pltpu.HBM) for a in srcs],
      *[pltpu.with_memory_space_constraint(lax.empty(l.shape, l.dtype), pltpu.HBM) for l in lands])
    return out[0], out[1], list(out[2:2 + n]), list(out[2 + n:2 + 2 * n]), out[-1]


def exchange_wait(started, after, per_peer, name):
    send_sems, recv_sems, srcs, lands, _ = started
    n = len(srcs)

    def body(*refs):
        src_refs, land_refs = refs[:n], refs[n:2 * n]
        send_sems, recv_sems = refs[2 * n], refs[2 * n + 1]
        for mask in range(1, N_DEV):
            for a in range(n):
                copy = pltpu.make_async_remote_copy(
                    src_ref=src_refs[a].at[0] if per_peer else src_refs[a], dst_ref=land_refs[a].at[0],
                    send_sem=send_sems.at[a * N_PEER + mask - 1], recv_sem=recv_sems.at[a * N_PEER + mask - 1],
                    device_id=_peer(mask), device_id_type=MESH)
                copy.wait_send()
                copy.wait_recv()

    out = pl.pallas_call(
        body, name=name,
        out_shape=tuple(pltpu.HBM(a.shape, a.dtype) for a in srcs + lands),
        in_specs=(HBM_SPEC,) * (2 * n) + (SEM_SPEC, SEM_SPEC, pl.BlockSpec(memory_space=pl.ANY)),
        out_specs=(HBM_SPEC,) * (2 * n), input_output_aliases={k: k for k in range(2 * n)},
        compiler_params=pltpu.CompilerParams(has_side_effects=SIDE_EFFECT),
    )(*srcs, *lands, send_sems, recv_sems, after)
    return list(out[:n]), list(out[n:])


IN_SHARD = NIN // N_DEV
SLOT_W = 768


def _slot_window(j):
    return (IN_SHARD * j // LANES) * LANES, -(-(IN_SHARD * (j + 1)) // LANES) * LANES


def _placement(j):
    a, b = _slot_window(j)
    r = lax.broadcasted_iota(jnp.int32, (SLOT_W, b - a), 0)
    c = lax.broadcasted_iota(jnp.int32, (SLOT_W, b - a), 1)
    return jnp.where(jnp.logical_and(c == r + (IN_SHARD * j - a), r < IN_SHARD), 1.0, 0.0).astype(BF16)


def assemble_w_in(land):
    tm = 256

    def body(l_ref, o_ref, acc):
        acc[...] = jnp.zeros_like(acc)
        for j in range(N_DEV):
            a, b = _slot_window(j)
            acc[:, a:b] += _nn(l_ref[j], _placement(j))
        o_ref[...] = acc[...].astype(BF16)

    return pl.pallas_call(
        body, grid=(D // tm,),
        in_specs=[pl.BlockSpec((N_DEV, tm, SLOT_W), lambda i: (0, i, 0))],
        out_specs=pl.BlockSpec((tm, NINP), lambda i: (i, 0)),
        out_shape=SDS((D, NINP), BF16),
        scratch_shapes=[pltpu.VMEM((tm, NINP), F32)],
        compiler_params=_cparams(("parallel",)), name="assemble_w_in")(land)


def scatter_w_in(dw):
    tm = 256

    def body(d_ref, o_ref):
        for j in range(N_DEV):
            a, b = _slot_window(j)
            o_ref[j] = _nt(d_ref[:, a:b], _placement(j)).astype(BF16)

    return pl.pallas_call(
        body, grid=(D // tm,),
        in_specs=[pl.BlockSpec((tm, NINP), lambda i: (i, 0))],
        out_specs=pl.BlockSpec((N_DEV, tm, SLOT_W), lambda i: (0, i, 0)),
        out_shape=SDS((N_DEV, D, SLOT_W), BF16),
        compiler_params=_cparams(("parallel",)), name="scatter_w_in")(dw)


def _adamw_math(g, w_ref, m_ref, v_ref, g_ref, d_ref, nm_ref, nv_ref):
    mn = ADAM_B1 * m_ref[...] + (1.0 - ADAM_B1) * g
    vn = ADAM_B2 * v_ref[...] + (1.0 - ADAM_B2) * jnp.square(g)
    m_hat = mn / (1.0 - ADAM_B1 ** ADAM_STEP)
    v_hat = vn / (1.0 - ADAM_B2 ** ADAM_STEP)
    g_ref[...] = g
    d_ref[...] = -ADAM_LR * (m_hat / (jnp.sqrt(v_hat) + ADAM_EPS) + ADAM_WD * w_ref[...])
    nm_ref[...] = mn
    nv_ref[...] = vn


def adamw_layers(w, slots, m, v, name):
    depth, r_len, c_len = w.shape
    cs = slots[0].shape[2]
    br = min(128, r_len)
    assert r_len % br == 0

    def body(w_ref, *rest):
        s_refs, (m_ref, v_ref, g_ref, d_ref, nm_ref, nv_ref) = rest[:depth], rest[depth:]
        layer = pl.program_id(0)
        for k in range(depth):
            @pl.when(layer == k)
            def _(k=k):
                g = s_refs[k][0, :, 0:c_len].astype(F32)
                for j in range(1, N_DEV):
                    g = g + s_refs[k][j, :, 0:c_len].astype(F32)
                _adamw_math(g, w_ref, m_ref, v_ref, g_ref, d_ref, nm_ref, nv_ref)

    spec = pl.BlockSpec((None, br, c_len), lambda l, i: (l, i, 0))
    s_specs = [pl.BlockSpec((N_DEV, br, cs), lambda l, i, k=k: (0, jnp.where(l == k, i, 0), 0))
               for k in range(depth)]
    return pl.pallas_call(
        body, grid=(depth, r_len // br),
        in_specs=[spec] + s_specs + [spec, spec],
        out_specs=[spec] * 4, out_shape=[SDS(w.shape, F32)] * 4,
        compiler_params=_cparams(("arbitrary", "arbitrary")), name=name)(w, *slots, m, v)


def adamw(w, slots, m, v, name):
    r_len, c_len = w.shape
    br = r_len if r_len <= 512 else 512
    assert r_len % br == 0

    def body(w_ref, s_ref, m_ref, v_ref, g_ref, d_ref, nm_ref, nv_ref):
        g = s_ref[0].astype(F32)
        for k in range(1, N_DEV):
            g = g + s_ref[k].astype(F32)
        _adamw_math(g, w_ref, m_ref, v_ref, g_ref, d_ref, nm_ref, nv_ref)

    spec = pl.BlockSpec((br, c_len), lambda i: (i, 0))
    return pl.pallas_call(
        body, grid=(r_len // br,),
        in_specs=[spec, pl.BlockSpec((N_DEV, br, c_len), lambda i: (0, i, 0)), spec, spec],
        out_specs=[spec] * 4, out_shape=[SDS((r_len, c_len), F32)] * 4,
        compiler_params=_cparams(("parallel",)), name=name)(w, slots, m, v)


def _adamw_nd(w, slots, m, v, name):
    shp = w.shape
    r = int(np.prod(shp[:-1]))
    outs = adamw(w.reshape(r, shp[-1]), slots.reshape(N_DEV, r, shp[-1]), m.reshape(r, shp[-1]),
                 v.reshape(r, shp[-1]), name)
    return [o.reshape(shp) for o in outs]


SMALL = [("norm_mix_w", DEPTH * D), ("ssd_conv_b", DEPTH * XBC), ("dt_bias", DEPTH * NHEAD),
         ("a_log", DEPTH * NHEAD), ("d_skip", DEPTH * NHEAD), ("ssd_norm_w", DEPTH * D),
         ("norm_mlp_w", DEPTH * D), ("final_norm_w", D)]
SMALL_LEN = sum(s for _, s in SMALL)
SMALL_ROWS = -(-SMALL_LEN // LANES)


def _pack_small(parts):
    flat = jnp.concatenate([parts[k].reshape(-1) for k, _ in SMALL])
    return jnp.pad(flat, (0, SMALL_ROWS * LANES - SMALL_LEN)).reshape(SMALL_ROWS, LANES)


def _unpack_small(packed, shapes):
    flat = packed.reshape(-1)
    out, off = {}, 0
    for k, s in SMALL:
        out[k] = flat[off:off + s].reshape(shapes[k])
        off += s
    return out


def kernel(x, norm_mix_w, w_in, short_conv_w, ssd_conv_w, ssd_conv_b, dt_bias, a_log, d_skip, ssd_norm_w, w_out, norm_mlp_w, w_up, w_down, final_norm_w, loss_target, m_norm_mix_w, m_w_in, m_short_conv_w, m_ssd_conv_w, m_ssd_conv_b, m_dt_bias, m_a_log, m_d_skip, m_ssd_norm_w, m_w_out, m_norm_mlp_w, m_w_up, m_w_down, m_final_norm_w, v_norm_mix_w, v_w_in, v_short_conv_w, v_ssd_conv_w, v_ssd_conv_b, v_dt_bias, v_a_log, v_d_skip, v_ssd_norm_w, v_w_out, v_norm_mlp_w, v_w_up, v_w_down, v_final_norm_w):
    xs = x[0]
    t_len = xs.shape[0]
    tt = min(256, t_len)
    eh, eht = _head_matrices()
    me = 4 * lax.axis_index("x") + 2 * lax.axis_index("y") + lax.axis_index("c")

    g_sc, g_cw = all_gather([short_conv_w, ssd_conv_w], "gather_conv_weights")
    scw = g_sc.transpose(1, 2, 0, 3).reshape(DEPTH, 3, D)
    cw = g_cw.transpose(1, 2, 0, 3).reshape(DEPTH, 4, XBC)
    in_flight = [exchange_start(
        [jnp.pad(w_in[i].astype(BF16), ((0, 0), (0, SLOT_W - IN_SHARD))), w_out[i].astype(BF16),
         w_up[i].astype(BF16), w_down[i].astype(BF16)], False, "weights_start_%d" % i) for i in range(DEPTH)]
    tokens = sum(f[4][0, 0] for f in in_flight)

    def fill_own(srcs, lands, per_peer):
        own = [lax.dynamic_index_in_dim(s_, me, 0, keepdims=False) for s_ in srcs] if per_peer else srcs
        return [lax.dynamic_update_index_in_dim(l_, o_, me, 0) for l_, o_ in zip(lands, own)]

    act = xs
    saved, layers = [], []
    for i in range(DEPTH):
        after = norm_mix_w[i] + tokens if i == 0 else act
        g_in, g_out, g_up, g_dn = fill_own(*exchange_wait(in_flight[i], after, False, "weights_wait_%d" % i), False)
        lw, prm = layer_params(
            i, norm_mix_w, [assemble_w_in(g_in)] * DEPTH, [g_out.reshape(MIX, D)] * DEPTH, [g_up] * DEPTH,
            [g_dn.reshape(DFF, D)] * DEPTH, scw, cw, ssd_conv_b, dt_bias, a_log, d_skip, ssd_norm_w, norm_mlp_w,
            eh, eht)
        if i == 0:
            lw["nw1"] = after[None, :]
        layers.append((lw, prm))
        act, sv = layer_fwd(act, lw, prm, tt)
        saved.append(sv)
    loss_acc, dx, dxb, g_fw = loss_head(act, final_norm_w[None, :], loss_target[0])

    grads = [None] * DEPTH
    sent = [None] * DEPTH
    token = None
    for i in reversed(range(DEPTH)):
        lw, prm = layers[i]
        if token is not None:
            lw = dict(lw, nw2=lw["nw2"] + token)
        dx, dxb, g = layer_bwd(dx, dxb, lw, prm, saved[i], tt)
        grads[i] = g
        sent[i] = exchange_start(
            [scatter_w_in(g["win"]), g["wout"].reshape(N_DEV, MIX // N_DEV, D), g["wup"],
             g["wdn"].reshape(N_DEV, DFF // N_DEV, D)], True, "grads_start_%d" % i)
        token = sent[i][4][0, 0]

    def stack(k):
        return jnp.stack([g[k] for g in grads])

    small = _pack_small({"norm_mix_w": stack("nw1"), "ssd_conv_b": stack("cb"), "dt_bias": stack("dtb"),
                         "a_log": stack("alog"), "d_skip": stack("dsk"), "ssd_norm_w": stack("nrm"),
                         "norm_mlp_w": stack("nw2"), "final_norm_w": g_fw[0]})
    r_small, r_sc, r_cw = all_gather([small + token, stack("scw"), stack("cw")], "gather_small_grads")
    r_sc = lax.dynamic_slice_in_dim(r_sc, me * (D // N_DEV), D // N_DEV, axis=3)
    r_cw = lax.dynamic_slice_in_dim(r_cw, me * (XBC // N_DEV), XBC // N_DEV, axis=3)
    recv = [fill_own(*exchange_wait(sent[i], r_small, True, "grads_wait_%d" % i), True) for i in range(DEPTH)]

    res = {}
    res["w_in"] = adamw_layers(w_in, [r[0] for r in recv], m_w_in, v_w_in, "adamw_w_in")
    res["w_out"] = adamw_layers(w_out, [r[1] for r in recv], m_w_out, v_w_out, "adamw_w_out")
    res["w_up"] = adamw_layers(w_up, [r[2] for r in recv], m_w_up, v_w_up, "adamw_w_up")
    res["w_down"] = adamw_layers(w_down, [r[3] for r in recv], m_w_down, v_w_down, "adamw_w_down")
    res["short_conv_w"] = _adamw_nd(short_conv_w, r_sc, m_short_conv_w, v_short_conv_w, "adamw_short_conv")
    res["ssd_conv_w"] = _adamw_nd(ssd_conv_w, r_cw, m_ssd_conv_w, v_ssd_conv_w, "adamw_ssd_conv")
    small_w = {"norm_mix_w": norm_mix_w, "ssd_conv_b": ssd_conv_b, "dt_bias": dt_bias, "a_log": a_log,
               "d_skip": d_skip, "ssd_norm_w": ssd_norm_w, "norm_mlp_w": norm_mlp_w, "final_norm_w": final_norm_w}
    small_m = {"norm_mix_w": m_norm_mix_w, "ssd_conv_b": m_ssd_conv_b, "dt_bias": m_dt_bias, "a_log": m_a_log,
               "d_skip": m_d_skip, "ssd_norm_w": m_ssd_norm_w, "norm_mlp_w": m_norm_mlp_w,
               "final_norm_w": m_final_norm_w}
    small_v = {"norm_mix_w": v_norm_mix_w, "ssd_conv_b": v_ssd_conv_b, "dt_bias": v_dt_bias, "a_log": v_a_log,
               "d_skip": v_d_skip, "ssd_norm_w": v_ssd_norm_w, "norm_mlp_w": v_norm_mlp_w,
               "final_norm_w": v_final_norm_w}
    shapes = {k: a.shape for k, a in small_w.items()}
    packed = adamw(_pack_small(small_w), r_small, _pack_small(small_m), _pack_small(small_v), "adamw_small")
    unpacked = [_unpack_small(p, shapes) for p in packed]
    for k in small_w:
        res[k] = [u[k] for u in unpacked]

    loss = lax.psum(loss_acc[0, 0], ("x", "y", "c"))
    order = ["norm_mix_w", "w_in", "short_conv_w", "ssd_conv_w", "ssd_conv_b", "dt_bias", "a_log", "d_skip",
             "ssd_norm_w", "w_out", "norm_mlp_w", "w_up", "w_down", "final_norm_w"]
    out = [loss, dx[None]]
    for part in range(4):
        out += [res[k][part] for k in order]
    return tuple(out)
```

```python
import functools

import numpy as np
import jax
import jax.numpy as jnp
from jax import lax
from jax.experimental import pallas as pl
from jax.experimental.pallas import tpu as pltpu

F32 = jnp.float32
BF16 = jnp.bfloat16
SDS = jax.ShapeDtypeStruct

N_DEV = 8
DEPTH = 4
D = 1024
NIN = 5648
NINP = 5760
DFF = 4096
MIX = 2048
NHEAD = 16
HDIM = 64
NSTATE = 128
CHUNK = 64
XBC = 1536
EPS = 1e-5
LANES = 128

C_UB, C_UC, C_UH, C_Z, C_XS, C_BC, C_DT = 0, 1024, 2048, 3072, 4096, 5120, 5632

ADAM_LR = 0.001
ADAM_B1 = 0.9
ADAM_B2 = 0.999
ADAM_EPS = 1e-08
ADAM_WD = 0.01
ADAM_STEP = 10

VMEM_LIMIT = 56 * 1024 * 1024
MESH = pl.DeviceIdType.MESH


def _cparams(sem):
    return pltpu.CompilerParams(dimension_semantics=sem, vmem_limit_bytes=VMEM_LIMIT)


def _nt(a, b):
    return lax.dot_general(a, b, (((1,), (1,)), ((), ())), preferred_element_type=F32)


def _tn(a, b):
    return lax.dot_general(a, b, (((0,), (0,)), ((), ())), preferred_element_type=F32)


def _nn(a, b):
    return jnp.dot(a, b, preferred_element_type=F32)


def _sigmoid(v):
    return 1.0 / (1.0 + jnp.exp(-v))


def _split3(v):
    v1 = v.astype(BF16)
    r1 = v - v1.astype(F32)
    v2 = r1.astype(BF16)
    v3 = (r1 - v2.astype(F32)).astype(BF16)
    return v1, v2, v3


def _expand(v, eh):
    v1, v2, v3 = _split3(v)
    return _nn(v1, eh) + _nn(v2, eh) + _nn(v3, eh)


def _head_reduce(v, eht):
    v1, v2, v3 = _split3(v)
    return _nn(v1, eht) + _nn(v2, eht) + _nn(v3, eht)


def _head_matrices():
    eh = np.zeros((LANES, D), np.float32)
    for h in range(NHEAD):
        eh[h, h * HDIM:(h + 1) * HDIM] = 1.0
    return jnp.asarray(eh, BF16), jnp.asarray(eh.T.copy(), BF16)


def _resident(shape):
    return pl.BlockSpec(shape, lambda *_: (0,) * len(shape), pipeline_mode=pl.Buffered(1))


def _col_chunks(n, step):
    return [(c, min(c + step, n)) for c in range(0, n, step)]


def norm_matmul(x, nw, w, name):
    t_len = x.shape[0]
    blocked = w.ndim == 3
    n_len = w.shape[0] * w.shape[2] if blocked else w.shape[1]
    tm = min(512, t_len)
    chunks = _col_chunks(n_len, n_len // N_DEV if blocked else 1536)

    def body(x_ref, nw_ref, w_ref, o_ref, h_ref):
        xv = x_ref[...]
        r = lax.rsqrt(jnp.mean(xv * xv, axis=-1, keepdims=True) + EPS)
        hv = (xv * r * nw_ref[...]).astype(BF16)
        h_ref[...] = hv
        for j, (c0, c1) in enumerate(chunks):
            wj = w_ref[j] if blocked else w_ref[:, c0:c1]
            o_ref[:, c0:c1] = _nn(hv, wj).astype(o_ref.dtype)

    return pl.pallas_call(
        body, grid=(t_len // tm,),
        in_specs=[pl.BlockSpec((tm, D), lambda i: (i, 0)), _resident((1, D)), _resident(w.shape)],
        out_specs=[pl.BlockSpec((tm, n_len), lambda i: (i, 0)),
                   pl.BlockSpec((tm, D), lambda i: (i, 0))],
        out_shape=[SDS((t_len, n_len), BF16), SDS((t_len, D), BF16)],
        compiler_params=_cparams(("parallel",)), name=name)(x, nw, w)


def matmul_residual(a, w, res, relu2, name):
    t_len, k_len = a.shape
    tm = min(512, t_len)

    def body(a_ref, w_ref, res_ref, o_ref):
        av = a_ref[...]
        if relu2:
            af = jnp.maximum(av.astype(F32), 0.0)
            av = (af * af).astype(BF16)
        o_ref[...] = res_ref[...] + _nn(av, w_ref[...])

    return pl.pallas_call(
        body, grid=(t_len // tm,),
        in_specs=[pl.BlockSpec((tm, k_len), lambda i: (i, 0)),
                  _resident((k_len, D)),
                  pl.BlockSpec((tm, D), lambda i: (i, 0))],
        out_specs=pl.BlockSpec((tm, D), lambda i: (i, 0)),
        out_shape=SDS((t_len, D), F32),
        compiler_params=_cparams(("parallel",)), name=name)(a, w, res)


def matmul_nt_act(dy, w, u, name):
    t_len = dy.shape[0]
    n_len = w.shape[0]
    tm = min(512, t_len)
    chunks = _col_chunks(n_len, 1024)

    def body(dy_ref, w_ref, *rest):
        if u is None:
            (o_ref,) = rest
        else:
            u_ref, o_ref = rest
        dyv = dy_ref[...]
        for c0, c1 in chunks:
            p = _nt(dyv, w_ref[c0:c1, :])
            if u is not None:
                p = p * (2.0 * jnp.maximum(u_ref[:, c0:c1].astype(F32), 0.0))
            o_ref[:, c0:c1] = p.astype(o_ref.dtype)

    in_specs = [pl.BlockSpec((tm, D), lambda i: (i, 0)), _resident((n_len, D))]
    args = [dy, w]
    if u is not None:
        in_specs.append(pl.BlockSpec((tm, n_len), lambda i: (i, 0)))
        args.append(u)
    return pl.pallas_call(
        body, grid=(t_len // tm,),
        in_specs=in_specs,
        out_specs=pl.BlockSpec((tm, n_len), lambda i: (i, 0)),
        out_shape=SDS((t_len, n_len), BF16),
        compiler_params=_cparams(("parallel",)), name=name)(*args)


def matmul_tn(a, b, a_spec, b_spec, o_spec, o_shape, n_out, relu2, name):
    t_len = a.shape[0]
    tt = min(2048, t_len)
    nt = t_len // tt

    def body(a_ref, b_ref, o_ref, acc):
        t = pl.program_id(1)
        av = a_ref[...]
        if relu2:
            af = jnp.maximum(av.astype(F32), 0.0)
            av = (af * af).astype(BF16)
        p = _tn(av, b_ref[...])

        @pl.when(t == 0)
        def _():
            acc[...] = p

        @pl.when(t > 0)
        def _():
            acc[...] += p

        @pl.when(t == nt - 1)
        def _():
            if len(blk) == 3:
                for j in range(blk[0]):
                    o_ref[j] = acc[:, j * blk[2]:(j + 1) * blk[2]].astype(o_ref.dtype)
            else:
                o_ref[...] = acc[...].astype(o_ref.dtype)

    blk = tuple(o_spec.block_shape)
    acc_shape = (blk[1], blk[0] * blk[2]) if len(blk) == 3 else blk
    return pl.pallas_call(
        body, grid=(n_out, nt),
        in_specs=[a_spec(tt), b_spec(tt)],
        out_specs=o_spec, out_shape=o_shape,
        scratch_shapes=[pltpu.VMEM(acc_shape, F32)],
        compiler_params=_cparams(("parallel", "arbitrary")), name=name)(a, b)


def matmul_nt_norm_bwd(dy, w, x, nw, dres, name):
    t_len = x.shape[0]
    blocked = w.ndim == 3
    k_len = dy.shape[1]
    kb = k_len // N_DEV
    tm = min(512, t_len)

    def body(dy_ref, w_ref, x_ref, nw_ref, dres_ref, dx_ref, dxb_ref, dnw_ref):
        @pl.when(pl.program_id(0) == 0)
        def _():
            dnw_ref[...] = jnp.zeros_like(dnw_ref)

        if blocked:
            dh = _nt(dy_ref[:, 0:kb], w_ref[0])
            for j in range(1, N_DEV):
                dh = dh + _nt(dy_ref[:, j * kb:(j + 1) * kb], w_ref[j])
        else:
            dh = _nt(dy_ref[...], w_ref[...])
        xv = x_ref[...]
        r = lax.rsqrt(jnp.mean(xv * xv, axis=-1, keepdims=True) + EPS)
        xh = xv * r
        dnw_ref[0:1, :] += jnp.sum(dh * xh, axis=0, keepdims=True)
        g = dh * nw_ref[...]
        dx = dres_ref[...] + r * (g - xh * jnp.mean(g * xh, axis=-1, keepdims=True))
        dx_ref[...] = dx
        dxb_ref[...] = dx.astype(BF16)

    return pl.pallas_call(
        body, grid=(t_len // tm,),
        in_specs=[pl.BlockSpec((tm, k_len), lambda i: (i, 0)),
                  _resident(w.shape),
                  pl.BlockSpec((tm, D), lambda i: (i, 0)),
                  _resident((1, D)),
                  pl.BlockSpec((tm, D), lambda i: (i, 0))],
        out_specs=[pl.BlockSpec((tm, D), lambda i: (i, 0)),
                   pl.BlockSpec((tm, D), lambda i: (i, 0)),
                   pl.BlockSpec((8, D), lambda i: (0, 0))],
        out_shape=[SDS((t_len, D), F32), SDS((t_len, D), BF16), SDS((8, D), F32)],
        compiler_params=_cparams(("arbitrary",)), name=name)(dy, w, x, nw, dres)


def loss_head(x, fw, tgt):
    t_len = x.shape[0]
    tm = min(512, t_len)

    def body(x_ref, fw_ref, t_ref, loss_ref, dx_ref, dxb_ref, dfw_ref):
        @pl.when(pl.program_id(0) == 0)
        def _():
            loss_ref[...] = jnp.zeros_like(loss_ref)
            dfw_ref[...] = jnp.zeros_like(dfw_ref)
        xv = x_ref[...]
        r = lax.rsqrt(jnp.mean(xv * xv, axis=-1, keepdims=True) + EPS)
        xh = xv * r
        w = fw_ref[...]
        e = xh * w - t_ref[...]
        row = jnp.sum(e * e, axis=-1, keepdims=True) * (1.0 / D)
        loss_ref[...] += 0.5 * jnp.sum(row, axis=0, keepdims=True)
        dyf = e * (1.0 / D)
        dfw_ref[0:1, :] += jnp.sum(dyf * xh, axis=0, keepdims=True)
        g = dyf * w
        dx = r * (g - xh * jnp.mean(g * xh, axis=-1, keepdims=True))
        dx_ref[...] = dx
        dxb_ref[...] = dx.astype(BF16)

    return pl.pallas_call(
        body, grid=(t_len // tm,),
        in_specs=[pl.BlockSpec((tm, D), lambda i: (i, 0)),
                  pl.BlockSpec((1, D), lambda i: (0, 0)),
                  pl.BlockSpec((tm, D), lambda i: (i, 0))],
        out_specs=[pl.BlockSpec((8, LANES), lambda i: (0, 0)),
                   pl.BlockSpec((tm, D), lambda i: (i, 0)),
                   pl.BlockSpec((tm, D), lambda i: (i, 0)),
                   pl.BlockSpec((8, D), lambda i: (0, 0))],
        out_shape=[SDS((8, LANES), F32), SDS((t_len, D), F32), SDS((t_len, D), BF16), SDS((8, D), F32)],
        compiler_params=_cparams(("arbitrary",)), name="loss_head")(x, fw, tgt)


def _shift_dn(x, halo, j):
    if j == 0:
        return x
    xr = pltpu.roll(x, j, 0)
    hr = pltpu.roll(halo, j, 0)
    row = lax.broadcasted_iota(jnp.int32, hr.shape, 0)
    top = jnp.where(row < j, hr, xr[0:8])
    return jnp.concatenate([top, xr[8:]], axis=0)


def _shift_up(x, nxt, j):
    if j == 0:
        return x
    n = x.shape[0]
    xr = pltpu.roll(x, n - j, 0)
    hr = pltpu.roll(nxt, 8 - j, 0)
    row = lax.broadcasted_iota(jnp.int32, hr.shape, 0)
    bot = jnp.where(row >= 8 - j, hr, xr[n - 8:n])
    return jnp.concatenate([xr[:n - 8], bot], axis=0)


def _conv_fwd(x, halo, w_ref, kw):
    acc = None
    for k in range(kw):
        term = w_ref[k:k + 1, :] * _shift_dn(x, halo, kw - 1 - k)
        acc = term if acc is None else acc + term
    return acc


def _chunk_cumsum(a, pos):
    for sh in (1, 2, 4, 8, 16, 32):
        a = a + jnp.where(pos >= sh, pltpu.roll(a, sh, 0), 0.0)
    return a


def _chunk_rcumsum(a, pos):
    n = a.shape[0]
    for sh in (1, 2, 4, 8, 16, 32):
        a = a + jnp.where(pos < CHUNK - sh, pltpu.roll(a, n - sh, 0), 0.0)
    return a


def _softplus(v):
    return jnp.maximum(v, 0.0) + jnp.log(1.0 + jnp.exp(-jnp.abs(v)))


def _silu(v):
    return v * _sigmoid(v)


def _dsilu(v):
    s = _sigmoid(v)
    return s * (1.0 + v * (1.0 - s))


def _lane_masks(width=D):
    lane = lax.broadcasted_iota(jnp.int32, (CHUNK, width), 1) & (HDIM - 1)
    row = lax.broadcasted_iota(jnp.int32, (CHUNK, width), 0)
    return lane == row, lane <= row


def _rep_matrix():
    lane = lax.broadcasted_iota(jnp.int32, (CHUNK, 512), 1) & (HDIM - 1)
    row = lax.broadcasted_iota(jnp.int32, (CHUNK, 512), 0)
    return jnp.where(lane == row, 1.0, 0.0).astype(BF16)


def _blockdiag(xp):
    lane = lax.broadcasted_iota(jnp.int32, xp.shape, 1)
    zero = jnp.zeros_like(xp)
    return jnp.concatenate([jnp.where(lane < HDIM, xp, zero), jnp.where(lane >= HDIM, xp, zero)], axis=0)


def _mixer_views(tt):
    r8 = tt // 8

    def main(width, col):
        return pl.BlockSpec((tt, width), lambda i, c=col // width: (i, c))

    def halo(width, col):
        return pl.BlockSpec((8, width), lambda i, c=col // width: (jnp.maximum(i * r8 - 1, 0), c))

    return main, halo


def mixer_fwd(proj, prm, tt):
    t_len = proj.shape[0]
    nblk = t_len // tt
    nc = tt // CHUNK
    main, halo = _mixer_views(tt)

    def body(ub_ref, uc_ref, uh_ref, z_ref, xr_ref, bcr_ref, dtr_ref, uch_ref, uhh_ref, xrh_ref, bcrh_ref,
             scw_ref, cwx_ref, cwbc_ref, cbx_ref, cbbc_ref, dtb_ref, alog_ref, dsk_ref, nrm_ref, eh_ref,
             y_ref, st_ref, hs, xs_s, bc_s, dtx_s, cumx_s, yssd_s):
        i = pl.program_id(0)
        first = i == 0

        @pl.when(first)
        def _():
            hs[...] = jnp.zeros_like(hs)

        keep = jnp.where(first, 0.0, 1.0)
        v = uc_ref[...].astype(F32) * uh_ref[...].astype(F32)
        vh = uch_ref[...].astype(F32) * uhh_ref[...].astype(F32) * keep
        y_ref[:, 0:D] = (ub_ref[...].astype(F32) * _conv_fwd(v, vh, scw_ref, 3)).astype(BF16)

        xs = _silu(_conv_fwd(xr_ref[...].astype(F32), xrh_ref[...].astype(F32) * keep, cwx_ref, 4) + cbx_ref[...])
        xs_s[...] = xs
        bc_s[...] = _silu(_conv_fwd(bcr_ref[...].astype(F32), bcrh_ref[...].astype(F32) * keep, cwbc_ref, 4)
                          + cbbc_ref[...])
        dt = _softplus(dtr_ref[...].astype(F32) + dtb_ref[...])
        a_neg = -jnp.exp(alog_ref[...])
        pos = lax.broadcasted_iota(jnp.int32, (tt, LANES), 0) & (CHUNK - 1)
        cum = _chunk_cumsum(dt * a_neg, pos)
        eh = eh_ref[...]
        dtx_s[...] = _expand(dt, eh)
        cumx_s[...] = _expand(cum, eh)
        irep, causal = _lane_masks()
        rep = _rep_matrix()

        def chunk(c, carry):
            r0 = pl.multiple_of(c * CHUNK, CHUNK)
            rows = pl.ds(r0, CHUNK)
            cumx = cumx_s[rows, :]
            cum_l = cumx[CHUNK - 1:CHUNK, :]
            xd = xs_s[rows, :] * dtx_s[rows, :]
            xf = xd * jnp.exp(cum_l - cumx)
            ex = jnp.exp(cumx)
            e_l = jnp.exp(cum_l)
            rvec = jnp.sum(jnp.where(irep, cumx, 0.0), axis=0, keepdims=True)
            lam = jnp.where(causal, jnp.exp(jnp.where(causal, cumx - rvec, 0.0)), 0.0)
            bc = bc_s[rows, :]
            for g in range(2):
                gs = slice(g * 512, (g + 1) * 512)
                bg = bc[:, g * NSTATE:(g + 1) * NSTATE].astype(BF16)
                cg = bc[:, 256 + g * NSTATE:256 + (g + 1) * NSTATE].astype(BF16)
                s_rep = _nn(_nt(cg, bg).astype(BF16), rep)
                m_g = (s_rep * lam[:, gs]).astype(BF16)
                h_g = hs[:, gs]
                h_b = h_g.astype(BF16)
                st_ref[c, :, gs] = h_b
                yo = _nn(cg, h_b) * ex[:, gs]
                xd_b = xd[:, gs].astype(BF16)
                for hp in range(4):
                    ps = slice(hp * LANES, (hp + 1) * LANES)
                    yd = _nn(m_g[:, ps], _blockdiag(xd_b[:, ps]))
                    yssd_s[rows, g * 512 + hp * LANES:g * 512 + (hp + 1) * LANES] = yd + yo[:, ps]
                hs[:, gs] = h_g * e_l[:, gs] + _tn(bg, xf[:, gs].astype(BF16))
            return carry

        lax.fori_loop(0, nc, chunk, 0)

        ys = yssd_s[...] + dsk_ref[...] * xs_s[...]
        gt = ys * _silu(z_ref[...].astype(F32))
        for g in range(2):
            gs = slice(g * 512, (g + 1) * 512)
            gg = gt[:, gs]
            rn = lax.rsqrt(jnp.mean(gg * gg, axis=-1, keepdims=True) + EPS)
            y_ref[:, D + g * 512:D + (g + 1) * 512] = (gg * rn * nrm_ref[:, gs]).astype(BF16)

    def const(shape):
        return pl.BlockSpec(shape, lambda i: (0, 0))

    in_specs = [main(D, C_UB), main(D, C_UC), main(D, C_UH), main(D, C_Z), main(D, C_XS), main(512, C_BC),
                main(LANES, C_DT), halo(D, C_UC), halo(D, C_UH), halo(D, C_XS), halo(512, C_BC),
                const((8, D)), const((8, D)), const((8, 512)), const((1, D)), const((1, 512)),
                const((1, LANES)), const((1, LANES)), const((1, D)), const((1, D)), const((LANES, D))]
    return pl.pallas_call(
        body, grid=(nblk,),
        in_specs=in_specs,
        out_specs=[pl.BlockSpec((tt, MIX), lambda i: (i, 0)),
                   pl.BlockSpec((nc, NSTATE, D), lambda i: (i, 0, 0))],
        out_shape=[SDS((t_len, MIX), BF16), SDS((t_len // CHUNK, NSTATE, D), BF16)],
        scratch_shapes=[pltpu.VMEM((NSTATE, D), F32), pltpu.VMEM((tt, D), F32), pltpu.VMEM((tt, 512), F32),
                        pltpu.VMEM((tt, D), F32), pltpu.VMEM((tt, D), F32), pltpu.VMEM((tt, D), F32)],
        compiler_params=_cparams(("arbitrary",)), name="mixer_fwd")(
            *([proj] * 11), prm["scw"], prm["cwx"], prm["cwbc"], prm["cbx"], prm["cbbc"], prm["dtb"],
            prm["alog"], prm["dskx"], prm["nrm"], prm["eh"])


def mixer_bwd(proj, dy, states, prm, tt):
    t_len = proj.shape[0]
    nblk = t_len // tt
    nc = tt // CHUNK
    r8 = tt // 8

    def rev(i):
        return nblk - 1 - i

    def main(width, col):
        return pl.BlockSpec((tt, width), lambda i, c=col // width: (rev(i), c))

    def halo(width, col):
        return pl.BlockSpec((8, width), lambda i, c=col // width: (jnp.maximum(rev(i) * r8 - 1, 0), c))

    def body(ub_ref, uc_ref, uh_ref, z_ref, xr_ref, bcr_ref, dtr_ref, uch_ref, uhh_ref, xrh_ref, bcrh_ref,
             dy_ref, st_ref,
             scw_ref, cwx_ref, cwbc_ref, cbx_ref, cbbc_ref, dtb_ref, alog_ref, dsk_ref, nrm_ref, eh_ref, eht_ref,
             dp_ref, gscw_ref, gcwx_ref, gcwbc_ref, gvec_ref, gdt_ref,
             dhs, xs_s, bc_s, dtx_s, cumx_s, dys_s, dxs_s, dbc_s, red_s, ddtx_s, nx_cv, nx_px, nx_pbc):
        i = pl.program_id(0)
        blk = rev(i)

        @pl.when(i == 0)
        def _():
            dhs[...] = jnp.zeros_like(dhs)
            nx_cv[...] = jnp.zeros_like(nx_cv)
            nx_px[...] = jnp.zeros_like(nx_px)
            nx_pbc[...] = jnp.zeros_like(nx_pbc)
            gscw_ref[...] = jnp.zeros_like(gscw_ref)
            gcwx_ref[...] = jnp.zeros_like(gcwx_ref)
            gcwbc_ref[...] = jnp.zeros_like(gcwbc_ref)
            gvec_ref[...] = jnp.zeros_like(gvec_ref)
            gdt_ref[...] = jnp.zeros_like(gdt_ref)

        keep = jnp.where(blk == 0, 0.0, 1.0)

        ub = ub_ref[...].astype(F32)
        uc = uc_ref[...].astype(F32)
        uh = uh_ref[...].astype(F32)
        v = uc * uh
        vh = uch_ref[...].astype(F32) * uhh_ref[...].astype(F32) * keep
        dya = dy_ref[:, 0:D].astype(F32)
        dp_ref[:, C_UB:C_UB + D] = (dya * _conv_fwd(v, vh, scw_ref, 3)).astype(BF16)
        dcv = dya * ub
        nxt = nx_cv[...]
        dv = None
        for k in range(3):
            gscw_ref[k:k + 1, :] += jnp.sum(dcv * _shift_dn(v, vh, 2 - k), axis=0, keepdims=True)
            term = scw_ref[k:k + 1, :] * _shift_up(dcv, nxt, 2 - k)
            dv = term if dv is None else dv + term
        nx_cv[...] = dcv[0:8]
        dp_ref[:, C_UC:C_UC + D] = (dv * uh).astype(BF16)
        dp_ref[:, C_UH:C_UH + D] = (dv * uc).astype(BF16)

        xraw = xr_ref[...].astype(F32)
        xrh = xrh_ref[...].astype(F32) * keep
        bcraw = bcr_ref[...].astype(F32)
        bcrh = bcrh_ref[...].astype(F32) * keep
        pre_x = _conv_fwd(xraw, xrh, cwx_ref, 4) + cbx_ref[...]
        pre_bc = _conv_fwd(bcraw, bcrh, cwbc_ref, 4) + cbbc_ref[...]
        xs = _silu(pre_x)
        xs_s[...] = xs
        bc_s[...] = _silu(pre_bc)
        dt_pre = dtr_ref[...].astype(F32) + dtb_ref[...]
        dt = _softplus(dt_pre)
        a_neg = -jnp.exp(alog_ref[...])
        pos = lax.broadcasted_iota(jnp.int32, (tt, LANES), 0) & (CHUNK - 1)
        cum = _chunk_cumsum(dt * a_neg, pos)
        eh = eh_ref[...]
        eht = eht_ref[...]
        dtx_s[...] = _expand(dt, eh)
        cumx_s[...] = _expand(cum, eh)

        irep, causal = _lane_masks()
        irep_g, _ = _lane_masks(512)
        rep = _rep_matrix()
        row64 = lax.broadcasted_iota(jnp.int32, (CHUNK, 512), 0)
        lane128 = lax.broadcasted_iota(jnp.int32, (CHUNK, LANES), 1)

        def fwd_chunk(c, carry):
            r0 = pl.multiple_of(c * CHUNK, CHUNK)
            rows = pl.ds(r0, CHUNK)
            cumx = cumx_s[rows, :]
            xd = xs_s[rows, :] * dtx_s[rows, :]
            ex = jnp.exp(cumx)
            rvec = jnp.sum(jnp.where(irep, cumx, 0.0), axis=0, keepdims=True)
            lam = jnp.where(causal, jnp.exp(jnp.where(causal, cumx - rvec, 0.0)), 0.0)
            bc = bc_s[rows, :]
            for g in range(2):
                gs = slice(g * 512, (g + 1) * 512)
                bg = bc[:, g * NSTATE:(g + 1) * NSTATE].astype(BF16)
                cg = bc[:, 256 + g * NSTATE:256 + (g + 1) * NSTATE].astype(BF16)
                s_rep = _nn(_nt(cg, bg).astype(BF16), rep)
                m_g = (s_rep * lam[:, gs]).astype(BF16)
                yo = _nn(cg, st_ref[c, :, gs]) * ex[:, gs]
                xd_b = xd[:, gs].astype(BF16)
                for hp in range(4):
                    ps = slice(hp * LANES, (hp + 1) * LANES)
                    yd = _nn(m_g[:, ps], _blockdiag(xd_b[:, ps]))
                    dys_s[rows, g * 512 + hp * LANES:g * 512 + (hp + 1) * LANES] = yd + yo[:, ps]
            return carry

        lax.fori_loop(0, nc, fwd_chunk, 0)

        z = z_ref[...].astype(F32)
        sz = _silu(z)
        ys = dys_s[...] + dsk_ref[...] * xs
        gt = ys * sz
        dyb = dy_ref[:, D:MIX].astype(F32)
        for g in range(2):
            gs = slice(g * 512, (g + 1) * 512)
            gg = gt[:, gs]
            rn = lax.rsqrt(jnp.mean(gg * gg, axis=-1, keepdims=True) + EPS)
            gvec_ref[0:1, gs] += jnp.sum(dyb[:, gs] * gg * rn, axis=0, keepdims=True)
            dgn = dyb[:, gs] * nrm_ref[:, gs]
            dgt = rn * (dgn - gg * (rn * rn) * jnp.mean(dgn * gg, axis=-1, keepdims=True))
            dys = dgt * sz[:, gs]
            dys_s[:, gs] = dys
            dp_ref[:, C_Z + g * 512:C_Z + (g + 1) * 512] = (dgt * ys[:, gs] * _dsilu(z[:, gs])).astype(BF16)
        dys_all = dys_s[...]
        gvec_ref[1:2, :] += jnp.sum(dys_all * xs, axis=0, keepdims=True)

        def bwd_chunk(cc, carry):
            c = nc - 1 - cc
            r0 = pl.multiple_of(c * CHUNK, CHUNK)
            rows = pl.ds(r0, CHUNK)
            cumx = cumx_s[rows, :]
            cum_l = cumx[CHUNK - 1:CHUNK, :]
            xs_c = xs_s[rows, :]
            dtx = dtx_s[rows, :]
            xd = xs_c * dtx
            f = jnp.exp(cum_l - cumx)
            xf = xd * f
            ex = jnp.exp(cumx)
            e_l = jnp.exp(cum_l)
            rvec = jnp.sum(jnp.where(irep, cumx, 0.0), axis=0, keepdims=True)
            lam = jnp.where(causal, jnp.exp(jnp.where(causal, cumx - rvec, 0.0)), 0.0)
            bc = bc_s[rows, :]
            dyc = dys_s[rows, :]
            for g in range(2):
                gs = slice(g * 512, (g + 1) * 512)
                bg = bc[:, g * NSTATE:(g + 1) * NSTATE].astype(BF16)
                cg = bc[:, 256 + g * NSTATE:256 + (g + 1) * NSTATE].astype(BF16)
                h0 = st_ref[c, :, gs]
                dh = dhs[:, gs]
                dh_b = dh.astype(BF16)
                xf_g = xf[:, gs]
                dxf = _nn(bg, dh_b)
                db = _nt(xf_g.astype(BF16), dh_b)
                s_rep = _nn(_nt(cg, bg).astype(BF16), rep)
                lam_g = lam[:, gs]
                m_g = s_rep * lam_g
                m_b = m_g.astype(BF16)
                ex_g = ex[:, gs]
                dy_g = dyc[:, gs]
                yo = _nn(cg, h0) * ex_g
                dg_b = (dy_g * ex_g).astype(BF16)
                dc = _nt(dg_b, h0)
                el_g = e_l[:, gs]
                dee = jnp.sum(dh * h0.astype(F32), axis=0, keepdims=True) * el_g
                dhs[:, gs] = dh * el_g + _tn(cg, dg_b)
                xd_b = xd[:, gs].astype(BF16)
                dy_b = dy_g.astype(BF16)
                dm_parts, dxd_parts = [], []
                for hp in range(4):
                    ps = slice(hp * LANES, (hp + 1) * LANES)
                    bd = _blockdiag(xd_b[:, ps])
                    dm_parts.append(_nt(dy_b[:, ps], bd))
                    t2 = _tn(m_b[:, ps], dy_b[:, ps])
                    dxd_parts.append(jnp.where(lane128 < HDIM, t2[0:CHUNK], t2[CHUNK:2 * CHUNK]))
                dm = jnp.concatenate(dm_parts, axis=1)
                dxd = jnp.concatenate(dxd_parts, axis=1) + dxf * f[:, gs]
                dseg = dm * m_g
                ds_b = _nt((dm * lam_g).astype(BF16), rep).astype(BF16)
                dc = dc + _nn(ds_b, bg)
                db = db + _tn(ds_b, cg)
                colsum = jnp.sum(dseg, axis=0, keepdims=True)
                dxfxf = dxf * xf_g
                red = dseg - jnp.where(irep_g, colsum, 0.0) + dy_g * yo - dxfxf
                last = jnp.sum(dxfxf, axis=0, keepdims=True) + dee
                red = red + jnp.where(row64 == CHUNK - 1, last, 0.0)
                red_s[rows, gs] = red
                ddtx_s[rows, gs] = dxd * xs_c[:, gs]
                dxs_s[rows, gs] = dxd * dtx[:, gs] + dsk_ref[:, gs] * dy_g
                dbc_s[rows, g * NSTATE:(g + 1) * NSTATE] = db
                dbc_s[rows, 256 + g * NSTATE:256 + (g + 1) * NSTATE] = dc
            return carry

        lax.fori_loop(0, nc, bwd_chunk, 0)

        dcum = _head_reduce(red_s[...], eht)
        da = _chunk_rcumsum(dcum, pos)
        ddt = _head_reduce(ddtx_s[...], eht) + da * a_neg
        gdt_ref[1:2, :] += jnp.sum(da * dt, axis=0, keepdims=True) * a_neg
        ddt_raw = ddt * _sigmoid(dt_pre)
        lane_t = lax.broadcasted_iota(jnp.int32, (tt, LANES), 1)
        ddt_raw = jnp.where(lane_t < NHEAD, ddt_raw, 0.0)
        gdt_ref[0:1, :] += jnp.sum(ddt_raw, axis=0, keepdims=True)
        dp_ref[:, C_DT:C_DT + LANES] = ddt_raw.astype(BF16)

        dpx = dxs_s[...] * _dsilu(pre_x)
        dpbc = dbc_s[...] * _dsilu(pre_bc)
        gvec_ref[2:3, :] += jnp.sum(dpx, axis=0, keepdims=True)
        gcwbc_ref[4:5, :] += jnp.sum(dpbc, axis=0, keepdims=True)
        nxt_x = nx_px[...]
        nxt_bc = nx_pbc[...]
        dxr, dbcr = None, None
        for k in range(4):
            gcwx_ref[k:k + 1, :] += jnp.sum(dpx * _shift_dn(xraw, xrh, 3 - k), axis=0, keepdims=True)
            gcwbc_ref[k:k + 1, :] += jnp.sum(dpbc * _shift_dn(bcraw, bcrh, 3 - k), axis=0, keepdims=True)
            tx = cwx_ref[k:k + 1, :] * _shift_up(dpx, nxt_x, 3 - k)
            tb = cwbc_ref[k:k + 1, :] * _shift_up(dpbc, nxt_bc, 3 - k)
            dxr = tx if dxr is None else dxr + tx
            dbcr = tb if dbcr is None else dbcr + tb
        nx_px[...] = dpx[0:8]
        nx_pbc[...] = dpbc[0:8]
        dp_ref[:, C_XS:C_XS + D] = dxr.astype(BF16)
        dp_ref[:, C_BC:C_BC + 512] = dbcr.astype(BF16)

        @pl.when(i == nblk - 1)
        def _():
            gdt_ref[2:3, :] = _head_reduce(gvec_ref[1:2, :] * jnp.ones((8, 1), F32), eht)[0:1, :]

    def const(shape):
        return pl.BlockSpec(shape, lambda i: (0, 0))

    in_specs = [main(D, C_UB), main(D, C_UC), main(D, C_UH), main(D, C_Z), main(D, C_XS), main(512, C_BC),
                main(LANES, C_DT), halo(D, C_UC), halo(D, C_UH), halo(D, C_XS), halo(512, C_BC),
                pl.BlockSpec((tt, MIX), lambda i: (rev(i), 0)),
                pl.BlockSpec((nc, NSTATE, D), lambda i: (rev(i), 0, 0)),
                const((8, D)), const((8, D)), const((8, 512)), const((1, D)), const((1, 512)),
                const((1, LANES)), const((1, LANES)), const((1, D)), const((1, D)), const((LANES, D)),
                const((D, LANES))]
    return pl.pallas_call(
        body, grid=(nblk,),
        in_specs=in_specs,
        out_specs=[pl.BlockSpec((tt, NINP), lambda i: (rev(i), 0)),
                   const((8, D)), const((8, D)), const((8, 512)), const((8, D)), const((8, LANES))],
        out_shape=[SDS((t_len, NINP), BF16), SDS((8, D), F32), SDS((8, D), F32), SDS((8, 512), F32),
                   SDS((8, D), F32), SDS((8, LANES), F32)],
        scratch_shapes=[pltpu.VMEM((NSTATE, D), F32),
                        pltpu.VMEM((tt, D), F32), pltpu.VMEM((tt, 512), F32),
                        pltpu.VMEM((tt, D), F32), pltpu.VMEM((tt, D), F32),
                        pltpu.VMEM((tt, D), F32), pltpu.VMEM((tt, D), F32),
                        pltpu.VMEM((tt, 512), F32),
                        pltpu.VMEM((tt, D), F32), pltpu.VMEM((tt, D), F32),
                        pltpu.VMEM((8, D), F32), pltpu.VMEM((8, D), F32), pltpu.VMEM((8, 512), F32)],
        compiler_params=_cparams(("arbitrary",)), name="mixer_bwd")(
            *([proj] * 11), dy, states, prm["scw"], prm["cwx"], prm["cwbc"], prm["cbx"], prm["cbbc"], prm["dtb"],
            prm["alog"], prm["dskx"], prm["nrm"], prm["eh"], prm["eht"])


TN_IN = 1920


def layer_fwd(x, lw, prm, tt):
    proj, h1 = norm_matmul(x, lw["nw1"], lw["win"], "in_proj")
    y, st = mixer_fwd(proj, prm, tt)
    x1 = matmul_residual(y, lw["wout"], x, False, "out_proj")
    u, h2 = norm_matmul(x1, lw["nw2"], lw["wup"], "up_proj")
    x2 = matmul_residual(u, lw["wdn"], x1, True, "down_proj")
    return x2, (x, h1, proj, st, y, x1, h2, u)


def _dw(a, b, a_cols, b_cols, relu2, name):
    m_len, n_len = a.shape[1], b.shape[1]
    n_a, n_b = m_len // a_cols, n_len // b_cols
    assert n_a == 1 or n_b == 1
    if n_b == 1:
        return matmul_tn(
            a, b,
            lambda t_: pl.BlockSpec((t_, a_cols), lambda n, t: (t, n)),
            lambda t_: pl.BlockSpec((t_, n_len), lambda n, t: (t, 0)),
            pl.BlockSpec((a_cols, n_len), lambda n, t: (n, 0)), SDS((m_len, n_len), BF16), n_a, relu2, name)
    return matmul_tn(
        a, b,
        lambda t_: pl.BlockSpec((t_, m_len), lambda n, t: (t, 0)),
        lambda t_: pl.BlockSpec((t_, b_cols), lambda n, t: (t, n)),
        pl.BlockSpec((m_len, b_cols), lambda n, t: (0, n)), SDS((m_len, n_len), BF16), n_b, relu2, name)


def layer_bwd_mlp(dx2, dx2b, lw, saved):
    _, _, _, _, y, x1, h2, u = saved
    du = matmul_nt_act(dx2b, lw["wdn"], u, "mlp_bwd_du")
    g_wdn = _dw(u, dx2b, 1024, D, True, "dw_down")
    dx1, dx1b, g_nw2 = matmul_nt_norm_bwd(du, lw["wup"], x1, lw["nw2"], dx2, "mlp_bwd_dx")
    cb = DFF // N_DEV
    g_wup = matmul_tn(
        h2, du,
        lambda t_: pl.BlockSpec((t_, D), lambda n, t: (t, 0)),
        lambda t_: pl.BlockSpec((t_, 2 * cb), lambda n, t: (t, n)),
        pl.BlockSpec((2, D, cb), lambda n, t: (n, 0, 0)), SDS((N_DEV, D, cb), BF16), N_DEV // 2, False, "dw_up")
    dy = matmul_nt_act(dx1b, lw["wout"], None, "out_bwd_dy")
    g_wout = _dw(y, dx1b, 1024, D, False, "dw_out")
    return dx1, dx1b, dy, {"wout": g_wout, "wup": g_wup, "wdn": g_wdn, "nw2": g_nw2[0]}


def layer_bwd_mix(dx1, dy, lw, prm, saved, tt):
    x, h1, proj, st = saved[:4]
    dproj, gscw, gcwx, gcwbc, gvec, gdt = mixer_bwd(proj, dy, st, prm, tt)
    dx0, dx0b, g_nw1 = matmul_nt_norm_bwd(dproj, lw["win"], x, lw["nw1"], dx1, "in_bwd_dx")
    g_win = _dw(h1, dproj, D, TN_IN, False, "dw_in")
    grads = {
        "win": g_win, "scw": gscw[0:3], "cw": jnp.concatenate([gcwx[0:4], gcwbc[0:4]], axis=1),
        "cb": jnp.concatenate([gvec[2], gcwbc[4]], axis=0),
        "dtb": gdt[0, :NHEAD], "alog": gdt[1, :NHEAD], "dsk": gdt[2, :NHEAD],
        "nrm": gvec[0], "nw1": g_nw1[0],
    }
    return dx0, dx0b, grads


def layer_bwd(dx2, dx2b, lw, prm, saved, tt):
    dx1, dx1b, dy, g_mlp = layer_bwd_mlp(dx2, dx2b, lw, saved)
    dx0, dx0b, g_mix = layer_bwd_mix(dx1, dy, lw, prm, saved, tt)
    return dx0, dx0b, {**g_mlp, **g_mix}


def layer_params(i, norm_mix_w, win, wout, wup, wdn, scw, cw, ssd_conv_b, dt_bias, a_log, d_skip, ssd_norm_w,
                 norm_mlp_w, eh, eht):
    def rows8(a):
        return jnp.pad(a, ((0, 8 - a.shape[0]), (0, 0)))

    def lanes128(a):
        return jnp.pad(a, (0, LANES - a.shape[0]))[None, :]

    lw = {"win": win[i], "wout": wout[i], "wup": wup[i], "wdn": wdn[i],
          "nw1": norm_mix_w[i][None, :], "nw2": norm_mlp_w[i][None, :]}
    prm = {"scw": rows8(scw[i]), "cwx": rows8(cw[i][:, :D]), "cwbc": rows8(cw[i][:, D:]),
           "cbx": ssd_conv_b[i][None, :D], "cbbc": ssd_conv_b[i][None, D:],
           "dtb": lanes128(dt_bias[i]), "alog": lanes128(a_log[i]),
           "dskx": jnp.repeat(d_skip[i], HDIM)[None, :], "nrm": ssd_norm_w[i][None, :], "eh": eh, "eht": eht}
    return lw, prm


def _flip(v, bit):
    return 1 - v if bit else v


def all_gather(arrs, name):
    n = len(arrs)

    def body(*refs):
        ins, outs = refs[:n], refs[n:2 * n]
        send_sems, recv_sems, local_sems = refs[2 * n:]
        x, y, c = lax.axis_index("x"), lax.axis_index("y"), lax.axis_index("c")
        sibling = (x, y, 1 - c)
        chips = [(1 - x, y), (x, 1 - y), (1 - x, 1 - y)]

        def idx(px, py, pc):
            return 4 * px + 2 * py + pc

        def copy(a, k, block, to, src=None):
            dst = outs[a].at[idx(*block)]
            return pltpu.make_async_remote_copy(
                src_ref=dst if src is None else src, dst_ref=dst,
                send_sem=send_sems.at[a, k], recv_sem=recv_sems.at[a, k], device_id=to, device_id_type=MESH)

        me = (x, y, c)
        mine = [pltpu.make_async_copy(ins[a], outs[a].at[idx(*me)], local_sems.at[a]) for a in range(n)]
        for cp in mine:
            cp.start()
        first = []
        for a in range(n):
            first.append(copy(a, 0, me, sibling, src=ins[a]))
            first += [copy(a, 1 + j, me, (*chip, c), src=ins[a]) for j, chip in enumerate(chips)]
        for cp in first:
            cp.start()
        passed = []
        for j, chip in enumerate(chips):
            for a in range(n):
                copy(a, 1 + j, (*chip, c), me).wait_recv()
                cp = copy(a, 4 + j, (*chip, c), sibling)
                cp.start()
                passed.append(cp)
        for a in range(n):
            copy(a, 0, sibling, me).wait_recv()
            for j, chip in enumerate(chips):
                copy(a, 4 + j, (*chip, 1 - c), me).wait_recv()
        for cp in first + passed:
            cp.wait_send()
        for cp in mine:
            cp.wait()

    any_spec = pl.BlockSpec(memory_space=pl.ANY)
    return pl.pallas_call(
        body, in_specs=[any_spec] * n, out_specs=[any_spec] * n,
        out_shape=[SDS((N_DEV,) + a.shape, a.dtype) for a in arrs],
        scratch_shapes=[pltpu.SemaphoreType.DMA((n, 7)), pltpu.SemaphoreType.DMA((n, 7)),
                        pltpu.SemaphoreType.DMA((n,))],
        name=name)(*arrs)


HBM_SPEC = pl.BlockSpec(memory_space=pltpu.HBM)
SEM_SPEC = pl.BlockSpec(memory_space=pltpu.SEMAPHORE)
SIDE_EFFECT = pltpu.SideEffectType.DATAFLOW_SIDE_EFFECTING
N_PEER = N_DEV - 1


def _peer(mask):
    x, y, c = lax.axis_index("x"), lax.axis_index("y"), lax.axis_index("c")
    return _flip(x, mask & 4), _flip(y, mask & 2), _flip(c, mask & 1)


def exchange_start(srcs, per_peer, name, after=None):
    n = len(srcs)
    lands = [SDS((N_DEV,) + (a.shape[1:] if per_peer else a.shape), a.dtype) for a in srcs]
    n_in = 2 * n + (after is not None)

    def body(*refs):
        src_refs, land_refs = refs[:n], refs[n:2 * n]
        send_sems, recv_sems = refs[n_in], refs[n_in + 1]
        token = refs[-1]
        x, y, c = lax.axis_index("x"), lax.axis_index("y"), lax.axis_index("c")
        me = 4 * x + 2 * y + c
        for a in range(n):
            for mask in range(1, N_DEV):
                px, py, pc = _peer(mask)
                part = src_refs[a].at[4 * px + 2 * py + pc] if per_peer else src_refs[a]
                pltpu.make_async_remote_copy(
                    src_ref=part, dst_ref=land_refs[a].at[me], send_sem=send_sems.at[a * N_PEER + mask - 1],
                    recv_sem=recv_sems.at[a * N_PEER + mask - 1], device_id=(px, py, pc),
                    device_id_type=MESH).start()
        token[...] = jnp.zeros_like(token)

    out = pl.pallas_call(
        body, name=name,
        out_shape=(pltpu.SemaphoreType.DMA((n * N_PEER,)), pltpu.SemaphoreType.DMA((n * N_PEER,)),
                   *[pltpu.HBM(a.shape, a.dtype) for a in srcs], *[pltpu.HBM(l.shape, l.dtype) for l in lands],
                   SDS((8, LANES), F32)),
        in_specs=(HBM_SPEC,) * (2 * n) + ((pl.BlockSpec(memory_space=pl.ANY),) if after is not None else ()),
        out_specs=(SEM_SPEC, SEM_SPEC) + (HBM_SPEC,) * (2 * n) + (pl.BlockSpec(memory_space=pltpu.VMEM),),
        input_output_aliases={k: 2 + k for k in range(2 * n)},
        compiler_params=pltpu.CompilerParams(has_side_effects=SIDE_EFFECT),
    )(*[pltpu.with_memory_space_constraint(a, pltpu.HBM) for a in srcs],
      *[pltpu.with_memory_space_constraint(lax.empty(l.shape, l.dtype), pltpu.HBM) for l in lands],
      *([after] if after is not None else []))
    return out[0], out[1], list(out[2:2 + n]), list(out[2 + n:2 + 2 * n]), out[-1]


def exchange_wait(started, after, per_peer, name):
    send_sems, recv_sems, srcs, lands, _ = started
    n = len(srcs)

    def body(*refs):
        src_refs, land_refs = refs[:n], refs[n:2 * n]
        send_sems, recv_sems = refs[2 * n], refs[2 * n + 1]
        for mask in range(1, N_DEV):
            for a in range(n):
                copy = pltpu.make_async_remote_copy(
                    src_ref=src_refs[a].at[0] if per_peer else src_refs[a], dst_ref=land_refs[a].at[0],
                    send_sem=send_sems.at[a * N_PEER + mask - 1], recv_sem=recv_sems.at[a * N_PEER + mask - 1],
                    device_id=_peer(mask), device_id_type=MESH)
                copy.wait_send()
                copy.wait_recv()

    out = pl.pallas_call(
        body, name=name,
        out_shape=tuple(pltpu.HBM(a.shape, a.dtype) for a in srcs + lands),
        in_specs=(HBM_SPEC,) * (2 * n) + (SEM_SPEC, SEM_SPEC, pl.BlockSpec(memory_space=pl.ANY)),
        out_specs=(HBM_SPEC,) * (2 * n), input_output_aliases={k: k for k in range(2 * n)},
        compiler_params=pltpu.CompilerParams(has_side_effects=SIDE_EFFECT),
    )(*srcs, *lands, send_sems, recv_sems, after)
    return list(out[:n]), list(out[n:])


IN_SHARD = NIN // N_DEV
SLOT_W = 768


def _slot_window(j):
    return (IN_SHARD * j // LANES) * LANES, -(-(IN_SHARD * (j + 1)) // LANES) * LANES


def _placement(j):
    a, b = _slot_window(j)
    r = lax.broadcasted_iota(jnp.int32, (SLOT_W, b - a), 0)
    c = lax.broadcasted_iota(jnp.int32, (SLOT_W, b - a), 1)
    return jnp.where(jnp.logical_and(c == r + (IN_SHARD * j - a), r < IN_SHARD), 1.0, 0.0).astype(BF16)


def assemble_w_in(land):
    tm = 256

    def body(l_ref, o_ref, acc):
        acc[...] = jnp.zeros_like(acc)
        for j in range(N_DEV):
            a, b = _slot_window(j)
            acc[:, a:b] += _nn(l_ref[j], _placement(j))
        o_ref[...] = acc[...].astype(BF16)

    return pl.pallas_call(
        body, grid=(D // tm,),
        in_specs=[pl.BlockSpec((N_DEV, tm, SLOT_W), lambda i: (0, i, 0))],
        out_specs=pl.BlockSpec((tm, NINP), lambda i: (i, 0)),
        out_shape=SDS((D, NINP), BF16),
        scratch_shapes=[pltpu.VMEM((tm, NINP), F32)],
        compiler_params=_cparams(("parallel",)), name="assemble_w_in")(land)


def scatter_w_in(dw):
    tm = 256

    def body(d_ref, o_ref):
        for j in range(N_DEV):
            a, b = _slot_window(j)
            o_ref[j] = _nt(d_ref[:, a:b], _placement(j)).astype(BF16)

    return pl.pallas_call(
        body, grid=(D // tm,),
        in_specs=[pl.BlockSpec((tm, NINP), lambda i: (i, 0))],
        out_specs=pl.BlockSpec((N_DEV, tm, SLOT_W), lambda i: (0, i, 0)),
        out_shape=SDS((N_DEV, D, SLOT_W), BF16),
        compiler_params=_cparams(("parallel",)), name="scatter_w_in")(dw)


def _adamw_math(g, w_ref, m_ref, v_ref, g_ref, d_ref, nm_ref, nv_ref):
    mn = ADAM_B1 * m_ref[...] + (1.0 - ADAM_B1) * g
    vn = ADAM_B2 * v_ref[...] + (1.0 - ADAM_B2) * jnp.square(g)
    m_hat = mn / (1.0 - ADAM_B1 ** ADAM_STEP)
    v_hat = vn / (1.0 - ADAM_B2 ** ADAM_STEP)
    g_ref[...] = g
    d_ref[...] = -ADAM_LR * (m_hat / (jnp.sqrt(v_hat) + ADAM_EPS) + ADAM_WD * w_ref[...])
    nm_ref[...] = mn
    nv_ref[...] = vn


def adamw_layers(w, slots, m, v, name):
    depth, r_len, c_len = w.shape
    cs = slots[0].shape[2]
    br = min(128, r_len)
    assert r_len % br == 0

    def body(w_ref, *rest):
        s_refs, (m_ref, v_ref, g_ref, d_ref, nm_ref, nv_ref) = rest[:depth], rest[depth:]
        layer = pl.program_id(0)
        for k in range(depth):
            @pl.when(layer == k)
            def _(k=k):
                g = s_refs[k][0, :, 0:c_len].astype(F32)
                for j in range(1, N_DEV):
                    g = g + s_refs[k][j, :, 0:c_len].astype(F32)
                _adamw_math(g, w_ref, m_ref, v_ref, g_ref, d_ref, nm_ref, nv_ref)

    spec = pl.BlockSpec((None, br, c_len), lambda l, i: (l, i, 0))
    s_specs = [pl.BlockSpec((N_DEV, br, cs), lambda l, i, k=k: (0, jnp.where(l == k, i, 0), 0))
               for k in range(depth)]
    return pl.pallas_call(
        body, grid=(depth, r_len // br),
        in_specs=[spec] + s_specs + [spec, spec],
        out_specs=[spec] * 4, out_shape=[SDS(w.shape, F32)] * 4,
        compiler_params=_cparams(("arbitrary", "arbitrary")), name=name)(w, *slots, m, v)


def adamw(w, slots, m, v, name):
    r_len, c_len = w.shape
    br = r_len if r_len <= 512 else 512
    assert r_len % br == 0

    def body(w_ref, s_ref, m_ref, v_ref, g_ref, d_ref, nm_ref, nv_ref):
        g = s_ref[0].astype(F32)
        for k in range(1, N_DEV):
            g = g + s_ref[k].astype(F32)
        _adamw_math(g, w_ref, m_ref, v_ref, g_ref, d_ref, nm_ref, nv_ref)

    spec = pl.BlockSpec((br, c_len), lambda i: (i, 0))
    return pl.pallas_call(
        body, grid=(r_len // br,),
        in_specs=[spec, pl.BlockSpec((N_DEV, br, c_len), lambda i: (0, i, 0)), spec, spec],
        out_specs=[spec] * 4, out_shape=[SDS((r_len, c_len), F32)] * 4,
        compiler_params=_cparams(("parallel",)), name=name)(w, slots, m, v)


def _adamw_nd(w, slots, m, v, name):
    shp = w.shape
    r = int(np.prod(shp[:-1]))
    outs = adamw(w.reshape(r, shp[-1]), slots.reshape(N_DEV, r, shp[-1]), m.reshape(r, shp[-1]),
                 v.reshape(r, shp[-1]), name)
    return [o.reshape(shp) for o in outs]


SMALL = [("norm_mix_w", DEPTH * D), ("ssd_conv_b", DEPTH * XBC), ("dt_bias", DEPTH * NHEAD),
         ("a_log", DEPTH * NHEAD), ("d_skip", DEPTH * NHEAD), ("ssd_norm_w", DEPTH * D),
         ("norm_mlp_w", DEPTH * D), ("final_norm_w", D)]
SMALL_LEN = sum(s for _, s in SMALL)
SMALL_ROWS = -(-SMALL_LEN // LANES)


def _pack_small(parts):
    flat = jnp.concatenate([parts[k].reshape(-1) for k, _ in SMALL])
    return jnp.pad(flat, (0, SMALL_ROWS * LANES - SMALL_LEN)).reshape(SMALL_ROWS, LANES)


def _unpack_small(packed, shapes):
    flat = packed.reshape(-1)
    out, off = {}, 0
    for k, s in SMALL:
        out[k] = flat[off:off + s].reshape(shapes[k])
        off += s
    return out


def kernel(x, norm_mix_w, w_in, short_conv_w, ssd_conv_w, ssd_conv_b, dt_bias, a_log, d_skip, ssd_norm_w, w_out, norm_mlp_w, w_up, w_down, final_norm_w, loss_target, m_norm_mix_w, m_w_in, m_short_conv_w, m_ssd_conv_w, m_ssd_conv_b, m_dt_bias, m_a_log, m_d_skip, m_ssd_norm_w, m_w_out, m_norm_mlp_w, m_w_up, m_w_down, m_final_norm_w, v_norm_mix_w, v_w_in, v_short_conv_w, v_ssd_conv_w, v_ssd_conv_b, v_dt_bias, v_a_log, v_d_skip, v_ssd_norm_w, v_w_out, v_norm_mlp_w, v_w_up, v_w_down, v_final_norm_w):
    xs = x[0]
    t_len = xs.shape[0]
    tt = min(256, t_len)
    eh, eht = _head_matrices()
    me = 4 * lax.axis_index("x") + 2 * lax.axis_index("y") + lax.axis_index("c")

    first, rest = [], []
    for i in range(DEPTH):
        first.append(exchange_start(
            [jnp.pad(w_in[i].astype(BF16), ((0, 0), (0, SLOT_W - IN_SHARD))), short_conv_w[i], ssd_conv_w[i]],
            False, "w_in_start_%d" % i))
        rest.append(exchange_start(
            [w_out[i].astype(BF16), w_up[i].astype(BF16), w_down[i].astype(BF16)], False, "w_rest_start_%d" % i))
    tokens = sum(f[4][0, 0] for f in first + rest)

    def fill_own(srcs, lands, per_peer):
        own = [lax.dynamic_index_in_dim(s_, me, 0, keepdims=False) for s_ in srcs] if per_peer else srcs
        return [lax.dynamic_update_index_in_dim(l_, o_, me, 0) for l_, o_ in zip(lands, own)]

    act = xs
    saved, layers = [], []
    for i in range(DEPTH):
        after = norm_mix_w[i] + tokens if i == 0 else act
        g_in, g_sc, g_cw = fill_own(*exchange_wait(first[i], after, False, "w_in_wait_%d" % i), False)
        win = assemble_w_in(g_in)
        g_out, g_up, g_dn = fill_own(*exchange_wait(rest[i], win, False, "w_rest_wait_%d" % i), False)
        lw, prm = layer_params(
            i, norm_mix_w, [win] * DEPTH, [g_out.reshape(MIX, D)] * DEPTH, [g_up] * DEPTH,
            [g_dn.reshape(DFF, D)] * DEPTH, [g_sc.transpose(1, 0, 2).reshape(3, D)] * DEPTH,
            [g_cw.transpose(1, 0, 2).reshape(4, XBC)] * DEPTH, ssd_conv_b, dt_bias, a_log, d_skip, ssd_norm_w,
            norm_mlp_w, eh, eht)
        if i == 0:
            lw["nw1"] = after[None, :]
        layers.append((lw, prm))
        act, sv = layer_fwd(act, lw, prm, tt)
        saved.append(sv)
    loss_acc, dx, dxb, g_fw = loss_head(act, final_norm_w[None, :], loss_target[0])

    grads = [None] * DEPTH
    sent_rest, sent_in = [None] * DEPTH, [None] * DEPTH
    token = None
    for i in reversed(range(DEPTH)):
        lw, prm = layers[i]
        if token is not None:
            lw = dict(lw, nw2=lw["nw2"] + token)
        dx1, _, dy, g_mlp = layer_bwd_mlp(dx, dxb, lw, saved[i])
        sent_rest[i] = exchange_start(
            [g_mlp["wout"].reshape(N_DEV, MIX // N_DEV, D), g_mlp["wup"], g_mlp["wdn"].reshape(N_DEV, DFF // N_DEV, D)],
            True, "g_rest_start_%d" % i)
        dx, dxb, g_mix = layer_bwd_mix(dx1, dy, lw, dict(prm, nrm=prm["nrm"] + sent_rest[i][4][0, 0]), saved[i], tt)
        grads[i] = {**g_mlp, **g_mix}
        if i > 0:
            sent_in[i] = exchange_start([scatter_w_in(g_mix["win"])], True, "g_in_start_%d" % i)
            token = sent_in[i][4][0, 0]

    def stack(k):
        return jnp.stack([g[k] for g in grads])

    small = _pack_small({"norm_mix_w": stack("nw1"), "ssd_conv_b": stack("cb"), "dt_bias": stack("dtb"),
                         "a_log": stack("alog"), "d_skip": stack("dsk"), "ssd_norm_w": stack("nrm"),
                         "norm_mlp_w": stack("nw2"), "final_norm_w": g_fw[0]})
    r_small, r_sc, r_cw = all_gather([small, stack("scw"), stack("cw")], "gather_small_grads")
    r_sc = lax.dynamic_slice_in_dim(r_sc, me * (D // N_DEV), D // N_DEV, axis=3)
    r_cw = lax.dynamic_slice_in_dim(r_cw, me * (XBC // N_DEV), XBC // N_DEV, axis=3)
    sent_in[0] = exchange_start([scatter_w_in(grads[0]["win"])], True, "g_in_start_0", after=r_small)

    after = sent_in[0][4]
    recv = [fill_own(*exchange_wait(sent_rest[i], after, True, "g_rest_wait_%d" % i), True) for i in range(DEPTH)]
    res = {}
    res["w_out"] = adamw_layers(w_out, [r[0] for r in recv], m_w_out, v_w_out, "adamw_w_out")
    res["w_up"] = adamw_layers(w_up, [r[1] for r in recv], m_w_up, v_w_up, "adamw_w_up")
    res["w_down"] = adamw_layers(w_down, [r[2] for r in recv], m_w_down, v_w_down, "adamw_w_down")
    after = res["w_down"][1]
    recv_in = [fill_own(*exchange_wait(sent_in[i], after, True, "g_in_wait_%d" % i), True)[0] for i in range(DEPTH)]
    res["w_in"] = adamw_layers(w_in, recv_in, m_w_in, v_w_in, "adamw_w_in")
    res["short_conv_w"] = _adamw_nd(short_conv_w, r_sc, m_short_conv_w, v_short_conv_w, "adamw_short_conv")
    res["ssd_conv_w"] = _adamw_nd(ssd_conv_w, r_cw, m_ssd_conv_w, v_ssd_conv_w, "adamw_ssd_conv")
    small_w = {"norm_mix_w": norm_mix_w, "ssd_conv_b": ssd_conv_b, "dt_bias": dt_bias, "a_log": a_log,
               "d_skip": d_skip, "ssd_norm_w": ssd_norm_w, "norm_mlp_w": norm_mlp_w, "final_norm_w": final_norm_w}
    small_m = {"norm_mix_w": m_norm_mix_w, "ssd_conv_b": m_ssd_conv_b, "dt_bias": m_dt_bias, "a_log": m_a_log,
               "d_skip": m_d_skip, "ssd_norm_w": m_ssd_norm_w, "norm_mlp_w": m_norm_mlp_w,
               "final_norm_w": m_final_norm_w}
    small_v = {"norm_mix_w": v_norm_mix_w, "ssd_conv_b": v_ssd_conv_b, "dt_bias": v_dt_bias, "a_log": v_a_log,
               "d_skip": v_d_skip, "ssd_norm_w": v_ssd_norm_w, "norm_mlp_w": v_norm_mlp_w,
               "final_norm_w": v_final_norm_w}
    shapes = {k: a.shape for k, a in small_w.items()}
    packed = adamw(_pack_small(small_w), r_small, _pack_small(small_m), _pack_small(small_v), "adamw_small")
    unpacked = [_unpack_small(p, shapes) for p in packed]
    for k in small_w:
        res[k] = [u[k] for u in unpacked]

    loss = lax.psum(loss_acc[0, 0], ("x", "y", "c"))
    order = ["norm_mix_w", "w_in", "short_conv_w", "ssd_conv_w", "ssd_conv_b", "dt_bias", "a_log", "d_skip",
             "ssd_norm_w", "w_out", "norm_mlp_w", "w_up", "w_down", "final_norm_w"]
    out = [loss, dx[None]]
    for part in range(4):
        out += [res[k][part] for k in order]
    return tuple(out)
```

```python
import functools

import numpy as np
import jax
import jax.numpy as jnp
from jax import lax
from jax.experimental import pallas as pl
from jax.experimental.pallas import tpu as pltpu

F32 = jnp.float32
BF16 = jnp.bfloat16
SDS = jax.ShapeDtypeStruct

N_DEV = 8
DEPTH = 4
D = 1024
NIN = 5648
NINP = 5760
DFF = 4096
MIX = 2048
NHEAD = 16
HDIM = 64
NSTATE = 128
CHUNK = 64
XBC = 1536
EPS = 1e-5
LANES = 128

C_UB, C_UC, C_UH, C_Z, C_XS, C_BC, C_DT = 0, 1024, 2048, 3072, 4096, 5120, 5632

ADAM_LR = 0.001
ADAM_B1 = 0.9
ADAM_B2 = 0.999
ADAM_EPS = 1e-08
ADAM_WD = 0.01
ADAM_STEP = 10

VMEM_LIMIT = 56 * 1024 * 1024
MESH = pl.DeviceIdType.MESH


def _cparams(sem):
    return pltpu.CompilerParams(dimension_semantics=sem, vmem_limit_bytes=VMEM_LIMIT)


def _nt(a, b):
    return lax.dot_general(a, b, (((1,), (1,)), ((), ())), preferred_element_type=F32)


def _tn(a, b):
    return lax.dot_general(a, b, (((0,), (0,)), ((), ())), preferred_element_type=F32)


def _nn(a, b):
    return jnp.dot(a, b, preferred_element_type=F32)


def _sigmoid(v):
    return 1.0 / (1.0 + jnp.exp(-v))


def _split3(v):
    v1 = v.astype(BF16)
    r1 = v - v1.astype(F32)
    v2 = r1.astype(BF16)
    v3 = (r1 - v2.astype(F32)).astype(BF16)
    return v1, v2, v3


def _expand(v, eh):
    v1, v2, v3 = _split3(v)
    return _nn(v1, eh) + _nn(v2, eh) + _nn(v3, eh)


def _head_reduce(v, eht):
    v1, v2, v3 = _split3(v)
    return _nn(v1, eht) + _nn(v2, eht) + _nn(v3, eht)


def _head_matrices():
    eh = np.zeros((LANES, D), np.float32)
    for h in range(NHEAD):
        eh[h, h * HDIM:(h + 1) * HDIM] = 1.0
    return jnp.asarray(eh, BF16), jnp.asarray(eh.T.copy(), BF16)


def _resident(shape):
    return pl.BlockSpec(shape, lambda *_: (0,) * len(shape), pipeline_mode=pl.Buffered(1))


def _col_chunks(n, step):
    return [(c, min(c + step, n)) for c in range(0, n, step)]


def norm_matmul(x, nw, w, name):
    t_len = x.shape[0]
    blocked = w.ndim == 3
    n_len = w.shape[0] * w.shape[2] if blocked else w.shape[1]
    tm = min(512, t_len)
    chunks = _col_chunks(n_len, n_len // N_DEV if blocked else 1536)

    def body(x_ref, nw_ref, w_ref, o_ref, h_ref):
        xv = x_ref[...]
        r = lax.rsqrt(jnp.mean(xv * xv, axis=-1, keepdims=True) + EPS)
        hv = (xv * r * nw_ref[...]).astype(BF16)
        h_ref[...] = hv
        for j, (c0, c1) in enumerate(chunks):
            wj = w_ref[j] if blocked else w_ref[:, c0:c1]
            o_ref[:, c0:c1] = _nn(hv, wj).astype(o_ref.dtype)

    return pl.pallas_call(
        body, grid=(t_len // tm,),
        in_specs=[pl.BlockSpec((tm, D), lambda i: (i, 0)), _resident((1, D)), _resident(w.shape)],
        out_specs=[pl.BlockSpec((tm, n_len), lambda i: (i, 0)),
                   pl.BlockSpec((tm, D), lambda i: (i, 0))],
        out_shape=[SDS((t_len, n_len), BF16), SDS((t_len, D), BF16)],
        compiler_params=_cparams(("parallel",)), name=name)(x, nw, w)


def matmul_residual(a, w, res, relu2, name):
    t_len, k_len = a.shape
    tm = min(512, t_len)

    def body(a_ref, w_ref, res_ref, o_ref):
        av = a_ref[...]
        if relu2:
            af = jnp.maximum(av.astype(F32), 0.0)
            av = (af * af).astype(BF16)
        o_ref[...] = res_ref[...] + _nn(av, w_ref[...])

    return pl.pallas_call(
        body, grid=(t_len // tm,),
        in_specs=[pl.BlockSpec((tm, k_len), lambda i: (i, 0)),
                  _resident((k_len, D)),
                  pl.BlockSpec((tm, D), lambda i: (i, 0))],
        out_specs=pl.BlockSpec((tm, D), lambda i: (i, 0)),
        out_shape=SDS((t_len, D), F32),
        compiler_params=_cparams(("parallel",)), name=name)(a, w, res)


def matmul_nt_act(dy, w, u, name):
    t_len = dy.shape[0]
    n_len = w.shape[0]
    tm = min(512, t_len)
    chunks = _col_chunks(n_len, 1024)

    def body(dy_ref, w_ref, *rest):
        if u is None:
            (o_ref,) = rest
        else:
            u_ref, o_ref = rest
        dyv = dy_ref[...]
        for c0, c1 in chunks:
            p = _nt(dyv, w_ref[c0:c1, :])
            if u is not None:
                p = p * (2.0 * jnp.maximum(u_ref[:, c0:c1].astype(F32), 0.0))
            o_ref[:, c0:c1] = p.astype(o_ref.dtype)

    in_specs = [pl.BlockSpec((tm, D), lambda i: (i, 0)), _resident((n_len, D))]
    args = [dy, w]
    if u is not None:
        in_specs.append(pl.BlockSpec((tm, n_len), lambda i: (i, 0)))
        args.append(u)
    return pl.pallas_call(
        body, grid=(t_len // tm,),
        in_specs=in_specs,
        out_specs=pl.BlockSpec((tm, n_len), lambda i: (i, 0)),
        out_shape=SDS((t_len, n_len), BF16),
        compiler_params=_cparams(("parallel",)), name=name)(*args)


def matmul_tn(a, b, a_spec, b_spec, o_spec, o_shape, n_out, relu2, name):
    t_len = a.shape[0]
    tt = min(2048, t_len)
    nt = t_len // tt

    def body(a_ref, b_ref, o_ref, acc):
        t = pl.program_id(1)
        av = a_ref[...]
        if relu2:
            af = jnp.maximum(av.astype(F32), 0.0)
            av = (af * af).astype(BF16)
        p = _tn(av, b_ref[...])

        @pl.when(t == 0)
        def _():
            acc[...] = p

        @pl.when(t > 0)
        def _():
            acc[...] += p

        @pl.when(t == nt - 1)
        def _():
            if len(blk) == 3:
                for j in range(blk[0]):
                    o_ref[j] = acc[:, j * blk[2]:(j + 1) * blk[2]].astype(o_ref.dtype)
            else:
                o_ref[...] = acc[...].astype(o_ref.dtype)

    blk = tuple(o_spec.block_shape)
    acc_shape = (blk[1], blk[0] * blk[2]) if len(blk) == 3 else blk
    return pl.pallas_call(
        body, grid=(n_out, nt),
        in_specs=[a_spec(tt), b_spec(tt)],
        out_specs=o_spec, out_shape=o_shape,
        scratch_shapes=[pltpu.VMEM(acc_shape, F32)],
        compiler_params=_cparams(("parallel", "arbitrary")), name=name)(a, b)


def matmul_nt_norm_bwd(dy, w, x, nw, dres, name):
    t_len = x.shape[0]
    blocked = w.ndim == 3
    k_len = dy.shape[1]
    kb = k_len // N_DEV
    tm = min(512, t_len)

    def body(dy_ref, w_ref, x_ref, nw_ref, dres_ref, dx_ref, dxb_ref, dnw_ref):
        @pl.when(pl.program_id(0) == 0)
        def _():
            dnw_ref[...] = jnp.zeros_like(dnw_ref)

        if blocked:
            dh = _nt(dy_ref[:, 0:kb], w_ref[0])
            for j in range(1, N_DEV):
                dh = dh + _nt(dy_ref[:, j * kb:(j + 1) * kb], w_ref[j])
        else:
            dh = _nt(dy_ref[...], w_ref[...])
        xv = x_ref[...]
        r = lax.rsqrt(jnp.mean(xv * xv, axis=-1, keepdims=True) + EPS)
        xh = xv * r
        dnw_ref[0:1, :] += jnp.sum(dh * xh, axis=0, keepdims=True)
        g = dh * nw_ref[...]
        dx = dres_ref[...] + r * (g - xh * jnp.mean(g * xh, axis=-1, keepdims=True))
        dx_ref[...] = dx
        dxb_ref[...] = dx.astype(BF16)

    return pl.pallas_call(
        body, grid=(t_len // tm,),
        in_specs=[pl.BlockSpec((tm, k_len), lambda i: (i, 0)),
                  _resident(w.shape),
                  pl.BlockSpec((tm, D), lambda i: (i, 0)),
                  _resident((1, D)),
                  pl.BlockSpec((tm, D), lambda i: (i, 0))],
        out_specs=[pl.BlockSpec((tm, D), lambda i: (i, 0)),
                   pl.BlockSpec((tm, D), lambda i: (i, 0)),
                   pl.BlockSpec((8, D), lambda i: (0, 0))],
        out_shape=[SDS((t_len, D), F32), SDS((t_len, D), BF16), SDS((8, D), F32)],
        compiler_params=_cparams(("arbitrary",)), name=name)(dy, w, x, nw, dres)


def loss_head(x, fw, tgt):
    t_len = x.shape[0]
    tm = min(512, t_len)

    def body(x_ref, fw_ref, t_ref, loss_ref, dx_ref, dxb_ref, dfw_ref):
        @pl.when(pl.program_id(0) == 0)
        def _():
            loss_ref[...] = jnp.zeros_like(loss_ref)
            dfw_ref[...] = jnp.zeros_like(dfw_ref)
        xv = x_ref[...]
        r = lax.rsqrt(jnp.mean(xv * xv, axis=-1, keepdims=True) + EPS)
        xh = xv * r
        w = fw_ref[...]
        e = xh * w - t_ref[...]
        row = jnp.sum(e * e, axis=-1, keepdims=True) * (1.0 / D)
        loss_ref[...] += 0.5 * jnp.sum(row, axis=0, keepdims=True)
        dyf = e * (1.0 / D)
        dfw_ref[0:1, :] += jnp.sum(dyf * xh, axis=0, keepdims=True)
        g = dyf * w
        dx = r * (g - xh * jnp.mean(g * xh, axis=-1, keepdims=True))
        dx_ref[...] = dx
        dxb_ref[...] = dx.astype(BF16)

    return pl.pallas_call(
        body, grid=(t_len // tm,),
        in_specs=[pl.BlockSpec((tm, D), lambda i: (i, 0)),
                  pl.BlockSpec((1, D), lambda i: (0, 0)),
                  pl.BlockSpec((tm, D), lambda i: (i, 0))],
        out_specs=[pl.BlockSpec((8, LANES), lambda i: (0, 0)),
                   pl.BlockSpec((tm, D), lambda i: (i, 0)),
                   pl.BlockSpec((tm, D), lambda i: (i, 0)),
                   pl.BlockSpec((8, D), lambda i: (0, 0))],
        out_shape=[SDS((8, LANES), F32), SDS((t_len, D), F32), SDS((t_len, D), BF16), SDS((8, D), F32)],
        compiler_params=_cparams(("arbitrary",)), name="loss_head")(x, fw, tgt)


def _shift_dn(x, halo, j):
    if j == 0:
        return x
    xr = pltpu.roll(x, j, 0)
    hr = pltpu.roll(halo, j, 0)
    row = lax.broadcasted_iota(jnp.int32, hr.shape, 0)
    top = jnp.where(row < j, hr, xr[0:8])
    return jnp.concatenate([top, xr[8:]], axis=0)


def _shift_up(x, nxt, j):
    if j == 0:
        return x
    n = x.shape[0]
    xr = pltpu.roll(x, n - j, 0)
    hr = pltpu.roll(nxt, 8 - j, 0)
    row = lax.broadcasted_iota(jnp.int32, hr.shape, 0)
    bot = jnp.where(row >= 8 - j, hr, xr[n - 8:n])
    return jnp.concatenate([xr[:n - 8], bot], axis=0)


def _conv_fwd(x, halo, w_ref, kw):
    acc = None
    for k in range(kw):
        term = w_ref[k:k + 1, :] * _shift_dn(x, halo, kw - 1 - k)
        acc = term if acc is None else acc + term
    return acc


def _chunk_cumsum(a, pos):
    for sh in (1, 2, 4, 8, 16, 32):
        a = a + jnp.where(pos >= sh, pltpu.roll(a, sh, 0), 0.0)
    return a


def _chunk_rcumsum(a, pos):
    n = a.shape[0]
    for sh in (1, 2, 4, 8, 16, 32):
        a = a + jnp.where(pos < CHUNK - sh, pltpu.roll(a, n - sh, 0), 0.0)
    return a


def _softplus(v):
    return jnp.maximum(v, 0.0) + jnp.log(1.0 + jnp.exp(-jnp.abs(v)))


def _silu(v):
    return v * _sigmoid(v)


def _dsilu(v):
    s = _sigmoid(v)
    return s * (1.0 + v * (1.0 - s))


def _lane_masks(width=D):
    lane = lax.broadcasted_iota(jnp.int32, (CHUNK, width), 1) & (HDIM - 1)
    row = lax.broadcasted_iota(jnp.int32, (CHUNK, width), 0)
    return lane == row, lane <= row


def _rep_matrix():
    lane = lax.broadcasted_iota(jnp.int32, (CHUNK, 512), 1) & (HDIM - 1)
    row = lax.broadcasted_iota(jnp.int32, (CHUNK, 512), 0)
    return jnp.where(lane == row, 1.0, 0.0).astype(BF16)


def _blockdiag(xp):
    lane = lax.broadcasted_iota(jnp.int32, xp.shape, 1)
    zero = jnp.zeros_like(xp)
    return jnp.concatenate([jnp.where(lane < HDIM, xp, zero), jnp.where(lane >= HDIM, xp, zero)], axis=0)


def _mixer_views(tt):
    r8 = tt // 8

    def main(width, col):
        return pl.BlockSpec((tt, width), lambda i, c=col // width: (i, c))

    def halo(width, col):
        return pl.BlockSpec((8, width), lambda i, c=col // width: (jnp.maximum(i * r8 - 1, 0), c))

    return main, halo


def mixer_fwd(proj, prm, tt):
    t_len = proj.shape[0]
    nblk = t_len // tt
    nc = tt // CHUNK
    main, halo = _mixer_views(tt)

    def body(ub_ref, uc_ref, uh_ref, z_ref, xr_ref, bcr_ref, dtr_ref, uch_ref, uhh_ref, xrh_ref, bcrh_ref,
             scw_ref, cwx_ref, cwbc_ref, cbx_ref, cbbc_ref, dtb_ref, alog_ref, dsk_ref, nrm_ref, eh_ref,
             y_ref, st_ref, hs, xs_s, bc_s, dtx_s, cumx_s, yssd_s):
        i = pl.program_id(0)
        first = i == 0

        @pl.when(first)
        def _():
            hs[...] = jnp.zeros_like(hs)

        keep = jnp.where(first, 0.0, 1.0)
        v = uc_ref[...].astype(F32) * uh_ref[...].astype(F32)
        vh = uch_ref[...].astype(F32) * uhh_ref[...].astype(F32) * keep
        y_ref[:, 0:D] = (ub_ref[...].astype(F32) * _conv_fwd(v, vh, scw_ref, 3)).astype(BF16)

        xs = _silu(_conv_fwd(xr_ref[...].astype(F32), xrh_ref[...].astype(F32) * keep, cwx_ref, 4) + cbx_ref[...])
        xs_s[...] = xs
        bc_s[...] = _silu(_conv_fwd(bcr_ref[...].astype(F32), bcrh_ref[...].astype(F32) * keep, cwbc_ref, 4)
                          + cbbc_ref[...])
        dt = _softplus(dtr_ref[...].astype(F32) + dtb_ref[...])
        a_neg = -jnp.exp(alog_ref[...])
        pos = lax.broadcasted_iota(jnp.int32, (tt, LANES), 0) & (CHUNK - 1)
        cum = _chunk_cumsum(dt * a_neg, pos)
        eh = eh_ref[...]
        dtx_s[...] = _expand(dt, eh)
        cumx_s[...] = _expand(cum, eh)
        irep, causal = _lane_masks()
        rep = _rep_matrix()

        def chunk(c, carry):
            r0 = pl.multiple_of(c * CHUNK, CHUNK)
            rows = pl.ds(r0, CHUNK)
            cumx = cumx_s[rows, :]
            cum_l = cumx[CHUNK - 1:CHUNK, :]
            xd = xs_s[rows, :] * dtx_s[rows, :]
            xf = xd * jnp.exp(cum_l - cumx)
            ex = jnp.exp(cumx)
            e_l = jnp.exp(cum_l)
            rvec = jnp.sum(jnp.where(irep, cumx, 0.0), axis=0, keepdims=True)
            lam = jnp.where(causal, jnp.exp(jnp.where(causal, cumx - rvec, 0.0)), 0.0)
            bc = bc_s[rows, :]
            for g in range(2):
                gs = slice(g * 512, (g + 1) * 512)
                bg = bc[:, g * NSTATE:(g + 1) * NSTATE].astype(BF16)
                cg = bc[:, 256 + g * NSTATE:256 + (g + 1) * NSTATE].astype(BF16)
                s_rep = _nn(_nt(cg, bg).astype(BF16), rep)
                m_g = (s_rep * lam[:, gs]).astype(BF16)
                h_g = hs[:, gs]
                h_b = h_g.astype(BF16)
                st_ref[c, :, gs] = h_b
                yo = _nn(cg, h_b) * ex[:, gs]
                xd_b = xd[:, gs].astype(BF16)
                for hp in range(4):
                    ps = slice(hp * LANES, (hp + 1) * LANES)
                    yd = _nn(m_g[:, ps], _blockdiag(xd_b[:, ps]))
                    yssd_s[rows, g * 512 + hp * LANES:g * 512 + (hp + 1) * LANES] = yd + yo[:, ps]
                hs[:, gs] = h_g * e_l[:, gs] + _tn(bg, xf[:, gs].astype(BF16))
            return carry

        lax.fori_loop(0, nc, chunk, 0)

        ys = yssd_s[...] + dsk_ref[...] * xs_s[...]
        gt = ys * _silu(z_ref[...].astype(F32))
        for g in range(2):
            gs = slice(g * 512, (g + 1) * 512)
            gg = gt[:, gs]
            rn = lax.rsqrt(jnp.mean(gg * gg, axis=-1, keepdims=True) + EPS)
            y_ref[:, D + g * 512:D + (g + 1) * 512] = (gg * rn * nrm_ref[:, gs]).astype(BF16)

    def const(shape):
        return pl.BlockSpec(shape, lambda i: (0, 0))

    in_specs = [main(D, C_UB), main(D, C_UC), main(D, C_UH), main(D, C_Z), main(D, C_XS), main(512, C_BC),
                main(LANES, C_DT), halo(D, C_UC), halo(D, C_UH), halo(D, C_XS), halo(512, C_BC),
                const((8, D)), const((8, D)), const((8, 512)), const((1, D)), const((1, 512)),
                const((1, LANES)), const((1, LANES)), const((1, D)), const((1, D)), const((LANES, D))]
    return pl.pallas_call(
        body, grid=(nblk,),
        in_specs=in_specs,
        out_specs=[pl.BlockSpec((tt, MIX), lambda i: (i, 0)),
                   pl.BlockSpec((nc, NSTATE, D), lambda i: (i, 0, 0))],
        out_shape=[SDS((t_len, MIX), BF16), SDS((t_len // CHUNK, NSTATE, D), BF16)],
        scratch_shapes=[pltpu.VMEM((NSTATE, D), F32), pltpu.VMEM((tt, D), F32), pltpu.VMEM((tt, 512), F32),
                        pltpu.VMEM((tt, D), F32), pltpu.VMEM((tt, D), F32), pltpu.VMEM((tt, D), F32)],
        compiler_params=_cparams(("arbitrary",)), name="mixer_fwd")(
            *([proj] * 11), prm["scw"], prm["cwx"], prm["cwbc"], prm["cbx"], prm["cbbc"], prm["dtb"],
            prm["alog"], prm["dskx"], prm["nrm"], prm["eh"])


def mixer_bwd(proj, dy, states, prm, tt):
    t_len = proj.shape[0]
    nblk = t_len // tt
    nc = tt // CHUNK
    r8 = tt // 8

    def rev(i):
        return nblk - 1 - i

    def main(width, col):
        return pl.BlockSpec((tt, width), lambda i, c=col // width: (rev(i), c))

    def halo(width, col):
        return pl.BlockSpec((8, width), lambda i, c=col // width: (jnp.maximum(rev(i) * r8 - 1, 0), c))

    def body(ub_ref, uc_ref, uh_ref, z_ref, xr_ref, bcr_ref, dtr_ref, uch_ref, uhh_ref, xrh_ref, bcrh_ref,
             dy_ref, st_ref,
             scw_ref, cwx_ref, cwbc_ref, cbx_ref, cbbc_ref, dtb_ref, alog_ref, dsk_ref, nrm_ref, eh_ref, eht_ref,
             dp_ref, gscw_ref, gcwx_ref, gcwbc_ref, gvec_ref, gdt_ref,
             dhs, xs_s, bc_s, dtx_s, cumx_s, dys_s, dxs_s, dbc_s, red_s, ddtx_s, nx_cv, nx_px, nx_pbc):
        i = pl.program_id(0)
        blk = rev(i)

        @pl.when(i == 0)
        def _():
            dhs[...] = jnp.zeros_like(dhs)
            nx_cv[...] = jnp.zeros_like(nx_cv)
            nx_px[...] = jnp.zeros_like(nx_px)
            nx_pbc[...] = jnp.zeros_like(nx_pbc)
            gscw_ref[...] = jnp.zeros_like(gscw_ref)
            gcwx_ref[...] = jnp.zeros_like(gcwx_ref)
            gcwbc_ref[...] = jnp.zeros_like(gcwbc_ref)
            gvec_ref[...] = jnp.zeros_like(gvec_ref)
            gdt_ref[...] = jnp.zeros_like(gdt_ref)

        keep = jnp.where(blk == 0, 0.0, 1.0)

        ub = ub_ref[...].astype(F32)
        uc = uc_ref[...].astype(F32)
        uh = uh_ref[...].astype(F32)
        v = uc * uh
        vh = uch_ref[...].astype(F32) * uhh_ref[...].astype(F32) * keep
        dya = dy_ref[:, 0:D].astype(F32)
        dp_ref[:, C_UB:C_UB + D] = (dya * _conv_fwd(v, vh, scw_ref, 3)).astype(BF16)
        dcv = dya * ub
        nxt = nx_cv[...]
        dv = None
        for k in range(3):
            gscw_ref[k:k + 1, :] += jnp.sum(dcv * _shift_dn(v, vh, 2 - k), axis=0, keepdims=True)
            term = scw_ref[k:k + 1, :] * _shift_up(dcv, nxt, 2 - k)
            dv = term if dv is None else dv + term
        nx_cv[...] = dcv[0:8]
        dp_ref[:, C_UC:C_UC + D] = (dv * uh).astype(BF16)
        dp_ref[:, C_UH:C_UH + D] = (dv * uc).astype(BF16)

        xraw = xr_ref[...].astype(F32)
        xrh = xrh_ref[...].astype(F32) * keep
        bcraw = bcr_ref[...].astype(F32)
        bcrh = bcrh_ref[...].astype(F32) * keep
        pre_x = _conv_fwd(xraw, xrh, cwx_ref, 4) + cbx_ref[...]
        pre_bc = _conv_fwd(bcraw, bcrh, cwbc_ref, 4) + cbbc_ref[...]
        xs = _silu(pre_x)
        xs_s[...] = xs
        bc_s[...] = _silu(pre_bc)
        dt_pre = dtr_ref[...].astype(F32) + dtb_ref[...]
        dt = _softplus(dt_pre)
        a_neg = -jnp.exp(alog_ref[...])
        pos = lax.broadcasted_iota(jnp.int32, (tt, LANES), 0) & (CHUNK - 1)
        cum = _chunk_cumsum(dt * a_neg, pos)
        eh = eh_ref[...]
        eht = eht_ref[...]
        dtx_s[...] = _expand(dt, eh)
        cumx_s[...] = _expand(cum, eh)

        irep, causal = _lane_masks()
        irep_g, _ = _lane_masks(512)
        rep = _rep_matrix()
        row64 = lax.broadcasted_iota(jnp.int32, (CHUNK, 512), 0)
        lane128 = lax.broadcasted_iota(jnp.int32, (CHUNK, LANES), 1)

        def fwd_chunk(c, carry):
            r0 = pl.multiple_of(c * CHUNK, CHUNK)
            rows = pl.ds(r0, CHUNK)
            cumx = cumx_s[rows, :]
            xd = xs_s[rows, :] * dtx_s[rows, :]
            ex = jnp.exp(cumx)
            rvec = jnp.sum(jnp.where(irep, cumx, 0.0), axis=0, keepdims=True)
            lam = jnp.where(causal, jnp.exp(jnp.where(causal, cumx - rvec, 0.0)), 0.0)
            bc = bc_s[rows, :]
            for g in range(2):
                gs = slice(g * 512, (g + 1) * 512)
                bg = bc[:, g * NSTATE:(g + 1) * NSTATE].astype(BF16)
                cg = bc[:, 256 + g * NSTATE:256 + (g + 1) * NSTATE].astype(BF16)
                s_rep = _nn(_nt(cg, bg).astype(BF16), rep)
                m_g = (s_rep * lam[:, gs]).astype(BF16)
                yo = _nn(cg, st_ref[c, :, gs]) * ex[:, gs]
                xd_b = xd[:, gs].astype(BF16)
                for hp in range(4):
                    ps = slice(hp * LANES, (hp + 1) * LANES)
                    yd = _nn(m_g[:, ps], _blockdiag(xd_b[:, ps]))
                    dys_s[rows, g * 512 + hp * LANES:g * 512 + (hp + 1) * LANES] = yd + yo[:, ps]
            return carry

        lax.fori_loop(0, nc, fwd_chunk, 0)

        z = z_ref[...].astype(F32)
        sz = _silu(z)
        ys = dys_s[...] + dsk_ref[...] * xs
        gt = ys * sz
        dyb = dy_ref[:, D:MIX].astype(F32)
        for g in range(2):
            gs = slice(g * 512, (g + 1) * 512)
            gg = gt[:, gs]
            rn = lax.rsqrt(jnp.mean(gg * gg, axis=-1, keepdims=True) + EPS)
            gvec_ref[0:1, gs] += jnp.sum(dyb[:, gs] * gg * rn, axis=0, keepdims=True)
            dgn = dyb[:, gs] * nrm_ref[:, gs]
            dgt = rn * (dgn - gg * (rn * rn) * jnp.mean(dgn * gg, axis=-1, keepdims=True))
            dys = dgt * sz[:, gs]
            dys_s[:, gs] = dys
            dp_ref[:, C_Z + g * 512:C_Z + (g + 1) * 512] = (dgt * ys[:, gs] * _dsilu(z[:, gs])).astype(BF16)
        dys_all = dys_s[...]
        gvec_ref[1:2, :] += jnp.sum(dys_all * xs, axis=0, keepdims=True)

        def bwd_chunk(cc, carry):
            c = nc - 1 - cc
            r0 = pl.multiple_of(c * CHUNK, CHUNK)
            rows = pl.ds(r0, CHUNK)
            cumx = cumx_s[rows, :]
            cum_l = cumx[CHUNK - 1:CHUNK, :]
            xs_c = xs_s[rows, :]
            dtx = dtx_s[rows, :]
            xd = xs_c * dtx
            f = jnp.exp(cum_l - cumx)
            xf = xd * f
            ex = jnp.exp(cumx)
            e_l = jnp.exp(cum_l)
            rvec = jnp.sum(jnp.where(irep, cumx, 0.0), axis=0, keepdims=True)
            lam = jnp.where(causal, jnp.exp(jnp.where(causal, cumx - rvec, 0.0)), 0.0)
            bc = bc_s[rows, :]
            dyc = dys_s[rows, :]
            for g in range(2):
                gs = slice(g * 512, (g + 1) * 512)
                bg = bc[:, g * NSTATE:(g + 1) * NSTATE].astype(BF16)
                cg = bc[:, 256 + g * NSTATE:256 + (g + 1) * NSTATE].astype(BF16)
                h0 = st_ref[c, :, gs]
                dh = dhs[:, gs]
                dh_b = dh.astype(BF16)
                xf_g = xf[:, gs]
                dxf = _nn(bg, dh_b)
                db = _nt(xf_g.astype(BF16), dh_b)
                s_rep = _nn(_nt(cg, bg).astype(BF16), rep)
                lam_g = lam[:, gs]
                m_g = s_rep * lam_g
                m_b = m_g.astype(BF16)
                ex_g = ex[:, gs]
                dy_g = dyc[:, gs]
                yo = _nn(cg, h0) * ex_g
                dg_b = (dy_g * ex_g).astype(BF16)
                dc = _nt(dg_b, h0)
                el_g = e_l[:, gs]
                dee = jnp.sum(dh * h0.astype(F32), axis=0, keepdims=True) * el_g
                dhs[:, gs] = dh * el_g + _tn(cg, dg_b)
                xd_b = xd[:, gs].astype(BF16)
                dy_b = dy_g.astype(BF16)
                dm_parts, dxd_parts = [], []
                for hp in range(4):
                    ps = slice(hp * LANES, (hp + 1) * LANES)
                    bd = _blockdiag(xd_b[:, ps])
                    dm_parts.append(_nt(dy_b[:, ps], bd))
                    t2 = _tn(m_b[:, ps], dy_b[:, ps])
                    dxd_parts.append(jnp.where(lane128 < HDIM, t2[0:CHUNK], t2[CHUNK:2 * CHUNK]))
                dm = jnp.concatenate(dm_parts, axis=1)
                dxd = jnp.concatenate(dxd_parts, axis=1) + dxf * f[:, gs]
                dseg = dm * m_g
                ds_b = _nt((dm * lam_g).astype(BF16), rep).astype(BF16)
                dc = dc + _nn(ds_b, bg)
                db = db + _tn(ds_b, cg)
                colsum = jnp.sum(dseg, axis=0, keepdims=True)
                dxfxf = dxf * xf_g
                red = dseg - jnp.where(irep_g, colsum, 0.0) + dy_g * yo - dxfxf
                last = jnp.sum(dxfxf, axis=0, keepdims=True) + dee
                red = red + jnp.where(row64 == CHUNK - 1, last, 0.0)
                red_s[rows, gs] = red
                ddtx_s[rows, gs] = dxd * xs_c[:, gs]
                dxs_s[rows, gs] = dxd * dtx[:, gs] + dsk_ref[:, gs] * dy_g
                dbc_s[rows, g * NSTATE:(g + 1) * NSTATE] = db
                dbc_s[rows, 256 + g * NSTATE:256 + (g + 1) * NSTATE] = dc
            return carry

        lax.fori_loop(0, nc, bwd_chunk, 0)

        dcum = _head_reduce(red_s[...], eht)
        da = _chunk_rcumsum(dcum, pos)
        ddt = _head_reduce(ddtx_s[...], eht) + da * a_neg
        gdt_ref[1:2, :] += jnp.sum(da * dt, axis=0, keepdims=True) * a_neg
        ddt_raw = ddt * _sigmoid(dt_pre)
        lane_t = lax.broadcasted_iota(jnp.int32, (tt, LANES), 1)
        ddt_raw = jnp.where(lane_t < NHEAD, ddt_raw, 0.0)
        gdt_ref[0:1, :] += jnp.sum(ddt_raw, axis=0, keepdims=True)
        dp_ref[:, C_DT:C_DT + LANES] = ddt_raw.astype(BF16)

        dpx = dxs_s[...] * _dsilu(pre_x)
        dpbc = dbc_s[...] * _dsilu(pre_bc)
        gvec_ref[2:3, :] += jnp.sum(dpx, axis=0, keepdims=True)
        gcwbc_ref[4:5, :] += jnp.sum(dpbc, axis=0, keepdims=True)
        nxt_x = nx_px[...]
        nxt_bc = nx_pbc[...]
        dxr, dbcr = None, None
        for k in range(4):
            gcwx_ref[k:k + 1, :] += jnp.sum(dpx * _shift_dn(xraw, xrh, 3 - k), axis=0, keepdims=True)
            gcwbc_ref[k:k + 1, :] += jnp.sum(dpbc * _shift_dn(bcraw, bcrh, 3 - k), axis=0, keepdims=True)
            tx = cwx_ref[k:k + 1, :] * _shift_up(dpx, nxt_x, 3 - k)
            tb = cwbc_ref[k:k + 1, :] * _shift_up(dpbc, nxt_bc, 3 - k)
            dxr = tx if dxr is None else dxr + tx
            dbcr = tb if dbcr is None else dbcr + tb
        nx_px[...] = dpx[0:8]
        nx_pbc[...] = dpbc[0:8]
        dp_ref[:, C_XS:C_XS + D] = dxr.astype(BF16)
        dp_ref[:, C_BC:C_BC + 512] = dbcr.astype(BF16)

        @pl.when(i == nblk - 1)
        def _():
            gdt_ref[2:3, :] = _head_reduce(gvec_ref[1:2, :] * jnp.ones((8, 1), F32), eht)[0:1, :]

    def const(shape):
        return pl.BlockSpec(shape, lambda i: (0, 0))

    in_specs = [main(D, C_UB), main(D, C_UC), main(D, C_UH), main(D, C_Z), main(D, C_XS), main(512, C_BC),
                main(LANES, C_DT), halo(D, C_UC), halo(D, C_UH), halo(D, C_XS), halo(512, C_BC),
                pl.BlockSpec((tt, MIX), lambda i: (rev(i), 0)),
                pl.BlockSpec((nc, NSTATE, D), lambda i: (rev(i), 0, 0)),
                const((8, D)), const((8, D)), const((8, 512)), const((1, D)), const((1, 512)),
                const((1, LANES)), const((1, LANES)), const((1, D)), const((1, D)), const((LANES, D)),
                const((D, LANES))]
    return pl.pallas_call(
        body, grid=(nblk,),
        in_specs=in_specs,
        out_specs=[pl.BlockSpec((tt, NINP), lambda i: (rev(i), 0)),
                   const((8, D)), const((8, D)), const((8, 512)), const((8, D)), const((8, LANES))],
        out_shape=[SDS((t_len, NINP), BF16), SDS((8, D), F32), SDS((8, D), F32), SDS((8, 512), F32),
                   SDS((8, D), F32), SDS((8, LANES), F32)],
        scratch_shapes=[pltpu.VMEM((NSTATE, D), F32),
                        pltpu.VMEM((tt, D), F32), pltpu.VMEM((tt, 512), F32),
                        pltpu.VMEM((tt, D), F32), pltpu.VMEM((tt, D), F32),
                        pltpu.VMEM((tt, D), F32), pltpu.VMEM((tt, D), F32),
                        pltpu.VMEM((tt, 512), F32),
                        pltpu.VMEM((tt, D), F32), pltpu.VMEM((tt, D), F32),
                        pltpu.VMEM((8, D), F32), pltpu.VMEM((8, D), F32), pltpu.VMEM((8, 512), F32)],
        compiler_params=_cparams(("arbitrary",)), name="mixer_bwd")(
            *([proj] * 11), dy, states, prm["scw"], prm["cwx"], prm["cwbc"], prm["cbx"], prm["cbbc"], prm["dtb"],
            prm["alog"], prm["dskx"], prm["nrm"], prm["eh"], prm["eht"])


TN_IN = 1920


def layer_fwd_mix(x, lw, prm, tt):
    proj, h1 = norm_matmul(x, lw["nw1"], lw["win"], "in_proj")
    y, st = mixer_fwd(proj, prm, tt)
    return h1, proj, st, y


def layer_fwd_mlp(x, mixed, lw):
    h1, proj, st, y = mixed
    x1 = matmul_residual(y, lw["wout"], x, False, "out_proj")
    u, h2 = norm_matmul(x1, lw["nw2"], lw["wup"], "up_proj")
    x2 = matmul_residual(u, lw["wdn"], x1, True, "down_proj")
    return x2, (x, h1, proj, st, y, x1, h2, u)


def layer_fwd(x, lw, prm, tt):
    return layer_fwd_mlp(x, layer_fwd_mix(x, lw, prm, tt), lw)


def _dw(a, b, a_cols, b_cols, relu2, name):
    m_len, n_len = a.shape[1], b.shape[1]
    n_a, n_b = m_len // a_cols, n_len // b_cols
    assert n_a == 1 or n_b == 1
    if n_b == 1:
        return matmul_tn(
            a, b,
            lambda t_: pl.BlockSpec((t_, a_cols), lambda n, t: (t, n)),
            lambda t_: pl.BlockSpec((t_, n_len), lambda n, t: (t, 0)),
            pl.BlockSpec((a_cols, n_len), lambda n, t: (n, 0)), SDS((m_len, n_len), BF16), n_a, relu2, name)
    return matmul_tn(
        a, b,
        lambda t_: pl.BlockSpec((t_, m_len), lambda n, t: (t, 0)),
        lambda t_: pl.BlockSpec((t_, b_cols), lambda n, t: (t, n)),
        pl.BlockSpec((m_len, b_cols), lambda n, t: (0, n)), SDS((m_len, n_len), BF16), n_b, relu2, name)


def layer_bwd_mlp(dx2, dx2b, lw, saved):
    _, _, _, _, y, x1, h2, u = saved
    du = matmul_nt_act(dx2b, lw["wdn"], u, "mlp_bwd_du")
    g_wdn = _dw(u, dx2b, 1024, D, True, "dw_down")
    dx1, dx1b, g_nw2 = matmul_nt_norm_bwd(du, lw["wup"], x1, lw["nw2"], dx2, "mlp_bwd_dx")
    cb = DFF // N_DEV
    g_wup = matmul_tn(
        h2, du,
        lambda t_: pl.BlockSpec((t_, D), lambda n, t: (t, 0)),
        lambda t_: pl.BlockSpec((t_, 2 * cb), lambda n, t: (t, n)),
        pl.BlockSpec((2, D, cb), lambda n, t: (n, 0, 0)), SDS((N_DEV, D, cb), BF16), N_DEV // 2, False, "dw_up")
    dy = matmul_nt_act(dx1b, lw["wout"], None, "out_bwd_dy")
    g_wout = _dw(y, dx1b, 1024, D, False, "dw_out")
    return dx1, dx1b, dy, {"wout": g_wout, "wup": g_wup, "wdn": g_wdn, "nw2": g_nw2[0]}


def layer_bwd_mix(dx1, dy, lw, prm, saved, tt):
    x, h1, proj, st = saved[:4]
    dproj, gscw, gcwx, gcwbc, gvec, gdt = mixer_bwd(proj, dy, st, prm, tt)
    dx0, dx0b, g_nw1 = matmul_nt_norm_bwd(dproj, lw["win"], x, lw["nw1"], dx1, "in_bwd_dx")
    g_win = _dw(h1, dproj, D, TN_IN, False, "dw_in")
    grads = {
        "win": g_win, "scw": gscw[0:3], "cw": jnp.concatenate([gcwx[0:4], gcwbc[0:4]], axis=1),
        "cb": jnp.concatenate([gvec[2], gcwbc[4]], axis=0),
        "dtb": gdt[0, :NHEAD], "alog": gdt[1, :NHEAD], "dsk": gdt[2, :NHEAD],
        "nrm": gvec[0], "nw1": g_nw1[0],
    }
    return dx0, dx0b, grads


def layer_bwd(dx2, dx2b, lw, prm, saved, tt):
    dx1, dx1b, dy, g_mlp = layer_bwd_mlp(dx2, dx2b, lw, saved)
    dx0, dx0b, g_mix = layer_bwd_mix(dx1, dy, lw, prm, saved, tt)
    return dx0, dx0b, {**g_mlp, **g_mix}


def layer_params(win, scw, cw, nw1, nw2, conv_b, dt_bias, a_log, d_skip, ssd_norm_w, eh, eht):
    def rows8(a):
        return jnp.pad(a, ((0, 8 - a.shape[0]), (0, 0)))

    def lanes128(a):
        return jnp.pad(a, (0, LANES - a.shape[0]))[None, :]

    lw = {"win": win, "nw1": nw1[None, :], "nw2": nw2[None, :]}
    prm = {"scw": rows8(scw), "cwx": rows8(cw[:, :D]), "cwbc": rows8(cw[:, D:]),
           "cbx": conv_b[None, :D], "cbbc": conv_b[None, D:],
           "dtb": lanes128(dt_bias), "alog": lanes128(a_log),
           "dskx": jnp.repeat(d_skip, HDIM)[None, :], "nrm": ssd_norm_w[None, :], "eh": eh, "eht": eht}
    return lw, prm


def _flip(v, bit):
    return 1 - v if bit else v


def all_gather(arrs, name):
    n = len(arrs)

    def body(*refs):
        ins, outs = refs[:n], refs[n:2 * n]
        send_sems, recv_sems, local_sems = refs[2 * n:]
        x, y, c = lax.axis_index("x"), lax.axis_index("y"), lax.axis_index("c")
        sibling = (x, y, 1 - c)
        chips = [(1 - x, y), (x, 1 - y), (1 - x, 1 - y)]

        def idx(px, py, pc):
            return 4 * px + 2 * py + pc

        def copy(a, k, block, to, src=None):
            dst = outs[a].at[idx(*block)]
            return pltpu.make_async_remote_copy(
                src_ref=dst if src is None else src, dst_ref=dst,
                send_sem=send_sems.at[a, k], recv_sem=recv_sems.at[a, k], device_id=to, device_id_type=MESH)

        me = (x, y, c)
        mine = [pltpu.make_async_copy(ins[a], outs[a].at[idx(*me)], local_sems.at[a]) for a in range(n)]
        for cp in mine:
            cp.start()
        first = []
        for a in range(n):
            first.append(copy(a, 0, me, sibling, src=ins[a]))
            first += [copy(a, 1 + j, me, (*chip, c), src=ins[a]) for j, chip in enumerate(chips)]
        for cp in first:
            cp.start()
        passed = []
        for j, chip in enumerate(chips):
            for a in range(n):
                copy(a, 1 + j, (*chip, c), me).wait_recv()
                cp = copy(a, 4 + j, (*chip, c), sibling)
                cp.start()
                passed.append(cp)
        for a in range(n):
            copy(a, 0, sibling, me).wait_recv()
            for j, chip in enumerate(chips):
                copy(a, 4 + j, (*chip, 1 - c), me).wait_recv()
        for cp in first + passed:
            cp.wait_send()
        for cp in mine:
            cp.wait()

    any_spec = pl.BlockSpec(memory_space=pl.ANY)
    return pl.pallas_call(
        body, in_specs=[any_spec] * n, out_specs=[any_spec] * n,
        out_shape=[SDS((N_DEV,) + a.shape, a.dtype) for a in arrs],
        scratch_shapes=[pltpu.SemaphoreType.DMA((n, 7)), pltpu.SemaphoreType.DMA((n, 7)),
                        pltpu.SemaphoreType.DMA((n,))],
        name=name)(*arrs)


HBM_SPEC = pl.BlockSpec(memory_space=pltpu.HBM)
SEM_SPEC = pl.BlockSpec(memory_space=pltpu.SEMAPHORE)
SIDE_EFFECT = pltpu.SideEffectType.DATAFLOW_SIDE_EFFECTING
N_PEER = N_DEV - 1


def _peer(mask):
    x, y, c = lax.axis_index("x"), lax.axis_index("y"), lax.axis_index("c")
    return _flip(x, mask & 4), _flip(y, mask & 2), _flip(c, mask & 1)


def exchange_start(srcs, per_peer, name, after=None):
    n = len(srcs)
    lands = [SDS((N_DEV,) + (a.shape[1:] if per_peer else a.shape), a.dtype) for a in srcs]
    n_in = 2 * n + (after is not None)

    def body(*refs):
        src_refs, land_refs = refs[:n], refs[n:2 * n]
        send_sems, recv_sems = refs[n_in], refs[n_in + 1]
        token = refs[-1]
        x, y, c = lax.axis_index("x"), lax.axis_index("y"), lax.axis_index("c")
        me = 4 * x + 2 * y + c
        for a in range(n):
            for mask in range(1, N_DEV):
                px, py, pc = _peer(mask)
                part = src_refs[a].at[4 * px + 2 * py + pc] if per_peer else src_refs[a]
                pltpu.make_async_remote_copy(
                    src_ref=part, dst_ref=land_refs[a].at[me], send_sem=send_sems.at[a * N_PEER + mask - 1],
                    recv_sem=recv_sems.at[a * N_PEER + mask - 1], device_id=(px, py, pc),
                    device_id_type=MESH).start()
        token[...] = jnp.zeros_like(token)

    out = pl.pallas_call(
        body, name=name,
        out_shape=(pltpu.SemaphoreType.DMA((n * N_PEER,)), pltpu.SemaphoreType.DMA((n * N_PEER,)),
                   *[pltpu.HBM(a.shape, a.dtype) for a in srcs], *[pltpu.HBM(l.shape, l.dtype) for l in lands],
                   SDS((8, LANES), F32)),
        in_specs=(HBM_SPEC,) * (2 * n) + ((pl.BlockSpec(memory_space=pl.ANY),) if after is not None else ()),
        out_specs=(SEM_SPEC, SEM_SPEC) + (HBM_SPEC,) * (2 * n) + (pl.BlockSpec(memory_space=pltpu.VMEM),),
        input_output_aliases={k: 2 + k for k in range(2 * n)},
        compiler_params=pltpu.CompilerParams(has_side_effects=SIDE_EFFECT),
    )(*[pltpu.with_memory_space_constraint(a, pltpu.HBM) for a in srcs],
      *[pltpu.with_memory_space_constraint(lax.empty(l.shape, l.dtype), pltpu.HBM) for l in lands],
      *([after] if after is not None else []))
    return out[0], out[1], list(out[2:2 + n]), list(out[2 + n:2 + 2 * n]), out[-1]


def exchange_wait(started, after, per_peer, name):
    send_sems, recv_sems, srcs, lands, _ = started
    n = len(srcs)

    def body(*refs):
        src_refs, land_refs = refs[:n], refs[n:2 * n]
        send_sems, recv_sems = refs[2 * n], refs[2 * n + 1]
        for mask in range(1, N_DEV):
            for a in range(n):
                copy = pltpu.make_async_remote_copy(
                    src_ref=src_refs[a].at[0] if per_peer else src_refs[a], dst_ref=land_refs[a].at[0],
                    send_sem=send_sems.at[a * N_PEER + mask - 1], recv_sem=recv_sems.at[a * N_PEER + mask - 1],
                    device_id=_peer(mask), device_id_type=MESH)
                copy.wait_send()
                copy.wait_recv()

    out = pl.pallas_call(
        body, name=name,
        out_shape=tuple(pltpu.HBM(a.shape, a.dtype) for a in srcs + lands),
        in_specs=(HBM_SPEC,) * (2 * n) + (SEM_SPEC, SEM_SPEC, pl.BlockSpec(memory_space=pl.ANY)),
        out_specs=(HBM_SPEC,) * (2 * n), input_output_aliases={k: k for k in range(2 * n)},
        compiler_params=pltpu.CompilerParams(has_side_effects=SIDE_EFFECT),
    )(*srcs, *lands, send_sems, recv_sems, after)
    return list(out[:n]), list(out[n:])


IN_SHARD = NIN // N_DEV
SLOT_W = 768


def _slot_window(j):
    return (IN_SHARD * j // LANES) * LANES, -(-(IN_SHARD * (j + 1)) // LANES) * LANES


def _placement(j):
    a, b = _slot_window(j)
    r = lax.broadcasted_iota(jnp.int32, (SLOT_W, b - a), 0)
    c = lax.broadcasted_iota(jnp.int32, (SLOT_W, b - a), 1)
    return jnp.where(jnp.logical_and(c == r + (IN_SHARD * j - a), r < IN_SHARD), 1.0, 0.0).astype(BF16)


def assemble_w_in(land):
    tm = 256

    def body(l_ref, o_ref, acc):
        acc[...] = jnp.zeros_like(acc)
        for j in range(N_DEV):
            a, b = _slot_window(j)
            acc[:, a:b] += _nn(l_ref[j], _placement(j))
        o_ref[...] = acc[...].astype(BF16)

    return pl.pallas_call(
        body, grid=(D // tm,),
        in_specs=[pl.BlockSpec((N_DEV, tm, SLOT_W), lambda i: (0, i, 0))],
        out_specs=pl.BlockSpec((tm, NINP), lambda i: (i, 0)),
        out_shape=SDS((D, NINP), BF16),
        scratch_shapes=[pltpu.VMEM((tm, NINP), F32)],
        compiler_params=_cparams(("parallel",)), name="assemble_w_in")(land)


def scatter_w_in(dw):
    tm = 256

    def body(d_ref, o_ref):
        for j in range(N_DEV):
            a, b = _slot_window(j)
            o_ref[j] = _nt(d_ref[:, a:b], _placement(j)).astype(BF16)

    return pl.pallas_call(
        body, grid=(D // tm,),
        in_specs=[pl.BlockSpec((tm, NINP), lambda i: (i, 0))],
        out_specs=pl.BlockSpec((N_DEV, tm, SLOT_W), lambda i: (0, i, 0)),
        out_shape=SDS((N_DEV, D, SLOT_W), BF16),
        compiler_params=_cparams(("parallel",)), name="scatter_w_in")(dw)


def _adamw_math(g, w_ref, m_ref, v_ref, g_ref, d_ref, nm_ref, nv_ref):
    mn = ADAM_B1 * m_ref[...] + (1.0 - ADAM_B1) * g
    vn = ADAM_B2 * v_ref[...] + (1.0 - ADAM_B2) * jnp.square(g)
    m_hat = mn / (1.0 - ADAM_B1 ** ADAM_STEP)
    v_hat = vn / (1.0 - ADAM_B2 ** ADAM_STEP)
    g_ref[...] = g
    d_ref[...] = -ADAM_LR * (m_hat / (jnp.sqrt(v_hat) + ADAM_EPS) + ADAM_WD * w_ref[...])
    nm_ref[...] = mn
    nv_ref[...] = vn


def adamw_layers(w, slots, m, v, name):
    depth, r_len, c_len = w.shape
    cs = slots[0].shape[2]
    br = min(128, r_len)
    assert r_len % br == 0

    def body(w_ref, *rest):
        s_refs, (m_ref, v_ref, g_ref, d_ref, nm_ref, nv_ref) = rest[:depth], rest[depth:]
        layer = pl.program_id(0)
        for k in range(depth):
            @pl.when(layer == k)
            def _(k=k):
                g = s_refs[k][0, :, 0:c_len].astype(F32)
                for j in range(1, N_DEV):
                    g = g + s_refs[k][j, :, 0:c_len].astype(F32)
                _adamw_math(g, w_ref, m_ref, v_ref, g_ref, d_ref, nm_ref, nv_ref)

    spec = pl.BlockSpec((None, br, c_len), lambda l, i: (l, i, 0))
    s_specs = [pl.BlockSpec((N_DEV, br, cs), lambda l, i, k=k: (0, jnp.where(l == k, i, 0), 0))
               for k in range(depth)]
    return pl.pallas_call(
        body, grid=(depth, r_len // br),
        in_specs=[spec] + s_specs + [spec, spec],
        out_specs=[spec] * 4, out_shape=[SDS(w.shape, F32)] * 4,
        compiler_params=_cparams(("arbitrary", "arbitrary")), name=name)(w, *slots, m, v)


def adamw(w, slots, m, v, name):
    r_len, c_len = w.shape
    br = r_len if r_len <= 512 else 512
    assert r_len % br == 0

    def body(w_ref, s_ref, m_ref, v_ref, g_ref, d_ref, nm_ref, nv_ref):
        g = s_ref[0].astype(F32)
        for k in range(1, N_DEV):
            g = g + s_ref[k].astype(F32)
        _adamw_math(g, w_ref, m_ref, v_ref, g_ref, d_ref, nm_ref, nv_ref)

    spec = pl.BlockSpec((br, c_len), lambda i: (i, 0))
    return pl.pallas_call(
        body, grid=(r_len // br,),
        in_specs=[spec, pl.BlockSpec((N_DEV, br, c_len), lambda i: (0, i, 0)), spec, spec],
        out_specs=[spec] * 4, out_shape=[SDS((r_len, c_len), F32)] * 4,
        compiler_params=_cparams(("parallel",)), name=name)(w, slots, m, v)


def _adamw_nd(w, slots, m, v, name):
    shp = w.shape
    r = int(np.prod(shp[:-1]))
    outs = adamw(w.reshape(r, shp[-1]), slots.reshape(N_DEV, r, shp[-1]), m.reshape(r, shp[-1]),
                 v.reshape(r, shp[-1]), name)
    return [o.reshape(shp) for o in outs]


SMALL = [("norm_mix_w", DEPTH * D), ("ssd_conv_b", DEPTH * XBC), ("dt_bias", DEPTH * NHEAD),
         ("a_log", DEPTH * NHEAD), ("d_skip", DEPTH * NHEAD), ("ssd_norm_w", DEPTH * D),
         ("norm_mlp_w", DEPTH * D), ("final_norm_w", D)]
SMALL_LEN = sum(s for _, s in SMALL)
SMALL_ROWS = -(-SMALL_LEN // LANES)


def _pack_small(parts):
    flat = jnp.concatenate([parts[k].reshape(-1) for k, _ in SMALL])
    return jnp.pad(flat, (0, SMALL_ROWS * LANES - SMALL_LEN)).reshape(SMALL_ROWS, LANES)


def _unpack_small(packed, shapes):
    flat = packed.reshape(-1)
    out, off = {}, 0
    for k, s in SMALL:
        out[k] = flat[off:off + s].reshape(shapes[k])
        off += s
    return out


def kernel(x, norm_mix_w, w_in, short_conv_w, ssd_conv_w, ssd_conv_b, dt_bias, a_log, d_skip, ssd_norm_w, w_out, norm_mlp_w, w_up, w_down, final_norm_w, loss_target, m_norm_mix_w, m_w_in, m_short_conv_w, m_ssd_conv_w, m_ssd_conv_b, m_dt_bias, m_a_log, m_d_skip, m_ssd_norm_w, m_w_out, m_norm_mlp_w, m_w_up, m_w_down, m_final_norm_w, v_norm_mix_w, v_w_in, v_short_conv_w, v_ssd_conv_w, v_ssd_conv_b, v_dt_bias, v_a_log, v_d_skip, v_ssd_norm_w, v_w_out, v_norm_mlp_w, v_w_up, v_w_down, v_final_norm_w):
    xs = x[0]
    t_len = xs.shape[0]
    tt = min(256, t_len)
    eh, eht = _head_matrices()
    me = 4 * lax.axis_index("x") + 2 * lax.axis_index("y") + lax.axis_index("c")

    def start_weights(i, after):
        first = exchange_start(
            [jnp.pad(w_in[i].astype(BF16), ((0, 0), (0, SLOT_W - IN_SHARD))), short_conv_w[i], ssd_conv_w[i]],
            False, "w_in_start_%d" % i, after)
        rest = exchange_start([w_out[i].astype(BF16), w_up[i].astype(BF16), w_down[i].astype(BF16)], False,
                              "w_rest_start_%d" % i, first[4] if after is None else after)
        return first, rest

    def fill_own(srcs, lands, per_peer):
        own = [lax.dynamic_index_in_dim(s_, me, 0, keepdims=False) for s_ in srcs] if per_peer else srcs
        return [lax.dynamic_update_index_in_dim(l_, o_, me, 0) for l_, o_ in zip(lands, own)]

    act = xs
    saved, layers = [], []
    first, rest = start_weights(0, None)
    token = first[4][0, 0] + rest[4][0, 0]
    for i in range(DEPTH):
        g_in, g_sc, g_cw = fill_own(*exchange_wait(first, act, False, "w_in_wait_%d" % i), False)
        lw, prm = layer_params(
            assemble_w_in(g_in), g_sc.transpose(1, 0, 2).reshape(3, D), g_cw.transpose(1, 0, 2).reshape(4, XBC),
            norm_mix_w[i], norm_mlp_w[i], ssd_conv_b[i], dt_bias[i], a_log[i], d_skip[i], ssd_norm_w[i], eh, eht)
        lw["nw1"] = lw["nw1"] + token
        mixed = layer_fwd_mix(act, lw, prm, tt)
        g_out, g_up, g_dn = fill_own(*exchange_wait(rest, mixed[3], False, "w_rest_wait_%d" % i), False)
        lw.update(wout=g_out.reshape(MIX, D), wup=g_up, wdn=g_dn.reshape(DFF, D))
        if i + 1 < DEPTH:
            first, rest = start_weights(i + 1, g_dn)
            token = first[4][0, 0] + rest[4][0, 0]
            lw["nw2"] = lw["nw2"] + token
        layers.append((lw, prm))
        act, sv = layer_fwd_mlp(act, mixed, lw)
        saved.append(sv)
    loss_acc, dx, dxb, g_fw = loss_head(act, final_norm_w[None, :], loss_target[0])

    grads = [None] * DEPTH
    sent_rest, sent_in = [None] * DEPTH, [None] * DEPTH
    token = None
    for i in reversed(range(DEPTH)):
        lw, prm = layers[i]
        if token is not None:
            lw = dict(lw, nw2=lw["nw2"] + token)
        dx1, _, dy, g_mlp = layer_bwd_mlp(dx, dxb, lw, saved[i])
        sent_rest[i] = exchange_start(
            [g_mlp["wout"].reshape(N_DEV, MIX // N_DEV, D), g_mlp["wup"], g_mlp["wdn"].reshape(N_DEV, DFF // N_DEV, D)],
            True, "g_rest_start_%d" % i)
        dx, dxb, g_mix = layer_bwd_mix(dx1, dy, lw, dict(prm, nrm=prm["nrm"] + sent_rest[i][4][0, 0]), saved[i], tt)
        grads[i] = {**g_mlp, **g_mix}
        if i > 0:
            sent_in[i] = exchange_start([scatter_w_in(g_mix["win"])], True, "g_in_start_%d" % i)
            token = sent_in[i][4][0, 0]

    def stack(k):
        return jnp.stack([g[k] for g in grads])

    small = _pack_small({"norm_mix_w": stack("nw1"), "ssd_conv_b": stack("cb"), "dt_bias": stack("dtb"),
                         "a_log": stack("alog"), "d_skip": stack("dsk"), "ssd_norm_w": stack("nrm"),
                         "norm_mlp_w": stack("nw2"), "final_norm_w": g_fw[0]})
    r_small, r_sc, r_cw = all_gather([small, stack("scw"), stack("cw")], "gather_small_grads")
    r_sc = lax.dynamic_slice_in_dim(r_sc, me * (D // N_DEV), D // N_DEV, axis=3)
    r_cw = lax.dynamic_slice_in_dim(r_cw, me * (XBC // N_DEV), XBC // N_DEV, axis=3)
    sent_in[0] = exchange_start([scatter_w_in(grads[0]["win"])], True, "g_in_start_0", after=r_small)

    after = sent_in[0][4]
    recv = [fill_own(*exchange_wait(sent_rest[i], after, True, "g_rest_wait_%d" % i), True) for i in range(DEPTH)]
    res = {}
    res["w_out"] = adamw_layers(w_out, [r[0] for r in recv], m_w_out, v_w_out, "adamw_w_out")
    res["w_up"] = adamw_layers(w_up, [r[1] for r in recv], m_w_up, v_w_up, "adamw_w_up")
    res["w_down"] = adamw_layers(w_down, [r[2] for r in recv], m_w_down, v_w_down, "adamw_w_down")
    after = res["w_down"][1]
    recv_in = [fill_own(*exchange_wait(sent_in[i], after, True, "g_in_wait_%d" % i), True)[0] for i in range(DEPTH)]
    res["w_in"] = adamw_layers(w_in, recv_in, m_w_in, v_w_in, "adamw_w_in")
    res["short_conv_w"] = _adamw_nd(short_conv_w, r_sc, m_short_conv_w, v_short_conv_w, "adamw_short_conv")
    res["ssd_conv_w"] = _adamw_nd(ssd_conv_w, r_cw, m_ssd_conv_w, v_ssd_conv_w, "adamw_ssd_conv")
    small_w = {"norm_mix_w": norm_mix_w, "ssd_conv_b": ssd_conv_b, "dt_bias": dt_bias, "a_log": a_log,
               "d_skip": d_skip, "ssd_norm_w": ssd_norm_w, "norm_mlp_w": norm_mlp_w, "final_norm_w": final_norm_w}
    small_m = {"norm_mix_w": m_norm_mix_w, "ssd_conv_b": m_ssd_conv_b, "dt_bias": m_dt_bias, "a_log": m_a_log,
               "d_skip": m_d_skip, "ssd_norm_w": m_ssd_norm_w, "norm_mlp_w": m_norm_mlp_w,
               "final_norm_w": m_final_norm_w}
    small_v = {"norm_mix_w": v_norm_mix_w, "ssd_conv_b": v_ssd_conv_b, "dt_bias": v_dt_bias, "a_log": v_a_log,
               "d_skip": v_d_skip, "ssd_norm_w": v_ssd_norm_w, "norm_mlp_w": v_norm_mlp_w,
               "final_norm_w": v_final_norm_w}
    shapes = {k: a.shape for k, a in small_w.items()}
    packed = adamw(_pack_small(small_w), r_small, _pack_small(small_m), _pack_small(small_v), "adamw_small")
    unpacked = [_unpack_small(p, shapes) for p in packed]
    for k in small_w:
        res[k] = [u[k] for u in unpacked]

    loss = lax.psum(loss_acc[0, 0], ("x", "y", "c"))
    order = ["norm_mix_w", "w_in", "short_conv_w", "ssd_conv_w", "ssd_conv_b", "dt_bias", "a_log", "d_skip",
             "ssd_norm_w", "w_out", "norm_mlp_w", "w_up", "w_down", "final_norm_w"]
    out = [loss, dx[None]]
    for part in range(4):
        out += [res[k][part] for k in order]
    return tuple(out)
```

```python
import functools

import numpy as np
import jax
import jax.numpy as jnp
from jax import lax
from jax.experimental import pallas as pl
from jax.experimental.pallas import tpu as pltpu

F32 = jnp.float32
BF16 = jnp.bfloat16
SDS = jax.ShapeDtypeStruct

N_DEV = 8
DEPTH = 4
D = 1024
NIN = 5648
NINP = 5760
DFF = 4096
MIX = 2048
NHEAD = 16
HDIM = 64
NSTATE = 128
CHUNK = 64
XBC = 1536
EPS = 1e-5
LANES = 128

C_UB, C_UC, C_UH, C_Z, C_XS, C_BC, C_DT = 0, 1024, 2048, 3072, 4096, 5120, 5632
A_CV, A_YS, A_PX, A_PBC, AUX_W = 0, 1024, 2048, 3072, 3584

ADAM_LR = 0.001
ADAM_B1 = 0.9
ADAM_B2 = 0.999
ADAM_EPS = 1e-08
ADAM_WD = 0.01
ADAM_STEP = 10

VMEM_LIMIT = 56 * 1024 * 1024
MESH = pl.DeviceIdType.MESH


def _cparams(sem):
    return pltpu.CompilerParams(dimension_semantics=sem, vmem_limit_bytes=VMEM_LIMIT)


def _nt(a, b):
    return lax.dot_general(a, b, (((1,), (1,)), ((), ())), preferred_element_type=F32)


def _tn(a, b):
    return lax.dot_general(a, b, (((0,), (0,)), ((), ())), preferred_element_type=F32)


def _nn(a, b):
    return jnp.dot(a, b, preferred_element_type=F32)


def _sigmoid(v):
    return 0.5 * jnp.tanh(0.5 * v) + 0.5


def _split3(v):
    v1 = v.astype(BF16)
    r1 = v - v1.astype(F32)
    v2 = r1.astype(BF16)
    v3 = (r1 - v2.astype(F32)).astype(BF16)
    return v1, v2, v3


def _expand(v, eh):
    v1, v2, v3 = _split3(v)
    return _nn(v1, eh) + _nn(v2, eh) + _nn(v3, eh)


def _head_reduce(v, eht):
    v1 = v.astype(BF16)
    v2 = (v - v1.astype(F32)).astype(BF16)
    return _nn(v1, eht) + _nn(v2, eht)


def _head_matrices():
    eh = np.zeros((LANES, D), np.float32)
    for h in range(NHEAD):
        eh[h, h * HDIM:(h + 1) * HDIM] = 1.0
    return jnp.asarray(eh, BF16), jnp.asarray(eh.T.copy(), BF16)


def _resident(shape):
    return pl.BlockSpec(shape, lambda *_: (0,) * len(shape), pipeline_mode=pl.Buffered(1))


def _col_chunks(n, step):
    return [(c, min(c + step, n)) for c in range(0, n, step)]


def norm_matmul(x, nw, w, name):
    t_len = x.shape[0]
    blocked = w.ndim == 3
    n_len = w.shape[0] * w.shape[2] if blocked else w.shape[1]
    tm = min(512, t_len)
    chunks = _col_chunks(n_len, n_len // N_DEV if blocked else 1536)

    def body(x_ref, nw_ref, w_ref, o_ref, h_ref):
        xv = x_ref[...]
        r = lax.rsqrt(jnp.mean(xv * xv, axis=-1, keepdims=True) + EPS)
        hv = (xv * r * nw_ref[...]).astype(BF16)
        h_ref[...] = hv
        for j, (c0, c1) in enumerate(chunks):
            wj = w_ref[j] if blocked else w_ref[:, c0:c1]
            o_ref[:, c0:c1] = _nn(hv, wj).astype(o_ref.dtype)

    return pl.pallas_call(
        body, grid=(t_len // tm,),
        in_specs=[pl.BlockSpec((tm, D), lambda i: (i, 0)), _resident((1, D)), _resident(w.shape)],
        out_specs=[pl.BlockSpec((tm, n_len), lambda i: (i, 0)),
                   pl.BlockSpec((tm, D), lambda i: (i, 0))],
        out_shape=[SDS((t_len, n_len), BF16), SDS((t_len, D), BF16)],
        compiler_params=_cparams(("parallel",)), name=name)(x, nw, w)


def matmul_residual(a, w, res, relu2, name):
    t_len, k_len = a.shape
    tm = min(512, t_len)

    def body(a_ref, w_ref, res_ref, o_ref):
        av = a_ref[...]
        if relu2:
            af = jnp.maximum(av.astype(F32), 0.0)
            av = (af * af).astype(BF16)
        o_ref[...] = res_ref[...] + _nn(av, w_ref[...])

    return pl.pallas_call(
        body, grid=(t_len // tm,),
        in_specs=[pl.BlockSpec((tm, k_len), lambda i: (i, 0)),
                  _resident((k_len, D)),
                  pl.BlockSpec((tm, D), lambda i: (i, 0))],
        out_specs=pl.BlockSpec((tm, D), lambda i: (i, 0)),
        out_shape=SDS((t_len, D), F32),
        compiler_params=_cparams(("parallel",)), name=name)(a, w, res)


def matmul_nt_act(dy, w, u, name):
    t_len = dy.shape[0]
    n_len = w.shape[0]
    tm = min(512, t_len)
    chunks = _col_chunks(n_len, 1024)

    def body(dy_ref, w_ref, *rest):
        if u is None:
            (o_ref,) = rest
        else:
            u_ref, o_ref = rest
        dyv = dy_ref[...]
        for c0, c1 in chunks:
            p = _nt(dyv, w_ref[c0:c1, :])
            if u is not None:
                p = p * (2.0 * jnp.maximum(u_ref[:, c0:c1].astype(F32), 0.0))
            o_ref[:, c0:c1] = p.astype(o_ref.dtype)

    in_specs = [pl.BlockSpec((tm, D), lambda i: (i, 0)), _resident((n_len, D))]
    args = [dy, w]
    if u is not None:
        in_specs.append(pl.BlockSpec((tm, n_len), lambda i: (i, 0)))
        args.append(u)
    return pl.pallas_call(
        body, grid=(t_len // tm,),
        in_specs=in_specs,
        out_specs=pl.BlockSpec((tm, n_len), lambda i: (i, 0)),
        out_shape=SDS((t_len, n_len), BF16),
        compiler_params=_cparams(("parallel",)), name=name)(*args)


def matmul_tn(a, b, a_spec, b_spec, o_spec, o_shape, n_out, relu2, name):
    t_len = a.shape[0]
    tt = min(2048, t_len)
    nt = t_len // tt

    def body(a_ref, b_ref, o_ref, acc):
        t = pl.program_id(1)
        av = a_ref[...]
        if relu2:
            af = jnp.maximum(av.astype(F32), 0.0)
            av = (af * af).astype(BF16)
        p = _tn(av, b_ref[...])

        @pl.when(t == 0)
        def _():
            acc[...] = p

        @pl.when(t > 0)
        def _():
            acc[...] += p

        @pl.when(t == nt - 1)
        def _():
            if len(blk) == 3:
                for j in range(blk[0]):
                    o_ref[j] = acc[:, j * blk[2]:(j + 1) * blk[2]].astype(o_ref.dtype)
            else:
                o_ref[...] = acc[...].astype(o_ref.dtype)

    blk = tuple(o_spec.block_shape)
    acc_shape = (blk[1], blk[0] * blk[2]) if len(blk) == 3 else blk
    return pl.pallas_call(
        body, grid=(n_out, nt),
        in_specs=[a_spec(tt), b_spec(tt)],
        out_specs=o_spec, out_shape=o_shape,
        scratch_shapes=[pltpu.VMEM(acc_shape, F32)],
        compiler_params=_cparams(("parallel", "arbitrary")), name=name)(a, b)


def matmul_nt_norm_bwd(dy, w, x, nw, dres, name):
    t_len = x.shape[0]
    blocked = w.ndim == 3
    k_len = dy.shape[1]
    kb = k_len // N_DEV
    tm = min(512, t_len)

    def body(dy_ref, w_ref, x_ref, nw_ref, dres_ref, dx_ref, dxb_ref, dnw_ref):
        @pl.when(pl.program_id(0) == 0)
        def _():
            dnw_ref[...] = jnp.zeros_like(dnw_ref)

        if blocked:
            dh = _nt(dy_ref[:, 0:kb], w_ref[0])
            for j in range(1, N_DEV):
                dh = dh + _nt(dy_ref[:, j * kb:(j + 1) * kb], w_ref[j])
        else:
            dh = _nt(dy_ref[...], w_ref[...])
        xv = x_ref[...]
        r = lax.rsqrt(jnp.mean(xv * xv, axis=-1, keepdims=True) + EPS)
        xh = xv * r
        dnw_ref[0:1, :] += jnp.sum(dh * xh, axis=0, keepdims=True)
        g = dh * nw_ref[...]
        dx = dres_ref[...] + r * (g - xh * jnp.mean(g * xh, axis=-1, keepdims=True))
        dx_ref[...] = dx
        dxb_ref[...] = dx.astype(BF16)

    return pl.pallas_call(
        body, grid=(t_len // tm,),
        in_specs=[pl.BlockSpec((tm, k_len), lambda i: (i, 0)),
                  _resident(w.shape),
                  pl.BlockSpec((tm, D), lambda i: (i, 0)),
                  _resident((1, D)),
                  pl.BlockSpec((tm, D), lambda i: (i, 0))],
        out_specs=[pl.BlockSpec((tm, D), lambda i: (i, 0)),
                   pl.BlockSpec((tm, D), lambda i: (i, 0)),
                   pl.BlockSpec((8, D), lambda i: (0, 0))],
        out_shape=[SDS((t_len, D), F32), SDS((t_len, D), BF16), SDS((8, D), F32)],
        compiler_params=_cparams(("arbitrary",)), name=name)(dy, w, x, nw, dres)


def loss_head(x, fw, tgt):
    t_len = x.shape[0]
    tm = min(512, t_len)

    def body(x_ref, fw_ref, t_ref, loss_ref, dx_ref, dxb_ref, dfw_ref):
        @pl.when(pl.program_id(0) == 0)
        def _():
            loss_ref[...] = jnp.zeros_like(loss_ref)
            dfw_ref[...] = jnp.zeros_like(dfw_ref)
        xv = x_ref[...]
        r = lax.rsqrt(jnp.mean(xv * xv, axis=-1, keepdims=True) + EPS)
        xh = xv * r
        w = fw_ref[...]
        e = xh * w - t_ref[...]
        row = jnp.sum(e * e, axis=-1, keepdims=True) * (1.0 / D)
        loss_ref[...] += 0.5 * jnp.sum(row, axis=0, keepdims=True)
        dyf = e * (1.0 / D)
        dfw_ref[0:1, :] += jnp.sum(dyf * xh, axis=0, keepdims=True)
        g = dyf * w
        dx = r * (g - xh * jnp.mean(g * xh, axis=-1, keepdims=True))
        dx_ref[...] = dx
        dxb_ref[...] = dx.astype(BF16)

    return pl.pallas_call(
        body, grid=(t_len // tm,),
        in_specs=[pl.BlockSpec((tm, D), lambda i: (i, 0)),
                  pl.BlockSpec((1, D), lambda i: (0, 0)),
                  pl.BlockSpec((tm, D), lambda i: (i, 0))],
        out_specs=[pl.BlockSpec((8, LANES), lambda i: (0, 0)),
                   pl.BlockSpec((tm, D), lambda i: (i, 0)),
                   pl.BlockSpec((tm, D), lambda i: (i, 0)),
                   pl.BlockSpec((8, D), lambda i: (0, 0))],
        out_shape=[SDS((8, LANES), F32), SDS((t_len, D), F32), SDS((t_len, D), BF16), SDS((8, D), F32)],
        compiler_params=_cparams(("arbitrary",)), name="loss_head")(x, fw, tgt)


def _shift_dn(x, halo, j):
    if j == 0:
        return x
    xr = pltpu.roll(x, j, 0)
    hr = pltpu.roll(halo, j, 0)
    row = lax.broadcasted_iota(jnp.int32, hr.shape, 0)
    top = jnp.where(row < j, hr, xr[0:8])
    return jnp.concatenate([top, xr[8:]], axis=0)


def _shift_up(x, nxt, j):
    if j == 0:
        return x
    n = x.shape[0]
    xr = pltpu.roll(x, n - j, 0)
    hr = pltpu.roll(nxt, 8 - j, 0)
    row = lax.broadcasted_iota(jnp.int32, hr.shape, 0)
    bot = jnp.where(row >= 8 - j, hr, xr[n - 8:n])
    return jnp.concatenate([xr[:n - 8], bot], axis=0)


def _conv_fwd(x, halo, w_ref, kw):
    acc = None
    for k in range(kw):
        term = w_ref[k:k + 1, :] * _shift_dn(x, halo, kw - 1 - k)
        acc = term if acc is None else acc + term
    return acc


def _chunk_cumsum(a, pos):
    for sh in (1, 2, 4, 8, 16, 32):
        a = a + jnp.where(pos >= sh, pltpu.roll(a, sh, 0), 0.0)
    return a


def _chunk_rcumsum(a, pos):
    n = a.shape[0]
    for sh in (1, 2, 4, 8, 16, 32):
        a = a + jnp.where(pos < CHUNK - sh, pltpu.roll(a, n - sh, 0), 0.0)
    return a


def _softplus(v):
    return jnp.maximum(v, 0.0) + jnp.log(1.0 + jnp.exp(-jnp.abs(v)))


def _silu(v):
    return v * _sigmoid(v)


def _dsilu(v):
    s = _sigmoid(v)
    return s * (1.0 + v * (1.0 - s))


def _lane_masks(width=D):
    lane = lax.broadcasted_iota(jnp.int32, (CHUNK, width), 1) & (HDIM - 1)
    row = lax.broadcasted_iota(jnp.int32, (CHUNK, width), 0)
    return lane == row, lane <= row


def _rep_matrix():
    lane = lax.broadcasted_iota(jnp.int32, (CHUNK, 512), 1) & (HDIM - 1)
    row = lax.broadcasted_iota(jnp.int32, (CHUNK, 512), 0)
    return jnp.where(lane == row, 1.0, 0.0).astype(BF16)


def _blockdiag(xp):
    lane = lax.broadcasted_iota(jnp.int32, xp.shape, 1)
    zero = jnp.zeros_like(xp)
    return jnp.concatenate([jnp.where(lane < HDIM, xp, zero), jnp.where(lane >= HDIM, xp, zero)], axis=0)


def _mixer_views(tt):
    r8 = tt // 8

    def main(width, col):
        return pl.BlockSpec((tt, width), lambda i, c=col // width: (i, c))

    def halo(width, col):
        return pl.BlockSpec((8, width), lambda i, c=col // width: (jnp.maximum(i * r8 - 1, 0), c))

    return main, halo


def mixer_fwd(proj, prm, tt):
    t_len = proj.shape[0]
    nblk = t_len // tt
    nc = tt // CHUNK
    main, halo = _mixer_views(tt)

    def body(ub_ref, uc_ref, uh_ref, z_ref, xr_ref, bcr_ref, dtr_ref, uch_ref, uhh_ref, xrh_ref, bcrh_ref,
             scw_ref, cwx_ref, cwbc_ref, cbx_ref, cbbc_ref, dtb_ref, alog_ref, dsk_ref, nrm_ref, eh_ref,
             y_ref, st_ref, aux_ref, hs, xs_s, bc_s, dtx_s, cumx_s, yssd_s):
        i = pl.program_id(0)
        first = i == 0

        @pl.when(first)
        def _():
            hs[...] = jnp.zeros_like(hs)

        keep = jnp.where(first, 0.0, 1.0)
        v = uc_ref[...].astype(F32) * uh_ref[...].astype(F32)
        vh = uch_ref[...].astype(F32) * uhh_ref[...].astype(F32) * keep
        cv = _conv_fwd(v, vh, scw_ref, 3)
        aux_ref[:, A_CV:A_CV + D] = cv.astype(BF16)
        y_ref[:, 0:D] = (ub_ref[...].astype(F32) * cv).astype(BF16)

        pre_x = _conv_fwd(xr_ref[...].astype(F32), xrh_ref[...].astype(F32) * keep, cwx_ref, 4) + cbx_ref[...]
        aux_ref[:, A_PX:A_PX + D] = pre_x.astype(BF16)
        xs_s[...] = _silu(pre_x)
        pre_bc = _conv_fwd(bcr_ref[...].astype(F32), bcrh_ref[...].astype(F32) * keep, cwbc_ref, 4) + cbbc_ref[...]
        aux_ref[:, A_PBC:A_PBC + 512] = pre_bc.astype(BF16)
        bc_s[...] = _silu(pre_bc)
        dt = _softplus(dtr_ref[...].astype(F32) + dtb_ref[...])
        a_neg = -jnp.exp(alog_ref[...])
        pos = lax.broadcasted_iota(jnp.int32, (tt, LANES), 0) & (CHUNK - 1)
        cum = _chunk_cumsum(dt * a_neg, pos)
        eh = eh_ref[...]
        dtx_s[...] = _expand(dt, eh)
        cumx_s[...] = _expand(cum, eh)
        irep, causal = _lane_masks()
        rep = _rep_matrix()

        def chunk(c, carry):
            r0 = pl.multiple_of(c * CHUNK, CHUNK)
            rows = pl.ds(r0, CHUNK)
            cumx = cumx_s[rows, :]
            cum_l = cumx[CHUNK - 1:CHUNK, :]
            xd = xs_s[rows, :] * dtx_s[rows, :]
            xf = xd * jnp.exp(cum_l - cumx)
            ex = jnp.exp(cumx)
            e_l = jnp.exp(cum_l)
            rvec = jnp.sum(jnp.where(irep, cumx, 0.0), axis=0, keepdims=True)
            lam = jnp.where(causal, jnp.exp(jnp.where(causal, cumx - rvec, 0.0)), 0.0)
            bc = bc_s[rows, :]
            for g in range(2):
                gs = slice(g * 512, (g + 1) * 512)
                bg = bc[:, g * NSTATE:(g + 1) * NSTATE].astype(BF16)
                cg = bc[:, 256 + g * NSTATE:256 + (g + 1) * NSTATE].astype(BF16)
                s_rep = _nn(_nt(cg, bg).astype(BF16), rep)
                m_g = (s_rep * lam[:, gs]).astype(BF16)
                h_g = hs[:, gs]
                h_b = h_g.astype(BF16)
                st_ref[c, :, gs] = h_b
                yo = _nn(cg, h_b) * ex[:, gs]
                xd_b = xd[:, gs].astype(BF16)
                for hp in range(4):
                    ps = slice(hp * LANES, (hp + 1) * LANES)
                    yd = _nn(m_g[:, ps], _blockdiag(xd_b[:, ps]))
                    yssd_s[rows, g * 512 + hp * LANES:g * 512 + (hp + 1) * LANES] = yd + yo[:, ps]
                hs[:, gs] = h_g * e_l[:, gs] + _tn(bg, xf[:, gs].astype(BF16))
            return carry

        lax.fori_loop(0, nc, chunk, 0)

        ys = yssd_s[...] + dsk_ref[...] * xs_s[...]
        aux_ref[:, A_YS:A_YS + D] = ys.astype(BF16)
        gt = ys * _silu(z_ref[...].astype(F32))
        for g in range(2):
            gs = slice(g * 512, (g + 1) * 512)
            gg = gt[:, gs]
            rn = lax.rsqrt(jnp.mean(gg * gg, axis=-1, keepdims=True) + EPS)
            y_ref[:, D + g * 512:D + (g + 1) * 512] = (gg * rn * nrm_ref[:, gs]).astype(BF16)

    def const(shape):
        return pl.BlockSpec(shape, lambda i: (0, 0))

    in_specs = [main(D, C_UB), main(D, C_UC), main(D, C_UH), main(D, C_Z), main(D, C_XS), main(512, C_BC),
                main(LANES, C_DT), halo(D, C_UC), halo(D, C_UH), halo(D, C_XS), halo(512, C_BC),
                const((8, D)), const((8, D)), const((8, 512)), const((1, D)), const((1, 512)),
                const((1, LANES)), const((1, LANES)), const((1, D)), const((1, D)), const((LANES, D))]
    return pl.pallas_call(
        body, grid=(nblk,),
        in_specs=in_specs,
        out_specs=[pl.BlockSpec((tt, MIX), lambda i: (i, 0)),
                   pl.BlockSpec((nc, NSTATE, D), lambda i: (i, 0, 0)),
                   pl.BlockSpec((tt, AUX_W), lambda i: (i, 0))],
        out_shape=[SDS((t_len, MIX), BF16), SDS((t_len // CHUNK, NSTATE, D), BF16), SDS((t_len, AUX_W), BF16)],
        scratch_shapes=[pltpu.VMEM((NSTATE, D), F32), pltpu.VMEM((tt, D), F32), pltpu.VMEM((tt, 512), F32),
                        pltpu.VMEM((tt, D), F32), pltpu.VMEM((tt, D), F32), pltpu.VMEM((tt, D), F32)],
        compiler_params=_cparams(("arbitrary",)), name="mixer_fwd")(
            *([proj] * 11), prm["scw"], prm["cwx"], prm["cwbc"], prm["cbx"], prm["cbbc"], prm["dtb"],
            prm["alog"], prm["dskx"], prm["nrm"], prm["eh"])


def mixer_bwd(proj, dy, states, aux, prm, tt):
    t_len = proj.shape[0]
    nblk = t_len // tt
    nc = tt // CHUNK

    def rev(i):
        return nblk - 1 - i

    def main(width, col):
        return pl.BlockSpec((tt, width), lambda i, c=col // width: (rev(i), c))

    def body(ub_ref, uc_ref, uh_ref, z_ref, xr_ref, bcr_ref, dtr_ref, dy_ref, st_ref, aux_ref,
             scw_ref, cwx_ref, cwbc_ref, cbx_ref, cbbc_ref, dtb_ref, alog_ref, dsk_ref, nrm_ref, eh_ref, eht_ref,
             dp_ref, gscw_ref, gcwx_ref, gcwbc_ref, gvec_ref, gdt_ref,
             dhs, xs_s, bc_s, dtx_s, cumx_s, dys_s, dxs_s, dbc_s, red_s, ddtx_s, nx_cv, nx_px, nx_pbc):
        i = pl.program_id(0)

        @pl.when(i == 0)
        def _():
            dhs[...] = jnp.zeros_like(dhs)
            nx_cv[...] = jnp.zeros_like(nx_cv)
            nx_px[...] = jnp.zeros_like(nx_px)
            nx_pbc[...] = jnp.zeros_like(nx_pbc)
            gscw_ref[...] = jnp.zeros_like(gscw_ref)
            gcwx_ref[...] = jnp.zeros_like(gcwx_ref)
            gcwbc_ref[...] = jnp.zeros_like(gcwbc_ref)
            gvec_ref[...] = jnp.zeros_like(gvec_ref)
            gdt_ref[...] = jnp.zeros_like(gdt_ref)

        uc = uc_ref[...].astype(F32)
        uh = uh_ref[...].astype(F32)
        v = uc * uh
        dya = dy_ref[:, 0:D].astype(F32)
        dp_ref[:, C_UB:C_UB + D] = (dya * aux_ref[:, A_CV:A_CV + D].astype(F32)).astype(BF16)
        dcv = dya * ub_ref[...].astype(F32)
        nxt = nx_cv[...]
        dv = None
        for k in range(3):
            up = _shift_up(dcv, nxt, 2 - k)
            gscw_ref[k:k + 1, :] += jnp.sum(v * up, axis=0, keepdims=True)
            term = scw_ref[k:k + 1, :] * up
            dv = term if dv is None else dv + term
        nx_cv[...] = dcv[0:8]
        dp_ref[:, C_UC:C_UC + D] = (dv * uh).astype(BF16)
        dp_ref[:, C_UH:C_UH + D] = (dv * uc).astype(BF16)

        pre_x = aux_ref[:, A_PX:A_PX + D].astype(F32)
        pre_bc = aux_ref[:, A_PBC:A_PBC + 512].astype(F32)
        xs = _silu(pre_x)
        xs_s[...] = xs
        bc_s[...] = _silu(pre_bc)
        dt_pre = dtr_ref[...].astype(F32) + dtb_ref[...]
        dt = _softplus(dt_pre)
        a_neg = -jnp.exp(alog_ref[...])
        pos = lax.broadcasted_iota(jnp.int32, (tt, LANES), 0) & (CHUNK - 1)
        cum = _chunk_cumsum(dt * a_neg, pos)
        eh = eh_ref[...]
        eht = eht_ref[...]
        dtx_s[...] = _expand(dt, eh)
        cumx_s[...] = _expand(cum, eh)

        irep, causal = _lane_masks()
        irep_g, _ = _lane_masks(512)
        rep = _rep_matrix()
        row64 = lax.broadcasted_iota(jnp.int32, (CHUNK, 512), 0)
        lane128 = lax.broadcasted_iota(jnp.int32, (CHUNK, LANES), 1)

        z = z_ref[...].astype(F32)
        sz = _silu(z)
        ys = aux_ref[:, A_YS:A_YS + D].astype(F32)
        gt = ys * sz
        dyb = dy_ref[:, D:MIX].astype(F32)
        for g in range(2):
            gs = slice(g * 512, (g + 1) * 512)
            gg = gt[:, gs]
            rn = lax.rsqrt(jnp.mean(gg * gg, axis=-1, keepdims=True) + EPS)
            gvec_ref[0:1, gs] += jnp.sum(dyb[:, gs] * gg * rn, axis=0, keepdims=True)
            dgn = dyb[:, gs] * nrm_ref[:, gs]
            dgt = rn * (dgn - gg * (rn * rn) * jnp.mean(dgn * gg, axis=-1, keepdims=True))
            dys = dgt * sz[:, gs]
            dys_s[:, gs] = dys
            dp_ref[:, C_Z + g * 512:C_Z + (g + 1) * 512] = (dgt * ys[:, gs] * _dsilu(z[:, gs])).astype(BF16)
        dys_all = dys_s[...]
        gvec_ref[1:2, :] += jnp.sum(dys_all * xs, axis=0, keepdims=True)

        def bwd_chunk(cc, carry):
            c = nc - 1 - cc
            r0 = pl.multiple_of(c * CHUNK, CHUNK)
            rows = pl.ds(r0, CHUNK)
            cumx = cumx_s[rows, :]
            cum_l = cumx[CHUNK - 1:CHUNK, :]
            xs_c = xs_s[rows, :]
            dtx = dtx_s[rows, :]
            xd = xs_c * dtx
            f = jnp.exp(cum_l - cumx)
            xf = xd * f
            ex = jnp.exp(cumx)
            e_l = jnp.exp(cum_l)
            rvec = jnp.sum(jnp.where(irep, cumx, 0.0), axis=0, keepdims=True)
            lam = jnp.where(causal, jnp.exp(jnp.where(causal, cumx - rvec, 0.0)), 0.0)
            bc = bc_s[rows, :]
            dyc = dys_s[rows, :]
            for g in range(2):
                gs = slice(g * 512, (g + 1) * 512)
                bg = bc[:, g * NSTATE:(g + 1) * NSTATE].astype(BF16)
                cg = bc[:, 256 + g * NSTATE:256 + (g + 1) * NSTATE].astype(BF16)
                h0 = st_ref[c, :, gs]
                dh = dhs[:, gs]
                dh_b = dh.astype(BF16)
                xf_g = xf[:, gs]
                dxf = _nn(bg, dh_b)
                db = _nt(xf_g.astype(BF16), dh_b)
                s_rep = _nn(_nt(cg, bg).astype(BF16), rep)
                lam_g = lam[:, gs]
                m_g = s_rep * lam_g
                m_b = m_g.astype(BF16)
                ex_g = ex[:, gs]
                dy_g = dyc[:, gs]
                yo = _nn(cg, h0) * ex_g
                dg_b = (dy_g * ex_g).astype(BF16)
                dc = _nt(dg_b, h0)
                el_g = e_l[:, gs]
                dee = jnp.sum(dh * h0.astype(F32), axis=0, keepdims=True) * el_g
                dhs[:, gs] = dh * el_g + _tn(cg, dg_b)
                xd_b = xd[:, gs].astype(BF16)
                dy_b = dy_g.astype(BF16)
                dm_parts, dxd_parts = [], []
                for hp in range(4):
                    ps = slice(hp * LANES, (hp + 1) * LANES)
                    bd = _blockdiag(xd_b[:, ps])
                    dm_parts.append(_nt(dy_b[:, ps], bd))
                    t2 = _tn(m_b[:, ps], dy_b[:, ps])
                    dxd_parts.append(jnp.where(lane128 < HDIM, t2[0:CHUNK], t2[CHUNK:2 * CHUNK]))
                dm = jnp.concatenate(dm_parts, axis=1)
                dxd = jnp.concatenate(dxd_parts, axis=1) + dxf * f[:, gs]
                dseg = dm * m_g
                ds_b = _nt((dm * lam_g).astype(BF16), rep).astype(BF16)
                dc = dc + _nn(ds_b, bg)
                db = db + _tn(ds_b, cg)
                colsum = jnp.sum(dseg, axis=0, keepdims=True)
                dxfxf = dxf * xf_g
                red = dseg - jnp.where(irep_g, colsum, 0.0) + dy_g * yo - dxfxf
                last = jnp.sum(dxfxf, axis=0, keepdims=True) + dee
                red = red + jnp.where(row64 == CHUNK - 1, last, 0.0)
                red_s[rows, gs] = red
                ddtx_s[rows, gs] = dxd * xs_c[:, gs]
                dxs_s[rows, gs] = dxd * dtx[:, gs] + dsk_ref[:, gs] * dy_g
                dbc_s[rows, g * NSTATE:(g + 1) * NSTATE] = db
                dbc_s[rows, 256 + g * NSTATE:256 + (g + 1) * NSTATE] = dc
            return carry

        lax.fori_loop(0, nc, bwd_chunk, 0)

        dcum = _head_reduce(red_s[...], eht)
        da = _chunk_rcumsum(dcum, pos)
        ddt = _head_reduce(ddtx_s[...], eht) + da * a_neg
        gdt_ref[1:2, :] += jnp.sum(da * dt, axis=0, keepdims=True) * a_neg
        ddt_raw = ddt * _sigmoid(dt_pre)
        lane_t = lax.broadcasted_iota(jnp.int32, (tt, LANES), 1)
        ddt_raw = jnp.where(lane_t < NHEAD, ddt_raw, 0.0)
        gdt_ref[0:1, :] += jnp.sum(ddt_raw, axis=0, keepdims=True)
        dp_ref[:, C_DT:C_DT + LANES] = ddt_raw.astype(BF16)

        dpx = dxs_s[...] * _dsilu(pre_x)
        dpbc = dbc_s[...] * _dsilu(pre_bc)
        gvec_ref[2:3, :] += jnp.sum(dpx, axis=0, keepdims=True)
        gcwbc_ref[4:5, :] += jnp.sum(dpbc, axis=0, keepdims=True)
        xraw = xr_ref[...].astype(F32)
        bcraw = bcr_ref[...].astype(F32)
        nxt_x = nx_px[...]
        nxt_bc = nx_pbc[...]
        dxr, dbcr = None, None
        for k in range(4):
            up_x = _shift_up(dpx, nxt_x, 3 - k)
            up_bc = _shift_up(dpbc, nxt_bc, 3 - k)
            gcwx_ref[k:k + 1, :] += jnp.sum(xraw * up_x, axis=0, keepdims=True)
            gcwbc_ref[k:k + 1, :] += jnp.sum(bcraw * up_bc, axis=0, keepdims=True)
            tx = cwx_ref[k:k + 1, :] * up_x
            tb = cwbc_ref[k:k + 1, :] * up_bc
            dxr = tx if dxr is None else dxr + tx
            dbcr = tb if dbcr is None else dbcr + tb
        nx_px[...] = dpx[0:8]
        nx_pbc[...] = dpbc[0:8]
        dp_ref[:, C_XS:C_XS + D] = dxr.astype(BF16)
        dp_ref[:, C_BC:C_BC + 512] = dbcr.astype(BF16)

        @pl.when(i == nblk - 1)
        def _():
            gdt_ref[2:3, :] = _head_reduce(gvec_ref[1:2, :] * jnp.ones((8, 1), F32), eht)[0:1, :]

    def const(shape):
        return pl.BlockSpec(shape, lambda i: (0, 0))

    in_specs = [main(D, C_UB), main(D, C_UC), main(D, C_UH), main(D, C_Z), main(D, C_XS), main(512, C_BC),
                main(LANES, C_DT),
                pl.BlockSpec((tt, MIX), lambda i: (rev(i), 0)),
                pl.BlockSpec((nc, NSTATE, D), lambda i: (rev(i), 0, 0)),
                pl.BlockSpec((tt, AUX_W), lambda i: (rev(i), 0)),
                const((8, D)), const((8, D)), const((8, 512)), const((1, D)), const((1, 512)),
                const((1, LANES)), const((1, LANES)), const((1, D)), const((1, D)), const((LANES, D)),
                const((D, LANES))]
    return pl.pallas_call(
        body, grid=(nblk,),
        in_specs=in_specs,
        out_specs=[pl.BlockSpec((tt, NINP), lambda i: (rev(i), 0)),
                   const((8, D)), const((8, D)), const((8, 512)), const((8, D)), const((8, LANES))],
        out_shape=[SDS((t_len, NINP), BF16), SDS((8, D), F32), SDS((8, D), F32), SDS((8, 512), F32),
                   SDS((8, D), F32), SDS((8, LANES), F32)],
        scratch_shapes=[pltpu.VMEM((NSTATE, D), F32),
                        pltpu.VMEM((tt, D), F32), pltpu.VMEM((tt, 512), F32),
                        pltpu.VMEM((tt, D), F32), pltpu.VMEM((tt, D), F32),
                        pltpu.VMEM((tt, D), F32), pltpu.VMEM((tt, D), F32),
                        pltpu.VMEM((tt, 512), F32),
                        pltpu.VMEM((tt, D), F32), pltpu.VMEM((tt, D), F32),
                        pltpu.VMEM((8, D), F32), pltpu.VMEM((8, D), F32), pltpu.VMEM((8, 512), F32)],
        compiler_params=_cparams(("arbitrary",)), name="mixer_bwd")(
            *([proj] * 7), dy, states, aux, prm["scw"], prm["cwx"], prm["cwbc"], prm["cbx"], prm["cbbc"], prm["dtb"],
            prm["alog"], prm["dskx"], prm["nrm"], prm["eh"], prm["eht"])


TN_IN = 1920


def layer_fwd_mix(x, lw, prm, tt):
    proj, h1 = norm_matmul(x, lw["nw1"], lw["win"], "in_proj")
    y, st, aux = mixer_fwd(proj, prm, tt)
    return h1, proj, (st, aux), y


def layer_fwd_mlp(x, mixed, lw):
    h1, proj, st, y = mixed
    x1 = matmul_residual(y, lw["wout"], x, False, "out_proj")
    u, h2 = norm_matmul(x1, lw["nw2"], lw["wup"], "up_proj")
    x2 = matmul_residual(u, lw["wdn"], x1, True, "down_proj")
    return x2, (x, h1, proj, st, y, x1, h2, u)


def layer_fwd(x, lw, prm, tt):
    return layer_fwd_mlp(x, layer_fwd_mix(x, lw, prm, tt), lw)


def _dw(a, b, a_cols, b_cols, relu2, name):
    m_len, n_len = a.shape[1], b.shape[1]
    n_a, n_b = m_len // a_cols, n_len // b_cols
    assert n_a == 1 or n_b == 1
    if n_b == 1:
        return matmul_tn(
            a, b,
            lambda t_: pl.BlockSpec((t_, a_cols), lambda n, t: (t, n)),
            lambda t_: pl.BlockSpec((t_, n_len), lambda n, t: (t, 0)),
            pl.BlockSpec((a_cols, n_len), lambda n, t: (n, 0)), SDS((m_len, n_len), BF16), n_a, relu2, name)
    return matmul_tn(
        a, b,
        lambda t_: pl.BlockSpec((t_, m_len), lambda n, t: (t, 0)),
        lambda t_: pl.BlockSpec((t_, b_cols), lambda n, t: (t, n)),
        pl.BlockSpec((m_len, b_cols), lambda n, t: (0, n)), SDS((m_len, n_len), BF16), n_b, relu2, name)


def layer_bwd_mlp(dx2, dx2b, lw, saved):
    _, _, _, _, y, x1, h2, u = saved
    du = matmul_nt_act(dx2b, lw["wdn"], u, "mlp_bwd_du")
    g_wdn = _dw(u, dx2b, 1024, D, True, "dw_down")
    dx1, dx1b, g_nw2 = matmul_nt_norm_bwd(du, lw["wup"], x1, lw["nw2"], dx2, "mlp_bwd_dx")
    cb = DFF // N_DEV
    g_wup = matmul_tn(
        h2, du,
        lambda t_: pl.BlockSpec((t_, D), lambda n, t: (t, 0)),
        lambda t_: pl.BlockSpec((t_, 2 * cb), lambda n, t: (t, n)),
        pl.BlockSpec((2, D, cb), lambda n, t: (n, 0, 0)), SDS((N_DEV, D, cb), BF16), N_DEV // 2, False, "dw_up")
    dy = matmul_nt_act(dx1b, lw["wout"], None, "out_bwd_dy")
    g_wout = _dw(y, dx1b, 1024, D, False, "dw_out")
    return dx1, dx1b, dy, {"wout": g_wout, "wup": g_wup, "wdn": g_wdn, "nw2": g_nw2[0]}


def layer_bwd_mix(dx1, dy, lw, prm, saved, tt):
    x, h1, proj, st = saved[:4]
    dproj, gscw, gcwx, gcwbc, gvec, gdt = mixer_bwd(proj, dy, st[0], st[1], prm, tt)
    dx0, dx0b, g_nw1 = matmul_nt_norm_bwd(dproj, lw["win"], x, lw["nw1"], dx1, "in_bwd_dx")
    g_win = _dw(h1, dproj, D, TN_IN, False, "dw_in")
    grads = {
        "win": g_win, "scw": gscw[0:3], "cw": jnp.concatenate([gcwx[0:4], gcwbc[0:4]], axis=1),
        "cb": jnp.concatenate([gvec[2], gcwbc[4]], axis=0),
        "dtb": gdt[0, :NHEAD], "alog": gdt[1, :NHEAD], "dsk": gdt[2, :NHEAD],
        "nrm": gvec[0], "nw1": g_nw1[0],
    }
    return dx0, dx0b, grads


def layer_bwd(dx2, dx2b, lw, prm, saved, tt):
    dx1, dx1b, dy, g_mlp = layer_bwd_mlp(dx2, dx2b, lw, saved)
    dx0, dx0b, g_mix = layer_bwd_mix(dx1, dy, lw, prm, saved, tt)
    return dx0, dx0b, {**g_mlp, **g_mix}


def layer_params(win, scw, cw, nw1, nw2, conv_b, dt_bias, a_log, d_skip, ssd_norm_w, eh, eht):
    def rows8(a):
        return jnp.pad(a, ((0, 8 - a.shape[0]), (0, 0)))

    def lanes128(a):
        return jnp.pad(a, (0, LANES - a.shape[0]))[None, :]

    lw = {"win": win, "nw1": nw1[None, :], "nw2": nw2[None, :]}
    prm = {"scw": rows8(scw), "cwx": rows8(cw[:, :D]), "cwbc": rows8(cw[:, D:]),
           "cbx": conv_b[None, :D], "cbbc": conv_b[None, D:],
           "dtb": lanes128(dt_bias), "alog": lanes128(a_log),
           "dskx": jnp.repeat(d_skip, HDIM)[None, :], "nrm": ssd_norm_w[None, :], "eh": eh, "eht": eht}
    return lw, prm


def _flip(v, bit):
    return 1 - v if bit else v


def all_gather(arrs, name):
    n = len(arrs)

    def body(*refs):
        ins, outs = refs[:n], refs[n:2 * n]
        send_sems, recv_sems, local_sems = refs[2 * n:]
        x, y, c = lax.axis_index("x"), lax.axis_index("y"), lax.axis_index("c")
        sibling = (x, y, 1 - c)
        chips = [(1 - x, y), (x, 1 - y), (1 - x, 1 - y)]

        def idx(px, py, pc):
            return 4 * px + 2 * py + pc

        def copy(a, k, block, to, src=None):
            dst = outs[a].at[idx(*block)]
            return pltpu.make_async_remote_copy(
                src_ref=dst if src is None else src, dst_ref=dst,
                send_sem=send_sems.at[a, k], recv_sem=recv_sems.at[a, k], device_id=to, device_id_type=MESH)

        me = (x, y, c)
        mine = [pltpu.make_async_copy(ins[a], outs[a].at[idx(*me)], local_sems.at[a]) for a in range(n)]
        for cp in mine:
            cp.start()
        first = []
        for a in range(n):
            first.append(copy(a, 0, me, sibling, src=ins[a]))
            first += [copy(a, 1 + j, me, (*chip, c), src=ins[a]) for j, chip in enumerate(chips)]
        for cp in first:
            cp.start()
        passed = []
        for j, chip in enumerate(chips):
            for a in range(n):
                copy(a, 1 + j, (*chip, c), me).wait_recv()
                cp = copy(a, 4 + j, (*chip, c), sibling)
                cp.start()
                passed.append(cp)
        for a in range(n):
            copy(a, 0, sibling, me).wait_recv()
            for j, chip in enumerate(chips):
                copy(a, 4 + j, (*chip, 1 - c), me).wait_recv()
        for cp in first + passed:
            cp.wait_send()
        for cp in mine:
            cp.wait()

    any_spec = pl.BlockSpec(memory_space=pl.ANY)
    return pl.pallas_call(
        body, in_specs=[any_spec] * n, out_specs=[any_spec] * n,
        out_shape=[SDS((N_DEV,) + a.shape, a.dtype) for a in arrs],
        scratch_shapes=[pltpu.SemaphoreType.DMA((n, 7)), pltpu.SemaphoreType.DMA((n, 7)),
                        pltpu.SemaphoreType.DMA((n,))],
        name=name)(*arrs)


HBM_SPEC = pl.BlockSpec(memory_space=pltpu.HBM)
SEM_SPEC = pl.BlockSpec(memory_space=pltpu.SEMAPHORE)
SIDE_EFFECT = pltpu.SideEffectType.DATAFLOW_SIDE_EFFECTING
N_PEER = N_DEV - 1


def _peer(mask):
    x, y, c = lax.axis_index("x"), lax.axis_index("y"), lax.axis_index("c")
    return _flip(x, mask & 4), _flip(y, mask & 2), _flip(c, mask & 1)


def exchange_start(srcs, per_peer, name, after=None):
    n = len(srcs)
    lands = [SDS((N_DEV,) + (a.shape[1:] if per_peer else a.shape), a.dtype) for a in srcs]
    n_in = 2 * n + (after is not None)

    def body(*refs):
        src_refs, land_refs = refs[:n], refs[n:2 * n]
        send_sems, recv_sems = refs[n_in], refs[n_in + 1]
        token = refs[-1]
        x, y, c = lax.axis_index("x"), lax.axis_index("y"), lax.axis_index("c")
        me = 4 * x + 2 * y + c
        for a in range(n):
            for mask in range(1, N_DEV):
                px, py, pc = _peer(mask)
                part = src_refs[a].at[4 * px + 2 * py + pc] if per_peer else src_refs[a]
                pltpu.make_async_remote_copy(
                    src_ref=part, dst_ref=land_refs[a].at[me], send_sem=send_sems.at[a * N_PEER + mask - 1],
                    recv_sem=recv_sems.at[a * N_PEER + mask - 1], device_id=(px, py, pc),
                    device_id_type=MESH).start()
        token[...] = jnp.zeros_like(token)

    out = pl.pallas_call(
        body, name=name,
        out_shape=(pltpu.SemaphoreType.DMA((n * N_PEER,)), pltpu.SemaphoreType.DMA((n * N_PEER,)),
                   *[pltpu.HBM(a.shape, a.dtype) for a in srcs], *[pltpu.HBM(l.shape, l.dtype) for l in lands],
                   SDS((8, LANES), F32)),
        in_specs=(HBM_SPEC,) * (2 * n) + ((pl.BlockSpec(memory_space=pl.ANY),) if after is not None else ()),
        out_specs=(SEM_SPEC, SEM_SPEC) + (HBM_SPEC,) * (2 * n) + (pl.BlockSpec(memory_space=pltpu.VMEM),),
        input_output_aliases={k: 2 + k for k in range(2 * n)},
        compiler_params=pltpu.CompilerParams(has_side_effects=SIDE_EFFECT),
    )(*[pltpu.with_memory_space_constraint(a, pltpu.HBM) for a in srcs],
      *[pltpu.with_memory_space_constraint(lax.empty(l.shape, l.dtype), pltpu.HBM) for l in lands],
      *([after] if after is not None else []))
    return out[0], out[1], list(out[2:2 + n]), list(out[2 + n:2 + 2 * n]), out[-1]


def exchange_wait(started, after, per_peer, name):
    send_sems, recv_sems, srcs, lands, _ = started
    n = len(srcs)

    def body(*refs):
        src_refs, land_refs = refs[:n], refs[n:2 * n]
        send_sems, recv_sems = refs[2 * n], refs[2 * n + 1]
        for mask in range(1, N_DEV):
            for a in range(n):
                copy = pltpu.make_async_remote_copy(
                    src_ref=src_refs[a].at[0] if per_peer else src_refs[a], dst_ref=land_refs[a].at[0],
                    send_sem=send_sems.at[a * N_PEER + mask - 1], recv_sem=recv_sems.at[a * N_PEER + mask - 1],
                    device_id=_peer(mask), device_id_type=MESH)
                copy.wait_send()
                copy.wait_recv()

    out = pl.pallas_call(
        body, name=name,
        out_shape=tuple(pltpu.HBM(a.shape, a.dtype) for a in srcs + lands),
        in_specs=(HBM_SPEC,) * (2 * n) + (SEM_SPEC, SEM_SPEC, pl.BlockSpec(memory_space=pl.ANY)),
        out_specs=(HBM_SPEC,) * (2 * n), input_output_aliases={k: k for k in range(2 * n)},
        compiler_params=pltpu.CompilerParams(has_side_effects=SIDE_EFFECT),
    )(*srcs, *lands, send_sems, recv_sems, after)
    return list(out[:n]), list(out[n:])


IN_SHARD = NIN // N_DEV
SLOT_W = 768


def _slot_window(j):
    return (IN_SHARD * j // LANES) * LANES, -(-(IN_SHARD * (j + 1)) // LANES) * LANES


def _placement(j):
    a, b = _slot_window(j)
    r = lax.broadcasted_iota(jnp.int32, (SLOT_W, b - a), 0)
    c = lax.broadcasted_iota(jnp.int32, (SLOT_W, b - a), 1)
    return jnp.where(jnp.logical_and(c == r + (IN_SHARD * j - a), r < IN_SHARD), 1.0, 0.0).astype(BF16)


def assemble_w_in(land):
    tm = 256

    def body(l_ref, o_ref, acc):
        acc[...] = jnp.zeros_like(acc)
        for j in range(N_DEV):
            a, b = _slot_window(j)
            acc[:, a:b] += _nn(l_ref[j], _placement(j))
        o_ref[...] = acc[...].astype(BF16)

    return pl.pallas_call(
        body, grid=(D // tm,),
        in_specs=[pl.BlockSpec((N_DEV, tm, SLOT_W), lambda i: (0, i, 0))],
        out_specs=pl.BlockSpec((tm, NINP), lambda i: (i, 0)),
        out_shape=SDS((D, NINP), BF16),
        scratch_shapes=[pltpu.VMEM((tm, NINP), F32)],
        compiler_params=_cparams(("parallel",)), name="assemble_w_in")(land)


def scatter_w_in(dw):
    tm = 256

    def body(d_ref, o_ref):
        for j in range(N_DEV):
            a, b = _slot_window(j)
            o_ref[j] = _nt(d_ref[:, a:b], _placement(j)).astype(BF16)

    return pl.pallas_call(
        body, grid=(D // tm,),
        in_specs=[pl.BlockSpec((tm, NINP), lambda i: (i, 0))],
        out_specs=pl.BlockSpec((N_DEV, tm, SLOT_W), lambda i: (0, i, 0)),
        out_shape=SDS((N_DEV, D, SLOT_W), BF16),
        compiler_params=_cparams(("parallel",)), name="scatter_w_in")(dw)


def _adamw_math(g, w_ref, m_ref, v_ref, g_ref, d_ref, nm_ref, nv_ref):
    mn = ADAM_B1 * m_ref[...] + (1.0 - ADAM_B1) * g
    vn = ADAM_B2 * v_ref[...] + (1.0 - ADAM_B2) * jnp.square(g)
    m_hat = mn / (1.0 - ADAM_B1 ** ADAM_STEP)
    v_hat = vn / (1.0 - ADAM_B2 ** ADAM_STEP)
    g_ref[...] = g
    d_ref[...] = -ADAM_LR * (m_hat / (jnp.sqrt(v_hat) + ADAM_EPS) + ADAM_WD * w_ref[...])
    nm_ref[...] = mn
    nv_ref[...] = vn


def adamw_layers(w, slots, m, v, name):
    depth, r_len, c_len = w.shape
    cs = slots[0].shape[2]
    br = min(128, r_len)
    assert r_len % br == 0

    def body(w_ref, *rest):
        s_refs, (m_ref, v_ref, g_ref, d_ref, nm_ref, nv_ref) = rest[:depth], rest[depth:]
        layer = pl.program_id(0)
        for k in range(depth):
            @pl.when(layer == k)
            def _(k=k):
                g = s_refs[k][0, :, 0:c_len].astype(F32)
                for j in range(1, N_DEV):
                    g = g + s_refs[k][j, :, 0:c_len].astype(F32)
                _adamw_math(g, w_ref, m_ref, v_ref, g_ref, d_ref, nm_ref, nv_ref)

    spec = pl.BlockSpec((None, br, c_len), lambda l, i: (l, i, 0))
    s_specs = [pl.BlockSpec((N_DEV, br, cs), lambda l, i, k=k: (0, jnp.where(l == k, i, 0), 0))
               for k in range(depth)]
    return pl.pallas_call(
        body, grid=(depth, r_len // br),
        in_specs=[spec] + s_specs + [spec, spec],
        out_specs=[spec] * 4, out_shape=[SDS(w.shape, F32)] * 4,
        compiler_params=_cparams(("arbitrary", "arbitrary")), name=name)(w, *slots, m, v)


def adamw(w, slots, m, v, name):
    r_len, c_len = w.shape
    br = r_len if r_len <= 512 else 512
    assert r_len % br == 0

    def body(w_ref, s_ref, m_ref, v_ref, g_ref, d_ref, nm_ref, nv_ref):
        g = s_ref[0].astype(F32)
        for k in range(1, N_DEV):
            g = g + s_ref[k].astype(F32)
        _adamw_math(g, w_ref, m_ref, v_ref, g_ref, d_ref, nm_ref, nv_ref)

    spec = pl.BlockSpec((br, c_len), lambda i: (i, 0))
    return pl.pallas_call(
        body, grid=(r_len // br,),
        in_specs=[spec, pl.BlockSpec((N_DEV, br, c_len), lambda i: (0, i, 0)), spec, spec],
        out_specs=[spec] * 4, out_shape=[SDS((r_len, c_len), F32)] * 4,
        compiler_params=_cparams(("parallel",)), name=name)(w, slots, m, v)


def _adamw_nd(w, slots, m, v, name):
    shp = w.shape
    r = int(np.prod(shp[:-1]))
    outs = adamw(w.reshape(r, shp[-1]), slots.reshape(N_DEV, r, shp[-1]), m.reshape(r, shp[-1]),
                 v.reshape(r, shp[-1]), name)
    return [o.reshape(shp) for o in outs]


SMALL = [("norm_mix_w", DEPTH * D), ("ssd_conv_b", DEPTH * XBC), ("dt_bias", DEPTH * NHEAD),
         ("a_log", DEPTH * NHEAD), ("d_skip", DEPTH * NHEAD), ("ssd_norm_w", DEPTH * D),
         ("norm_mlp_w", DEPTH * D), ("final_norm_w", D)]
SMALL_LEN = sum(s for _, s in SMALL)
SMALL_ROWS = -(-SMALL_LEN // LANES)


def _pack_small(parts):
    flat = jnp.concatenate([parts[k].reshape(-1) for k, _ in SMALL])
    return jnp.pad(flat, (0, SMALL_ROWS * LANES - SMALL_LEN)).reshape(SMALL_ROWS, LANES)


def _unpack_small(packed, shapes):
    flat = packed.reshape(-1)
    out, off = {}, 0
    for k, s in SMALL:
        out[k] = flat[off:off + s].reshape(shapes[k])
        off += s
    return out


def kernel(x, norm_mix_w, w_in, short_conv_w, ssd_conv_w, ssd_conv_b, dt_bias, a_log, d_skip, ssd_norm_w, w_out, norm_mlp_w, w_up, w_down, final_norm_w, loss_target, m_norm_mix_w, m_w_in, m_short_conv_w, m_ssd_conv_w, m_ssd_conv_b, m_dt_bias, m_a_log, m_d_skip, m_ssd_norm_w, m_w_out, m_norm_mlp_w, m_w_up, m_w_down, m_final_norm_w, v_norm_mix_w, v_w_in, v_short_conv_w, v_ssd_conv_w, v_ssd_conv_b, v_dt_bias, v_a_log, v_d_skip, v_ssd_norm_w, v_w_out, v_norm_mlp_w, v_w_up, v_w_down, v_final_norm_w):
    xs = x[0]
    t_len = xs.shape[0]
    tt = min(256, t_len)
    eh, eht = _head_matrices()
    me = 4 * lax.axis_index("x") + 2 * lax.axis_index("y") + lax.axis_index("c")

    def start_weights(i, after):
        first = exchange_start(
            [jnp.pad(w_in[i].astype(BF16), ((0, 0), (0, SLOT_W - IN_SHARD))), short_conv_w[i], ssd_conv_w[i]],
            False, "w_in_start_%d" % i, after)
        rest = exchange_start([w_out[i].astype(BF16), w_up[i].astype(BF16), w_down[i].astype(BF16)], False,
                              "w_rest_start_%d" % i, first[4] if after is None else after)
        return first, rest

    def fill_own(srcs, lands, per_peer):
        own = [lax.dynamic_index_in_dim(s_, me, 0, keepdims=False) for s_ in srcs] if per_peer else srcs
        return [lax.dynamic_update_index_in_dim(l_, o_, me, 0) for l_, o_ in zip(lands, own)]

    act = xs
    saved, layers = [], []
    first, rest = start_weights(0, None)
    token = first[4][0, 0] + rest[4][0, 0]
    for i in range(DEPTH):
        g_in, g_sc, g_cw = fill_own(*exchange_wait(first, act, False, "w_in_wait_%d" % i), False)
        lw, prm = layer_params(
            assemble_w_in(g_in), g_sc.transpose(1, 0, 2).reshape(3, D), g_cw.transpose(1, 0, 2).reshape(4, XBC),
            norm_mix_w[i], norm_mlp_w[i], ssd_conv_b[i], dt_bias[i], a_log[i], d_skip[i], ssd_norm_w[i], eh, eht)
        lw["nw1"] = lw["nw1"] + token
        mixed = layer_fwd_mix(act, lw, prm, tt)
        g_out, g_up, g_dn = fill_own(*exchange_wait(rest, mixed[3], False, "w_rest_wait_%d" % i), False)
        lw.update(wout=g_out.reshape(MIX, D), wup=g_up, wdn=g_dn.reshape(DFF, D))
        if i + 1 < DEPTH:
            first, rest = start_weights(i + 1, g_dn)
            token = first[4][0, 0] + rest[4][0, 0]
            lw["nw2"] = lw["nw2"] + token
        layers.append((lw, prm))
        act, sv = layer_fwd_mlp(act, mixed, lw)
        saved.append(sv)
    loss_acc, dx, dxb, g_fw = loss_head(act, final_norm_w[None, :], loss_target[0])

    grads = [None] * DEPTH
    sent_rest, sent_in = [None] * DEPTH, [None] * DEPTH
    token = None
    for i in reversed(range(DEPTH)):
        lw, prm = layers[i]
        if token is not None:
            lw = dict(lw, nw2=lw["nw2"] + token)
        dx1, _, dy, g_mlp = layer_bwd_mlp(dx, dxb, lw, saved[i])
        sent_rest[i] = exchange_start(
            [g_mlp["wout"].reshape(N_DEV, MIX // N_DEV, D), g_mlp["wup"], g_mlp["wdn"].reshape(N_DEV, DFF // N_DEV, D)],
            True, "g_rest_start_%d" % i)
        dx, dxb, g_mix = layer_bwd_mix(dx1, dy, lw, dict(prm, nrm=prm["nrm"] + sent_rest[i][4][0, 0]), saved[i], tt)
        grads[i] = {**g_mlp, **g_mix}
        if i > 0:
            sent_in[i] = exchange_start([scatter_w_in(g_mix["win"])], True, "g_in_start_%d" % i)
            token = sent_in[i][4][0, 0]

    def stack(k):
        return jnp.stack([g[k] for g in grads])

    small = _pack_small({"norm_mix_w": stack("nw1"), "ssd_conv_b": stack("cb"), "dt_bias": stack("dtb"),
                         "a_log": stack("alog"), "d_skip": stack("dsk"), "ssd_norm_w": stack("nrm"),
                         "norm_mlp_w": stack("nw2"), "final_norm_w": g_fw[0]})
    r_small, r_sc, r_cw = all_gather([small, stack("scw"), stack("cw")], "gather_small_grads")
    r_sc = lax.dynamic_slice_in_dim(r_sc, me * (D // N_DEV), D // N_DEV, axis=3)
    r_cw = lax.dynamic_slice_in_dim(r_cw, me * (XBC // N_DEV), XBC // N_DEV, axis=3)
    sent_in[0] = exchange_start([scatter_w_in(grads[0]["win"])], True, "g_in_start_0", after=r_small)

    after = sent_in[0][4]
    recv = [fill_own(*exchange_wait(sent_rest[i], after, True, "g_rest_wait_%d" % i), True) for i in range(DEPTH)]
    res = {}
    res["w_out"] = adamw_layers(w_out, [r[0] for r in recv], m_w_out, v_w_out, "adamw_w_out")
    res["w_up"] = adamw_layers(w_up, [r[1] for r in recv], m_w_up, v_w_up, "adamw_w_up")
    res["w_down"] = adamw_layers(w_down, [r[2] for r in recv], m_w_down, v_w_down, "adamw_w_down")
    after = res["w_down"][1]
    recv_in = [fill_own(*exchange_wait(sent_in[i], after, True, "g_in_wait_%d" % i), True)[0] for i in range(DEPTH)]
    res["w_in"] = adamw_layers(w_in, recv_in, m_w_in, v_w_in, "adamw_w_in")
    res["short_conv_w"] = _adamw_nd(short_conv_w, r_sc, m_short_conv_w, v_short_conv_w, "adamw_short_conv")
    res["ssd_conv_w"] = _adamw_nd(ssd_conv_w, r_cw, m_ssd_conv_w, v_ssd_conv_w, "adamw_ssd_conv")
    small_w = {"norm_mix_w": norm_mix_w, "ssd_conv_b": ssd_conv_b, "dt_bias": dt_bias, "a_log": a_log,
               "d_skip": d_skip, "ssd_norm_w": ssd_norm_w, "norm_mlp_w": norm_mlp_w, "final_norm_w": final_norm_w}
    small_m = {"norm_mix_w": m_norm_mix_w, "ssd_conv_b": m_ssd_conv_b, "dt_bias": m_dt_bias, "a_log": m_a_log,
               "d_skip": m_d_skip, "ssd_norm_w": m_ssd_norm_w, "norm_mlp_w": m_norm_mlp_w,
               "final_norm_w": m_final_norm_w}
    small_v = {"norm_mix_w": v_norm_mix_w, "ssd_conv_b": v_ssd_conv_b, "dt_bias": v_dt_bias, "a_log": v_a_log,
               "d_skip": v_d_skip, "ssd_norm_w": v_ssd_norm_w, "norm_mlp_w": v_norm_mlp_w,
               "final_norm_w": v_final_norm_w}
    shapes = {k: a.shape for k, a in small_w.items()}
    packed = adamw(_pack_small(small_w), r_small, _pack_small(small_m), _pack_small(small_v), "adamw_small")
    unpacked = [_unpack_small(p, shapes) for p in packed]
    for k in small_w:
        res[k] = [u[k] for u in unpacked]

    loss = lax.psum(loss_acc[0, 0], ("x", "y", "c"))
    order = ["norm_mix_w", "w_in", "short_conv_w", "ssd_conv_w", "ssd_conv_b", "dt_bias", "a_log", "d_skip",
             "ssd_norm_w", "w_out", "norm_mlp_w", "w_up", "w_down", "final_norm_w"]
    out = [loss, dx[None]]
    for part in range(4):
        out += [res[k][part] for k in order]
    return tuple(out)
```

```python
import functools

import numpy as np
import jax
import jax.numpy as jnp
from jax import lax
from jax.experimental import pallas as pl
from jax.experimental.pallas import tpu as pltpu

F32 = jnp.float32
BF16 = jnp.bfloat16
SDS = jax.ShapeDtypeStruct

N_DEV = 8
DEPTH = 4
D = 1024
NIN = 5648
NINP = 5760
DFF = 4096
MIX = 2048
NHEAD = 16
HDIM = 64
NSTATE = 128
CHUNK = 64
XBC = 1536
EPS = 1e-5
LANES = 128

C_UB, C_UC, C_UH, C_Z, C_XS, C_BC, C_DT = 0, 1024, 2048, 3072, 4096, 5120, 5632
A_CV, A_YS, A_PX, A_PBC, AUX_W = 0, 1024, 2048, 3072, 3584

ADAM_LR = 0.001
ADAM_B1 = 0.9
ADAM_B2 = 0.999
ADAM_EPS = 1e-08
ADAM_WD = 0.01
ADAM_STEP = 10

VMEM_LIMIT = 56 * 1024 * 1024
MESH = pl.DeviceIdType.MESH


def _cparams(sem):
    return pltpu.CompilerParams(dimension_semantics=sem, vmem_limit_bytes=VMEM_LIMIT)


def _nt(a, b):
    return lax.dot_general(a, b, (((1,), (1,)), ((), ())), preferred_element_type=F32)


def _tn(a, b):
    return lax.dot_general(a, b, (((0,), (0,)), ((), ())), preferred_element_type=F32)


def _nn(a, b):
    return jnp.dot(a, b, preferred_element_type=F32)


def _sigmoid(v):
    return 0.5 * jnp.tanh(0.5 * v) + 0.5


def _split3(v):
    v1 = v.astype(BF16)
    r1 = v - v1.astype(F32)
    v2 = r1.astype(BF16)
    v3 = (r1 - v2.astype(F32)).astype(BF16)
    return v1, v2, v3


def _expand(v, eh):
    v1, v2, v3 = _split3(v)
    return _nn(v1, eh) + _nn(v2, eh) + _nn(v3, eh)


def _head_reduce(v, eht):
    v1 = v.astype(BF16)
    v2 = (v - v1.astype(F32)).astype(BF16)
    return _nn(v1, eht) + _nn(v2, eht)


def _head_matrices():
    eh = np.zeros((LANES, D), np.float32)
    for h in range(NHEAD):
        eh[h, h * HDIM:(h + 1) * HDIM] = 1.0
    return jnp.asarray(eh, BF16), jnp.asarray(eh.T.copy(), BF16)


def _resident(shape):
    return pl.BlockSpec(shape, lambda *_: (0,) * len(shape), pipeline_mode=pl.Buffered(1))


def _col_chunks(n, step):
    return [(c, min(c + step, n)) for c in range(0, n, step)]


def norm_matmul(x, nw, w, name):
    t_len = x.shape[0]
    blocked = w.ndim == 3
    n_len = w.shape[0] * w.shape[2] if blocked else w.shape[1]
    tm = min(512, t_len)
    chunks = _col_chunks(n_len, n_len // N_DEV if blocked else 1536)

    def body(x_ref, nw_ref, w_ref, o_ref, h_ref):
        xv = x_ref[...]
        r = lax.rsqrt(jnp.mean(xv * xv, axis=-1, keepdims=True) + EPS)
        hv = (xv * r * nw_ref[...]).astype(BF16)
        h_ref[...] = hv
        for j, (c0, c1) in enumerate(chunks):
            wj = w_ref[j] if blocked else w_ref[:, c0:c1]
            o_ref[:, c0:c1] = _nn(hv, wj).astype(o_ref.dtype)

    return pl.pallas_call(
        body, grid=(t_len // tm,),
        in_specs=[pl.BlockSpec((tm, D), lambda i: (i, 0)), _resident((1, D)), _resident(w.shape)],
        out_specs=[pl.BlockSpec((tm, n_len), lambda i: (i, 0)),
                   pl.BlockSpec((tm, D), lambda i: (i, 0))],
        out_shape=[SDS((t_len, n_len), BF16), SDS((t_len, D), BF16)],
        compiler_params=_cparams(("parallel",)), name=name)(x, nw, w)


def matmul_residual(a, w, res, relu2, name):
    t_len, k_len = a.shape
    tm = min(512, t_len)

    def body(a_ref, w_ref, res_ref, o_ref):
        av = a_ref[...]
        if relu2:
            af = jnp.maximum(av.astype(F32), 0.0)
            av = (af * af).astype(BF16)
        o_ref[...] = res_ref[...] + _nn(av, w_ref[...])

    return pl.pallas_call(
        body, grid=(t_len // tm,),
        in_specs=[pl.BlockSpec((tm, k_len), lambda i: (i, 0)),
                  _resident((k_len, D)),
                  pl.BlockSpec((tm, D), lambda i: (i, 0))],
        out_specs=pl.BlockSpec((tm, D), lambda i: (i, 0)),
        out_shape=SDS((t_len, D), F32),
        compiler_params=_cparams(("parallel",)), name=name)(a, w, res)


def matmul_nt_act(dy, w, u, name):
    t_len = dy.shape[0]
    n_len = w.shape[0]
    tm = min(512, t_len)
    chunks = _col_chunks(n_len, 1024)

    def body(dy_ref, w_ref, *rest):
        if u is None:
            (o_ref,) = rest
        else:
            u_ref, o_ref = rest
        dyv = dy_ref[...]
        for c0, c1 in chunks:
            p = _nt(dyv, w_ref[c0:c1, :])
            if u is not None:
                p = p * (2.0 * jnp.maximum(u_ref[:, c0:c1].astype(F32), 0.0))
            o_ref[:, c0:c1] = p.astype(o_ref.dtype)

    in_specs = [pl.BlockSpec((tm, D), lambda i: (i, 0)), _resident((n_len, D))]
    args = [dy, w]
    if u is not None:
        in_specs.append(pl.BlockSpec((tm, n_len), lambda i: (i, 0)))
        args.append(u)
    return pl.pallas_call(
        body, grid=(t_len // tm,),
        in_specs=in_specs,
        out_specs=pl.BlockSpec((tm, n_len), lambda i: (i, 0)),
        out_shape=SDS((t_len, n_len), BF16),
        compiler_params=_cparams(("parallel",)), name=name)(*args)


def matmul_tn(a, b, a_spec, b_spec, o_spec, o_shape, n_out, relu2, name):
    t_len = a.shape[0]
    tt = min(2048, t_len)
    nt = t_len // tt

    def body(a_ref, b_ref, o_ref, acc):
        t = pl.program_id(1)
        av = a_ref[...]
        if relu2:
            af = jnp.maximum(av.astype(F32), 0.0)
            av = (af * af).astype(BF16)
        p = _tn(av, b_ref[...])

        @pl.when(t == 0)
        def _():
            acc[...] = p

        @pl.when(t > 0)
        def _():
            acc[...] += p

        @pl.when(t == nt - 1)
        def _():
            if len(blk) == 3:
                for j in range(blk[0]):
                    o_ref[j] = acc[:, j * blk[2]:(j + 1) * blk[2]].astype(o_ref.dtype)
            else:
                o_ref[...] = acc[...].astype(o_ref.dtype)

    blk = tuple(o_spec.block_shape)
    acc_shape = (blk[1], blk[0] * blk[2]) if len(blk) == 3 else blk
    return pl.pallas_call(
        body, grid=(n_out, nt),
        in_specs=[a_spec(tt), b_spec(tt)],
        out_specs=o_spec, out_shape=o_shape,
        scratch_shapes=[pltpu.VMEM(acc_shape, F32)],
        compiler_params=_cparams(("parallel", "arbitrary")), name=name)(a, b)


def matmul_nt_norm_bwd(dy, w, x, nw, dres, name):
    t_len = x.shape[0]
    blocked = w.ndim == 3
    k_len = dy.shape[1]
    kb = k_len // N_DEV
    tm = min(512, t_len)

    def body(dy_ref, w_ref, x_ref, nw_ref, dres_ref, dx_ref, dxb_ref, dnw_ref):
        @pl.when(pl.program_id(0) == 0)
        def _():
            dnw_ref[...] = jnp.zeros_like(dnw_ref)

        if blocked:
            dh = _nt(dy_ref[:, 0:kb], w_ref[0])
            for j in range(1, N_DEV):
                dh = dh + _nt(dy_ref[:, j * kb:(j + 1) * kb], w_ref[j])
        else:
            dh = _nt(dy_ref[...], w_ref[...])
        xv = x_ref[...]
        r = lax.rsqrt(jnp.mean(xv * xv, axis=-1, keepdims=True) + EPS)
        xh = xv * r
        dnw_ref[0:1, :] += jnp.sum(dh * xh, axis=0, keepdims=True)
        g = dh * nw_ref[...]
        dx = dres_ref[...] + r * (g - xh * jnp.mean(g * xh, axis=-1, keepdims=True))
        dx_ref[...] = dx
        dxb_ref[...] = dx.astype(BF16)

    return pl.pallas_call(
        body, grid=(t_len // tm,),
        in_specs=[pl.BlockSpec((tm, k_len), lambda i: (i, 0)),
                  _resident(w.shape),
                  pl.BlockSpec((tm, D), lambda i: (i, 0)),
                  _resident((1, D)),
                  pl.BlockSpec((tm, D), lambda i: (i, 0))],
        out_specs=[pl.BlockSpec((tm, D), lambda i: (i, 0)),
                   pl.BlockSpec((tm, D), lambda i: (i, 0)),
                   pl.BlockSpec((8, D), lambda i: (0, 0))],
        out_shape=[SDS((t_len, D), F32), SDS((t_len, D), BF16), SDS((8, D), F32)],
        compiler_params=_cparams(("arbitrary",)), name=name)(dy, w, x, nw, dres)


def loss_head(x, fw, tgt):
    t_len = x.shape[0]
    tm = min(512, t_len)

    def body(x_ref, fw_ref, t_ref, loss_ref, dx_ref, dxb_ref, dfw_ref):
        @pl.when(pl.program_id(0) == 0)
        def _():
            loss_ref[...] = jnp.zeros_like(loss_ref)
            dfw_ref[...] = jnp.zeros_like(dfw_ref)
        xv = x_ref[...]
        r = lax.rsqrt(jnp.mean(xv * xv, axis=-1, keepdims=True) + EPS)
        xh = xv * r
        w = fw_ref[...]
        e = xh * w - t_ref[...]
        row = jnp.sum(e * e, axis=-1, keepdims=True) * (1.0 / D)
        loss_ref[...] += 0.5 * jnp.sum(row, axis=0, keepdims=True)
        dyf = e * (1.0 / D)
        dfw_ref[0:1, :] += jnp.sum(dyf * xh, axis=0, keepdims=True)
        g = dyf * w
        dx = r * (g - xh * jnp.mean(g * xh, axis=-1, keepdims=True))
        dx_ref[...] = dx
        dxb_ref[...] = dx.astype(BF16)

    return pl.pallas_call(
        body, grid=(t_len // tm,),
        in_specs=[pl.BlockSpec((tm, D), lambda i: (i, 0)),
                  pl.BlockSpec((1, D), lambda i: (0, 0)),
                  pl.BlockSpec((tm, D), lambda i: (i, 0))],
        out_specs=[pl.BlockSpec((8, LANES), lambda i: (0, 0)),
                   pl.BlockSpec((tm, D), lambda i: (i, 0)),
                   pl.BlockSpec((tm, D), lambda i: (i, 0)),
                   pl.BlockSpec((8, D), lambda i: (0, 0))],
        out_shape=[SDS((8, LANES), F32), SDS((t_len, D), F32), SDS((t_len, D), BF16), SDS((8, D), F32)],
        compiler_params=_cparams(("arbitrary",)), name="loss_head")(x, fw, tgt)


def _shift_dn(x, halo, j):
    if j == 0:
        return x
    xr = pltpu.roll(x, j, 0)
    hr = pltpu.roll(halo, j, 0)
    row = lax.broadcasted_iota(jnp.int32, hr.shape, 0)
    top = jnp.where(row < j, hr, xr[0:8])
    return jnp.concatenate([top, xr[8:]], axis=0)


def _shift_up(x, nxt, j):
    if j == 0:
        return x
    n = x.shape[0]
    xr = pltpu.roll(x, n - j, 0)
    hr = pltpu.roll(nxt, 8 - j, 0)
    row = lax.broadcasted_iota(jnp.int32, hr.shape, 0)
    bot = jnp.where(row >= 8 - j, hr, xr[n - 8:n])
    return jnp.concatenate([xr[:n - 8], bot], axis=0)


def _conv_fwd(x, halo, w_ref, kw):
    acc = None
    for k in range(kw):
        term = w_ref[k:k + 1, :] * _shift_dn(x, halo, kw - 1 - k)
        acc = term if acc is None else acc + term
    return acc


def _chunk_cumsum(a, pos):
    for sh in (1, 2, 4, 8, 16, 32):
        a = a + jnp.where(pos >= sh, pltpu.roll(a, sh, 0), 0.0)
    return a


def _chunk_rcumsum(a, pos):
    n = a.shape[0]
    for sh in (1, 2, 4, 8, 16, 32):
        a = a + jnp.where(pos < CHUNK - sh, pltpu.roll(a, n - sh, 0), 0.0)
    return a


def _softplus(v):
    return jnp.maximum(v, 0.0) + jnp.log(1.0 + jnp.exp(-jnp.abs(v)))


def _silu(v):
    return v * _sigmoid(v)


def _dsilu(v):
    s = _sigmoid(v)
    return s * (1.0 + v * (1.0 - s))


def _lane_masks(width=D):
    lane = lax.broadcasted_iota(jnp.int32, (CHUNK, width), 1) & (HDIM - 1)
    row = lax.broadcasted_iota(jnp.int32, (CHUNK, width), 0)
    return lane == row, lane <= row


def _rep_matrix():
    lane = lax.broadcasted_iota(jnp.int32, (CHUNK, 512), 1) & (HDIM - 1)
    row = lax.broadcasted_iota(jnp.int32, (CHUNK, 512), 0)
    return jnp.where(lane == row, 1.0, 0.0).astype(BF16)


def _blockdiag(xp):
    lane = lax.broadcasted_iota(jnp.int32, xp.shape, 1)
    zero = jnp.zeros_like(xp)
    return jnp.concatenate([jnp.where(lane < HDIM, xp, zero), jnp.where(lane >= HDIM, xp, zero)], axis=0)


def _mixer_views(tt):
    r8 = tt // 8

    def main(width, col):
        return pl.BlockSpec((tt, width), lambda i, c=col // width: (i, c))

    def halo(width, col):
        return pl.BlockSpec((8, width), lambda i, c=col // width: (jnp.maximum(i * r8 - 1, 0), c))

    return main, halo


def mixer_fwd(proj, prm, tt):
    t_len = proj.shape[0]
    nblk = t_len // tt
    nc = tt // CHUNK
    main, halo = _mixer_views(tt)

    def body(ub_ref, uc_ref, uh_ref, z_ref, xr_ref, bcr_ref, dtr_ref, uch_ref, uhh_ref, xrh_ref, bcrh_ref,
             scw_ref, cwx_ref, cwbc_ref, cbx_ref, cbbc_ref, dtb_ref, alog_ref, dsk_ref, nrm_ref, eh_ref,
             y_ref, st_ref, aux_ref, hs, xs_s, bc_s, dtx_s, cumx_s, yssd_s):
        i = pl.program_id(0)
        first = i == 0

        @pl.when(first)
        def _():
            hs[...] = jnp.zeros_like(hs)

        keep = jnp.where(first, 0.0, 1.0)
        v = uc_ref[...].astype(F32) * uh_ref[...].astype(F32)
        vh = uch_ref[...].astype(F32) * uhh_ref[...].astype(F32) * keep
        cv = _conv_fwd(v, vh, scw_ref, 3)
        aux_ref[:, A_CV:A_CV + D] = cv.astype(BF16)
        y_ref[:, 0:D] = (ub_ref[...].astype(F32) * cv).astype(BF16)

        pre_x = _conv_fwd(xr_ref[...].astype(F32), xrh_ref[...].astype(F32) * keep, cwx_ref, 4) + cbx_ref[...]
        aux_ref[:, A_PX:A_PX + D] = pre_x.astype(BF16)
        xs_s[...] = _silu(pre_x)
        pre_bc = _conv_fwd(bcr_ref[...].astype(F32), bcrh_ref[...].astype(F32) * keep, cwbc_ref, 4) + cbbc_ref[...]
        aux_ref[:, A_PBC:A_PBC + 512] = pre_bc.astype(BF16)
        bc_s[...] = _silu(pre_bc)
        dt = _softplus(dtr_ref[...].astype(F32) + dtb_ref[...])
        a_neg = -jnp.exp(alog_ref[...])
        pos = lax.broadcasted_iota(jnp.int32, (tt, LANES), 0) & (CHUNK - 1)
        cum = _chunk_cumsum(dt * a_neg, pos)
        eh = eh_ref[...]
        dtx_s[...] = _expand(dt, eh)
        cumx_s[...] = _expand(cum, eh)
        irep, causal = _lane_masks()
        rep = _rep_matrix()

        def chunk(c, carry):
            r0 = pl.multiple_of(c * CHUNK, CHUNK)
            rows = pl.ds(r0, CHUNK)
            cumx = cumx_s[rows, :]
            cum_l = cumx[CHUNK - 1:CHUNK, :]
            xd = xs_s[rows, :] * dtx_s[rows, :]
            xf = xd * jnp.exp(cum_l - cumx)
            ex = jnp.exp(cumx)
            e_l = jnp.exp(cum_l)
            rvec = jnp.sum(jnp.where(irep, cumx, 0.0), axis=0, keepdims=True)
            lam = jnp.where(causal, jnp.exp(jnp.where(causal, cumx - rvec, 0.0)), 0.0)
            bc = bc_s[rows, :]
            for g in range(2):
                gs = slice(g * 512, (g + 1) * 512)
                bg = bc[:, g * NSTATE:(g + 1) * NSTATE].astype(BF16)
                cg = bc[:, 256 + g * NSTATE:256 + (g + 1) * NSTATE].astype(BF16)
                s_rep = _nn(_nt(cg, bg).astype(BF16), rep)
                m_g = (s_rep * lam[:, gs]).astype(BF16)
                h_g = hs[:, gs]
                h_b = h_g.astype(BF16)
                st_ref[c, :, gs] = h_b
                yo = _nn(cg, h_b) * ex[:, gs]
                xd_b = xd[:, gs].astype(BF16)
                for hp in range(4):
                    ps = slice(hp * LANES, (hp + 1) * LANES)
                    yd = _nn(m_g[:, ps], _blockdiag(xd_b[:, ps]))
                    yssd_s[rows, g * 512 + hp * LANES:g * 512 + (hp + 1) * LANES] = yd + yo[:, ps]
                hs[:, gs] = h_g * e_l[:, gs] + _tn(bg, xf[:, gs].astype(BF16))
            return carry

        lax.fori_loop(0, nc, chunk, 0)

        ys = yssd_s[...] + dsk_ref[...] * xs_s[...]
        aux_ref[:, A_YS:A_YS + D] = ys.astype(BF16)
        gt = ys * _silu(z_ref[...].astype(F32))
        for g in range(2):
            gs = slice(g * 512, (g + 1) * 512)
            gg = gt[:, gs]
            rn = lax.rsqrt(jnp.mean(gg * gg, axis=-1, keepdims=True) + EPS)
            y_ref[:, D + g * 512:D + (g + 1) * 512] = (gg * rn * nrm_ref[:, gs]).astype(BF16)

    def const(shape):
        return pl.BlockSpec(shape, lambda i: (0, 0))

    in_specs = [main(D, C_UB), main(D, C_UC), main(D, C_UH), main(D, C_Z), main(D, C_XS), main(512, C_BC),
                main(LANES, C_DT), halo(D, C_UC), halo(D, C_UH), halo(D, C_XS), halo(512, C_BC),
                const((8, D)), const((8, D)), const((8, 512)), const((1, D)), const((1, 512)),
                const((1, LANES)), const((1, LANES)), const((1, D)), const((1, D)), const((LANES, D))]
    return pl.pallas_call(
        body, grid=(nblk,),
        in_specs=in_specs,
        out_specs=[pl.BlockSpec((tt, MIX), lambda i: (i, 0)),
                   pl.BlockSpec((nc, NSTATE, D), lambda i: (i, 0, 0)),
                   pl.BlockSpec((tt, AUX_W), lambda i: (i, 0))],
        out_shape=[SDS((t_len, MIX), BF16), SDS((t_len // CHUNK, NSTATE, D), BF16), SDS((t_len, AUX_W), BF16)],
        scratch_shapes=[pltpu.VMEM((NSTATE, D), F32), pltpu.VMEM((tt, D), F32), pltpu.VMEM((tt, 512), F32),
                        pltpu.VMEM((tt, D), F32), pltpu.VMEM((tt, D), F32), pltpu.VMEM((tt, D), F32)],
        compiler_params=_cparams(("arbitrary",)), name="mixer_fwd")(
            *([proj] * 11), prm["scw"], prm["cwx"], prm["cwbc"], prm["cbx"], prm["cbbc"], prm["dtb"],
            prm["alog"], prm["dskx"], prm["nrm"], prm["eh"])


def mixer_bwd(proj, dy, states, aux, prm, tt):
    t_len = proj.shape[0]
    nblk = t_len // tt
    nc = tt // CHUNK

    def rev(i):
        return nblk - 1 - i

    def main(width, col):
        return pl.BlockSpec((tt, width), lambda i, c=col // width: (rev(i), c))

    def body(ub_ref, uc_ref, uh_ref, z_ref, xr_ref, bcr_ref, dtr_ref, dy_ref, st_ref, aux_ref,
             scw_ref, cwx_ref, cwbc_ref, cbx_ref, cbbc_ref, dtb_ref, alog_ref, dsk_ref, nrm_ref, eh_ref, eht_ref,
             dp_ref, gscw_ref, gcwx_ref, gcwbc_ref, gvec_ref, gdt_ref,
             dhs, xs_s, bc_s, dtx_s, cumx_s, dys_s, dxs_s, dbc_s, red_s, ddtx_s, nx_cv, nx_px, nx_pbc):
        i = pl.program_id(0)

        @pl.when(i == 0)
        def _():
            dhs[...] = jnp.zeros_like(dhs)
            nx_cv[...] = jnp.zeros_like(nx_cv)
            nx_px[...] = jnp.zeros_like(nx_px)
            nx_pbc[...] = jnp.zeros_like(nx_pbc)
            gscw_ref[...] = jnp.zeros_like(gscw_ref)
            gcwx_ref[...] = jnp.zeros_like(gcwx_ref)
            gcwbc_ref[...] = jnp.zeros_like(gcwbc_ref)
            gvec_ref[...] = jnp.zeros_like(gvec_ref)
            gdt_ref[...] = jnp.zeros_like(gdt_ref)

        uc = uc_ref[...].astype(F32)
        uh = uh_ref[...].astype(F32)
        v = uc * uh
        dya = dy_ref[:, 0:D].astype(F32)
        dp_ref[:, C_UB:C_UB + D] = (dya * aux_ref[:, A_CV:A_CV + D].astype(F32)).astype(BF16)
        dcv = dya * ub_ref[...].astype(F32)
        nxt = nx_cv[...]
        dv = None
        for k in range(3):
            up = _shift_up(dcv, nxt, 2 - k)
            gscw_ref[k:k + 1, :] += jnp.sum(v * up, axis=0, keepdims=True)
            term = scw_ref[k:k + 1, :] * up
            dv = term if dv is None else dv + term
        nx_cv[...] = dcv[0:8]
        dp_ref[:, C_UC:C_UC + D] = (dv * uh).astype(BF16)
        dp_ref[:, C_UH:C_UH + D] = (dv * uc).astype(BF16)

        pre_x = aux_ref[:, A_PX:A_PX + D].astype(F32)
        pre_bc = aux_ref[:, A_PBC:A_PBC + 512].astype(F32)
        xs = _silu(pre_x)
        xs_s[...] = xs
        bc_s[...] = _silu(pre_bc)
        dt_pre = dtr_ref[...].astype(F32) + dtb_ref[...]
        dt = _softplus(dt_pre)
        a_neg = -jnp.exp(alog_ref[...])
        pos = lax.broadcasted_iota(jnp.int32, (tt, LANES), 0) & (CHUNK - 1)
        cum = _chunk_cumsum(dt * a_neg, pos)
        eh = eh_ref[...]
        eht = eht_ref[...]
        dtx_s[...] = _expand(dt, eh)
        cumx_s[...] = _expand(cum, eh)

        irep, causal = _lane_masks()
        irep_g, _ = _lane_masks(512)
        rep = _rep_matrix()
        row64 = lax.broadcasted_iota(jnp.int32, (CHUNK, 512), 0)
        lane128 = lax.broadcasted_iota(jnp.int32, (CHUNK, LANES), 1)

        z = z_ref[...].astype(F32)
        sz = _silu(z)
        ys = aux_ref[:, A_YS:A_YS + D].astype(F32)
        gt = ys * sz
        dyb = dy_ref[:, D:MIX].astype(F32)
        for g in range(2):
            gs = slice(g * 512, (g + 1) * 512)
            gg = gt[:, gs]
            rn = lax.rsqrt(jnp.mean(gg * gg, axis=-1, keepdims=True) + EPS)
            gvec_ref[0:1, gs] += jnp.sum(dyb[:, gs] * gg * rn, axis=0, keepdims=True)
            dgn = dyb[:, gs] * nrm_ref[:, gs]
            dgt = rn * (dgn - gg * (rn * rn) * jnp.mean(dgn * gg, axis=-1, keepdims=True))
            dys = dgt * sz[:, gs]
            dys_s[:, gs] = dys
            dp_ref[:, C_Z + g * 512:C_Z + (g + 1) * 512] = (dgt * ys[:, gs] * _dsilu(z[:, gs])).astype(BF16)
        dys_all = dys_s[...]
        gvec_ref[1:2, :] += jnp.sum(dys_all * xs, axis=0, keepdims=True)

        def bwd_chunk(cc, carry):
            c = nc - 1 - cc
            r0 = pl.multiple_of(c * CHUNK, CHUNK)
            rows = pl.ds(r0, CHUNK)
            cumx = cumx_s[rows, :]
            cum_l = cumx[CHUNK - 1:CHUNK, :]
            xs_c = xs_s[rows, :]
            dtx = dtx_s[rows, :]
            xd = xs_c * dtx
            f = jnp.exp(cum_l - cumx)
            xf = xd * f
            ex = jnp.exp(cumx)
            e_l = jnp.exp(cum_l)
            rvec = jnp.sum(jnp.where(irep, cumx, 0.0), axis=0, keepdims=True)
            lam = jnp.where(causal, jnp.exp(jnp.where(causal, cumx - rvec, 0.0)), 0.0)
            bc = bc_s[rows, :]
            dyc = dys_s[rows, :]
            for g in range(2):
                gs = slice(g * 512, (g + 1) * 512)
                bg = bc[:, g * NSTATE:(g + 1) * NSTATE].astype(BF16)
                cg = bc[:, 256 + g * NSTATE:256 + (g + 1) * NSTATE].astype(BF16)
                h0 = st_ref[c, :, gs]
                dh = dhs[:, gs]
                dh_b = dh.astype(BF16)
                xf_g = xf[:, gs]
                dxf = _nn(bg, dh_b)
                db = _nt(xf_g.astype(BF16), dh_b)
                s_rep = _nn(_nt(cg, bg).astype(BF16), rep)
                lam_g = lam[:, gs]
                m_g = s_rep * lam_g
                m_b = m_g.astype(BF16)
                ex_g = ex[:, gs]
                dy_g = dyc[:, gs]
                yo = _nn(cg, h0) * ex_g
                dg_b = (dy_g * ex_g).astype(BF16)
                dc = _nt(dg_b, h0)
                el_g = e_l[:, gs]
                dee = jnp.sum(dh * h0.astype(F32), axis=0, keepdims=True) * el_g
                dhs[:, gs] = dh * el_g + _tn(cg, dg_b)
                xd_b = xd[:, gs].astype(BF16)
                dy_b = dy_g.astype(BF16)
                dm_parts, dxd_parts = [], []
                for hp in range(4):
                    ps = slice(hp * LANES, (hp + 1) * LANES)
                    bd = _blockdiag(xd_b[:, ps])
                    dm_parts.append(_nt(dy_b[:, ps], bd))
                    t2 = _tn(m_b[:, ps], dy_b[:, ps])
                    dxd_parts.append(jnp.where(lane128 < HDIM, t2[0:CHUNK], t2[CHUNK:2 * CHUNK]))
                dm = jnp.concatenate(dm_parts, axis=1)
                dxd = jnp.concatenate(dxd_parts, axis=1) + dxf * f[:, gs]
                dseg = dm * m_g
                ds_b = _nt((dm * lam_g).astype(BF16), rep).astype(BF16)
                dc = dc + _nn(ds_b, bg)
                db = db + _tn(ds_b, cg)
                colsum = jnp.sum(dseg, axis=0, keepdims=True)
                dxfxf = dxf * xf_g
                red = dseg - jnp.where(irep_g, colsum, 0.0) + dy_g * yo - dxfxf
                last = jnp.sum(dxfxf, axis=0, keepdims=True) + dee
                red = red + jnp.where(row64 == CHUNK - 1, last, 0.0)
                red_s[rows, gs] = red
                ddtx_s[rows, gs] = dxd * xs_c[:, gs]
                dxs_s[rows, gs] = dxd * dtx[:, gs] + dsk_ref[:, gs] * dy_g
                dbc_s[rows, g * NSTATE:(g + 1) * NSTATE] = db
                dbc_s[rows, 256 + g * NSTATE:256 + (g + 1) * NSTATE] = dc
            return carry

        lax.fori_loop(0, nc, bwd_chunk, 0)

        dcum = _head_reduce(red_s[...], eht)
        da = _chunk_rcumsum(dcum, pos)
        ddt = _head_reduce(ddtx_s[...], eht) + da * a_neg
        gdt_ref[1:2, :] += jnp.sum(da * dt, axis=0, keepdims=True) * a_neg
        ddt_raw = ddt * _sigmoid(dt_pre)
        lane_t = lax.broadcasted_iota(jnp.int32, (tt, LANES), 1)
        ddt_raw = jnp.where(lane_t < NHEAD, ddt_raw, 0.0)
        gdt_ref[0:1, :] += jnp.sum(ddt_raw, axis=0, keepdims=True)
        dp_ref[:, C_DT:C_DT + LANES] = ddt_raw.astype(BF16)

        dpx = dxs_s[...] * _dsilu(pre_x)
        dpbc = dbc_s[...] * _dsilu(pre_bc)
        gvec_ref[2:3, :] += jnp.sum(dpx, axis=0, keepdims=True)
        gcwbc_ref[4:5, :] += jnp.sum(dpbc, axis=0, keepdims=True)
        xraw = xr_ref[...].astype(F32)
        bcraw = bcr_ref[...].astype(F32)
        nxt_x = nx_px[...]
        nxt_bc = nx_pbc[...]
        dxr, dbcr = None, None
        for k in range(4):
            up_x = _shift_up(dpx, nxt_x, 3 - k)
            up_bc = _shift_up(dpbc, nxt_bc, 3 - k)
            gcwx_ref[k:k + 1, :] += jnp.sum(xraw * up_x, axis=0, keepdims=True)
            gcwbc_ref[k:k + 1, :] += jnp.sum(bcraw * up_bc, axis=0, keepdims=True)
            tx = cwx_ref[k:k + 1, :] * up_x
            tb = cwbc_ref[k:k + 1, :] * up_bc
            dxr = tx if dxr is None else dxr + tx
            dbcr = tb if dbcr is None else dbcr + tb
        nx_px[...] = dpx[0:8]
        nx_pbc[...] = dpbc[0:8]
        dp_ref[:, C_XS:C_XS + D] = dxr.astype(BF16)
        dp_ref[:, C_BC:C_BC + 512] = dbcr.astype(BF16)

        @pl.when(i == nblk - 1)
        def _():
            gdt_ref[2:3, :] = _head_reduce(gvec_ref[1:2, :] * jnp.ones((8, 1), F32), eht)[0:1, :]

    def const(shape):
        return pl.BlockSpec(shape, lambda i: (0, 0))

    in_specs = [main(D, C_UB), main(D, C_UC), main(D, C_UH), main(D, C_Z), main(D, C_XS), main(512, C_BC),
                main(LANES, C_DT),
                pl.BlockSpec((tt, MIX), lambda i: (rev(i), 0)),
                pl.BlockSpec((nc, NSTATE, D), lambda i: (rev(i), 0, 0)),
                pl.BlockSpec((tt, AUX_W), lambda i: (rev(i), 0)),
                const((8, D)), const((8, D)), const((8, 512)), const((1, D)), const((1, 512)),
                const((1, LANES)), const((1, LANES)), const((1, D)), const((1, D)), const((LANES, D)),
                const((D, LANES))]
    return pl.pallas_call(
        body, grid=(nblk,),
        in_specs=in_specs,
        out_specs=[pl.BlockSpec((tt, NINP), lambda i: (rev(i), 0)),
                   const((8, D)), const((8, D)), const((8, 512)), const((8, D)), const((8, LANES))],
        out_shape=[SDS((t_len, NINP), BF16), SDS((8, D), F32), SDS((8, D), F32), SDS((8, 512), F32),
                   SDS((8, D), F32), SDS((8, LANES), F32)],
        scratch_shapes=[pltpu.VMEM((NSTATE, D), F32),
                        pltpu.VMEM((tt, D), F32), pltpu.VMEM((tt, 512), F32),
                        pltpu.VMEM((tt, D), F32), pltpu.VMEM((tt, D), F32),
                        pltpu.VMEM((tt, D), F32), pltpu.VMEM((tt, D), F32),
                        pltpu.VMEM((tt, 512), F32),
                        pltpu.VMEM((tt, D), F32), pltpu.VMEM((tt, D), F32),
                        pltpu.VMEM((8, D), F32), pltpu.VMEM((8, D), F32), pltpu.VMEM((8, 512), F32)],
        compiler_params=_cparams(("arbitrary",)), name="mixer_bwd")(
            *([proj] * 7), dy, states, aux, prm["scw"], prm["cwx"], prm["cwbc"], prm["cbx"], prm["cbbc"], prm["dtb"],
            prm["alog"], prm["dskx"], prm["nrm"], prm["eh"], prm["eht"])


TN_IN = 1920


def layer_fwd_mix(x, lw, prm, tt):
    proj, h1 = norm_matmul(x, lw["nw1"], lw["win"], "in_proj")
    y, st, aux = mixer_fwd(proj, prm, tt)
    return h1, proj, (st, aux), y


def layer_fwd_mlp(x, mixed, lw):
    h1, proj, st, y = mixed
    x1 = matmul_residual(y, lw["wout"], x, False, "out_proj")
    u, h2 = norm_matmul(x1, lw["nw2"], lw["wup"], "up_proj")
    x2 = matmul_residual(u, lw["wdn"], x1, True, "down_proj")
    return x2, (x, h1, proj, st, y, x1, h2, u)


def layer_fwd(x, lw, prm, tt):
    return layer_fwd_mlp(x, layer_fwd_mix(x, lw, prm, tt), lw)


def _dw(a, b, a_cols, b_cols, relu2, name):
    m_len, n_len = a.shape[1], b.shape[1]
    n_a, n_b = m_len // a_cols, n_len // b_cols
    assert n_a == 1 or n_b == 1
    if n_b == 1:
        return matmul_tn(
            a, b,
            lambda t_: pl.BlockSpec((t_, a_cols), lambda n, t: (t, n)),
            lambda t_: pl.BlockSpec((t_, n_len), lambda n, t: (t, 0)),
            pl.BlockSpec((a_cols, n_len), lambda n, t: (n, 0)), SDS((m_len, n_len), BF16), n_a, relu2, name)
    return matmul_tn(
        a, b,
        lambda t_: pl.BlockSpec((t_, m_len), lambda n, t: (t, 0)),
        lambda t_: pl.BlockSpec((t_, b_cols), lambda n, t: (t, n)),
        pl.BlockSpec((m_len, b_cols), lambda n, t: (0, n)), SDS((m_len, n_len), BF16), n_b, relu2, name)


def layer_bwd_mlp(dx2, dx2b, lw, saved):
    _, _, _, _, y, x1, h2, u = saved
    du = matmul_nt_act(dx2b, lw["wdn"], u, "mlp_bwd_du")
    g_wdn = _dw(u, dx2b, 1024, D, True, "dw_down")
    dx1, dx1b, g_nw2 = matmul_nt_norm_bwd(du, lw["wup"], x1, lw["nw2"], dx2, "mlp_bwd_dx")
    cb = DFF // N_DEV
    g_wup = matmul_tn(
        h2, du,
        lambda t_: pl.BlockSpec((t_, D), lambda n, t: (t, 0)),
        lambda t_: pl.BlockSpec((t_, 2 * cb), lambda n, t: (t, n)),
        pl.BlockSpec((2, D, cb), lambda n, t: (n, 0, 0)), SDS((N_DEV, D, cb), BF16), N_DEV // 2, False, "dw_up")
    dy = matmul_nt_act(dx1b, lw["wout"], None, "out_bwd_dy")
    g_wout = _dw(y, dx1b, 1024, D, False, "dw_out")
    return dx1, dx1b, dy, {"wout": g_wout, "wup": g_wup, "wdn": g_wdn, "nw2": g_nw2[0]}


def layer_bwd_mix(dx1, dy, lw, prm, saved, tt):
    x, h1, proj, st = saved[:4]
    dproj, gscw, gcwx, gcwbc, gvec, gdt = mixer_bwd(proj, dy, st[0], st[1], prm, tt)
    dx0, dx0b, g_nw1 = matmul_nt_norm_bwd(dproj, lw["win"], x, lw["nw1"], dx1, "in_bwd_dx")
    g_win = _dw(h1, dproj, D, TN_IN, False, "dw_in")
    grads = {
        "win": g_win, "scw": gscw[0:3], "cw": jnp.concatenate([gcwx[0:4], gcwbc[0:4]], axis=1),
        "cb": jnp.concatenate([gvec[2], gcwbc[4]], axis=0),
        "dtb": gdt[0, :NHEAD], "alog": gdt[1, :NHEAD], "dsk": gdt[2, :NHEAD],
        "nrm": gvec[0], "nw1": g_nw1[0],
    }
    return dx0, dx0b, grads


def layer_bwd(dx2, dx2b, lw, prm, saved, tt):
    dx1, dx1b, dy, g_mlp = layer_bwd_mlp(dx2, dx2b, lw, saved)
    dx0, dx0b, g_mix = layer_bwd_mix(dx1, dy, lw, prm, saved, tt)
    return dx0, dx0b, {**g_mlp, **g_mix}


def layer_params(win, scw, cw, nw1, nw2, conv_b, dt_bias, a_log, d_skip, ssd_norm_w, eh, eht):
    def rows8(a):
        return jnp.pad(a, ((0, 8 - a.shape[0]), (0, 0)))

    def lanes128(a):
        return jnp.pad(a, (0, LANES - a.shape[0]))[None, :]

    lw = {"win": win, "nw1": nw1[None, :], "nw2": nw2[None, :]}
    prm = {"scw": rows8(scw), "cwx": rows8(cw[:, :D]), "cwbc": rows8(cw[:, D:]),
           "cbx": conv_b[None, :D], "cbbc": conv_b[None, D:],
           "dtb": lanes128(dt_bias), "alog": lanes128(a_log),
           "dskx": jnp.repeat(d_skip, HDIM)[None, :], "nrm": ssd_norm_w[None, :], "eh": eh, "eht": eht}
    return lw, prm


def _flip(v, bit):
    return 1 - v if bit else v


def all_gather(arrs, name):
    n = len(arrs)

    def body(*refs):
        ins, outs = refs[:n], refs[n:2 * n]
        send_sems, recv_sems, local_sems = refs[2 * n:]
        x, y, c = lax.axis_index("x"), lax.axis_index("y"), lax.axis_index("c")
        sibling = (x, y, 1 - c)
        chips = [(1 - x, y), (x, 1 - y), (1 - x, 1 - y)]

        def idx(px, py, pc):
            return 4 * px + 2 * py + pc

        def copy(a, k, block, to, src=None):
            dst = outs[a].at[idx(*block)]
            return pltpu.make_async_remote_copy(
                src_ref=dst if src is None else src, dst_ref=dst,
                send_sem=send_sems.at[a, k], recv_sem=recv_sems.at[a, k], device_id=to, device_id_type=MESH)

        me = (x, y, c)
        mine = [pltpu.make_async_copy(ins[a], outs[a].at[idx(*me)], local_sems.at[a]) for a in range(n)]
        for cp in mine:
            cp.start()
        first = []
        for a in range(n):
            first.append(copy(a, 0, me, sibling, src=ins[a]))
            first += [copy(a, 1 + j, me, (*chip, c), src=ins[a]) for j, chip in enumerate(chips)]
        for cp in first:
            cp.start()
        passed = []
        for j, chip in enumerate(chips):
            for a in range(n):
                copy(a, 1 + j, (*chip, c), me).wait_recv()
                cp = copy(a, 4 + j, (*chip, c), sibling)
                cp.start()
                passed.append(cp)
        for a in range(n):
            copy(a, 0, sibling, me).wait_recv()
            for j, chip in enumerate(chips):
                copy(a, 4 + j, (*chip, 1 - c), me).wait_recv()
        for cp in first + passed:
            cp.wait_send()
        for cp in mine:
            cp.wait()

    any_spec = pl.BlockSpec(memory_space=pl.ANY)
    return pl.pallas_call(
        body, in_specs=[any_spec] * n, out_specs=[any_spec] * n,
        out_shape=[SDS((N_DEV,) + a.shape, a.dtype) for a in arrs],
        scratch_shapes=[pltpu.SemaphoreType.DMA((n, 7)), pltpu.SemaphoreType.DMA((n, 7)),
                        pltpu.SemaphoreType.DMA((n,))],
        name=name)(*arrs)


HBM_SPEC = pl.BlockSpec(memory_space=pltpu.HBM)
SEM_SPEC = pl.BlockSpec(memory_space=pltpu.SEMAPHORE)
SIDE_EFFECT = pltpu.SideEffectType.DATAFLOW_SIDE_EFFECTING
N_PEER = N_DEV - 1


def _peer(mask):
    x, y, c = lax.axis_index("x"), lax.axis_index("y"), lax.axis_index("c")
    return _flip(x, mask & 4), _flip(y, mask & 2), _flip(c, mask & 1)


ALL_PEERS = tuple(range(1, N_DEV))
SIBLING_AND_CHIPS = (1, 2, 4, 6)


def exchange_start(srcs, per_peer, name, after=None, masks=ALL_PEERS):
    n = len(srcs)
    npeer = len(masks)
    lands = [SDS((N_DEV,) + (a.shape[1:] if per_peer else a.shape), a.dtype) for a in srcs]
    n_in = 2 * n + (after is not None)

    def body(*refs):
        src_refs, land_refs = refs[:n], refs[n:2 * n]
        send_sems, recv_sems = refs[n_in], refs[n_in + 1]
        token = refs[-1]
        x, y, c = lax.axis_index("x"), lax.axis_index("y"), lax.axis_index("c")
        me = 4 * x + 2 * y + c
        for a in range(n):
            for k, mask in enumerate(masks):
                px, py, pc = _peer(mask)
                part = src_refs[a].at[4 * px + 2 * py + pc] if per_peer else src_refs[a]
                pltpu.make_async_remote_copy(
                    src_ref=part, dst_ref=land_refs[a].at[me], send_sem=send_sems.at[a * npeer + k],
                    recv_sem=recv_sems.at[a * npeer + k], device_id=(px, py, pc), device_id_type=MESH).start()
        token[...] = jnp.zeros_like(token)

    out = pl.pallas_call(
        body, name=name,
        out_shape=(pltpu.SemaphoreType.DMA((n * npeer,)), pltpu.SemaphoreType.DMA((n * npeer,)),
                   *[pltpu.HBM(a.shape, a.dtype) for a in srcs], *[pltpu.HBM(l.shape, l.dtype) for l in lands],
                   SDS((8, LANES), F32)),
        in_specs=(HBM_SPEC,) * (2 * n) + ((pl.BlockSpec(memory_space=pl.ANY),) if after is not None else ()),
        out_specs=(SEM_SPEC, SEM_SPEC) + (HBM_SPEC,) * (2 * n) + (pl.BlockSpec(memory_space=pltpu.VMEM),),
        input_output_aliases={k: 2 + k for k in range(2 * n)},
        compiler_params=pltpu.CompilerParams(has_side_effects=SIDE_EFFECT),
    )(*[pltpu.with_memory_space_constraint(a, pltpu.HBM) for a in srcs],
      *[pltpu.with_memory_space_constraint(lax.empty(l.shape, l.dtype), pltpu.HBM) for l in lands],
      *([after] if after is not None else []))
    return out[0], out[1], list(out[2:2 + n]), list(out[2 + n:2 + 2 * n]), out[-1]


def exchange_wait(started, after, per_peer, name, masks=ALL_PEERS):
    send_sems, recv_sems, srcs, lands, _ = started
    n = len(srcs)
    npeer = len(masks)

    def body(*refs):
        src_refs, land_refs = refs[:n], refs[n:2 * n]
        send_sems, recv_sems = refs[2 * n], refs[2 * n + 1]
        for k, mask in enumerate(masks):
            for a in range(n):
                copy = pltpu.make_async_remote_copy(
                    src_ref=src_refs[a].at[0] if per_peer else src_refs[a], dst_ref=land_refs[a].at[0],
                    send_sem=send_sems.at[a * npeer + k], recv_sem=recv_sems.at[a * npeer + k],
                    device_id=_peer(mask), device_id_type=MESH)
                copy.wait_send()
                copy.wait_recv()

    out = pl.pallas_call(
        body, name=name,
        out_shape=tuple(pltpu.HBM(a.shape, a.dtype) for a in srcs + lands),
        in_specs=(HBM_SPEC,) * (2 * n) + (SEM_SPEC, SEM_SPEC, pl.BlockSpec(memory_space=pl.ANY)),
        out_specs=(HBM_SPEC,) * (2 * n), input_output_aliases={k: k for k in range(2 * n)},
        compiler_params=pltpu.CompilerParams(has_side_effects=SIDE_EFFECT),
    )(*srcs, *lands, send_sems, recv_sems, after)
    return list(out[:n]), list(out[n:])


def relay_to_sibling(lands, name):
    n = len(lands)
    chips = (2, 4, 6)

    def body(*refs):
        land_refs = refs[n:2 * n]
        send_sems, recv_sems = refs[2 * n], refs[2 * n + 1]
        x, y, c = lax.axis_index("x"), lax.axis_index("y"), lax.axis_index("c")
        copies = []
        for a in range(n):
            for k, mask in enumerate(chips):
                px, py, _ = _peer(mask)
                block = land_refs[a].at[4 * px + 2 * py + c]
                cp = pltpu.make_async_remote_copy(
                    src_ref=block, dst_ref=block, send_sem=send_sems.at[a * 3 + k], recv_sem=recv_sems.at[a * 3 + k],
                    device_id=(x, y, 1 - c), device_id_type=MESH)
                cp.start()
                copies.append((cp, a, k, land_refs[a].at[4 * px + 2 * py + 1 - c]))
        for cp, a, k, arriving in copies:
            cp.wait_send()
            pltpu.make_async_remote_copy(
                src_ref=arriving, dst_ref=arriving, send_sem=send_sems.at[a * 3 + k], recv_sem=recv_sems.at[a * 3 + k],
                device_id=(x, y, 1 - c), device_id_type=MESH).wait_recv()

    any_spec = pl.BlockSpec(memory_space=pl.ANY)
    return list(pl.pallas_call(
        body, in_specs=[any_spec] * n, out_specs=[any_spec] * n,
        out_shape=[SDS(a.shape, a.dtype) for a in lands],
        input_output_aliases={k: k for k in range(n)},
        scratch_shapes=[pltpu.SemaphoreType.DMA((n * 3,)), pltpu.SemaphoreType.DMA((n * 3,))],
        name=name)(*lands))


IN_SHARD = NIN // N_DEV
SLOT_W = 768


def _slot_window(j):
    return (IN_SHARD * j // LANES) * LANES, -(-(IN_SHARD * (j + 1)) // LANES) * LANES


def _placement(j):
    a, b = _slot_window(j)
    r = lax.broadcasted_iota(jnp.int32, (SLOT_W, b - a), 0)
    c = lax.broadcasted_iota(jnp.int32, (SLOT_W, b - a), 1)
    return jnp.where(jnp.logical_and(c == r + (IN_SHARD * j - a), r < IN_SHARD), 1.0, 0.0).astype(BF16)


def assemble_w_in(land):
    tm = 256

    def body(l_ref, o_ref, acc):
        acc[...] = jnp.zeros_like(acc)
        for j in range(N_DEV):
            a, b = _slot_window(j)
            acc[:, a:b] += _nn(l_ref[j], _placement(j))
        o_ref[...] = acc[...].astype(BF16)

    return pl.pallas_call(
        body, grid=(D // tm,),
        in_specs=[pl.BlockSpec((N_DEV, tm, SLOT_W), lambda i: (0, i, 0))],
        out_specs=pl.BlockSpec((tm, NINP), lambda i: (i, 0)),
        out_shape=SDS((D, NINP), BF16),
        scratch_shapes=[pltpu.VMEM((tm, NINP), F32)],
        compiler_params=_cparams(("parallel",)), name="assemble_w_in")(land)


def scatter_w_in(dw):
    tm = 256

    def body(d_ref, o_ref):
        for j in range(N_DEV):
            a, b = _slot_window(j)
            o_ref[j] = _nt(d_ref[:, a:b], _placement(j)).astype(BF16)

    return pl.pallas_call(
        body, grid=(D // tm,),
        in_specs=[pl.BlockSpec((tm, NINP), lambda i: (i, 0))],
        out_specs=pl.BlockSpec((N_DEV, tm, SLOT_W), lambda i: (0, i, 0)),
        out_shape=SDS((N_DEV, D, SLOT_W), BF16),
        compiler_params=_cparams(("parallel",)), name="scatter_w_in")(dw)


def _adamw_math(g, w_ref, m_ref, v_ref, g_ref, d_ref, nm_ref, nv_ref):
    mn = ADAM_B1 * m_ref[...] + (1.0 - ADAM_B1) * g
    vn = ADAM_B2 * v_ref[...] + (1.0 - ADAM_B2) * jnp.square(g)
    m_hat = mn / (1.0 - ADAM_B1 ** ADAM_STEP)
    v_hat = vn / (1.0 - ADAM_B2 ** ADAM_STEP)
    g_ref[...] = g
    d_ref[...] = -ADAM_LR * (m_hat / (jnp.sqrt(v_hat) + ADAM_EPS) + ADAM_WD * w_ref[...])
    nm_ref[...] = mn
    nv_ref[...] = vn


def adamw_layers(w, slots, m, v, name):
    depth, r_len, c_len = w.shape
    cs = slots[0].shape[2]
    br = min(128, r_len)
    assert r_len % br == 0

    def body(w_ref, *rest):
        s_refs, (m_ref, v_ref, g_ref, d_ref, nm_ref, nv_ref) = rest[:depth], rest[depth:]
        layer = pl.program_id(0)
        for k in range(depth):
            @pl.when(layer == k)
            def _(k=k):
                g = s_refs[k][0, :, 0:c_len].astype(F32)
                for j in range(1, N_DEV):
                    g = g + s_refs[k][j, :, 0:c_len].astype(F32)
                _adamw_math(g, w_ref, m_ref, v_ref, g_ref, d_ref, nm_ref, nv_ref)

    spec = pl.BlockSpec((None, br, c_len), lambda l, i: (l, i, 0))
    s_specs = [pl.BlockSpec((N_DEV, br, cs), lambda l, i, k=k: (0, jnp.where(l == k, i, 0), 0))
               for k in range(depth)]
    return pl.pallas_call(
        body, grid=(depth, r_len // br),
        in_specs=[spec] + s_specs + [spec, spec],
        out_specs=[spec] * 4, out_shape=[SDS(w.shape, F32)] * 4,
        compiler_params=_cparams(("arbitrary", "arbitrary")), name=name)(w, *slots, m, v)


def adamw(w, slots, m, v, name):
    r_len, c_len = w.shape
    br = r_len if r_len <= 512 else 512
    assert r_len % br == 0

    def body(w_ref, s_ref, m_ref, v_ref, g_ref, d_ref, nm_ref, nv_ref):
        g = s_ref[0].astype(F32)
        for k in range(1, N_DEV):
            g = g + s_ref[k].astype(F32)
        _adamw_math(g, w_ref, m_ref, v_ref, g_ref, d_ref, nm_ref, nv_ref)

    spec = pl.BlockSpec((br, c_len), lambda i: (i, 0))
    return pl.pallas_call(
        body, grid=(r_len // br,),
        in_specs=[spec, pl.BlockSpec((N_DEV, br, c_len), lambda i: (0, i, 0)), spec, spec],
        out_specs=[spec] * 4, out_shape=[SDS((r_len, c_len), F32)] * 4,
        compiler_params=_cparams(("parallel",)), name=name)(w, slots, m, v)


def _adamw_nd(w, slots, m, v, name):
    shp = w.shape
    r = int(np.prod(shp[:-1]))
    outs = adamw(w.reshape(r, shp[-1]), slots.reshape(N_DEV, r, shp[-1]), m.reshape(r, shp[-1]),
                 v.reshape(r, shp[-1]), name)
    return [o.reshape(shp) for o in outs]


SMALL = [("norm_mix_w", DEPTH * D), ("ssd_conv_b", DEPTH * XBC), ("dt_bias", DEPTH * NHEAD),
         ("a_log", DEPTH * NHEAD), ("d_skip", DEPTH * NHEAD), ("ssd_norm_w", DEPTH * D),
         ("norm_mlp_w", DEPTH * D), ("final_norm_w", D)]
SMALL_LEN = sum(s for _, s in SMALL)
SMALL_ROWS = -(-SMALL_LEN // LANES)


def _pack_small(parts):
    flat = jnp.concatenate([parts[k].reshape(-1) for k, _ in SMALL])
    return jnp.pad(flat, (0, SMALL_ROWS * LANES - SMALL_LEN)).reshape(SMALL_ROWS, LANES)


def _unpack_small(packed, shapes):
    flat = packed.reshape(-1)
    out, off = {}, 0
    for k, s in SMALL:
        out[k] = flat[off:off + s].reshape(shapes[k])
        off += s
    return out


def kernel(x, norm_mix_w, w_in, short_conv_w, ssd_conv_w, ssd_conv_b, dt_bias, a_log, d_skip, ssd_norm_w, w_out, norm_mlp_w, w_up, w_down, final_norm_w, loss_target, m_norm_mix_w, m_w_in, m_short_conv_w, m_ssd_conv_w, m_ssd_conv_b, m_dt_bias, m_a_log, m_d_skip, m_ssd_norm_w, m_w_out, m_norm_mlp_w, m_w_up, m_w_down, m_final_norm_w, v_norm_mix_w, v_w_in, v_short_conv_w, v_ssd_conv_w, v_ssd_conv_b, v_dt_bias, v_a_log, v_d_skip, v_ssd_norm_w, v_w_out, v_norm_mlp_w, v_w_up, v_w_down, v_final_norm_w):
    xs = x[0]
    t_len = xs.shape[0]
    tt = min(256, t_len)
    eh, eht = _head_matrices()
    me = 4 * lax.axis_index("x") + 2 * lax.axis_index("y") + lax.axis_index("c")

    def start_weights(i, after):
        first = exchange_start(
            [jnp.pad(w_in[i].astype(BF16), ((0, 0), (0, SLOT_W - IN_SHARD))), short_conv_w[i], ssd_conv_w[i]],
            False, "w_in_start_%d" % i, after, SIBLING_AND_CHIPS)
        rest = exchange_start([w_out[i].astype(BF16), w_up[i].astype(BF16), w_down[i].astype(BF16)], False,
                              "w_rest_start_%d" % i, first[4] if after is None else after, SIBLING_AND_CHIPS)
        return first, rest

    def fill_own(srcs, lands, per_peer):
        own = [lax.dynamic_index_in_dim(s_, me, 0, keepdims=False) for s_ in srcs] if per_peer else srcs
        return [lax.dynamic_update_index_in_dim(l_, o_, me, 0) for l_, o_ in zip(lands, own)]

    def finish_weights(started, after, name):
        srcs, lands = exchange_wait(started, after, False, name + "_wait", SIBLING_AND_CHIPS)
        return fill_own(srcs, relay_to_sibling(lands, name + "_relay"), False)

    act = xs
    saved, layers = [], []
    first, rest = start_weights(0, None)
    token = first[4][0, 0] + rest[4][0, 0]
    for i in range(DEPTH):
        g_in, g_sc, g_cw = finish_weights(first, act, "w_in_%d" % i)
        lw, prm = layer_params(
            assemble_w_in(g_in), g_sc.transpose(1, 0, 2).reshape(3, D), g_cw.transpose(1, 0, 2).reshape(4, XBC),
            norm_mix_w[i], norm_mlp_w[i], ssd_conv_b[i], dt_bias[i], a_log[i], d_skip[i], ssd_norm_w[i], eh, eht)
        lw["nw1"] = lw["nw1"] + token
        mixed = layer_fwd_mix(act, lw, prm, tt)
        g_out, g_up, g_dn = finish_weights(rest, mixed[3], "w_rest_%d" % i)
        lw.update(wout=g_out.reshape(MIX, D), wup=g_up, wdn=g_dn.reshape(DFF, D))
        if i + 1 < DEPTH:
            first, rest = start_weights(i + 1, g_dn)
            token = first[4][0, 0] + rest[4][0, 0]
            lw["nw2"] = lw["nw2"] + token
        layers.append((lw, prm))
        act, sv = layer_fwd_mlp(act, mixed, lw)
        saved.append(sv)
    loss_acc, dx, dxb, g_fw = loss_head(act, final_norm_w[None, :], loss_target[0])

    grads = [None] * DEPTH
    sent_rest, sent_in = [None] * DEPTH, [None] * DEPTH
    token = None
    for i in reversed(range(DEPTH)):
        lw, prm = layers[i]
        if token is not None:
            lw = dict(lw, nw2=lw["nw2"] + token)
        dx1, _, dy, g_mlp = layer_bwd_mlp(dx, dxb, lw, saved[i])
        sent_rest[i] = exchange_start(
            [g_mlp["wout"].reshape(N_DEV, MIX // N_DEV, D), g_mlp["wup"], g_mlp["wdn"].reshape(N_DEV, DFF // N_DEV, D)],
            True, "g_rest_start_%d" % i)
        dx, dxb, g_mix = layer_bwd_mix(dx1, dy, lw, dict(prm, nrm=prm["nrm"] + sent_rest[i][4][0, 0]), saved[i], tt)
        grads[i] = {**g_mlp, **g_mix}
        if i > 0:
            sent_in[i] = exchange_start([scatter_w_in(g_mix["win"])], True, "g_in_start_%d" % i)
            token = sent_in[i][4][0, 0]

    def stack(k):
        return jnp.stack([g[k] for g in grads])

    small = _pack_small({"norm_mix_w": stack("nw1"), "ssd_conv_b": stack("cb"), "dt_bias": stack("dtb"),
                         "a_log": stack("alog"), "d_skip": stack("dsk"), "ssd_norm_w": stack("nrm"),
                         "norm_mlp_w": stack("nw2"), "final_norm_w": g_fw[0]})
    r_small, r_sc, r_cw = all_gather([small, stack("scw"), stack("cw")], "gather_small_grads")
    r_sc = lax.dynamic_slice_in_dim(r_sc, me * (D // N_DEV), D // N_DEV, axis=3)
    r_cw = lax.dynamic_slice_in_dim(r_cw, me * (XBC // N_DEV), XBC // N_DEV, axis=3)
    sent_in[0] = exchange_start([scatter_w_in(grads[0]["win"])], True, "g_in_start_0", after=r_small)

    after = sent_in[0][4]
    recv = [fill_own(*exchange_wait(sent_rest[i], after, True, "g_rest_wait_%d" % i), True) for i in range(DEPTH)]
    res = {}
    res["w_out"] = adamw_layers(w_out, [r[0] for r in recv], m_w_out, v_w_out, "adamw_w_out")
    res["w_up"] = adamw_layers(w_up, [r[1] for r in recv], m_w_up, v_w_up, "adamw_w_up")
    res["w_down"] = adamw_layers(w_down, [r[2] for r in recv], m_w_down, v_w_down, "adamw_w_down")
    after = res["w_down"][1]
    recv_in = [fill_own(*exchange_wait(sent_in[i], after, True, "g_in_wait_%d" % i), True)[0] for i in range(DEPTH)]
    res["w_in"] = adamw_layers(w_in, recv_in, m_w_in, v_w_in, "adamw_w_in")
    res["short_conv_w"] = _adamw_nd(short_conv_w, r_sc, m_short_conv_w, v_short_conv_w, "adamw_short_conv")
    res["ssd_conv_w"] = _adamw_nd(ssd_conv_w, r_cw, m_ssd_conv_w, v_ssd_conv_w, "adamw_ssd_conv")
    small_w = {"norm_mix_w": norm_mix_w, "ssd_conv_b": ssd_conv_b, "dt_bias": dt_bias, "a_log": a_log,
               "d_skip": d_skip, "ssd_norm_w": ssd_norm_w, "norm_mlp_w": norm_mlp_w, "final_norm_w": final_norm_w}
    small_m = {"norm_mix_w": m_norm_mix_w, "ssd_conv_b": m_ssd_conv_b, "dt_bias": m_dt_bias, "a_log": m_a_log,
               "d_skip": m_d_skip, "ssd_norm_w": m_ssd_norm_w, "norm_mlp_w": m_norm_mlp_w,
               "final_norm_w": m_final_norm_w}
    small_v = {"norm_mix_w": v_norm_mix_w, "ssd_conv_b": v_ssd_conv_b, "dt_bias": v_dt_bias, "a_log": v_a_log,
               "d_skip": v_d_skip, "ssd_norm_w": v_ssd_norm_w, "norm_mlp_w": v_norm_mlp_w,
               "final_norm_w": v_final_norm_w}
    shapes = {k: a.shape for k, a in small_w.items()}
    packed = adamw(_pack_small(small_w), r_small, _pack_small(small_m), _pack_small(small_v), "adamw_small")
    unpacked = [_unpack_small(p, shapes) for p in packed]
    for k in small_w:
        res[k] = [u[k] for u in unpacked]

    loss = lax.psum(loss_acc[0, 0], ("x", "y", "c"))
    order = ["norm_mix_w", "w_in", "short_conv_w", "ssd_conv_w", "ssd_conv_b", "dt_bias", "a_log", "d_skip",
             "ssd_norm_w", "w_out", "norm_mlp_w", "w_up", "w_down", "final_norm_w"]
    out = [loss, dx[None]]
    for part in range(4):
        out += [res[k][part] for k in order]
    return tuple(out)
```

```python
import functools

import numpy as np
import jax
import jax.numpy as jnp
from jax import lax
from jax.experimental import pallas as pl
from jax.experimental.pallas import tpu as pltpu

F32 = jnp.float32
BF16 = jnp.bfloat16
SDS = jax.ShapeDtypeStruct

N_DEV = 8
DEPTH = 4
D = 1024
NIN = 5648
NINP = 5760
DFF = 4096
MIX = 2048
NHEAD = 16
HDIM = 64
NSTATE = 128
CHUNK = 64
XBC = 1536
EPS = 1e-5
LANES = 128

C_UB, C_UC, C_UH, C_Z, C_XS, C_BC, C_DT = 0, 1024, 2048, 3072, 4096, 5120, 5632
A_CV, A_YS, A_PX, A_PBC, AUX_W = 0, 1024, 2048, 3072, 3584

ADAM_LR = 0.001
ADAM_B1 = 0.9
ADAM_B2 = 0.999
ADAM_EPS = 1e-08
ADAM_WD = 0.01
ADAM_STEP = 10

VMEM_LIMIT = 56 * 1024 * 1024
MESH = pl.DeviceIdType.MESH


def _cparams(sem):
    return pltpu.CompilerParams(dimension_semantics=sem, vmem_limit_bytes=VMEM_LIMIT)


def _nt(a, b):
    return lax.dot_general(a, b, (((1,), (1,)), ((), ())), preferred_element_type=F32)


def _tn(a, b):
    return lax.dot_general(a, b, (((0,), (0,)), ((), ())), preferred_element_type=F32)


def _nn(a, b):
    return jnp.dot(a, b, preferred_element_type=F32)


def _sigmoid(v):
    return 0.5 * jnp.tanh(0.5 * v) + 0.5


def _split3(v):
    v1 = v.astype(BF16)
    r1 = v - v1.astype(F32)
    v2 = r1.astype(BF16)
    v3 = (r1 - v2.astype(F32)).astype(BF16)
    return v1, v2, v3


def _expand(v, eh):
    v1, v2, v3 = _split3(v)
    return _nn(v1, eh) + _nn(v2, eh) + _nn(v3, eh)


def _head_reduce(v, eht):
    v1 = v.astype(BF16)
    v2 = (v - v1.astype(F32)).astype(BF16)
    return _nn(v1, eht) + _nn(v2, eht)


def _head_matrices():
    eh = np.zeros((LANES, D), np.float32)
    for h in range(NHEAD):
        eh[h, h * HDIM:(h + 1) * HDIM] = 1.0
    return jnp.asarray(eh, BF16), jnp.asarray(eh.T.copy(), BF16)


def _resident(shape):
    return pl.BlockSpec(shape, lambda *_: (0,) * len(shape), pipeline_mode=pl.Buffered(1))


def _col_chunks(n, step):
    return [(c, min(c + step, n)) for c in range(0, n, step)]


def norm_matmul(x, nw, w, name):
    t_len = x.shape[0]
    blocked = w.ndim == 3
    n_len = w.shape[0] * w.shape[2] if blocked else w.shape[1]
    tm = min(512, t_len)
    chunks = _col_chunks(n_len, n_len // N_DEV if blocked else 1536)

    def body(x_ref, nw_ref, w_ref, o_ref, h_ref):
        xv = x_ref[...]
        r = lax.rsqrt(jnp.mean(xv * xv, axis=-1, keepdims=True) + EPS)
        hv = (xv * r * nw_ref[...]).astype(BF16)
        h_ref[...] = hv
        for j, (c0, c1) in enumerate(chunks):
            wj = w_ref[j] if blocked else w_ref[:, c0:c1]
            o_ref[:, c0:c1] = _nn(hv, wj).astype(o_ref.dtype)

    return pl.pallas_call(
        body, grid=(t_len // tm,),
        in_specs=[pl.BlockSpec((tm, D), lambda i: (i, 0)), _resident((1, D)), _resident(w.shape)],
        out_specs=[pl.BlockSpec((tm, n_len), lambda i: (i, 0)),
                   pl.BlockSpec((tm, D), lambda i: (i, 0))],
        out_shape=[SDS((t_len, n_len), BF16), SDS((t_len, D), BF16)],
        compiler_params=_cparams(("parallel",)), name=name)(x, nw, w)


def matmul_residual(a, w, res, relu2, name):
    t_len, k_len = a.shape
    tm = min(512, t_len)

    def body(a_ref, w_ref, res_ref, o_ref):
        av = a_ref[...]
        if relu2:
            af = jnp.maximum(av.astype(F32), 0.0)
            av = (af * af).astype(BF16)
        o_ref[...] = res_ref[...] + _nn(av, w_ref[...])

    return pl.pallas_call(
        body, grid=(t_len // tm,),
        in_specs=[pl.BlockSpec((tm, k_len), lambda i: (i, 0)),
                  _resident((k_len, D)),
                  pl.BlockSpec((tm, D), lambda i: (i, 0))],
        out_specs=pl.BlockSpec((tm, D), lambda i: (i, 0)),
        out_shape=SDS((t_len, D), F32),
        compiler_params=_cparams(("parallel",)), name=name)(a, w, res)


def matmul_nt_act(dy, w, u, name):
    t_len = dy.shape[0]
    n_len = w.shape[0]
    tm = min(512, t_len)
    chunks = _col_chunks(n_len, 1024)

    def body(dy_ref, w_ref, *rest):
        if u is None:
            (o_ref,) = rest
        else:
            u_ref, o_ref = rest
        dyv = dy_ref[...]
        for c0, c1 in chunks:
            p = _nt(dyv, w_ref[c0:c1, :])
            if u is not None:
                p = p * (2.0 * jnp.maximum(u_ref[:, c0:c1].astype(F32), 0.0))
            o_ref[:, c0:c1] = p.astype(o_ref.dtype)

    in_specs = [pl.BlockSpec((tm, D), lambda i: (i, 0)), _resident((n_len, D))]
    args = [dy, w]
    if u is not None:
        in_specs.append(pl.BlockSpec((tm, n_len), lambda i: (i, 0)))
        args.append(u)
    return pl.pallas_call(
        body, grid=(t_len // tm,),
        in_specs=in_specs,
        out_specs=pl.BlockSpec((tm, n_len), lambda i: (i, 0)),
        out_shape=SDS((t_len, n_len), BF16),
        compiler_params=_cparams(("parallel",)), name=name)(*args)


def matmul_tn(a, b, a_spec, b_spec, o_spec, o_shape, n_out, relu2, name):
    t_len = a.shape[0]
    tt = min(2048, t_len)
    nt = t_len // tt

    def body(a_ref, b_ref, o_ref, acc):
        t = pl.program_id(1)
        av = a_ref[...]
        if relu2:
            af = jnp.maximum(av.astype(F32), 0.0)
            av = (af * af).astype(BF16)
        p = _tn(av, b_ref[...])

        @pl.when(t == 0)
        def _():
            acc[...] = p

        @pl.when(t > 0)
        def _():
            acc[...] += p

        @pl.when(t == nt - 1)
        def _():
            if len(blk) == 3:
                for j in range(blk[0]):
                    o_ref[j] = acc[:, j * blk[2]:(j + 1) * blk[2]].astype(o_ref.dtype)
            else:
                o_ref[...] = acc[...].astype(o_ref.dtype)

    blk = tuple(o_spec.block_shape)
    acc_shape = (blk[1], blk[0] * blk[2]) if len(blk) == 3 else blk
    return pl.pallas_call(
        body, grid=(n_out, nt),
        in_specs=[a_spec(tt), b_spec(tt)],
        out_specs=o_spec, out_shape=o_shape,
        scratch_shapes=[pltpu.VMEM(acc_shape, F32)],
        compiler_params=_cparams(("parallel", "arbitrary")), name=name)(a, b)


def matmul_nt_norm_bwd(dy, w, x, nw, dres, name):
    t_len = x.shape[0]
    blocked = w.ndim == 3
    k_len = dy.shape[1]
    kb = k_len // N_DEV
    tm = min(512, t_len)

    def body(dy_ref, w_ref, x_ref, nw_ref, dres_ref, dx_ref, dxb_ref, dnw_ref):
        @pl.when(pl.program_id(0) == 0)
        def _():
            dnw_ref[...] = jnp.zeros_like(dnw_ref)

        if blocked:
            dh = _nt(dy_ref[:, 0:kb], w_ref[0])
            for j in range(1, N_DEV):
                dh = dh + _nt(dy_ref[:, j * kb:(j + 1) * kb], w_ref[j])
        else:
            dh = _nt(dy_ref[...], w_ref[...])
        xv = x_ref[...]
        r = lax.rsqrt(jnp.mean(xv * xv, axis=-1, keepdims=True) + EPS)
        xh = xv * r
        dnw_ref[0:1, :] += jnp.sum(dh * xh, axis=0, keepdims=True)
        g = dh * nw_ref[...]
        dx = dres_ref[...] + r * (g - xh * jnp.mean(g * xh, axis=-1, keepdims=True))
        dx_ref[...] = dx
        dxb_ref[...] = dx.astype(BF16)

    return pl.pallas_call(
        body, grid=(t_len // tm,),
        in_specs=[pl.BlockSpec((tm, k_len), lambda i: (i, 0)),
                  _resident(w.shape),
                  pl.BlockSpec((tm, D), lambda i: (i, 0)),
                  _resident((1, D)),
                  pl.BlockSpec((tm, D), lambda i: (i, 0))],
        out_specs=[pl.BlockSpec((tm, D), lambda i: (i, 0)),
                   pl.BlockSpec((tm, D), lambda i: (i, 0)),
                   pl.BlockSpec((8, D), lambda i: (0, 0))],
        out_shape=[SDS((t_len, D), F32), SDS((t_len, D), BF16), SDS((8, D), F32)],
        compiler_params=_cparams(("arbitrary",)), name=name)(dy, w, x, nw, dres)


def loss_head(x, fw, tgt):
    t_len = x.shape[0]
    tm = min(512, t_len)

    def body(x_ref, fw_ref, t_ref, loss_ref, dx_ref, dxb_ref, dfw_ref):
        @pl.when(pl.program_id(0) == 0)
        def _():
            loss_ref[...] = jnp.zeros_like(loss_ref)
            dfw_ref[...] = jnp.zeros_like(dfw_ref)
        xv = x_ref[...]
        r = lax.rsqrt(jnp.mean(xv * xv, axis=-1, keepdims=True) + EPS)
        xh = xv * r
        w = fw_ref[...]
        e = xh * w - t_ref[...]
        row = jnp.sum(e * e, axis=-1, keepdims=True) * (1.0 / D)
        loss_ref[...] += 0.5 * jnp.sum(row, axis=0, keepdims=True)
        dyf = e * (1.0 / D)
        dfw_ref[0:1, :] += jnp.sum(dyf * xh, axis=0, keepdims=True)
        g = dyf * w
        dx = r * (g - xh * jnp.mean(g * xh, axis=-1, keepdims=True))
        dx_ref[...] = dx
        dxb_ref[...] = dx.astype(BF16)

    return pl.pallas_call(
        body, grid=(t_len // tm,),
        in_specs=[pl.BlockSpec((tm, D), lambda i: (i, 0)),
                  pl.BlockSpec((1, D), lambda i: (0, 0)),
                  pl.BlockSpec((tm, D), lambda i: (i, 0))],
        out_specs=[pl.BlockSpec((8, LANES), lambda i: (0, 0)),
                   pl.BlockSpec((tm, D), lambda i: (i, 0)),
                   pl.BlockSpec((tm, D), lambda i: (i, 0)),
                   pl.BlockSpec((8, D), lambda i: (0, 0))],
        out_shape=[SDS((8, LANES), F32), SDS((t_len, D), F32), SDS((t_len, D), BF16), SDS((8, D), F32)],
        compiler_params=_cparams(("arbitrary",)), name="loss_head")(x, fw, tgt)


def _shift_dn(x, halo, j):
    if j == 0:
        return x
    xr = pltpu.roll(x, j, 0)
    hr = pltpu.roll(halo, j, 0)
    row = lax.broadcasted_iota(jnp.int32, hr.shape, 0)
    top = jnp.where(row < j, hr, xr[0:8])
    return jnp.concatenate([top, xr[8:]], axis=0)


def _shift_up(x, nxt, j):
    if j == 0:
        return x
    n = x.shape[0]
    xr = pltpu.roll(x, n - j, 0)
    hr = pltpu.roll(nxt, 8 - j, 0)
    row = lax.broadcasted_iota(jnp.int32, hr.shape, 0)
    bot = jnp.where(row >= 8 - j, hr, xr[n - 8:n])
    return jnp.concatenate([xr[:n - 8], bot], axis=0)


def _conv_fwd(x, halo, w_ref, kw):
    acc = None
    for k in range(kw):
        term = w_ref[k:k + 1, :] * _shift_dn(x, halo, kw - 1 - k)
        acc = term if acc is None else acc + term
    return acc


def _chunk_cumsum(a, pos):
    for sh in (1, 2, 4, 8, 16, 32):
        a = a + jnp.where(pos >= sh, pltpu.roll(a, sh, 0), 0.0)
    return a


def _chunk_rcumsum(a, pos):
    n = a.shape[0]
    for sh in (1, 2, 4, 8, 16, 32):
        a = a + jnp.where(pos < CHUNK - sh, pltpu.roll(a, n - sh, 0), 0.0)
    return a


def _softplus(v):
    return jnp.maximum(v, 0.0) + jnp.log(1.0 + jnp.exp(-jnp.abs(v)))


def _silu(v):
    return v * _sigmoid(v)


def _dsilu(v):
    s = _sigmoid(v)
    return s * (1.0 + v * (1.0 - s))


def _lane_masks(width=D):
    lane = lax.broadcasted_iota(jnp.int32, (CHUNK, width), 1) & (HDIM - 1)
    row = lax.broadcasted_iota(jnp.int32, (CHUNK, width), 0)
    return lane == row, lane <= row


def _rep_matrix():
    lane = lax.broadcasted_iota(jnp.int32, (CHUNK, 512), 1) & (HDIM - 1)
    row = lax.broadcasted_iota(jnp.int32, (CHUNK, 512), 0)
    return jnp.where(lane == row, 1.0, 0.0).astype(BF16)


def _blockdiag(xp):
    lane = lax.broadcasted_iota(jnp.int32, xp.shape, 1)
    zero = jnp.zeros_like(xp)
    return jnp.concatenate([jnp.where(lane < HDIM, xp, zero), jnp.where(lane >= HDIM, xp, zero)], axis=0)


def _mixer_views(tt):
    r8 = tt // 8

    def main(width, col):
        return pl.BlockSpec((tt, width), lambda i, c=col // width: (i, c))

    def halo(width, col):
        return pl.BlockSpec((8, width), lambda i, c=col // width: (jnp.maximum(i * r8 - 1, 0), c))

    return main, halo


def mixer_fwd(proj, prm, tt):
    t_len = proj.shape[0]
    nblk = t_len // tt
    nc = tt // CHUNK
    main, halo = _mixer_views(tt)

    def body(ub_ref, uc_ref, uh_ref, z_ref, xr_ref, bcr_ref, dtr_ref, uch_ref, uhh_ref, xrh_ref, bcrh_ref,
             scw_ref, cwx_ref, cwbc_ref, cbx_ref, cbbc_ref, dtb_ref, alog_ref, dsk_ref, nrm_ref, eh_ref,
             y_ref, st_ref, aux_ref, hs, xs_s, bc_s, dtx_s, cumx_s, yssd_s):
        i = pl.program_id(0)
        first = i == 0

        @pl.when(first)
        def _():
            hs[...] = jnp.zeros_like(hs)

        keep = jnp.where(first, 0.0, 1.0)
        v = uc_ref[...].astype(F32) * uh_ref[...].astype(F32)
        vh = uch_ref[...].astype(F32) * uhh_ref[...].astype(F32) * keep
        cv = _conv_fwd(v, vh, scw_ref, 3)
        aux_ref[:, A_CV:A_CV + D] = cv.astype(BF16)
        y_ref[:, 0:D] = (ub_ref[...].astype(F32) * cv).astype(BF16)

        pre_x = _conv_fwd(xr_ref[...].astype(F32), xrh_ref[...].astype(F32) * keep, cwx_ref, 4) + cbx_ref[...]
        aux_ref[:, A_PX:A_PX + D] = pre_x.astype(BF16)
        xs_s[...] = _silu(pre_x)
        pre_bc = _conv_fwd(bcr_ref[...].astype(F32), bcrh_ref[...].astype(F32) * keep, cwbc_ref, 4) + cbbc_ref[...]
        aux_ref[:, A_PBC:A_PBC + 512] = pre_bc.astype(BF16)
        bc_s[...] = _silu(pre_bc)
        dt = _softplus(dtr_ref[...].astype(F32) + dtb_ref[...])
        a_neg = -jnp.exp(alog_ref[...])
        pos = lax.broadcasted_iota(jnp.int32, (tt, LANES), 0) & (CHUNK - 1)
        cum = _chunk_cumsum(dt * a_neg, pos)
        eh = eh_ref[...]
        dtx_s[...] = _expand(dt, eh)
        cumx_s[...] = _expand(cum, eh)
        irep, causal = _lane_masks()
        rep = _rep_matrix()

        def chunk(c, carry):
            r0 = pl.multiple_of(c * CHUNK, CHUNK)
            rows = pl.ds(r0, CHUNK)
            cumx = cumx_s[rows, :]
            cum_l = cumx[CHUNK - 1:CHUNK, :]
            xd = xs_s[rows, :] * dtx_s[rows, :]
            xf = xd * jnp.exp(cum_l - cumx)
            ex = jnp.exp(cumx)
            e_l = jnp.exp(cum_l)
            rvec = jnp.sum(jnp.where(irep, cumx, 0.0), axis=0, keepdims=True)
            lam = jnp.where(causal, jnp.exp(jnp.where(causal, cumx - rvec, 0.0)), 0.0)
            bc = bc_s[rows, :]
            for g in range(2):
                gs = slice(g * 512, (g + 1) * 512)
                bg = bc[:, g * NSTATE:(g + 1) * NSTATE].astype(BF16)
                cg = bc[:, 256 + g * NSTATE:256 + (g + 1) * NSTATE].astype(BF16)
                s_rep = _nn(_nt(cg, bg).astype(BF16), rep)
                m_g = (s_rep * lam[:, gs]).astype(BF16)
                h_g = hs[:, gs]
                h_b = h_g.astype(BF16)
                st_ref[c, :, gs] = h_b
                yo = _nn(cg, h_b) * ex[:, gs]
                xd_b = xd[:, gs].astype(BF16)
                for hp in range(4):
                    ps = slice(hp * LANES, (hp + 1) * LANES)
                    yd = _nn(m_g[:, ps], _blockdiag(xd_b[:, ps]))
                    yssd_s[rows, g * 512 + hp * LANES:g * 512 + (hp + 1) * LANES] = yd + yo[:, ps]
                hs[:, gs] = h_g * e_l[:, gs] + _tn(bg, xf[:, gs].astype(BF16))
            return carry

        lax.fori_loop(0, nc, chunk, 0, unroll=True)

        ys = yssd_s[...] + dsk_ref[...] * xs_s[...]
        aux_ref[:, A_YS:A_YS + D] = ys.astype(BF16)
        gt = ys * _silu(z_ref[...].astype(F32))
        for g in range(2):
            gs = slice(g * 512, (g + 1) * 512)
            gg = gt[:, gs]
            rn = lax.rsqrt(jnp.mean(gg * gg, axis=-1, keepdims=True) + EPS)
            y_ref[:, D + g * 512:D + (g + 1) * 512] = (gg * rn * nrm_ref[:, gs]).astype(BF16)

    def const(shape):
        return pl.BlockSpec(shape, lambda i: (0, 0))

    in_specs = [main(D, C_UB), main(D, C_UC), main(D, C_UH), main(D, C_Z), main(D, C_XS), main(512, C_BC),
                main(LANES, C_DT), halo(D, C_UC), halo(D, C_UH), halo(D, C_XS), halo(512, C_BC),
                const((8, D)), const((8, D)), const((8, 512)), const((1, D)), const((1, 512)),
                const((1, LANES)), const((1, LANES)), const((1, D)), const((1, D)), const((LANES, D))]
    return pl.pallas_call(
        body, grid=(nblk,),
        in_specs=in_specs,
        out_specs=[pl.BlockSpec((tt, MIX), lambda i: (i, 0)),
                   pl.BlockSpec((nc, NSTATE, D), lambda i: (i, 0, 0)),
                   pl.BlockSpec((tt, AUX_W), lambda i: (i, 0))],
        out_shape=[SDS((t_len, MIX), BF16), SDS((t_len // CHUNK, NSTATE, D), BF16), SDS((t_len, AUX_W), BF16)],
        scratch_shapes=[pltpu.VMEM((NSTATE, D), F32), pltpu.VMEM((tt, D), F32), pltpu.VMEM((tt, 512), F32),
                        pltpu.VMEM((tt, D), F32), pltpu.VMEM((tt, D), F32), pltpu.VMEM((tt, D), F32)],
        compiler_params=_cparams(("arbitrary",)), name="mixer_fwd")(
            *([proj] * 11), prm["scw"], prm["cwx"], prm["cwbc"], prm["cbx"], prm["cbbc"], prm["dtb"],
            prm["alog"], prm["dskx"], prm["nrm"], prm["eh"])


def mixer_bwd(proj, dy, states, aux, prm, tt):
    t_len = proj.shape[0]
    nblk = t_len // tt
    nc = tt // CHUNK

    def rev(i):
        return nblk - 1 - i

    def main(width, col):
        return pl.BlockSpec((tt, width), lambda i, c=col // width: (rev(i), c))

    def body(ub_ref, uc_ref, uh_ref, z_ref, xr_ref, bcr_ref, dtr_ref, dy_ref, st_ref, aux_ref,
             scw_ref, cwx_ref, cwbc_ref, cbx_ref, cbbc_ref, dtb_ref, alog_ref, dsk_ref, nrm_ref, eh_ref, eht_ref,
             dp_ref, gscw_ref, gcwx_ref, gcwbc_ref, gvec_ref, gdt_ref,
             dhs, xs_s, bc_s, dtx_s, cumx_s, dys_s, dxs_s, dbc_s, red_s, ddtx_s, nx_cv, nx_px, nx_pbc):
        i = pl.program_id(0)

        @pl.when(i == 0)
        def _():
            dhs[...] = jnp.zeros_like(dhs)
            nx_cv[...] = jnp.zeros_like(nx_cv)
            nx_px[...] = jnp.zeros_like(nx_px)
            nx_pbc[...] = jnp.zeros_like(nx_pbc)
            gscw_ref[...] = jnp.zeros_like(gscw_ref)
            gcwx_ref[...] = jnp.zeros_like(gcwx_ref)
            gcwbc_ref[...] = jnp.zeros_like(gcwbc_ref)
            gvec_ref[...] = jnp.zeros_like(gvec_ref)
            gdt_ref[...] = jnp.zeros_like(gdt_ref)

        uc = uc_ref[...].astype(F32)
        uh = uh_ref[...].astype(F32)
        v = uc * uh
        dya = dy_ref[:, 0:D].astype(F32)
        dp_ref[:, C_UB:C_UB + D] = (dya * aux_ref[:, A_CV:A_CV + D].astype(F32)).astype(BF16)
        dcv = dya * ub_ref[...].astype(F32)
        nxt = nx_cv[...]
        dv = None
        for k in range(3):
            up = _shift_up(dcv, nxt, 2 - k)
            gscw_ref[k:k + 1, :] += jnp.sum(v * up, axis=0, keepdims=True)
            term = scw_ref[k:k + 1, :] * up
            dv = term if dv is None else dv + term
        nx_cv[...] = dcv[0:8]
        dp_ref[:, C_UC:C_UC + D] = (dv * uh).astype(BF16)
        dp_ref[:, C_UH:C_UH + D] = (dv * uc).astype(BF16)

        pre_x = aux_ref[:, A_PX:A_PX + D].astype(F32)
        pre_bc = aux_ref[:, A_PBC:A_PBC + 512].astype(F32)
        xs = _silu(pre_x)
        xs_s[...] = xs
        bc_s[...] = _silu(pre_bc)
        dt_pre = dtr_ref[...].astype(F32) + dtb_ref[...]
        dt = _softplus(dt_pre)
        a_neg = -jnp.exp(alog_ref[...])
        pos = lax.broadcasted_iota(jnp.int32, (tt, LANES), 0) & (CHUNK - 1)
        cum = _chunk_cumsum(dt * a_neg, pos)
        eh = eh_ref[...]
        eht = eht_ref[...]
        dtx_s[...] = _expand(dt, eh)
        cumx_s[...] = _expand(cum, eh)

        irep, causal = _lane_masks()
        irep_g, _ = _lane_masks(512)
        rep = _rep_matrix()
        row64 = lax.broadcasted_iota(jnp.int32, (CHUNK, 512), 0)
        lane128 = lax.broadcasted_iota(jnp.int32, (CHUNK, LANES), 1)

        z = z_ref[...].astype(F32)
        sz = _silu(z)
        ys = aux_ref[:, A_YS:A_YS + D].astype(F32)
        gt = ys * sz
        dyb = dy_ref[:, D:MIX].astype(F32)
        for g in range(2):
            gs = slice(g * 512, (g + 1) * 512)
            gg = gt[:, gs]
            rn = lax.rsqrt(jnp.mean(gg * gg, axis=-1, keepdims=True) + EPS)
            gvec_ref[0:1, gs] += jnp.sum(dyb[:, gs] * gg * rn, axis=0, keepdims=True)
            dgn = dyb[:, gs] * nrm_ref[:, gs]
            dgt = rn * (dgn - gg * (rn * rn) * jnp.mean(dgn * gg, axis=-1, keepdims=True))
            dys = dgt * sz[:, gs]
            dys_s[:, gs] = dys
            dp_ref[:, C_Z + g * 512:C_Z + (g + 1) * 512] = (dgt * ys[:, gs] * _dsilu(z[:, gs])).astype(BF16)
        dys_all = dys_s[...]
        gvec_ref[1:2, :] += jnp.sum(dys_all * xs, axis=0, keepdims=True)

        def bwd_chunk(cc, carry):
            c = nc - 1 - cc
            r0 = pl.multiple_of(c * CHUNK, CHUNK)
            rows = pl.ds(r0, CHUNK)
            cumx = cumx_s[rows, :]
            cum_l = cumx[CHUNK - 1:CHUNK, :]
            xs_c = xs_s[rows, :]
            dtx = dtx_s[rows, :]
            xd = xs_c * dtx
            f = jnp.exp(cum_l - cumx)
            xf = xd * f
            ex = jnp.exp(cumx)
            e_l = jnp.exp(cum_l)
            rvec = jnp.sum(jnp.where(irep, cumx, 0.0), axis=0, keepdims=True)
            lam = jnp.where(causal, jnp.exp(jnp.where(causal, cumx - rvec, 0.0)), 0.0)
            bc = bc_s[rows, :]
            dyc = dys_s[rows, :]
            for g in range(2):
                gs = slice(g * 512, (g + 1) * 512)
                bg = bc[:, g * NSTATE:(g + 1) * NSTATE].astype(BF16)
                cg = bc[:, 256 + g * NSTATE:256 + (g + 1) * NSTATE].astype(BF16)
                h0 = st_ref[c, :, gs]
                dh = dhs[:, gs]
                dh_b = dh.astype(BF16)
                xf_g = xf[:, gs]
                dxf = _nn(bg, dh_b)
                db = _nt(xf_g.astype(BF16), dh_b)
                s_rep = _nn(_nt(cg, bg).astype(BF16), rep)
                lam_g = lam[:, gs]
                m_g = s_rep * lam_g
                m_b = m_g.astype(BF16)
                ex_g = ex[:, gs]
                dy_g = dyc[:, gs]
                yo = _nn(cg, h0) * ex_g
                dg_b = (dy_g * ex_g).astype(BF16)
                dc = _nt(dg_b, h0)
                el_g = e_l[:, gs]
                dee = jnp.sum(dh * h0.astype(F32), axis=0, keepdims=True) * el_g
                dhs[:, gs] = dh * el_g + _tn(cg, dg_b)
                xd_b = xd[:, gs].astype(BF16)
                dy_b = dy_g.astype(BF16)
                dm_parts, dxd_parts = [], []
                for hp in range(4):
                    ps = slice(hp * LANES, (hp + 1) * LANES)
                    bd = _blockdiag(xd_b[:, ps])
                    dm_parts.append(_nt(dy_b[:, ps], bd))
                    t2 = _tn(m_b[:, ps], dy_b[:, ps])
                    dxd_parts.append(jnp.where(lane128 < HDIM, t2[0:CHUNK], t2[CHUNK:2 * CHUNK]))
                dm = jnp.concatenate(dm_parts, axis=1)
                dxd = jnp.concatenate(dxd_parts, axis=1) + dxf * f[:, gs]
                dseg = dm * m_g
                ds_b = _nt((dm * lam_g).astype(BF16), rep).astype(BF16)
                dc = dc + _nn(ds_b, bg)
                db = db + _tn(ds_b, cg)
                colsum = jnp.sum(dseg, axis=0, keepdims=True)
                dxfxf = dxf * xf_g
                red = dseg - jnp.where(irep_g, colsum, 0.0) + dy_g * yo - dxfxf
                last = jnp.sum(dxfxf, axis=0, keepdims=True) + dee
                red = red + jnp.where(row64 == CHUNK - 1, last, 0.0)
                red_s[rows, gs] = red
                ddtx_s[rows, gs] = dxd * xs_c[:, gs]
                dxs_s[rows, gs] = dxd * dtx[:, gs] + dsk_ref[:, gs] * dy_g
                dbc_s[rows, g * NSTATE:(g + 1) * NSTATE] = db
                dbc_s[rows, 256 + g * NSTATE:256 + (g + 1) * NSTATE] = dc
            return carry

        lax.fori_loop(0, nc, bwd_chunk, 0, unroll=True)

        dcum = _head_reduce(red_s[...], eht)
        da = _chunk_rcumsum(dcum, pos)
        ddt = _head_reduce(ddtx_s[...], eht) + da * a_neg
        gdt_ref[1:2, :] += jnp.sum(da * dt, axis=0, keepdims=True) * a_neg
        ddt_raw = ddt * _sigmoid(dt_pre)
        lane_t = lax.broadcasted_iota(jnp.int32, (tt, LANES), 1)
        ddt_raw = jnp.where(lane_t < NHEAD, ddt_raw, 0.0)
        gdt_ref[0:1, :] += jnp.sum(ddt_raw, axis=0, keepdims=True)
        dp_ref[:, C_DT:C_DT + LANES] = ddt_raw.astype(BF16)

        dpx = dxs_s[...] * _dsilu(pre_x)
        dpbc = dbc_s[...] * _dsilu(pre_bc)
        gvec_ref[2:3, :] += jnp.sum(dpx, axis=0, keepdims=True)
        gcwbc_ref[4:5, :] += jnp.sum(dpbc, axis=0, keepdims=True)
        xraw = xr_ref[...].astype(F32)
        bcraw = bcr_ref[...].astype(F32)
        nxt_x = nx_px[...]
        nxt_bc = nx_pbc[...]
        dxr, dbcr = None, None
        for k in range(4):
            up_x = _shift_up(dpx, nxt_x, 3 - k)
            up_bc = _shift_up(dpbc, nxt_bc, 3 - k)
            gcwx_ref[k:k + 1, :] += jnp.sum(xraw * up_x, axis=0, keepdims=True)
            gcwbc_ref[k:k + 1, :] += jnp.sum(bcraw * up_bc, axis=0, keepdims=True)
            tx = cwx_ref[k:k + 1, :] * up_x
            tb = cwbc_ref[k:k + 1, :] * up_bc
            dxr = tx if dxr is None else dxr + tx
            dbcr = tb if dbcr is None else dbcr + tb
        nx_px[...] = dpx[0:8]
        nx_pbc[...] = dpbc[0:8]
        dp_ref[:, C_XS:C_XS + D] = dxr.astype(BF16)
        dp_ref[:, C_BC:C_BC + 512] = dbcr.astype(BF16)

        @pl.when(i == nblk - 1)
        def _():
            gdt_ref[2:3, :] = _head_reduce(gvec_ref[1:2, :] * jnp.ones((8, 1), F32), eht)[0:1, :]

    def const(shape):
        return pl.BlockSpec(shape, lambda i: (0, 0))

    in_specs = [main(D, C_UB), main(D, C_UC), main(D, C_UH), main(D, C_Z), main(D, C_XS), main(512, C_BC),
                main(LANES, C_DT),
                pl.BlockSpec((tt, MIX), lambda i: (rev(i), 0)),
                pl.BlockSpec((nc, NSTATE, D), lambda i: (rev(i), 0, 0)),
                pl.BlockSpec((tt, AUX_W), lambda i: (rev(i), 0)),
                const((8, D)), const((8, D)), const((8, 512)), const((1, D)), const((1, 512)),
                const((1, LANES)), const((1, LANES)), const((1, D)), const((1, D)), const((LANES, D)),
                const((D, LANES))]
    return pl.pallas_call(
        body, grid=(nblk,),
        in_specs=in_specs,
        out_specs=[pl.BlockSpec((tt, NINP), lambda i: (rev(i), 0)),
                   const((8, D)), const((8, D)), const((8, 512)), const((8, D)), const((8, LANES))],
        out_shape=[SDS((t_len, NINP), BF16), SDS((8, D), F32), SDS((8, D), F32), SDS((8, 512), F32),
                   SDS((8, D), F32), SDS((8, LANES), F32)],
        scratch_shapes=[pltpu.VMEM((NSTATE, D), F32),
                        pltpu.VMEM((tt, D), F32), pltpu.VMEM((tt, 512), F32),
                        pltpu.VMEM((tt, D), F32), pltpu.VMEM((tt, D), F32),
                        pltpu.VMEM((tt, D), F32), pltpu.VMEM((tt, D), F32),
                        pltpu.VMEM((tt, 512), F32),
                        pltpu.VMEM((tt, D), F32), pltpu.VMEM((tt, D), F32),
                        pltpu.VMEM((8, D), F32), pltpu.VMEM((8, D), F32), pltpu.VMEM((8, 512), F32)],
        compiler_params=_cparams(("arbitrary",)), name="mixer_bwd")(
            *([proj] * 7), dy, states, aux, prm["scw"], prm["cwx"], prm["cwbc"], prm["cbx"], prm["cbbc"], prm["dtb"],
            prm["alog"], prm["dskx"], prm["nrm"], prm["eh"], prm["eht"])


TN_IN = 1920


def layer_fwd_mix(x, lw, prm, tt):
    proj, h1 = norm_matmul(x, lw["nw1"], lw["win"], "in_proj")
    y, st, aux = mixer_fwd(proj, prm, tt)
    return h1, proj, (st, aux), y


def layer_fwd_mlp(x, mixed, lw):
    h1, proj, st, y = mixed
    x1 = matmul_residual(y, lw["wout"], x, False, "out_proj")
    u, h2 = norm_matmul(x1, lw["nw2"], lw["wup"], "up_proj")
    x2 = matmul_residual(u, lw["wdn"], x1, True, "down_proj")
    return x2, (x, h1, proj, st, y, x1, h2, u)


def layer_fwd(x, lw, prm, tt):
    return layer_fwd_mlp(x, layer_fwd_mix(x, lw, prm, tt), lw)


def _dw(a, b, a_cols, b_cols, relu2, name):
    m_len, n_len = a.shape[1], b.shape[1]
    n_a, n_b = m_len // a_cols, n_len // b_cols
    assert n_a == 1 or n_b == 1
    if n_b == 1:
        return matmul_tn(
            a, b,
            lambda t_: pl.BlockSpec((t_, a_cols), lambda n, t: (t, n)),
            lambda t_: pl.BlockSpec((t_, n_len), lambda n, t: (t, 0)),
            pl.BlockSpec((a_cols, n_len), lambda n, t: (n, 0)), SDS((m_len, n_len), BF16), n_a, relu2, name)
    return matmul_tn(
        a, b,
        lambda t_: pl.BlockSpec((t_, m_len), lambda n, t: (t, 0)),
        lambda t_: pl.BlockSpec((t_, b_cols), lambda n, t: (t, n)),
        pl.BlockSpec((m_len, b_cols), lambda n, t: (0, n)), SDS((m_len, n_len), BF16), n_b, relu2, name)


def layer_bwd_mlp(dx2, dx2b, lw, saved):
    _, _, _, _, y, x1, h2, u = saved
    du = matmul_nt_act(dx2b, lw["wdn"], u, "mlp_bwd_du")
    g_wdn = _dw(u, dx2b, 1024, D, True, "dw_down")
    dx1, dx1b, g_nw2 = matmul_nt_norm_bwd(du, lw["wup"], x1, lw["nw2"], dx2, "mlp_bwd_dx")
    cb = DFF // N_DEV
    g_wup = matmul_tn(
        h2, du,
        lambda t_: pl.BlockSpec((t_, D), lambda n, t: (t, 0)),
        lambda t_: pl.BlockSpec((t_, 2 * cb), lambda n, t: (t, n)),
        pl.BlockSpec((2, D, cb), lambda n, t: (n, 0, 0)), SDS((N_DEV, D, cb), BF16), N_DEV // 2, False, "dw_up")
    dy = matmul_nt_act(dx1b, lw["wout"], None, "out_bwd_dy")
    g_wout = _dw(y, dx1b, 1024, D, False, "dw_out")
    return dx1, dx1b, dy, {"wout": g_wout, "wup": g_wup, "wdn": g_wdn, "nw2": g_nw2[0]}


def layer_bwd_mix(dx1, dy, lw, prm, saved, tt):
    x, h1, proj, st = saved[:4]
    dproj, gscw, gcwx, gcwbc, gvec, gdt = mixer_bwd(proj, dy, st[0], st[1], prm, tt)
    dx0, dx0b, g_nw1 = matmul_nt_norm_bwd(dproj, lw["win"], x, lw["nw1"], dx1, "in_bwd_dx")
    g_win = _dw(h1, dproj, D, TN_IN, False, "dw_in")
    grads = {
        "win": g_win, "scw": gscw[0:3], "cw": jnp.concatenate([gcwx[0:4], gcwbc[0:4]], axis=1),
        "cb": jnp.concatenate([gvec[2], gcwbc[4]], axis=0),
        "dtb": gdt[0, :NHEAD], "alog": gdt[1, :NHEAD], "dsk": gdt[2, :NHEAD],
        "nrm": gvec[0], "nw1": g_nw1[0],
    }
    return dx0, dx0b, grads


def layer_bwd(dx2, dx2b, lw, prm, saved, tt):
    dx1, dx1b, dy, g_mlp = layer_bwd_mlp(dx2, dx2b, lw, saved)
    dx0, dx0b, g_mix = layer_bwd_mix(dx1, dy, lw, prm, saved, tt)
    return dx0, dx0b, {**g_mlp, **g_mix}


def layer_params(win, scw, cw, nw1, nw2, conv_b, dt_bias, a_log, d_skip, ssd_norm_w, eh, eht):
    def rows8(a):
        return jnp.pad(a, ((0, 8 - a.shape[0]), (0, 0)))

    def lanes128(a):
        return jnp.pad(a, (0, LANES - a.shape[0]))[None, :]

    lw = {"win": win, "nw1": nw1[None, :], "nw2": nw2[None, :]}
    prm = {"scw": rows8(scw), "cwx": rows8(cw[:, :D]), "cwbc": rows8(cw[:, D:]),
           "cbx": conv_b[None, :D], "cbbc": conv_b[None, D:],
           "dtb": lanes128(dt_bias), "alog": lanes128(a_log),
           "dskx": jnp.repeat(d_skip, HDIM)[None, :], "nrm": ssd_norm_w[None, :], "eh": eh, "eht": eht}
    return lw, prm


def _flip(v, bit):
    return 1 - v if bit else v


def all_gather(arrs, name):
    n = len(arrs)

    def body(*refs):
        ins, outs = refs[:n], refs[n:2 * n]
        send_sems, recv_sems, local_sems = refs[2 * n:]
        x, y, c = lax.axis_index("x"), lax.axis_index("y"), lax.axis_index("c")
        sibling = (x, y, 1 - c)
        chips = [(1 - x, y), (x, 1 - y), (1 - x, 1 - y)]

        def idx(px, py, pc):
            return 4 * px + 2 * py + pc

        def copy(a, k, block, to, src=None):
            dst = outs[a].at[idx(*block)]
            return pltpu.make_async_remote_copy(
                src_ref=dst if src is None else src, dst_ref=dst,
                send_sem=send_sems.at[a, k], recv_sem=recv_sems.at[a, k], device_id=to, device_id_type=MESH)

        me = (x, y, c)
        mine = [pltpu.make_async_copy(ins[a], outs[a].at[idx(*me)], local_sems.at[a]) for a in range(n)]
        for cp in mine:
            cp.start()
        first = []
        for a in range(n):
            first.append(copy(a, 0, me, sibling, src=ins[a]))
            first += [copy(a, 1 + j, me, (*chip, c), src=ins[a]) for j, chip in enumerate(chips)]
        for cp in first:
            cp.start()
        passed = []
        for j, chip in enumerate(chips):
            for a in range(n):
                copy(a, 1 + j, (*chip, c), me).wait_recv()
                cp = copy(a, 4 + j, (*chip, c), sibling)
                cp.start()
                passed.append(cp)
        for a in range(n):
            copy(a, 0, sibling, me).wait_recv()
            for j, chip in enumerate(chips):
                copy(a, 4 + j, (*chip, 1 - c), me).wait_recv()
        for cp in first + passed:
            cp.wait_send()
        for cp in mine:
            cp.wait()

    any_spec = pl.BlockSpec(memory_space=pl.ANY)
    return pl.pallas_call(
        body, in_specs=[any_spec] * n, out_specs=[any_spec] * n,
        out_shape=[SDS((N_DEV,) + a.shape, a.dtype) for a in arrs],
        scratch_shapes=[pltpu.SemaphoreType.DMA((n, 7)), pltpu.SemaphoreType.DMA((n, 7)),
                        pltpu.SemaphoreType.DMA((n,))],
        name=name)(*arrs)


HBM_SPEC = pl.BlockSpec(memory_space=pltpu.HBM)
SEM_SPEC = pl.BlockSpec(memory_space=pltpu.SEMAPHORE)
SIDE_EFFECT = pltpu.SideEffectType.DATAFLOW_SIDE_EFFECTING
N_PEER = N_DEV - 1


def _peer(mask):
    x, y, c = lax.axis_index("x"), lax.axis_index("y"), lax.axis_index("c")
    return _flip(x, mask & 4), _flip(y, mask & 2), _flip(c, mask & 1)


ALL_PEERS = tuple(range(1, N_DEV))
SIBLING_AND_CHIPS = (1, 2, 4, 6)


def exchange_start(srcs, per_peer, name, after=None, masks=ALL_PEERS):
    n = len(srcs)
    npeer = len(masks)
    lands = [SDS((N_DEV,) + (a.shape[1:] if per_peer else a.shape), a.dtype) for a in srcs]
    n_in = 2 * n + (after is not None)

    def body(*refs):
        src_refs, land_refs = refs[:n], refs[n:2 * n]
        send_sems, recv_sems = refs[n_in], refs[n_in + 1]
        token = refs[-1]
        x, y, c = lax.axis_index("x"), lax.axis_index("y"), lax.axis_index("c")
        me = 4 * x + 2 * y + c
        for a in range(n):
            for k, mask in enumerate(masks):
                px, py, pc = _peer(mask)
                part = src_refs[a].at[4 * px + 2 * py + pc] if per_peer else src_refs[a]
                pltpu.make_async_remote_copy(
                    src_ref=part, dst_ref=land_refs[a].at[me], send_sem=send_sems.at[a * npeer + k],
                    recv_sem=recv_sems.at[a * npeer + k], device_id=(px, py, pc), device_id_type=MESH).start()
        token[...] = jnp.zeros_like(token)

    out = pl.pallas_call(
        body, name=name,
        out_shape=(pltpu.SemaphoreType.DMA((n * npeer,)), pltpu.SemaphoreType.DMA((n * npeer,)),
                   *[pltpu.HBM(a.shape, a.dtype) for a in srcs], *[pltpu.HBM(l.shape, l.dtype) for l in lands],
                   SDS((8, LANES), F32)),
        in_specs=(HBM_SPEC,) * (2 * n) + ((pl.BlockSpec(memory_space=pl.ANY),) if after is not None else ()),
        out_specs=(SEM_SPEC, SEM_SPEC) + (HBM_SPEC,) * (2 * n) + (pl.BlockSpec(memory_space=pltpu.VMEM),),
        input_output_aliases={k: 2 + k for k in range(2 * n)},
        compiler_params=pltpu.CompilerParams(has_side_effects=SIDE_EFFECT),
    )(*[pltpu.with_memory_space_constraint(a, pltpu.HBM) for a in srcs],
      *[pltpu.with_memory_space_constraint(lax.empty(l.shape, l.dtype), pltpu.HBM) for l in lands],
      *([after] if after is not None else []))
    return out[0], out[1], list(out[2:2 + n]), list(out[2 + n:2 + 2 * n]), out[-1]


def exchange_wait(started, after, per_peer, name, masks=ALL_PEERS):
    send_sems, recv_sems, srcs, lands, _ = started
    n = len(srcs)
    npeer = len(masks)

    def body(*refs):
        src_refs, land_refs = refs[:n], refs[n:2 * n]
        send_sems, recv_sems = refs[2 * n], refs[2 * n + 1]
        for k, mask in enumerate(masks):
            for a in range(n):
                copy = pltpu.make_async_remote_copy(
                    src_ref=src_refs[a].at[0] if per_peer else src_refs[a], dst_ref=land_refs[a].at[0],
                    send_sem=send_sems.at[a * npeer + k], recv_sem=recv_sems.at[a * npeer + k],
                    device_id=_peer(mask), device_id_type=MESH)
                copy.wait_send()
                copy.wait_recv()

    out = pl.pallas_call(
        body, name=name,
        out_shape=tuple(pltpu.HBM(a.shape, a.dtype) for a in srcs + lands),
        in_specs=(HBM_SPEC,) * (2 * n) + (SEM_SPEC, SEM_SPEC, pl.BlockSpec(memory_space=pl.ANY)),
        out_specs=(HBM_SPEC,) * (2 * n), input_output_aliases={k: k for k in range(2 * n)},
        compiler_params=pltpu.CompilerParams(has_side_effects=SIDE_EFFECT),
    )(*srcs, *lands, send_sems, recv_sems, after)
    return list(out[:n]), list(out[n:])


def relay_to_sibling(lands, name):
    n = len(lands)
    chips = (2, 4, 6)

    def body(*refs):
        land_refs = refs[n:2 * n]
        send_sems, recv_sems = refs[2 * n], refs[2 * n + 1]
        x, y, c = lax.axis_index("x"), lax.axis_index("y"), lax.axis_index("c")
        copies = []
        for a in range(n):
            for k, mask in enumerate(chips):
                px, py, _ = _peer(mask)
                block = land_refs[a].at[4 * px + 2 * py + c]
                cp = pltpu.make_async_remote_copy(
                    src_ref=block, dst_ref=block, send_sem=send_sems.at[a * 3 + k], recv_sem=recv_sems.at[a * 3 + k],
                    device_id=(x, y, 1 - c), device_id_type=MESH)
                cp.start()
                copies.append((cp, a, k, land_refs[a].at[4 * px + 2 * py + 1 - c]))
        for cp, a, k, arriving in copies:
            cp.wait_send()
            pltpu.make_async_remote_copy(
                src_ref=arriving, dst_ref=arriving, send_sem=send_sems.at[a * 3 + k], recv_sem=recv_sems.at[a * 3 + k],
                device_id=(x, y, 1 - c), device_id_type=MESH).wait_recv()

    any_spec = pl.BlockSpec(memory_space=pl.ANY)
    return list(pl.pallas_call(
        body, in_specs=[any_spec] * n, out_specs=[any_spec] * n,
        out_shape=[SDS(a.shape, a.dtype) for a in lands],
        input_output_aliases={k: k for k in range(n)},
        scratch_shapes=[pltpu.SemaphoreType.DMA((n * 3,)), pltpu.SemaphoreType.DMA((n * 3,))],
        name=name)(*lands))


IN_SHARD = NIN // N_DEV
SLOT_W = 768


def _slot_window(j):
    return (IN_SHARD * j // LANES) * LANES, -(-(IN_SHARD * (j + 1)) // LANES) * LANES


def _placement(j):
    a, b = _slot_window(j)
    r = lax.broadcasted_iota(jnp.int32, (SLOT_W, b - a), 0)
    c = lax.broadcasted_iota(jnp.int32, (SLOT_W, b - a), 1)
    return jnp.where(jnp.logical_and(c == r + (IN_SHARD * j - a), r < IN_SHARD), 1.0, 0.0).astype(BF16)


def assemble_w_in(land):
    tm = 256

    def body(l_ref, o_ref, acc):
        acc[...] = jnp.zeros_like(acc)
        for j in range(N_DEV):
            a, b = _slot_window(j)
            acc[:, a:b] += _nn(l_ref[j], _placement(j))
        o_ref[...] = acc[...].astype(BF16)

    return pl.pallas_call(
        body, grid=(D // tm,),
        in_specs=[pl.BlockSpec((N_DEV, tm, SLOT_W), lambda i: (0, i, 0))],
        out_specs=pl.BlockSpec((tm, NINP), lambda i: (i, 0)),
        out_shape=SDS((D, NINP), BF16),
        scratch_shapes=[pltpu.VMEM((tm, NINP), F32)],
        compiler_params=_cparams(("parallel",)), name="assemble_w_in")(land)


def scatter_w_in(dw):
    tm = 256

    def body(d_ref, o_ref):
        for j in range(N_DEV):
            a, b = _slot_window(j)
            o_ref[j] = _nt(d_ref[:, a:b], _placement(j)).astype(BF16)

    return pl.pallas_call(
        body, grid=(D // tm,),
        in_specs=[pl.BlockSpec((tm, NINP), lambda i: (i, 0))],
        out_specs=pl.BlockSpec((N_DEV, tm, SLOT_W), lambda i: (0, i, 0)),
        out_shape=SDS((N_DEV, D, SLOT_W), BF16),
        compiler_params=_cparams(("parallel",)), name="scatter_w_in")(dw)


def _adamw_math(g, w_ref, m_ref, v_ref, g_ref, d_ref, nm_ref, nv_ref):
    mn = ADAM_B1 * m_ref[...] + (1.0 - ADAM_B1) * g
    vn = ADAM_B2 * v_ref[...] + (1.0 - ADAM_B2) * jnp.square(g)
    m_hat = mn / (1.0 - ADAM_B1 ** ADAM_STEP)
    v_hat = vn / (1.0 - ADAM_B2 ** ADAM_STEP)
    g_ref[...] = g
    d_ref[...] = -ADAM_LR * (m_hat / (jnp.sqrt(v_hat) + ADAM_EPS) + ADAM_WD * w_ref[...])
    nm_ref[...] = mn
    nv_ref[...] = vn


def adamw_layers(w, slots, m, v, name):
    depth, r_len, c_len = w.shape
    cs = slots[0].shape[2]
    br = min(128, r_len)
    assert r_len % br == 0

    def body(w_ref, *rest):
        s_refs, (m_ref, v_ref, g_ref, d_ref, nm_ref, nv_ref) = rest[:depth], rest[depth:]
        layer = pl.program_id(0)
        for k in range(depth):
            @pl.when(layer == k)
            def _(k=k):
                g = s_refs[k][0, :, 0:c_len].astype(F32)
                for j in range(1, N_DEV):
                    g = g + s_refs[k][j, :, 0:c_len].astype(F32)
                _adamw_math(g, w_ref, m_ref, v_ref, g_ref, d_ref, nm_ref, nv_ref)

    spec = pl.BlockSpec((None, br, c_len), lambda l, i: (l, i, 0))
    s_specs = [pl.BlockSpec((N_DEV, br, cs), lambda l, i, k=k: (0, jnp.where(l == k, i, 0), 0))
               for k in range(depth)]
    return pl.pallas_call(
        body, grid=(depth, r_len // br),
        in_specs=[spec] + s_specs + [spec, spec],
        out_specs=[spec] * 4, out_shape=[SDS(w.shape, F32)] * 4,
        compiler_params=_cparams(("arbitrary", "arbitrary")), name=name)(w, *slots, m, v)


def adamw(w, slots, m, v, name):
    r_len, c_len = w.shape
    br = r_len if r_len <= 512 else 512
    assert r_len % br == 0

    def body(w_ref, s_ref, m_ref, v_ref, g_ref, d_ref, nm_ref, nv_ref):
        g = s_ref[0].astype(F32)
        for k in range(1, N_DEV):
            g = g + s_ref[k].astype(F32)
        _adamw_math(g, w_ref, m_ref, v_ref, g_ref, d_ref, nm_ref, nv_ref)

    spec = pl.BlockSpec((br, c_len), lambda i: (i, 0))
    return pl.pallas_call(
        body, grid=(r_len // br,),
        in_specs=[spec, pl.BlockSpec((N_DEV, br, c_len), lambda i: (0, i, 0)), spec, spec],
        out_specs=[spec] * 4, out_shape=[SDS((r_len, c_len), F32)] * 4,
        compiler_params=_cparams(("parallel",)), name=name)(w, slots, m, v)


def _adamw_nd(w, slots, m, v, name):
    shp = w.shape
    r = int(np.prod(shp[:-1]))
    outs = adamw(w.reshape(r, shp[-1]), slots.reshape(N_DEV, r, shp[-1]), m.reshape(r, shp[-1]),
                 v.reshape(r, shp[-1]), name)
    return [o.reshape(shp) for o in outs]


SMALL = [("norm_mix_w", DEPTH * D), ("ssd_conv_b", DEPTH * XBC), ("dt_bias", DEPTH * NHEAD),
         ("a_log", DEPTH * NHEAD), ("d_skip", DEPTH * NHEAD), ("ssd_norm_w", DEPTH * D),
         ("norm_mlp_w", DEPTH * D), ("final_norm_w", D)]
SMALL_LEN = sum(s for _, s in SMALL)
SMALL_ROWS = -(-SMALL_LEN // LANES)


def _pack_small(parts):
    flat = jnp.concatenate([parts[k].reshape(-1) for k, _ in SMALL])
    return jnp.pad(flat, (0, SMALL_ROWS * LANES - SMALL_LEN)).reshape(SMALL_ROWS, LANES)


def _unpack_small(packed, shapes):
    flat = packed.reshape(-1)
    out, off = {}, 0
    for k, s in SMALL:
        out[k] = flat[off:off + s].reshape(shapes[k])
        off += s
    return out


def kernel(x, norm_mix_w, w_in, short_conv_w, ssd_conv_w, ssd_conv_b, dt_bias, a_log, d_skip, ssd_norm_w, w_out, norm_mlp_w, w_up, w_down, final_norm_w, loss_target, m_norm_mix_w, m_w_in, m_short_conv_w, m_ssd_conv_w, m_ssd_conv_b, m_dt_bias, m_a_log, m_d_skip, m_ssd_norm_w, m_w_out, m_norm_mlp_w, m_w_up, m_w_down, m_final_norm_w, v_norm_mix_w, v_w_in, v_short_conv_w, v_ssd_conv_w, v_ssd_conv_b, v_dt_bias, v_a_log, v_d_skip, v_ssd_norm_w, v_w_out, v_norm_mlp_w, v_w_up, v_w_down, v_final_norm_w):
    xs = x[0]
    t_len = xs.shape[0]
    tt = min(256, t_len)
    eh, eht = _head_matrices()
    me = 4 * lax.axis_index("x") + 2 * lax.axis_index("y") + lax.axis_index("c")

    def start_weights(i, after):
        first = exchange_start(
            [jnp.pad(w_in[i].astype(BF16), ((0, 0), (0, SLOT_W - IN_SHARD))), short_conv_w[i], ssd_conv_w[i]],
            False, "w_in_start_%d" % i, after, SIBLING_AND_CHIPS)
        rest = exchange_start([w_out[i].astype(BF16), w_up[i].astype(BF16), w_down[i].astype(BF16)], False,
                              "w_rest_start_%d" % i, first[4] if after is None else after, SIBLING_AND_CHIPS)
        return first, rest

    def fill_own(srcs, lands, per_peer):
        own = [lax.dynamic_index_in_dim(s_, me, 0, keepdims=False) for s_ in srcs] if per_peer else srcs
        return [lax.dynamic_update_index_in_dim(l_, o_, me, 0) for l_, o_ in zip(lands, own)]

    def finish_weights(started, after, name):
        srcs, lands = exchange_wait(started, after, False, name + "_wait", SIBLING_AND_CHIPS)
        return fill_own(srcs, relay_to_sibling(lands, name + "_relay"), False)

    act = xs
    saved, layers = [], []
    first, rest = start_weights(0, None)
    token = first[4][0, 0] + rest[4][0, 0]
    for i in range(DEPTH):
        g_in, g_sc, g_cw = finish_weights(first, act, "w_in_%d" % i)
        lw, prm = layer_params(
            assemble_w_in(g_in), g_sc.transpose(1, 0, 2).reshape(3, D), g_cw.transpose(1, 0, 2).reshape(4, XBC),
            norm_mix_w[i], norm_mlp_w[i], ssd_conv_b[i], dt_bias[i], a_log[i], d_skip[i], ssd_norm_w[i], eh, eht)
        lw["nw1"] = lw["nw1"] + token
        mixed = layer_fwd_mix(act, lw, prm, tt)
        g_out, g_up, g_dn = finish_weights(rest, mixed[3], "w_rest_%d" % i)
        lw.update(wout=g_out.reshape(MIX, D), wup=g_up, wdn=g_dn.reshape(DFF, D))
        if i + 1 < DEPTH:
            first, rest = start_weights(i + 1, g_dn)
            token = first[4][0, 0] + rest[4][0, 0]
            lw["nw2"] = lw["nw2"] + token
        layers.append((lw, prm))
        act, sv = layer_fwd_mlp(act, mixed, lw)
        saved.append(sv)
    loss_acc, dx, dxb, g_fw = loss_head(act, final_norm_w[None, :], loss_target[0])

    grads = [None] * DEPTH
    sent_rest, sent_in = [None] * DEPTH, [None] * DEPTH
    token = None
    for i in reversed(range(DEPTH)):
        lw, prm = layers[i]
        if token is not None:
            lw = dict(lw, nw2=lw["nw2"] + token)
        dx1, _, dy, g_mlp = layer_bwd_mlp(dx, dxb, lw, saved[i])
        sent_rest[i] = exchange_start(
            [g_mlp["wout"].reshape(N_DEV, MIX // N_DEV, D), g_mlp["wup"], g_mlp["wdn"].reshape(N_DEV, DFF // N_DEV, D)],
            True, "g_rest_start_%d" % i)
        dx, dxb, g_mix = layer_bwd_mix(dx1, dy, lw, dict(prm, nrm=prm["nrm"] + sent_rest[i][4][0, 0]), saved[i], tt)
        grads[i] = {**g_mlp, **g_mix}
        if i > 0:
            sent_in[i] = exchange_start([scatter_w_in(g_mix["win"])], True, "g_in_start_%d" % i)
            token = sent_in[i][4][0, 0]

    def stack(k):
        return jnp.stack([g[k] for g in grads])

    small = _pack_small({"norm_mix_w": stack("nw1"), "ssd_conv_b": stack("cb"), "dt_bias": stack("dtb"),
                         "a_log": stack("alog"), "d_skip": stack("dsk"), "ssd_norm_w": stack("nrm"),
                         "norm_mlp_w": stack("nw2"), "final_norm_w": g_fw[0]})
    r_small, r_sc, r_cw = all_gather([small, stack("scw"), stack("cw")], "gather_small_grads")
    r_sc = lax.dynamic_slice_in_dim(r_sc, me * (D // N_DEV), D // N_DEV, axis=3)
    r_cw = lax.dynamic_slice_in_dim(r_cw, me * (XBC // N_DEV), XBC // N_DEV, axis=3)
    sent_in[0] = exchange_start([scatter_w_in(grads[0]["win"])], True, "g_in_start_0", after=r_small)

    after = sent_in[0][4]
    recv = [fill_own(*exchange_wait(sent_rest[i], after, True, "g_rest_wait_%d" % i), True) for i in range(DEPTH)]
    res = {}
    res["w_out"] = adamw_layers(w_out, [r[0] for r in recv], m_w_out, v_w_out, "adamw_w_out")
    res["w_up"] = adamw_layers(w_up, [r[1] for r in recv], m_w_up, v_w_up, "adamw_w_up")
    res["w_down"] = adamw_layers(w_down, [r[2] for r in recv], m_w_down, v_w_down, "adamw_w_down")
    after = res["w_down"][1]
    recv_in = [fill_own(*exchange_wait(sent_in[i], after, True, "g_in_wait_%d" % i), True)[0] for i in range(DEPTH)]
    res["w_in"] = adamw_layers(w_in, recv_in, m_w_in, v_w_in, "adamw_w_in")
    res["short_conv_w"] = _adamw_nd(short_conv_w, r_sc, m_short_conv_w, v_short_conv_w, "adamw_short_conv")
    res["ssd_conv_w"] = _adamw_nd(ssd_conv_w, r_cw, m_ssd_conv_w, v_ssd_conv_w, "adamw_ssd_conv")
    small_w = {"norm_mix_w": norm_mix_w, "ssd_conv_b": ssd_conv_b, "dt_bias": dt_bias, "a_log": a_log,
               "d_skip": d_skip, "ssd_norm_w": ssd_norm_w, "norm_mlp_w": norm_mlp_w, "final_norm_w": final_norm_w}
    small_m = {"norm_mix_w": m_norm_mix_w, "ssd_conv_b": m_ssd_conv_b, "dt_bias": m_dt_bias, "a_log": m_a_log,
               "d_skip": m_d_skip, "ssd_norm_w": m_ssd_norm_w, "norm_mlp_w": m_norm_mlp_w,
               "final_norm_w": m_final_norm_w}
    small_v = {"norm_mix_w": v_norm_mix_w, "ssd_conv_b": v_ssd_conv_b, "dt_bias": v_dt_bias, "a_log": v_a_log,
               "d_skip": v_d_skip, "ssd_norm_w": v_ssd_norm_w, "norm_mlp_w": v_norm_mlp_w,
               "final_norm_w": v_final_norm_w}
    shapes = {k: a.shape for k, a in small_w.items()}
    packed = adamw(_pack_small(small_w), r_small, _pack_small(small_m), _pack_small(small_v), "adamw_small")
    unpacked = [_unpack_small(p, shapes) for p in packed]
    for k in small_w:
        res[k] = [u[k] for u in unpacked]

    loss = lax.psum(loss_acc[0, 0], ("x", "y", "c"))
    order = ["norm_mix_w", "w_in", "short_conv_w", "ssd_conv_w", "ssd_conv_b", "dt_bias", "a_log", "d_skip",
             "ssd_norm_w", "w_out", "norm_mlp_w", "w_up", "w_down", "final_norm_w"]
    out = [loss, dx[None]]
    for part in range(4):
        out += [res[k][part] for k in order]
    return tuple(out)
```

```python
import functools

import numpy as np
import jax
import jax.numpy as jnp
from jax import lax
from jax.experimental import pallas as pl
from jax.experimental.pallas import tpu as pltpu

F32 = jnp.float32
BF16 = jnp.bfloat16
SDS = jax.ShapeDtypeStruct

N_DEV = 8
DEPTH = 4
D = 1024
NIN = 5648
NINP = 5760
DFF = 4096
MIX = 2048
NHEAD = 16
HDIM = 64
NSTATE = 128
CHUNK = 64
XBC = 1536
EPS = 1e-5
LANES = 128

C_UB, C_UC, C_UH, C_Z, C_XS, C_BC, C_DT = 0, 1024, 2048, 3072, 4096, 5120, 5632
A_CV, A_YS, A_PX, A_PBC, AUX_W = 0, 1024, 2048, 3072, 3584

ADAM_LR = 0.001
ADAM_B1 = 0.9
ADAM_B2 = 0.999
ADAM_EPS = 1e-08
ADAM_WD = 0.01
ADAM_STEP = 10

VMEM_LIMIT = 56 * 1024 * 1024
MESH = pl.DeviceIdType.MESH


def _cparams(sem):
    return pltpu.CompilerParams(dimension_semantics=sem, vmem_limit_bytes=VMEM_LIMIT)


def _nt(a, b):
    return lax.dot_general(a, b, (((1,), (1,)), ((), ())), preferred_element_type=F32)


def _tn(a, b):
    return lax.dot_general(a, b, (((0,), (0,)), ((), ())), preferred_element_type=F32)


def _nn(a, b):
    return jnp.dot(a, b, preferred_element_type=F32)


def _sigmoid(v):
    return 0.5 * jnp.tanh(0.5 * v) + 0.5


def _split3(v):
    v1 = v.astype(BF16)
    r1 = v - v1.astype(F32)
    v2 = r1.astype(BF16)
    v3 = (r1 - v2.astype(F32)).astype(BF16)
    return v1, v2, v3


def _expand(v, eh):
    v1, v2, v3 = _split3(v)
    return _nn(v1, eh) + _nn(v2, eh) + _nn(v3, eh)


def _head_reduce(v, eht):
    v1 = v.astype(BF16)
    v2 = (v - v1.astype(F32)).astype(BF16)
    return _nn(v1, eht) + _nn(v2, eht)


def _head_matrices():
    eh = np.zeros((LANES, D), np.float32)
    for h in range(NHEAD):
        eh[h, h * HDIM:(h + 1) * HDIM] = 1.0
    return jnp.asarray(eh, BF16), jnp.asarray(eh.T.copy(), BF16)


def _resident(shape):
    return pl.BlockSpec(shape, lambda *_: (0,) * len(shape), pipeline_mode=pl.Buffered(1))


def _col_chunks(n, step):
    return [(c, min(c + step, n)) for c in range(0, n, step)]


def norm_matmul(x, nw, w, name):
    t_len = x.shape[0]
    blocked = w.ndim == 3
    n_len = w.shape[0] * w.shape[2] if blocked else w.shape[1]
    tm = min(512, t_len)
    chunks = _col_chunks(n_len, n_len // N_DEV if blocked else 1536)

    def body(x_ref, nw_ref, w_ref, o_ref, h_ref):
        xv = x_ref[...]
        r = lax.rsqrt(jnp.mean(xv * xv, axis=-1, keepdims=True) + EPS)
        hv = (xv * r * nw_ref[...]).astype(BF16)
        h_ref[...] = hv
        for j, (c0, c1) in enumerate(chunks):
            wj = w_ref[j] if blocked else w_ref[:, c0:c1]
            o_ref[:, c0:c1] = _nn(hv, wj).astype(o_ref.dtype)

    return pl.pallas_call(
        body, grid=(t_len // tm,),
        in_specs=[pl.BlockSpec((tm, D), lambda i: (i, 0)), _resident((1, D)), _resident(w.shape)],
        out_specs=[pl.BlockSpec((tm, n_len), lambda i: (i, 0)),
                   pl.BlockSpec((tm, D), lambda i: (i, 0))],
        out_shape=[SDS((t_len, n_len), BF16), SDS((t_len, D), BF16)],
        compiler_params=_cparams(("parallel",)), name=name)(x, nw, w)


def matmul_residual(a, w, res, relu2, name):
    t_len, k_len = a.shape
    tm = min(512, t_len)

    def body(a_ref, w_ref, res_ref, o_ref):
        av = a_ref[...]
        if relu2:
            af = jnp.maximum(av.astype(F32), 0.0)
            av = (af * af).astype(BF16)
        o_ref[...] = res_ref[...] + _nn(av, w_ref[...])

    return pl.pallas_call(
        body, grid=(t_len // tm,),
        in_specs=[pl.BlockSpec((tm, k_len), lambda i: (i, 0)),
                  _resident((k_len, D)),
                  pl.BlockSpec((tm, D), lambda i: (i, 0))],
        out_specs=pl.BlockSpec((tm, D), lambda i: (i, 0)),
        out_shape=SDS((t_len, D), F32),
        compiler_params=_cparams(("parallel",)), name=name)(a, w, res)


def matmul_nt_act(dy, w, u, name):
    t_len = dy.shape[0]
    n_len = w.shape[0]
    tm = min(512, t_len)
    chunks = _col_chunks(n_len, 1024)

    def body(dy_ref, w_ref, *rest):
        if u is None:
            (o_ref,) = rest
        else:
            u_ref, o_ref = rest
        dyv = dy_ref[...]
        for c0, c1 in chunks:
            p = _nt(dyv, w_ref[c0:c1, :])
            if u is not None:
                p = p * (2.0 * jnp.maximum(u_ref[:, c0:c1].astype(F32), 0.0))
            o_ref[:, c0:c1] = p.astype(o_ref.dtype)

    in_specs = [pl.BlockSpec((tm, D), lambda i: (i, 0)), _resident((n_len, D))]
    args = [dy, w]
    if u is not None:
        in_specs.append(pl.BlockSpec((tm, n_len), lambda i: (i, 0)))
        args.append(u)
    return pl.pallas_call(
        body, grid=(t_len // tm,),
        in_specs=in_specs,
        out_specs=pl.BlockSpec((tm, n_len), lambda i: (i, 0)),
        out_shape=SDS((t_len, n_len), BF16),
        compiler_params=_cparams(("parallel",)), name=name)(*args)


def matmul_tn(a, b, a_spec, b_spec, o_spec, o_shape, n_out, relu2, name):
    t_len = a.shape[0]
    tt = min(2048, t_len)
    nt = t_len // tt

    def body(a_ref, b_ref, o_ref, acc):
        t = pl.program_id(1)
        av = a_ref[...]
        if relu2:
            af = jnp.maximum(av.astype(F32), 0.0)
            av = (af * af).astype(BF16)
        p = _tn(av, b_ref[...])

        @pl.when(t == 0)
        def _():
            acc[...] = p

        @pl.when(t > 0)
        def _():
            acc[...] += p

        @pl.when(t == nt - 1)
        def _():
            if len(blk) == 3:
                for j in range(blk[0]):
                    o_ref[j] = acc[:, j * blk[2]:(j + 1) * blk[2]].astype(o_ref.dtype)
            else:
                o_ref[...] = acc[...].astype(o_ref.dtype)

    blk = tuple(o_spec.block_shape)
    acc_shape = (blk[1], blk[0] * blk[2]) if len(blk) == 3 else blk
    return pl.pallas_call(
        body, grid=(n_out, nt),
        in_specs=[a_spec(tt), b_spec(tt)],
        out_specs=o_spec, out_shape=o_shape,
        scratch_shapes=[pltpu.VMEM(acc_shape, F32)],
        compiler_params=_cparams(("parallel", "arbitrary")), name=name)(a, b)


def matmul_nt_norm_bwd(dy, w, x, nw, dres, name):
    t_len = x.shape[0]
    blocked = w.ndim == 3
    k_len = dy.shape[1]
    kb = k_len // N_DEV
    tm = min(512, t_len)

    def body(dy_ref, w_ref, x_ref, nw_ref, dres_ref, dx_ref, dxb_ref, dnw_ref):
        @pl.when(pl.program_id(0) == 0)
        def _():
            dnw_ref[...] = jnp.zeros_like(dnw_ref)

        if blocked:
            dh = _nt(dy_ref[:, 0:kb], w_ref[0])
            for j in range(1, N_DEV):
                dh = dh + _nt(dy_ref[:, j * kb:(j + 1) * kb], w_ref[j])
        else:
            dh = _nt(dy_ref[...], w_ref[...])
        xv = x_ref[...]
        r = lax.rsqrt(jnp.mean(xv * xv, axis=-1, keepdims=True) + EPS)
        xh = xv * r
        dnw_ref[0:1, :] += jnp.sum(dh * xh, axis=0, keepdims=True)
        g = dh * nw_ref[...]
        dx = dres_ref[...] + r * (g - xh * jnp.mean(g * xh, axis=-1, keepdims=True))
        dx_ref[...] = dx
        dxb_ref[...] = dx.astype(BF16)

    return pl.pallas_call(
        body, grid=(t_len // tm,),
        in_specs=[pl.BlockSpec((tm, k_len), lambda i: (i, 0)),
                  _resident(w.shape),
                  pl.BlockSpec((tm, D), lambda i: (i, 0)),
                  _resident((1, D)),
                  pl.BlockSpec((tm, D), lambda i: (i, 0))],
        out_specs=[pl.BlockSpec((tm, D), lambda i: (i, 0)),
                   pl.BlockSpec((tm, D), lambda i: (i, 0)),
                   pl.BlockSpec((8, D), lambda i: (0, 0))],
        out_shape=[SDS((t_len, D), F32), SDS((t_len, D), BF16), SDS((8, D), F32)],
        compiler_params=_cparams(("arbitrary",)), name=name)(dy, w, x, nw, dres)


def loss_head(x, fw, tgt):
    t_len = x.shape[0]
    tm = min(512, t_len)

    def body(x_ref, fw_ref, t_ref, loss_ref, dx_ref, dxb_ref, dfw_ref):
        @pl.when(pl.program_id(0) == 0)
        def _():
            loss_ref[...] = jnp.zeros_like(loss_ref)
            dfw_ref[...] = jnp.zeros_like(dfw_ref)
        xv = x_ref[...]
        r = lax.rsqrt(jnp.mean(xv * xv, axis=-1, keepdims=True) + EPS)
        xh = xv * r
        w = fw_ref[...]
        e = xh * w - t_ref[...]
        row = jnp.sum(e * e, axis=-1, keepdims=True) * (1.0 / D)
        loss_ref[...] += 0.5 * jnp.sum(row, axis=0, keepdims=True)
        dyf = e * (1.0 / D)
        dfw_ref[0:1, :] += jnp.sum(dyf * xh, axis=0, keepdims=True)
        g = dyf * w
        dx = r * (g - xh * jnp.mean(g * xh, axis=-1, keepdims=True))
        dx_ref[...] = dx
        dxb_ref[...] = dx.astype(BF16)

    return pl.pallas_call(
        body, grid=(t_len // tm,),
        in_specs=[pl.BlockSpec((tm, D), lambda i: (i, 0)),
                  pl.BlockSpec((1, D), lambda i: (0, 0)),
                  pl.BlockSpec((tm, D), lambda i: (i, 0))],
        out_specs=[pl.BlockSpec((8, LANES), lambda i: (0, 0)),
                   pl.BlockSpec((tm, D), lambda i: (i, 0)),
                   pl.BlockSpec((tm, D), lambda i: (i, 0)),
                   pl.BlockSpec((8, D), lambda i: (0, 0))],
        out_shape=[SDS((8, LANES), F32), SDS((t_len, D), F32), SDS((t_len, D), BF16), SDS((8, D), F32)],
        compiler_params=_cparams(("arbitrary",)), name="loss_head")(x, fw, tgt)


TAP_SHIFTS = (3, 2, 1)


def _shift_matrix(n, up):
    r = lax.broadcasted_iota(jnp.int32, (n, n), 0)
    c = lax.broadcasted_iota(jnp.int32, (n, n), 1)
    return jnp.concatenate([jnp.where(c == (r + j if up else r - j), 1.0, 0.0).astype(BF16) for j in TAP_SHIFTS],
                           axis=0)


def _shifts_dn(xb, halo, sm, n_shifts):
    n = xb.shape[0]
    first = len(TAP_SHIFTS) - n_shifts
    moved = _nn(sm[first * n:], xb)
    row = lax.broadcasted_iota(jnp.int32, halo.shape, 0)
    outs = []
    for k in range(n_shifts):
        j = TAP_SHIFTS[first + k]
        o = moved[k * n:(k + 1) * n]
        top = jnp.where(row < j, pltpu.roll(halo, j, 0), o[0:8])
        outs.append(jnp.concatenate([top, o[8:]], axis=0))
    return outs


def _shifts_up(xb, nxt, sm, n_shifts):
    n = xb.shape[0]
    first = len(TAP_SHIFTS) - n_shifts
    moved = _nn(sm[first * n:], xb)
    row = lax.broadcasted_iota(jnp.int32, nxt.shape, 0)
    outs = []
    for k in range(n_shifts):
        j = TAP_SHIFTS[first + k]
        o = moved[k * n:(k + 1) * n]
        bot = jnp.where(row >= 8 - j, pltpu.roll(nxt, 8 - j, 0), o[n - 8:n])
        outs.append(jnp.concatenate([o[:n - 8], bot], axis=0))
    return outs


def _conv_fwd(x, xb, halo, w_ref, kw, sm):
    shifted = _shifts_dn(xb, halo, sm, kw - 1)
    acc = w_ref[kw - 1:kw, :] * x
    for k in range(kw - 1):
        acc = acc + w_ref[k:k + 1, :] * shifted[k]
    return acc


def _chunk_cumsum(a, pos):
    for sh in (1, 2, 4, 8, 16, 32):
        a = a + jnp.where(pos >= sh, pltpu.roll(a, sh, 0), 0.0)
    return a


def _chunk_rcumsum(a, pos):
    n = a.shape[0]
    for sh in (1, 2, 4, 8, 16, 32):
        a = a + jnp.where(pos < CHUNK - sh, pltpu.roll(a, n - sh, 0), 0.0)
    return a


def _softplus(v):
    return jnp.maximum(v, 0.0) + jnp.log(1.0 + jnp.exp(-jnp.abs(v)))


def _silu(v):
    return v * _sigmoid(v)


def _dsilu(v):
    s = _sigmoid(v)
    return s * (1.0 + v * (1.0 - s))


def _lane_masks(width=D):
    lane = lax.broadcasted_iota(jnp.int32, (CHUNK, width), 1) & (HDIM - 1)
    row = lax.broadcasted_iota(jnp.int32, (CHUNK, width), 0)
    return lane == row, lane <= row


def _rep_matrix():
    lane = lax.broadcasted_iota(jnp.int32, (CHUNK, 512), 1) & (HDIM - 1)
    row = lax.broadcasted_iota(jnp.int32, (CHUNK, 512), 0)
    return jnp.where(lane == row, 1.0, 0.0).astype(BF16)


def _blockdiag(xp):
    lane = lax.broadcasted_iota(jnp.int32, xp.shape, 1)
    zero = jnp.zeros_like(xp)
    return jnp.concatenate([jnp.where(lane < HDIM, xp, zero), jnp.where(lane >= HDIM, xp, zero)], axis=0)


def _mixer_views(tt):
    r8 = tt // 8

    def main(width, col):
        return pl.BlockSpec((tt, width), lambda i, c=col // width: (i, c))

    def halo(width, col):
        return pl.BlockSpec((8, width), lambda i, c=col // width: (jnp.maximum(i * r8 - 1, 0), c))

    return main, halo


def mixer_fwd(proj, prm, tt):
    t_len = proj.shape[0]
    nblk = t_len // tt
    nc = tt // CHUNK
    main, halo = _mixer_views(tt)

    def body(ub_ref, uc_ref, uh_ref, z_ref, xr_ref, bcr_ref, dtr_ref, uch_ref, uhh_ref, xrh_ref, bcrh_ref,
             scw_ref, cwx_ref, cwbc_ref, cbx_ref, cbbc_ref, dtb_ref, alog_ref, dsk_ref, nrm_ref, eh_ref,
             y_ref, st_ref, aux_ref, hs, xs_s, bc_s, dtx_s, cumx_s, yssd_s):
        i = pl.program_id(0)
        first = i == 0

        @pl.when(first)
        def _():
            hs[...] = jnp.zeros_like(hs)

        keep = jnp.where(first, 0.0, 1.0)
        sm = _shift_matrix(tt, False)
        v = uc_ref[...].astype(F32) * uh_ref[...].astype(F32)
        vh = uch_ref[...].astype(F32) * uhh_ref[...].astype(F32) * keep
        cv = _conv_fwd(v, v.astype(BF16), vh, scw_ref, 3, sm)
        aux_ref[:, A_CV:A_CV + D] = cv.astype(BF16)
        y_ref[:, 0:D] = (ub_ref[...].astype(F32) * cv).astype(BF16)

        xrb = xr_ref[...]
        pre_x = _conv_fwd(xrb.astype(F32), xrb, xrh_ref[...].astype(F32) * keep, cwx_ref, 4, sm) + cbx_ref[...]
        aux_ref[:, A_PX:A_PX + D] = pre_x.astype(BF16)
        xs_s[...] = _silu(pre_x)
        bcrb = bcr_ref[...]
        pre_bc = _conv_fwd(bcrb.astype(F32), bcrb, bcrh_ref[...].astype(F32) * keep, cwbc_ref, 4, sm) + cbbc_ref[...]
        aux_ref[:, A_PBC:A_PBC + 512] = pre_bc.astype(BF16)
        bc_s[...] = _silu(pre_bc)
        dt = _softplus(dtr_ref[...].astype(F32) + dtb_ref[...])
        a_neg = -jnp.exp(alog_ref[...])
        pos = lax.broadcasted_iota(jnp.int32, (tt, LANES), 0) & (CHUNK - 1)
        cum = _chunk_cumsum(dt * a_neg, pos)
        eh = eh_ref[...]
        dtx_s[...] = _expand(dt, eh)
        cumx_s[...] = _expand(cum, eh)
        irep, causal = _lane_masks()
        rep = _rep_matrix()

        def chunk(c, carry):
            r0 = pl.multiple_of(c * CHUNK, CHUNK)
            rows = pl.ds(r0, CHUNK)
            cumx = cumx_s[rows, :]
            cum_l = cumx[CHUNK - 1:CHUNK, :]
            xd = xs_s[rows, :] * dtx_s[rows, :]
            xf = xd * jnp.exp(cum_l - cumx)
            ex = jnp.exp(cumx)
            e_l = jnp.exp(cum_l)
            rvec = jnp.sum(jnp.where(irep, cumx, 0.0), axis=0, keepdims=True)
            lam = jnp.where(causal, jnp.exp(jnp.where(causal, cumx - rvec, 0.0)), 0.0)
            bc = bc_s[rows, :]
            for g in range(2):
                gs = slice(g * 512, (g + 1) * 512)
                bg = bc[:, g * NSTATE:(g + 1) * NSTATE].astype(BF16)
                cg = bc[:, 256 + g * NSTATE:256 + (g + 1) * NSTATE].astype(BF16)
                s_rep = _nn(_nt(cg, bg).astype(BF16), rep)
                m_g = (s_rep * lam[:, gs]).astype(BF16)
                h_g = hs[:, gs]
                h_b = h_g.astype(BF16)
                st_ref[c, :, gs] = h_b
                yo = _nn(cg, h_b) * ex[:, gs]
                xd_b = xd[:, gs].astype(BF16)
                for hp in range(4):
                    ps = slice(hp * LANES, (hp + 1) * LANES)
                    yd = _nn(m_g[:, ps], _blockdiag(xd_b[:, ps]))
                    yssd_s[rows, g * 512 + hp * LANES:g * 512 + (hp + 1) * LANES] = yd + yo[:, ps]
                hs[:, gs] = h_g * e_l[:, gs] + _tn(bg, xf[:, gs].astype(BF16))
            return carry

        lax.fori_loop(0, nc, chunk, 0, unroll=True)

        ys = yssd_s[...] + dsk_ref[...] * xs_s[...]
        aux_ref[:, A_YS:A_YS + D] = ys.astype(BF16)
        gt = ys * _silu(z_ref[...].astype(F32))
        for g in range(2):
            gs = slice(g * 512, (g + 1) * 512)
            gg = gt[:, gs]
            rn = lax.rsqrt(jnp.mean(gg * gg, axis=-1, keepdims=True) + EPS)
            y_ref[:, D + g * 512:D + (g + 1) * 512] = (gg * rn * nrm_ref[:, gs]).astype(BF16)

    def const(shape):
        return pl.BlockSpec(shape, lambda i: (0, 0))

    in_specs = [main(D, C_UB), main(D, C_UC), main(D, C_UH), main(D, C_Z), main(D, C_XS), main(512, C_BC),
                main(LANES, C_DT), halo(D, C_UC), halo(D, C_UH), halo(D, C_XS), halo(512, C_BC),
                const((8, D)), const((8, D)), const((8, 512)), const((1, D)), const((1, 512)),
                const((1, LANES)), const((1, LANES)), const((1, D)), const((1, D)), const((LANES, D))]
    return pl.pallas_call(
        body, grid=(nblk,),
        in_specs=in_specs,
        out_specs=[pl.BlockSpec((tt, MIX), lambda i: (i, 0)),
                   pl.BlockSpec((nc, NSTATE, D), lambda i: (i, 0, 0)),
                   pl.BlockSpec((tt, AUX_W), lambda i: (i, 0))],
        out_shape=[SDS((t_len, MIX), BF16), SDS((t_len // CHUNK, NSTATE, D), BF16), SDS((t_len, AUX_W), BF16)],
        scratch_shapes=[pltpu.VMEM((NSTATE, D), F32), pltpu.VMEM((tt, D), F32), pltpu.VMEM((tt, 512), F32),
                        pltpu.VMEM((tt, D), F32), pltpu.VMEM((tt, D), F32), pltpu.VMEM((tt, D), F32)],
        compiler_params=_cparams(("arbitrary",)), name="mixer_fwd")(
            *([proj] * 11), prm["scw"], prm["cwx"], prm["cwbc"], prm["cbx"], prm["cbbc"], prm["dtb"],
            prm["alog"], prm["dskx"], prm["nrm"], prm["eh"])


def mixer_bwd(proj, dy, states, aux, prm, tt):
    t_len = proj.shape[0]
    nblk = t_len // tt
    nc = tt // CHUNK

    def rev(i):
        return nblk - 1 - i

    def main(width, col):
        return pl.BlockSpec((tt, width), lambda i, c=col // width: (rev(i), c))

    def body(ub_ref, uc_ref, uh_ref, z_ref, xr_ref, bcr_ref, dtr_ref, dy_ref, st_ref, aux_ref,
             scw_ref, cwx_ref, cwbc_ref, cbx_ref, cbbc_ref, dtb_ref, alog_ref, dsk_ref, nrm_ref, eh_ref, eht_ref,
             dp_ref, gscw_ref, gcwx_ref, gcwbc_ref, gvec_ref, gdt_ref,
             dhs, xs_s, bc_s, dtx_s, cumx_s, dys_s, dxs_s, dbc_s, red_s, ddtx_s, nx_cv, nx_px, nx_pbc, sgx_s, sgbc_s):
        i = pl.program_id(0)

        @pl.when(i == 0)
        def _():
            dhs[...] = jnp.zeros_like(dhs)
            nx_cv[...] = jnp.zeros_like(nx_cv)
            nx_px[...] = jnp.zeros_like(nx_px)
            nx_pbc[...] = jnp.zeros_like(nx_pbc)
            gscw_ref[...] = jnp.zeros_like(gscw_ref)
            gcwx_ref[...] = jnp.zeros_like(gcwx_ref)
            gcwbc_ref[...] = jnp.zeros_like(gcwbc_ref)
            gvec_ref[...] = jnp.zeros_like(gvec_ref)
            gdt_ref[...] = jnp.zeros_like(gdt_ref)

        uc = uc_ref[...].astype(F32)
        uh = uh_ref[...].astype(F32)
        v = uc * uh
        dya = dy_ref[:, 0:D].astype(F32)
        dp_ref[:, C_UB:C_UB + D] = (dya * aux_ref[:, A_CV:A_CV + D].astype(F32)).astype(BF16)
        dcv = dya * ub_ref[...].astype(F32)
        sm = _shift_matrix(tt, True)
        ups = _shifts_up(dcv.astype(BF16), nx_cv[...], sm, 2) + [dcv]
        dv = None
        for k in range(3):
            gscw_ref[k:k + 1, :] += jnp.sum(v * ups[k], axis=0, keepdims=True)
            term = scw_ref[k:k + 1, :] * ups[k]
            dv = term if dv is None else dv + term
        nx_cv[...] = dcv[0:8]
        dp_ref[:, C_UC:C_UC + D] = (dv * uh).astype(BF16)
        dp_ref[:, C_UH:C_UH + D] = (dv * uc).astype(BF16)

        pre_x = aux_ref[:, A_PX:A_PX + D].astype(F32)
        pre_bc = aux_ref[:, A_PBC:A_PBC + 512].astype(F32)
        sg_x = _sigmoid(pre_x)
        sg_bc = _sigmoid(pre_bc)
        sgx_s[...] = sg_x
        sgbc_s[...] = sg_bc
        xs = pre_x * sg_x
        xs_s[...] = xs
        bc_s[...] = pre_bc * sg_bc
        dt_pre = dtr_ref[...].astype(F32) + dtb_ref[...]
        dt = _softplus(dt_pre)
        a_neg = -jnp.exp(alog_ref[...])
        pos = lax.broadcasted_iota(jnp.int32, (tt, LANES), 0) & (CHUNK - 1)
        cum = _chunk_cumsum(dt * a_neg, pos)
        eh = eh_ref[...]
        eht = eht_ref[...]
        dtx_s[...] = _expand(dt, eh)
        cumx_s[...] = _expand(cum, eh)

        irep, causal = _lane_masks()
        irep_g, _ = _lane_masks(512)
        rep = _rep_matrix()
        row64 = lax.broadcasted_iota(jnp.int32, (CHUNK, 512), 0)
        lane128 = lax.broadcasted_iota(jnp.int32, (CHUNK, LANES), 1)

        z = z_ref[...].astype(F32)
        sg_z = _sigmoid(z)
        sz = z * sg_z
        dsz = sg_z * (1.0 + z * (1.0 - sg_z))
        ys = aux_ref[:, A_YS:A_YS + D].astype(F32)
        gt = ys * sz
        dyb = dy_ref[:, D:MIX].astype(F32)
        for g in range(2):
            gs = slice(g * 512, (g + 1) * 512)
            gg = gt[:, gs]
            rn = lax.rsqrt(jnp.mean(gg * gg, axis=-1, keepdims=True) + EPS)
            gvec_ref[0:1, gs] += jnp.sum(dyb[:, gs] * gg * rn, axis=0, keepdims=True)
            dgn = dyb[:, gs] * nrm_ref[:, gs]
            dgt = rn * (dgn - gg * (rn * rn) * jnp.mean(dgn * gg, axis=-1, keepdims=True))
            dys = dgt * sz[:, gs]
            dys_s[:, gs] = dys
            dp_ref[:, C_Z + g * 512:C_Z + (g + 1) * 512] = (dgt * ys[:, gs] * dsz[:, gs]).astype(BF16)
        dys_all = dys_s[...]
        gvec_ref[1:2, :] += jnp.sum(dys_all * xs, axis=0, keepdims=True)

        def bwd_chunk(cc, carry):
            c = nc - 1 - cc
            r0 = pl.multiple_of(c * CHUNK, CHUNK)
            rows = pl.ds(r0, CHUNK)
            cumx = cumx_s[rows, :]
            cum_l = cumx[CHUNK - 1:CHUNK, :]
            xs_c = xs_s[rows, :]
            dtx = dtx_s[rows, :]
            xd = xs_c * dtx
            f = jnp.exp(cum_l - cumx)
            xf = xd * f
            ex = jnp.exp(cumx)
            e_l = jnp.exp(cum_l)
            rvec = jnp.sum(jnp.where(irep, cumx, 0.0), axis=0, keepdims=True)
            lam = jnp.where(causal, jnp.exp(jnp.where(causal, cumx - rvec, 0.0)), 0.0)
            bc = bc_s[rows, :]
            dyc = dys_s[rows, :]
            for g in range(2):
                gs = slice(g * 512, (g + 1) * 512)
                bg = bc[:, g * NSTATE:(g + 1) * NSTATE].astype(BF16)
                cg = bc[:, 256 + g * NSTATE:256 + (g + 1) * NSTATE].astype(BF16)
                h0 = st_ref[c, :, gs]
                dh = dhs[:, gs]
                dh_b = dh.astype(BF16)
                xf_g = xf[:, gs]
                dxf = _nn(bg, dh_b)
                db = _nt(xf_g.astype(BF16), dh_b)
                s_rep = _nn(_nt(cg, bg).astype(BF16), rep)
                lam_g = lam[:, gs]
                m_g = s_rep * lam_g
                m_b = m_g.astype(BF16)
                ex_g = ex[:, gs]
                dy_g = dyc[:, gs]
                yo = _nn(cg, h0) * ex_g
                dg_b = (dy_g * ex_g).astype(BF16)
                dc = _nt(dg_b, h0)
                el_g = e_l[:, gs]
                dee = jnp.sum(dh * h0.astype(F32), axis=0, keepdims=True) * el_g
                dhs[:, gs] = dh * el_g + _tn(cg, dg_b)
                xd_b = xd[:, gs].astype(BF16)
                dy_b = dy_g.astype(BF16)
                dm_parts, dxd_parts = [], []
                for hp in range(4):
                    ps = slice(hp * LANES, (hp + 1) * LANES)
                    bd = _blockdiag(xd_b[:, ps])
                    dm_parts.append(_nt(dy_b[:, ps], bd))
                    t2 = _tn(m_b[:, ps], dy_b[:, ps])
                    dxd_parts.append(jnp.where(lane128 < HDIM, t2[0:CHUNK], t2[CHUNK:2 * CHUNK]))
                dm = jnp.concatenate(dm_parts, axis=1)
                dxd = jnp.concatenate(dxd_parts, axis=1) + dxf * f[:, gs]
                dseg = dm * m_g
                ds_b = _nt((dm * lam_g).astype(BF16), rep).astype(BF16)
                dc = dc + _nn(ds_b, bg)
                db = db + _tn(ds_b, cg)
                colsum = jnp.sum(dseg, axis=0, keepdims=True)
                dxfxf = dxf * xf_g
                red = dseg - jnp.where(irep_g, colsum, 0.0) + dy_g * yo - dxfxf
                last = jnp.sum(dxfxf, axis=0, keepdims=True) + dee
                red = red + jnp.where(row64 == CHUNK - 1, last, 0.0)
                red_s[rows, gs] = red
                ddtx_s[rows, gs] = dxd * xs_c[:, gs]
                dxs_s[rows, gs] = dxd * dtx[:, gs] + dsk_ref[:, gs] * dy_g
                dbc_s[rows, g * NSTATE:(g + 1) * NSTATE] = db
                dbc_s[rows, 256 + g * NSTATE:256 + (g + 1) * NSTATE] = dc
            return carry

        lax.fori_loop(0, nc, bwd_chunk, 0, unroll=True)

        dcum = _head_reduce(red_s[...], eht)
        da = _chunk_rcumsum(dcum, pos)
        ddt = _head_reduce(ddtx_s[...], eht) + da * a_neg
        gdt_ref[1:2, :] += jnp.sum(da * dt, axis=0, keepdims=True) * a_neg
        ddt_raw = ddt * _sigmoid(dt_pre)
        lane_t = lax.broadcasted_iota(jnp.int32, (tt, LANES), 1)
        ddt_raw = jnp.where(lane_t < NHEAD, ddt_raw, 0.0)
        gdt_ref[0:1, :] += jnp.sum(ddt_raw, axis=0, keepdims=True)
        dp_ref[:, C_DT:C_DT + LANES] = ddt_raw.astype(BF16)

        sg_x = sgx_s[...]
        sg_bc = sgbc_s[...]
        pre_x = aux_ref[:, A_PX:A_PX + D].astype(F32)
        pre_bc = aux_ref[:, A_PBC:A_PBC + 512].astype(F32)
        dpx = dxs_s[...] * (sg_x * (1.0 + pre_x * (1.0 - sg_x)))
        dpbc = dbc_s[...] * (sg_bc * (1.0 + pre_bc * (1.0 - sg_bc)))
        gvec_ref[2:3, :] += jnp.sum(dpx, axis=0, keepdims=True)
        gcwbc_ref[4:5, :] += jnp.sum(dpbc, axis=0, keepdims=True)
        xraw = xr_ref[...].astype(F32)
        bcraw = bcr_ref[...].astype(F32)
        ups_x = _shifts_up(dpx.astype(BF16), nx_px[...], sm, 3) + [dpx]
        ups_bc = _shifts_up(dpbc.astype(BF16), nx_pbc[...], sm, 3) + [dpbc]
        dxr, dbcr = None, None
        for k in range(4):
            up_x = ups_x[k]
            up_bc = ups_bc[k]
            gcwx_ref[k:k + 1, :] += jnp.sum(xraw * up_x, axis=0, keepdims=True)
            gcwbc_ref[k:k + 1, :] += jnp.sum(bcraw * up_bc, axis=0, keepdims=True)
            tx = cwx_ref[k:k + 1, :] * up_x
            tb = cwbc_ref[k:k + 1, :] * up_bc
            dxr = tx if dxr is None else dxr + tx
            dbcr = tb if dbcr is None else dbcr + tb
        nx_px[...] = dpx[0:8]
        nx_pbc[...] = dpbc[0:8]
        dp_ref[:, C_XS:C_XS + D] = dxr.astype(BF16)
        dp_ref[:, C_BC:C_BC + 512] = dbcr.astype(BF16)

        @pl.when(i == nblk - 1)
        def _():
            gdt_ref[2:3, :] = _head_reduce(gvec_ref[1:2, :] * jnp.ones((8, 1), F32), eht)[0:1, :]

    def const(shape):
        return pl.BlockSpec(shape, lambda i: (0, 0))

    in_specs = [main(D, C_UB), main(D, C_UC), main(D, C_UH), main(D, C_Z), main(D, C_XS), main(512, C_BC),
                main(LANES, C_DT),
                pl.BlockSpec((tt, MIX), lambda i: (rev(i), 0)),
                pl.BlockSpec((nc, NSTATE, D), lambda i: (rev(i), 0, 0)),
                pl.BlockSpec((tt, AUX_W), lambda i: (rev(i), 0)),
                const((8, D)), const((8, D)), const((8, 512)), const((1, D)), const((1, 512)),
                const((1, LANES)), const((1, LANES)), const((1, D)), const((1, D)), const((LANES, D)),
                const((D, LANES))]
    return pl.pallas_call(
        body, grid=(nblk,),
        in_specs=in_specs,
        out_specs=[pl.BlockSpec((tt, NINP), lambda i: (rev(i), 0)),
                   const((8, D)), const((8, D)), const((8, 512)), const((8, D)), const((8, LANES))],
        out_shape=[SDS((t_len, NINP), BF16), SDS((8, D), F32), SDS((8, D), F32), SDS((8, 512), F32),
                   SDS((8, D), F32), SDS((8, LANES), F32)],
        scratch_shapes=[pltpu.VMEM((NSTATE, D), F32),
                        pltpu.VMEM((tt, D), F32), pltpu.VMEM((tt, 512), F32),
                        pltpu.VMEM((tt, D), F32), pltpu.VMEM((tt, D), F32),
                        pltpu.VMEM((tt, D), F32), pltpu.VMEM((tt, D), F32),
                        pltpu.VMEM((tt, 512), F32),
                        pltpu.VMEM((tt, D), F32), pltpu.VMEM((tt, D), F32),
                        pltpu.VMEM((8, D), F32), pltpu.VMEM((8, D), F32), pltpu.VMEM((8, 512), F32),
                        pltpu.VMEM((tt, D), F32), pltpu.VMEM((tt, 512), F32)],
        compiler_params=_cparams(("arbitrary",)), name="mixer_bwd")(
            *([proj] * 7), dy, states, aux, prm["scw"], prm["cwx"], prm["cwbc"], prm["cbx"], prm["cbbc"], prm["dtb"],
            prm["alog"], prm["dskx"], prm["nrm"], prm["eh"], prm["eht"])


TN_IN = 1920


def layer_fwd_mix(x, lw, prm, tt):
    proj, h1 = norm_matmul(x, lw["nw1"], lw["win"], "in_proj")
    y, st, aux = mixer_fwd(proj, prm, tt)
    return h1, proj, (st, aux), y


def layer_fwd_mlp(x, mixed, lw):
    h1, proj, st, y = mixed
    x1 = matmul_residual(y, lw["wout"], x, False, "out_proj")
    u, h2 = norm_matmul(x1, lw["nw2"], lw["wup"], "up_proj")
    x2 = matmul_residual(u, lw["wdn"], x1, True, "down_proj")
    return x2, (x, h1, proj, st, y, x1, h2, u)


def layer_fwd(x, lw, prm, tt):
    return layer_fwd_mlp(x, layer_fwd_mix(x, lw, prm, tt), lw)


def _dw(a, b, a_cols, b_cols, relu2, name):
    m_len, n_len = a.shape[1], b.shape[1]
    n_a, n_b = m_len // a_cols, n_len // b_cols
    assert n_a == 1 or n_b == 1
    if n_b == 1:
        return matmul_tn(
            a, b,
            lambda t_: pl.BlockSpec((t_, a_cols), lambda n, t: (t, n)),
            lambda t_: pl.BlockSpec((t_, n_len), lambda n, t: (t, 0)),
            pl.BlockSpec((a_cols, n_len), lambda n, t: (n, 0)), SDS((m_len, n_len), BF16), n_a, relu2, name)
    return matmul_tn(
        a, b,
        lambda t_: pl.BlockSpec((t_, m_len), lambda n, t: (t, 0)),
        lambda t_: pl.BlockSpec((t_, b_cols), lambda n, t: (t, n)),
        pl.BlockSpec((m_len, b_cols), lambda n, t: (0, n)), SDS((m_len, n_len), BF16), n_b, relu2, name)


def layer_bwd_mlp(dx2, dx2b, lw, saved):
    _, _, _, _, y, x1, h2, u = saved
    du = matmul_nt_act(dx2b, lw["wdn"], u, "mlp_bwd_du")
    g_wdn = _dw(u, dx2b, 1024, D, True, "dw_down")
    dx1, dx1b, g_nw2 = matmul_nt_norm_bwd(du, lw["wup"], x1, lw["nw2"], dx2, "mlp_bwd_dx")
    cb = DFF // N_DEV
    g_wup = matmul_tn(
        h2, du,
        lambda t_: pl.BlockSpec((t_, D), lambda n, t: (t, 0)),
        lambda t_: pl.BlockSpec((t_, 2 * cb), lambda n, t: (t, n)),
        pl.BlockSpec((2, D, cb), lambda n, t: (n, 0, 0)), SDS((N_DEV, D, cb), BF16), N_DEV // 2, False, "dw_up")
    dy = matmul_nt_act(dx1b, lw["wout"], None, "out_bwd_dy")
    g_wout = _dw(y, dx1b, 1024, D, False, "dw_out")
    return dx1, dx1b, dy, {"wout": g_wout, "wup": g_wup, "wdn": g_wdn, "nw2": g_nw2[0]}


def layer_bwd_mix(dx1, dy, lw, prm, saved, tt):
    x, h1, proj, st = saved[:4]
    dproj, gscw, gcwx, gcwbc, gvec, gdt = mixer_bwd(proj, dy, st[0], st[1], prm, tt)
    dx0, dx0b, g_nw1 = matmul_nt_norm_bwd(dproj, lw["win"], x, lw["nw1"], dx1, "in_bwd_dx")
    g_win = _dw(h1, dproj, D, TN_IN, False, "dw_in")
    grads = {
        "win": g_win, "scw": gscw[0:3], "cw": jnp.concatenate([gcwx[0:4], gcwbc[0:4]], axis=1),
        "cb": jnp.concatenate([gvec[2], gcwbc[4]], axis=0),
        "dtb": gdt[0, :NHEAD], "alog": gdt[1, :NHEAD], "dsk": gdt[2, :NHEAD],
        "nrm": gvec[0], "nw1": g_nw1[0],
    }
    return dx0, dx0b, grads


def layer_bwd(dx2, dx2b, lw, prm, saved, tt):
    dx1, dx1b, dy, g_mlp = layer_bwd_mlp(dx2, dx2b, lw, saved)
    dx0, dx0b, g_mix = layer_bwd_mix(dx1, dy, lw, prm, saved, tt)
    return dx0, dx0b, {**g_mlp, **g_mix}


def layer_params(win, scw, cw, nw1, nw2, conv_b, dt_bias, a_log, d_skip, ssd_norm_w, eh, eht):
    def rows8(a):
        return jnp.pad(a, ((0, 8 - a.shape[0]), (0, 0)))

    def lanes128(a):
        return jnp.pad(a, (0, LANES - a.shape[0]))[None, :]

    lw = {"win": win, "nw1": nw1[None, :], "nw2": nw2[None, :]}
    prm = {"scw": rows8(scw), "cwx": rows8(cw[:, :D]), "cwbc": rows8(cw[:, D:]),
           "cbx": conv_b[None, :D], "cbbc": conv_b[None, D:],
           "dtb": lanes128(dt_bias), "alog": lanes128(a_log),
           "dskx": jnp.repeat(d_skip, HDIM)[None, :], "nrm": ssd_norm_w[None, :], "eh": eh, "eht": eht}
    return lw, prm


def _flip(v, bit):
    return 1 - v if bit else v


def all_gather(arrs, name):
    n = len(arrs)

    def body(*refs):
        ins, outs = refs[:n], refs[n:2 * n]
        send_sems, recv_sems, local_sems = refs[2 * n:]
        x, y, c = lax.axis_index("x"), lax.axis_index("y"), lax.axis_index("c")
        sibling = (x, y, 1 - c)
        chips = [(1 - x, y), (x, 1 - y), (1 - x, 1 - y)]

        def idx(px, py, pc):
            return 4 * px + 2 * py + pc

        def copy(a, k, block, to, src=None):
            dst = outs[a].at[idx(*block)]
            return pltpu.make_async_remote_copy(
                src_ref=dst if src is None else src, dst_ref=dst,
                send_sem=send_sems.at[a, k], recv_sem=recv_sems.at[a, k], device_id=to, device_id_type=MESH)

        me = (x, y, c)
        mine = [pltpu.make_async_copy(ins[a], outs[a].at[idx(*me)], local_sems.at[a]) for a in range(n)]
        for cp in mine:
            cp.start()
        first = []
        for a in range(n):
            first.append(copy(a, 0, me, sibling, src=ins[a]))
            first += [copy(a, 1 + j, me, (*chip, c), src=ins[a]) for j, chip in enumerate(chips)]
        for cp in first:
            cp.start()
        passed = []
        for j, chip in enumerate(chips):
            for a in range(n):
                copy(a, 1 + j, (*chip, c), me).wait_recv()
                cp = copy(a, 4 + j, (*chip, c), sibling)
                cp.start()
                passed.append(cp)
        for a in range(n):
            copy(a, 0, sibling, me).wait_recv()
            for j, chip in enumerate(chips):
                copy(a, 4 + j, (*chip, 1 - c), me).wait_recv()
        for cp in first + passed:
            cp.wait_send()
        for cp in mine:
            cp.wait()

    any_spec = pl.BlockSpec(memory_space=pl.ANY)
    return pl.pallas_call(
        body, in_specs=[any_spec] * n, out_specs=[any_spec] * n,
        out_shape=[SDS((N_DEV,) + a.shape, a.dtype) for a in arrs],
        scratch_shapes=[pltpu.SemaphoreType.DMA((n, 7)), pltpu.SemaphoreType.DMA((n, 7)),
                        pltpu.SemaphoreType.DMA((n,))],
        name=name)(*arrs)


HBM_SPEC = pl.BlockSpec(memory_space=pltpu.HBM)
SEM_SPEC = pl.BlockSpec(memory_space=pltpu.SEMAPHORE)
SIDE_EFFECT = pltpu.SideEffectType.DATAFLOW_SIDE_EFFECTING
N_PEER = N_DEV - 1


def _peer(mask):
    x, y, c = lax.axis_index("x"), lax.axis_index("y"), lax.axis_index("c")
    return _flip(x, mask & 4), _flip(y, mask & 2), _flip(c, mask & 1)


ALL_PEERS = tuple(range(1, N_DEV))
SIBLING_AND_CHIPS = (1, 2, 4, 6)


def exchange_start(srcs, per_peer, name, after=None, masks=ALL_PEERS):
    n = len(srcs)
    npeer = len(masks)
    lands = [SDS((N_DEV,) + (a.shape[1:] if per_peer else a.shape), a.dtype) for a in srcs]
    n_in = 2 * n + (after is not None)

    def body(*refs):
        src_refs, land_refs = refs[:n], refs[n:2 * n]
        send_sems, recv_sems = refs[n_in], refs[n_in + 1]
        token = refs[-1]
        x, y, c = lax.axis_index("x"), lax.axis_index("y"), lax.axis_index("c")
        me = 4 * x + 2 * y + c
        for a in range(n):
            for k, mask in enumerate(masks):
                px, py, pc = _peer(mask)
                part = src_refs[a].at[4 * px + 2 * py + pc] if per_peer else src_refs[a]
                pltpu.make_async_remote_copy(
                    src_ref=part, dst_ref=land_refs[a].at[me], send_sem=send_sems.at[a * npeer + k],
                    recv_sem=recv_sems.at[a * npeer + k], device_id=(px, py, pc), device_id_type=MESH).start()
        token[...] = jnp.zeros_like(token)

    out = pl.pallas_call(
        body, name=name,
        out_shape=(pltpu.SemaphoreType.DMA((n * npeer,)), pltpu.SemaphoreType.DMA((n * npeer,)),
                   *[pltpu.HBM(a.shape, a.dtype) for a in srcs], *[pltpu.HBM(l.shape, l.dtype) for l in lands],
                   SDS((8, LANES), F32)),
        in_specs=(HBM_SPEC,) * (2 * n) + ((pl.BlockSpec(memory_space=pl.ANY),) if after is not None else ()),
        out_specs=(SEM_SPEC, SEM_SPEC) + (HBM_SPEC,) * (2 * n) + (pl.BlockSpec(memory_space=pltpu.VMEM),),
        input_output_aliases={k: 2 + k for k in range(2 * n)},
        compiler_params=pltpu.CompilerParams(has_side_effects=SIDE_EFFECT),
    )(*[pltpu.with_memory_space_constraint(a, pltpu.HBM) for a in srcs],
      *[pltpu.with_memory_space_constraint(lax.empty(l.shape, l.dtype), pltpu.HBM) for l in lands],
      *([after] if after is not None else []))
    return out[0], out[1], list(out[2:2 + n]), list(out[2 + n:2 + 2 * n]), out[-1]


def exchange_wait(started, after, per_peer, name, masks=ALL_PEERS):
    send_sems, recv_sems, srcs, lands, _ = started
    n = len(srcs)
    npeer = len(masks)

    def body(*refs):
        src_refs, land_refs = refs[:n], refs[n:2 * n]
        send_sems, recv_sems = refs[2 * n], refs[2 * n + 1]
        for k, mask in enumerate(masks):
            for a in range(n):
                copy = pltpu.make_async_remote_copy(
                    src_ref=src_refs[a].at[0] if per_peer else src_refs[a], dst_ref=land_refs[a].at[0],
                    send_sem=send_sems.at[a * npeer + k], recv_sem=recv_sems.at[a * npeer + k],
                    device_id=_peer(mask), device_id_type=MESH)
                copy.wait_send()
                copy.wait_recv()

    out = pl.pallas_call(
        body, name=name,
        out_shape=tuple(pltpu.HBM(a.shape, a.dtype) for a in srcs + lands),
        in_specs=(HBM_SPEC,) * (2 * n) + (SEM_SPEC, SEM_SPEC, pl.BlockSpec(memory_space=pl.ANY)),
        out_specs=(HBM_SPEC,) * (2 * n), input_output_aliases={k: k for k in range(2 * n)},
        compiler_params=pltpu.CompilerParams(has_side_effects=SIDE_EFFECT),
    )(*srcs, *lands, send_sems, recv_sems, after)
    return list(out[:n]), list(out[n:])


def relay_to_sibling(lands, name):
    n = len(lands)
    chips = (2, 4, 6)

    def body(*refs):
        land_refs = refs[n:2 * n]
        send_sems, recv_sems = refs[2 * n], refs[2 * n + 1]
        x, y, c = lax.axis_index("x"), lax.axis_index("y"), lax.axis_index("c")
        copies = []
        for a in range(n):
            for k, mask in enumerate(chips):
                px, py, _ = _peer(mask)
                block = land_refs[a].at[4 * px + 2 * py + c]
                cp = pltpu.make_async_remote_copy(
                    src_ref=block, dst_ref=block, send_sem=send_sems.at[a * 3 + k], recv_sem=recv_sems.at[a * 3 + k],
                    device_id=(x, y, 1 - c), device_id_type=MESH)
                cp.start()
                copies.append((cp, a, k, land_refs[a].at[4 * px + 2 * py + 1 - c]))
        for cp, a, k, arriving in copies:
            cp.wait_send()
            pltpu.make_async_remote_copy(
                src_ref=arriving, dst_ref=arriving, send_sem=send_sems.at[a * 3 + k], recv_sem=recv_sems.at[a * 3 + k],
                device_id=(x, y, 1 - c), device_id_type=MESH).wait_recv()

    any_spec = pl.BlockSpec(memory_space=pl.ANY)
    return list(pl.pallas_call(
        body, in_specs=[any_spec] * n, out_specs=[any_spec] * n,
        out_shape=[SDS(a.shape, a.dtype) for a in lands],
        input_output_aliases={k: k for k in range(n)},
        scratch_shapes=[pltpu.SemaphoreType.DMA((n * 3,)), pltpu.SemaphoreType.DMA((n * 3,))],
        name=name)(*lands))


IN_SHARD = NIN // N_DEV
SLOT_W = 768


def _slot_window(j):
    return (IN_SHARD * j // LANES) * LANES, -(-(IN_SHARD * (j + 1)) // LANES) * LANES


def _placement(j):
    a, b = _slot_window(j)
    r = lax.broadcasted_iota(jnp.int32, (SLOT_W, b - a), 0)
    c = lax.broadcasted_iota(jnp.int32, (SLOT_W, b - a), 1)
    return jnp.where(jnp.logical_and(c == r + (IN_SHARD * j - a), r < IN_SHARD), 1.0, 0.0).astype(BF16)


def assemble_w_in(land):
    tm = 256

    def body(l_ref, o_ref, acc):
        acc[...] = jnp.zeros_like(acc)
        for j in range(N_DEV):
            a, b = _slot_window(j)
            acc[:, a:b] += _nn(l_ref[j], _placement(j))
        o_ref[...] = acc[...].astype(BF16)

    return pl.pallas_call(
        body, grid=(D // tm,),
        in_specs=[pl.BlockSpec((N_DEV, tm, SLOT_W), lambda i: (0, i, 0))],
        out_specs=pl.BlockSpec((tm, NINP), lambda i: (i, 0)),
        out_shape=SDS((D, NINP), BF16),
        scratch_shapes=[pltpu.VMEM((tm, NINP), F32)],
        compiler_params=_cparams(("parallel",)), name="assemble_w_in")(land)


def scatter_w_in(dw):
    tm = 256

    def body(d_ref, o_ref):
        for j in range(N_DEV):
            a, b = _slot_window(j)
            o_ref[j] = _nt(d_ref[:, a:b], _placement(j)).astype(BF16)

    return pl.pallas_call(
        body, grid=(D // tm,),
        in_specs=[pl.BlockSpec((tm, NINP), lambda i: (i, 0))],
        out_specs=pl.BlockSpec((N_DEV, tm, SLOT_W), lambda i: (0, i, 0)),
        out_shape=SDS((N_DEV, D, SLOT_W), BF16),
        compiler_params=_cparams(("parallel",)), name="scatter_w_in")(dw)


def _adamw_math(g, w_ref, m_ref, v_ref, g_ref, d_ref, nm_ref, nv_ref):
    mn = ADAM_B1 * m_ref[...] + (1.0 - ADAM_B1) * g
    vn = ADAM_B2 * v_ref[...] + (1.0 - ADAM_B2) * jnp.square(g)
    m_hat = mn / (1.0 - ADAM_B1 ** ADAM_STEP)
    v_hat = vn / (1.0 - ADAM_B2 ** ADAM_STEP)
    g_ref[...] = g
    d_ref[...] = -ADAM_LR * (m_hat / (jnp.sqrt(v_hat) + ADAM_EPS) + ADAM_WD * w_ref[...])
    nm_ref[...] = mn
    nv_ref[...] = vn


def adamw_layers(w, slots, m, v, name):
    depth, r_len, c_len = w.shape
    cs = slots[0].shape[2]
    br = min(128, r_len)
    assert r_len % br == 0

    def body(w_ref, *rest):
        s_refs, (m_ref, v_ref, g_ref, d_ref, nm_ref, nv_ref) = rest[:depth], rest[depth:]
        layer = pl.program_id(0)
        for k in range(depth):
            @pl.when(layer == k)
            def _(k=k):
                g = s_refs[k][0, :, 0:c_len].astype(F32)
                for j in range(1, N_DEV):
                    g = g + s_refs[k][j, :, 0:c_len].astype(F32)
                _adamw_math(g, w_ref, m_ref, v_ref, g_ref, d_ref, nm_ref, nv_ref)

    spec = pl.BlockSpec((None, br, c_len), lambda l, i: (l, i, 0))
    s_specs = [pl.BlockSpec((N_DEV, br, cs), lambda l, i, k=k: (0, jnp.where(l == k, i, 0), 0))
               for k in range(depth)]
    return pl.pallas_call(
        body, grid=(depth, r_len // br),
        in_specs=[spec] + s_specs + [spec, spec],
        out_specs=[spec] * 4, out_shape=[SDS(w.shape, F32)] * 4,
        compiler_params=_cparams(("arbitrary", "arbitrary")), name=name)(w, *slots, m, v)


def adamw(w, slots, m, v, name):
    r_len, c_len = w.shape
    br = r_len if r_len <= 512 else 512
    assert r_len % br == 0

    def body(w_ref, s_ref, m_ref, v_ref, g_ref, d_ref, nm_ref, nv_ref):
        g = s_ref[0].astype(F32)
        for k in range(1, N_DEV):
            g = g + s_ref[k].astype(F32)
        _adamw_math(g, w_ref, m_ref, v_ref, g_ref, d_ref, nm_ref, nv_ref)

    spec = pl.BlockSpec((br, c_len), lambda i: (i, 0))
    return pl.pallas_call(
        body, grid=(r_len // br,),
        in_specs=[spec, pl.BlockSpec((N_DEV, br, c_len), lambda i: (0, i, 0)), spec, spec],
        out_specs=[spec] * 4, out_shape=[SDS((r_len, c_len), F32)] * 4,
        compiler_params=_cparams(("parallel",)), name=name)(w, slots, m, v)


def _adamw_nd(w, slots, m, v, name):
    shp = w.shape
    r = int(np.prod(shp[:-1]))
    outs = adamw(w.reshape(r, shp[-1]), slots.reshape(N_DEV, r, shp[-1]), m.reshape(r, shp[-1]),
                 v.reshape(r, shp[-1]), name)
    return [o.reshape(shp) for o in outs]


SMALL = [("norm_mix_w", DEPTH * D), ("ssd_conv_b", DEPTH * XBC), ("dt_bias", DEPTH * NHEAD),
         ("a_log", DEPTH * NHEAD), ("d_skip", DEPTH * NHEAD), ("ssd_norm_w", DEPTH * D),
         ("norm_mlp_w", DEPTH * D), ("final_norm_w", D)]
SMALL_LEN = sum(s for _, s in SMALL)
SMALL_ROWS = -(-SMALL_LEN // LANES)


def _pack_small(parts):
    flat = jnp.concatenate([parts[k].reshape(-1) for k, _ in SMALL])
    return jnp.pad(flat, (0, SMALL_ROWS * LANES - SMALL_LEN)).reshape(SMALL_ROWS, LANES)


def _unpack_small(packed, shapes):
    flat = packed.reshape(-1)
    out, off = {}, 0
    for k, s in SMALL:
        out[k] = flat[off:off + s].reshape(shapes[k])
        off += s
    return out


def kernel(x, norm_mix_w, w_in, short_conv_w, ssd_conv_w, ssd_conv_b, dt_bias, a_log, d_skip, ssd_norm_w, w_out, norm_mlp_w, w_up, w_down, final_norm_w, loss_target, m_norm_mix_w, m_w_in, m_short_conv_w, m_ssd_conv_w, m_ssd_conv_b, m_dt_bias, m_a_log, m_d_skip, m_ssd_norm_w, m_w_out, m_norm_mlp_w, m_w_up, m_w_down, m_final_norm_w, v_norm_mix_w, v_w_in, v_short_conv_w, v_ssd_conv_w, v_ssd_conv_b, v_dt_bias, v_a_log, v_d_skip, v_ssd_norm_w, v_w_out, v_norm_mlp_w, v_w_up, v_w_down, v_final_norm_w):
    xs = x[0]
    t_len = xs.shape[0]
    tt = min(256, t_len)
    eh, eht = _head_matrices()
    me = 4 * lax.axis_index("x") + 2 * lax.axis_index("y") + lax.axis_index("c")

    def start_weights(i, after):
        first = exchange_start(
            [jnp.pad(w_in[i].astype(BF16), ((0, 0), (0, SLOT_W - IN_SHARD))), short_conv_w[i], ssd_conv_w[i]],
            False, "w_in_start_%d" % i, after, SIBLING_AND_CHIPS)
        rest = exchange_start([w_out[i].astype(BF16), w_up[i].astype(BF16), w_down[i].astype(BF16)], False,
                              "w_rest_start_%d" % i, first[4] if after is None else after, SIBLING_AND_CHIPS)
        return first, rest

    def fill_own(srcs, lands, per_peer):
        own = [lax.dynamic_index_in_dim(s_, me, 0, keepdims=False) for s_ in srcs] if per_peer else srcs
        return [lax.dynamic_update_index_in_dim(l_, o_, me, 0) for l_, o_ in zip(lands, own)]

    def finish_weights(started, after, name):
        srcs, lands = exchange_wait(started, after, False, name + "_wait", SIBLING_AND_CHIPS)
        return fill_own(srcs, relay_to_sibling(lands, name + "_relay"), False)

    act = xs
    saved, layers = [], []
    first, rest = start_weights(0, None)
    token = first[4][0, 0] + rest[4][0, 0]
    for i in range(DEPTH):
        g_in, g_sc, g_cw = finish_weights(first, act, "w_in_%d" % i)
        lw, prm = layer_params(
            assemble_w_in(g_in), g_sc.transpose(1, 0, 2).reshape(3, D), g_cw.transpose(1, 0, 2).reshape(4, XBC),
            norm_mix_w[i], norm_mlp_w[i], ssd_conv_b[i], dt_bias[i], a_log[i], d_skip[i], ssd_norm_w[i], eh, eht)
        lw["nw1"] = lw["nw1"] + token
        mixed = layer_fwd_mix(act, lw, prm, tt)
        g_out, g_up, g_dn = finish_weights(rest, mixed[3], "w_rest_%d" % i)
        lw.update(wout=g_out.reshape(MIX, D), wup=g_up, wdn=g_dn.reshape(DFF, D))
        if i + 1 < DEPTH:
            first, rest = start_weights(i + 1, g_dn)
            token = first[4][0, 0] + rest[4][0, 0]
            lw["nw2"] = lw["nw2"] + token
        layers.append((lw, prm))
        act, sv = layer_fwd_mlp(act, mixed, lw)
        saved.append(sv)
    loss_acc, dx, dxb, g_fw = loss_head(act, final_norm_w[None, :], loss_target[0])

    grads = [None] * DEPTH
    sent_rest, sent_in = [None] * DEPTH, [None] * DEPTH
    token = None
    for i in reversed(range(DEPTH)):
        lw, prm = layers[i]
        if token is not None:
            lw = dict(lw, nw2=lw["nw2"] + token)
        dx1, _, dy, g_mlp = layer_bwd_mlp(dx, dxb, lw, saved[i])
        sent_rest[i] = exchange_start(
            [g_mlp["wout"].reshape(N_DEV, MIX // N_DEV, D), g_mlp["wup"], g_mlp["wdn"].reshape(N_DEV, DFF // N_DEV, D)],
            True, "g_rest_start_%d" % i)
        dx, dxb, g_mix = layer_bwd_mix(dx1, dy, lw, dict(prm, nrm=prm["nrm"] + sent_rest[i][4][0, 0]), saved[i], tt)
        grads[i] = {**g_mlp, **g_mix}
        if i > 0:
            sent_in[i] = exchange_start([scatter_w_in(g_mix["win"])], True, "g_in_start_%d" % i)
            token = sent_in[i][4][0, 0]

    def stack(k):
        return jnp.stack([g[k] for g in grads])

    small = _pack_small({"norm_mix_w": stack("nw1"), "ssd_conv_b": stack("cb"), "dt_bias": stack("dtb"),
                         "a_log": stack("alog"), "d_skip": stack("dsk"), "ssd_norm_w": stack("nrm"),
                         "norm_mlp_w": stack("nw2"), "final_norm_w": g_fw[0]})
    r_small, r_sc, r_cw = all_gather([small, stack("scw"), stack("cw")], "gather_small_grads")
    r_sc = lax.dynamic_slice_in_dim(r_sc, me * (D // N_DEV), D // N_DEV, axis=3)
    r_cw = lax.dynamic_slice_in_dim(r_cw, me * (XBC // N_DEV), XBC // N_DEV, axis=3)
    sent_in[0] = exchange_start([scatter_w_in(grads[0]["win"])], True, "g_in_start_0", after=r_small)

    after = sent_in[0][4]
    recv = [fill_own(*exchange_wait(sent_rest[i], after, True, "g_rest_wait_%d" % i), True) for i in range(DEPTH)]
    res = {}
    res["w_out"] = adamw_layers(w_out, [r[0] for r in recv], m_w_out, v_w_out, "adamw_w_out")
    res["w_up"] = adamw_layers(w_up, [r[1] for r in recv], m_w_up, v_w_up, "adamw_w_up")
    res["w_down"] = adamw_layers(w_down, [r[2] for r in recv], m_w_down, v_w_down, "adamw_w_down")
    after = res["w_down"][1]
    recv_in = [fill_own(*exchange_wait(sent_in[i], after, True, "g_in_wait_%d" % i), True)[0] for i in range(DEPTH)]
    res["w_in"] = adamw_layers(w_in, recv_in, m_w_in, v_w_in, "adamw_w_in")
    res["short_conv_w"] = _adamw_nd(short_conv_w, r_sc, m_short_conv_w, v_short_conv_w, "adamw_short_conv")
    res["ssd_conv_w"] = _adamw_nd(ssd_conv_w, r_cw, m_ssd_conv_w, v_ssd_conv_w, "adamw_ssd_conv")
    small_w = {"norm_mix_w": norm_mix_w, "ssd_conv_b": ssd_conv_b, "dt_bias": dt_bias, "a_log": a_log,
               "d_skip": d_skip, "ssd_norm_w": ssd_norm_w, "norm_mlp_w": norm_mlp_w, "final_norm_w": final_norm_w}
    small_m = {"norm_mix_w": m_norm_mix_w, "ssd_conv_b": m_ssd_conv_b, "dt_bias": m_dt_bias, "a_log": m_a_log,
               "d_skip": m_d_skip, "ssd_norm_w": m_ssd_norm_w, "norm_mlp_w": m_norm_mlp_w,
               "final_norm_w": m_final_norm_w}
    small_v = {"norm_mix_w": v_norm_mix_w, "ssd_conv_b": v_ssd_conv_b, "dt_bias": v_dt_bias, "a_log": v_a_log,
               "d_skip": v_d_skip, "ssd_norm_w": v_ssd_norm_w, "norm_mlp_w": v_norm_mlp_w,
               "final_norm_w": v_final_norm_w}
    shapes = {k: a.shape for k, a in small_w.items()}
    packed = adamw(_pack_small(small_w), r_small, _pack_small(small_m), _pack_small(small_v), "adamw_small")
    unpacked = [_unpack_small(p, shapes) for p in packed]
    for k in small_w:
        res[k] = [u[k] for u in unpacked]

    loss = lax.psum(loss_acc[0, 0], ("x", "y", "c"))
    order = ["norm_mix_w", "w_in", "short_conv_w", "ssd_conv_w", "ssd_conv_b", "dt_bias", "a_log", "d_skip",
             "ssd_norm_w", "w_out", "norm_mlp_w", "w_up", "w_down", "final_norm_w"]
    out = [loss, dx[None]]
    for part in range(4):
        out += [res[k][part] for k in order]
    return tuple(out)
```

```python
import functools

import numpy as np
import jax
import jax.numpy as jnp
from jax import lax
from jax.experimental import pallas as pl
from jax.experimental.pallas import tpu as pltpu

F32 = jnp.float32
BF16 = jnp.bfloat16
SDS = jax.ShapeDtypeStruct

N_DEV = 8
DEPTH = 4
D = 1024
NIN = 5648
NINP = 5760
DFF = 4096
MIX = 2048
NHEAD = 16
HDIM = 64
NSTATE = 128
CHUNK = 64
XBC = 1536
EPS = 1e-5
LANES = 128

C_UB, C_UC, C_UH, C_Z, C_XS, C_BC, C_DT = 0, 1024, 2048, 3072, 4096, 5120, 5632
A_CV, A_YS, A_PX, A_PBC, AUX_W = 0, 1024, 2048, 3072, 3584

ADAM_LR = 0.001
ADAM_B1 = 0.9
ADAM_B2 = 0.999
ADAM_EPS = 1e-08
ADAM_WD = 0.01
ADAM_STEP = 10

VMEM_LIMIT = 56 * 1024 * 1024
MESH = pl.DeviceIdType.MESH


def _cparams(sem):
    return pltpu.CompilerParams(dimension_semantics=sem, vmem_limit_bytes=VMEM_LIMIT)


def _nt(a, b):
    return lax.dot_general(a, b, (((1,), (1,)), ((), ())), preferred_element_type=F32)


def _tn(a, b):
    return lax.dot_general(a, b, (((0,), (0,)), ((), ())), preferred_element_type=F32)


def _nn(a, b):
    return jnp.dot(a, b, preferred_element_type=F32)


def _sigmoid(v):
    return 0.5 * jnp.tanh(0.5 * v) + 0.5


def _split3(v):
    v1 = v.astype(BF16)
    r1 = v - v1.astype(F32)
    v2 = r1.astype(BF16)
    v3 = (r1 - v2.astype(F32)).astype(BF16)
    return v1, v2, v3


def _expand(v, eh):
    v1, v2, v3 = _split3(v)
    return _nn(v1, eh) + _nn(v2, eh) + _nn(v3, eh)


def _head_reduce(v, eht):
    v1 = v.astype(BF16)
    v2 = (v - v1.astype(F32)).astype(BF16)
    return _nn(v1, eht) + _nn(v2, eht)


def _head_matrices():
    eh = np.zeros((LANES, D), np.float32)
    for h in range(NHEAD):
        eh[h, h * HDIM:(h + 1) * HDIM] = 1.0
    return jnp.asarray(eh, BF16), jnp.asarray(eh.T.copy(), BF16)


def _resident(shape):
    return pl.BlockSpec(shape, lambda *_: (0,) * len(shape), pipeline_mode=pl.Buffered(1))


def _col_chunks(n, step):
    return [(c, min(c + step, n)) for c in range(0, n, step)]


def norm_matmul(x, nw, w, name):
    t_len = x.shape[0]
    blocked = w.ndim == 3
    n_len = w.shape[0] * w.shape[2] if blocked else w.shape[1]
    tm = min(512, t_len)
    chunks = _col_chunks(n_len, n_len // N_DEV if blocked else 1536)

    def body(x_ref, nw_ref, w_ref, o_ref, h_ref):
        xv = x_ref[...]
        r = lax.rsqrt(jnp.mean(xv * xv, axis=-1, keepdims=True) + EPS)
        hv = (xv * r * nw_ref[...]).astype(BF16)
        h_ref[...] = hv
        for j, (c0, c1) in enumerate(chunks):
            wj = w_ref[j] if blocked else w_ref[:, c0:c1]
            o_ref[:, c0:c1] = _nn(hv, wj).astype(o_ref.dtype)

    return pl.pallas_call(
        body, grid=(t_len // tm,),
        in_specs=[pl.BlockSpec((tm, D), lambda i: (i, 0)), _resident((1, D)), _resident(w.shape)],
        out_specs=[pl.BlockSpec((tm, n_len), lambda i: (i, 0)),
                   pl.BlockSpec((tm, D), lambda i: (i, 0))],
        out_shape=[SDS((t_len, n_len), BF16), SDS((t_len, D), BF16)],
        compiler_params=_cparams(("parallel",)), name=name)(x, nw, w)


def matmul_residual(a, w, res, relu2, name):
    t_len, k_len = a.shape
    tm = min(512, t_len)

    def body(a_ref, w_ref, res_ref, o_ref):
        av = a_ref[...]
        if relu2:
            af = jnp.maximum(av.astype(F32), 0.0)
            av = (af * af).astype(BF16)
        o_ref[...] = res_ref[...] + _nn(av, w_ref[...])

    return pl.pallas_call(
        body, grid=(t_len // tm,),
        in_specs=[pl.BlockSpec((tm, k_len), lambda i: (i, 0)),
                  _resident((k_len, D)),
                  pl.BlockSpec((tm, D), lambda i: (i, 0))],
        out_specs=pl.BlockSpec((tm, D), lambda i: (i, 0)),
        out_shape=SDS((t_len, D), F32),
        compiler_params=_cparams(("parallel",)), name=name)(a, w, res)


def matmul_nt_act(dy, w, u, name):
    t_len = dy.shape[0]
    n_len = w.shape[0]
    tm = min(512, t_len)
    chunks = _col_chunks(n_len, 1024)

    def body(dy_ref, w_ref, *rest):
        if u is None:
            (o_ref,) = rest
        else:
            u_ref, o_ref = rest
        dyv = dy_ref[...]
        for c0, c1 in chunks:
            p = _nt(dyv, w_ref[c0:c1, :])
            if u is not None:
                p = p * (2.0 * jnp.maximum(u_ref[:, c0:c1].astype(F32), 0.0))
            o_ref[:, c0:c1] = p.astype(o_ref.dtype)

    in_specs = [pl.BlockSpec((tm, D), lambda i: (i, 0)), _resident((n_len, D))]
    args = [dy, w]
    if u is not None:
        in_specs.append(pl.BlockSpec((tm, n_len), lambda i: (i, 0)))
        args.append(u)
    return pl.pallas_call(
        body, grid=(t_len // tm,),
        in_specs=in_specs,
        out_specs=pl.BlockSpec((tm, n_len), lambda i: (i, 0)),
        out_shape=SDS((t_len, n_len), BF16),
        compiler_params=_cparams(("parallel",)), name=name)(*args)


def matmul_tn(a, b, a_spec, b_spec, o_spec, o_shape, n_out, relu2, name):
    t_len = a.shape[0]
    tt = min(2048, t_len)
    nt = t_len // tt

    def body(a_ref, b_ref, o_ref, acc):
        t = pl.program_id(1)
        av = a_ref[...]
        if relu2:
            af = jnp.maximum(av.astype(F32), 0.0)
            av = (af * af).astype(BF16)
        p = _tn(av, b_ref[...])

        @pl.when(t == 0)
        def _():
            acc[...] = p

        @pl.when(t > 0)
        def _():
            acc[...] += p

        @pl.when(t == nt - 1)
        def _():
            if len(blk) == 3:
                for j in range(blk[0]):
                    o_ref[j] = acc[:, j * blk[2]:(j + 1) * blk[2]].astype(o_ref.dtype)
            else:
                o_ref[...] = acc[...].astype(o_ref.dtype)

    blk = tuple(o_spec.block_shape)
    acc_shape = (blk[1], blk[0] * blk[2]) if len(blk) == 3 else blk
    return pl.pallas_call(
        body, grid=(n_out, nt),
        in_specs=[a_spec(tt), b_spec(tt)],
        out_specs=o_spec, out_shape=o_shape,
        scratch_shapes=[pltpu.VMEM(acc_shape, F32)],
        compiler_params=_cparams(("parallel", "arbitrary")), name=name)(a, b)


def matmul_nt_norm_bwd(dy, w, x, nw, dres, name):
    t_len = x.shape[0]
    blocked = w.ndim == 3
    k_len = dy.shape[1]
    kb = k_len // N_DEV
    tm = min(512, t_len)

    def body(dy_ref, w_ref, x_ref, nw_ref, dres_ref, dx_ref, dxb_ref, dnw_ref):
        @pl.when(pl.program_id(0) == 0)
        def _():
            dnw_ref[...] = jnp.zeros_like(dnw_ref)

        if blocked:
            dh = _nt(dy_ref[:, 0:kb], w_ref[0])
            for j in range(1, N_DEV):
                dh = dh + _nt(dy_ref[:, j * kb:(j + 1) * kb], w_ref[j])
        else:
            dh = _nt(dy_ref[...], w_ref[...])
        xv = x_ref[...]
        r = lax.rsqrt(jnp.mean(xv * xv, axis=-1, keepdims=True) + EPS)
        xh = xv * r
        dnw_ref[0:1, :] += jnp.sum(dh * xh, axis=0, keepdims=True)
        g = dh * nw_ref[...]
        dx = dres_ref[...] + r * (g - xh * jnp.mean(g * xh, axis=-1, keepdims=True))
        dx_ref[...] = dx
        dxb_ref[...] = dx.astype(BF16)

    return pl.pallas_call(
        body, grid=(t_len // tm,),
        in_specs=[pl.BlockSpec((tm, k_len), lambda i: (i, 0)),
                  _resident(w.shape),
                  pl.BlockSpec((tm, D), lambda i: (i, 0)),
                  _resident((1, D)),
                  pl.BlockSpec((tm, D), lambda i: (i, 0))],
        out_specs=[pl.BlockSpec((tm, D), lambda i: (i, 0)),
                   pl.BlockSpec((tm, D), lambda i: (i, 0)),
                   pl.BlockSpec((8, D), lambda i: (0, 0))],
        out_shape=[SDS((t_len, D), F32), SDS((t_len, D), BF16), SDS((8, D), F32)],
        compiler_params=_cparams(("arbitrary",)), name=name)(dy, w, x, nw, dres)


def loss_head(x, fw, tgt):
    t_len = x.shape[0]
    tm = min(512, t_len)

    def body(x_ref, fw_ref, t_ref, loss_ref, dx_ref, dxb_ref, dfw_ref):
        @pl.when(pl.program_id(0) == 0)
        def _():
            loss_ref[...] = jnp.zeros_like(loss_ref)
            dfw_ref[...] = jnp.zeros_like(dfw_ref)
        xv = x_ref[...]
        r = lax.rsqrt(jnp.mean(xv * xv, axis=-1, keepdims=True) + EPS)
        xh = xv * r
        w = fw_ref[...]
        e = xh * w - t_ref[...]
        row = jnp.sum(e * e, axis=-1, keepdims=True) * (1.0 / D)
        loss_ref[...] += 0.5 * jnp.sum(row, axis=0, keepdims=True)
        dyf = e * (1.0 / D)
        dfw_ref[0:1, :] += jnp.sum(dyf * xh, axis=0, keepdims=True)
        g = dyf * w
        dx = r * (g - xh * jnp.mean(g * xh, axis=-1, keepdims=True))
        dx_ref[...] = dx
        dxb_ref[...] = dx.astype(BF16)

    return pl.pallas_call(
        body, grid=(t_len // tm,),
        in_specs=[pl.BlockSpec((tm, D), lambda i: (i, 0)),
                  pl.BlockSpec((1, D), lambda i: (0, 0)),
                  pl.BlockSpec((tm, D), lambda i: (i, 0))],
        out_specs=[pl.BlockSpec((8, LANES), lambda i: (0, 0)),
                   pl.BlockSpec((tm, D), lambda i: (i, 0)),
                   pl.BlockSpec((tm, D), lambda i: (i, 0)),
                   pl.BlockSpec((8, D), lambda i: (0, 0))],
        out_shape=[SDS((8, LANES), F32), SDS((t_len, D), F32), SDS((t_len, D), BF16), SDS((8, D), F32)],
        compiler_params=_cparams(("arbitrary",)), name="loss_head")(x, fw, tgt)


TAP_SHIFTS = (3, 2, 1)


def _shift_matrix(n, up):
    r = lax.broadcasted_iota(jnp.int32, (n, n), 0)
    c = lax.broadcasted_iota(jnp.int32, (n, n), 1)
    return jnp.concatenate([jnp.where(c == (r + j if up else r - j), 1.0, 0.0).astype(BF16) for j in TAP_SHIFTS],
                           axis=0)


def _shifts_dn(xb, halo, sm, n_shifts):
    n = xb.shape[0]
    first = len(TAP_SHIFTS) - n_shifts
    moved = _nn(sm[first * n:], xb)
    row = lax.broadcasted_iota(jnp.int32, halo.shape, 0)
    outs = []
    for k in range(n_shifts):
        j = TAP_SHIFTS[first + k]
        o = moved[k * n:(k + 1) * n]
        top = jnp.where(row < j, pltpu.roll(halo, j, 0), o[0:8])
        outs.append(jnp.concatenate([top, o[8:]], axis=0))
    return outs


def _shifts_up(xb, nxt, sm, n_shifts):
    n = xb.shape[0]
    first = len(TAP_SHIFTS) - n_shifts
    moved = _nn(sm[first * n:], xb)
    row = lax.broadcasted_iota(jnp.int32, nxt.shape, 0)
    outs = []
    for k in range(n_shifts):
        j = TAP_SHIFTS[first + k]
        o = moved[k * n:(k + 1) * n]
        bot = jnp.where(row >= 8 - j, pltpu.roll(nxt, 8 - j, 0), o[n - 8:n])
        outs.append(jnp.concatenate([o[:n - 8], bot], axis=0))
    return outs


def _conv_fwd(x, xb, halo, w_ref, kw, sm):
    shifted = _shifts_dn(xb, halo, sm, kw - 1)
    acc = w_ref[kw - 1:kw, :] * x
    for k in range(kw - 1):
        acc = acc + w_ref[k:k + 1, :] * shifted[k]
    return acc


def _chunk_cumsum(a, pos):
    for sh in (1, 2, 4, 8, 16, 32):
        a = a + jnp.where(pos >= sh, pltpu.roll(a, sh, 0), 0.0)
    return a


def _chunk_rcumsum(a, pos):
    n = a.shape[0]
    for sh in (1, 2, 4, 8, 16, 32):
        a = a + jnp.where(pos < CHUNK - sh, pltpu.roll(a, n - sh, 0), 0.0)
    return a


def _softplus(v):
    return jnp.maximum(v, 0.0) + jnp.log(1.0 + jnp.exp(-jnp.abs(v)))


def _silu(v):
    return v * _sigmoid(v)


def _dsilu(v):
    s = _sigmoid(v)
    return s * (1.0 + v * (1.0 - s))


def _lane_masks(width=D):
    lane = lax.broadcasted_iota(jnp.int32, (CHUNK, width), 1) & (HDIM - 1)
    row = lax.broadcasted_iota(jnp.int32, (CHUNK, width), 0)
    return lane == row, lane <= row


def _rep_matrix():
    lane = lax.broadcasted_iota(jnp.int32, (CHUNK, 512), 1) & (HDIM - 1)
    row = lax.broadcasted_iota(jnp.int32, (CHUNK, 512), 0)
    return jnp.where(lane == row, 1.0, 0.0).astype(BF16)


def _blockdiag(xp):
    lane = lax.broadcasted_iota(jnp.int32, xp.shape, 1)
    zero = jnp.zeros_like(xp)
    return jnp.concatenate([jnp.where(lane < HDIM, xp, zero), jnp.where(lane >= HDIM, xp, zero)], axis=0)


def _mixer_views(tt):
    r8 = tt // 8

    def main(width, col):
        return pl.BlockSpec((tt, width), lambda i, c=col // width: (i, c))

    def halo(width, col):
        return pl.BlockSpec((8, width), lambda i, c=col // width: (jnp.maximum(i * r8 - 1, 0), c))

    return main, halo


def mixer_fwd(proj, prm, tt):
    t_len = proj.shape[0]
    nblk = t_len // tt
    nc = tt // CHUNK
    main, halo = _mixer_views(tt)

    def body(ub_ref, uc_ref, uh_ref, z_ref, xr_ref, bcr_ref, dtr_ref, uch_ref, uhh_ref, xrh_ref, bcrh_ref,
             scw_ref, cwx_ref, cwbc_ref, cbx_ref, cbbc_ref, dtb_ref, alog_ref, dsk_ref, nrm_ref, eh_ref,
             y_ref, st_ref, aux_ref, hs, xs_s, bc_s, dtx_s, cumx_s, yssd_s):
        i = pl.program_id(0)
        first = i == 0

        @pl.when(first)
        def _():
            hs[...] = jnp.zeros_like(hs)

        keep = jnp.where(first, 0.0, 1.0)
        sm = _shift_matrix(tt, False)
        v = uc_ref[...].astype(F32) * uh_ref[...].astype(F32)
        vh = uch_ref[...].astype(F32) * uhh_ref[...].astype(F32) * keep
        cv = _conv_fwd(v, v.astype(BF16), vh, scw_ref, 3, sm)
        aux_ref[:, A_CV:A_CV + D] = cv.astype(BF16)
        y_ref[:, 0:D] = (ub_ref[...].astype(F32) * cv).astype(BF16)

        xrb = xr_ref[...]
        pre_x = _conv_fwd(xrb.astype(F32), xrb, xrh_ref[...].astype(F32) * keep, cwx_ref, 4, sm) + cbx_ref[...]
        aux_ref[:, A_PX:A_PX + D] = pre_x.astype(BF16)
        xs_s[...] = _silu(pre_x)
        bcrb = bcr_ref[...]
        pre_bc = _conv_fwd(bcrb.astype(F32), bcrb, bcrh_ref[...].astype(F32) * keep, cwbc_ref, 4, sm) + cbbc_ref[...]
        aux_ref[:, A_PBC:A_PBC + 512] = pre_bc.astype(BF16)
        bc_s[...] = _silu(pre_bc)
        dt = _softplus(dtr_ref[...].astype(F32) + dtb_ref[...])
        a_neg = -jnp.exp(alog_ref[...])
        pos = lax.broadcasted_iota(jnp.int32, (tt, LANES), 0) & (CHUNK - 1)
        cum = _chunk_cumsum(dt * a_neg, pos)
        eh = eh_ref[...]
        dtx_s[...] = _expand(dt, eh)
        cumx_s[...] = _expand(cum, eh)
        irep, causal = _lane_masks()
        rep = _rep_matrix()

        def chunk(c, carry):
            r0 = pl.multiple_of(c * CHUNK, CHUNK)
            rows = pl.ds(r0, CHUNK)
            cumx = cumx_s[rows, :]
            cum_l = cumx[CHUNK - 1:CHUNK, :]
            xd = xs_s[rows, :] * dtx_s[rows, :]
            xf = xd * jnp.exp(cum_l - cumx)
            ex = jnp.exp(cumx)
            e_l = jnp.exp(cum_l)
            rvec = jnp.sum(jnp.where(irep, cumx, 0.0), axis=0, keepdims=True)
            lam = jnp.where(causal, jnp.exp(jnp.where(causal, cumx - rvec, 0.0)), 0.0)
            bc = bc_s[rows, :]
            for g in range(2):
                gs = slice(g * 512, (g + 1) * 512)
                bg = bc[:, g * NSTATE:(g + 1) * NSTATE].astype(BF16)
                cg = bc[:, 256 + g * NSTATE:256 + (g + 1) * NSTATE].astype(BF16)
                s_rep = _nn(_nt(cg, bg).astype(BF16), rep)
                m_g = (s_rep * lam[:, gs]).astype(BF16)
                h_g = hs[:, gs]
                h_b = h_g.astype(BF16)
                st_ref[c, :, gs] = h_b
                yo = _nn(cg, h_b) * ex[:, gs]
                xd_b = xd[:, gs].astype(BF16)
                for hp in range(4):
                    ps = slice(hp * LANES, (hp + 1) * LANES)
                    yd = _nn(m_g[:, ps], _blockdiag(xd_b[:, ps]))
                    yssd_s[rows, g * 512 + hp * LANES:g * 512 + (hp + 1) * LANES] = yd + yo[:, ps]
                hs[:, gs] = h_g * e_l[:, gs] + _tn(bg, xf[:, gs].astype(BF16))
            return carry

        lax.fori_loop(0, nc, chunk, 0, unroll=True)

        ys = yssd_s[...] + dsk_ref[...] * xs_s[...]
        aux_ref[:, A_YS:A_YS + D] = ys.astype(BF16)
        gt = ys * _silu(z_ref[...].astype(F32))
        for g in range(2):
            gs = slice(g * 512, (g + 1) * 512)
            gg = gt[:, gs]
            rn = lax.rsqrt(jnp.mean(gg * gg, axis=-1, keepdims=True) + EPS)
            y_ref[:, D + g * 512:D + (g + 1) * 512] = (gg * rn * nrm_ref[:, gs]).astype(BF16)

    def const(shape):
        return pl.BlockSpec(shape, lambda i: (0, 0))

    in_specs = [main(D, C_UB), main(D, C_UC), main(D, C_UH), main(D, C_Z), main(D, C_XS), main(512, C_BC),
                main(LANES, C_DT), halo(D, C_UC), halo(D, C_UH), halo(D, C_XS), halo(512, C_BC),
                const((8, D)), const((8, D)), const((8, 512)), const((1, D)), const((1, 512)),
                const((1, LANES)), const((1, LANES)), const((1, D)), const((1, D)), const((LANES, D))]
    return pl.pallas_call(
        body, grid=(nblk,),
        in_specs=in_specs,
        out_specs=[pl.BlockSpec((tt, MIX), lambda i: (i, 0)),
                   pl.BlockSpec((nc, NSTATE, D), lambda i: (i, 0, 0)),
                   pl.BlockSpec((tt, AUX_W), lambda i: (i, 0))],
        out_shape=[SDS((t_len, MIX), BF16), SDS((t_len // CHUNK, NSTATE, D), BF16), SDS((t_len, AUX_W), BF16)],
        scratch_shapes=[pltpu.VMEM((NSTATE, D), F32), pltpu.VMEM((tt, D), F32), pltpu.VMEM((tt, 512), F32),
                        pltpu.VMEM((tt, D), F32), pltpu.VMEM((tt, D), F32), pltpu.VMEM((tt, D), F32)],
        compiler_params=_cparams(("arbitrary",)), name="mixer_fwd")(
            *([proj] * 11), prm["scw"], prm["cwx"], prm["cwbc"], prm["cbx"], prm["cbbc"], prm["dtb"],
            prm["alog"], prm["dskx"], prm["nrm"], prm["eh"])


def mixer_bwd(proj, dy, states, aux, prm, tt):
    t_len = proj.shape[0]
    nblk = t_len // tt
    nc = tt // CHUNK

    def rev(i):
        return nblk - 1 - i

    def main(width, col):
        return pl.BlockSpec((tt, width), lambda i, c=col // width: (rev(i), c))

    def body(ub_ref, uc_ref, uh_ref, z_ref, xr_ref, bcr_ref, dtr_ref, dy_ref, st_ref, aux_ref,
             scw_ref, cwx_ref, cwbc_ref, cbx_ref, cbbc_ref, dtb_ref, alog_ref, dsk_ref, nrm_ref, eh_ref, eht_ref,
             dp_ref, gscw_ref, gcwx_ref, gcwbc_ref, gvec_ref, gdt_ref,
             dhs, xs_s, bc_s, dtx_s, cumx_s, dys_s, dxs_s, dbc_s, red_s, ddtx_s, nx_cv, nx_px, nx_pbc, sgx_s, sgbc_s):
        i = pl.program_id(0)

        @pl.when(i == 0)
        def _():
            dhs[...] = jnp.zeros_like(dhs)
            nx_cv[...] = jnp.zeros_like(nx_cv)
            nx_px[...] = jnp.zeros_like(nx_px)
            nx_pbc[...] = jnp.zeros_like(nx_pbc)
            gscw_ref[...] = jnp.zeros_like(gscw_ref)
            gcwx_ref[...] = jnp.zeros_like(gcwx_ref)
            gcwbc_ref[...] = jnp.zeros_like(gcwbc_ref)
            gvec_ref[...] = jnp.zeros_like(gvec_ref)
            gdt_ref[...] = jnp.zeros_like(gdt_ref)

        uc = uc_ref[...].astype(F32)
        uh = uh_ref[...].astype(F32)
        v = uc * uh
        dya = dy_ref[:, 0:D].astype(F32)
        dp_ref[:, C_UB:C_UB + D] = (dya * aux_ref[:, A_CV:A_CV + D].astype(F32)).astype(BF16)
        dcv = dya * ub_ref[...].astype(F32)
        sm = _shift_matrix(tt, True)
        ups = _shifts_up(dcv.astype(BF16), nx_cv[...], sm, 2) + [dcv]
        dv = None
        for k in range(3):
            gscw_ref[k:k + 1, :] += jnp.sum(v * ups[k], axis=0, keepdims=True)
            term = scw_ref[k:k + 1, :] * ups[k]
            dv = term if dv is None else dv + term
        nx_cv[...] = dcv[0:8]
        dp_ref[:, C_UC:C_UC + D] = (dv * uh).astype(BF16)
        dp_ref[:, C_UH:C_UH + D] = (dv * uc).astype(BF16)

        pre_x = aux_ref[:, A_PX:A_PX + D].astype(F32)
        pre_bc = aux_ref[:, A_PBC:A_PBC + 512].astype(F32)
        sg_x = _sigmoid(pre_x)
        sg_bc = _sigmoid(pre_bc)
        sgx_s[...] = sg_x
        sgbc_s[...] = sg_bc
        xs = pre_x * sg_x
        xs_s[...] = xs
        bc_s[...] = pre_bc * sg_bc
        dt_pre = dtr_ref[...].astype(F32) + dtb_ref[...]
        dt = _softplus(dt_pre)
        a_neg = -jnp.exp(alog_ref[...])
        pos = lax.broadcasted_iota(jnp.int32, (tt, LANES), 0) & (CHUNK - 1)
        cum = _chunk_cumsum(dt * a_neg, pos)
        eh = eh_ref[...]
        eht = eht_ref[...]
        dtx_s[...] = _expand(dt, eh)
        cumx_s[...] = _expand(cum, eh)

        irep, causal = _lane_masks()
        irep_g, _ = _lane_masks(512)
        rep = _rep_matrix()
        row64 = lax.broadcasted_iota(jnp.int32, (CHUNK, 512), 0)
        lane128 = lax.broadcasted_iota(jnp.int32, (CHUNK, LANES), 1)

        z = z_ref[...].astype(F32)
        sg_z = _sigmoid(z)
        sz = z * sg_z
        dsz = sg_z * (1.0 + z * (1.0 - sg_z))
        ys = aux_ref[:, A_YS:A_YS + D].astype(F32)
        gt = ys * sz
        dyb = dy_ref[:, D:MIX].astype(F32)
        for g in range(2):
            gs = slice(g * 512, (g + 1) * 512)
            gg = gt[:, gs]
            rn = lax.rsqrt(jnp.mean(gg * gg, axis=-1, keepdims=True) + EPS)
            gvec_ref[0:1, gs] += jnp.sum(dyb[:, gs] * gg * rn, axis=0, keepdims=True)
            dgn = dyb[:, gs] * nrm_ref[:, gs]
            dgt = rn * (dgn - gg * (rn * rn) * jnp.mean(dgn * gg, axis=-1, keepdims=True))
            dys = dgt * sz[:, gs]
            dys_s[:, gs] = dys
            dp_ref[:, C_Z + g * 512:C_Z + (g + 1) * 512] = (dgt * ys[:, gs] * dsz[:, gs]).astype(BF16)
        dys_all = dys_s[...]
        gvec_ref[1:2, :] += jnp.sum(dys_all * xs, axis=0, keepdims=True)

        def bwd_chunk(cc, carry):
            c = nc - 1 - cc
            r0 = pl.multiple_of(c * CHUNK, CHUNK)
            rows = pl.ds(r0, CHUNK)
            cumx = cumx_s[rows, :]
            cum_l = cumx[CHUNK - 1:CHUNK, :]
            xs_c = xs_s[rows, :]
            dtx = dtx_s[rows, :]
            xd = xs_c * dtx
            f = jnp.exp(cum_l - cumx)
            xf = xd * f
            ex = jnp.exp(cumx)
            e_l = jnp.exp(cum_l)
            rvec = jnp.sum(jnp.where(irep, cumx, 0.0), axis=0, keepdims=True)
            lam = jnp.where(causal, jnp.exp(jnp.where(causal, cumx - rvec, 0.0)), 0.0)
            bc = bc_s[rows, :]
            dyc = dys_s[rows, :]
            for g in range(2):
                gs = slice(g * 512, (g + 1) * 512)
                bg = bc[:, g * NSTATE:(g + 1) * NSTATE].astype(BF16)
                cg = bc[:, 256 + g * NSTATE:256 + (g + 1) * NSTATE].astype(BF16)
                h0 = st_ref[c, :, gs]
                dh = dhs[:, gs]
                dh_b = dh.astype(BF16)
                xf_g = xf[:, gs]
                dxf = _nn(bg, dh_b)
                db = _nt(xf_g.astype(BF16), dh_b)
                s_rep = _nn(_nt(cg, bg).astype(BF16), rep)
                lam_g = lam[:, gs]
                m_g = s_rep * lam_g
                m_b = m_g.astype(BF16)
                ex_g = ex[:, gs]
                dy_g = dyc[:, gs]
                yo = _nn(cg, h0) * ex_g
                dg_b = (dy_g * ex_g).astype(BF16)
                dc = _nt(dg_b, h0)
                el_g = e_l[:, gs]
                dee = jnp.sum(dh * h0.astype(F32), axis=0, keepdims=True) * el_g
                dhs[:, gs] = dh * el_g + _tn(cg, dg_b)
                xd_b = xd[:, gs].astype(BF16)
                dy_b = dy_g.astype(BF16)
                dm_parts, dxd_parts = [], []
                for hp in range(4):
                    ps = slice(hp * LANES, (hp + 1) * LANES)
                    bd = _blockdiag(xd_b[:, ps])
                    dm_parts.append(_nt(dy_b[:, ps], bd))
                    t2 = _tn(m_b[:, ps], dy_b[:, ps])
                    dxd_parts.append(jnp.where(lane128 < HDIM, t2[0:CHUNK], t2[CHUNK:2 * CHUNK]))
                dm = jnp.concatenate(dm_parts, axis=1)
                dxd = jnp.concatenate(dxd_parts, axis=1) + dxf * f[:, gs]
                dseg = dm * m_g
                ds_b = _nt((dm * lam_g).astype(BF16), rep).astype(BF16)
                dc = dc + _nn(ds_b, bg)
                db = db + _tn(ds_b, cg)
                colsum = jnp.sum(dseg, axis=0, keepdims=True)
                dxfxf = dxf * xf_g
                red = dseg - jnp.where(irep_g, colsum, 0.0) + dy_g * yo - dxfxf
                last = jnp.sum(dxfxf, axis=0, keepdims=True) + dee
                red = red + jnp.where(row64 == CHUNK - 1, last, 0.0)
                red_s[rows, gs] = red
                ddtx_s[rows, gs] = dxd * xs_c[:, gs]
                dxs_s[rows, gs] = dxd * dtx[:, gs] + dsk_ref[:, gs] * dy_g
                dbc_s[rows, g * NSTATE:(g + 1) * NSTATE] = db
                dbc_s[rows, 256 + g * NSTATE:256 + (g + 1) * NSTATE] = dc
            return carry

        lax.fori_loop(0, nc, bwd_chunk, 0, unroll=True)

        dcum = _head_reduce(red_s[...], eht)
        da = _chunk_rcumsum(dcum, pos)
        ddt = _head_reduce(ddtx_s[...], eht) + da * a_neg
        gdt_ref[1:2, :] += jnp.sum(da * dt, axis=0, keepdims=True) * a_neg
        ddt_raw = ddt * _sigmoid(dt_pre)
        lane_t = lax.broadcasted_iota(jnp.int32, (tt, LANES), 1)
        ddt_raw = jnp.where(lane_t < NHEAD, ddt_raw, 0.0)
        gdt_ref[0:1, :] += jnp.sum(ddt_raw, axis=0, keepdims=True)
        dp_ref[:, C_DT:C_DT + LANES] = ddt_raw.astype(BF16)

        sg_x = sgx_s[...]
        sg_bc = sgbc_s[...]
        pre_x = aux_ref[:, A_PX:A_PX + D].astype(F32)
        pre_bc = aux_ref[:, A_PBC:A_PBC + 512].astype(F32)
        dpx = dxs_s[...] * (sg_x * (1.0 + pre_x * (1.0 - sg_x)))
        dpbc = dbc_s[...] * (sg_bc * (1.0 + pre_bc * (1.0 - sg_bc)))
        gvec_ref[2:3, :] += jnp.sum(dpx, axis=0, keepdims=True)
        gcwbc_ref[4:5, :] += jnp.sum(dpbc, axis=0, keepdims=True)
        xraw = xr_ref[...].astype(F32)
        bcraw = bcr_ref[...].astype(F32)
        ups_x = _shifts_up(dpx.astype(BF16), nx_px[...], sm, 3) + [dpx]
        ups_bc = _shifts_up(dpbc.astype(BF16), nx_pbc[...], sm, 3) + [dpbc]
        dxr, dbcr = None, None
        for k in range(4):
            up_x = ups_x[k]
            up_bc = ups_bc[k]
            gcwx_ref[k:k + 1, :] += jnp.sum(xraw * up_x, axis=0, keepdims=True)
            gcwbc_ref[k:k + 1, :] += jnp.sum(bcraw * up_bc, axis=0, keepdims=True)
            tx = cwx_ref[k:k + 1, :] * up_x
            tb = cwbc_ref[k:k + 1, :] * up_bc
            dxr = tx if dxr is None else dxr + tx
            dbcr = tb if dbcr is None else dbcr + tb
        nx_px[...] = dpx[0:8]
        nx_pbc[...] = dpbc[0:8]
        dp_ref[:, C_XS:C_XS + D] = dxr.astype(BF16)
        dp_ref[:, C_BC:C_BC + 512] = dbcr.astype(BF16)

        @pl.when(i == nblk - 1)
        def _():
            gdt_ref[2:3, :] = _head_reduce(gvec_ref[1:2, :] * jnp.ones((8, 1), F32), eht)[0:1, :]

    def const(shape):
        return pl.BlockSpec(shape, lambda i: (0, 0))

    in_specs = [main(D, C_UB), main(D, C_UC), main(D, C_UH), main(D, C_Z), main(D, C_XS), main(512, C_BC),
                main(LANES, C_DT),
                pl.BlockSpec((tt, MIX), lambda i: (rev(i), 0)),
                pl.BlockSpec((nc, NSTATE, D), lambda i: (rev(i), 0, 0)),
                pl.BlockSpec((tt, AUX_W), lambda i: (rev(i), 0)),
                const((8, D)), const((8, D)), const((8, 512)), const((1, D)), const((1, 512)),
                const((1, LANES)), const((1, LANES)), const((1, D)), const((1, D)), const((LANES, D)),
                const((D, LANES))]
    return pl.pallas_call(
        body, grid=(nblk,),
        in_specs=in_specs,
        out_specs=[pl.BlockSpec((tt, NINP), lambda i: (rev(i), 0)),
                   const((8, D)), const((8, D)), const((8, 512)), const((8, D)), const((8, LANES))],
        out_shape=[SDS((t_len, NINP), BF16), SDS((8, D), F32), SDS((8, D), F32), SDS((8, 512), F32),
                   SDS((8, D), F32), SDS((8, LANES), F32)],
        scratch_shapes=[pltpu.VMEM((NSTATE, D), F32),
                        pltpu.VMEM((tt, D), F32), pltpu.VMEM((tt, 512), F32),
                        pltpu.VMEM((tt, D), F32), pltpu.VMEM((tt, D), F32),
                        pltpu.VMEM((tt, D), F32), pltpu.VMEM((tt, D), F32),
                        pltpu.VMEM((tt, 512), F32),
                        pltpu.VMEM((tt, D), F32), pltpu.VMEM((tt, D), F32),
                        pltpu.VMEM((8, D), F32), pltpu.VMEM((8, D), F32), pltpu.VMEM((8, 512), F32),
                        pltpu.VMEM((tt, D), F32), pltpu.VMEM((tt, 512), F32)],
        compiler_params=_cparams(("arbitrary",)), name="mixer_bwd")(
            *([proj] * 7), dy, states, aux, prm["scw"], prm["cwx"], prm["cwbc"], prm["cbx"], prm["cbbc"], prm["dtb"],
            prm["alog"], prm["dskx"], prm["nrm"], prm["eh"], prm["eht"])


TN_IN = 1920


def layer_fwd_mix(x, lw, prm, tt):
    proj, h1 = norm_matmul(x, lw["nw1"], lw["win"], "in_proj")
    y, st, aux = mixer_fwd(proj, prm, tt)
    return h1, proj, (st, aux), y


def layer_fwd_mlp(x, mixed, lw):
    h1, proj, st, y = mixed
    x1 = matmul_residual(y, lw["wout"], x, False, "out_proj")
    u, h2 = norm_matmul(x1, lw["nw2"], lw["wup"], "up_proj")
    x2 = matmul_residual(u, lw["wdn"], x1, True, "down_proj")
    return x2, (x, h1, proj, st, y, x1, h2, u)


def layer_fwd(x, lw, prm, tt):
    return layer_fwd_mlp(x, layer_fwd_mix(x, lw, prm, tt), lw)


def _dw(a, b, a_cols, b_cols, relu2, name):
    m_len, n_len = a.shape[1], b.shape[1]
    n_a, n_b = m_len // a_cols, n_len // b_cols
    assert n_a == 1 or n_b == 1
    if n_b == 1:
        return matmul_tn(
            a, b,
            lambda t_: pl.BlockSpec((t_, a_cols), lambda n, t: (t, n)),
            lambda t_: pl.BlockSpec((t_, n_len), lambda n, t: (t, 0)),
            pl.BlockSpec((a_cols, n_len), lambda n, t: (n, 0)), SDS((m_len, n_len), BF16), n_a, relu2, name)
    return matmul_tn(
        a, b,
        lambda t_: pl.BlockSpec((t_, m_len), lambda n, t: (t, 0)),
        lambda t_: pl.BlockSpec((t_, b_cols), lambda n, t: (t, n)),
        pl.BlockSpec((m_len, b_cols), lambda n, t: (0, n)), SDS((m_len, n_len), BF16), n_b, relu2, name)


def layer_bwd_mlp(dx2, dx2b, lw, saved):
    _, _, _, _, y, x1, h2, u = saved
    du = matmul_nt_act(dx2b, lw["wdn"], u, "mlp_bwd_du")
    g_wdn = _dw(u, dx2b, 1024, D, True, "dw_down")
    dx1, dx1b, g_nw2 = matmul_nt_norm_bwd(du, lw["wup"], x1, lw["nw2"], dx2, "mlp_bwd_dx")
    cb = DFF // N_DEV
    g_wup = matmul_tn(
        h2, du,
        lambda t_: pl.BlockSpec((t_, D), lambda n, t: (t, 0)),
        lambda t_: pl.BlockSpec((t_, 2 * cb), lambda n, t: (t, n)),
        pl.BlockSpec((2, D, cb), lambda n, t: (n, 0, 0)), SDS((N_DEV, D, cb), BF16), N_DEV // 2, False, "dw_up")
    dy = matmul_nt_act(dx1b, lw["wout"], None, "out_bwd_dy")
    g_wout = _dw(y, dx1b, 1024, D, False, "dw_out")
    return dx1, dx1b, dy, {"wout": g_wout, "wup": g_wup, "wdn": g_wdn, "nw2": g_nw2[0]}


def layer_bwd_mix(dx1, dy, lw, prm, saved, tt):
    x, h1, proj, st = saved[:4]
    dproj, gscw, gcwx, gcwbc, gvec, gdt = mixer_bwd(proj, dy, st[0], st[1], prm, tt)
    dx0, dx0b, g_nw1 = matmul_nt_norm_bwd(dproj, lw["win"], x, lw["nw1"], dx1, "in_bwd_dx")
    g_win = _dw(h1, dproj, D, TN_IN, False, "dw_in")
    grads = {
        "win": g_win, "scw": gscw[0:3], "cw": jnp.concatenate([gcwx[0:4], gcwbc[0:4]], axis=1),
        "cb": jnp.concatenate([gvec[2], gcwbc[4]], axis=0),
        "dtb": gdt[0, :NHEAD], "alog": gdt[1, :NHEAD], "dsk": gdt[2, :NHEAD],
        "nrm": gvec[0], "nw1": g_nw1[0],
    }
    return dx0, dx0b, grads


def layer_bwd(dx2, dx2b, lw, prm, saved, tt):
    dx1, dx1b, dy, g_mlp = layer_bwd_mlp(dx2, dx2b, lw, saved)
    dx0, dx0b, g_mix = layer_bwd_mix(dx1, dy, lw, prm, saved, tt)
    return dx0, dx0b, {**g_mlp, **g_mix}


def layer_params(win, scw, cw, nw1, nw2, conv_b, dt_bias, a_log, d_skip, ssd_norm_w, eh, eht):
    def rows8(a):
        return jnp.pad(a, ((0, 8 - a.shape[0]), (0, 0)))

    def lanes128(a):
        return jnp.pad(a, (0, LANES - a.shape[0]))[None, :]

    lw = {"win": win, "nw1": nw1[None, :], "nw2": nw2[None, :]}
    prm = {"scw": rows8(scw), "cwx": rows8(cw[:, :D]), "cwbc": rows8(cw[:, D:]),
           "cbx": conv_b[None, :D], "cbbc": conv_b[None, D:],
           "dtb": lanes128(dt_bias), "alog": lanes128(a_log),
           "dskx": jnp.repeat(d_skip, HDIM)[None, :], "nrm": ssd_norm_w[None, :], "eh": eh, "eht": eht}
    return lw, prm


def _flip(v, bit):
    return 1 - v if bit else v


def all_gather(arrs, name):
    n = len(arrs)

    def body(*refs):
        ins, outs = refs[:n], refs[n:2 * n]
        send_sems, recv_sems, local_sems = refs[2 * n:]
        x, y, c = lax.axis_index("x"), lax.axis_index("y"), lax.axis_index("c")
        sibling = (x, y, 1 - c)
        chips = [(1 - x, y), (x, 1 - y), (1 - x, 1 - y)]

        def idx(px, py, pc):
            return 4 * px + 2 * py + pc

        def copy(a, k, block, to, src=None):
            dst = outs[a].at[idx(*block)]
            return pltpu.make_async_remote_copy(
                src_ref=dst if src is None else src, dst_ref=dst,
                send_sem=send_sems.at[a, k], recv_sem=recv_sems.at[a, k], device_id=to, device_id_type=MESH)

        me = (x, y, c)
        mine = [pltpu.make_async_copy(ins[a], outs[a].at[idx(*me)], local_sems.at[a]) for a in range(n)]
        for cp in mine:
            cp.start()
        first = []
        for a in range(n):
            first.append(copy(a, 0, me, sibling, src=ins[a]))
            first += [copy(a, 1 + j, me, (*chip, c), src=ins[a]) for j, chip in enumerate(chips)]
        for cp in first:
            cp.start()
        passed = []
        for j, chip in enumerate(chips):
            for a in range(n):
                copy(a, 1 + j, (*chip, c), me).wait_recv()
                cp = copy(a, 4 + j, (*chip, c), sibling)
                cp.start()
                passed.append(cp)
        for a in range(n):
            copy(a, 0, sibling, me).wait_recv()
            for j, chip in enumerate(chips):
                copy(a, 4 + j, (*chip, 1 - c), me).wait_recv()
        for cp in first + passed:
            cp.wait_send()
        for cp in mine:
            cp.wait()

    any_spec = pl.BlockSpec(memory_space=pl.ANY)
    return pl.pallas_call(
        body, in_specs=[any_spec] * n, out_specs=[any_spec] * n,
        out_shape=[SDS((N_DEV,) + a.shape, a.dtype) for a in arrs],
        scratch_shapes=[pltpu.SemaphoreType.DMA((n, 7)), pltpu.SemaphoreType.DMA((n, 7)),
                        pltpu.SemaphoreType.DMA((n,))],
        name=name)(*arrs)


HBM_SPEC = pl.BlockSpec(memory_space=pltpu.HBM)
SEM_SPEC = pl.BlockSpec(memory_space=pltpu.SEMAPHORE)
SIDE_EFFECT = pltpu.SideEffectType.DATAFLOW_SIDE_EFFECTING
N_PEER = N_DEV - 1


def _peer(mask):
    x, y, c = lax.axis_index("x"), lax.axis_index("y"), lax.axis_index("c")
    return _flip(x, mask & 4), _flip(y, mask & 2), _flip(c, mask & 1)


ALL_PEERS = tuple(range(1, N_DEV))
SIBLING_AND_CHIPS = (1, 2, 4, 6)


def exchange_start(srcs, per_peer, name, after=None, masks=ALL_PEERS):
    n = len(srcs)
    npeer = len(masks)
    lands = [SDS((N_DEV,) + (a.shape[1:] if per_peer else a.shape), a.dtype) for a in srcs]
    n_in = 2 * n + (after is not None)

    def body(*refs):
        src_refs, land_refs = refs[:n], refs[n:2 * n]
        send_sems, recv_sems = refs[n_in], refs[n_in + 1]
        token = refs[-1]
        x, y, c = lax.axis_index("x"), lax.axis_index("y"), lax.axis_index("c")
        me = 4 * x + 2 * y + c
        for a in range(n):
            for k, mask in enumerate(masks):
                px, py, pc = _peer(mask)
                part = src_refs[a].at[4 * px + 2 * py + pc] if per_peer else src_refs[a]
                pltpu.make_async_remote_copy(
                    src_ref=part, dst_ref=land_refs[a].at[me], send_sem=send_sems.at[a * npeer + k],
                    recv_sem=recv_sems.at[a * npeer + k], device_id=(px, py, pc), device_id_type=MESH).start()
        token[...] = jnp.zeros_like(token)

    out = pl.pallas_call(
        body, name=name,
        out_shape=(pltpu.SemaphoreType.DMA((n * npeer,)), pltpu.SemaphoreType.DMA((n * npeer,)),
                   *[pltpu.HBM(a.shape, a.dtype) for a in srcs], *[pltpu.HBM(l.shape, l.dtype) for l in lands],
                   SDS((8, LANES), F32)),
        in_specs=(HBM_SPEC,) * (2 * n) + ((pl.BlockSpec(memory_space=pl.ANY),) if after is not None else ()),
        out_specs=(SEM_SPEC, SEM_SPEC) + (HBM_SPEC,) * (2 * n) + (pl.BlockSpec(memory_space=pltpu.VMEM),),
        input_output_aliases={k: 2 + k for k in range(2 * n)},
        compiler_params=pltpu.CompilerParams(has_side_effects=SIDE_EFFECT),
    )(*[pltpu.with_memory_space_constraint(a, pltpu.HBM) for a in srcs],
      *[pltpu.with_memory_space_constraint(lax.empty(l.shape, l.dtype), pltpu.HBM) for l in lands],
      *([after] if after is not None else []))
    return out[0], out[1], list(out[2:2 + n]), list(out[2 + n:2 + 2 * n]), out[-1]


def exchange_wait(started, after, per_peer, name, masks=ALL_PEERS):
    send_sems, recv_sems, srcs, lands, _ = started
    n = len(srcs)
    npeer = len(masks)

    def body(*refs):
        src_refs, land_refs = refs[:n], refs[n:2 * n]
        send_sems, recv_sems = refs[2 * n], refs[2 * n + 1]
        for k, mask in enumerate(masks):
            for a in range(n):
                copy = pltpu.make_async_remote_copy(
                    src_ref=src_refs[a].at[0] if per_peer else src_refs[a], dst_ref=land_refs[a].at[0],
                    send_sem=send_sems.at[a * npeer + k], recv_sem=recv_sems.at[a * npeer + k],
                    device_id=_peer(mask), device_id_type=MESH)
                copy.wait_send()
                copy.wait_recv()

    out = pl.pallas_call(
        body, name=name,
        out_shape=tuple(pltpu.HBM(a.shape, a.dtype) for a in srcs + lands),
        in_specs=(HBM_SPEC,) * (2 * n) + (SEM_SPEC, SEM_SPEC, pl.BlockSpec(memory_space=pl.ANY)),
        out_specs=(HBM_SPEC,) * (2 * n), input_output_aliases={k: k for k in range(2 * n)},
        compiler_params=pltpu.CompilerParams(has_side_effects=SIDE_EFFECT),
    )(*srcs, *lands, send_sems, recv_sems, after)
    return list(out[:n]), list(out[n:])


def relay_to_sibling(lands, name):
    n = len(lands)
    chips = (2, 4, 6)

    def body(*refs):
        land_refs = refs[n:2 * n]
        send_sems, recv_sems = refs[2 * n], refs[2 * n + 1]
        x, y, c = lax.axis_index("x"), lax.axis_index("y"), lax.axis_index("c")
        copies = []
        for a in range(n):
            for k, mask in enumerate(chips):
                px, py, _ = _peer(mask)
                block = land_refs[a].at[4 * px + 2 * py + c]
                cp = pltpu.make_async_remote_copy(
                    src_ref=block, dst_ref=block, send_sem=send_sems.at[a * 3 + k], recv_sem=recv_sems.at[a * 3 + k],
                    device_id=(x, y, 1 - c), device_id_type=MESH)
                cp.start()
                copies.append((cp, a, k, land_refs[a].at[4 * px + 2 * py + 1 - c]))
        for cp, a, k, arriving in copies:
            cp.wait_send()
            pltpu.make_async_remote_copy(
                src_ref=arriving, dst_ref=arriving, send_sem=send_sems.at[a * 3 + k], recv_sem=recv_sems.at[a * 3 + k],
                device_id=(x, y, 1 - c), device_id_type=MESH).wait_recv()

    any_spec = pl.BlockSpec(memory_space=pl.ANY)
    return list(pl.pallas_call(
        body, in_specs=[any_spec] * n, out_specs=[any_spec] * n,
        out_shape=[SDS(a.shape, a.dtype) for a in lands],
        input_output_aliases={k: k for k in range(n)},
        scratch_shapes=[pltpu.SemaphoreType.DMA((n * 3,)), pltpu.SemaphoreType.DMA((n * 3,))],
        name=name)(*lands))


IN_SHARD = NIN // N_DEV
SLOT_W = 768


def _slot_window(j):
    return (IN_SHARD * j // LANES) * LANES, -(-(IN_SHARD * (j + 1)) // LANES) * LANES


def _placement(j):
    a, b = _slot_window(j)
    r = lax.broadcasted_iota(jnp.int32, (SLOT_W, b - a), 0)
    c = lax.broadcasted_iota(jnp.int32, (SLOT_W, b - a), 1)
    return jnp.where(jnp.logical_and(c == r + (IN_SHARD * j - a), r < IN_SHARD), 1.0, 0.0).astype(BF16)


WINDOW_MAX = max(b - a for a, b in map(_slot_window, range(N_DEV)))


def _fill_placements(place):
    @pl.when(pl.program_id(0) == 0)
    def _():
        for j in range(N_DEV):
            a, b = _slot_window(j)
            place[j, :, 0:b - a] = _placement(j)


def assemble_w_in(land):
    tm = 256

    def body(l_ref, o_ref, acc, place):
        _fill_placements(place)
        acc[...] = jnp.zeros_like(acc)
        for j in range(N_DEV):
            a, b = _slot_window(j)
            acc[:, a:b] += _nn(l_ref[j], place[j, :, 0:b - a])
        o_ref[...] = acc[...].astype(BF16)

    return pl.pallas_call(
        body, grid=(D // tm,),
        in_specs=[pl.BlockSpec((N_DEV, tm, SLOT_W), lambda i: (0, i, 0))],
        out_specs=pl.BlockSpec((tm, NINP), lambda i: (i, 0)),
        out_shape=SDS((D, NINP), BF16),
        scratch_shapes=[pltpu.VMEM((tm, NINP), F32), pltpu.VMEM((N_DEV, SLOT_W, WINDOW_MAX), BF16)],
        compiler_params=_cparams(("arbitrary",)), name="assemble_w_in")(land)


def scatter_w_in(dw):
    tm = 256

    def body(d_ref, o_ref, place):
        _fill_placements(place)
        for j in range(N_DEV):
            a, b = _slot_window(j)
            o_ref[j] = _nt(d_ref[:, a:b], place[j, :, 0:b - a]).astype(BF16)

    return pl.pallas_call(
        body, grid=(D // tm,),
        in_specs=[pl.BlockSpec((tm, NINP), lambda i: (i, 0))],
        out_specs=pl.BlockSpec((N_DEV, tm, SLOT_W), lambda i: (0, i, 0)),
        out_shape=SDS((N_DEV, D, SLOT_W), BF16),
        scratch_shapes=[pltpu.VMEM((N_DEV, SLOT_W, WINDOW_MAX), BF16)],
        compiler_params=_cparams(("arbitrary",)), name="scatter_w_in")(dw)


def _adamw_math(g, w_ref, m_ref, v_ref, g_ref, d_ref, nm_ref, nv_ref):
    mn = ADAM_B1 * m_ref[...] + (1.0 - ADAM_B1) * g
    vn = ADAM_B2 * v_ref[...] + (1.0 - ADAM_B2) * jnp.square(g)
    m_hat = mn / (1.0 - ADAM_B1 ** ADAM_STEP)
    v_hat = vn / (1.0 - ADAM_B2 ** ADAM_STEP)
    g_ref[...] = g
    d_ref[...] = -ADAM_LR * (m_hat / (jnp.sqrt(v_hat) + ADAM_EPS) + ADAM_WD * w_ref[...])
    nm_ref[...] = mn
    nv_ref[...] = vn


def adamw_layers(w, slots, m, v, name):
    depth, r_len, c_len = w.shape
    cs = slots[0].shape[2]
    br = min(128, r_len)
    assert r_len % br == 0

    def body(w_ref, *rest):
        s_refs, (m_ref, v_ref, g_ref, d_ref, nm_ref, nv_ref) = rest[:depth], rest[depth:]
        layer = pl.program_id(0)
        for k in range(depth):
            @pl.when(layer == k)
            def _(k=k):
                g = s_refs[k][0, :, 0:c_len].astype(F32)
                for j in range(1, N_DEV):
                    g = g + s_refs[k][j, :, 0:c_len].astype(F32)
                _adamw_math(g, w_ref, m_ref, v_ref, g_ref, d_ref, nm_ref, nv_ref)

    spec = pl.BlockSpec((None, br, c_len), lambda l, i: (l, i, 0))
    s_specs = [pl.BlockSpec((N_DEV, br, cs), lambda l, i, k=k: (0, jnp.where(l == k, i, 0), 0))
               for k in range(depth)]
    return pl.pallas_call(
        body, grid=(depth, r_len // br),
        in_specs=[spec] + s_specs + [spec, spec],
        out_specs=[spec] * 4, out_shape=[SDS(w.shape, F32)] * 4,
        compiler_params=_cparams(("arbitrary", "arbitrary")), name=name)(w, *slots, m, v)


def adamw(w, slots, m, v, name):
    r_len, c_len = w.shape
    br = r_len if r_len <= 512 else 512
    assert r_len % br == 0

    def body(w_ref, s_ref, m_ref, v_ref, g_ref, d_ref, nm_ref, nv_ref):
        g = s_ref[0].astype(F32)
        for k in range(1, N_DEV):
            g = g + s_ref[k].astype(F32)
        _adamw_math(g, w_ref, m_ref, v_ref, g_ref, d_ref, nm_ref, nv_ref)

    spec = pl.BlockSpec((br, c_len), lambda i: (i, 0))
    return pl.pallas_call(
        body, grid=(r_len // br,),
        in_specs=[spec, pl.BlockSpec((N_DEV, br, c_len), lambda i: (0, i, 0)), spec, spec],
        out_specs=[spec] * 4, out_shape=[SDS((r_len, c_len), F32)] * 4,
        compiler_params=_cparams(("parallel",)), name=name)(w, slots, m, v)


def _adamw_nd(w, slots, m, v, name):
    shp = w.shape
    r = int(np.prod(shp[:-1]))
    outs = adamw(w.reshape(r, shp[-1]), slots.reshape(N_DEV, r, shp[-1]), m.reshape(r, shp[-1]),
                 v.reshape(r, shp[-1]), name)
    return [o.reshape(shp) for o in outs]


SMALL = [("norm_mix_w", DEPTH * D), ("ssd_conv_b", DEPTH * XBC), ("dt_bias", DEPTH * NHEAD),
         ("a_log", DEPTH * NHEAD), ("d_skip", DEPTH * NHEAD), ("ssd_norm_w", DEPTH * D),
         ("norm_mlp_w", DEPTH * D), ("final_norm_w", D)]
SMALL_LEN = sum(s for _, s in SMALL)
SMALL_ROWS = -(-SMALL_LEN // LANES)


def _pack_small(parts):
    flat = jnp.concatenate([parts[k].reshape(-1) for k, _ in SMALL])
    return jnp.pad(flat, (0, SMALL_ROWS * LANES - SMALL_LEN)).reshape(SMALL_ROWS, LANES)


def _unpack_small(packed, shapes):
    flat = packed.reshape(-1)
    out, off = {}, 0
    for k, s in SMALL:
        out[k] = flat[off:off + s].reshape(shapes[k])
        off += s
    return out


def kernel(x, norm_mix_w, w_in, short_conv_w, ssd_conv_w, ssd_conv_b, dt_bias, a_log, d_skip, ssd_norm_w, w_out, norm_mlp_w, w_up, w_down, final_norm_w, loss_target, m_norm_mix_w, m_w_in, m_short_conv_w, m_ssd_conv_w, m_ssd_conv_b, m_dt_bias, m_a_log, m_d_skip, m_ssd_norm_w, m_w_out, m_norm_mlp_w, m_w_up, m_w_down, m_final_norm_w, v_norm_mix_w, v_w_in, v_short_conv_w, v_ssd_conv_w, v_ssd_conv_b, v_dt_bias, v_a_log, v_d_skip, v_ssd_norm_w, v_w_out, v_norm_mlp_w, v_w_up, v_w_down, v_final_norm_w):
    xs = x[0]
    t_len = xs.shape[0]
    tt = min(256, t_len)
    eh, eht = _head_matrices()
    me = 4 * lax.axis_index("x") + 2 * lax.axis_index("y") + lax.axis_index("c")

    def start_weights(i, after):
        first = exchange_start(
            [jnp.pad(w_in[i].astype(BF16), ((0, 0), (0, SLOT_W - IN_SHARD))), short_conv_w[i], ssd_conv_w[i]],
            False, "w_in_start_%d" % i, after, SIBLING_AND_CHIPS)
        rest = exchange_start([w_out[i].astype(BF16), w_up[i].astype(BF16), w_down[i].astype(BF16)], False,
                              "w_rest_start_%d" % i, first[4] if after is None else after, SIBLING_AND_CHIPS)
        return first, rest

    def fill_own(srcs, lands, per_peer):
        own = [lax.dynamic_index_in_dim(s_, me, 0, keepdims=False) for s_ in srcs] if per_peer else srcs
        return [lax.dynamic_update_index_in_dim(l_, o_, me, 0) for l_, o_ in zip(lands, own)]

    def finish_weights(started, after, name):
        srcs, lands = exchange_wait(started, after, False, name + "_wait", SIBLING_AND_CHIPS)
        return fill_own(srcs, relay_to_sibling(lands, name + "_relay"), False)

    act = xs
    saved, layers = [], []
    first, rest = start_weights(0, None)
    token = first[4][0, 0] + rest[4][0, 0]
    for i in range(DEPTH):
        g_in, g_sc, g_cw = finish_weights(first, act, "w_in_%d" % i)
        lw, prm = layer_params(
            assemble_w_in(g_in), g_sc.transpose(1, 0, 2).reshape(3, D), g_cw.transpose(1, 0, 2).reshape(4, XBC),
            norm_mix_w[i], norm_mlp_w[i], ssd_conv_b[i], dt_bias[i], a_log[i], d_skip[i], ssd_norm_w[i], eh, eht)
        lw["nw1"] = lw["nw1"] + token
        mixed = layer_fwd_mix(act, lw, prm, tt)
        g_out, g_up, g_dn = finish_weights(rest, mixed[3], "w_rest_%d" % i)
        lw.update(wout=g_out.reshape(MIX, D), wup=g_up, wdn=g_dn.reshape(DFF, D))
        if i + 1 < DEPTH:
            first, rest = start_weights(i + 1, g_dn)
            token = first[4][0, 0] + rest[4][0, 0]
            lw["nw2"] = lw["nw2"] + token
        layers.append((lw, prm))
        act, sv = layer_fwd_mlp(act, mixed, lw)
        saved.append(sv)
    loss_acc, dx, dxb, g_fw = loss_head(act, final_norm_w[None, :], loss_target[0])

    grads = [None] * DEPTH
    sent_rest, sent_in = [None] * DEPTH, [None] * DEPTH
    token = None
    for i in reversed(range(DEPTH)):
        lw, prm = layers[i]
        if token is not None:
            lw = dict(lw, nw2=lw["nw2"] + token)
        dx1, _, dy, g_mlp = layer_bwd_mlp(dx, dxb, lw, saved[i])
        sent_rest[i] = exchange_start(
            [g_mlp["wout"].reshape(N_DEV, MIX // N_DEV, D), g_mlp["wup"], g_mlp["wdn"].reshape(N_DEV, DFF // N_DEV, D)],
            True, "g_rest_start_%d" % i)
        dx, dxb, g_mix = layer_bwd_mix(dx1, dy, lw, dict(prm, nrm=prm["nrm"] + sent_rest[i][4][0, 0]), saved[i], tt)
        grads[i] = {**g_mlp, **g_mix}
        if i > 0:
            sent_in[i] = exchange_start([scatter_w_in(g_mix["win"])], True, "g_in_start_%d" % i)
            token = sent_in[i][4][0, 0]

    def stack(k):
        return jnp.stack([g[k] for g in grads])

    small = _pack_small({"norm_mix_w": stack("nw1"), "ssd_conv_b": stack("cb"), "dt_bias": stack("dtb"),
                         "a_log": stack("alog"), "d_skip": stack("dsk"), "ssd_norm_w": stack("nrm"),
                         "norm_mlp_w": stack("nw2"), "final_norm_w": g_fw[0]})
    r_small, r_sc, r_cw = all_gather([small, stack("scw"), stack("cw")], "gather_small_grads")
    r_sc = lax.dynamic_slice_in_dim(r_sc, me * (D // N_DEV), D // N_DEV, axis=3)
    r_cw = lax.dynamic_slice_in_dim(r_cw, me * (XBC // N_DEV), XBC // N_DEV, axis=3)
    sent_in[0] = exchange_start([scatter_w_in(grads[0]["win"])], True, "g_in_start_0", after=r_small)

    after = sent_in[0][4]
    recv = [fill_own(*exchange_wait(sent_rest[i], after, True, "g_rest_wait_%d" % i), True) for i in range(DEPTH)]
    res = {}
    res["w_out"] = adamw_layers(w_out, [r[0] for r in recv], m_w_out, v_w_out, "adamw_w_out")
    res["w_up"] = adamw_layers(w_up, [r[1] for r in recv], m_w_up, v_w_up, "adamw_w_up")
    res["w_down"] = adamw_layers(w_down, [r[2] for r in recv], m_w_down, v_w_down, "adamw_w_down")
    after = res["w_down"][1]
    recv_in = [fill_own(*exchange_wait(sent_in[i], after, True, "g_in_wait_%d" % i), True)[0] for i in range(DEPTH)]
    res["w_in"] = adamw_layers(w_in, recv_in, m_w_in, v_w_in, "adamw_w_in")
    res["short_conv_w"] = _adamw_nd(short_conv_w, r_sc, m_short_conv_w, v_short_conv_w, "adamw_short_conv")
    res["ssd_conv_w"] = _adamw_nd(ssd_conv_w, r_cw, m_ssd_conv_w, v_ssd_conv_w, "adamw_ssd_conv")
    small_w = {"norm_mix_w": norm_mix_w, "ssd_conv_b": ssd_conv_b, "dt_bias": dt_bias, "a_log": a_log,
               "d_skip": d_skip, "ssd_norm_w": ssd_norm_w, "norm_mlp_w": norm_mlp_w, "final_norm_w": final_norm_w}
    small_m = {"norm_mix_w": m_norm_mix_w, "ssd_conv_b": m_ssd_conv_b, "dt_bias": m_dt_bias, "a_log": m_a_log,
               "d_skip": m_d_skip, "ssd_norm_w": m_ssd_norm_w, "norm_mlp_w": m_norm_mlp_w,
               "final_norm_w": m_final_norm_w}
    small_v = {"norm_mix_w": v_norm_mix_w, "ssd_conv_b": v_ssd_conv_b, "dt_bias": v_dt_bias, "a_log": v_a_log,
               "d_skip": v_d_skip, "ssd_norm_w": v_ssd_norm_w, "norm_mlp_w": v_norm_mlp_w,
               "final_norm_w": v_final_norm_w}
    shapes = {k: a.shape for k, a in small_w.items()}
    packed = adamw(_pack_small(small_w), r_small, _pack_small(small_m), _pack_small(small_v), "adamw_small")
    unpacked = [_unpack_small(p, shapes) for p in packed]
    for k in small_w:
        res[k] = [u[k] for u in unpacked]

    loss = lax.psum(loss_acc[0, 0], ("x", "y", "c"))
    order = ["norm_mix_w", "w_in", "short_conv_w", "ssd_conv_w", "ssd_conv_b", "dt_bias", "a_log", "d_skip",
             "ssd_norm_w", "w_out", "norm_mlp_w", "w_up", "w_down", "final_norm_w"]
    out = [loss, dx[None]]
    for part in range(4):
        out += [res[k][part] for k in order]
    return tuple(out)
```

```python
import functools

import numpy as np
import jax
import jax.numpy as jnp
from jax import lax
from jax.experimental import pallas as pl
from jax.experimental.pallas import tpu as pltpu

F32 = jnp.float32
BF16 = jnp.bfloat16
SDS = jax.ShapeDtypeStruct

N_DEV = 8
DEPTH = 4
D = 1024
NIN = 5648
NINP = 5760
DFF = 4096
MIX = 2048
NHEAD = 16
HDIM = 64
NSTATE = 128
CHUNK = 64
XBC = 1536
EPS = 1e-5
LANES = 128
NEG_BIG = -1e30

C_UB, C_UC, C_UH, C_Z, C_XS, C_BC, C_DT = 0, 1024, 2048, 3072, 4096, 5120, 5632
A_CV, A_YS, A_PX, A_PBC, AUX_W = 0, 1024, 2048, 3072, 3584

ADAM_LR = 0.001
ADAM_B1 = 0.9
ADAM_B2 = 0.999
ADAM_EPS = 1e-08
ADAM_WD = 0.01
ADAM_STEP = 10

VMEM_LIMIT = 56 * 1024 * 1024
MESH = pl.DeviceIdType.MESH


def _cparams(sem):
    return pltpu.CompilerParams(dimension_semantics=sem, vmem_limit_bytes=VMEM_LIMIT)


def _nt(a, b):
    return lax.dot_general(a, b, (((1,), (1,)), ((), ())), preferred_element_type=F32)


def _tn(a, b):
    return lax.dot_general(a, b, (((0,), (0,)), ((), ())), preferred_element_type=F32)


def _nn(a, b):
    return jnp.dot(a, b, preferred_element_type=F32)


def _sigmoid(v):
    return 0.5 * jnp.tanh(0.5 * v) + 0.5


def _split3(v):
    v1 = v.astype(BF16)
    r1 = v - v1.astype(F32)
    v2 = r1.astype(BF16)
    v3 = (r1 - v2.astype(F32)).astype(BF16)
    return v1, v2, v3


def _expand(v, eh):
    v1, v2, v3 = _split3(v)
    return _nn(v1, eh) + _nn(v2, eh) + _nn(v3, eh)


def _head_reduce(v, eht):
    v1 = v.astype(BF16)
    v2 = (v - v1.astype(F32)).astype(BF16)
    return _nn(v1, eht) + _nn(v2, eht)


def _head_matrices():
    eh = np.zeros((LANES, D), np.float32)
    for h in range(NHEAD):
        eh[h, h * HDIM:(h + 1) * HDIM] = 1.0
    return jnp.asarray(eh, BF16), jnp.asarray(eh.T.copy(), BF16)


def _resident(shape):
    return pl.BlockSpec(shape, lambda *_: (0,) * len(shape), pipeline_mode=pl.Buffered(1))


def _col_chunks(n, step):
    return [(c, min(c + step, n)) for c in range(0, n, step)]


def norm_matmul(x, nw, w, name):
    t_len = x.shape[0]
    blocked = w.ndim == 3
    n_len = w.shape[0] * w.shape[2] if blocked else w.shape[1]
    tm = min(512, t_len)
    chunks = _col_chunks(n_len, n_len // N_DEV if blocked else 1536)

    def body(x_ref, nw_ref, w_ref, o_ref, h_ref):
        xv = x_ref[...]
        r = lax.rsqrt(jnp.mean(xv * xv, axis=-1, keepdims=True) + EPS)
        hv = (xv * r * nw_ref[...]).astype(BF16)
        h_ref[...] = hv
        for j, (c0, c1) in enumerate(chunks):
            wj = w_ref[j] if blocked else w_ref[:, c0:c1]
            o_ref[:, c0:c1] = _nn(hv, wj).astype(o_ref.dtype)

    return pl.pallas_call(
        body, grid=(t_len // tm,),
        in_specs=[pl.BlockSpec((tm, D), lambda i: (i, 0)), _resident((1, D)), _resident(w.shape)],
        out_specs=[pl.BlockSpec((tm, n_len), lambda i: (i, 0)),
                   pl.BlockSpec((tm, D), lambda i: (i, 0))],
        out_shape=[SDS((t_len, n_len), BF16), SDS((t_len, D), BF16)],
        compiler_params=_cparams(("parallel",)), name=name)(x, nw, w)


def matmul_residual(a, w, res, relu2, name):
    t_len, k_len = a.shape
    tm = min(512, t_len)

    def body(a_ref, w_ref, res_ref, o_ref):
        av = a_ref[...]
        if relu2:
            af = jnp.maximum(av.astype(F32), 0.0)
            av = (af * af).astype(BF16)
        o_ref[...] = res_ref[...] + _nn(av, w_ref[...])

    return pl.pallas_call(
        body, grid=(t_len // tm,),
        in_specs=[pl.BlockSpec((tm, k_len), lambda i: (i, 0)),
                  _resident((k_len, D)),
                  pl.BlockSpec((tm, D), lambda i: (i, 0))],
        out_specs=pl.BlockSpec((tm, D), lambda i: (i, 0)),
        out_shape=SDS((t_len, D), F32),
        compiler_params=_cparams(("parallel",)), name=name)(a, w, res)


def matmul_nt_act(dy, w, u, name):
    t_len = dy.shape[0]
    n_len = w.shape[0]
    tm = min(512, t_len)
    chunks = _col_chunks(n_len, 1024)

    def body(dy_ref, w_ref, *rest):
        if u is None:
            (o_ref,) = rest
        else:
            u_ref, o_ref = rest
        dyv = dy_ref[...]
        for c0, c1 in chunks:
            p = _nt(dyv, w_ref[c0:c1, :])
            if u is not None:
                p = p * (2.0 * jnp.maximum(u_ref[:, c0:c1].astype(F32), 0.0))
            o_ref[:, c0:c1] = p.astype(o_ref.dtype)

    in_specs = [pl.BlockSpec((tm, D), lambda i: (i, 0)), _resident((n_len, D))]
    args = [dy, w]
    if u is not None:
        in_specs.append(pl.BlockSpec((tm, n_len), lambda i: (i, 0)))
        args.append(u)
    return pl.pallas_call(
        body, grid=(t_len // tm,),
        in_specs=in_specs,
        out_specs=pl.BlockSpec((tm, n_len), lambda i: (i, 0)),
        out_shape=SDS((t_len, n_len), BF16),
        compiler_params=_cparams(("parallel",)), name=name)(*args)


def matmul_tn(a, b, a_spec, b_spec, o_spec, o_shape, n_out, relu2, name, tt_max=2048):
    t_len = a.shape[0]
    tt = min(tt_max, t_len)
    nt = t_len // tt

    def body(a_ref, b_ref, o_ref, acc):
        t = pl.program_id(1)
        av = a_ref[...]
        if relu2:
            af = jnp.maximum(av.astype(F32), 0.0)
            av = (af * af).astype(BF16)
        p = _tn(av, b_ref[...])

        @pl.when(t == 0)
        def _():
            acc[...] = p

        @pl.when(t > 0)
        def _():
            acc[...] += p

        @pl.when(t == nt - 1)
        def _():
            if len(blk) == 3:
                for j in range(blk[0]):
                    o_ref[j] = acc[:, j * blk[2]:(j + 1) * blk[2]].astype(o_ref.dtype)
            else:
                o_ref[...] = acc[...].astype(o_ref.dtype)

    blk = tuple(o_spec.block_shape)
    acc_shape = (blk[1], blk[0] * blk[2]) if len(blk) == 3 else blk
    return pl.pallas_call(
        body, grid=(n_out, nt),
        in_specs=[a_spec(tt), b_spec(tt)],
        out_specs=o_spec, out_shape=o_shape,
        scratch_shapes=[pltpu.VMEM(acc_shape, F32)],
        compiler_params=_cparams(("parallel", "arbitrary")), name=name)(a, b)


def matmul_nt_norm_bwd(dy, w, x, nw, dres, name):
    t_len = x.shape[0]
    blocked = w.ndim == 3
    k_len = dy.shape[1]
    kb = k_len // N_DEV
    tm = min(512, t_len)

    def body(dy_ref, w_ref, x_ref, nw_ref, dres_ref, dx_ref, dxb_ref, dnw_ref):
        @pl.when(pl.program_id(0) == 0)
        def _():
            dnw_ref[...] = jnp.zeros_like(dnw_ref)

        if blocked:
            dh = _nt(dy_ref[:, 0:kb], w_ref[0])
            for j in range(1, N_DEV):
                dh = dh + _nt(dy_ref[:, j * kb:(j + 1) * kb], w_ref[j])
        else:
            dh = _nt(dy_ref[...], w_ref[...])
        xv = x_ref[...]
        r = lax.rsqrt(jnp.mean(xv * xv, axis=-1, keepdims=True) + EPS)
        xh = xv * r
        dnw_ref[0:1, :] += jnp.sum(dh * xh, axis=0, keepdims=True)
        g = dh * nw_ref[...]
        dx = dres_ref[...] + r * (g - xh * jnp.mean(g * xh, axis=-1, keepdims=True))
        dx_ref[...] = dx
        dxb_ref[...] = dx.astype(BF16)

    return pl.pallas_call(
        body, grid=(t_len // tm,),
        in_specs=[pl.BlockSpec((tm, k_len), lambda i: (i, 0)),
                  _resident(w.shape),
                  pl.BlockSpec((tm, D), lambda i: (i, 0)),
                  _resident((1, D)),
                  pl.BlockSpec((tm, D), lambda i: (i, 0))],
        out_specs=[pl.BlockSpec((tm, D), lambda i: (i, 0)),
                   pl.BlockSpec((tm, D), lambda i: (i, 0)),
                   pl.BlockSpec((8, D), lambda i: (0, 0))],
        out_shape=[SDS((t_len, D), F32), SDS((t_len, D), BF16), SDS((8, D), F32)],
        compiler_params=_cparams(("arbitrary",)), name=name)(dy, w, x, nw, dres)


def loss_head(x, fw, tgt):
    t_len = x.shape[0]
    tm = min(512, t_len)

    def body(x_ref, fw_ref, t_ref, loss_ref, dx_ref, dxb_ref, dfw_ref):
        @pl.when(pl.program_id(0) == 0)
        def _():
            loss_ref[...] = jnp.zeros_like(loss_ref)
            dfw_ref[...] = jnp.zeros_like(dfw_ref)
        xv = x_ref[...]
        r = lax.rsqrt(jnp.mean(xv * xv, axis=-1, keepdims=True) + EPS)
        xh = xv * r
        w = fw_ref[...]
        e = xh * w - t_ref[...]
        row = jnp.sum(e * e, axis=-1, keepdims=True) * (1.0 / D)
        loss_ref[...] += 0.5 * jnp.sum(row, axis=0, keepdims=True)
        dyf = e * (1.0 / D)
        dfw_ref[0:1, :] += jnp.sum(dyf * xh, axis=0, keepdims=True)
        g = dyf * w
        dx = r * (g - xh * jnp.mean(g * xh, axis=-1, keepdims=True))
        dx_ref[...] = dx
        dxb_ref[...] = dx.astype(BF16)

    return pl.pallas_call(
        body, grid=(t_len // tm,),
        in_specs=[pl.BlockSpec((tm, D), lambda i: (i, 0)),
                  pl.BlockSpec((1, D), lambda i: (0, 0)),
                  pl.BlockSpec((tm, D), lambda i: (i, 0))],
        out_specs=[pl.BlockSpec((8, LANES), lambda i: (0, 0)),
                   pl.BlockSpec((tm, D), lambda i: (i, 0)),
                   pl.BlockSpec((tm, D), lambda i: (i, 0)),
                   pl.BlockSpec((8, D), lambda i: (0, 0))],
        out_shape=[SDS((8, LANES), F32), SDS((t_len, D), F32), SDS((t_len, D), BF16), SDS((8, D), F32)],
        compiler_params=_cparams(("arbitrary",)), name="loss_head")(x, fw, tgt)


TAP_SHIFTS = (3, 2, 1)


def _shift_matrix(n, up):
    r = lax.broadcasted_iota(jnp.int32, (n, n), 0)
    c = lax.broadcasted_iota(jnp.int32, (n, n), 1)
    return jnp.concatenate([jnp.where(c == (r + j if up else r - j), 1.0, 0.0).astype(BF16) for j in TAP_SHIFTS],
                           axis=0)


def _shifts_dn(xb, halo, sm, n_shifts):
    n = xb.shape[0]
    first = len(TAP_SHIFTS) - n_shifts
    moved = _nn(sm[first * n:], xb)
    row = lax.broadcasted_iota(jnp.int32, halo.shape, 0)
    outs = []
    for k in range(n_shifts):
        j = TAP_SHIFTS[first + k]
        o = moved[k * n:(k + 1) * n]
        top = jnp.where(row < j, pltpu.roll(halo, j, 0), o[0:8])
        outs.append(jnp.concatenate([top, o[8:]], axis=0))
    return outs


def _shifts_up(xb, nxt, sm, n_shifts):
    n = xb.shape[0]
    first = len(TAP_SHIFTS) - n_shifts
    moved = _nn(sm[first * n:], xb)
    row = lax.broadcasted_iota(jnp.int32, nxt.shape, 0)
    outs = []
    for k in range(n_shifts):
        j = TAP_SHIFTS[first + k]
        o = moved[k * n:(k + 1) * n]
        bot = jnp.where(row >= 8 - j, pltpu.roll(nxt, 8 - j, 0), o[n - 8:n])
        outs.append(jnp.concatenate([o[:n - 8], bot], axis=0))
    return outs


def _conv_fwd(x, xb, halo, w_ref, kw, sm):
    shifted = _shifts_dn(xb, halo, sm, kw - 1)
    acc = w_ref[kw - 1:kw, :] * x
    for k in range(kw - 1):
        acc = acc + w_ref[k:k + 1, :] * shifted[k]
    return acc


def _chunk_cumsum(a, pos):
    for sh in (1, 2, 4, 8, 16, 32):
        a = a + jnp.where(pos >= sh, pltpu.roll(a, sh, 0), 0.0)
    return a


def _chunk_rcumsum(a, pos):
    n = a.shape[0]
    for sh in (1, 2, 4, 8, 16, 32):
        a = a + jnp.where(pos < CHUNK - sh, pltpu.roll(a, n - sh, 0), 0.0)
    return a


def _softplus(v):
    return jnp.maximum(v, 0.0) + jnp.log(1.0 + jnp.exp(-jnp.abs(v)))


def _silu(v):
    return v * _sigmoid(v)


def _dsilu(v):
    s = _sigmoid(v)
    return s * (1.0 + v * (1.0 - s))


def _lane_masks(width=D):
    lane = lax.broadcasted_iota(jnp.int32, (CHUNK, width), 1) & (HDIM - 1)
    row = lax.broadcasted_iota(jnp.int32, (CHUNK, width), 0)
    return lane == row, lane <= row


def _rep_matrix():
    lane = lax.broadcasted_iota(jnp.int32, (CHUNK, 512), 1) & (HDIM - 1)
    row = lax.broadcasted_iota(jnp.int32, (CHUNK, 512), 0)
    return jnp.where(lane == row, 1.0, 0.0).astype(BF16)


def _blockdiag(xp):
    lane = lax.broadcasted_iota(jnp.int32, xp.shape, 1)
    zero = jnp.zeros_like(xp)
    return jnp.concatenate([jnp.where(lane < HDIM, xp, zero), jnp.where(lane >= HDIM, xp, zero)], axis=0)


def _mixer_views(tt):
    r8 = tt // 8

    def main(width, col):
        return pl.BlockSpec((tt, width), lambda i, c=col // width: (i, c))

    def halo(width, col):
        return pl.BlockSpec((8, width), lambda i, c=col // width: (jnp.maximum(i * r8 - 1, 0), c))

    return main, halo


def mixer_fwd(proj, prm, tt):
    t_len = proj.shape[0]
    nblk = t_len // tt
    nc = tt // CHUNK
    main, halo = _mixer_views(tt)

    def body(ub_ref, uc_ref, uh_ref, z_ref, xr_ref, bcr_ref, dtr_ref, uch_ref, uhh_ref, xrh_ref, bcrh_ref,
             scw_ref, cwx_ref, cwbc_ref, cbx_ref, cbbc_ref, dtb_ref, alog_ref, dsk_ref, nrm_ref, eh_ref,
             y_ref, st_ref, aux_ref, hs, xs_s, bc_s, dtx_s, cumx_s, yssd_s):
        i = pl.program_id(0)
        first = i == 0

        @pl.when(first)
        def _():
            hs[...] = jnp.zeros_like(hs)

        keep = jnp.where(first, 0.0, 1.0)
        sm = _shift_matrix(tt, False)
        v = uc_ref[...].astype(F32) * uh_ref[...].astype(F32)
        vh = uch_ref[...].astype(F32) * uhh_ref[...].astype(F32) * keep
        cv = _conv_fwd(v, v.astype(BF16), vh, scw_ref, 3, sm)
        aux_ref[:, A_CV:A_CV + D] = cv.astype(BF16)
        y_ref[:, 0:D] = (ub_ref[...].astype(F32) * cv).astype(BF16)

        xrb = xr_ref[...]
        pre_x = _conv_fwd(xrb.astype(F32), xrb, xrh_ref[...].astype(F32) * keep, cwx_ref, 4, sm) + cbx_ref[...]
        aux_ref[:, A_PX:A_PX + D] = pre_x.astype(BF16)
        xs_s[...] = _silu(pre_x)
        bcrb = bcr_ref[...]
        pre_bc = _conv_fwd(bcrb.astype(F32), bcrb, bcrh_ref[...].astype(F32) * keep, cwbc_ref, 4, sm) + cbbc_ref[...]
        aux_ref[:, A_PBC:A_PBC + 512] = pre_bc.astype(BF16)
        bc_s[...] = _silu(pre_bc)
        dt = _softplus(dtr_ref[...].astype(F32) + dtb_ref[...])
        a_neg = -jnp.exp(alog_ref[...])
        pos = lax.broadcasted_iota(jnp.int32, (tt, LANES), 0) & (CHUNK - 1)
        cum = _chunk_cumsum(dt * a_neg, pos)
        eh = eh_ref[...]
        dtx_s[...] = _expand(dt, eh)
        cumx_s[...] = _expand(cum, eh)
        irep, causal = _lane_masks()
        rep = _rep_matrix()

        def chunk(c, carry):
            r0 = pl.multiple_of(c * CHUNK, CHUNK)
            rows = pl.ds(r0, CHUNK)
            cumx = cumx_s[rows, :]
            cum_l = cumx[CHUNK - 1:CHUNK, :]
            xd = xs_s[rows, :] * dtx_s[rows, :]
            xf = xd * jnp.exp(cum_l - cumx)
            ex = jnp.exp(cumx)
            e_l = jnp.exp(cum_l)
            rvec = jnp.sum(jnp.where(irep, cumx, 0.0), axis=0, keepdims=True)
            lam = jnp.exp(jnp.where(causal, cumx - rvec, NEG_BIG))
            bc = bc_s[rows, :]
            for g in range(2):
                gs = slice(g * 512, (g + 1) * 512)
                bg = bc[:, g * NSTATE:(g + 1) * NSTATE].astype(BF16)
                cg = bc[:, 256 + g * NSTATE:256 + (g + 1) * NSTATE].astype(BF16)
                s_rep = _nn(_nt(cg, bg).astype(BF16), rep)
                m_g = (s_rep * lam[:, gs]).astype(BF16)
                h_g = hs[:, gs]
                h_b = h_g.astype(BF16)
                st_ref[c, :, gs] = h_b
                yo = _nn(cg, h_b) * ex[:, gs]
                xd_b = xd[:, gs].astype(BF16)
                for hp in range(4):
                    ps = slice(hp * LANES, (hp + 1) * LANES)
                    yd = _nn(m_g[:, ps], _blockdiag(xd_b[:, ps]))
                    yssd_s[rows, g * 512 + hp * LANES:g * 512 + (hp + 1) * LANES] = yd + yo[:, ps]
                hs[:, gs] = h_g * e_l[:, gs] + _tn(bg, xf[:, gs].astype(BF16))
            return carry

        lax.fori_loop(0, nc, chunk, 0, unroll=True)

        ys = yssd_s[...] + dsk_ref[...] * xs_s[...]
        aux_ref[:, A_YS:A_YS + D] = ys.astype(BF16)
        gt = ys * _silu(z_ref[...].astype(F32))
        for g in range(2):
            gs = slice(g * 512, (g + 1) * 512)
            gg = gt[:, gs]
            rn = lax.rsqrt(jnp.mean(gg * gg, axis=-1, keepdims=True) + EPS)
            y_ref[:, D + g * 512:D + (g + 1) * 512] = (gg * rn * nrm_ref[:, gs]).astype(BF16)

    def const(shape):
        return pl.BlockSpec(shape, lambda i: (0, 0))

    in_specs = [main(D, C_UB), main(D, C_UC), main(D, C_UH), main(D, C_Z), main(D, C_XS), main(512, C_BC),
                main(LANES, C_DT), halo(D, C_UC), halo(D, C_UH), halo(D, C_XS), halo(512, C_BC),
                const((8, D)), const((8, D)), const((8, 512)), const((1, D)), const((1, 512)),
                const((1, LANES)), const((1, LANES)), const((1, D)), const((1, D)), const((LANES, D))]
    return pl.pallas_call(
        body, grid=(nblk,),
        in_specs=in_specs,
        out_specs=[pl.BlockSpec((tt, MIX), lambda i: (i, 0)),
                   pl.BlockSpec((nc, NSTATE, D), lambda i: (i, 0, 0)),
                   pl.BlockSpec((tt, AUX_W), lambda i: (i, 0))],
        out_shape=[SDS((t_len, MIX), BF16), SDS((t_len // CHUNK, NSTATE, D), BF16), SDS((t_len, AUX_W), BF16)],
        scratch_shapes=[pltpu.VMEM((NSTATE, D), F32), pltpu.VMEM((tt, D), F32), pltpu.VMEM((tt, 512), F32),
                        pltpu.VMEM((tt, D), F32), pltpu.VMEM((tt, D), F32), pltpu.VMEM((tt, D), F32)],
        compiler_params=_cparams(("arbitrary",)), name="mixer_fwd")(
            *([proj] * 11), prm["scw"], prm["cwx"], prm["cwbc"], prm["cbx"], prm["cbbc"], prm["dtb"],
            prm["alog"], prm["dskx"], prm["nrm"], prm["eh"])


def mixer_bwd(proj, dy, states, aux, prm, tt):
    t_len = proj.shape[0]
    nblk = t_len // tt
    nc = tt // CHUNK

    def rev(i):
        return nblk - 1 - i

    def main(width, col):
        return pl.BlockSpec((tt, width), lambda i, c=col // width: (rev(i), c))

    def body(ub_ref, uc_ref, uh_ref, z_ref, xr_ref, bcr_ref, dtr_ref, dy_ref, st_ref, aux_ref,
             scw_ref, cwx_ref, cwbc_ref, cbx_ref, cbbc_ref, dtb_ref, alog_ref, dsk_ref, nrm_ref, eh_ref, eht_ref,
             dp_ref, gscw_ref, gcwx_ref, gcwbc_ref, gvec_ref, gdt_ref,
             dhs, xs_s, bc_s, dtx_s, cumx_s, dys_s, dxs_s, dbc_s, red_s, ddtx_s, nx_cv, nx_px, nx_pbc, sgx_s, sgbc_s):
        i = pl.program_id(0)

        @pl.when(i == 0)
        def _():
            dhs[...] = jnp.zeros_like(dhs)
            nx_cv[...] = jnp.zeros_like(nx_cv)
            nx_px[...] = jnp.zeros_like(nx_px)
            nx_pbc[...] = jnp.zeros_like(nx_pbc)
            gscw_ref[...] = jnp.zeros_like(gscw_ref)
            gcwx_ref[...] = jnp.zeros_like(gcwx_ref)
            gcwbc_ref[...] = jnp.zeros_like(gcwbc_ref)
            gvec_ref[...] = jnp.zeros_like(gvec_ref)
            gdt_ref[...] = jnp.zeros_like(gdt_ref)

        uc = uc_ref[...].astype(F32)
        uh = uh_ref[...].astype(F32)
        v = uc * uh
        dya = dy_ref[:, 0:D].astype(F32)
        dp_ref[:, C_UB:C_UB + D] = (dya * aux_ref[:, A_CV:A_CV + D].astype(F32)).astype(BF16)
        dcv = dya * ub_ref[...].astype(F32)
        sm = _shift_matrix(tt, True)
        ups = _shifts_up(dcv.astype(BF16), nx_cv[...], sm, 2) + [dcv]
        dv = None
        for k in range(3):
            gscw_ref[k:k + 1, :] += jnp.sum(v * ups[k], axis=0, keepdims=True)
            term = scw_ref[k:k + 1, :] * ups[k]
            dv = term if dv is None else dv + term
        nx_cv[...] = dcv[0:8]
        dp_ref[:, C_UC:C_UC + D] = (dv * uh).astype(BF16)
        dp_ref[:, C_UH:C_UH + D] = (dv * uc).astype(BF16)

        pre_x = aux_ref[:, A_PX:A_PX + D].astype(F32)
        pre_bc = aux_ref[:, A_PBC:A_PBC + 512].astype(F32)
        sg_x = _sigmoid(pre_x)
        sg_bc = _sigmoid(pre_bc)
        sgx_s[...] = sg_x
        sgbc_s[...] = sg_bc
        xs = pre_x * sg_x
        xs_s[...] = xs
        bc_s[...] = pre_bc * sg_bc
        dt_pre = dtr_ref[...].astype(F32) + dtb_ref[...]
        dt = _softplus(dt_pre)
        a_neg = -jnp.exp(alog_ref[...])
        pos = lax.broadcasted_iota(jnp.int32, (tt, LANES), 0) & (CHUNK - 1)
        cum = _chunk_cumsum(dt * a_neg, pos)
        eh = eh_ref[...]
        eht = eht_ref[...]
        dtx_s[...] = _expand(dt, eh)
        cumx_s[...] = _expand(cum, eh)

        irep, causal = _lane_masks()
        irep_g, _ = _lane_masks(512)
        rep = _rep_matrix()
        row8 = lax.broadcasted_iota(jnp.int32, (8, 512), 0)
        lane128 = lax.broadcasted_iota(jnp.int32, (CHUNK, LANES), 1)

        z = z_ref[...].astype(F32)
        sg_z = _sigmoid(z)
        sz = z * sg_z
        dsz = sg_z * (1.0 + z * (1.0 - sg_z))
        ys = aux_ref[:, A_YS:A_YS + D].astype(F32)
        gt = ys * sz
        dyb = dy_ref[:, D:MIX].astype(F32)
        for g in range(2):
            gs = slice(g * 512, (g + 1) * 512)
            gg = gt[:, gs]
            rn = lax.rsqrt(jnp.mean(gg * gg, axis=-1, keepdims=True) + EPS)
            gvec_ref[0:1, gs] += jnp.sum(dyb[:, gs] * gg * rn, axis=0, keepdims=True)
            dgn = dyb[:, gs] * nrm_ref[:, gs]
            dgt = rn * (dgn - gg * (rn * rn) * jnp.mean(dgn * gg, axis=-1, keepdims=True))
            dys = dgt * sz[:, gs]
            dys_s[:, gs] = dys
            dp_ref[:, C_Z + g * 512:C_Z + (g + 1) * 512] = (dgt * ys[:, gs] * dsz[:, gs]).astype(BF16)
        dys_all = dys_s[...]
        gvec_ref[1:2, :] += jnp.sum(dys_all * xs, axis=0, keepdims=True)

        def bwd_chunk(cc, carry):
            c = nc - 1 - cc
            r0 = pl.multiple_of(c * CHUNK, CHUNK)
            rows = pl.ds(r0, CHUNK)
            cumx = cumx_s[rows, :]
            cum_l = cumx[CHUNK - 1:CHUNK, :]
            xs_c = xs_s[rows, :]
            dtx = dtx_s[rows, :]
            xd = xs_c * dtx
            f = jnp.exp(cum_l - cumx)
            xf = xd * f
            ex = jnp.exp(cumx)
            e_l = jnp.exp(cum_l)
            rvec = jnp.sum(jnp.where(irep, cumx, 0.0), axis=0, keepdims=True)
            lam = jnp.exp(jnp.where(causal, cumx - rvec, NEG_BIG))
            bc = bc_s[rows, :]
            dyc = dys_s[rows, :]
            for g in range(2):
                gs = slice(g * 512, (g + 1) * 512)
                bg = bc[:, g * NSTATE:(g + 1) * NSTATE].astype(BF16)
                cg = bc[:, 256 + g * NSTATE:256 + (g + 1) * NSTATE].astype(BF16)
                h0 = st_ref[c, :, gs]
                dh = dhs[:, gs]
                dh_b = dh.astype(BF16)
                xf_g = xf[:, gs]
                dxf = _nn(bg, dh_b)
                db = _nt(xf_g.astype(BF16), dh_b)
                s_rep = _nn(_nt(cg, bg).astype(BF16), rep)
                lam_g = lam[:, gs]
                m_g = s_rep * lam_g
                m_b = m_g.astype(BF16)
                ex_g = ex[:, gs]
                dy_g = dyc[:, gs]
                yo = _nn(cg, h0) * ex_g
                dg_b = (dy_g * ex_g).astype(BF16)
                dc = _nt(dg_b, h0)
                el_g = e_l[:, gs]
                dee = jnp.sum(dh * h0.astype(F32), axis=0, keepdims=True) * el_g
                dhs[:, gs] = dh * el_g + _tn(cg, dg_b)
                xd_b = xd[:, gs].astype(BF16)
                dy_b = dy_g.astype(BF16)
                dm_parts, dxd_parts = [], []
                for hp in range(4):
                    ps = slice(hp * LANES, (hp + 1) * LANES)
                    bd = _blockdiag(xd_b[:, ps])
                    dm_parts.append(_nt(dy_b[:, ps], bd))
                    t2 = _tn(m_b[:, ps], dy_b[:, ps])
                    dxd_parts.append(jnp.where(lane128 < HDIM, t2[0:CHUNK], t2[CHUNK:2 * CHUNK]))
                dm = jnp.concatenate(dm_parts, axis=1)
                dxd = jnp.concatenate(dxd_parts, axis=1) + dxf * f[:, gs]
                dseg = dm * m_g
                ds_b = _nt((dm * lam_g).astype(BF16), rep).astype(BF16)
                dc = dc + _nn(ds_b, bg)
                db = db + _tn(ds_b, cg)
                colsum = jnp.sum(dseg, axis=0, keepdims=True)
                dxfxf = dxf * xf_g
                red = dseg - jnp.where(irep_g, colsum, 0.0) + dy_g * yo - dxfxf
                last = jnp.sum(dxfxf, axis=0, keepdims=True) + dee
                red_s[rows, gs] = red
                tail = pl.ds(pl.multiple_of(r0 + CHUNK - 8, 8), 8)
                red_s[tail, gs] += jnp.where(row8 == 7, last, 0.0)
                ddtx_s[rows, gs] = dxd * xs_c[:, gs]
                dxs_s[rows, gs] = dxd * dtx[:, gs] + dsk_ref[:, gs] * dy_g
                dbc_s[rows, g * NSTATE:(g + 1) * NSTATE] = db
                dbc_s[rows, 256 + g * NSTATE:256 + (g + 1) * NSTATE] = dc
            return carry

        lax.fori_loop(0, nc, bwd_chunk, 0, unroll=True)

        dcum = _head_reduce(red_s[...], eht)
        da = _chunk_rcumsum(dcum, pos)
        ddt = _head_reduce(ddtx_s[...], eht) + da * a_neg
        gdt_ref[1:2, :] += jnp.sum(da * dt, axis=0, keepdims=True) * a_neg
        ddt_raw = ddt * _sigmoid(dt_pre)
        lane_t = lax.broadcasted_iota(jnp.int32, (tt, LANES), 1)
        ddt_raw = jnp.where(lane_t < NHEAD, ddt_raw, 0.0)
        gdt_ref[0:1, :] += jnp.sum(ddt_raw, axis=0, keepdims=True)
        dp_ref[:, C_DT:C_DT + LANES] = ddt_raw.astype(BF16)

        sg_x = sgx_s[...]
        sg_bc = sgbc_s[...]
        pre_x = aux_ref[:, A_PX:A_PX + D].astype(F32)
        pre_bc = aux_ref[:, A_PBC:A_PBC + 512].astype(F32)
        dpx = dxs_s[...] * (sg_x * (1.0 + pre_x * (1.0 - sg_x)))
        dpbc = dbc_s[...] * (sg_bc * (1.0 + pre_bc * (1.0 - sg_bc)))
        gvec_ref[2:3, :] += jnp.sum(dpx, axis=0, keepdims=True)
        gcwbc_ref[4:5, :] += jnp.sum(dpbc, axis=0, keepdims=True)
        xraw = xr_ref[...].astype(F32)
        bcraw = bcr_ref[...].astype(F32)
        ups_x = _shifts_up(dpx.astype(BF16), nx_px[...], sm, 3) + [dpx]
        ups_bc = _shifts_up(dpbc.astype(BF16), nx_pbc[...], sm, 3) + [dpbc]
        dxr, dbcr = None, None
        for k in range(4):
            up_x = ups_x[k]
            up_bc = ups_bc[k]
            gcwx_ref[k:k + 1, :] += jnp.sum(xraw * up_x, axis=0, keepdims=True)
            gcwbc_ref[k:k + 1, :] += jnp.sum(bcraw * up_bc, axis=0, keepdims=True)
            tx = cwx_ref[k:k + 1, :] * up_x
            tb = cwbc_ref[k:k + 1, :] * up_bc
            dxr = tx if dxr is None else dxr + tx
            dbcr = tb if dbcr is None else dbcr + tb
        nx_px[...] = dpx[0:8]
        nx_pbc[...] = dpbc[0:8]
        dp_ref[:, C_XS:C_XS + D] = dxr.astype(BF16)
        dp_ref[:, C_BC:C_BC + 512] = dbcr.astype(BF16)

        @pl.when(i == nblk - 1)
        def _():
            gdt_ref[2:3, :] = _head_reduce(gvec_ref[1:2, :] * jnp.ones((8, 1), F32), eht)[0:1, :]

    def const(shape):
        return pl.BlockSpec(shape, lambda i: (0, 0))

    in_specs = [main(D, C_UB), main(D, C_UC), main(D, C_UH), main(D, C_Z), main(D, C_XS), main(512, C_BC),
                main(LANES, C_DT),
                pl.BlockSpec((tt, MIX), lambda i: (rev(i), 0)),
                pl.BlockSpec((nc, NSTATE, D), lambda i: (rev(i), 0, 0)),
                pl.BlockSpec((tt, AUX_W), lambda i: (rev(i), 0)),
                const((8, D)), const((8, D)), const((8, 512)), const((1, D)), const((1, 512)),
                const((1, LANES)), const((1, LANES)), const((1, D)), const((1, D)), const((LANES, D)),
                const((D, LANES))]
    return pl.pallas_call(
        body, grid=(nblk,),
        in_specs=in_specs,
        out_specs=[pl.BlockSpec((tt, NINP), lambda i: (rev(i), 0)),
                   const((8, D)), const((8, D)), const((8, 512)), const((8, D)), const((8, LANES))],
        out_shape=[SDS((t_len, NINP), BF16), SDS((8, D), F32), SDS((8, D), F32), SDS((8, 512), F32),
                   SDS((8, D), F32), SDS((8, LANES), F32)],
        scratch_shapes=[pltpu.VMEM((NSTATE, D), F32),
                        pltpu.VMEM((tt, D), F32), pltpu.VMEM((tt, 512), F32),
                        pltpu.VMEM((tt, D), F32), pltpu.VMEM((tt, D), F32),
                        pltpu.VMEM((tt, D), F32), pltpu.VMEM((tt, D), F32),
                        pltpu.VMEM((tt, 512), F32),
                        pltpu.VMEM((tt, D), F32), pltpu.VMEM((tt, D), F32),
                        pltpu.VMEM((8, D), F32), pltpu.VMEM((8, D), F32), pltpu.VMEM((8, 512), F32),
                        pltpu.VMEM((tt, D), F32), pltpu.VMEM((tt, 512), F32)],
        compiler_params=_cparams(("arbitrary",)), name="mixer_bwd")(
            *([proj] * 7), dy, states, aux, prm["scw"], prm["cwx"], prm["cwbc"], prm["cbx"], prm["cbbc"], prm["dtb"],
            prm["alog"], prm["dskx"], prm["nrm"], prm["eh"], prm["eht"])


TN_IN = 1152
DW_TOKENS = 4096


def layer_fwd_mix(x, lw, prm, tt):
    proj, h1 = norm_matmul(x, lw["nw1"], lw["win"], "in_proj")
    y, st, aux = mixer_fwd(proj, prm, tt)
    return h1, proj, (st, aux), y


def layer_fwd_mlp(x, mixed, lw):
    h1, proj, st, y = mixed
    x1 = matmul_residual(y, lw["wout"], x, False, "out_proj")
    u, h2 = norm_matmul(x1, lw["nw2"], lw["wup"], "up_proj")
    x2 = matmul_residual(u, lw["wdn"], x1, True, "down_proj")
    return x2, (x, h1, proj, st, y, x1, h2, u)


def layer_fwd(x, lw, prm, tt):
    return layer_fwd_mlp(x, layer_fwd_mix(x, lw, prm, tt), lw)


def _dw(a, b, a_cols, b_cols, relu2, name, tt_max=2048):
    m_len, n_len = a.shape[1], b.shape[1]
    n_a, n_b = m_len // a_cols, n_len // b_cols
    assert n_a == 1 or n_b == 1
    if n_b == 1:
        return matmul_tn(
            a, b,
            lambda t_: pl.BlockSpec((t_, a_cols), lambda n, t: (t, n)),
            lambda t_: pl.BlockSpec((t_, n_len), lambda n, t: (t, 0)),
            pl.BlockSpec((a_cols, n_len), lambda n, t: (n, 0)), SDS((m_len, n_len), BF16), n_a, relu2, name, tt_max)
    return matmul_tn(
        a, b,
        lambda t_: pl.BlockSpec((t_, m_len), lambda n, t: (t, 0)),
        lambda t_: pl.BlockSpec((t_, b_cols), lambda n, t: (t, n)),
        pl.BlockSpec((m_len, b_cols), lambda n, t: (0, n)), SDS((m_len, n_len), BF16), n_b, relu2, name, tt_max)


def layer_bwd_mlp(dx2, dx2b, lw, saved):
    _, _, _, _, y, x1, h2, u = saved
    du = matmul_nt_act(dx2b, lw["wdn"], u, "mlp_bwd_du")
    g_wdn = _dw(u, dx2b, 1024, D, True, "dw_down")
    dx1, dx1b, g_nw2 = matmul_nt_norm_bwd(du, lw["wup"], x1, lw["nw2"], dx2, "mlp_bwd_dx")
    cb = DFF // N_DEV
    g_wup = matmul_tn(
        h2, du,
        lambda t_: pl.BlockSpec((t_, D), lambda n, t: (t, 0)),
        lambda t_: pl.BlockSpec((t_, 2 * cb), lambda n, t: (t, n)),
        pl.BlockSpec((2, D, cb), lambda n, t: (n, 0, 0)), SDS((N_DEV, D, cb), BF16), N_DEV // 2, False, "dw_up",
        DW_TOKENS)
    dy = matmul_nt_act(dx1b, lw["wout"], None, "out_bwd_dy")
    g_wout = _dw(y, dx1b, 1024, D, False, "dw_out", DW_TOKENS)
    return dx1, dx1b, dy, {"wout": g_wout, "wup": g_wup, "wdn": g_wdn, "nw2": g_nw2[0]}


def layer_bwd_mix(dx1, dy, lw, prm, saved, tt):
    x, h1, proj, st = saved[:4]
    dproj, gscw, gcwx, gcwbc, gvec, gdt = mixer_bwd(proj, dy, st[0], st[1], prm, tt)
    dx0, dx0b, g_nw1 = matmul_nt_norm_bwd(dproj, lw["win"], x, lw["nw1"], dx1, "in_bwd_dx")
    g_win = _dw(h1, dproj, D, TN_IN, False, "dw_in", DW_TOKENS)
    grads = {
        "win": g_win, "scw": gscw[0:3], "cw": jnp.concatenate([gcwx[0:4], gcwbc[0:4]], axis=1),
        "cb": jnp.concatenate([gvec[2], gcwbc[4]], axis=0),
        "dtb": gdt[0, :NHEAD], "alog": gdt[1, :NHEAD], "dsk": gdt[2, :NHEAD],
        "nrm": gvec[0], "nw1": g_nw1[0],
    }
    return dx0, dx0b, grads


def layer_bwd(dx2, dx2b, lw, prm, saved, tt):
    dx1, dx1b, dy, g_mlp = layer_bwd_mlp(dx2, dx2b, lw, saved)
    dx0, dx0b, g_mix = layer_bwd_mix(dx1, dy, lw, prm, saved, tt)
    return dx0, dx0b, {**g_mlp, **g_mix}


def layer_params(win, scw, cw, nw1, nw2, conv_b, dt_bias, a_log, d_skip, ssd_norm_w, eh, eht):
    def rows8(a):
        return jnp.pad(a, ((0, 8 - a.shape[0]), (0, 0)))

    def lanes128(a):
        return jnp.pad(a, (0, LANES - a.shape[0]))[None, :]

    lw = {"win": win, "nw1": nw1[None, :], "nw2": nw2[None, :]}
    prm = {"scw": rows8(scw), "cwx": rows8(cw[:, :D]), "cwbc": rows8(cw[:, D:]),
           "cbx": conv_b[None, :D], "cbbc": conv_b[None, D:],
           "dtb": lanes128(dt_bias), "alog": lanes128(a_log),
           "dskx": jnp.repeat(d_skip, HDIM)[None, :], "nrm": ssd_norm_w[None, :], "eh": eh, "eht": eht}
    return lw, prm


def _flip(v, bit):
    return 1 - v if bit else v


def all_gather(arrs, name):
    n = len(arrs)

    def body(*refs):
        ins, outs = refs[:n], refs[n:2 * n]
        send_sems, recv_sems, local_sems = refs[2 * n:]
        x, y, c = lax.axis_index("x"), lax.axis_index("y"), lax.axis_index("c")
        sibling = (x, y, 1 - c)
        chips = [(1 - x, y), (x, 1 - y), (1 - x, 1 - y)]

        def idx(px, py, pc):
            return 4 * px + 2 * py + pc

        def copy(a, k, block, to, src=None):
            dst = outs[a].at[idx(*block)]
            return pltpu.make_async_remote_copy(
                src_ref=dst if src is None else src, dst_ref=dst,
                send_sem=send_sems.at[a, k], recv_sem=recv_sems.at[a, k], device_id=to, device_id_type=MESH)

        me = (x, y, c)
        mine = [pltpu.make_async_copy(ins[a], outs[a].at[idx(*me)], local_sems.at[a]) for a in range(n)]
        for cp in mine:
            cp.start()
        first = []
        for a in range(n):
            first.append(copy(a, 0, me, sibling, src=ins[a]))
            first += [copy(a, 1 + j, me, (*chip, c), src=ins[a]) for j, chip in enumerate(chips)]
        for cp in first:
            cp.start()
        passed = []
        for j, chip in enumerate(chips):
            for a in range(n):
                copy(a, 1 + j, (*chip, c), me).wait_recv()
                cp = copy(a, 4 + j, (*chip, c), sibling)
                cp.start()
                passed.append(cp)
        for a in range(n):
            copy(a, 0, sibling, me).wait_recv()
            for j, chip in enumerate(chips):
                copy(a, 4 + j, (*chip, 1 - c), me).wait_recv()
        for cp in first + passed:
            cp.wait_send()
        for cp in mine:
            cp.wait()

    any_spec = pl.BlockSpec(memory_space=pl.ANY)
    return pl.pallas_call(
        body, in_specs=[any_spec] * n, out_specs=[any_spec] * n,
        out_shape=[SDS((N_DEV,) + a.shape, a.dtype) for a in arrs],
        scratch_shapes=[pltpu.SemaphoreType.DMA((n, 7)), pltpu.SemaphoreType.DMA((n, 7)),
                        pltpu.SemaphoreType.DMA((n,))],
        name=name)(*arrs)


HBM_SPEC = pl.BlockSpec(memory_space=pltpu.HBM)
SEM_SPEC = pl.BlockSpec(memory_space=pltpu.SEMAPHORE)
SIDE_EFFECT = pltpu.SideEffectType.DATAFLOW_SIDE_EFFECTING
N_PEER = N_DEV - 1


def _peer(mask):
    x, y, c = lax.axis_index("x"), lax.axis_index("y"), lax.axis_index("c")
    return _flip(x, mask & 4), _flip(y, mask & 2), _flip(c, mask & 1)


ALL_PEERS = tuple(range(1, N_DEV))
SIBLING_AND_CHIPS = (1, 2, 4, 6)


def exchange_start(srcs, per_peer, name, after=None, masks=ALL_PEERS):
    n = len(srcs)
    npeer = len(masks)
    lands = [SDS((N_DEV,) + (a.shape[1:] if per_peer else a.shape), a.dtype) for a in srcs]
    n_in = 2 * n + (after is not None)

    def body(*refs):
        src_refs, land_refs = refs[:n], refs[n:2 * n]
        send_sems, recv_sems = refs[n_in], refs[n_in + 1]
        token = refs[-1]
        x, y, c = lax.axis_index("x"), lax.axis_index("y"), lax.axis_index("c")
        me = 4 * x + 2 * y + c
        for a in range(n):
            for k, mask in enumerate(masks):
                px, py, pc = _peer(mask)
                part = src_refs[a].at[4 * px + 2 * py + pc] if per_peer else src_refs[a]
                pltpu.make_async_remote_copy(
                    src_ref=part, dst_ref=land_refs[a].at[me], send_sem=send_sems.at[a * npeer + k],
                    recv_sem=recv_sems.at[a * npeer + k], device_id=(px, py, pc), device_id_type=MESH).start()
        token[...] = jnp.zeros_like(token)

    out = pl.pallas_call(
        body, name=name,
        out_shape=(pltpu.SemaphoreType.DMA((n * npeer,)), pltpu.SemaphoreType.DMA((n * npeer,)),
                   *[pltpu.HBM(a.shape, a.dtype) for a in srcs], *[pltpu.HBM(l.shape, l.dtype) for l in lands],
                   SDS((8, LANES), F32)),
        in_specs=(HBM_SPEC,) * (2 * n) + ((pl.BlockSpec(memory_space=pl.ANY),) if after is not None else ()),
        out_specs=(SEM_SPEC, SEM_SPEC) + (HBM_SPEC,) * (2 * n) + (pl.BlockSpec(memory_space=pltpu.VMEM),),
        input_output_aliases={k: 2 + k for k in range(2 * n)},
        compiler_params=pltpu.CompilerParams(has_side_effects=SIDE_EFFECT),
    )(*[pltpu.with_memory_space_constraint(a, pltpu.HBM) for a in srcs],
      *[pltpu.with_memory_space_constraint(lax.empty(l.shape, l.dtype), pltpu.HBM) for l in lands],
      *([after] if after is not None else []))
    return out[0], out[1], list(out[2:2 + n]), list(out[2 + n:2 + 2 * n]), out[-1]


def exchange_wait(started, after, per_peer, name, masks=ALL_PEERS):
    send_sems, recv_sems, srcs, lands, _ = started
    n = len(srcs)
    npeer = len(masks)

    def body(*refs):
        src_refs, land_refs = refs[:n], refs[n:2 * n]
        send_sems, recv_sems = refs[2 * n], refs[2 * n + 1]
        for k, mask in enumerate(masks):
            for a in range(n):
                copy = pltpu.make_async_remote_copy(
                    src_ref=src_refs[a].at[0] if per_peer else src_refs[a], dst_ref=land_refs[a].at[0],
                    send_sem=send_sems.at[a * npeer + k], recv_sem=recv_sems.at[a * npeer + k],
                    device_id=_peer(mask), device_id_type=MESH)
                copy.wait_send()
                copy.wait_recv()

    out = pl.pallas_call(
        body, name=name,
        out_shape=tuple(pltpu.HBM(a.shape, a.dtype) for a in srcs + lands),
        in_specs=(HBM_SPEC,) * (2 * n) + (SEM_SPEC, SEM_SPEC, pl.BlockSpec(memory_space=pl.ANY)),
        out_specs=(HBM_SPEC,) * (2 * n), input_output_aliases={k: k for k in range(2 * n)},
        compiler_params=pltpu.CompilerParams(has_side_effects=SIDE_EFFECT),
    )(*srcs, *lands, send_sems, recv_sems, after)
    return list(out[:n]), list(out[n:])


def relay_to_sibling(lands, name):
    n = len(lands)
    chips = (2, 4, 6)

    def body(*refs):
        land_refs = refs[n:2 * n]
        send_sems, recv_sems = refs[2 * n], refs[2 * n + 1]
        x, y, c = lax.axis_index("x"), lax.axis_index("y"), lax.axis_index("c")
        copies = []
        for a in range(n):
            for k, mask in enumerate(chips):
                px, py, _ = _peer(mask)
                block = land_refs[a].at[4 * px + 2 * py + c]
                cp = pltpu.make_async_remote_copy(
                    src_ref=block, dst_ref=block, send_sem=send_sems.at[a * 3 + k], recv_sem=recv_sems.at[a * 3 + k],
                    device_id=(x, y, 1 - c), device_id_type=MESH)
                cp.start()
                copies.append((cp, a, k, land_refs[a].at[4 * px + 2 * py + 1 - c]))
        for cp, a, k, arriving in copies:
            cp.wait_send()
            pltpu.make_async_remote_copy(
                src_ref=arriving, dst_ref=arriving, send_sem=send_sems.at[a * 3 + k], recv_sem=recv_sems.at[a * 3 + k],
                device_id=(x, y, 1 - c), device_id_type=MESH).wait_recv()

    any_spec = pl.BlockSpec(memory_space=pl.ANY)
    return list(pl.pallas_call(
        body, in_specs=[any_spec] * n, out_specs=[any_spec] * n,
        out_shape=[SDS(a.shape, a.dtype) for a in lands],
        input_output_aliases={k: k for k in range(n)},
        scratch_shapes=[pltpu.SemaphoreType.DMA((n * 3,)), pltpu.SemaphoreType.DMA((n * 3,))],
        name=name)(*lands))


IN_SHARD = NIN // N_DEV
SLOT_W = 768


def _slot_window(j):
    return (IN_SHARD * j // LANES) * LANES, -(-(IN_SHARD * (j + 1)) // LANES) * LANES


def _placement(j):
    a, b = _slot_window(j)
    r = lax.broadcasted_iota(jnp.int32, (SLOT_W, b - a), 0)
    c = lax.broadcasted_iota(jnp.int32, (SLOT_W, b - a), 1)
    return jnp.where(jnp.logical_and(c == r + (IN_SHARD * j - a), r < IN_SHARD), 1.0, 0.0).astype(BF16)


def assemble_w_in(land):
    tm = 256

    def body(l_ref, o_ref, acc):
        acc[...] = jnp.zeros_like(acc)
        for j in range(N_DEV):
            a, b = _slot_window(j)
            acc[:, a:b] += _nn(l_ref[j], _placement(j))
        o_ref[...] = acc[...].astype(BF16)

    return pl.pallas_call(
        body, grid=(D // tm,),
        in_specs=[pl.BlockSpec((N_DEV, tm, SLOT_W), lambda i: (0, i, 0))],
        out_specs=pl.BlockSpec((tm, NINP), lambda i: (i, 0)),
        out_shape=SDS((D, NINP), BF16),
        scratch_shapes=[pltpu.VMEM((tm, NINP), F32)],
        compiler_params=_cparams(("parallel",)), name="assemble_w_in")(land)


def scatter_w_in(dw):
    tm = 256

    def body(d_ref, o_ref):
        for j in range(N_DEV):
            a, b = _slot_window(j)
            o_ref[j] = _nt(d_ref[:, a:b], _placement(j)).astype(BF16)

    return pl.pallas_call(
        body, grid=(D // tm,),
        in_specs=[pl.BlockSpec((tm, NINP), lambda i: (i, 0))],
        out_specs=pl.BlockSpec((N_DEV, tm, SLOT_W), lambda i: (0, i, 0)),
        out_shape=SDS((N_DEV, D, SLOT_W), BF16),
        compiler_params=_cparams(("parallel",)), name="scatter_w_in")(dw)


def _adamw_math(g, w_ref, m_ref, v_ref, g_ref, d_ref, nm_ref, nv_ref):
    mn = ADAM_B1 * m_ref[...] + (1.0 - ADAM_B1) * g
    vn = ADAM_B2 * v_ref[...] + (1.0 - ADAM_B2) * jnp.square(g)
    m_hat = mn / (1.0 - ADAM_B1 ** ADAM_STEP)
    v_hat = vn / (1.0 - ADAM_B2 ** ADAM_STEP)
    g_ref[...] = g
    d_ref[...] = -ADAM_LR * (m_hat / (jnp.sqrt(v_hat) + ADAM_EPS) + ADAM_WD * w_ref[...])
    nm_ref[...] = mn
    nv_ref[...] = vn


def adamw_layers(w, slots, m, v, name):
    depth, r_len, c_len = w.shape
    cs = slots[0].shape[2]
    br = min(128, r_len)
    assert r_len % br == 0

    def body(w_ref, *rest):
        s_refs, (m_ref, v_ref, g_ref, d_ref, nm_ref, nv_ref) = rest[:depth], rest[depth:]
        layer = pl.program_id(0)
        for k in range(depth):
            @pl.when(layer == k)
            def _(k=k):
                g = s_refs[k][0, :, 0:c_len].astype(F32)
                for j in range(1, N_DEV):
                    g = g + s_refs[k][j, :, 0:c_len].astype(F32)
                _adamw_math(g, w_ref, m_ref, v_ref, g_ref, d_ref, nm_ref, nv_ref)

    spec = pl.BlockSpec((None, br, c_len), lambda l, i: (l, i, 0))
    s_specs = [pl.BlockSpec((N_DEV, br, cs), lambda l, i, k=k: (0, jnp.where(l == k, i, 0), 0))
               for k in range(depth)]
    return pl.pallas_call(
        body, grid=(depth, r_len // br),
        in_specs=[spec] + s_specs + [spec, spec],
        out_specs=[spec] * 4, out_shape=[SDS(w.shape, F32)] * 4,
        compiler_params=_cparams(("arbitrary", "arbitrary")), name=name)(w, *slots, m, v)


def adamw(w, slots, m, v, name):
    r_len, c_len = w.shape
    br = r_len if r_len <= 512 else 512
    assert r_len % br == 0

    def body(w_ref, s_ref, m_ref, v_ref, g_ref, d_ref, nm_ref, nv_ref):
        g = s_ref[0].astype(F32)
        for k in range(1, N_DEV):
            g = g + s_ref[k].astype(F32)
        _adamw_math(g, w_ref, m_ref, v_ref, g_ref, d_ref, nm_ref, nv_ref)

    spec = pl.BlockSpec((br, c_len), lambda i: (i, 0))
    return pl.pallas_call(
        body, grid=(r_len // br,),
        in_specs=[spec, pl.BlockSpec((N_DEV, br, c_len), lambda i: (0, i, 0)), spec, spec],
        out_specs=[spec] * 4, out_shape=[SDS((r_len, c_len), F32)] * 4,
        compiler_params=_cparams(("parallel",)), name=name)(w, slots, m, v)


def _adamw_nd(w, slots, m, v, name):
    shp = w.shape
    r = int(np.prod(shp[:-1]))
    outs = adamw(w.reshape(r, shp[-1]), slots.reshape(N_DEV, r, shp[-1]), m.reshape(r, shp[-1]),
                 v.reshape(r, shp[-1]), name)
    return [o.reshape(shp) for o in outs]


SMALL = [("norm_mix_w", DEPTH * D), ("ssd_conv_b", DEPTH * XBC), ("dt_bias", DEPTH * NHEAD),
         ("a_log", DEPTH * NHEAD), ("d_skip", DEPTH * NHEAD), ("ssd_norm_w", DEPTH * D),
         ("norm_mlp_w", DEPTH * D), ("final_norm_w", D)]
SMALL_LEN = sum(s for _, s in SMALL)
SMALL_ROWS = -(-SMALL_LEN // LANES)


def _pack_small(parts):
    flat = jnp.concatenate([parts[k].reshape(-1) for k, _ in SMALL])
    return jnp.pad(flat, (0, SMALL_ROWS * LANES - SMALL_LEN)).reshape(SMALL_ROWS, LANES)


def _unpack_small(packed, shapes):
    flat = packed.reshape(-1)
    out, off = {}, 0
    for k, s in SMALL:
        out[k] = flat[off:off + s].reshape(shapes[k])
        off += s
    return out


def kernel(x, norm_mix_w, w_in, short_conv_w, ssd_conv_w, ssd_conv_b, dt_bias, a_log, d_skip, ssd_norm_w, w_out, norm_mlp_w, w_up, w_down, final_norm_w, loss_target, m_norm_mix_w, m_w_in, m_short_conv_w, m_ssd_conv_w, m_ssd_conv_b, m_dt_bias, m_a_log, m_d_skip, m_ssd_norm_w, m_w_out, m_norm_mlp_w, m_w_up, m_w_down, m_final_norm_w, v_norm_mix_w, v_w_in, v_short_conv_w, v_ssd_conv_w, v_ssd_conv_b, v_dt_bias, v_a_log, v_d_skip, v_ssd_norm_w, v_w_out, v_norm_mlp_w, v_w_up, v_w_down, v_final_norm_w):
    xs = x[0]
    t_len = xs.shape[0]
    tt = min(256, t_len)
    eh, eht = _head_matrices()
    me = 4 * lax.axis_index("x") + 2 * lax.axis_index("y") + lax.axis_index("c")

    def start_weights(i, after):
        first = exchange_start(
            [jnp.pad(w_in[i].astype(BF16), ((0, 0), (0, SLOT_W - IN_SHARD))), short_conv_w[i], ssd_conv_w[i]],
            False, "w_in_start_%d" % i, after, SIBLING_AND_CHIPS)
        rest = exchange_start([w_out[i].astype(BF16), w_up[i].astype(BF16), w_down[i].astype(BF16)], False,
                              "w_rest_start_%d" % i, first[4] if after is None else after, SIBLING_AND_CHIPS)
        return first, rest

    def fill_own(srcs, lands, per_peer):
        own = [lax.dynamic_index_in_dim(s_, me, 0, keepdims=False) for s_ in srcs] if per_peer else srcs
        return [lax.dynamic_update_index_in_dim(l_, o_, me, 0) for l_, o_ in zip(lands, own)]

    def finish_weights(started, after, name):
        srcs, lands = exchange_wait(started, after, False, name + "_wait", SIBLING_AND_CHIPS)
        return fill_own(srcs, relay_to_sibling(lands, name + "_relay"), False)

    act = xs
    saved, layers = [], []
    first, rest = start_weights(0, None)
    token = first[4][0, 0] + rest[4][0, 0]
    for i in range(DEPTH):
        g_in, g_sc, g_cw = finish_weights(first, act, "w_in_%d" % i)
        lw, prm = layer_params(
            assemble_w_in(g_in), g_sc.transpose(1, 0, 2).reshape(3, D), g_cw.transpose(1, 0, 2).reshape(4, XBC),
            norm_mix_w[i], norm_mlp_w[i], ssd_conv_b[i], dt_bias[i], a_log[i], d_skip[i], ssd_norm_w[i], eh, eht)
        lw["nw1"] = lw["nw1"] + token
        mixed = layer_fwd_mix(act, lw, prm, tt)
        g_out, g_up, g_dn = finish_weights(rest, mixed[3], "w_rest_%d" % i)
        lw.update(wout=g_out.reshape(MIX, D), wup=g_up, wdn=g_dn.reshape(DFF, D))
        if i + 1 < DEPTH:
            first, rest = start_weights(i + 1, g_dn)
            token = first[4][0, 0] + rest[4][0, 0]
            lw["nw2"] = lw["nw2"] + token
        layers.append((lw, prm))
        act, sv = layer_fwd_mlp(act, mixed, lw)
        saved.append(sv)
    loss_acc, dx, dxb, g_fw = loss_head(act, final_norm_w[None, :], loss_target[0])

    grads = [None] * DEPTH
    sent_rest, sent_in = [None] * DEPTH, [None] * DEPTH
    token = None
    for i in reversed(range(DEPTH)):
        lw, prm = layers[i]
        if token is not None:
            lw = dict(lw, nw2=lw["nw2"] + token)
        dx1, _, dy, g_mlp = layer_bwd_mlp(dx, dxb, lw, saved[i])
        sent_rest[i] = exchange_start(
            [g_mlp["wout"].reshape(N_DEV, MIX // N_DEV, D), g_mlp["wup"], g_mlp["wdn"].reshape(N_DEV, DFF // N_DEV, D)],
            True, "g_rest_start_%d" % i)
        dx, dxb, g_mix = layer_bwd_mix(dx1, dy, lw, dict(prm, nrm=prm["nrm"] + sent_rest[i][4][0, 0]), saved[i], tt)
        grads[i] = {**g_mlp, **g_mix}
        if i > 0:
            sent_in[i] = exchange_start([scatter_w_in(g_mix["win"])], True, "g_in_start_%d" % i)
            token = sent_in[i][4][0, 0]

    def stack(k):
        return jnp.stack([g[k] for g in grads])

    small = _pack_small({"norm_mix_w": stack("nw1"), "ssd_conv_b": stack("cb"), "dt_bias": stack("dtb"),
                         "a_log": stack("alog"), "d_skip": stack("dsk"), "ssd_norm_w": stack("nrm"),
                         "norm_mlp_w": stack("nw2"), "final_norm_w": g_fw[0]})
    r_small, r_sc, r_cw = all_gather([small, stack("scw"), stack("cw")], "gather_small_grads")
    r_sc = lax.dynamic_slice_in_dim(r_sc, me * (D // N_DEV), D // N_DEV, axis=3)
    r_cw = lax.dynamic_slice_in_dim(r_cw, me * (XBC // N_DEV), XBC // N_DEV, axis=3)
    sent_in[0] = exchange_start([scatter_w_in(grads[0]["win"])], True, "g_in_start_0", after=r_small)

    after = sent_in[0][4]
    recv = [fill_own(*exchange_wait(sent_rest[i], after, True, "g_rest_wait_%d" % i), True) for i in range(DEPTH)]
    res = {}
    res["w_out"] = adamw_layers(w_out, [r[0] for r in recv], m_w_out, v_w_out, "adamw_w_out")
    res["w_up"] = adamw_layers(w_up, [r[1] for r in recv], m_w_up, v_w_up, "adamw_w_up")
    res["w_down"] = adamw_layers(w_down, [r[2] for r in recv], m_w_down, v_w_down, "adamw_w_down")
    after = res["w_down"][1]
    recv_in = [fill_own(*exchange_wait(sent_in[i], after, True, "g_in_wait_%d" % i), True)[0] for i in range(DEPTH)]
    res["w_in"] = adamw_layers(w_in, recv_in, m_w_in, v_w_in, "adamw_w_in")
    res["short_conv_w"] = _adamw_nd(short_conv_w, r_sc, m_short_conv_w, v_short_conv_w, "adamw_short_conv")
    res["ssd_conv_w"] = _adamw_nd(ssd_conv_w, r_cw, m_ssd_conv_w, v_ssd_conv_w, "adamw_ssd_conv")
    small_w = {"norm_mix_w": norm_mix_w, "ssd_conv_b": ssd_conv_b, "dt_bias": dt_bias, "a_log": a_log,
               "d_skip": d_skip, "ssd_norm_w": ssd_norm_w, "norm_mlp_w": norm_mlp_w, "final_norm_w": final_norm_w}
    small_m = {"norm_mix_w": m_norm_mix_w, "ssd_conv_b": m_ssd_conv_b, "dt_bias": m_dt_bias, "a_log": m_a_log,
               "d_skip": m_d_skip, "ssd_norm_w": m_ssd_norm_w, "norm_mlp_w": m_norm_mlp_w,
               "final_norm_w": m_final_norm_w}
    small_v = {"norm_mix_w": v_norm_mix_w, "ssd_conv_b": v_ssd_conv_b, "dt_bias": v_dt_bias, "a_log": v_a_log,
               "d_skip": v_d_skip, "ssd_norm_w": v_ssd_norm_w, "norm_mlp_w": v_norm_mlp_w,
               "final_norm_w": v_final_norm_w}
    shapes = {k: a.shape for k, a in small_w.items()}
    packed = adamw(_pack_small(small_w), r_small, _pack_small(small_m), _pack_small(small_v), "adamw_small")
    unpacked = [_unpack_small(p, shapes) for p in packed]
    for k in small_w:
        res[k] = [u[k] for u in unpacked]

    loss = lax.psum(loss_acc[0, 0], ("x", "y", "c"))
    order = ["norm_mix_w", "w_in", "short_conv_w", "ssd_conv_w", "ssd_conv_b", "dt_bias", "a_log", "d_skip",
             "ssd_norm_w", "w_out", "norm_mlp_w", "w_up", "w_down", "final_norm_w"]
    out = [loss, dx[None]]
    for part in range(4):
        out += [res[k][part] for k in order]
    return tuple(out)
```

```python
import functools

import numpy as np
import jax
import jax.numpy as jnp
from jax import lax
from jax.experimental import pallas as pl
from jax.experimental.pallas import tpu as pltpu

F32 = jnp.float32
BF16 = jnp.bfloat16
SDS = jax.ShapeDtypeStruct

N_DEV = 8
DEPTH = 4
D = 1024
NIN = 5648
NINP = 5760
DFF = 4096
MIX = 2048
NHEAD = 16
HDIM = 64
NSTATE = 128
CHUNK = 64
XBC = 1536
EPS = 1e-5
LANES = 128
NEG_BIG = -1e30

C_UB, C_UC, C_UH, C_Z, C_XS, C_BC, C_DT = 0, 1024, 2048, 3072, 4096, 5120, 5632
A_CV, A_YS, A_PX, A_PBC, AUX_W = 0, 1024, 2048, 3072, 3584

ADAM_LR = 0.001
ADAM_B1 = 0.9
ADAM_B2 = 0.999
ADAM_EPS = 1e-08
ADAM_WD = 0.01
ADAM_STEP = 10

VMEM_LIMIT = 56 * 1024 * 1024
MESH = pl.DeviceIdType.MESH


def _cparams(sem):
    return pltpu.CompilerParams(dimension_semantics=sem, vmem_limit_bytes=VMEM_LIMIT)


def _nt(a, b):
    return lax.dot_general(a, b, (((1,), (1,)), ((), ())), preferred_element_type=F32)


def _tn(a, b):
    return lax.dot_general(a, b, (((0,), (0,)), ((), ())), preferred_element_type=F32)


def _nn(a, b):
    return jnp.dot(a, b, preferred_element_type=F32)


def _sigmoid(v):
    return 0.5 * jnp.tanh(0.5 * v) + 0.5


def _split3(v):
    v1 = v.astype(BF16)
    r1 = v - v1.astype(F32)
    v2 = r1.astype(BF16)
    v3 = (r1 - v2.astype(F32)).astype(BF16)
    return v1, v2, v3


def _expand(v, eh):
    v1, v2, v3 = _split3(v)
    return _nn(v1, eh) + _nn(v2, eh) + _nn(v3, eh)


def _head_reduce(v, eht):
    v1 = v.astype(BF16)
    v2 = (v - v1.astype(F32)).astype(BF16)
    return _nn(v1, eht) + _nn(v2, eht)


def _head_matrices():
    eh = np.zeros((LANES, D), np.float32)
    for h in range(NHEAD):
        eh[h, h * HDIM:(h + 1) * HDIM] = 1.0
    return jnp.asarray(eh, BF16), jnp.asarray(eh.T.copy(), BF16)


def _resident(shape):
    return pl.BlockSpec(shape, lambda *_: (0,) * len(shape), pipeline_mode=pl.Buffered(1))


def _col_chunks(n, step):
    return [(c, min(c + step, n)) for c in range(0, n, step)]


def norm_matmul(x, nw, w, name):
    t_len = x.shape[0]
    blocked = w.ndim == 3
    n_len = w.shape[0] * w.shape[2] if blocked else w.shape[1]
    tm = min(512, t_len)
    chunks = _col_chunks(n_len, n_len // N_DEV if blocked else 1536)

    def body(x_ref, nw_ref, w_ref, o_ref, h_ref):
        xv = x_ref[...]
        r = lax.rsqrt(jnp.mean(xv * xv, axis=-1, keepdims=True) + EPS)
        hv = (xv * r * nw_ref[...]).astype(BF16)
        h_ref[...] = hv
        for j, (c0, c1) in enumerate(chunks):
            wj = w_ref[j] if blocked else w_ref[:, c0:c1]
            o_ref[:, c0:c1] = _nn(hv, wj).astype(o_ref.dtype)

    return pl.pallas_call(
        body, grid=(t_len // tm,),
        in_specs=[pl.BlockSpec((tm, D), lambda i: (i, 0)), _resident((1, D)), _resident(w.shape)],
        out_specs=[pl.BlockSpec((tm, n_len), lambda i: (i, 0)),
                   pl.BlockSpec((tm, D), lambda i: (i, 0))],
        out_shape=[SDS((t_len, n_len), BF16), SDS((t_len, D), BF16)],
        compiler_params=_cparams(("parallel",)), name=name)(x, nw, w)


def matmul_residual(a, w, res, relu2, name):
    t_len, k_len = a.shape
    tm = min(512, t_len)

    def body(a_ref, w_ref, res_ref, o_ref):
        av = a_ref[...]
        if relu2:
            af = jnp.maximum(av.astype(F32), 0.0)
            av = (af * af).astype(BF16)
        o_ref[...] = res_ref[...] + _nn(av, w_ref[...])

    return pl.pallas_call(
        body, grid=(t_len // tm,),
        in_specs=[pl.BlockSpec((tm, k_len), lambda i: (i, 0)),
                  _resident((k_len, D)),
                  pl.BlockSpec((tm, D), lambda i: (i, 0))],
        out_specs=pl.BlockSpec((tm, D), lambda i: (i, 0)),
        out_shape=SDS((t_len, D), F32),
        compiler_params=_cparams(("parallel",)), name=name)(a, w, res)


def matmul_nt_act(dy, w, u, name):
    t_len = dy.shape[0]
    n_len = w.shape[0]
    tm = min(512, t_len)
    chunks = _col_chunks(n_len, 1024)

    def body(dy_ref, w_ref, *rest):
        if u is None:
            (o_ref,) = rest
        else:
            u_ref, o_ref = rest
        dyv = dy_ref[...]
        for c0, c1 in chunks:
            p = _nt(dyv, w_ref[c0:c1, :])
            if u is not None:
                p = p * (2.0 * jnp.maximum(u_ref[:, c0:c1].astype(F32), 0.0))
            o_ref[:, c0:c1] = p.astype(o_ref.dtype)

    in_specs = [pl.BlockSpec((tm, D), lambda i: (i, 0)), _resident((n_len, D))]
    args = [dy, w]
    if u is not None:
        in_specs.append(pl.BlockSpec((tm, n_len), lambda i: (i, 0)))
        args.append(u)
    return pl.pallas_call(
        body, grid=(t_len // tm,),
        in_specs=in_specs,
        out_specs=pl.BlockSpec((tm, n_len), lambda i: (i, 0)),
        out_shape=SDS((t_len, n_len), BF16),
        compiler_params=_cparams(("parallel",)), name=name)(*args)


def matmul_tn(a, b, a_spec, b_spec, o_spec, o_shape, n_out, relu2, name, tt_max=2048):
    t_len = a.shape[0]
    tt = min(tt_max, t_len)
    nt = t_len // tt

    def body(a_ref, b_ref, o_ref, acc):
        t = pl.program_id(1)
        av = a_ref[...]
        if relu2:
            af = jnp.maximum(av.astype(F32), 0.0)
            av = (af * af).astype(BF16)
        p = _tn(av, b_ref[...])

        @pl.when(t == 0)
        def _():
            acc[...] = p

        @pl.when(t > 0)
        def _():
            acc[...] += p

        @pl.when(t == nt - 1)
        def _():
            if len(blk) == 3:
                for j in range(blk[0]):
                    o_ref[j] = acc[:, j * blk[2]:(j + 1) * blk[2]].astype(o_ref.dtype)
            else:
                o_ref[...] = acc[...].astype(o_ref.dtype)

    blk = tuple(o_spec.block_shape)
    acc_shape = (blk[1], blk[0] * blk[2]) if len(blk) == 3 else blk
    return pl.pallas_call(
        body, grid=(n_out, nt),
        in_specs=[a_spec(tt), b_spec(tt)],
        out_specs=o_spec, out_shape=o_shape,
        scratch_shapes=[pltpu.VMEM(acc_shape, F32)],
        compiler_params=_cparams(("parallel", "arbitrary")), name=name)(a, b)


def matmul_nt_norm_bwd(dy, w, x, nw, dres, name):
    t_len = x.shape[0]
    blocked = w.ndim == 3
    k_len = dy.shape[1]
    kb = k_len // N_DEV
    tm = min(512, t_len)

    def body(dy_ref, w_ref, x_ref, nw_ref, dres_ref, dx_ref, dxb_ref, dnw_ref):
        @pl.when(pl.program_id(0) == 0)
        def _():
            dnw_ref[...] = jnp.zeros_like(dnw_ref)

        if blocked:
            dh = _nt(dy_ref[:, 0:kb], w_ref[0])
            for j in range(1, N_DEV):
                dh = dh + _nt(dy_ref[:, j * kb:(j + 1) * kb], w_ref[j])
        else:
            dh = _nt(dy_ref[...], w_ref[...])
        xv = x_ref[...]
        r = lax.rsqrt(jnp.mean(xv * xv, axis=-1, keepdims=True) + EPS)
        xh = xv * r
        dnw_ref[0:1, :] += jnp.sum(dh * xh, axis=0, keepdims=True)
        g = dh * nw_ref[...]
        dx = dres_ref[...] + r * (g - xh * jnp.mean(g * xh, axis=-1, keepdims=True))
        dx_ref[...] = dx
        dxb_ref[...] = dx.astype(BF16)

    return pl.pallas_call(
        body, grid=(t_len // tm,),
        in_specs=[pl.BlockSpec((tm, k_len), lambda i: (i, 0)),
                  _resident(w.shape),
                  pl.BlockSpec((tm, D), lambda i: (i, 0)),
                  _resident((1, D)),
                  pl.BlockSpec((tm, D), lambda i: (i, 0))],
        out_specs=[pl.BlockSpec((tm, D), lambda i: (i, 0)),
                   pl.BlockSpec((tm, D), lambda i: (i, 0)),
                   pl.BlockSpec((8, D), lambda i: (0, 0))],
        out_shape=[SDS((t_len, D), F32), SDS((t_len, D), BF16), SDS((8, D), F32)],
        compiler_params=_cparams(("arbitrary",)), name=name)(dy, w, x, nw, dres)


def loss_head(x, fw, tgt):
    t_len = x.shape[0]
    tm = min(512, t_len)

    def body(x_ref, fw_ref, t_ref, loss_ref, dx_ref, dxb_ref, dfw_ref):
        @pl.when(pl.program_id(0) == 0)
        def _():
            loss_ref[...] = jnp.zeros_like(loss_ref)
            dfw_ref[...] = jnp.zeros_like(dfw_ref)
        xv = x_ref[...]
        r = lax.rsqrt(jnp.mean(xv * xv, axis=-1, keepdims=True) + EPS)
        xh = xv * r
        w = fw_ref[...]
        e = xh * w - t_ref[...]
        row = jnp.sum(e * e, axis=-1, keepdims=True) * (1.0 / D)
        loss_ref[...] += 0.5 * jnp.sum(row, axis=0, keepdims=True)
        dyf = e * (1.0 / D)
        dfw_ref[0:1, :] += jnp.sum(dyf * xh, axis=0, keepdims=True)
        g = dyf * w
        dx = r * (g - xh * jnp.mean(g * xh, axis=-1, keepdims=True))
        dx_ref[...] = dx
        dxb_ref[...] = dx.astype(BF16)

    return pl.pallas_call(
        body, grid=(t_len // tm,),
        in_specs=[pl.BlockSpec((tm, D), lambda i: (i, 0)),
                  pl.BlockSpec((1, D), lambda i: (0, 0)),
                  pl.BlockSpec((tm, D), lambda i: (i, 0))],
        out_specs=[pl.BlockSpec((8, LANES), lambda i: (0, 0)),
                   pl.BlockSpec((tm, D), lambda i: (i, 0)),
                   pl.BlockSpec((tm, D), lambda i: (i, 0)),
                   pl.BlockSpec((8, D), lambda i: (0, 0))],
        out_shape=[SDS((8, LANES), F32), SDS((t_len, D), F32), SDS((t_len, D), BF16), SDS((8, D), F32)],
        compiler_params=_cparams(("arbitrary",)), name="loss_head")(x, fw, tgt)


TAP_SHIFTS = (3, 2, 1)


def _shift_matrix(n, up):
    r = lax.broadcasted_iota(jnp.int32, (n, n), 0)
    c = lax.broadcasted_iota(jnp.int32, (n, n), 1)
    return jnp.concatenate([jnp.where(c == (r + j if up else r - j), 1.0, 0.0).astype(BF16) for j in TAP_SHIFTS],
                           axis=0)


def _shifts_dn(xb, halo, sm, n_shifts):
    n = xb.shape[0]
    first = len(TAP_SHIFTS) - n_shifts
    moved = _nn(sm[first * n:], xb)
    row = lax.broadcasted_iota(jnp.int32, halo.shape, 0)
    outs = []
    for k in range(n_shifts):
        j = TAP_SHIFTS[first + k]
        o = moved[k * n:(k + 1) * n]
        top = jnp.where(row < j, pltpu.roll(halo, j, 0), o[0:8])
        outs.append(jnp.concatenate([top, o[8:]], axis=0))
    return outs


def _shifts_up(xb, nxt, sm, n_shifts):
    n = xb.shape[0]
    first = len(TAP_SHIFTS) - n_shifts
    moved = _nn(sm[first * n:], xb)
    row = lax.broadcasted_iota(jnp.int32, nxt.shape, 0)
    outs = []
    for k in range(n_shifts):
        j = TAP_SHIFTS[first + k]
        o = moved[k * n:(k + 1) * n]
        bot = jnp.where(row >= 8 - j, pltpu.roll(nxt, 8 - j, 0), o[n - 8:n])
        outs.append(jnp.concatenate([o[:n - 8], bot], axis=0))
    return outs


def _conv_fwd(x, xb, halo, w_ref, kw, sm):
    shifted = _shifts_dn(xb, halo, sm, kw - 1)
    acc = w_ref[kw - 1:kw, :] * x
    for k in range(kw - 1):
        acc = acc + w_ref[k:k + 1, :] * shifted[k]
    return acc


def _chunk_cumsum(a, pos):
    for sh in (1, 2, 4, 8, 16, 32):
        a = a + jnp.where(pos >= sh, pltpu.roll(a, sh, 0), 0.0)
    return a


def _chunk_rcumsum(a, pos):
    n = a.shape[0]
    for sh in (1, 2, 4, 8, 16, 32):
        a = a + jnp.where(pos < CHUNK - sh, pltpu.roll(a, n - sh, 0), 0.0)
    return a


def _softplus(v):
    return jnp.maximum(v, 0.0) + jnp.log(1.0 + jnp.exp(-jnp.abs(v)))


def _silu(v):
    return v * _sigmoid(v)


def _dsilu(v):
    s = _sigmoid(v)
    return s * (1.0 + v * (1.0 - s))


def _lane_masks(width=D):
    lane = lax.broadcasted_iota(jnp.int32, (CHUNK, width), 1) & (HDIM - 1)
    row = lax.broadcasted_iota(jnp.int32, (CHUNK, width), 0)
    return lane == row, lane <= row


def _rep_matrix():
    lane = lax.broadcasted_iota(jnp.int32, (CHUNK, 512), 1) & (HDIM - 1)
    row = lax.broadcasted_iota(jnp.int32, (CHUNK, 512), 0)
    return jnp.where(lane == row, 1.0, 0.0).astype(BF16)


def _blockdiag(xp):
    lane = lax.broadcasted_iota(jnp.int32, xp.shape, 1)
    zero = jnp.zeros_like(xp)
    return jnp.concatenate([jnp.where(lane < HDIM, xp, zero), jnp.where(lane >= HDIM, xp, zero)], axis=0)


def _mixer_views(tt):
    r8 = tt // 8

    def main(width, col):
        return pl.BlockSpec((tt, width), lambda i, c=col // width: (i, c))

    def halo(width, col):
        return pl.BlockSpec((8, width), lambda i, c=col // width: (jnp.maximum(i * r8 - 1, 0), c))

    return main, halo


def mixer_fwd(proj, prm, tt):
    t_len = proj.shape[0]
    nblk = t_len // tt
    nc = tt // CHUNK
    main, halo = _mixer_views(tt)

    def body(ub_ref, uc_ref, uh_ref, z_ref, xr_ref, bcr_ref, dtr_ref, uch_ref, uhh_ref, xrh_ref, bcrh_ref,
             scw_ref, cwx_ref, cwbc_ref, cbx_ref, cbbc_ref, dtb_ref, alog_ref, dsk_ref, nrm_ref, eh_ref,
             y_ref, st_ref, aux_ref, hs, xs_s, bc_s, dtx_s, cumx_s, yssd_s):
        i = pl.program_id(0)
        first = i == 0

        @pl.when(first)
        def _():
            hs[...] = jnp.zeros_like(hs)

        keep = jnp.where(first, 0.0, 1.0)
        sm = _shift_matrix(tt, False)
        v = uc_ref[...].astype(F32) * uh_ref[...].astype(F32)
        vh = uch_ref[...].astype(F32) * uhh_ref[...].astype(F32) * keep
        cv = _conv_fwd(v, v.astype(BF16), vh, scw_ref, 3, sm)
        aux_ref[:, A_CV:A_CV + D] = cv.astype(BF16)
        y_ref[:, 0:D] = (ub_ref[...].astype(F32) * cv).astype(BF16)

        xrb = xr_ref[...]
        pre_x = _conv_fwd(xrb.astype(F32), xrb, xrh_ref[...].astype(F32) * keep, cwx_ref, 4, sm) + cbx_ref[...]
        aux_ref[:, A_PX:A_PX + D] = pre_x.astype(BF16)
        xs_s[...] = _silu(pre_x)
        bcrb = bcr_ref[...]
        pre_bc = _conv_fwd(bcrb.astype(F32), bcrb, bcrh_ref[...].astype(F32) * keep, cwbc_ref, 4, sm) + cbbc_ref[...]
        aux_ref[:, A_PBC:A_PBC + 512] = pre_bc.astype(BF16)
        bc_s[...] = _silu(pre_bc)
        dt = _softplus(dtr_ref[...].astype(F32) + dtb_ref[...])
        a_neg = -jnp.exp(alog_ref[...])
        pos = lax.broadcasted_iota(jnp.int32, (tt, LANES), 0) & (CHUNK - 1)
        cum = _chunk_cumsum(dt * a_neg, pos)
        eh = eh_ref[...]
        dtx_s[...] = _expand(dt, eh)
        cumx_s[...] = _expand(cum, eh)
        irep, causal = _lane_masks()
        rep = _rep_matrix()

        def chunk(c, carry):
            r0 = pl.multiple_of(c * CHUNK, CHUNK)
            rows = pl.ds(r0, CHUNK)
            cumx = cumx_s[rows, :]
            cum_l = cumx[CHUNK - 1:CHUNK, :]
            xd = xs_s[rows, :] * dtx_s[rows, :]
            xf = xd * jnp.exp(cum_l - cumx)
            ex = jnp.exp(cumx)
            e_l = jnp.exp(cum_l)
            rvec = jnp.sum(jnp.where(irep, cumx, 0.0), axis=0, keepdims=True)
            lam = jnp.exp(jnp.where(causal, cumx - rvec, NEG_BIG))
            bc = bc_s[rows, :]
            for g in range(2):
                gs = slice(g * 512, (g + 1) * 512)
                bg = bc[:, g * NSTATE:(g + 1) * NSTATE].astype(BF16)
                cg = bc[:, 256 + g * NSTATE:256 + (g + 1) * NSTATE].astype(BF16)
                s_rep = _nn(_nt(cg, bg).astype(BF16), rep)
                m_g = (s_rep * lam[:, gs]).astype(BF16)
                h_g = hs[:, gs]
                h_b = h_g.astype(BF16)
                st_ref[c, :, gs] = h_b
                yo = _nn(cg, h_b) * ex[:, gs]
                xd_b = xd[:, gs].astype(BF16)
                for hp in range(4):
                    ps = slice(hp * LANES, (hp + 1) * LANES)
                    yd = _nn(m_g[:, ps], _blockdiag(xd_b[:, ps]))
                    yssd_s[rows, g * 512 + hp * LANES:g * 512 + (hp + 1) * LANES] = yd + yo[:, ps]
                hs[:, gs] = h_g * e_l[:, gs] + _tn(bg, xf[:, gs].astype(BF16))
            return carry

        lax.fori_loop(0, nc, chunk, 0, unroll=True)

        ys = yssd_s[...] + dsk_ref[...] * xs_s[...]
        aux_ref[:, A_YS:A_YS + D] = ys.astype(BF16)
        gt = ys * _silu(z_ref[...].astype(F32))
        for g in range(2):
            gs = slice(g * 512, (g + 1) * 512)
            gg = gt[:, gs]
            rn = lax.rsqrt(jnp.mean(gg * gg, axis=-1, keepdims=True) + EPS)
            y_ref[:, D + g * 512:D + (g + 1) * 512] = (gg * rn * nrm_ref[:, gs]).astype(BF16)

    params = [prm[k] for k in ("scw", "cwx", "cwbc", "cbx", "cbbc", "dtb", "alog", "dskx", "nrm", "eh")]
    in_specs = [main(D, C_UB), main(D, C_UC), main(D, C_UH), main(D, C_Z), main(D, C_XS), main(512, C_BC),
                main(LANES, C_DT), halo(D, C_UC), halo(D, C_UH), halo(D, C_XS), halo(512, C_BC)]
    in_specs += [_param_spec(a, prm["layer"]) for a in params]
    return pl.pallas_call(
        body, grid=(nblk,),
        in_specs=in_specs,
        out_specs=[pl.BlockSpec((tt, MIX), lambda i: (i, 0)),
                   pl.BlockSpec((nc, NSTATE, D), lambda i: (i, 0, 0)),
                   pl.BlockSpec((tt, AUX_W), lambda i: (i, 0))],
        out_shape=[SDS((t_len, MIX), BF16), SDS((t_len // CHUNK, NSTATE, D), BF16), SDS((t_len, AUX_W), BF16)],
        scratch_shapes=[pltpu.VMEM((NSTATE, D), F32), pltpu.VMEM((tt, D), F32), pltpu.VMEM((tt, 512), F32),
                        pltpu.VMEM((tt, D), F32), pltpu.VMEM((tt, D), F32), pltpu.VMEM((tt, D), F32)],
        compiler_params=_cparams(("arbitrary",)), name="mixer_fwd")(*([proj] * 11), *params)


def mixer_bwd(proj, dy, states, aux, prm, tt):
    t_len = proj.shape[0]
    nblk = t_len // tt
    nc = tt // CHUNK

    def rev(i):
        return nblk - 1 - i

    def main(width, col):
        return pl.BlockSpec((tt, width), lambda i, c=col // width: (rev(i), c))

    def body(ub_ref, uc_ref, uh_ref, z_ref, xr_ref, bcr_ref, dtr_ref, dy_ref, st_ref, aux_ref,
             scw_ref, cwx_ref, cwbc_ref, cbx_ref, cbbc_ref, dtb_ref, alog_ref, dsk_ref, nrm_ref, eh_ref, eht_ref,
             dp_ref, gscw_ref, gcwx_ref, gcwbc_ref, gvec_ref, gdt_ref,
             dhs, xs_s, bc_s, dtx_s, cumx_s, dys_s, dxs_s, dbc_s, red_s, ddtx_s, nx_cv, nx_px, nx_pbc, sgx_s, sgbc_s):
        i = pl.program_id(0)

        @pl.when(i == 0)
        def _():
            dhs[...] = jnp.zeros_like(dhs)
            nx_cv[...] = jnp.zeros_like(nx_cv)
            nx_px[...] = jnp.zeros_like(nx_px)
            nx_pbc[...] = jnp.zeros_like(nx_pbc)
            gscw_ref[...] = jnp.zeros_like(gscw_ref)
            gcwx_ref[...] = jnp.zeros_like(gcwx_ref)
            gcwbc_ref[...] = jnp.zeros_like(gcwbc_ref)
            gvec_ref[...] = jnp.zeros_like(gvec_ref)
            gdt_ref[...] = jnp.zeros_like(gdt_ref)

        uc = uc_ref[...].astype(F32)
        uh = uh_ref[...].astype(F32)
        v = uc * uh
        dya = dy_ref[:, 0:D].astype(F32)
        dp_ref[:, C_UB:C_UB + D] = (dya * aux_ref[:, A_CV:A_CV + D].astype(F32)).astype(BF16)
        dcv = dya * ub_ref[...].astype(F32)
        sm = _shift_matrix(tt, True)
        ups = _shifts_up(dcv.astype(BF16), nx_cv[...], sm, 2) + [dcv]
        dv = None
        for k in range(3):
            gscw_ref[k:k + 1, :] += jnp.sum(v * ups[k], axis=0, keepdims=True)
            term = scw_ref[k:k + 1, :] * ups[k]
            dv = term if dv is None else dv + term
        nx_cv[...] = dcv[0:8]
        dp_ref[:, C_UC:C_UC + D] = (dv * uh).astype(BF16)
        dp_ref[:, C_UH:C_UH + D] = (dv * uc).astype(BF16)

        pre_x = aux_ref[:, A_PX:A_PX + D].astype(F32)
        pre_bc = aux_ref[:, A_PBC:A_PBC + 512].astype(F32)
        sg_x = _sigmoid(pre_x)
        sg_bc = _sigmoid(pre_bc)
        sgx_s[...] = sg_x
        sgbc_s[...] = sg_bc
        xs = pre_x * sg_x
        xs_s[...] = xs
        bc_s[...] = pre_bc * sg_bc
        dt_pre = dtr_ref[...].astype(F32) + dtb_ref[...]
        dt = _softplus(dt_pre)
        a_neg = -jnp.exp(alog_ref[...])
        pos = lax.broadcasted_iota(jnp.int32, (tt, LANES), 0) & (CHUNK - 1)
        cum = _chunk_cumsum(dt * a_neg, pos)
        eh = eh_ref[...]
        eht = eht_ref[...]
        dtx_s[...] = _expand(dt, eh)
        cumx_s[...] = _expand(cum, eh)

        irep, causal = _lane_masks()
        irep_g, _ = _lane_masks(512)
        rep = _rep_matrix()
        row8 = lax.broadcasted_iota(jnp.int32, (8, 512), 0)
        lane128 = lax.broadcasted_iota(jnp.int32, (CHUNK, LANES), 1)

        z = z_ref[...].astype(F32)
        sg_z = _sigmoid(z)
        sz = z * sg_z
        dsz = sg_z * (1.0 + z * (1.0 - sg_z))
        ys = aux_ref[:, A_YS:A_YS + D].astype(F32)
        gt = ys * sz
        dyb = dy_ref[:, D:MIX].astype(F32)
        for g in range(2):
            gs = slice(g * 512, (g + 1) * 512)
            gg = gt[:, gs]
            rn = lax.rsqrt(jnp.mean(gg * gg, axis=-1, keepdims=True) + EPS)
            gvec_ref[0:1, gs] += jnp.sum(dyb[:, gs] * gg * rn, axis=0, keepdims=True)
            dgn = dyb[:, gs] * nrm_ref[:, gs]
            dgt = rn * (dgn - gg * (rn * rn) * jnp.mean(dgn * gg, axis=-1, keepdims=True))
            dys = dgt * sz[:, gs]
            dys_s[:, gs] = dys
            dp_ref[:, C_Z + g * 512:C_Z + (g + 1) * 512] = (dgt * ys[:, gs] * dsz[:, gs]).astype(BF16)
        dys_all = dys_s[...]
        gvec_ref[1:2, :] += jnp.sum(dys_all * xs, axis=0, keepdims=True)

        def bwd_chunk(cc, carry):
            c = nc - 1 - cc
            r0 = pl.multiple_of(c * CHUNK, CHUNK)
            rows = pl.ds(r0, CHUNK)
            cumx = cumx_s[rows, :]
            cum_l = cumx[CHUNK - 1:CHUNK, :]
            xs_c = xs_s[rows, :]
            dtx = dtx_s[rows, :]
            xd = xs_c * dtx
            f = jnp.exp(cum_l - cumx)
            xf = xd * f
            ex = jnp.exp(cumx)
            e_l = jnp.exp(cum_l)
            rvec = jnp.sum(jnp.where(irep, cumx, 0.0), axis=0, keepdims=True)
            lam = jnp.exp(jnp.where(causal, cumx - rvec, NEG_BIG))
            bc = bc_s[rows, :]
            dyc = dys_s[rows, :]
            for g in range(2):
                gs = slice(g * 512, (g + 1) * 512)
                bg = bc[:, g * NSTATE:(g + 1) * NSTATE].astype(BF16)
                cg = bc[:, 256 + g * NSTATE:256 + (g + 1) * NSTATE].astype(BF16)
                h0 = st_ref[c, :, gs]
                dh = dhs[:, gs]
                dh_b = dh.astype(BF16)
                xf_g = xf[:, gs]
                dxf = _nn(bg, dh_b)
                db = _nt(xf_g.astype(BF16), dh_b)
                s_rep = _nn(_nt(cg, bg).astype(BF16), rep)
                lam_g = lam[:, gs]
                m_g = s_rep * lam_g
                m_b = m_g.astype(BF16)
                ex_g = ex[:, gs]
                dy_g = dyc[:, gs]
                yo = _nn(cg, h0) * ex_g
                dg_b = (dy_g * ex_g).astype(BF16)
                dc = _nt(dg_b, h0)
                el_g = e_l[:, gs]
                dee = jnp.sum(dh * h0.astype(F32), axis=0, keepdims=True) * el_g
                dhs[:, gs] = dh * el_g + _tn(cg, dg_b)
                xd_b = xd[:, gs].astype(BF16)
                dy_b = dy_g.astype(BF16)
                dm_parts, dxd_parts = [], []
                for hp in range(4):
                    ps = slice(hp * LANES, (hp + 1) * LANES)
                    bd = _blockdiag(xd_b[:, ps])
                    dm_parts.append(_nt(dy_b[:, ps], bd))
                    t2 = _tn(m_b[:, ps], dy_b[:, ps])
                    dxd_parts.append(jnp.where(lane128 < HDIM, t2[0:CHUNK], t2[CHUNK:2 * CHUNK]))
                dm = jnp.concatenate(dm_parts, axis=1)
                dxd = jnp.concatenate(dxd_parts, axis=1) + dxf * f[:, gs]
                dseg = dm * m_g
                ds_b = _nt((dm * lam_g).astype(BF16), rep).astype(BF16)
                dc = dc + _nn(ds_b, bg)
                db = db + _tn(ds_b, cg)
                colsum = jnp.sum(dseg, axis=0, keepdims=True)
                dxfxf = dxf * xf_g
                red = dseg - jnp.where(irep_g, colsum, 0.0) + dy_g * yo - dxfxf
                last = jnp.sum(dxfxf, axis=0, keepdims=True) + dee
                red_s[rows, gs] = red
                tail = pl.ds(pl.multiple_of(r0 + CHUNK - 8, 8), 8)
                red_s[tail, gs] += jnp.where(row8 == 7, last, 0.0)
                ddtx_s[rows, gs] = dxd * xs_c[:, gs]
                dxs_s[rows, gs] = dxd * dtx[:, gs] + dsk_ref[:, gs] * dy_g
                dbc_s[rows, g * NSTATE:(g + 1) * NSTATE] = db
                dbc_s[rows, 256 + g * NSTATE:256 + (g + 1) * NSTATE] = dc
            return carry

        lax.fori_loop(0, nc, bwd_chunk, 0, unroll=True)

        dcum = _head_reduce(red_s[...], eht)
        da = _chunk_rcumsum(dcum, pos)
        ddt = _head_reduce(ddtx_s[...], eht) + da * a_neg
        gdt_ref[1:2, :] += jnp.sum(da * dt, axis=0, keepdims=True) * a_neg
        ddt_raw = ddt * _sigmoid(dt_pre)
        lane_t = lax.broadcasted_iota(jnp.int32, (tt, LANES), 1)
        ddt_raw = jnp.where(lane_t < NHEAD, ddt_raw, 0.0)
        gdt_ref[0:1, :] += jnp.sum(ddt_raw, axis=0, keepdims=True)
        dp_ref[:, C_DT:C_DT + LANES] = ddt_raw.astype(BF16)

        sg_x = sgx_s[...]
        sg_bc = sgbc_s[...]
        pre_x = aux_ref[:, A_PX:A_PX + D].astype(F32)
        pre_bc = aux_ref[:, A_PBC:A_PBC + 512].astype(F32)
        dpx = dxs_s[...] * (sg_x * (1.0 + pre_x * (1.0 - sg_x)))
        dpbc = dbc_s[...] * (sg_bc * (1.0 + pre_bc * (1.0 - sg_bc)))
        gvec_ref[2:3, :] += jnp.sum(dpx, axis=0, keepdims=True)
        gcwbc_ref[4:5, :] += jnp.sum(dpbc, axis=0, keepdims=True)
        xraw = xr_ref[...].astype(F32)
        bcraw = bcr_ref[...].astype(F32)
        ups_x = _shifts_up(dpx.astype(BF16), nx_px[...], sm, 3) + [dpx]
        ups_bc = _shifts_up(dpbc.astype(BF16), nx_pbc[...], sm, 3) + [dpbc]
        dxr, dbcr = None, None
        for k in range(4):
            up_x = ups_x[k]
            up_bc = ups_bc[k]
            gcwx_ref[k:k + 1, :] += jnp.sum(xraw * up_x, axis=0, keepdims=True)
            gcwbc_ref[k:k + 1, :] += jnp.sum(bcraw * up_bc, axis=0, keepdims=True)
            tx = cwx_ref[k:k + 1, :] * up_x
            tb = cwbc_ref[k:k + 1, :] * up_bc
            dxr = tx if dxr is None else dxr + tx
            dbcr = tb if dbcr is None else dbcr + tb
        nx_px[...] = dpx[0:8]
        nx_pbc[...] = dpbc[0:8]
        dp_ref[:, C_XS:C_XS + D] = dxr.astype(BF16)
        dp_ref[:, C_BC:C_BC + 512] = dbcr.astype(BF16)

        @pl.when(i == nblk - 1)
        def _():
            gdt_ref[2:3, :] = _head_reduce(gvec_ref[1:2, :] * jnp.ones((8, 1), F32), eht)[0:1, :]

    def const(shape):
        return pl.BlockSpec(shape, lambda i: (0, 0))

    params = [prm[k] for k in ("scw", "cwx", "cwbc", "cbx", "cbbc", "dtb", "alog", "dskx", "nrm", "eh", "eht")]
    in_specs = [main(D, C_UB), main(D, C_UC), main(D, C_UH), main(D, C_Z), main(D, C_XS), main(512, C_BC),
                main(LANES, C_DT),
                pl.BlockSpec((tt, MIX), lambda i: (rev(i), 0)),
                pl.BlockSpec((nc, NSTATE, D), lambda i: (rev(i), 0, 0)),
                pl.BlockSpec((tt, AUX_W), lambda i: (rev(i), 0))]
    in_specs += [_param_spec(a, prm["layer"]) for a in params]
    return pl.pallas_call(
        body, grid=(nblk,),
        in_specs=in_specs,
        out_specs=[pl.BlockSpec((tt, NINP), lambda i: (rev(i), 0)),
                   const((8, D)), const((8, D)), const((8, 512)), const((8, D)), const((8, LANES))],
        out_shape=[SDS((t_len, NINP), BF16), SDS((8, D), F32), SDS((8, D), F32), SDS((8, 512), F32),
                   SDS((8, D), F32), SDS((8, LANES), F32)],
        scratch_shapes=[pltpu.VMEM((NSTATE, D), F32),
                        pltpu.VMEM((tt, D), F32), pltpu.VMEM((tt, 512), F32),
                        pltpu.VMEM((tt, D), F32), pltpu.VMEM((tt, D), F32),
                        pltpu.VMEM((tt, D), F32), pltpu.VMEM((tt, D), F32),
                        pltpu.VMEM((tt, 512), F32),
                        pltpu.VMEM((tt, D), F32), pltpu.VMEM((tt, D), F32),
                        pltpu.VMEM((8, D), F32), pltpu.VMEM((8, D), F32), pltpu.VMEM((8, 512), F32),
                        pltpu.VMEM((tt, D), F32), pltpu.VMEM((tt, 512), F32)],
        compiler_params=_cparams(("arbitrary",)), name="mixer_bwd")(
            *([proj] * 7), dy, states, aux, *params)


TN_IN = 1152
DW_TOKENS = 4096


def layer_fwd_mix(x, lw, prm, tt):
    proj, h1 = norm_matmul(x, lw["nw1"], lw["win"], "in_proj")
    y, st, aux = mixer_fwd(proj, prm, tt)
    return h1, proj, (st, aux), y


def layer_fwd_mlp(x, mixed, lw):
    h1, proj, st, y = mixed
    x1 = matmul_residual(y, lw["wout"], x, False, "out_proj")
    u, h2 = norm_matmul(x1, lw["nw2"], lw["wup"], "up_proj")
    x2 = matmul_residual(u, lw["wdn"], x1, True, "down_proj")
    return x2, (x, h1, proj, st, y, x1, h2, u)


def layer_fwd(x, lw, prm, tt):
    return layer_fwd_mlp(x, layer_fwd_mix(x, lw, prm, tt), lw)


def _dw(a, b, a_cols, b_cols, relu2, name, tt_max=2048):
    m_len, n_len = a.shape[1], b.shape[1]
    n_a, n_b = m_len // a_cols, n_len // b_cols
    assert n_a == 1 or n_b == 1
    if n_b == 1:
        return matmul_tn(
            a, b,
            lambda t_: pl.BlockSpec((t_, a_cols), lambda n, t: (t, n)),
            lambda t_: pl.BlockSpec((t_, n_len), lambda n, t: (t, 0)),
            pl.BlockSpec((a_cols, n_len), lambda n, t: (n, 0)), SDS((m_len, n_len), BF16), n_a, relu2, name, tt_max)
    return matmul_tn(
        a, b,
        lambda t_: pl.BlockSpec((t_, m_len), lambda n, t: (t, 0)),
        lambda t_: pl.BlockSpec((t_, b_cols), lambda n, t: (t, n)),
        pl.BlockSpec((m_len, b_cols), lambda n, t: (0, n)), SDS((m_len, n_len), BF16), n_b, relu2, name, tt_max)


def layer_bwd_mlp(dx2, dx2b, lw, saved):
    _, _, _, _, y, x1, h2, u = saved
    du = matmul_nt_act(dx2b, lw["wdn"], u, "mlp_bwd_du")
    g_wdn = _dw(u, dx2b, 1024, D, True, "dw_down")
    dx1, dx1b, g_nw2 = matmul_nt_norm_bwd(du, lw["wup"], x1, lw["nw2"], dx2, "mlp_bwd_dx")
    cb = DFF // N_DEV
    g_wup = matmul_tn(
        h2, du,
        lambda t_: pl.BlockSpec((t_, D), lambda n, t: (t, 0)),
        lambda t_: pl.BlockSpec((t_, 2 * cb), lambda n, t: (t, n)),
        pl.BlockSpec((2, D, cb), lambda n, t: (n, 0, 0)), SDS((N_DEV, D, cb), BF16), N_DEV // 2, False, "dw_up",
        DW_TOKENS)
    dy = matmul_nt_act(dx1b, lw["wout"], None, "out_bwd_dy")
    g_wout = _dw(y, dx1b, 1024, D, False, "dw_out", DW_TOKENS)
    return dx1, dx1b, dy, {"wout": g_wout, "wup": g_wup, "wdn": g_wdn, "nw2": g_nw2[0]}


def layer_bwd_mix(dx1, dy, lw, prm, saved, tt):
    x, h1, proj, st = saved[:4]
    dproj, gscw, gcwx, gcwbc, gvec, gdt = mixer_bwd(proj, dy, st[0], st[1], prm, tt)
    dx0, dx0b, g_nw1 = matmul_nt_norm_bwd(dproj, lw["win"], x, lw["nw1"], dx1, "in_bwd_dx")
    g_win = _dw(h1, dproj, D, TN_IN, False, "dw_in", DW_TOKENS)
    grads = {
        "win": g_win, "scw": gscw[0:3], "cw": jnp.concatenate([gcwx[0:4], gcwbc[0:4]], axis=1),
        "cb": jnp.concatenate([gvec[2], gcwbc[4]], axis=0),
        "dtb": gdt[0, :NHEAD], "alog": gdt[1, :NHEAD], "dsk": gdt[2, :NHEAD],
        "nrm": gvec[0], "nw1": g_nw1[0],
    }
    return dx0, dx0b, grads


def layer_bwd(dx2, dx2b, lw, prm, saved, tt):
    dx1, dx1b, dy, g_mlp = layer_bwd_mlp(dx2, dx2b, lw, saved)
    dx0, dx0b, g_mix = layer_bwd_mix(dx1, dy, lw, prm, saved, tt)
    return dx0, dx0b, {**g_mlp, **g_mix}


def stacked_params(conv_b, dt_bias, a_log, d_skip, ssd_norm_w):
    def lanes128(a):
        return jnp.pad(a, ((0, 0), (0, LANES - a.shape[1])))[:, None, :]

    return {"cbx": conv_b[:, None, :D], "cbbc": conv_b[:, None, D:], "dtb": lanes128(dt_bias),
            "alog": lanes128(a_log), "dskx": jnp.repeat(d_skip, HDIM, axis=1)[:, None, :],
            "nrm": ssd_norm_w[:, None, :]}


def layer_params(layer, win, scw, cw, nw1, nw2, stacked, eh, eht):
    def rows8(a):
        return jnp.pad(a, ((0, 8 - a.shape[0]), (0, 0)))

    lw = {"win": win, "nw1": nw1[None, :], "nw2": nw2[None, :]}
    prm = dict(stacked, layer=layer, scw=rows8(scw), cwx=rows8(cw[:, :D]), cwbc=rows8(cw[:, D:]), eh=eh, eht=eht)
    return lw, prm


def _param_spec(arr, layer):
    if arr.ndim == 3:
        return pl.BlockSpec((None,) + arr.shape[1:], lambda i: (layer, 0, 0))
    return pl.BlockSpec(arr.shape, lambda i: (0, 0))


def _flip(v, bit):
    return 1 - v if bit else v


def all_gather(arrs, name):
    n = len(arrs)

    def body(*refs):
        ins, outs = refs[:n], refs[n:2 * n]
        send_sems, recv_sems, local_sems = refs[2 * n:]
        x, y, c = lax.axis_index("x"), lax.axis_index("y"), lax.axis_index("c")
        sibling = (x, y, 1 - c)
        chips = [(1 - x, y), (x, 1 - y), (1 - x, 1 - y)]

        def idx(px, py, pc):
            return 4 * px + 2 * py + pc

        def copy(a, k, block, to, src=None):
            dst = outs[a].at[idx(*block)]
            return pltpu.make_async_remote_copy(
                src_ref=dst if src is None else src, dst_ref=dst,
                send_sem=send_sems.at[a, k], recv_sem=recv_sems.at[a, k], device_id=to, device_id_type=MESH)

        me = (x, y, c)
        mine = [pltpu.make_async_copy(ins[a], outs[a].at[idx(*me)], local_sems.at[a]) for a in range(n)]
        for cp in mine:
            cp.start()
        first = []
        for a in range(n):
            first.append(copy(a, 0, me, sibling, src=ins[a]))
            first += [copy(a, 1 + j, me, (*chip, c), src=ins[a]) for j, chip in enumerate(chips)]
        for cp in first:
            cp.start()
        passed = []
        for j, chip in enumerate(chips):
            for a in range(n):
                copy(a, 1 + j, (*chip, c), me).wait_recv()
                cp = copy(a, 4 + j, (*chip, c), sibling)
                cp.start()
                passed.append(cp)
        for a in range(n):
            copy(a, 0, sibling, me).wait_recv()
            for j, chip in enumerate(chips):
                copy(a, 4 + j, (*chip, 1 - c), me).wait_recv()
        for cp in first + passed:
            cp.wait_send()
        for cp in mine:
            cp.wait()

    any_spec = pl.BlockSpec(memory_space=pl.ANY)
    return pl.pallas_call(
        body, in_specs=[any_spec] * n, out_specs=[any_spec] * n,
        out_shape=[SDS((N_DEV,) + a.shape, a.dtype) for a in arrs],
        scratch_shapes=[pltpu.SemaphoreType.DMA((n, 7)), pltpu.SemaphoreType.DMA((n, 7)),
                        pltpu.SemaphoreType.DMA((n,))],
        name=name)(*arrs)


HBM_SPEC = pl.BlockSpec(memory_space=pltpu.HBM)
SEM_SPEC = pl.BlockSpec(memory_space=pltpu.SEMAPHORE)
SIDE_EFFECT = pltpu.SideEffectType.DATAFLOW_SIDE_EFFECTING
N_PEER = N_DEV - 1


def _peer(mask):
    x, y, c = lax.axis_index("x"), lax.axis_index("y"), lax.axis_index("c")
    return _flip(x, mask & 4), _flip(y, mask & 2), _flip(c, mask & 1)


ALL_PEERS = tuple(range(1, N_DEV))
SIBLING_AND_CHIPS = (1, 2, 4, 6)


def exchange_start(srcs, per_peer, name, after=None, masks=ALL_PEERS):
    n = len(srcs)
    npeer = len(masks)
    lands = [SDS((N_DEV,) + (a.shape[1:] if per_peer else a.shape), a.dtype) for a in srcs]
    n_in = 2 * n + (after is not None)

    def body(*refs):
        src_refs, land_refs = refs[:n], refs[n:2 * n]
        send_sems, recv_sems = refs[n_in], refs[n_in + 1]
        token = refs[-1]
        x, y, c = lax.axis_index("x"), lax.axis_index("y"), lax.axis_index("c")
        me = 4 * x + 2 * y + c
        for a in range(n):
            for k, mask in enumerate(masks):
                px, py, pc = _peer(mask)
                part = src_refs[a].at[4 * px + 2 * py + pc] if per_peer else src_refs[a]
                pltpu.make_async_remote_copy(
                    src_ref=part, dst_ref=land_refs[a].at[me], send_sem=send_sems.at[a * npeer + k],
                    recv_sem=recv_sems.at[a * npeer + k], device_id=(px, py, pc), device_id_type=MESH).start()
        token[...] = jnp.zeros_like(token)

    out = pl.pallas_call(
        body, name=name,
        out_shape=(pltpu.SemaphoreType.DMA((n * npeer,)), pltpu.SemaphoreType.DMA((n * npeer,)),
                   *[pltpu.HBM(a.shape, a.dtype) for a in srcs], *[pltpu.HBM(l.shape, l.dtype) for l in lands],
                   SDS((8, LANES), F32)),
        in_specs=(HBM_SPEC,) * (2 * n) + ((pl.BlockSpec(memory_space=pl.ANY),) if after is not None else ()),
        out_specs=(SEM_SPEC, SEM_SPEC) + (HBM_SPEC,) * (2 * n) + (pl.BlockSpec(memory_space=pltpu.VMEM),),
        input_output_aliases={k: 2 + k for k in range(2 * n)},
        compiler_params=pltpu.CompilerParams(has_side_effects=SIDE_EFFECT),
    )(*[pltpu.with_memory_space_constraint(a, pltpu.HBM) for a in srcs],
      *[pltpu.with_memory_space_constraint(lax.empty(l.shape, l.dtype), pltpu.HBM) for l in lands],
      *([after] if after is not None else []))
    return out[0], out[1], list(out[2:2 + n]), list(out[2 + n:2 + 2 * n]), out[-1]


def exchange_wait(started, after, per_peer, name, masks=ALL_PEERS):
    send_sems, recv_sems, srcs, lands, _ = started
    n = len(srcs)
    npeer = len(masks)

    def body(*refs):
        src_refs, land_refs = refs[:n], refs[n:2 * n]
        send_sems, recv_sems = refs[2 * n], refs[2 * n + 1]
        for k, mask in enumerate(masks):
            for a in range(n):
                copy = pltpu.make_async_remote_copy(
                    src_ref=src_refs[a].at[0] if per_peer else src_refs[a], dst_ref=land_refs[a].at[0],
                    send_sem=send_sems.at[a * npeer + k], recv_sem=recv_sems.at[a * npeer + k],
                    device_id=_peer(mask), device_id_type=MESH)
                copy.wait_send()
                copy.wait_recv()

    out = pl.pallas_call(
        body, name=name,
        out_shape=tuple(pltpu.HBM(a.shape, a.dtype) for a in srcs + lands),
        in_specs=(HBM_SPEC,) * (2 * n) + (SEM_SPEC, SEM_SPEC, pl.BlockSpec(memory_space=pl.ANY)),
        out_specs=(HBM_SPEC,) * (2 * n), input_output_aliases={k: k for k in range(2 * n)},
        compiler_params=pltpu.CompilerParams(has_side_effects=SIDE_EFFECT),
    )(*srcs, *lands, send_sems, recv_sems, after)
    return list(out[:n]), list(out[n:])


def relay_to_sibling(lands, name):
    n = len(lands)
    chips = (2, 4, 6)

    def body(*refs):
        land_refs = refs[n:2 * n]
        send_sems, recv_sems = refs[2 * n], refs[2 * n + 1]
        x, y, c = lax.axis_index("x"), lax.axis_index("y"), lax.axis_index("c")
        copies = []
        for a in range(n):
            for k, mask in enumerate(chips):
                px, py, _ = _peer(mask)
                block = land_refs[a].at[4 * px + 2 * py + c]
                cp = pltpu.make_async_remote_copy(
                    src_ref=block, dst_ref=block, send_sem=send_sems.at[a * 3 + k], recv_sem=recv_sems.at[a * 3 + k],
                    device_id=(x, y, 1 - c), device_id_type=MESH)
                cp.start()
                copies.append((cp, a, k, land_refs[a].at[4 * px + 2 * py + 1 - c]))
        for cp, a, k, arriving in copies:
            cp.wait_send()
            pltpu.make_async_remote_copy(
                src_ref=arriving, dst_ref=arriving, send_sem=send_sems.at[a * 3 + k], recv_sem=recv_sems.at[a * 3 + k],
                device_id=(x, y, 1 - c), device_id_type=MESH).wait_recv()

    any_spec = pl.BlockSpec(memory_space=pl.ANY)
    return list(pl.pallas_call(
        body, in_specs=[any_spec] * n, out_specs=[any_spec] * n,
        out_shape=[SDS(a.shape, a.dtype) for a in lands],
        input_output_aliases={k: k for k in range(n)},
        scratch_shapes=[pltpu.SemaphoreType.DMA((n * 3,)), pltpu.SemaphoreType.DMA((n * 3,))],
        name=name)(*lands))


IN_SHARD = NIN // N_DEV
SLOT_W = 768


def _slot_window(j):
    return (IN_SHARD * j // LANES) * LANES, -(-(IN_SHARD * (j + 1)) // LANES) * LANES


def _placement(j):
    a, b = _slot_window(j)
    r = lax.broadcasted_iota(jnp.int32, (SLOT_W, b - a), 0)
    c = lax.broadcasted_iota(jnp.int32, (SLOT_W, b - a), 1)
    return jnp.where(jnp.logical_and(c == r + (IN_SHARD * j - a), r < IN_SHARD), 1.0, 0.0).astype(BF16)


def assemble_w_in(land):
    tm = 256

    def body(l_ref, o_ref, acc):
        acc[...] = jnp.zeros_like(acc)
        for j in range(N_DEV):
            a, b = _slot_window(j)
            acc[:, a:b] += _nn(l_ref[j], _placement(j))
        o_ref[...] = acc[...].astype(BF16)

    return pl.pallas_call(
        body, grid=(D // tm,),
        in_specs=[pl.BlockSpec((N_DEV, tm, SLOT_W), lambda i: (0, i, 0))],
        out_specs=pl.BlockSpec((tm, NINP), lambda i: (i, 0)),
        out_shape=SDS((D, NINP), BF16),
        scratch_shapes=[pltpu.VMEM((tm, NINP), F32)],
        compiler_params=_cparams(("parallel",)), name="assemble_w_in")(land)


def scatter_w_in(dw):
    tm = 256

    def body(d_ref, o_ref):
        for j in range(N_DEV):
            a, b = _slot_window(j)
            o_ref[j] = _nt(d_ref[:, a:b], _placement(j)).astype(BF16)

    return pl.pallas_call(
        body, grid=(D // tm,),
        in_specs=[pl.BlockSpec((tm, NINP), lambda i: (i, 0))],
        out_specs=pl.BlockSpec((N_DEV, tm, SLOT_W), lambda i: (0, i, 0)),
        out_shape=SDS((N_DEV, D, SLOT_W), BF16),
        compiler_params=_cparams(("parallel",)), name="scatter_w_in")(dw)


def _adamw_math(g, w_ref, m_ref, v_ref, g_ref, d_ref, nm_ref, nv_ref):
    mn = ADAM_B1 * m_ref[...] + (1.0 - ADAM_B1) * g
    vn = ADAM_B2 * v_ref[...] + (1.0 - ADAM_B2) * jnp.square(g)
    m_hat = mn / (1.0 - ADAM_B1 ** ADAM_STEP)
    v_hat = vn / (1.0 - ADAM_B2 ** ADAM_STEP)
    g_ref[...] = g
    d_ref[...] = -ADAM_LR * (m_hat / (jnp.sqrt(v_hat) + ADAM_EPS) + ADAM_WD * w_ref[...])
    nm_ref[...] = mn
    nv_ref[...] = vn


def adamw_layers(w, slots, m, v, name):
    depth, r_len, c_len = w.shape
    cs = slots[0].shape[2]
    br = min(128, r_len)
    assert r_len % br == 0

    def body(w_ref, *rest):
        s_refs, (m_ref, v_ref, g_ref, d_ref, nm_ref, nv_ref) = rest[:depth], rest[depth:]
        layer = pl.program_id(0)
        for k in range(depth):
            @pl.when(layer == k)
            def _(k=k):
                g = s_refs[k][0, :, 0:c_len].astype(F32)
                for j in range(1, N_DEV):
                    g = g + s_refs[k][j, :, 0:c_len].astype(F32)
                _adamw_math(g, w_ref, m_ref, v_ref, g_ref, d_ref, nm_ref, nv_ref)

    spec = pl.BlockSpec((None, br, c_len), lambda l, i: (l, i, 0))
    s_specs = [pl.BlockSpec((N_DEV, br, cs), lambda l, i, k=k: (0, jnp.where(l == k, i, 0), 0))
               for k in range(depth)]
    return pl.pallas_call(
        body, grid=(depth, r_len // br),
        in_specs=[spec] + s_specs + [spec, spec],
        out_specs=[spec] * 4, out_shape=[SDS(w.shape, F32)] * 4,
        compiler_params=_cparams(("arbitrary", "arbitrary")), name=name)(w, *slots, m, v)


def adamw(w, slots, m, v, name):
    r_len, c_len = w.shape
    br = r_len if r_len <= 512 else 512
    assert r_len % br == 0

    def body(w_ref, s_ref, m_ref, v_ref, g_ref, d_ref, nm_ref, nv_ref):
        g = s_ref[0].astype(F32)
        for k in range(1, N_DEV):
            g = g + s_ref[k].astype(F32)
        _adamw_math(g, w_ref, m_ref, v_ref, g_ref, d_ref, nm_ref, nv_ref)

    spec = pl.BlockSpec((br, c_len), lambda i: (i, 0))
    return pl.pallas_call(
        body, grid=(r_len // br,),
        in_specs=[spec, pl.BlockSpec((N_DEV, br, c_len), lambda i: (0, i, 0)), spec, spec],
        out_specs=[spec] * 4, out_shape=[SDS((r_len, c_len), F32)] * 4,
        compiler_params=_cparams(("parallel",)), name=name)(w, slots, m, v)


def _adamw_nd(w, slots, m, v, name):
    shp = w.shape
    r = int(np.prod(shp[:-1]))
    outs = adamw(w.reshape(r, shp[-1]), slots.reshape(N_DEV, r, shp[-1]), m.reshape(r, shp[-1]),
                 v.reshape(r, shp[-1]), name)
    return [o.reshape(shp) for o in outs]


SMALL = [("norm_mix_w", DEPTH * D), ("ssd_conv_b", DEPTH * XBC), ("dt_bias", DEPTH * NHEAD),
         ("a_log", DEPTH * NHEAD), ("d_skip", DEPTH * NHEAD), ("ssd_norm_w", DEPTH * D),
         ("norm_mlp_w", DEPTH * D), ("final_norm_w", D)]
SMALL_LEN = sum(s for _, s in SMALL)
SMALL_ROWS = -(-SMALL_LEN // LANES)


def _pack_small(parts):
    flat = jnp.concatenate([parts[k].reshape(-1) for k, _ in SMALL])
    return jnp.pad(flat, (0, SMALL_ROWS * LANES - SMALL_LEN)).reshape(SMALL_ROWS, LANES)


def _unpack_small(packed, shapes):
    flat = packed.reshape(-1)
    out, off = {}, 0
    for k, s in SMALL:
        out[k] = flat[off:off + s].reshape(shapes[k])
        off += s
    return out


def kernel(x, norm_mix_w, w_in, short_conv_w, ssd_conv_w, ssd_conv_b, dt_bias, a_log, d_skip, ssd_norm_w, w_out, norm_mlp_w, w_up, w_down, final_norm_w, loss_target, m_norm_mix_w, m_w_in, m_short_conv_w, m_ssd_conv_w, m_ssd_conv_b, m_dt_bias, m_a_log, m_d_skip, m_ssd_norm_w, m_w_out, m_norm_mlp_w, m_w_up, m_w_down, m_final_norm_w, v_norm_mix_w, v_w_in, v_short_conv_w, v_ssd_conv_w, v_ssd_conv_b, v_dt_bias, v_a_log, v_d_skip, v_ssd_norm_w, v_w_out, v_norm_mlp_w, v_w_up, v_w_down, v_final_norm_w):
    xs = x[0]
    t_len = xs.shape[0]
    tt = min(256, t_len)
    eh, eht = _head_matrices()
    stacked = stacked_params(ssd_conv_b, dt_bias, a_log, d_skip, ssd_norm_w)
    me = 4 * lax.axis_index("x") + 2 * lax.axis_index("y") + lax.axis_index("c")

    def start_weights(i, after):
        first = exchange_start(
            [jnp.pad(w_in[i].astype(BF16), ((0, 0), (0, SLOT_W - IN_SHARD))), short_conv_w[i], ssd_conv_w[i]],
            False, "w_in_start_%d" % i, after, SIBLING_AND_CHIPS)
        rest = exchange_start([w_out[i].astype(BF16), w_up[i].astype(BF16), w_down[i].astype(BF16)], False,
                              "w_rest_start_%d" % i, first[4] if after is None else after, SIBLING_AND_CHIPS)
        return first, rest

    def fill_own(srcs, lands, per_peer):
        own = [lax.dynamic_index_in_dim(s_, me, 0, keepdims=False) for s_ in srcs] if per_peer else srcs
        return [lax.dynamic_update_index_in_dim(l_, o_, me, 0) for l_, o_ in zip(lands, own)]

    def finish_weights(started, after, name):
        srcs, lands = exchange_wait(started, after, False, name + "_wait", SIBLING_AND_CHIPS)
        return fill_own(srcs, relay_to_sibling(lands, name + "_relay"), False)

    act = xs
    saved, layers = [], []
    first, rest = start_weights(0, None)
    token = first[4][0, 0] + rest[4][0, 0]
    for i in range(DEPTH):
        g_in, g_sc, g_cw = finish_weights(first, act, "w_in_%d" % i)
        lw, prm = layer_params(
            i, assemble_w_in(g_in), g_sc.transpose(1, 0, 2).reshape(3, D), g_cw.transpose(1, 0, 2).reshape(4, XBC),
            norm_mix_w[i], norm_mlp_w[i], stacked, eh, eht)
        lw["nw1"] = lw["nw1"] + token
        mixed = layer_fwd_mix(act, lw, prm, tt)
        g_out, g_up, g_dn = finish_weights(rest, mixed[3], "w_rest_%d" % i)
        lw.update(wout=g_out.reshape(MIX, D), wup=g_up, wdn=g_dn.reshape(DFF, D))
        if i + 1 < DEPTH:
            first, rest = start_weights(i + 1, g_dn)
            token = first[4][0, 0] + rest[4][0, 0]
            lw["nw2"] = lw["nw2"] + token
        layers.append((lw, prm))
        act, sv = layer_fwd_mlp(act, mixed, lw)
        saved.append(sv)
    loss_acc, dx, dxb, g_fw = loss_head(act, final_norm_w[None, :], loss_target[0])

    grads = [None] * DEPTH
    sent_rest, sent_in = [None] * DEPTH, [None] * DEPTH
    token = None
    for i in reversed(range(DEPTH)):
        lw, prm = layers[i]
        if token is not None:
            lw = dict(lw, nw2=lw["nw2"] + token)
        dx1, _, dy, g_mlp = layer_bwd_mlp(dx, dxb, lw, saved[i])
        sent_rest[i] = exchange_start(
            [g_mlp["wout"].reshape(N_DEV, MIX // N_DEV, D), g_mlp["wup"], g_mlp["wdn"].reshape(N_DEV, DFF // N_DEV, D)],
            True, "g_rest_start_%d" % i)
        dx, dxb, g_mix = layer_bwd_mix(dx1, dy, lw, dict(prm, nrm=prm["nrm"] + sent_rest[i][4][0, 0]), saved[i], tt)
        grads[i] = {**g_mlp, **g_mix}
        if i > 0:
            sent_in[i] = exchange_start([scatter_w_in(g_mix["win"])], True, "g_in_start_%d" % i)
            token = sent_in[i][4][0, 0]

    def stack(k):
        return jnp.stack([g[k] for g in grads])

    small = _pack_small({"norm_mix_w": stack("nw1"), "ssd_conv_b": stack("cb"), "dt_bias": stack("dtb"),
                         "a_log": stack("alog"), "d_skip": stack("dsk"), "ssd_norm_w": stack("nrm"),
                         "norm_mlp_w": stack("nw2"), "final_norm_w": g_fw[0]})
    r_small, r_sc, r_cw = all_gather([small, stack("scw"), stack("cw")], "gather_small_grads")
    r_sc = lax.dynamic_slice_in_dim(r_sc, me * (D // N_DEV), D // N_DEV, axis=3)
    r_cw = lax.dynamic_slice_in_dim(r_cw, me * (XBC // N_DEV), XBC // N_DEV, axis=3)
    sent_in[0] = exchange_start([scatter_w_in(grads[0]["win"])], True, "g_in_start_0", after=r_small)

    after = sent_in[0][4]
    recv = [fill_own(*exchange_wait(sent_rest[i], after, True, "g_rest_wait_%d" % i), True) for i in range(DEPTH)]
    res = {}
    res["w_out"] = adamw_layers(w_out, [r[0] for r in recv], m_w_out, v_w_out, "adamw_w_out")
    res["w_up"] = adamw_layers(w_up, [r[1] for r in recv], m_w_up, v_w_up, "adamw_w_up")
    res["w_down"] = adamw_layers(w_down, [r[2] for r in recv], m_w_down, v_w_down, "adamw_w_down")
    after = res["w_down"][1]
    recv_in = [fill_own(*exchange_wait(sent_in[i], after, True, "g_in_wait_%d" % i), True)[0] for i in range(DEPTH)]
    res["w_in"] = adamw_layers(w_in, recv_in, m_w_in, v_w_in, "adamw_w_in")
    res["short_conv_w"] = _adamw_nd(short_conv_w, r_sc, m_short_conv_w, v_short_conv_w, "adamw_short_conv")
    res["ssd_conv_w"] = _adamw_nd(ssd_conv_w, r_cw, m_ssd_conv_w, v_ssd_conv_w, "adamw_ssd_conv")
    small_w = {"norm_mix_w": norm_mix_w, "ssd_conv_b": ssd_conv_b, "dt_bias": dt_bias, "a_log": a_log,
               "d_skip": d_skip, "ssd_norm_w": ssd_norm_w, "norm_mlp_w": norm_mlp_w, "final_norm_w": final_norm_w}
    small_m = {"norm_mix_w": m_norm_mix_w, "ssd_conv_b": m_ssd_conv_b, "dt_bias": m_dt_bias, "a_log": m_a_log,
               "d_skip": m_d_skip, "ssd_norm_w": m_ssd_norm_w, "norm_mlp_w": m_norm_mlp_w,
               "final_norm_w": m_final_norm_w}
    small_v = {"norm_mix_w": v_norm_mix_w, "ssd_conv_b": v_ssd_conv_b, "dt_bias": v_dt_bias, "a_log": v_a_log,
               "d_skip": v_d_skip, "ssd_norm_w": v_ssd_norm_w, "norm_mlp_w": v_norm_mlp_w,
               "final_norm_w": v_final_norm_w}
    shapes = {k: a.shape for k, a in small_w.items()}
    packed = adamw(_pack_small(small_w), r_small, _pack_small(small_m), _pack_small(small_v), "adamw_small")
    unpacked = [_unpack_small(p, shapes) for p in packed]
    for k in small_w:
        res[k] = [u[k] for u in unpacked]

    loss = lax.psum(loss_acc[0, 0], ("x", "y", "c"))
    order = ["norm_mix_w", "w_in", "short_conv_w", "ssd_conv_w", "ssd_conv_b", "dt_bias", "a_log", "d_skip",
             "ssd_norm_w", "w_out", "norm_mlp_w", "w_up", "w_down", "final_norm_w"]
    out = [loss, dx[None]]
    for part in range(4):
        out += [res[k][part] for k in order]
    return tuple(out)
```

```python
import functools

import numpy as np
import jax
import jax.numpy as jnp
from jax import lax
from jax.experimental import pallas as pl
from jax.experimental.pallas import tpu as pltpu

F32 = jnp.float32
BF16 = jnp.bfloat16
SDS = jax.ShapeDtypeStruct

N_DEV = 8
DEPTH = 4
D = 1024
NIN = 5648
NINP = 5760
DFF = 4096
MIX = 2048
NHEAD = 16
HDIM = 64
NSTATE = 128
CHUNK = 64
XBC = 1536
EPS = 1e-5
LANES = 128
NEG_BIG = -1e30

C_UB, C_UC, C_UH, C_Z, C_XS, C_BC, C_DT = 0, 1024, 2048, 3072, 4096, 5120, 5632
A_CV, A_YS, A_PX, A_PBC, AUX_W = 0, 1024, 2048, 3072, 3584

ADAM_LR = 0.001
ADAM_B1 = 0.9
ADAM_B2 = 0.999
ADAM_EPS = 1e-08
ADAM_WD = 0.01
ADAM_STEP = 10

VMEM_LIMIT = 56 * 1024 * 1024
MESH = pl.DeviceIdType.MESH


def _cparams(sem):
    return pltpu.CompilerParams(dimension_semantics=sem, vmem_limit_bytes=VMEM_LIMIT)


def _nt(a, b):
    return lax.dot_general(a, b, (((1,), (1,)), ((), ())), preferred_element_type=F32)


def _tn(a, b):
    return lax.dot_general(a, b, (((0,), (0,)), ((), ())), preferred_element_type=F32)


def _nn(a, b):
    return jnp.dot(a, b, preferred_element_type=F32)


def _sigmoid(v):
    return 0.5 * jnp.tanh(0.5 * v) + 0.5


def _split3(v):
    v1 = v.astype(BF16)
    r1 = v - v1.astype(F32)
    v2 = r1.astype(BF16)
    v3 = (r1 - v2.astype(F32)).astype(BF16)
    return v1, v2, v3


def _expand(v, eh):
    v1, v2, v3 = _split3(v)
    return _nn(v1, eh) + _nn(v2, eh) + _nn(v3, eh)


def _head_reduce(v, eht):
    v1 = v.astype(BF16)
    v2 = (v - v1.astype(F32)).astype(BF16)
    return _nn(v1, eht) + _nn(v2, eht)


def _head_matrices():
    eh = np.zeros((LANES, D), np.float32)
    for h in range(NHEAD):
        eh[h, h * HDIM:(h + 1) * HDIM] = 1.0
    return jnp.asarray(eh, BF16), jnp.asarray(eh.T.copy(), BF16)


def _resident(shape):
    return pl.BlockSpec(shape, lambda *_: (0,) * len(shape), pipeline_mode=pl.Buffered(1))


def _col_chunks(n, step):
    return [(c, min(c + step, n)) for c in range(0, n, step)]


def norm_matmul(x, nw, w, name):
    t_len = x.shape[0]
    blocked = w.ndim == 3
    n_len = w.shape[0] * w.shape[2] if blocked else w.shape[1]
    tm = min(512, t_len)
    chunks = _col_chunks(n_len, n_len // N_DEV if blocked else 1536)

    def body(x_ref, nw_ref, w_ref, o_ref, h_ref):
        xv = x_ref[...]
        r = lax.rsqrt(jnp.mean(xv * xv, axis=-1, keepdims=True) + EPS)
        hv = (xv * r * nw_ref[...]).astype(BF16)
        h_ref[...] = hv
        for j, (c0, c1) in enumerate(chunks):
            wj = w_ref[j] if blocked else w_ref[:, c0:c1]
            o_ref[:, c0:c1] = _nn(hv, wj).astype(o_ref.dtype)

    return pl.pallas_call(
        body, grid=(t_len // tm,),
        in_specs=[pl.BlockSpec((tm, D), lambda i: (i, 0)), _resident((1, D)), _resident(w.shape)],
        out_specs=[pl.BlockSpec((tm, n_len), lambda i: (i, 0)),
                   pl.BlockSpec((tm, D), lambda i: (i, 0))],
        out_shape=[SDS((t_len, n_len), BF16), SDS((t_len, D), BF16)],
        compiler_params=_cparams(("parallel",)), name=name)(x, nw, w)


def matmul_residual(a, w, res, relu2, name):
    t_len, k_len = a.shape
    tm = min(512, t_len)

    def body(a_ref, w_ref, res_ref, o_ref):
        av = a_ref[...]
        if relu2:
            af = jnp.maximum(av.astype(F32), 0.0)
            av = (af * af).astype(BF16)
        o_ref[...] = res_ref[...] + _nn(av, w_ref[...])

    return pl.pallas_call(
        body, grid=(t_len // tm,),
        in_specs=[pl.BlockSpec((tm, k_len), lambda i: (i, 0)),
                  _resident((k_len, D)),
                  pl.BlockSpec((tm, D), lambda i: (i, 0))],
        out_specs=pl.BlockSpec((tm, D), lambda i: (i, 0)),
        out_shape=SDS((t_len, D), F32),
        compiler_params=_cparams(("parallel",)), name=name)(a, w, res)


def matmul_nt_act(dy, w, u, name):
    t_len = dy.shape[0]
    n_len = w.shape[0]
    tm = min(512, t_len)
    chunks = _col_chunks(n_len, 1024)

    def body(dy_ref, w_ref, *rest):
        if u is None:
            (o_ref,) = rest
        else:
            u_ref, o_ref = rest
        dyv = dy_ref[...]
        for c0, c1 in chunks:
            p = _nt(dyv, w_ref[c0:c1, :])
            if u is not None:
                p = p * (2.0 * jnp.maximum(u_ref[:, c0:c1].astype(F32), 0.0))
            o_ref[:, c0:c1] = p.astype(o_ref.dtype)

    in_specs = [pl.BlockSpec((tm, D), lambda i: (i, 0)), _resident((n_len, D))]
    args = [dy, w]
    if u is not None:
        in_specs.append(pl.BlockSpec((tm, n_len), lambda i: (i, 0)))
        args.append(u)
    return pl.pallas_call(
        body, grid=(t_len // tm,),
        in_specs=in_specs,
        out_specs=pl.BlockSpec((tm, n_len), lambda i: (i, 0)),
        out_shape=SDS((t_len, n_len), BF16),
        compiler_params=_cparams(("parallel",)), name=name)(*args)


def matmul_tn(a, b, a_spec, b_spec, o_spec, o_shape, n_out, relu2, name, tt_max=2048):
    t_len = a.shape[0]
    tt = min(tt_max, t_len)
    nt = t_len // tt

    def body(a_ref, b_ref, o_ref, acc):
        t = pl.program_id(1)
        av = a_ref[...]
        if relu2:
            af = jnp.maximum(av.astype(F32), 0.0)
            av = (af * af).astype(BF16)
        p = _tn(av, b_ref[...])

        @pl.when(t == 0)
        def _():
            acc[...] = p

        @pl.when(t > 0)
        def _():
            acc[...] += p

        @pl.when(t == nt - 1)
        def _():
            if len(blk) == 3:
                for j in range(blk[0]):
                    o_ref[j] = acc[:, j * blk[2]:(j + 1) * blk[2]].astype(o_ref.dtype)
            else:
                o_ref[...] = acc[...].astype(o_ref.dtype)

    blk = tuple(o_spec.block_shape)
    acc_shape = (blk[1], blk[0] * blk[2]) if len(blk) == 3 else blk
    return pl.pallas_call(
        body, grid=(n_out, nt),
        in_specs=[a_spec(tt), b_spec(tt)],
        out_specs=o_spec, out_shape=o_shape,
        scratch_shapes=[pltpu.VMEM(acc_shape, F32)],
        compiler_params=_cparams(("parallel", "arbitrary")), name=name)(a, b)


def matmul_nt_norm_bwd(dy, w, x, nw, dres, name):
    t_len = x.shape[0]
    blocked = w.ndim == 3
    k_len = dy.shape[1]
    kb = k_len // N_DEV
    tm = min(512, t_len)

    def body(dy_ref, w_ref, x_ref, nw_ref, dres_ref, dx_ref, dxb_ref, dnw_ref):
        @pl.when(pl.program_id(0) == 0)
        def _():
            dnw_ref[...] = jnp.zeros_like(dnw_ref)

        if blocked:
            dh = _nt(dy_ref[:, 0:kb], w_ref[0])
            for j in range(1, N_DEV):
                dh = dh + _nt(dy_ref[:, j * kb:(j + 1) * kb], w_ref[j])
        else:
            dh = _nt(dy_ref[...], w_ref[...])
        xv = x_ref[...]
        r = lax.rsqrt(jnp.mean(xv * xv, axis=-1, keepdims=True) + EPS)
        xh = xv * r
        dnw_ref[0:1, :] += jnp.sum(dh * xh, axis=0, keepdims=True)
        g = dh * nw_ref[...]
        dx = dres_ref[...] + r * (g - xh * jnp.mean(g * xh, axis=-1, keepdims=True))
        dx_ref[...] = dx
        dxb_ref[...] = dx.astype(BF16)

    return pl.pallas_call(
        body, grid=(t_len // tm,),
        in_specs=[pl.BlockSpec((tm, k_len), lambda i: (i, 0)),
                  _resident(w.shape),
                  pl.BlockSpec((tm, D), lambda i: (i, 0)),
                  _resident((1, D)),
                  pl.BlockSpec((tm, D), lambda i: (i, 0))],
        out_specs=[pl.BlockSpec((tm, D), lambda i: (i, 0)),
                   pl.BlockSpec((tm, D), lambda i: (i, 0)),
                   pl.BlockSpec((8, D), lambda i: (0, 0))],
        out_shape=[SDS((t_len, D), F32), SDS((t_len, D), BF16), SDS((8, D), F32)],
        compiler_params=_cparams(("arbitrary",)), name=name)(dy, w, x, nw, dres)


def loss_head(x, fw, tgt):
    t_len = x.shape[0]
    tm = min(512, t_len)

    def body(x_ref, fw_ref, t_ref, loss_ref, dx_ref, dxb_ref, dfw_ref):
        @pl.when(pl.program_id(0) == 0)
        def _():
            loss_ref[...] = jnp.zeros_like(loss_ref)
            dfw_ref[...] = jnp.zeros_like(dfw_ref)
        xv = x_ref[...]
        r = lax.rsqrt(jnp.mean(xv * xv, axis=-1, keepdims=True) + EPS)
        xh = xv * r
        w = fw_ref[...]
        e = xh * w - t_ref[...]
        row = jnp.sum(e * e, axis=-1, keepdims=True) * (1.0 / D)
        loss_ref[...] += 0.5 * jnp.sum(row, axis=0, keepdims=True)
        dyf = e * (1.0 / D)
        dfw_ref[0:1, :] += jnp.sum(dyf * xh, axis=0, keepdims=True)
        g = dyf * w
        dx = r * (g - xh * jnp.mean(g * xh, axis=-1, keepdims=True))
        dx_ref[...] = dx
        dxb_ref[...] = dx.astype(BF16)

    return pl.pallas_call(
        body, grid=(t_len // tm,),
        in_specs=[pl.BlockSpec((tm, D), lambda i: (i, 0)),
                  pl.BlockSpec((1, D), lambda i: (0, 0)),
                  pl.BlockSpec((tm, D), lambda i: (i, 0))],
        out_specs=[pl.BlockSpec((8, LANES), lambda i: (0, 0)),
                   pl.BlockSpec((tm, D), lambda i: (i, 0)),
                   pl.BlockSpec((tm, D), lambda i: (i, 0)),
                   pl.BlockSpec((8, D), lambda i: (0, 0))],
        out_shape=[SDS((8, LANES), F32), SDS((t_len, D), F32), SDS((t_len, D), BF16), SDS((8, D), F32)],
        compiler_params=_cparams(("arbitrary",)), name="loss_head")(x, fw, tgt)


TAP_SHIFTS = (3, 2, 1)


def _shift_matrix(n, up):
    r = lax.broadcasted_iota(jnp.int32, (n, n), 0)
    c = lax.broadcasted_iota(jnp.int32, (n, n), 1)
    return jnp.concatenate([jnp.where(c == (r + j if up else r - j), 1.0, 0.0).astype(BF16) for j in TAP_SHIFTS],
                           axis=0)


def _shifts_dn(xb, halo, sm, n_shifts):
    n = xb.shape[0]
    first = len(TAP_SHIFTS) - n_shifts
    moved = _nn(sm[first * n:], xb)
    row = lax.broadcasted_iota(jnp.int32, halo.shape, 0)
    outs = []
    for k in range(n_shifts):
        j = TAP_SHIFTS[first + k]
        o = moved[k * n:(k + 1) * n]
        top = jnp.where(row < j, pltpu.roll(halo, j, 0), o[0:8])
        outs.append(jnp.concatenate([top, o[8:]], axis=0))
    return outs


def _shifts_up(xb, nxt, sm, n_shifts):
    n = xb.shape[0]
    first = len(TAP_SHIFTS) - n_shifts
    moved = _nn(sm[first * n:], xb)
    row = lax.broadcasted_iota(jnp.int32, nxt.shape, 0)
    outs = []
    for k in range(n_shifts):
        j = TAP_SHIFTS[first + k]
        o = moved[k * n:(k + 1) * n]
        bot = jnp.where(row >= 8 - j, pltpu.roll(nxt, 8 - j, 0), o[n - 8:n])
        outs.append(jnp.concatenate([o[:n - 8], bot], axis=0))
    return outs


def _conv_fwd(x, xb, halo, w_ref, kw, sm):
    shifted = _shifts_dn(xb, halo, sm, kw - 1)
    acc = w_ref[kw - 1:kw, :] * x
    for k in range(kw - 1):
        acc = acc + w_ref[k:k + 1, :] * shifted[k]
    return acc


def _chunk_cumsum(a, pos):
    for sh in (1, 2, 4, 8, 16, 32):
        a = a + jnp.where(pos >= sh, pltpu.roll(a, sh, 0), 0.0)
    return a


def _chunk_rcumsum(a, pos):
    n = a.shape[0]
    for sh in (1, 2, 4, 8, 16, 32):
        a = a + jnp.where(pos < CHUNK - sh, pltpu.roll(a, n - sh, 0), 0.0)
    return a


def _softplus(v):
    return jnp.maximum(v, 0.0) + jnp.log(1.0 + jnp.exp(-jnp.abs(v)))


def _silu(v):
    return v * _sigmoid(v)


def _dsilu(v):
    s = _sigmoid(v)
    return s * (1.0 + v * (1.0 - s))


def _lane_masks(width=D):
    lane = lax.broadcasted_iota(jnp.int32, (CHUNK, width), 1) & (HDIM - 1)
    row = lax.broadcasted_iota(jnp.int32, (CHUNK, width), 0)
    return lane == row, lane <= row


def _rep_matrix():
    lane = lax.broadcasted_iota(jnp.int32, (CHUNK, 512), 1) & (HDIM - 1)
    row = lax.broadcasted_iota(jnp.int32, (CHUNK, 512), 0)
    return jnp.where(lane == row, 1.0, 0.0).astype(BF16)


def _blockdiag(xp):
    lane = lax.broadcasted_iota(jnp.int32, xp.shape, 1)
    zero = jnp.zeros_like(xp)
    return jnp.concatenate([jnp.where(lane < HDIM, xp, zero), jnp.where(lane >= HDIM, xp, zero)], axis=0)


def _mixer_views(tt):
    r8 = tt // 8

    def main(width, col):
        return pl.BlockSpec((tt, width), lambda i, c=col // width: (i, c))

    def halo(width, col):
        return pl.BlockSpec((8, width), lambda i, c=col // width: (jnp.maximum(i * r8 - 1, 0), c))

    return main, halo


def mixer_fwd(proj, prm, tt):
    t_len = proj.shape[0]
    nblk = t_len // tt
    nc = tt // CHUNK
    main, halo = _mixer_views(tt)

    def body(ub_ref, uc_ref, uh_ref, z_ref, xr_ref, bcr_ref, dtr_ref, uch_ref, uhh_ref, xrh_ref, bcrh_ref,
             scw_ref, cwx_ref, cwbc_ref, cbx_ref, cbbc_ref, dtb_ref, alog_ref, dsk_ref, nrm_ref, eh_ref,
             y_ref, st_ref, aux_ref, hs, xs_s, bc_s, dtx_s, cumx_s, yssd_s):
        i = pl.program_id(0)
        first = i == 0

        @pl.when(first)
        def _():
            hs[...] = jnp.zeros_like(hs)

        keep = jnp.where(first, 0.0, 1.0)
        sm = _shift_matrix(tt, False)
        v = uc_ref[...].astype(F32) * uh_ref[...].astype(F32)
        vh = uch_ref[...].astype(F32) * uhh_ref[...].astype(F32) * keep
        cv = _conv_fwd(v, v.astype(BF16), vh, scw_ref, 3, sm)
        aux_ref[:, A_CV:A_CV + D] = cv.astype(BF16)
        y_ref[:, 0:D] = (ub_ref[...].astype(F32) * cv).astype(BF16)

        xrb = xr_ref[...]
        pre_x = _conv_fwd(xrb.astype(F32), xrb, xrh_ref[...].astype(F32) * keep, cwx_ref, 4, sm) + cbx_ref[...]
        aux_ref[:, A_PX:A_PX + D] = pre_x.astype(BF16)
        xs_s[...] = _silu(pre_x)
        bcrb = bcr_ref[...]
        pre_bc = _conv_fwd(bcrb.astype(F32), bcrb, bcrh_ref[...].astype(F32) * keep, cwbc_ref, 4, sm) + cbbc_ref[...]
        aux_ref[:, A_PBC:A_PBC + 512] = pre_bc.astype(BF16)
        bc_s[...] = _silu(pre_bc)
        dt = _softplus(dtr_ref[...].astype(F32) + dtb_ref[...])
        a_neg = -jnp.exp(alog_ref[...])
        pos = lax.broadcasted_iota(jnp.int32, (tt, LANES), 0) & (CHUNK - 1)
        cum = _chunk_cumsum(dt * a_neg, pos)
        eh = eh_ref[...]
        dtx_s[...] = _expand(dt, eh)
        cumx_s[...] = _expand(cum, eh)
        irep, causal = _lane_masks()
        rep = _rep_matrix()

        def chunk(c, carry):
            r0 = pl.multiple_of(c * CHUNK, CHUNK)
            rows = pl.ds(r0, CHUNK)
            cumx = cumx_s[rows, :]
            cum_l = cumx[CHUNK - 1:CHUNK, :]
            xd = xs_s[rows, :] * dtx_s[rows, :]
            xf = xd * jnp.exp(cum_l - cumx)
            ex = jnp.exp(cumx)
            e_l = jnp.exp(cum_l)
            rvec = jnp.sum(jnp.where(irep, cumx, 0.0), axis=0, keepdims=True)
            lam = jnp.exp(jnp.where(causal, cumx - rvec, NEG_BIG))
            bc = bc_s[rows, :]
            for g in range(2):
                gs = slice(g * 512, (g + 1) * 512)
                bg = bc[:, g * NSTATE:(g + 1) * NSTATE].astype(BF16)
                cg = bc[:, 256 + g * NSTATE:256 + (g + 1) * NSTATE].astype(BF16)
                s_rep = _nn(_nt(cg, bg).astype(BF16), rep)
                m_g = (s_rep * lam[:, gs]).astype(BF16)
                h_g = hs[:, gs]
                h_b = h_g.astype(BF16)
                st_ref[c, :, gs] = h_b
                yo = _nn(cg, h_b) * ex[:, gs]
                xd_b = xd[:, gs].astype(BF16)
                for hp in range(4):
                    ps = slice(hp * LANES, (hp + 1) * LANES)
                    yd = _nn(m_g[:, ps], _blockdiag(xd_b[:, ps]))
                    yssd_s[rows, g * 512 + hp * LANES:g * 512 + (hp + 1) * LANES] = yd + yo[:, ps]
                hs[:, gs] = h_g * e_l[:, gs] + _tn(bg, xf[:, gs].astype(BF16))
            return carry

        lax.fori_loop(0, nc, chunk, 0, unroll=True)

        ys = yssd_s[...] + dsk_ref[...] * xs_s[...]
        aux_ref[:, A_YS:A_YS + D] = ys.astype(BF16)
        gt = ys * _silu(z_ref[...].astype(F32))
        for g in range(2):
            gs = slice(g * 512, (g + 1) * 512)
            gg = gt[:, gs]
            rn = lax.rsqrt(jnp.mean(gg * gg, axis=-1, keepdims=True) + EPS)
            y_ref[:, D + g * 512:D + (g + 1) * 512] = (gg * rn * nrm_ref[:, gs]).astype(BF16)

    params = [prm[k] for k in ("scw", "cwx", "cwbc", "cbx", "cbbc", "dtb", "alog", "dskx", "nrm", "eh")]
    in_specs = [main(D, C_UB), main(D, C_UC), main(D, C_UH), main(D, C_Z), main(D, C_XS), main(512, C_BC),
                main(LANES, C_DT), halo(D, C_UC), halo(D, C_UH), halo(D, C_XS), halo(512, C_BC)]
    in_specs += [_param_spec(a, prm["layer"]) for a in params]
    return pl.pallas_call(
        body, grid=(nblk,),
        in_specs=in_specs,
        out_specs=[pl.BlockSpec((tt, MIX), lambda i: (i, 0)),
                   pl.BlockSpec((nc, NSTATE, D), lambda i: (i, 0, 0)),
                   pl.BlockSpec((tt, AUX_W), lambda i: (i, 0))],
        out_shape=[SDS((t_len, MIX), BF16), SDS((t_len // CHUNK, NSTATE, D), BF16), SDS((t_len, AUX_W), BF16)],
        scratch_shapes=[pltpu.VMEM((NSTATE, D), F32), pltpu.VMEM((tt, D), F32), pltpu.VMEM((tt, 512), F32),
                        pltpu.VMEM((tt, D), F32), pltpu.VMEM((tt, D), F32), pltpu.VMEM((tt, D), F32)],
        compiler_params=_cparams(("arbitrary",)), name="mixer_fwd")(*([proj] * 11), *params)


def mixer_bwd(proj, dy, states, aux, prm, tt):
    t_len = proj.shape[0]
    nblk = t_len // tt
    nc = tt // CHUNK

    def rev(i):
        return nblk - 1 - i

    def main(width, col):
        return pl.BlockSpec((tt, width), lambda i, c=col // width: (rev(i), c))

    def body(ub_ref, uc_ref, uh_ref, z_ref, xr_ref, bcr_ref, dtr_ref, dy_ref, st_ref, aux_ref,
             scw_ref, cwx_ref, cwbc_ref, cbx_ref, cbbc_ref, dtb_ref, alog_ref, dsk_ref, nrm_ref, eh_ref, eht_ref,
             dp_ref, gscw_ref, gcwx_ref, gcwbc_ref, gvec_ref, gdt_ref,
             dhs, xs_s, bc_s, dtx_s, cumx_s, dys_s, dxs_s, dbc_s, red_s, ddtx_s, nx_cv, nx_px, nx_pbc, sgx_s, sgbc_s):
        i = pl.program_id(0)

        @pl.when(i == 0)
        def _():
            dhs[...] = jnp.zeros_like(dhs)
            nx_cv[...] = jnp.zeros_like(nx_cv)
            nx_px[...] = jnp.zeros_like(nx_px)
            nx_pbc[...] = jnp.zeros_like(nx_pbc)
            gscw_ref[...] = jnp.zeros_like(gscw_ref)
            gcwx_ref[...] = jnp.zeros_like(gcwx_ref)
            gcwbc_ref[...] = jnp.zeros_like(gcwbc_ref)
            gvec_ref[...] = jnp.zeros_like(gvec_ref)
            gdt_ref[...] = jnp.zeros_like(gdt_ref)

        uc = uc_ref[...].astype(F32)
        uh = uh_ref[...].astype(F32)
        v = uc * uh
        dya = dy_ref[:, 0:D].astype(F32)
        dp_ref[:, C_UB:C_UB + D] = (dya * aux_ref[:, A_CV:A_CV + D].astype(F32)).astype(BF16)
        dcv = dya * ub_ref[...].astype(F32)
        sm = _shift_matrix(tt, True)
        ups = _shifts_up(dcv.astype(BF16), nx_cv[...], sm, 2) + [dcv]
        dv = None
        for k in range(3):
            gscw_ref[k:k + 1, :] += jnp.sum(v * ups[k], axis=0, keepdims=True)
            term = scw_ref[k:k + 1, :] * ups[k]
            dv = term if dv is None else dv + term
        nx_cv[...] = dcv[0:8]
        dp_ref[:, C_UC:C_UC + D] = (dv * uh).astype(BF16)
        dp_ref[:, C_UH:C_UH + D] = (dv * uc).astype(BF16)

        pre_x = aux_ref[:, A_PX:A_PX + D].astype(F32)
        pre_bc = aux_ref[:, A_PBC:A_PBC + 512].astype(F32)
        sg_x = _sigmoid(pre_x)
        sg_bc = _sigmoid(pre_bc)
        sgx_s[...] = sg_x
        sgbc_s[...] = sg_bc
        xs = pre_x * sg_x
        xs_s[...] = xs
        bc_s[...] = pre_bc * sg_bc
        dt_pre = dtr_ref[...].astype(F32) + dtb_ref[...]
        dt = _softplus(dt_pre)
        a_neg = -jnp.exp(alog_ref[...])
        pos = lax.broadcasted_iota(jnp.int32, (tt, LANES), 0) & (CHUNK - 1)
        cum = _chunk_cumsum(dt * a_neg, pos)
        eh = eh_ref[...]
        eht = eht_ref[...]
        dtx_s[...] = _expand(dt, eh)
        cumx_s[...] = _expand(cum, eh)

        irep, causal = _lane_masks()
        irep_g, _ = _lane_masks(512)
        rep = _rep_matrix()
        row8 = lax.broadcasted_iota(jnp.int32, (8, 512), 0)
        lane128 = lax.broadcasted_iota(jnp.int32, (CHUNK, LANES), 1)

        z = z_ref[...].astype(F32)
        sg_z = _sigmoid(z)
        sz = z * sg_z
        dsz = sg_z * (1.0 + z * (1.0 - sg_z))
        ys = aux_ref[:, A_YS:A_YS + D].astype(F32)
        gt = ys * sz
        dyb = dy_ref[:, D:MIX].astype(F32)
        for g in range(2):
            gs = slice(g * 512, (g + 1) * 512)
            gg = gt[:, gs]
            rn = lax.rsqrt(jnp.mean(gg * gg, axis=-1, keepdims=True) + EPS)
            gvec_ref[0:1, gs] += jnp.sum(dyb[:, gs] * gg * rn, axis=0, keepdims=True)
            dgn = dyb[:, gs] * nrm_ref[:, gs]
            dgt = rn * (dgn - gg * (rn * rn) * jnp.mean(dgn * gg, axis=-1, keepdims=True))
            dys = dgt * sz[:, gs]
            dys_s[:, gs] = dys
            dp_ref[:, C_Z + g * 512:C_Z + (g + 1) * 512] = (dgt * ys[:, gs] * dsz[:, gs]).astype(BF16)
        dys_all = dys_s[...]
        gvec_ref[1:2, :] += jnp.sum(dys_all * xs, axis=0, keepdims=True)

        def bwd_chunk(cc, carry):
            c = nc - 1 - cc
            r0 = pl.multiple_of(c * CHUNK, CHUNK)
            rows = pl.ds(r0, CHUNK)
            cumx = cumx_s[rows, :]
            cum_l = cumx[CHUNK - 1:CHUNK, :]
            xs_c = xs_s[rows, :]
            dtx = dtx_s[rows, :]
            xd = xs_c * dtx
            f = jnp.exp(cum_l - cumx)
            xf = xd * f
            ex = jnp.exp(cumx)
            e_l = jnp.exp(cum_l)
            rvec = jnp.sum(jnp.where(irep, cumx, 0.0), axis=0, keepdims=True)
            lam = jnp.exp(jnp.where(causal, cumx - rvec, NEG_BIG))
            bc = bc_s[rows, :]
            dyc = dys_s[rows, :]
            for g in range(2):
                gs = slice(g * 512, (g + 1) * 512)
                bg = bc[:, g * NSTATE:(g + 1) * NSTATE].astype(BF16)
                cg = bc[:, 256 + g * NSTATE:256 + (g + 1) * NSTATE].astype(BF16)
                h0 = st_ref[c, :, gs]
                dh = dhs[:, gs]
                dh_b = dh.astype(BF16)
                xf_g = xf[:, gs]
                dxf = _nn(bg, dh_b)
                db = _nt(xf_g.astype(BF16), dh_b)
                s_rep = _nn(_nt(cg, bg).astype(BF16), rep)
                lam_g = lam[:, gs]
                m_g = s_rep * lam_g
                m_b = m_g.astype(BF16)
                ex_g = ex[:, gs]
                dy_g = dyc[:, gs]
                yo = _nn(cg, h0) * ex_g
                dg_b = (dy_g * ex_g).astype(BF16)
                dc = _nt(dg_b, h0)
                el_g = e_l[:, gs]
                dee = jnp.sum(dh * h0.astype(F32), axis=0, keepdims=True) * el_g
                dhs[:, gs] = dh * el_g + _tn(cg, dg_b)
                xd_b = xd[:, gs].astype(BF16)
                dy_b = dy_g.astype(BF16)
                dm_parts, dxd_parts = [], []
                for hp in range(4):
                    ps = slice(hp * LANES, (hp + 1) * LANES)
                    bd = _blockdiag(xd_b[:, ps])
                    dm_parts.append(_nt(dy_b[:, ps], bd))
                    t2 = _tn(m_b[:, ps], dy_b[:, ps])
                    dxd_parts.append(jnp.where(lane128 < HDIM, t2[0:CHUNK], t2[CHUNK:2 * CHUNK]))
                dm = jnp.concatenate(dm_parts, axis=1)
                dxd = jnp.concatenate(dxd_parts, axis=1) + dxf * f[:, gs]
                dseg = dm * m_g
                ds_b = _nt((dm * lam_g).astype(BF16), rep).astype(BF16)
                dc = dc + _nn(ds_b, bg)
                db = db + _tn(ds_b, cg)
                colsum = jnp.sum(dseg, axis=0, keepdims=True)
                dxfxf = dxf * xf_g
                red = dseg - jnp.where(irep_g, colsum, 0.0) + dy_g * yo - dxfxf
                last = jnp.sum(dxfxf, axis=0, keepdims=True) + dee
                red_s[rows, gs] = red
                tail = pl.ds(pl.multiple_of(r0 + CHUNK - 8, 8), 8)
                red_s[tail, gs] += jnp.where(row8 == 7, last, 0.0)
                ddtx_s[rows, gs] = dxd * xs_c[:, gs]
                dxs_s[rows, gs] = dxd * dtx[:, gs] + dsk_ref[:, gs] * dy_g
                dbc_s[rows, g * NSTATE:(g + 1) * NSTATE] = db
                dbc_s[rows, 256 + g * NSTATE:256 + (g + 1) * NSTATE] = dc
            return carry

        lax.fori_loop(0, nc, bwd_chunk, 0, unroll=True)

        dcum = _head_reduce(red_s[...], eht)
        da = _chunk_rcumsum(dcum, pos)
        ddt = _head_reduce(ddtx_s[...], eht) + da * a_neg
        gdt_ref[1:2, :] += jnp.sum(da * dt, axis=0, keepdims=True) * a_neg
        ddt_raw = ddt * _sigmoid(dt_pre)
        lane_t = lax.broadcasted_iota(jnp.int32, (tt, LANES), 1)
        ddt_raw = jnp.where(lane_t < NHEAD, ddt_raw, 0.0)
        gdt_ref[0:1, :] += jnp.sum(ddt_raw, axis=0, keepdims=True)
        dp_ref[:, C_DT:C_DT + LANES] = ddt_raw.astype(BF16)

        sg_x = sgx_s[...]
        sg_bc = sgbc_s[...]
        pre_x = aux_ref[:, A_PX:A_PX + D].astype(F32)
        pre_bc = aux_ref[:, A_PBC:A_PBC + 512].astype(F32)
        dpx = dxs_s[...] * (sg_x * (1.0 + pre_x * (1.0 - sg_x)))
        dpbc = dbc_s[...] * (sg_bc * (1.0 + pre_bc * (1.0 - sg_bc)))
        gvec_ref[2:3, :] += jnp.sum(dpx, axis=0, keepdims=True)
        gcwbc_ref[4:5, :] += jnp.sum(dpbc, axis=0, keepdims=True)
        xraw = xr_ref[...].astype(F32)
        bcraw = bcr_ref[...].astype(F32)
        ups_x = _shifts_up(dpx.astype(BF16), nx_px[...], sm, 3) + [dpx]
        ups_bc = _shifts_up(dpbc.astype(BF16), nx_pbc[...], sm, 3) + [dpbc]
        dxr, dbcr = None, None
        for k in range(4):
            up_x = ups_x[k]
            up_bc = ups_bc[k]
            gcwx_ref[k:k + 1, :] += jnp.sum(xraw * up_x, axis=0, keepdims=True)
            gcwbc_ref[k:k + 1, :] += jnp.sum(bcraw * up_bc, axis=0, keepdims=True)
            tx = cwx_ref[k:k + 1, :] * up_x
            tb = cwbc_ref[k:k + 1, :] * up_bc
            dxr = tx if dxr is None else dxr + tx
            dbcr = tb if dbcr is None else dbcr + tb
        nx_px[...] = dpx[0:8]
        nx_pbc[...] = dpbc[0:8]
        dp_ref[:, C_XS:C_XS + D] = dxr.astype(BF16)
        dp_ref[:, C_BC:C_BC + 512] = dbcr.astype(BF16)

        @pl.when(i == nblk - 1)
        def _():
            gdt_ref[2:3, :] = _head_reduce(gvec_ref[1:2, :] * jnp.ones((8, 1), F32), eht)[0:1, :]

    def const(shape):
        return pl.BlockSpec(shape, lambda i: (0, 0))

    params = [prm[k] for k in ("scw", "cwx", "cwbc", "cbx", "cbbc", "dtb", "alog", "dskx", "nrm", "eh", "eht")]
    in_specs = [main(D, C_UB), main(D, C_UC), main(D, C_UH), main(D, C_Z), main(D, C_XS), main(512, C_BC),
                main(LANES, C_DT),
                pl.BlockSpec((tt, MIX), lambda i: (rev(i), 0)),
                pl.BlockSpec((nc, NSTATE, D), lambda i: (rev(i), 0, 0)),
                pl.BlockSpec((tt, AUX_W), lambda i: (rev(i), 0))]
    in_specs += [_param_spec(a, prm["layer"]) for a in params]
    return pl.pallas_call(
        body, grid=(nblk,),
        in_specs=in_specs,
        out_specs=[pl.BlockSpec((tt, NINP), lambda i: (rev(i), 0)),
                   const((8, D)), const((8, D)), const((8, 512)), const((8, D)), const((8, LANES))],
        out_shape=[SDS((t_len, NINP), BF16), SDS((8, D), F32), SDS((8, D), F32), SDS((8, 512), F32),
                   SDS((8, D), F32), SDS((8, LANES), F32)],
        scratch_shapes=[pltpu.VMEM((NSTATE, D), F32),
                        pltpu.VMEM((tt, D), F32), pltpu.VMEM((tt, 512), F32),
                        pltpu.VMEM((tt, D), F32), pltpu.VMEM((tt, D), F32),
                        pltpu.VMEM((tt, D), F32), pltpu.VMEM((tt, D), F32),
                        pltpu.VMEM((tt, 512), F32),
                        pltpu.VMEM((tt, D), F32), pltpu.VMEM((tt, D), F32),
                        pltpu.VMEM((8, D), F32), pltpu.VMEM((8, D), F32), pltpu.VMEM((8, 512), F32),
                        pltpu.VMEM((tt, D), F32), pltpu.VMEM((tt, 512), F32)],
        compiler_params=_cparams(("arbitrary",)), name="mixer_bwd")(
            *([proj] * 7), dy, states, aux, *params)


TN_IN = 1152
DW_TOKENS = 4096


def layer_fwd_mix(x, lw, prm, tt):
    proj, h1 = norm_matmul(x, lw["nw1"], lw["win"], "in_proj")
    y, st, aux = mixer_fwd(proj, prm, tt)
    return h1, proj, (st, aux), y


def layer_fwd_mlp(x, mixed, lw):
    h1, proj, st, y = mixed
    x1 = matmul_residual(y, lw["wout"], x, False, "out_proj")
    u, h2 = norm_matmul(x1, lw["nw2"], lw["wup"], "up_proj")
    x2 = matmul_residual(u, lw["wdn"], x1, True, "down_proj")
    return x2, (x, h1, proj, st, y, x1, h2, u)


def layer_fwd(x, lw, prm, tt):
    return layer_fwd_mlp(x, layer_fwd_mix(x, lw, prm, tt), lw)


def _dw(a, b, a_cols, b_cols, relu2, name, tt_max=2048):
    m_len, n_len = a.shape[1], b.shape[1]
    n_a, n_b = m_len // a_cols, n_len // b_cols
    assert n_a == 1 or n_b == 1
    if n_b == 1:
        return matmul_tn(
            a, b,
            lambda t_: pl.BlockSpec((t_, a_cols), lambda n, t: (t, n)),
            lambda t_: pl.BlockSpec((t_, n_len), lambda n, t: (t, 0)),
            pl.BlockSpec((a_cols, n_len), lambda n, t: (n, 0)), SDS((m_len, n_len), BF16), n_a, relu2, name, tt_max)
    return matmul_tn(
        a, b,
        lambda t_: pl.BlockSpec((t_, m_len), lambda n, t: (t, 0)),
        lambda t_: pl.BlockSpec((t_, b_cols), lambda n, t: (t, n)),
        pl.BlockSpec((m_len, b_cols), lambda n, t: (0, n)), SDS((m_len, n_len), BF16), n_b, relu2, name, tt_max)


def layer_bwd_mlp(dx2, dx2b, lw, saved):
    _, _, _, _, y, x1, h2, u = saved
    du = matmul_nt_act(dx2b, lw["wdn"], u, "mlp_bwd_du")
    g_wdn = _dw(u, dx2b, 1024, D, True, "dw_down")
    dx1, dx1b, g_nw2 = matmul_nt_norm_bwd(du, lw["wup"], x1, lw["nw2"], dx2, "mlp_bwd_dx")
    cb = DFF // N_DEV
    g_wup = matmul_tn(
        h2, du,
        lambda t_: pl.BlockSpec((t_, D), lambda n, t: (t, 0)),
        lambda t_: pl.BlockSpec((t_, 2 * cb), lambda n, t: (t, n)),
        pl.BlockSpec((2, D, cb), lambda n, t: (n, 0, 0)), SDS((N_DEV, D, cb), BF16), N_DEV // 2, False, "dw_up",
        DW_TOKENS)
    dy = matmul_nt_act(dx1b, lw["wout"], None, "out_bwd_dy")
    g_wout = _dw(y, dx1b, 1024, D, False, "dw_out", DW_TOKENS)
    return dx1, dx1b, dy, {"wout": g_wout, "wup": g_wup, "wdn": g_wdn, "nw2": g_nw2[0]}


def layer_bwd_mix(dx1, dy, lw, prm, saved, tt):
    x, h1, proj, st = saved[:4]
    dproj, gscw, gcwx, gcwbc, gvec, gdt = mixer_bwd(proj, dy, st[0], st[1], prm, tt)
    dx0, dx0b, g_nw1 = matmul_nt_norm_bwd(dproj, lw["win"], x, lw["nw1"], dx1, "in_bwd_dx")
    g_win = _dw(h1, dproj, D, TN_IN, False, "dw_in", DW_TOKENS)
    grads = {
        "win": g_win, "scw": gscw[0:3], "cw": jnp.concatenate([gcwx[0:4], gcwbc[0:4]], axis=1),
        "cb": jnp.concatenate([gvec[2], gcwbc[4]], axis=0),
        "dtb": gdt[0, :NHEAD], "alog": gdt[1, :NHEAD], "dsk": gdt[2, :NHEAD],
        "nrm": gvec[0], "nw1": g_nw1[0],
    }
    return dx0, dx0b, grads


def layer_bwd(dx2, dx2b, lw, prm, saved, tt):
    dx1, dx1b, dy, g_mlp = layer_bwd_mlp(dx2, dx2b, lw, saved)
    dx0, dx0b, g_mix = layer_bwd_mix(dx1, dy, lw, prm, saved, tt)
    return dx0, dx0b, {**g_mlp, **g_mix}


def stacked_params(conv_b, dt_bias, a_log, d_skip, ssd_norm_w):
    def lanes128(a):
        return jnp.pad(a, ((0, 0), (0, LANES - a.shape[1])))[:, None, :]

    return {"cbx": conv_b[:, None, :D], "cbbc": conv_b[:, None, D:], "dtb": lanes128(dt_bias),
            "alog": lanes128(a_log), "dskx": jnp.repeat(d_skip, HDIM, axis=1)[:, None, :],
            "nrm": ssd_norm_w[:, None, :]}


def layer_params(layer, win, scw, cw, nw1, nw2, stacked, eh, eht):
    def rows8(a):
        return jnp.pad(a, ((0, 8 - a.shape[0]), (0, 0)))

    lw = {"win": win, "nw1": nw1[None, :], "nw2": nw2[None, :]}
    prm = dict(stacked, layer=layer, scw=rows8(scw), cwx=rows8(cw[:, :D]), cwbc=rows8(cw[:, D:]), eh=eh, eht=eht)
    return lw, prm


def _param_spec(arr, layer):
    if arr.ndim == 3:
        return pl.BlockSpec((None,) + arr.shape[1:], lambda i: (layer, 0, 0))
    return pl.BlockSpec(arr.shape, lambda i: (0, 0))


def _flip(v, bit):
    return 1 - v if bit else v


def all_gather(arrs, name):
    n = len(arrs)

    def body(*refs):
        ins, outs = refs[:n], refs[n:2 * n]
        send_sems, recv_sems, local_sems = refs[2 * n:]
        x, y, c = lax.axis_index("x"), lax.axis_index("y"), lax.axis_index("c")
        sibling = (x, y, 1 - c)
        chips = [(1 - x, y), (x, 1 - y), (1 - x, 1 - y)]

        def idx(px, py, pc):
            return 4 * px + 2 * py + pc

        def copy(a, k, block, to, src=None):
            dst = outs[a].at[idx(*block)]
            return pltpu.make_async_remote_copy(
                src_ref=dst if src is None else src, dst_ref=dst,
                send_sem=send_sems.at[a, k], recv_sem=recv_sems.at[a, k], device_id=to, device_id_type=MESH)

        me = (x, y, c)
        mine = [pltpu.make_async_copy(ins[a], outs[a].at[idx(*me)], local_sems.at[a]) for a in range(n)]
        for cp in mine:
            cp.start()
        first = []
        for a in range(n):
            first.append(copy(a, 0, me, sibling, src=ins[a]))
            first += [copy(a, 1 + j, me, (*chip, c), src=ins[a]) for j, chip in enumerate(chips)]
        for cp in first:
            cp.start()
        passed = []
        for j, chip in enumerate(chips):
            for a in range(n):
                copy(a, 1 + j, (*chip, c), me).wait_recv()
                cp = copy(a, 4 + j, (*chip, c), sibling)
                cp.start()
                passed.append(cp)
        for a in range(n):
            copy(a, 0, sibling, me).wait_recv()
            for j, chip in enumerate(chips):
                copy(a, 4 + j, (*chip, 1 - c), me).wait_recv()
        for cp in first + passed:
            cp.wait_send()
        for cp in mine:
            cp.wait()

    any_spec = pl.BlockSpec(memory_space=pl.ANY)
    return pl.pallas_call(
        body, in_specs=[any_spec] * n, out_specs=[any_spec] * n,
        out_shape=[SDS((N_DEV,) + a.shape, a.dtype) for a in arrs],
        scratch_shapes=[pltpu.SemaphoreType.DMA((n, 7)), pltpu.SemaphoreType.DMA((n, 7)),
                        pltpu.SemaphoreType.DMA((n,))],
        name=name)(*arrs)


HBM_SPEC = pl.BlockSpec(memory_space=pltpu.HBM)
SEM_SPEC = pl.BlockSpec(memory_space=pltpu.SEMAPHORE)
SIDE_EFFECT = pltpu.SideEffectType.DATAFLOW_SIDE_EFFECTING
N_PEER = N_DEV - 1


def _peer(mask):
    x, y, c = lax.axis_index("x"), lax.axis_index("y"), lax.axis_index("c")
    return _flip(x, mask & 4), _flip(y, mask & 2), _flip(c, mask & 1)


ALL_PEERS = tuple(range(1, N_DEV))
SIBLING_AND_CHIPS = (1, 2, 4, 6)


def exchange_start(srcs, per_peer, name, after=None, masks=ALL_PEERS):
    n = len(srcs)
    npeer = len(masks)
    lands = [SDS((N_DEV,) + (a.shape[1:] if per_peer else a.shape), a.dtype) for a in srcs]
    n_in = 2 * n + (after is not None)

    def body(*refs):
        src_refs, land_refs = refs[:n], refs[n:2 * n]
        send_sems, recv_sems = refs[n_in], refs[n_in + 1]
        token = refs[-1]
        x, y, c = lax.axis_index("x"), lax.axis_index("y"), lax.axis_index("c")
        me = 4 * x + 2 * y + c
        for a in range(n):
            for k, mask in enumerate(masks):
                px, py, pc = _peer(mask)
                part = src_refs[a].at[4 * px + 2 * py + pc] if per_peer else src_refs[a]
                pltpu.make_async_remote_copy(
                    src_ref=part, dst_ref=land_refs[a].at[me], send_sem=send_sems.at[a * npeer + k],
                    recv_sem=recv_sems.at[a * npeer + k], device_id=(px, py, pc), device_id_type=MESH).start()
        token[...] = jnp.zeros_like(token)

    out = pl.pallas_call(
        body, name=name,
        out_shape=(pltpu.SemaphoreType.DMA((n * npeer,)), pltpu.SemaphoreType.DMA((n * npeer,)),
                   *[pltpu.HBM(a.shape, a.dtype) for a in srcs], *[pltpu.HBM(l.shape, l.dtype) for l in lands],
                   SDS((8, LANES), F32)),
        in_specs=(HBM_SPEC,) * (2 * n) + ((pl.BlockSpec(memory_space=pl.ANY),) if after is not None else ()),
        out_specs=(SEM_SPEC, SEM_SPEC) + (HBM_SPEC,) * (2 * n) + (pl.BlockSpec(memory_space=pltpu.VMEM),),
        input_output_aliases={k: 2 + k for k in range(2 * n)},
        compiler_params=pltpu.CompilerParams(has_side_effects=SIDE_EFFECT),
    )(*[pltpu.with_memory_space_constraint(a, pltpu.HBM) for a in srcs],
      *[pltpu.with_memory_space_constraint(lax.empty(l.shape, l.dtype), pltpu.HBM) for l in lands],
      *([after] if after is not None else []))
    return out[0], out[1], list(out[2:2 + n]), list(out[2 + n:2 + 2 * n]), out[-1]


def exchange_wait(started, after, per_peer, name, masks=ALL_PEERS):
    send_sems, recv_sems, srcs, lands, _ = started
    n = len(srcs)
    npeer = len(masks)

    def body(*refs):
        src_refs, land_refs = refs[:n], refs[n:2 * n]
        send_sems, recv_sems = refs[2 * n], refs[2 * n + 1]
        for k, mask in enumerate(masks):
            for a in range(n):
                copy = pltpu.make_async_remote_copy(
                    src_ref=src_refs[a].at[0] if per_peer else src_refs[a], dst_ref=land_refs[a].at[0],
                    send_sem=send_sems.at[a * npeer + k], recv_sem=recv_sems.at[a * npeer + k],
                    device_id=_peer(mask), device_id_type=MESH)
                copy.wait_send()
                copy.wait_recv()

    out = pl.pallas_call(
        body, name=name,
        out_shape=tuple(pltpu.HBM(a.shape, a.dtype) for a in srcs + lands),
        in_specs=(HBM_SPEC,) * (2 * n) + (SEM_SPEC, SEM_SPEC, pl.BlockSpec(memory_space=pl.ANY)),
        out_specs=(HBM_SPEC,) * (2 * n), input_output_aliases={k: k for k in range(2 * n)},
        compiler_params=pltpu.CompilerParams(has_side_effects=SIDE_EFFECT),
    )(*srcs, *lands, send_sems, recv_sems, after)
    return list(out[:n]), list(out[n:])


def relay_to_sibling(lands, name):
    n = len(lands)
    chips = (2, 4, 6)

    def body(*refs):
        land_refs = refs[n:2 * n]
        send_sems, recv_sems = refs[2 * n], refs[2 * n + 1]
        x, y, c = lax.axis_index("x"), lax.axis_index("y"), lax.axis_index("c")
        copies = []
        for a in range(n):
            for k, mask in enumerate(chips):
                px, py, _ = _peer(mask)
                block = land_refs[a].at[4 * px + 2 * py + c]
                cp = pltpu.make_async_remote_copy(
                    src_ref=block, dst_ref=block, send_sem=send_sems.at[a * 3 + k], recv_sem=recv_sems.at[a * 3 + k],
                    device_id=(x, y, 1 - c), device_id_type=MESH)
                cp.start()
                copies.append((cp, a, k, land_refs[a].at[4 * px + 2 * py + 1 - c]))
        for cp, a, k, arriving in copies:
            cp.wait_send()
            pltpu.make_async_remote_copy(
                src_ref=arriving, dst_ref=arriving, send_sem=send_sems.at[a * 3 + k], recv_sem=recv_sems.at[a * 3 + k],
                device_id=(x, y, 1 - c), device_id_type=MESH).wait_recv()

    any_spec = pl.BlockSpec(memory_space=pl.ANY)
    return list(pl.pallas_call(
        body, in_specs=[any_spec] * n, out_specs=[any_spec] * n,
        out_shape=[SDS(a.shape, a.dtype) for a in lands],
        input_output_aliases={k: k for k in range(n)},
        scratch_shapes=[pltpu.SemaphoreType.DMA((n * 3,)), pltpu.SemaphoreType.DMA((n * 3,))],
        name=name)(*lands))


def relay_start(lands, name):
    n = len(lands)

    def body(*refs):
        land_refs = refs[:n]
        send_sems, recv_sems = refs[n], refs[n + 1]
        token = refs[-1]
        x, y, c = lax.axis_index("x"), lax.axis_index("y"), lax.axis_index("c")
        for a in range(n):
            for k, mask in enumerate((2, 4, 6)):
                px, py, _ = _peer(mask)
                block = land_refs[a].at[4 * px + 2 * py + c]
                pltpu.make_async_remote_copy(
                    src_ref=block, dst_ref=block, send_sem=send_sems.at[a * 3 + k], recv_sem=recv_sems.at[a * 3 + k],
                    device_id=(x, y, 1 - c), device_id_type=MESH).start()
        token[...] = jnp.zeros_like(token)

    out = pl.pallas_call(
        body, name=name,
        out_shape=(pltpu.SemaphoreType.DMA((n * 3,)), pltpu.SemaphoreType.DMA((n * 3,)),
                   *[pltpu.HBM(a.shape, a.dtype) for a in lands], SDS((8, LANES), F32)),
        in_specs=(HBM_SPEC,) * n,
        out_specs=(SEM_SPEC, SEM_SPEC) + (HBM_SPEC,) * n + (pl.BlockSpec(memory_space=pltpu.VMEM),),
        input_output_aliases={k: 2 + k for k in range(n)},
        compiler_params=pltpu.CompilerParams(has_side_effects=SIDE_EFFECT),
    )(*lands)
    return out[0], out[1], list(out[2:2 + n]), out[-1]


def relay_wait(started, after, name):
    send_sems, recv_sems, lands, _ = started
    n = len(lands)

    def body(*refs):
        land_refs = refs[:n]
        send_sems, recv_sems = refs[n], refs[n + 1]
        x, y, c = lax.axis_index("x"), lax.axis_index("y"), lax.axis_index("c")
        for a in range(n):
            for k in range(3):
                copy = pltpu.make_async_remote_copy(
                    src_ref=land_refs[a].at[0], dst_ref=land_refs[a].at[0], send_sem=send_sems.at[a * 3 + k],
                    recv_sem=recv_sems.at[a * 3 + k], device_id=(x, y, 1 - c), device_id_type=MESH)
                copy.wait_send()
                copy.wait_recv()

    return list(pl.pallas_call(
        body, name=name,
        out_shape=tuple(pltpu.HBM(a.shape, a.dtype) for a in lands),
        in_specs=(HBM_SPEC,) * n + (SEM_SPEC, SEM_SPEC, pl.BlockSpec(memory_space=pl.ANY)),
        out_specs=(HBM_SPEC,) * n, input_output_aliases={k: k for k in range(n)},
        compiler_params=pltpu.CompilerParams(has_side_effects=SIDE_EFFECT),
    )(*lands, send_sems, recv_sems, after))


IN_SHARD = NIN // N_DEV
SLOT_W = 768


def _slot_window(j):
    return (IN_SHARD * j // LANES) * LANES, -(-(IN_SHARD * (j + 1)) // LANES) * LANES


def _placement(j):
    a, b = _slot_window(j)
    r = lax.broadcasted_iota(jnp.int32, (SLOT_W, b - a), 0)
    c = lax.broadcasted_iota(jnp.int32, (SLOT_W, b - a), 1)
    return jnp.where(jnp.logical_and(c == r + (IN_SHARD * j - a), r < IN_SHARD), 1.0, 0.0).astype(BF16)


def assemble_w_in(land):
    tm = 256

    def body(l_ref, o_ref, acc):
        acc[...] = jnp.zeros_like(acc)
        for j in range(N_DEV):
            a, b = _slot_window(j)
            acc[:, a:b] += _nn(l_ref[j], _placement(j))
        o_ref[...] = acc[...].astype(BF16)

    return pl.pallas_call(
        body, grid=(D // tm,),
        in_specs=[pl.BlockSpec((N_DEV, tm, SLOT_W), lambda i: (0, i, 0))],
        out_specs=pl.BlockSpec((tm, NINP), lambda i: (i, 0)),
        out_shape=SDS((D, NINP), BF16),
        scratch_shapes=[pltpu.VMEM((tm, NINP), F32)],
        compiler_params=_cparams(("parallel",)), name="assemble_w_in")(land)


def scatter_w_in(dw):
    tm = 256

    def body(d_ref, o_ref):
        for j in range(N_DEV):
            a, b = _slot_window(j)
            o_ref[j] = _nt(d_ref[:, a:b], _placement(j)).astype(BF16)

    return pl.pallas_call(
        body, grid=(D // tm,),
        in_specs=[pl.BlockSpec((tm, NINP), lambda i: (i, 0))],
        out_specs=pl.BlockSpec((N_DEV, tm, SLOT_W), lambda i: (0, i, 0)),
        out_shape=SDS((N_DEV, D, SLOT_W), BF16),
        compiler_params=_cparams(("parallel",)), name="scatter_w_in")(dw)


def _adamw_math(g, w_ref, m_ref, v_ref, g_ref, d_ref, nm_ref, nv_ref):
    mn = ADAM_B1 * m_ref[...] + (1.0 - ADAM_B1) * g
    vn = ADAM_B2 * v_ref[...] + (1.0 - ADAM_B2) * jnp.square(g)
    m_hat = mn / (1.0 - ADAM_B1 ** ADAM_STEP)
    v_hat = vn / (1.0 - ADAM_B2 ** ADAM_STEP)
    g_ref[...] = g
    d_ref[...] = -ADAM_LR * (m_hat / (jnp.sqrt(v_hat) + ADAM_EPS) + ADAM_WD * w_ref[...])
    nm_ref[...] = mn
    nv_ref[...] = vn


def adamw_layers(w, slots, m, v, name):
    depth, r_len, c_len = w.shape
    cs = slots[0].shape[2]
    br = min(128, r_len)
    assert r_len % br == 0

    def body(w_ref, *rest):
        s_refs, (m_ref, v_ref, g_ref, d_ref, nm_ref, nv_ref) = rest[:depth], rest[depth:]
        layer = pl.program_id(0)
        for k in range(depth):
            @pl.when(layer == k)
            def _(k=k):
                g = s_refs[k][0, :, 0:c_len].astype(F32)
                for j in range(1, N_DEV):
                    g = g + s_refs[k][j, :, 0:c_len].astype(F32)
                _adamw_math(g, w_ref, m_ref, v_ref, g_ref, d_ref, nm_ref, nv_ref)

    spec = pl.BlockSpec((None, br, c_len), lambda l, i: (l, i, 0))
    s_specs = [pl.BlockSpec((N_DEV, br, cs), lambda l, i, k=k: (0, jnp.where(l == k, i, 0), 0))
               for k in range(depth)]
    return pl.pallas_call(
        body, grid=(depth, r_len // br),
        in_specs=[spec] + s_specs + [spec, spec],
        out_specs=[spec] * 4, out_shape=[SDS(w.shape, F32)] * 4,
        compiler_params=_cparams(("arbitrary", "arbitrary")), name=name)(w, *slots, m, v)


def adamw(w, slots, m, v, name):
    r_len, c_len = w.shape
    br = r_len if r_len <= 512 else 512
    assert r_len % br == 0

    def body(w_ref, s_ref, m_ref, v_ref, g_ref, d_ref, nm_ref, nv_ref):
        g = s_ref[0].astype(F32)
        for k in range(1, N_DEV):
            g = g + s_ref[k].astype(F32)
        _adamw_math(g, w_ref, m_ref, v_ref, g_ref, d_ref, nm_ref, nv_ref)

    spec = pl.BlockSpec((br, c_len), lambda i: (i, 0))
    return pl.pallas_call(
        body, grid=(r_len // br,),
        in_specs=[spec, pl.BlockSpec((N_DEV, br, c_len), lambda i: (0, i, 0)), spec, spec],
        out_specs=[spec] * 4, out_shape=[SDS((r_len, c_len), F32)] * 4,
        compiler_params=_cparams(("parallel",)), name=name)(w, slots, m, v)


def _adamw_nd(w, slots, m, v, name):
    shp = w.shape
    r = int(np.prod(shp[:-1]))
    outs = adamw(w.reshape(r, shp[-1]), slots.reshape(N_DEV, r, shp[-1]), m.reshape(r, shp[-1]),
                 v.reshape(r, shp[-1]), name)
    return [o.reshape(shp) for o in outs]


SMALL = [("norm_mix_w", DEPTH * D), ("ssd_conv_b", DEPTH * XBC), ("dt_bias", DEPTH * NHEAD),
         ("a_log", DEPTH * NHEAD), ("d_skip", DEPTH * NHEAD), ("ssd_norm_w", DEPTH * D),
         ("norm_mlp_w", DEPTH * D), ("final_norm_w", D)]
SMALL_LEN = sum(s for _, s in SMALL)
SMALL_ROWS = -(-SMALL_LEN // LANES)


def _pack_small(parts):
    flat = jnp.concatenate([parts[k].reshape(-1) for k, _ in SMALL])
    return jnp.pad(flat, (0, SMALL_ROWS * LANES - SMALL_LEN)).reshape(SMALL_ROWS, LANES)


def _unpack_small(packed, shapes):
    flat = packed.reshape(-1)
    out, off = {}, 0
    for k, s in SMALL:
        out[k] = flat[off:off + s].reshape(shapes[k])
        off += s
    return out


def kernel(x, norm_mix_w, w_in, short_conv_w, ssd_conv_w, ssd_conv_b, dt_bias, a_log, d_skip, ssd_norm_w, w_out, norm_mlp_w, w_up, w_down, final_norm_w, loss_target, m_norm_mix_w, m_w_in, m_short_conv_w, m_ssd_conv_w, m_ssd_conv_b, m_dt_bias, m_a_log, m_d_skip, m_ssd_norm_w, m_w_out, m_norm_mlp_w, m_w_up, m_w_down, m_final_norm_w, v_norm_mix_w, v_w_in, v_short_conv_w, v_ssd_conv_w, v_ssd_conv_b, v_dt_bias, v_a_log, v_d_skip, v_ssd_norm_w, v_w_out, v_norm_mlp_w, v_w_up, v_w_down, v_final_norm_w):
    xs = x[0]
    t_len = xs.shape[0]
    tt = min(256, t_len)
    eh, eht = _head_matrices()
    stacked = stacked_params(ssd_conv_b, dt_bias, a_log, d_skip, ssd_norm_w)
    me = 4 * lax.axis_index("x") + 2 * lax.axis_index("y") + lax.axis_index("c")

    def start_weights(i, after):
        first = exchange_start(
            [jnp.pad(w_in[i].astype(BF16), ((0, 0), (0, SLOT_W - IN_SHARD))), short_conv_w[i], ssd_conv_w[i]],
            False, "w_in_start_%d" % i, after, SIBLING_AND_CHIPS)
        rest = exchange_start([w_out[i].astype(BF16), w_up[i].astype(BF16), w_down[i].astype(BF16)], False,
                              "w_rest_start_%d" % i, first[4] if after is None else after, SIBLING_AND_CHIPS)
        return first, rest

    def fill_own(srcs, lands, per_peer):
        own = [lax.dynamic_index_in_dim(s_, me, 0, keepdims=False) for s_ in srcs] if per_peer else srcs
        return [lax.dynamic_update_index_in_dim(l_, o_, me, 0) for l_, o_ in zip(lands, own)]

    def finish_weights(started, after, name):
        srcs, lands = exchange_wait(started, after, False, name + "_wait", SIBLING_AND_CHIPS)
        return fill_own(srcs, relay_to_sibling(lands, name + "_relay"), False)

    act = xs
    saved, layers = [], []
    first, rest = start_weights(0, None)
    token = first[4][0, 0] + rest[4][0, 0]
    for i in range(DEPTH):
        g_in, g_sc, g_cw = finish_weights(first, act, "w_in_%d" % i)
        lw, prm = layer_params(
            i, assemble_w_in(g_in), g_sc.transpose(1, 0, 2).reshape(3, D), g_cw.transpose(1, 0, 2).reshape(4, XBC),
            norm_mix_w[i], norm_mlp_w[i], stacked, eh, eht)
        lw["nw1"] = lw["nw1"] + token
        proj, h1 = norm_matmul(act, lw["nw1"], lw["win"], "in_proj")
        srcs_r, lands_r = exchange_wait(rest, proj, False, "w_rest_%d_wait" % i, SIBLING_AND_CHIPS)
        relay = relay_start(lands_r, "w_rest_%d_relay_start" % i)
        y, st, aux = mixer_fwd(proj, dict(prm, nrm=prm["nrm"] + relay[3][0, 0]), tt)
        mixed = (h1, proj, (st, aux), y)
        g_out, g_up, g_dn = fill_own(srcs_r, relay_wait(relay, y, "w_rest_%d_relay_wait" % i), False)
        lw.update(wout=g_out.reshape(MIX, D), wup=g_up, wdn=g_dn.reshape(DFF, D))
        if i + 1 < DEPTH:
            first, rest = start_weights(i + 1, g_dn)
            token = first[4][0, 0] + rest[4][0, 0]
            lw["nw2"] = lw["nw2"] + token
        layers.append((lw, prm))
        act, sv = layer_fwd_mlp(act, mixed, lw)
        saved.append(sv)
    loss_acc, dx, dxb, g_fw = loss_head(act, final_norm_w[None, :], loss_target[0])

    grads = [None] * DEPTH
    sent_rest, sent_in = [None] * DEPTH, [None] * DEPTH
    token = None
    for i in reversed(range(DEPTH)):
        lw, prm = layers[i]
        if token is not None:
            lw = dict(lw, nw2=lw["nw2"] + token)
        dx1, _, dy, g_mlp = layer_bwd_mlp(dx, dxb, lw, saved[i])
        sent_rest[i] = exchange_start(
            [g_mlp["wout"].reshape(N_DEV, MIX // N_DEV, D), g_mlp["wup"], g_mlp["wdn"].reshape(N_DEV, DFF // N_DEV, D)],
            True, "g_rest_start_%d" % i)
        dx, dxb, g_mix = layer_bwd_mix(dx1, dy, lw, dict(prm, nrm=prm["nrm"] + sent_rest[i][4][0, 0]), saved[i], tt)
        grads[i] = {**g_mlp, **g_mix}
        if i > 0:
            sent_in[i] = exchange_start([scatter_w_in(g_mix["win"])], True, "g_in_start_%d" % i)
            token = sent_in[i][4][0, 0]

    def stack(k):
        return jnp.stack([g[k] for g in grads])

    small = _pack_small({"norm_mix_w": stack("nw1"), "ssd_conv_b": stack("cb"), "dt_bias": stack("dtb"),
                         "a_log": stack("alog"), "d_skip": stack("dsk"), "ssd_norm_w": stack("nrm"),
                         "norm_mlp_w": stack("nw2"), "final_norm_w": g_fw[0]})
    r_small, r_sc, r_cw = all_gather([small, stack("scw"), stack("cw")], "gather_small_grads")
    r_sc = lax.dynamic_slice_in_dim(r_sc, me * (D // N_DEV), D // N_DEV, axis=3)
    r_cw = lax.dynamic_slice_in_dim(r_cw, me * (XBC // N_DEV), XBC // N_DEV, axis=3)
    sent_in[0] = exchange_start([scatter_w_in(grads[0]["win"])], True, "g_in_start_0", after=r_small)

    after = sent_in[0][4]
    recv = [fill_own(*exchange_wait(sent_rest[i], after, True, "g_rest_wait_%d" % i), True) for i in range(DEPTH)]
    res = {}
    res["w_out"] = adamw_layers(w_out, [r[0] for r in recv], m_w_out, v_w_out, "adamw_w_out")
    res["w_up"] = adamw_layers(w_up, [r[1] for r in recv], m_w_up, v_w_up, "adamw_w_up")
    res["w_down"] = adamw_layers(w_down, [r[2] for r in recv], m_w_down, v_w_down, "adamw_w_down")
    after = res["w_down"][1]
    recv_in = [fill_own(*exchange_wait(sent_in[i], after, True, "g_in_wait_%d" % i), True)[0] for i in range(DEPTH)]
    res["w_in"] = adamw_layers(w_in, recv_in, m_w_in, v_w_in, "adamw_w_in")
    res["short_conv_w"] = _adamw_nd(short_conv_w, r_sc, m_short_conv_w, v_short_conv_w, "adamw_short_conv")
    res["ssd_conv_w"] = _adamw_nd(ssd_conv_w, r_cw, m_ssd_conv_w, v_ssd_conv_w, "adamw_ssd_conv")
    small_w = {"norm_mix_w": norm_mix_w, "ssd_conv_b": ssd_conv_b, "dt_bias": dt_bias, "a_log": a_log,
               "d_skip": d_skip, "ssd_norm_w": ssd_norm_w, "norm_mlp_w": norm_mlp_w, "final_norm_w": final_norm_w}
    small_m = {"norm_mix_w": m_norm_mix_w, "ssd_conv_b": m_ssd_conv_b, "dt_bias": m_dt_bias, "a_log": m_a_log,
               "d_skip": m_d_skip, "ssd_norm_w": m_ssd_norm_w, "norm_mlp_w": m_norm_mlp_w,
               "final_norm_w": m_final_norm_w}
    small_v = {"norm_mix_w": v_norm_mix_w, "ssd_conv_b": v_ssd_conv_b, "dt_bias": v_dt_bias, "a_log": v_a_log,
               "d_skip": v_d_skip, "ssd_norm_w": v_ssd_norm_w, "norm_mlp_w": v_norm_mlp_w,
               "final_norm_w": v_final_norm_w}
    shapes = {k: a.shape for k, a in small_w.items()}
    packed = adamw(_pack_small(small_w), r_small, _pack_small(small_m), _pack_small(small_v), "adamw_small")
    unpacked = [_unpack_small(p, shapes) for p in packed]
    for k in small_w:
        res[k] = [u[k] for u in unpacked]

    loss = lax.psum(loss_acc[0, 0], ("x", "y", "c"))
    order = ["norm_mix_w", "w_in", "short_conv_w", "ssd_conv_w", "ssd_conv_b", "dt_bias", "a_log", "d_skip",
             "ssd_norm_w", "w_out", "norm_mlp_w", "w_up", "w_down", "final_norm_w"]
    out = [loss, dx[None]]
    for part in range(4):
        out += [res[k][part] for k in order]
    return tuple(out)
```

```python
import functools

import numpy as np
import jax
import jax.numpy as jnp
from jax import lax
from jax.experimental import pallas as pl
from jax.experimental.pallas import tpu as pltpu

F32 = jnp.float32
BF16 = jnp.bfloat16
SDS = jax.ShapeDtypeStruct

N_DEV = 8
DEPTH = 4
D = 1024
NIN = 5648
NINP = 5760
DFF = 4096
MIX = 2048
NHEAD = 16
HDIM = 64
NSTATE = 128
CHUNK = 64
XBC = 1536
EPS = 1e-5
LANES = 128
NEG_BIG = -1e30

C_UB, C_UC, C_UH, C_Z, C_XS, C_BC, C_DT = 0, 1024, 2048, 3072, 4096, 5120, 5632
A_CV, A_YS, A_PX, A_PBC, AUX_W = 0, 1024, 2048, 3072, 3584

ADAM_LR = 0.001
ADAM_B1 = 0.9
ADAM_B2 = 0.999
ADAM_EPS = 1e-08
ADAM_WD = 0.01
ADAM_STEP = 10

VMEM_LIMIT = 56 * 1024 * 1024
MESH = pl.DeviceIdType.MESH


def _cparams(sem):
    return pltpu.CompilerParams(dimension_semantics=sem, vmem_limit_bytes=VMEM_LIMIT)


def _nt(a, b):
    return lax.dot_general(a, b, (((1,), (1,)), ((), ())), preferred_element_type=F32)


def _tn(a, b):
    return lax.dot_general(a, b, (((0,), (0,)), ((), ())), preferred_element_type=F32)


def _nn(a, b):
    return jnp.dot(a, b, preferred_element_type=F32)


def _sigmoid(v):
    return 0.5 * jnp.tanh(0.5 * v) + 0.5


def _split3(v):
    v1 = v.astype(BF16)
    r1 = v - v1.astype(F32)
    v2 = r1.astype(BF16)
    v3 = (r1 - v2.astype(F32)).astype(BF16)
    return v1, v2, v3


def _expand(v, eh):
    v1, v2, v3 = _split3(v)
    return _nn(v1, eh) + _nn(v2, eh) + _nn(v3, eh)


def _head_reduce(v, eht):
    v1 = v.astype(BF16)
    v2 = (v - v1.astype(F32)).astype(BF16)
    return _nn(v1, eht) + _nn(v2, eht)


def _head_matrices():
    eh = np.zeros((LANES, D), np.float32)
    for h in range(NHEAD):
        eh[h, h * HDIM:(h + 1) * HDIM] = 1.0
    return jnp.asarray(eh, BF16), jnp.asarray(eh.T.copy(), BF16)


def _resident(shape):
    return pl.BlockSpec(shape, lambda *_: (0,) * len(shape), pipeline_mode=pl.Buffered(1))


def _col_chunks(n, step):
    return [(c, min(c + step, n)) for c in range(0, n, step)]


def norm_matmul(x, nw, w, name):
    t_len = x.shape[0]
    blocked = w.ndim == 3
    n_len = w.shape[0] * w.shape[2] if blocked else w.shape[1]
    tm = min(512, t_len)
    chunks = _col_chunks(n_len, n_len // N_DEV if blocked else 1536)

    def body(x_ref, nw_ref, w_ref, o_ref, h_ref):
        xv = x_ref[...]
        r = lax.rsqrt(jnp.mean(xv * xv, axis=-1, keepdims=True) + EPS)
        hv = (xv * r * nw_ref[...]).astype(BF16)
        h_ref[...] = hv
        for j, (c0, c1) in enumerate(chunks):
            wj = w_ref[j] if blocked else w_ref[:, c0:c1]
            o_ref[:, c0:c1] = _nn(hv, wj).astype(o_ref.dtype)

    return pl.pallas_call(
        body, grid=(t_len // tm,),
        in_specs=[pl.BlockSpec((tm, D), lambda i: (i, 0)), _resident((1, D)), _resident(w.shape)],
        out_specs=[pl.BlockSpec((tm, n_len), lambda i: (i, 0)),
                   pl.BlockSpec((tm, D), lambda i: (i, 0))],
        out_shape=[SDS((t_len, n_len), BF16), SDS((t_len, D), BF16)],
        compiler_params=_cparams(("parallel",)), name=name)(x, nw, w)


def matmul_residual(a, w, res, relu2, name, after=None):
    t_len, k_len = a.shape
    tm = min(512, t_len)

    def body(a_ref, w_ref, res_ref, *rest):
        o_ref = rest[-1]
        av = a_ref[...]
        if relu2:
            af = jnp.maximum(av.astype(F32), 0.0)
            av = (af * af).astype(BF16)
        o_ref[...] = res_ref[...] + _nn(av, w_ref[...])

    extra = [] if after is None else [after]
    return pl.pallas_call(
        body, grid=(t_len // tm,),
        in_specs=[pl.BlockSpec((tm, k_len), lambda i: (i, 0)),
                  _resident((k_len, D)),
                  pl.BlockSpec((tm, D), lambda i: (i, 0))] + [pl.BlockSpec(memory_space=pl.ANY)] * len(extra),
        out_specs=pl.BlockSpec((tm, D), lambda i: (i, 0)),
        out_shape=SDS((t_len, D), F32),
        compiler_params=_cparams(("parallel",)), name=name)(a, w, res, *extra)


def matmul_nt_act(dy, w, u, name):
    t_len = dy.shape[0]
    n_len = w.shape[0]
    tm = min(512, t_len)
    chunks = _col_chunks(n_len, 1024)

    def body(dy_ref, w_ref, *rest):
        if u is None:
            (o_ref,) = rest
        else:
            u_ref, o_ref = rest
        dyv = dy_ref[...]
        for c0, c1 in chunks:
            p = _nt(dyv, w_ref[c0:c1, :])
            if u is not None:
                p = p * (2.0 * jnp.maximum(u_ref[:, c0:c1].astype(F32), 0.0))
            o_ref[:, c0:c1] = p.astype(o_ref.dtype)

    in_specs = [pl.BlockSpec((tm, D), lambda i: (i, 0)), _resident((n_len, D))]
    args = [dy, w]
    if u is not None:
        in_specs.append(pl.BlockSpec((tm, n_len), lambda i: (i, 0)))
        args.append(u)
    return pl.pallas_call(
        body, grid=(t_len // tm,),
        in_specs=in_specs,
        out_specs=pl.BlockSpec((tm, n_len), lambda i: (i, 0)),
        out_shape=SDS((t_len, n_len), BF16),
        compiler_params=_cparams(("parallel",)), name=name)(*args)


def matmul_tn(a, b, a_spec, b_spec, o_spec, o_shape, n_out, relu2, name, tt_max=2048):
    t_len = a.shape[0]
    tt = min(tt_max, t_len)
    nt = t_len // tt

    def body(a_ref, b_ref, o_ref, acc):
        t = pl.program_id(1)
        av = a_ref[...]
        if relu2:
            af = jnp.maximum(av.astype(F32), 0.0)
            av = (af * af).astype(BF16)
        p = _tn(av, b_ref[...])

        @pl.when(t == 0)
        def _():
            acc[...] = p

        @pl.when(t > 0)
        def _():
            acc[...] += p

        @pl.when(t == nt - 1)
        def _():
            if len(blk) == 3:
                for j in range(blk[0]):
                    o_ref[j] = acc[:, j * blk[2]:(j + 1) * blk[2]].astype(o_ref.dtype)
            else:
                o_ref[...] = acc[...].astype(o_ref.dtype)

    blk = tuple(o_spec.block_shape)
    acc_shape = (blk[1], blk[0] * blk[2]) if len(blk) == 3 else blk
    return pl.pallas_call(
        body, grid=(n_out, nt),
        in_specs=[a_spec(tt), b_spec(tt)],
        out_specs=o_spec, out_shape=o_shape,
        scratch_shapes=[pltpu.VMEM(acc_shape, F32)],
        compiler_params=_cparams(("parallel", "arbitrary")), name=name)(a, b)


def matmul_nt_norm_bwd(dy, w, x, nw, dres, name):
    t_len = x.shape[0]
    blocked = w.ndim == 3
    k_len = dy.shape[1]
    kb = k_len // N_DEV
    tm = min(512, t_len)

    def body(dy_ref, w_ref, x_ref, nw_ref, dres_ref, dx_ref, dxb_ref, dnw_ref):
        @pl.when(pl.program_id(0) == 0)
        def _():
            dnw_ref[...] = jnp.zeros_like(dnw_ref)

        if blocked:
            dh = _nt(dy_ref[:, 0:kb], w_ref[0])
            for j in range(1, N_DEV):
                dh = dh + _nt(dy_ref[:, j * kb:(j + 1) * kb], w_ref[j])
        else:
            dh = _nt(dy_ref[...], w_ref[...])
        xv = x_ref[...]
        r = lax.rsqrt(jnp.mean(xv * xv, axis=-1, keepdims=True) + EPS)
        xh = xv * r
        dnw_ref[0:1, :] += jnp.sum(dh * xh, axis=0, keepdims=True)
        g = dh * nw_ref[...]
        dx = dres_ref[...] + r * (g - xh * jnp.mean(g * xh, axis=-1, keepdims=True))
        dx_ref[...] = dx
        dxb_ref[...] = dx.astype(BF16)

    return pl.pallas_call(
        body, grid=(t_len // tm,),
        in_specs=[pl.BlockSpec((tm, k_len), lambda i: (i, 0)),
                  _resident(w.shape),
                  pl.BlockSpec((tm, D), lambda i: (i, 0)),
                  _resident((1, D)),
                  pl.BlockSpec((tm, D), lambda i: (i, 0))],
        out_specs=[pl.BlockSpec((tm, D), lambda i: (i, 0)),
                   pl.BlockSpec((tm, D), lambda i: (i, 0)),
                   pl.BlockSpec((8, D), lambda i: (0, 0))],
        out_shape=[SDS((t_len, D), F32), SDS((t_len, D), BF16), SDS((8, D), F32)],
        compiler_params=_cparams(("arbitrary",)), name=name)(dy, w, x, nw, dres)


def loss_head(x, fw, tgt):
    t_len = x.shape[0]
    tm = min(512, t_len)

    def body(x_ref, fw_ref, t_ref, loss_ref, dx_ref, dxb_ref, dfw_ref):
        @pl.when(pl.program_id(0) == 0)
        def _():
            loss_ref[...] = jnp.zeros_like(loss_ref)
            dfw_ref[...] = jnp.zeros_like(dfw_ref)
        xv = x_ref[...]
        r = lax.rsqrt(jnp.mean(xv * xv, axis=-1, keepdims=True) + EPS)
        xh = xv * r
        w = fw_ref[...]
        e = xh * w - t_ref[...]
        row = jnp.sum(e * e, axis=-1, keepdims=True) * (1.0 / D)
        loss_ref[...] += 0.5 * jnp.sum(row, axis=0, keepdims=True)
        dyf = e * (1.0 / D)
        dfw_ref[0:1, :] += jnp.sum(dyf * xh, axis=0, keepdims=True)
        g = dyf * w
        dx = r * (g - xh * jnp.mean(g * xh, axis=-1, keepdims=True))
        dx_ref[...] = dx
        dxb_ref[...] = dx.astype(BF16)

    return pl.pallas_call(
        body, grid=(t_len // tm,),
        in_specs=[pl.BlockSpec((tm, D), lambda i: (i, 0)),
                  pl.BlockSpec((1, D), lambda i: (0, 0)),
                  pl.BlockSpec((tm, D), lambda i: (i, 0))],
        out_specs=[pl.BlockSpec((8, LANES), lambda i: (0, 0)),
                   pl.BlockSpec((tm, D), lambda i: (i, 0)),
                   pl.BlockSpec((tm, D), lambda i: (i, 0)),
                   pl.BlockSpec((8, D), lambda i: (0, 0))],
        out_shape=[SDS((8, LANES), F32), SDS((t_len, D), F32), SDS((t_len, D), BF16), SDS((8, D), F32)],
        compiler_params=_cparams(("arbitrary",)), name="loss_head")(x, fw, tgt)


TAP_SHIFTS = (3, 2, 1)


def _shift_matrix(n, up):
    r = lax.broadcasted_iota(jnp.int32, (n, n), 0)
    c = lax.broadcasted_iota(jnp.int32, (n, n), 1)
    return jnp.concatenate([jnp.where(c == (r + j if up else r - j), 1.0, 0.0).astype(BF16) for j in TAP_SHIFTS],
                           axis=0)


def _shifts_dn(xb, halo, sm, n_shifts):
    n = xb.shape[0]
    first = len(TAP_SHIFTS) - n_shifts
    moved = _nn(sm[first * n:], xb)
    row = lax.broadcasted_iota(jnp.int32, halo.shape, 0)
    outs = []
    for k in range(n_shifts):
        j = TAP_SHIFTS[first + k]
        o = moved[k * n:(k + 1) * n]
        top = jnp.where(row < j, pltpu.roll(halo, j, 0), o[0:8])
        outs.append(jnp.concatenate([top, o[8:]], axis=0))
    return outs


def _shifts_up(xb, nxt, sm, n_shifts):
    n = xb.shape[0]
    first = len(TAP_SHIFTS) - n_shifts
    moved = _nn(sm[first * n:], xb)
    row = lax.broadcasted_iota(jnp.int32, nxt.shape, 0)
    outs = []
    for k in range(n_shifts):
        j = TAP_SHIFTS[first + k]
        o = moved[k * n:(k + 1) * n]
        bot = jnp.where(row >= 8 - j, pltpu.roll(nxt, 8 - j, 0), o[n - 8:n])
        outs.append(jnp.concatenate([o[:n - 8], bot], axis=0))
    return outs


def _conv_fwd(x, xb, halo, w_ref, kw, sm):
    shifted = _shifts_dn(xb, halo, sm, kw - 1)
    acc = w_ref[kw - 1:kw, :] * x
    for k in range(kw - 1):
        acc = acc + w_ref[k:k + 1, :] * shifted[k]
    return acc


def _chunk_cumsum(a, pos):
    for sh in (1, 2, 4, 8, 16, 32):
        a = a + jnp.where(pos >= sh, pltpu.roll(a, sh, 0), 0.0)
    return a


def _chunk_rcumsum(a, pos):
    n = a.shape[0]
    for sh in (1, 2, 4, 8, 16, 32):
        a = a + jnp.where(pos < CHUNK - sh, pltpu.roll(a, n - sh, 0), 0.0)
    return a


def _softplus(v):
    return jnp.maximum(v, 0.0) + jnp.log(1.0 + jnp.exp(-jnp.abs(v)))


def _silu(v):
    return v * _sigmoid(v)


def _dsilu(v):
    s = _sigmoid(v)
    return s * (1.0 + v * (1.0 - s))


def _lane_masks(width=D):
    lane = lax.broadcasted_iota(jnp.int32, (CHUNK, width), 1) & (HDIM - 1)
    row = lax.broadcasted_iota(jnp.int32, (CHUNK, width), 0)
    return lane == row, lane <= row


def _rep_matrix():
    lane = lax.broadcasted_iota(jnp.int32, (CHUNK, 512), 1) & (HDIM - 1)
    row = lax.broadcasted_iota(jnp.int32, (CHUNK, 512), 0)
    return jnp.where(lane == row, 1.0, 0.0).astype(BF16)


def _blockdiag(xp):
    lane = lax.broadcasted_iota(jnp.int32, xp.shape, 1)
    zero = jnp.zeros_like(xp)
    return jnp.concatenate([jnp.where(lane < HDIM, xp, zero), jnp.where(lane >= HDIM, xp, zero)], axis=0)


def _mixer_views(tt):
    r8 = tt // 8

    def main(width, col):
        return pl.BlockSpec((tt, width), lambda i, c=col // width: (i, c))

    def halo(width, col):
        return pl.BlockSpec((8, width), lambda i, c=col // width: (jnp.maximum(i * r8 - 1, 0), c))

    return main, halo


def mixer_fwd(proj, prm, tt):
    t_len = proj.shape[0]
    nblk = t_len // tt
    nc = tt // CHUNK
    main, halo = _mixer_views(tt)

    def body(ub_ref, uc_ref, uh_ref, z_ref, xr_ref, bcr_ref, dtr_ref, uch_ref, uhh_ref, xrh_ref, bcrh_ref,
             scw_ref, cwx_ref, cwbc_ref, cbx_ref, cbbc_ref, dtb_ref, alog_ref, dsk_ref, nrm_ref, eh_ref,
             y_ref, st_ref, aux_ref, hs, xs_s, bc_s, dtx_s, cumx_s, yssd_s):
        i = pl.program_id(0)
        first = i == 0

        @pl.when(first)
        def _():
            hs[...] = jnp.zeros_like(hs)

        keep = jnp.where(first, 0.0, 1.0)
        sm = _shift_matrix(tt, False)
        v = uc_ref[...].astype(F32) * uh_ref[...].astype(F32)
        vh = uch_ref[...].astype(F32) * uhh_ref[...].astype(F32) * keep
        cv = _conv_fwd(v, v.astype(BF16), vh, scw_ref, 3, sm)
        aux_ref[:, A_CV:A_CV + D] = cv.astype(BF16)
        y_ref[:, 0:D] = (ub_ref[...].astype(F32) * cv).astype(BF16)

        xrb = xr_ref[...]
        pre_x = _conv_fwd(xrb.astype(F32), xrb, xrh_ref[...].astype(F32) * keep, cwx_ref, 4, sm) + cbx_ref[...]
        aux_ref[:, A_PX:A_PX + D] = pre_x.astype(BF16)
        xs_s[...] = _silu(pre_x)
        bcrb = bcr_ref[...]
        pre_bc = _conv_fwd(bcrb.astype(F32), bcrb, bcrh_ref[...].astype(F32) * keep, cwbc_ref, 4, sm) + cbbc_ref[...]
        aux_ref[:, A_PBC:A_PBC + 512] = pre_bc.astype(BF16)
        bc_s[...] = _silu(pre_bc)
        dt = _softplus(dtr_ref[...].astype(F32) + dtb_ref[...])
        a_neg = -jnp.exp(alog_ref[...])
        pos = lax.broadcasted_iota(jnp.int32, (tt, LANES), 0) & (CHUNK - 1)
        cum = _chunk_cumsum(dt * a_neg, pos)
        eh = eh_ref[...]
        dtx_s[...] = _expand(dt, eh)
        cumx_s[...] = _expand(cum, eh)
        irep, causal = _lane_masks()
        rep = _rep_matrix()

        def chunk(c, carry):
            r0 = pl.multiple_of(c * CHUNK, CHUNK)
            rows = pl.ds(r0, CHUNK)
            cumx = cumx_s[rows, :]
            cum_l = cumx[CHUNK - 1:CHUNK, :]
            xd = xs_s[rows, :] * dtx_s[rows, :]
            xf = xd * jnp.exp(cum_l - cumx)
            ex = jnp.exp(cumx)
            e_l = jnp.exp(cum_l)
            rvec = jnp.sum(jnp.where(irep, cumx, 0.0), axis=0, keepdims=True)
            lam = jnp.exp(jnp.where(causal, cumx - rvec, NEG_BIG))
            bc = bc_s[rows, :]
            for g in range(2):
                gs = slice(g * 512, (g + 1) * 512)
                bg = bc[:, g * NSTATE:(g + 1) * NSTATE].astype(BF16)
                cg = bc[:, 256 + g * NSTATE:256 + (g + 1) * NSTATE].astype(BF16)
                s_rep = _nn(_nt(cg, bg).astype(BF16), rep)
                m_g = (s_rep * lam[:, gs]).astype(BF16)
                h_g = hs[:, gs]
                h_b = h_g.astype(BF16)
                st_ref[c, :, gs] = h_b
                yo = _nn(cg, h_b) * ex[:, gs]
                xd_b = xd[:, gs].astype(BF16)
                for hp in range(4):
                    ps = slice(hp * LANES, (hp + 1) * LANES)
                    yd = _nn(m_g[:, ps], _blockdiag(xd_b[:, ps]))
                    yssd_s[rows, g * 512 + hp * LANES:g * 512 + (hp + 1) * LANES] = yd + yo[:, ps]
                hs[:, gs] = h_g * e_l[:, gs] + _tn(bg, xf[:, gs].astype(BF16))
            return carry

        lax.fori_loop(0, nc, chunk, 0, unroll=True)

        ys = yssd_s[...] + dsk_ref[...] * xs_s[...]
        aux_ref[:, A_YS:A_YS + D] = ys.astype(BF16)
        gt = ys * _silu(z_ref[...].astype(F32))
        for g in range(2):
            gs = slice(g * 512, (g + 1) * 512)
            gg = gt[:, gs]
            rn = lax.rsqrt(jnp.mean(gg * gg, axis=-1, keepdims=True) + EPS)
            y_ref[:, D + g * 512:D + (g + 1) * 512] = (gg * rn * nrm_ref[:, gs]).astype(BF16)

    params = [prm[k] for k in ("scw", "cwx", "cwbc", "cbx", "cbbc", "dtb", "alog", "dskx", "nrm", "eh")]
    in_specs = [main(D, C_UB), main(D, C_UC), main(D, C_UH), main(D, C_Z), main(D, C_XS), main(512, C_BC),
                main(LANES, C_DT), halo(D, C_UC), halo(D, C_UH), halo(D, C_XS), halo(512, C_BC)]
    in_specs += [_param_spec(a, prm["layer"]) for a in params]
    return pl.pallas_call(
        body, grid=(nblk,),
        in_specs=in_specs,
        out_specs=[pl.BlockSpec((tt, MIX), lambda i: (i, 0)),
                   pl.BlockSpec((nc, NSTATE, D), lambda i: (i, 0, 0)),
                   pl.BlockSpec((tt, AUX_W), lambda i: (i, 0))],
        out_shape=[SDS((t_len, MIX), BF16), SDS((t_len // CHUNK, NSTATE, D), BF16), SDS((t_len, AUX_W), BF16)],
        scratch_shapes=[pltpu.VMEM((NSTATE, D), F32), pltpu.VMEM((tt, D), F32), pltpu.VMEM((tt, 512), F32),
                        pltpu.VMEM((tt, D), F32), pltpu.VMEM((tt, D), F32), pltpu.VMEM((tt, D), F32)],
        compiler_params=_cparams(("arbitrary",)), name="mixer_fwd")(*([proj] * 11), *params)


def mixer_bwd(proj, dy, states, aux, prm, tt):
    t_len = proj.shape[0]
    nblk = t_len // tt
    nc = tt // CHUNK

    def rev(i):
        return nblk - 1 - i

    def main(width, col):
        return pl.BlockSpec((tt, width), lambda i, c=col // width: (rev(i), c))

    def body(ub_ref, uc_ref, uh_ref, z_ref, xr_ref, bcr_ref, dtr_ref, dy_ref, st_ref, aux_ref,
             scw_ref, cwx_ref, cwbc_ref, cbx_ref, cbbc_ref, dtb_ref, alog_ref, dsk_ref, nrm_ref, eh_ref, eht_ref,
             dp_ref, gscw_ref, gcwx_ref, gcwbc_ref, gvec_ref, gdt_ref,
             dhs, xs_s, bc_s, dtx_s, cumx_s, dys_s, dxs_s, dbc_s, red_s, ddtx_s, nx_cv, nx_px, nx_pbc, sgx_s, sgbc_s):
        i = pl.program_id(0)

        @pl.when(i == 0)
        def _():
            dhs[...] = jnp.zeros_like(dhs)
            nx_cv[...] = jnp.zeros_like(nx_cv)
            nx_px[...] = jnp.zeros_like(nx_px)
            nx_pbc[...] = jnp.zeros_like(nx_pbc)
            gscw_ref[...] = jnp.zeros_like(gscw_ref)
            gcwx_ref[...] = jnp.zeros_like(gcwx_ref)
            gcwbc_ref[...] = jnp.zeros_like(gcwbc_ref)
            gvec_ref[...] = jnp.zeros_like(gvec_ref)
            gdt_ref[...] = jnp.zeros_like(gdt_ref)

        uc = uc_ref[...].astype(F32)
        uh = uh_ref[...].astype(F32)
        v = uc * uh
        dya = dy_ref[:, 0:D].astype(F32)
        dp_ref[:, C_UB:C_UB + D] = (dya * aux_ref[:, A_CV:A_CV + D].astype(F32)).astype(BF16)
        dcv = dya * ub_ref[...].astype(F32)
        sm = _shift_matrix(tt, True)
        ups = _shifts_up(dcv.astype(BF16), nx_cv[...], sm, 2) + [dcv]
        dv = None
        for k in range(3):
            gscw_ref[k:k + 1, :] += jnp.sum(v * ups[k], axis=0, keepdims=True)
            term = scw_ref[k:k + 1, :] * ups[k]
            dv = term if dv is None else dv + term
        nx_cv[...] = dcv[0:8]
        dp_ref[:, C_UC:C_UC + D] = (dv * uh).astype(BF16)
        dp_ref[:, C_UH:C_UH + D] = (dv * uc).astype(BF16)

        pre_x = aux_ref[:, A_PX:A_PX + D].astype(F32)
        pre_bc = aux_ref[:, A_PBC:A_PBC + 512].astype(F32)
        sg_x = _sigmoid(pre_x)
        sg_bc = _sigmoid(pre_bc)
        sgx_s[...] = sg_x
        sgbc_s[...] = sg_bc
        xs = pre_x * sg_x
        xs_s[...] = xs
        bc_s[...] = pre_bc * sg_bc
        dt_pre = dtr_ref[...].astype(F32) + dtb_ref[...]
        dt = _softplus(dt_pre)
        a_neg = -jnp.exp(alog_ref[...])
        pos = lax.broadcasted_iota(jnp.int32, (tt, LANES), 0) & (CHUNK - 1)
        cum = _chunk_cumsum(dt * a_neg, pos)
        eh = eh_ref[...]
        eht = eht_ref[...]
        dtx_s[...] = _expand(dt, eh)
        cumx_s[...] = _expand(cum, eh)

        irep, causal = _lane_masks()
        irep_g, _ = _lane_masks(512)
        rep = _rep_matrix()
        row8 = lax.broadcasted_iota(jnp.int32, (8, 512), 0)
        lane128 = lax.broadcasted_iota(jnp.int32, (CHUNK, LANES), 1)

        z = z_ref[...].astype(F32)
        sg_z = _sigmoid(z)
        sz = z * sg_z
        dsz = sg_z * (1.0 + z * (1.0 - sg_z))
        ys = aux_ref[:, A_YS:A_YS + D].astype(F32)
        gt = ys * sz
        dyb = dy_ref[:, D:MIX].astype(F32)
        for g in range(2):
            gs = slice(g * 512, (g + 1) * 512)
            gg = gt[:, gs]
            rn = lax.rsqrt(jnp.mean(gg * gg, axis=-1, keepdims=True) + EPS)
            gvec_ref[0:1, gs] += jnp.sum(dyb[:, gs] * gg * rn, axis=0, keepdims=True)
            dgn = dyb[:, gs] * nrm_ref[:, gs]
            dgt = rn * (dgn - gg * (rn * rn) * jnp.mean(dgn * gg, axis=-1, keepdims=True))
            dys = dgt * sz[:, gs]
            dys_s[:, gs] = dys
            dp_ref[:, C_Z + g * 512:C_Z + (g + 1) * 512] = (dgt * ys[:, gs] * dsz[:, gs]).astype(BF16)
        dys_all = dys_s[...]
        gvec_ref[1:2, :] += jnp.sum(dys_all * xs, axis=0, keepdims=True)

        def bwd_chunk(cc, carry):
            c = nc - 1 - cc
            r0 = pl.multiple_of(c * CHUNK, CHUNK)
            rows = pl.ds(r0, CHUNK)
            cumx = cumx_s[rows, :]
            cum_l = cumx[CHUNK - 1:CHUNK, :]
            xs_c = xs_s[rows, :]
            dtx = dtx_s[rows, :]
            xd = xs_c * dtx
            f = jnp.exp(cum_l - cumx)
            xf = xd * f
            ex = jnp.exp(cumx)
            e_l = jnp.exp(cum_l)
            rvec = jnp.sum(jnp.where(irep, cumx, 0.0), axis=0, keepdims=True)
            lam = jnp.exp(jnp.where(causal, cumx - rvec, NEG_BIG))
            bc = bc_s[rows, :]
            dyc = dys_s[rows, :]
            for g in range(2):
                gs = slice(g * 512, (g + 1) * 512)
                bg = bc[:, g * NSTATE:(g + 1) * NSTATE].astype(BF16)
                cg = bc[:, 256 + g * NSTATE:256 + (g + 1) * NSTATE].astype(BF16)
                h0 = st_ref[c, :, gs]
                dh = dhs[:, gs]
                dh_b = dh.astype(BF16)
                xf_g = xf[:, gs]
                dxf = _nn(bg, dh_b)
                db = _nt(xf_g.astype(BF16), dh_b)
                s_rep = _nn(_nt(cg, bg).astype(BF16), rep)
                lam_g = lam[:, gs]
                m_g = s_rep * lam_g
                m_b = m_g.astype(BF16)
                ex_g = ex[:, gs]
                dy_g = dyc[:, gs]
                yo = _nn(cg, h0) * ex_g
                dg_b = (dy_g * ex_g).astype(BF16)
                dc = _nt(dg_b, h0)
                el_g = e_l[:, gs]
                dee = jnp.sum(dh * h0.astype(F32), axis=0, keepdims=True) * el_g
                dhs[:, gs] = dh * el_g + _tn(cg, dg_b)
                xd_b = xd[:, gs].astype(BF16)
                dy_b = dy_g.astype(BF16)
                dm_parts, dxd_parts = [], []
                for hp in range(4):
                    ps = slice(hp * LANES, (hp + 1) * LANES)
                    bd = _blockdiag(xd_b[:, ps])
                    dm_parts.append(_nt(dy_b[:, ps], bd))
                    t2 = _tn(m_b[:, ps], dy_b[:, ps])
                    dxd_parts.append(jnp.where(lane128 < HDIM, t2[0:CHUNK], t2[CHUNK:2 * CHUNK]))
                dm = jnp.concatenate(dm_parts, axis=1)
                dxd = jnp.concatenate(dxd_parts, axis=1) + dxf * f[:, gs]
                dseg = dm * m_g
                ds_b = _nt((dm * lam_g).astype(BF16), rep).astype(BF16)
                dc = dc + _nn(ds_b, bg)
                db = db + _tn(ds_b, cg)
                colsum = jnp.sum(dseg, axis=0, keepdims=True)
                dxfxf = dxf * xf_g
                red = dseg - jnp.where(irep_g, colsum, 0.0) + dy_g * yo - dxfxf
                last = jnp.sum(dxfxf, axis=0, keepdims=True) + dee
                red_s[rows, gs] = red
                tail = pl.ds(pl.multiple_of(r0 + CHUNK - 8, 8), 8)
                red_s[tail, gs] += jnp.where(row8 == 7, last, 0.0)
                ddtx_s[rows, gs] = dxd * xs_c[:, gs]
                dxs_s[rows, gs] = dxd * dtx[:, gs] + dsk_ref[:, gs] * dy_g
                dbc_s[rows, g * NSTATE:(g + 1) * NSTATE] = db
                dbc_s[rows, 256 + g * NSTATE:256 + (g + 1) * NSTATE] = dc
            return carry

        lax.fori_loop(0, nc, bwd_chunk, 0, unroll=True)

        dcum = _head_reduce(red_s[...], eht)
        da = _chunk_rcumsum(dcum, pos)
        ddt = _head_reduce(ddtx_s[...], eht) + da * a_neg
        gdt_ref[1:2, :] += jnp.sum(da * dt, axis=0, keepdims=True) * a_neg
        ddt_raw = ddt * _sigmoid(dt_pre)
        lane_t = lax.broadcasted_iota(jnp.int32, (tt, LANES), 1)
        ddt_raw = jnp.where(lane_t < NHEAD, ddt_raw, 0.0)
        gdt_ref[0:1, :] += jnp.sum(ddt_raw, axis=0, keepdims=True)
        dp_ref[:, C_DT:C_DT + LANES] = ddt_raw.astype(BF16)

        sg_x = sgx_s[...]
        sg_bc = sgbc_s[...]
        pre_x = aux_ref[:, A_PX:A_PX + D].astype(F32)
        pre_bc = aux_ref[:, A_PBC:A_PBC + 512].astype(F32)
        dpx = dxs_s[...] * (sg_x * (1.0 + pre_x * (1.0 - sg_x)))
        dpbc = dbc_s[...] * (sg_bc * (1.0 + pre_bc * (1.0 - sg_bc)))
        gvec_ref[2:3, :] += jnp.sum(dpx, axis=0, keepdims=True)
        gcwbc_ref[4:5, :] += jnp.sum(dpbc, axis=0, keepdims=True)
        xraw = xr_ref[...].astype(F32)
        bcraw = bcr_ref[...].astype(F32)
        ups_x = _shifts_up(dpx.astype(BF16), nx_px[...], sm, 3) + [dpx]
        ups_bc = _shifts_up(dpbc.astype(BF16), nx_pbc[...], sm, 3) + [dpbc]
        dxr, dbcr = None, None
        for k in range(4):
            up_x = ups_x[k]
            up_bc = ups_bc[k]
            gcwx_ref[k:k + 1, :] += jnp.sum(xraw * up_x, axis=0, keepdims=True)
            gcwbc_ref[k:k + 1, :] += jnp.sum(bcraw * up_bc, axis=0, keepdims=True)
            tx = cwx_ref[k:k + 1, :] * up_x
            tb = cwbc_ref[k:k + 1, :] * up_bc
            dxr = tx if dxr is None else dxr + tx
            dbcr = tb if dbcr is None else dbcr + tb
        nx_px[...] = dpx[0:8]
        nx_pbc[...] = dpbc[0:8]
        dp_ref[:, C_XS:C_XS + D] = dxr.astype(BF16)
        dp_ref[:, C_BC:C_BC + 512] = dbcr.astype(BF16)

        @pl.when(i == nblk - 1)
        def _():
            gdt_ref[2:3, :] = _head_reduce(gvec_ref[1:2, :] * jnp.ones((8, 1), F32), eht)[0:1, :]

    def const(shape):
        return pl.BlockSpec(shape, lambda i: (0, 0))

    params = [prm[k] for k in ("scw", "cwx", "cwbc", "cbx", "cbbc", "dtb", "alog", "dskx", "nrm", "eh", "eht")]
    in_specs = [main(D, C_UB), main(D, C_UC), main(D, C_UH), main(D, C_Z), main(D, C_XS), main(512, C_BC),
                main(LANES, C_DT),
                pl.BlockSpec((tt, MIX), lambda i: (rev(i), 0)),
                pl.BlockSpec((nc, NSTATE, D), lambda i: (rev(i), 0, 0)),
                pl.BlockSpec((tt, AUX_W), lambda i: (rev(i), 0))]
    in_specs += [_param_spec(a, prm["layer"]) for a in params]
    return pl.pallas_call(
        body, grid=(nblk,),
        in_specs=in_specs,
        out_specs=[pl.BlockSpec((tt, NINP), lambda i: (rev(i), 0)),
                   const((8, D)), const((8, D)), const((8, 512)), const((8, D)), const((8, LANES))],
        out_shape=[SDS((t_len, NINP), BF16), SDS((8, D), F32), SDS((8, D), F32), SDS((8, 512), F32),
                   SDS((8, D), F32), SDS((8, LANES), F32)],
        scratch_shapes=[pltpu.VMEM((NSTATE, D), F32),
                        pltpu.VMEM((tt, D), F32), pltpu.VMEM((tt, 512), F32),
                        pltpu.VMEM((tt, D), F32), pltpu.VMEM((tt, D), F32),
                        pltpu.VMEM((tt, D), F32), pltpu.VMEM((tt, D), F32),
                        pltpu.VMEM((tt, 512), F32),
                        pltpu.VMEM((tt, D), F32), pltpu.VMEM((tt, D), F32),
                        pltpu.VMEM((8, D), F32), pltpu.VMEM((8, D), F32), pltpu.VMEM((8, 512), F32),
                        pltpu.VMEM((tt, D), F32), pltpu.VMEM((tt, 512), F32)],
        compiler_params=_cparams(("arbitrary",)), name="mixer_bwd")(
            *([proj] * 7), dy, states, aux, *params)


TN_IN = 1152
DW_TOKENS = 4096


def layer_fwd_mix(x, lw, prm, tt):
    proj, h1 = norm_matmul(x, lw["nw1"], lw["win"], "in_proj")
    y, st, aux = mixer_fwd(proj, prm, tt)
    return h1, proj, (st, aux), y


def layer_fwd_mlp(x, mixed, lw, between=None):
    h1, proj, st, y = mixed
    x1 = matmul_residual(y, lw["wout"], x, False, "out_proj")
    u, h2 = norm_matmul(x1, lw["nw2"], lw["wup"], "up_proj")
    x2 = matmul_residual(u, lw["wdn"], x1, True, "down_proj", None if between is None else between(u))
    return x2, (x, h1, proj, st, y, x1, h2, u)


def layer_fwd(x, lw, prm, tt):
    return layer_fwd_mlp(x, layer_fwd_mix(x, lw, prm, tt), lw)


def _dw(a, b, a_cols, b_cols, relu2, name, tt_max=2048):
    m_len, n_len = a.shape[1], b.shape[1]
    n_a, n_b = m_len // a_cols, n_len // b_cols
    assert n_a == 1 or n_b == 1
    if n_b == 1:
        return matmul_tn(
            a, b,
            lambda t_: pl.BlockSpec((t_, a_cols), lambda n, t: (t, n)),
            lambda t_: pl.BlockSpec((t_, n_len), lambda n, t: (t, 0)),
            pl.BlockSpec((a_cols, n_len), lambda n, t: (n, 0)), SDS((m_len, n_len), BF16), n_a, relu2, name, tt_max)
    return matmul_tn(
        a, b,
        lambda t_: pl.BlockSpec((t_, m_len), lambda n, t: (t, 0)),
        lambda t_: pl.BlockSpec((t_, b_cols), lambda n, t: (t, n)),
        pl.BlockSpec((m_len, b_cols), lambda n, t: (0, n)), SDS((m_len, n_len), BF16), n_b, relu2, name, tt_max)


def layer_bwd_mlp(dx2, dx2b, lw, saved):
    _, _, _, _, y, x1, h2, u = saved
    du = matmul_nt_act(dx2b, lw["wdn"], u, "mlp_bwd_du")
    g_wdn = _dw(u, dx2b, 1024, D, True, "dw_down")
    dx1, dx1b, g_nw2 = matmul_nt_norm_bwd(du, lw["wup"], x1, lw["nw2"], dx2, "mlp_bwd_dx")
    cb = DFF // N_DEV
    g_wup = matmul_tn(
        h2, du,
        lambda t_: pl.BlockSpec((t_, D), lambda n, t: (t, 0)),
        lambda t_: pl.BlockSpec((t_, 2 * cb), lambda n, t: (t, n)),
        pl.BlockSpec((2, D, cb), lambda n, t: (n, 0, 0)), SDS((N_DEV, D, cb), BF16), N_DEV // 2, False, "dw_up",
        DW_TOKENS)
    dy = matmul_nt_act(dx1b, lw["wout"], None, "out_bwd_dy")
    g_wout = _dw(y, dx1b, 1024, D, False, "dw_out", DW_TOKENS)
    return dx1, dx1b, dy, {"wout": g_wout, "wup": g_wup, "wdn": g_wdn, "nw2": g_nw2[0]}


def layer_bwd_mix(dx1, dy, lw, prm, saved, tt):
    x, h1, proj, st = saved[:4]
    dproj, gscw, gcwx, gcwbc, gvec, gdt = mixer_bwd(proj, dy, st[0], st[1], prm, tt)
    dx0, dx0b, g_nw1 = matmul_nt_norm_bwd(dproj, lw["win"], x, lw["nw1"], dx1, "in_bwd_dx")
    g_win = _dw(h1, dproj, D, TN_IN, False, "dw_in", DW_TOKENS)
    grads = {
        "win": g_win, "scw": gscw[0:3], "cw": jnp.concatenate([gcwx[0:4], gcwbc[0:4]], axis=1),
        "cb": jnp.concatenate([gvec[2], gcwbc[4]], axis=0),
        "dtb": gdt[0, :NHEAD], "alog": gdt[1, :NHEAD], "dsk": gdt[2, :NHEAD],
        "nrm": gvec[0], "nw1": g_nw1[0],
    }
    return dx0, dx0b, grads


def layer_bwd(dx2, dx2b, lw, prm, saved, tt):
    dx1, dx1b, dy, g_mlp = layer_bwd_mlp(dx2, dx2b, lw, saved)
    dx0, dx0b, g_mix = layer_bwd_mix(dx1, dy, lw, prm, saved, tt)
    return dx0, dx0b, {**g_mlp, **g_mix}


def stacked_params(conv_b, dt_bias, a_log, d_skip, ssd_norm_w):
    def lanes128(a):
        return jnp.pad(a, ((0, 0), (0, LANES - a.shape[1])))[:, None, :]

    return {"cbx": conv_b[:, None, :D], "cbbc": conv_b[:, None, D:], "dtb": lanes128(dt_bias),
            "alog": lanes128(a_log), "dskx": jnp.repeat(d_skip, HDIM, axis=1)[:, None, :],
            "nrm": ssd_norm_w[:, None, :]}


def layer_params(layer, win, scw, cw, nw1, nw2, stacked, eh, eht):
    def rows8(a):
        return jnp.pad(a, ((0, 8 - a.shape[0]), (0, 0)))

    lw = {"win": win, "nw1": nw1[None, :], "nw2": nw2[None, :]}
    prm = dict(stacked, layer=layer, scw=rows8(scw), cwx=rows8(cw[:, :D]), cwbc=rows8(cw[:, D:]), eh=eh, eht=eht)
    return lw, prm


def _param_spec(arr, layer):
    if arr.ndim == 3:
        return pl.BlockSpec((None,) + arr.shape[1:], lambda i: (layer, 0, 0))
    return pl.BlockSpec(arr.shape, lambda i: (0, 0))


def _flip(v, bit):
    return 1 - v if bit else v


def all_gather(arrs, name):
    n = len(arrs)

    def body(*refs):
        ins, outs = refs[:n], refs[n:2 * n]
        send_sems, recv_sems, local_sems = refs[2 * n:]
        x, y, c = lax.axis_index("x"), lax.axis_index("y"), lax.axis_index("c")
        sibling = (x, y, 1 - c)
        chips = [(1 - x, y), (x, 1 - y), (1 - x, 1 - y)]

        def idx(px, py, pc):
            return 4 * px + 2 * py + pc

        def copy(a, k, block, to, src=None):
            dst = outs[a].at[idx(*block)]
            return pltpu.make_async_remote_copy(
                src_ref=dst if src is None else src, dst_ref=dst,
                send_sem=send_sems.at[a, k], recv_sem=recv_sems.at[a, k], device_id=to, device_id_type=MESH)

        me = (x, y, c)
        mine = [pltpu.make_async_copy(ins[a], outs[a].at[idx(*me)], local_sems.at[a]) for a in range(n)]
        for cp in mine:
            cp.start()
        first = []
        for a in range(n):
            first.append(copy(a, 0, me, sibling, src=ins[a]))
            first += [copy(a, 1 + j, me, (*chip, c), src=ins[a]) for j, chip in enumerate(chips)]
        for cp in first:
            cp.start()
        passed = []
        for j, chip in enumerate(chips):
            for a in range(n):
                copy(a, 1 + j, (*chip, c), me).wait_recv()
                cp = copy(a, 4 + j, (*chip, c), sibling)
                cp.start()
                passed.append(cp)
        for a in range(n):
            copy(a, 0, sibling, me).wait_recv()
            for j, chip in enumerate(chips):
                copy(a, 4 + j, (*chip, 1 - c), me).wait_recv()
        for cp in first + passed:
            cp.wait_send()
        for cp in mine:
            cp.wait()

    any_spec = pl.BlockSpec(memory_space=pl.ANY)
    return pl.pallas_call(
        body, in_specs=[any_spec] * n, out_specs=[any_spec] * n,
        out_shape=[SDS((N_DEV,) + a.shape, a.dtype) for a in arrs],
        scratch_shapes=[pltpu.SemaphoreType.DMA((n, 7)), pltpu.SemaphoreType.DMA((n, 7)),
                        pltpu.SemaphoreType.DMA((n,))],
        name=name)(*arrs)


HBM_SPEC = pl.BlockSpec(memory_space=pltpu.HBM)
SEM_SPEC = pl.BlockSpec(memory_space=pltpu.SEMAPHORE)
SIDE_EFFECT = pltpu.SideEffectType.DATAFLOW_SIDE_EFFECTING
N_PEER = N_DEV - 1


def _peer(mask):
    x, y, c = lax.axis_index("x"), lax.axis_index("y"), lax.axis_index("c")
    return _flip(x, mask & 4), _flip(y, mask & 2), _flip(c, mask & 1)


ALL_PEERS = tuple(range(1, N_DEV))
SIBLING_AND_CHIPS = (1, 2, 4, 6)


def exchange_start(srcs, per_peer, name, after=None, masks=ALL_PEERS):
    n = len(srcs)
    npeer = len(masks)
    lands = [SDS((N_DEV,) + (a.shape[1:] if per_peer else a.shape), a.dtype) for a in srcs]
    n_in = 2 * n + (after is not None)

    def body(*refs):
        src_refs, land_refs = refs[:n], refs[n:2 * n]
        send_sems, recv_sems = refs[n_in], refs[n_in + 1]
        token = refs[-1]
        x, y, c = lax.axis_index("x"), lax.axis_index("y"), lax.axis_index("c")
        me = 4 * x + 2 * y + c
        for a in range(n):
            for k, mask in enumerate(masks):
                px, py, pc = _peer(mask)
                part = src_refs[a].at[4 * px + 2 * py + pc] if per_peer else src_refs[a]
                pltpu.make_async_remote_copy(
                    src_ref=part, dst_ref=land_refs[a].at[me], send_sem=send_sems.at[a * npeer + k],
                    recv_sem=recv_sems.at[a * npeer + k], device_id=(px, py, pc), device_id_type=MESH).start()
        token[...] = jnp.zeros_like(token)

    out = pl.pallas_call(
        body, name=name,
        out_shape=(pltpu.SemaphoreType.DMA((n * npeer,)), pltpu.SemaphoreType.DMA((n * npeer,)),
                   *[pltpu.HBM(a.shape, a.dtype) for a in srcs], *[pltpu.HBM(l.shape, l.dtype) for l in lands],
                   SDS((8, LANES), F32)),
        in_specs=(HBM_SPEC,) * (2 * n) + ((pl.BlockSpec(memory_space=pl.ANY),) if after is not None else ()),
        out_specs=(SEM_SPEC, SEM_SPEC) + (HBM_SPEC,) * (2 * n) + (pl.BlockSpec(memory_space=pltpu.VMEM),),
        input_output_aliases={k: 2 + k for k in range(2 * n)},
        compiler_params=pltpu.CompilerParams(has_side_effects=SIDE_EFFECT),
    )(*[pltpu.with_memory_space_constraint(a, pltpu.HBM) for a in srcs],
      *[pltpu.with_memory_space_constraint(lax.empty(l.shape, l.dtype), pltpu.HBM) for l in lands],
      *([after] if after is not None else []))
    return out[0], out[1], list(out[2:2 + n]), list(out[2 + n:2 + 2 * n]), out[-1]


def exchange_wait(started, after, per_peer, name, masks=ALL_PEERS):
    send_sems, recv_sems, srcs, lands, _ = started
    n = len(srcs)
    npeer = len(masks)

    def body(*refs):
        src_refs, land_refs = refs[:n], refs[n:2 * n]
        send_sems, recv_sems = refs[2 * n], refs[2 * n + 1]
        for k, mask in enumerate(masks):
            for a in range(n):
                copy = pltpu.make_async_remote_copy(
                    src_ref=src_refs[a].at[0] if per_peer else src_refs[a], dst_ref=land_refs[a].at[0],
                    send_sem=send_sems.at[a * npeer + k], recv_sem=recv_sems.at[a * npeer + k],
                    device_id=_peer(mask), device_id_type=MESH)
                copy.wait_send()
                copy.wait_recv()

    out = pl.pallas_call(
        body, name=name,
        out_shape=tuple(pltpu.HBM(a.shape, a.dtype) for a in srcs + lands),
        in_specs=(HBM_SPEC,) * (2 * n) + (SEM_SPEC, SEM_SPEC, pl.BlockSpec(memory_space=pl.ANY)),
        out_specs=(HBM_SPEC,) * (2 * n), input_output_aliases={k: k for k in range(2 * n)},
        compiler_params=pltpu.CompilerParams(has_side_effects=SIDE_EFFECT),
    )(*srcs, *lands, send_sems, recv_sems, after)
    return list(out[:n]), list(out[n:])


def relay_to_sibling(lands, name):
    n = len(lands)
    chips = (2, 4, 6)

    def body(*refs):
        land_refs = refs[n:2 * n]
        send_sems, recv_sems = refs[2 * n], refs[2 * n + 1]
        x, y, c = lax.axis_index("x"), lax.axis_index("y"), lax.axis_index("c")
        copies = []
        for a in range(n):
            for k, mask in enumerate(chips):
                px, py, _ = _peer(mask)
                block = land_refs[a].at[4 * px + 2 * py + c]
                cp = pltpu.make_async_remote_copy(
                    src_ref=block, dst_ref=block, send_sem=send_sems.at[a * 3 + k], recv_sem=recv_sems.at[a * 3 + k],
                    device_id=(x, y, 1 - c), device_id_type=MESH)
                cp.start()
                copies.append((cp, a, k, land_refs[a].at[4 * px + 2 * py + 1 - c]))
        for cp, a, k, arriving in copies:
            cp.wait_send()
            pltpu.make_async_remote_copy(
                src_ref=arriving, dst_ref=arriving, send_sem=send_sems.at[a * 3 + k], recv_sem=recv_sems.at[a * 3 + k],
                device_id=(x, y, 1 - c), device_id_type=MESH).wait_recv()

    any_spec = pl.BlockSpec(memory_space=pl.ANY)
    return list(pl.pallas_call(
        body, in_specs=[any_spec] * n, out_specs=[any_spec] * n,
        out_shape=[SDS(a.shape, a.dtype) for a in lands],
        input_output_aliases={k: k for k in range(n)},
        scratch_shapes=[pltpu.SemaphoreType.DMA((n * 3,)), pltpu.SemaphoreType.DMA((n * 3,))],
        name=name)(*lands))


def relay_start(lands, name):
    n = len(lands)

    def body(*refs):
        land_refs = refs[:n]
        send_sems, recv_sems = refs[n], refs[n + 1]
        token = refs[-1]
        x, y, c = lax.axis_index("x"), lax.axis_index("y"), lax.axis_index("c")
        for a in range(n):
            for k, mask in enumerate((2, 4, 6)):
                px, py, _ = _peer(mask)
                block = land_refs[a].at[4 * px + 2 * py + c]
                pltpu.make_async_remote_copy(
                    src_ref=block, dst_ref=block, send_sem=send_sems.at[a * 3 + k], recv_sem=recv_sems.at[a * 3 + k],
                    device_id=(x, y, 1 - c), device_id_type=MESH).start()
        token[...] = jnp.zeros_like(token)

    out = pl.pallas_call(
        body, name=name,
        out_shape=(pltpu.SemaphoreType.DMA((n * 3,)), pltpu.SemaphoreType.DMA((n * 3,)),
                   *[pltpu.HBM(a.shape, a.dtype) for a in lands], SDS((8, LANES), F32)),
        in_specs=(HBM_SPEC,) * n,
        out_specs=(SEM_SPEC, SEM_SPEC) + (HBM_SPEC,) * n + (pl.BlockSpec(memory_space=pltpu.VMEM),),
        input_output_aliases={k: 2 + k for k in range(n)},
        compiler_params=pltpu.CompilerParams(has_side_effects=SIDE_EFFECT),
    )(*lands)
    return out[0], out[1], list(out[2:2 + n]), out[-1]


def relay_wait(started, after, name):
    send_sems, recv_sems, lands, _ = started
    n = len(lands)

    def body(*refs):
        land_refs = refs[:n]
        send_sems, recv_sems = refs[n], refs[n + 1]
        x, y, c = lax.axis_index("x"), lax.axis_index("y"), lax.axis_index("c")
        for a in range(n):
            for k in range(3):
                copy = pltpu.make_async_remote_copy(
                    src_ref=land_refs[a].at[0], dst_ref=land_refs[a].at[0], send_sem=send_sems.at[a * 3 + k],
                    recv_sem=recv_sems.at[a * 3 + k], device_id=(x, y, 1 - c), device_id_type=MESH)
                copy.wait_send()
                copy.wait_recv()

    return list(pl.pallas_call(
        body, name=name,
        out_shape=tuple(pltpu.HBM(a.shape, a.dtype) for a in lands),
        in_specs=(HBM_SPEC,) * n + (SEM_SPEC, SEM_SPEC, pl.BlockSpec(memory_space=pl.ANY)),
        out_specs=(HBM_SPEC,) * n, input_output_aliases={k: k for k in range(n)},
        compiler_params=pltpu.CompilerParams(has_side_effects=SIDE_EFFECT),
    )(*lands, send_sems, recv_sems, after))


IN_SHARD = NIN // N_DEV
SLOT_W = 768


def _slot_window(j):
    return (IN_SHARD * j // LANES) * LANES, -(-(IN_SHARD * (j + 1)) // LANES) * LANES


def _placement(j):
    a, b = _slot_window(j)
    r = lax.broadcasted_iota(jnp.int32, (SLOT_W, b - a), 0)
    c = lax.broadcasted_iota(jnp.int32, (SLOT_W, b - a), 1)
    return jnp.where(jnp.logical_and(c == r + (IN_SHARD * j - a), r < IN_SHARD), 1.0, 0.0).astype(BF16)


def assemble_w_in(land):
    tm = 256

    def body(l_ref, o_ref, acc):
        acc[...] = jnp.zeros_like(acc)
        for j in range(N_DEV):
            a, b = _slot_window(j)
            acc[:, a:b] += _nn(l_ref[j], _placement(j))
        o_ref[...] = acc[...].astype(BF16)

    return pl.pallas_call(
        body, grid=(D // tm,),
        in_specs=[pl.BlockSpec((N_DEV, tm, SLOT_W), lambda i: (0, i, 0))],
        out_specs=pl.BlockSpec((tm, NINP), lambda i: (i, 0)),
        out_shape=SDS((D, NINP), BF16),
        scratch_shapes=[pltpu.VMEM((tm, NINP), F32)],
        compiler_params=_cparams(("parallel",)), name="assemble_w_in")(land)


def scatter_w_in(dw):
    tm = 256

    def body(d_ref, o_ref):
        for j in range(N_DEV):
            a, b = _slot_window(j)
            o_ref[j] = _nt(d_ref[:, a:b], _placement(j)).astype(BF16)

    return pl.pallas_call(
        body, grid=(D // tm,),
        in_specs=[pl.BlockSpec((tm, NINP), lambda i: (i, 0))],
        out_specs=pl.BlockSpec((N_DEV, tm, SLOT_W), lambda i: (0, i, 0)),
        out_shape=SDS((N_DEV, D, SLOT_W), BF16),
        compiler_params=_cparams(("parallel",)), name="scatter_w_in")(dw)


def _adamw_math(g, w_ref, m_ref, v_ref, g_ref, d_ref, nm_ref, nv_ref):
    mn = ADAM_B1 * m_ref[...] + (1.0 - ADAM_B1) * g
    vn = ADAM_B2 * v_ref[...] + (1.0 - ADAM_B2) * jnp.square(g)
    m_hat = mn / (1.0 - ADAM_B1 ** ADAM_STEP)
    v_hat = vn / (1.0 - ADAM_B2 ** ADAM_STEP)
    g_ref[...] = g
    d_ref[...] = -ADAM_LR * (m_hat / (jnp.sqrt(v_hat) + ADAM_EPS) + ADAM_WD * w_ref[...])
    nm_ref[...] = mn
    nv_ref[...] = vn


def adamw_layers(w, slots, m, v, name):
    depth, r_len, c_len = w.shape
    cs = slots[0].shape[2]
    br = min(128, r_len)
    assert r_len % br == 0

    def body(w_ref, *rest):
        s_refs, (m_ref, v_ref, g_ref, d_ref, nm_ref, nv_ref) = rest[:depth], rest[depth:]
        layer = pl.program_id(0)
        for k in range(depth):
            @pl.when(layer == k)
            def _(k=k):
                g = s_refs[k][0, :, 0:c_len].astype(F32)
                for j in range(1, N_DEV):
                    g = g + s_refs[k][j, :, 0:c_len].astype(F32)
                _adamw_math(g, w_ref, m_ref, v_ref, g_ref, d_ref, nm_ref, nv_ref)

    spec = pl.BlockSpec((None, br, c_len), lambda l, i: (l, i, 0))
    s_specs = [pl.BlockSpec((N_DEV, br, cs), lambda l, i, k=k: (0, jnp.where(l == k, i, 0), 0))
               for k in range(depth)]
    return pl.pallas_call(
        body, grid=(depth, r_len // br),
        in_specs=[spec] + s_specs + [spec, spec],
        out_specs=[spec] * 4, out_shape=[SDS(w.shape, F32)] * 4,
        compiler_params=_cparams(("arbitrary", "arbitrary")), name=name)(w, *slots, m, v)


def adamw(w, slots, m, v, name):
    r_len, c_len = w.shape
    br = r_len if r_len <= 512 else 512
    assert r_len % br == 0

    def body(w_ref, s_ref, m_ref, v_ref, g_ref, d_ref, nm_ref, nv_ref):
        g = s_ref[0].astype(F32)
        for k in range(1, N_DEV):
            g = g + s_ref[k].astype(F32)
        _adamw_math(g, w_ref, m_ref, v_ref, g_ref, d_ref, nm_ref, nv_ref)

    spec = pl.BlockSpec((br, c_len), lambda i: (i, 0))
    return pl.pallas_call(
        body, grid=(r_len // br,),
        in_specs=[spec, pl.BlockSpec((N_DEV, br, c_len), lambda i: (0, i, 0)), spec, spec],
        out_specs=[spec] * 4, out_shape=[SDS((r_len, c_len), F32)] * 4,
        compiler_params=_cparams(("parallel",)), name=name)(w, slots, m, v)


def _adamw_nd(w, slots, m, v, name):
    shp = w.shape
    r = int(np.prod(shp[:-1]))
    outs = adamw(w.reshape(r, shp[-1]), slots.reshape(N_DEV, r, shp[-1]), m.reshape(r, shp[-1]),
                 v.reshape(r, shp[-1]), name)
    return [o.reshape(shp) for o in outs]


SMALL = [("norm_mix_w", DEPTH * D), ("ssd_conv_b", DEPTH * XBC), ("dt_bias", DEPTH * NHEAD),
         ("a_log", DEPTH * NHEAD), ("d_skip", DEPTH * NHEAD), ("ssd_norm_w", DEPTH * D),
         ("norm_mlp_w", DEPTH * D), ("final_norm_w", D)]
SMALL_LEN = sum(s for _, s in SMALL)
SMALL_ROWS = -(-SMALL_LEN // LANES)


def _pack_small(parts):
    flat = jnp.concatenate([parts[k].reshape(-1) for k, _ in SMALL])
    return jnp.pad(flat, (0, SMALL_ROWS * LANES - SMALL_LEN)).reshape(SMALL_ROWS, LANES)


def _unpack_small(packed, shapes):
    flat = packed.reshape(-1)
    out, off = {}, 0
    for k, s in SMALL:
        out[k] = flat[off:off + s].reshape(shapes[k])
        off += s
    return out


def kernel(x, norm_mix_w, w_in, short_conv_w, ssd_conv_w, ssd_conv_b, dt_bias, a_log, d_skip, ssd_norm_w, w_out, norm_mlp_w, w_up, w_down, final_norm_w, loss_target, m_norm_mix_w, m_w_in, m_short_conv_w, m_ssd_conv_w, m_ssd_conv_b, m_dt_bias, m_a_log, m_d_skip, m_ssd_norm_w, m_w_out, m_norm_mlp_w, m_w_up, m_w_down, m_final_norm_w, v_norm_mix_w, v_w_in, v_short_conv_w, v_ssd_conv_w, v_ssd_conv_b, v_dt_bias, v_a_log, v_d_skip, v_ssd_norm_w, v_w_out, v_norm_mlp_w, v_w_up, v_w_down, v_final_norm_w):
    xs = x[0]
    t_len = xs.shape[0]
    tt = min(256, t_len)
    eh, eht = _head_matrices()
    stacked = stacked_params(ssd_conv_b, dt_bias, a_log, d_skip, ssd_norm_w)
    me = 4 * lax.axis_index("x") + 2 * lax.axis_index("y") + lax.axis_index("c")

    def start_weights(i, after):
        first = exchange_start(
            [jnp.pad(w_in[i].astype(BF16), ((0, 0), (0, SLOT_W - IN_SHARD))), short_conv_w[i], ssd_conv_w[i]],
            False, "w_in_start_%d" % i, after, SIBLING_AND_CHIPS)
        rest = exchange_start([w_out[i].astype(BF16), w_up[i].astype(BF16), w_down[i].astype(BF16)], False,
                              "w_rest_start_%d" % i, first[4] if after is None else after, SIBLING_AND_CHIPS)
        return first, rest

    def fill_own(srcs, lands, per_peer):
        own = [lax.dynamic_index_in_dim(s_, me, 0, keepdims=False) for s_ in srcs] if per_peer else srcs
        return [lax.dynamic_update_index_in_dim(l_, o_, me, 0) for l_, o_ in zip(lands, own)]

    def finish_weights(started, after, name):
        srcs, lands = exchange_wait(started, after, False, name + "_wait", SIBLING_AND_CHIPS)
        return fill_own(srcs, relay_to_sibling(lands, name + "_relay"), False)

    act = xs
    saved, layers = [], []
    first, rest = start_weights(0, None)
    token = first[4][0, 0] + rest[4][0, 0]
    pending = []
    for i in range(DEPTH):
        if i == 0:
            g_in, g_sc, g_cw = finish_weights(first, act, "w_in_0")
        else:
            srcs_f, relay_f = pending.pop()
            g_in, g_sc, g_cw = fill_own(srcs_f, relay_wait(relay_f, act, "w_in_%d_relay_wait" % i), False)
        lw, prm = layer_params(
            i, assemble_w_in(g_in), g_sc.transpose(1, 0, 2).reshape(3, D), g_cw.transpose(1, 0, 2).reshape(4, XBC),
            norm_mix_w[i], norm_mlp_w[i], stacked, eh, eht)
        lw["nw1"] = lw["nw1"] + token
        proj, h1 = norm_matmul(act, lw["nw1"], lw["win"], "in_proj")
        srcs_r, lands_r = exchange_wait(rest, proj, False, "w_rest_%d_wait" % i, SIBLING_AND_CHIPS)
        relay = relay_start(lands_r, "w_rest_%d_relay_start" % i)
        y, st, aux = mixer_fwd(proj, dict(prm, nrm=prm["nrm"] + relay[3][0, 0]), tt)
        mixed = (h1, proj, (st, aux), y)
        g_out, g_up, g_dn = fill_own(srcs_r, relay_wait(relay, y, "w_rest_%d_relay_wait" % i), False)
        lw.update(wout=g_out.reshape(MIX, D), wup=g_up, wdn=g_dn.reshape(DFF, D))
        between = None
        if i + 1 < DEPTH:
            first, rest = start_weights(i + 1, g_dn)
            token = first[4][0, 0] + rest[4][0, 0]
            lw["nw2"] = lw["nw2"] + token

            def between(u, first=first, nxt=i + 1):
                srcs_n, lands_n = exchange_wait(first, u, False, "w_in_%d_wait" % nxt, SIBLING_AND_CHIPS)
                started = relay_start(lands_n, "w_in_%d_relay_start" % nxt)
                pending.append((srcs_n, started))
                return started[3]
        layers.append((lw, prm))
        act, sv = layer_fwd_mlp(act, mixed, lw, between)
        saved.append(sv)
    loss_acc, dx, dxb, g_fw = loss_head(act, final_norm_w[None, :], loss_target[0])

    grads = [None] * DEPTH
    sent_rest, sent_in = [None] * DEPTH, [None] * DEPTH
    token = None
    for i in reversed(range(DEPTH)):
        lw, prm = layers[i]
        if token is not None:
            lw = dict(lw, nw2=lw["nw2"] + token)
        dx1, _, dy, g_mlp = layer_bwd_mlp(dx, dxb, lw, saved[i])
        sent_rest[i] = exchange_start(
            [g_mlp["wout"].reshape(N_DEV, MIX // N_DEV, D), g_mlp["wup"], g_mlp["wdn"].reshape(N_DEV, DFF // N_DEV, D)],
            True, "g_rest_start_%d" % i)
        dx, dxb, g_mix = layer_bwd_mix(dx1, dy, lw, dict(prm, nrm=prm["nrm"] + sent_rest[i][4][0, 0]), saved[i], tt)
        grads[i] = {**g_mlp, **g_mix}
        if i > 0:
            sent_in[i] = exchange_start([scatter_w_in(g_mix["win"])], True, "g_in_start_%d" % i)
            token = sent_in[i][4][0, 0]

    def stack(k):
        return jnp.stack([g[k] for g in grads])

    small = _pack_small({"norm_mix_w": stack("nw1"), "ssd_conv_b": stack("cb"), "dt_bias": stack("dtb"),
                         "a_log": stack("alog"), "d_skip": stack("dsk"), "ssd_norm_w": stack("nrm"),
                         "norm_mlp_w": stack("nw2"), "final_norm_w": g_fw[0]})
    r_small, r_sc, r_cw = all_gather([small, stack("scw"), stack("cw")], "gather_small_grads")
    r_sc = lax.dynamic_slice_in_dim(r_sc, me * (D // N_DEV), D // N_DEV, axis=3)
    r_cw = lax.dynamic_slice_in_dim(r_cw, me * (XBC // N_DEV), XBC // N_DEV, axis=3)
    sent_in[0] = exchange_start([scatter_w_in(grads[0]["win"])], True, "g_in_start_0", after=r_small)

    after = sent_in[0][4]
    recv = [fill_own(*exchange_wait(sent_rest[i], after, True, "g_rest_wait_%d" % i), True) for i in range(DEPTH)]
    res = {}
    res["w_out"] = adamw_layers(w_out, [r[0] for r in recv], m_w_out, v_w_out, "adamw_w_out")
    res["w_up"] = adamw_layers(w_up, [r[1] for r in recv], m_w_up, v_w_up, "adamw_w_up")
    res["w_down"] = adamw_layers(w_down, [r[2] for r in recv], m_w_down, v_w_down, "adamw_w_down")
    after = res["w_down"][1]
    recv_in = [fill_own(*exchange_wait(sent_in[i], after, True, "g_in_wait_%d" % i), True)[0] for i in range(DEPTH)]
    res["w_in"] = adamw_layers(w_in, recv_in, m_w_in, v_w_in, "adamw_w_in")
    res["short_conv_w"] = _adamw_nd(short_conv_w, r_sc, m_short_conv_w, v_short_conv_w, "adamw_short_conv")
    res["ssd_conv_w"] = _adamw_nd(ssd_conv_w, r_cw, m_ssd_conv_w, v_ssd_conv_w, "adamw_ssd_conv")
    small_w = {"norm_mix_w": norm_mix_w, "ssd_conv_b": ssd_conv_b, "dt_bias": dt_bias, "a_log": a_log,
               "d_skip": d_skip, "ssd_norm_w": ssd_norm_w, "norm_mlp_w": norm_mlp_w, "final_norm_w": final_norm_w}
    small_m = {"norm_mix_w": m_norm_mix_w, "ssd_conv_b": m_ssd_conv_b, "dt_bias": m_dt_bias, "a_log": m_a_log,
               "d_skip": m_d_skip, "ssd_norm_w": m_ssd_norm_w, "norm_mlp_w": m_norm_mlp_w,
               "final_norm_w": m_final_norm_w}
    small_v = {"norm_mix_w": v_norm_mix_w, "ssd_conv_b": v_ssd_conv_b, "dt_bias": v_dt_bias, "a_log": v_a_log,
               "d_skip": v_d_skip, "ssd_norm_w": v_ssd_norm_w, "norm_mlp_w": v_norm_mlp_w,
               "final_norm_w": v_final_norm_w}
    shapes = {k: a.shape for k, a in small_w.items()}
    packed = adamw(_pack_small(small_w), r_small, _pack_small(small_m), _pack_small(small_v), "adamw_small")
    unpacked = [_unpack_small(p, shapes) for p in packed]
    for k in small_w:
        res[k] = [u[k] for u in unpacked]

    loss = lax.psum(loss_acc[0, 0], ("x", "y", "c"))
    order = ["norm_mix_w", "w_in", "short_conv_w", "ssd_conv_w", "ssd_conv_b", "dt_bias", "a_log", "d_skip",
             "ssd_norm_w", "w_out", "norm_mlp_w", "w_up", "w_down", "final_norm_w"]
    out = [loss, dx[None]]
    for part in range(4):
        out += [res[k][part] for k in order]
    return tuple(out)
```

```python
import functools

import numpy as np
import jax
import jax.numpy as jnp
from jax import lax
from jax.experimental import pallas as pl
from jax.experimental.pallas import tpu as pltpu

F32 = jnp.float32
BF16 = jnp.bfloat16
SDS = jax.ShapeDtypeStruct

N_DEV = 8
DEPTH = 4
D = 1024
NIN = 5648
NINP = 5760
DFF = 4096
MIX = 2048
NHEAD = 16
HDIM = 64
NSTATE = 128
CHUNK = 64
XBC = 1536
EPS = 1e-5
LANES = 128
NEG_BIG = -1e30

C_UB, C_UC, C_UH, C_Z, C_XS, C_BC, C_DT = 0, 1024, 2048, 3072, 4096, 5120, 5632
A_CV, A_YS, A_PX, A_PBC, AUX_W = 0, 1024, 2048, 3072, 3584

ADAM_LR = 0.001
ADAM_B1 = 0.9
ADAM_B2 = 0.999
ADAM_EPS = 1e-08
ADAM_WD = 0.01
ADAM_STEP = 10

VMEM_LIMIT = 56 * 1024 * 1024
MESH = pl.DeviceIdType.MESH


def _cparams(sem):
    return pltpu.CompilerParams(dimension_semantics=sem, vmem_limit_bytes=VMEM_LIMIT)


def _nt(a, b):
    return lax.dot_general(a, b, (((1,), (1,)), ((), ())), preferred_element_type=F32)


def _tn(a, b):
    return lax.dot_general(a, b, (((0,), (0,)), ((), ())), preferred_element_type=F32)


def _nn(a, b):
    return jnp.dot(a, b, preferred_element_type=F32)


def _sigmoid(v):
    return 0.5 * jnp.tanh(0.5 * v) + 0.5


def _split3(v):
    v1 = v.astype(BF16)
    r1 = v - v1.astype(F32)
    v2 = r1.astype(BF16)
    v3 = (r1 - v2.astype(F32)).astype(BF16)
    return v1, v2, v3


def _expand(v, eh):
    v1, v2, v3 = _split3(v)
    return _nn(v1, eh) + _nn(v2, eh) + _nn(v3, eh)


def _head_reduce(v, eht):
    v1 = v.astype(BF16)
    v2 = (v - v1.astype(F32)).astype(BF16)
    return _nn(v1, eht) + _nn(v2, eht)


def _head_matrices():
    eh = np.zeros((LANES, D), np.float32)
    for h in range(NHEAD):
        eh[h, h * HDIM:(h + 1) * HDIM] = 1.0
    return jnp.asarray(eh, BF16), jnp.asarray(eh.T.copy(), BF16)


def _resident(shape):
    return pl.BlockSpec(shape, lambda *_: (0,) * len(shape), pipeline_mode=pl.Buffered(1))


def _col_chunks(n, step):
    return [(c, min(c + step, n)) for c in range(0, n, step)]


def norm_matmul(x, nw, w, name):
    t_len = x.shape[0]
    blocked = w.ndim == 3
    n_len = w.shape[0] * w.shape[2] if blocked else w.shape[1]
    tm = min(512, t_len)
    chunks = _col_chunks(n_len, n_len // N_DEV if blocked else 1536)

    def body(x_ref, nw_ref, w_ref, o_ref, h_ref):
        xv = x_ref[...]
        r = lax.rsqrt(jnp.mean(xv * xv, axis=-1, keepdims=True) + EPS)
        hv = (xv * r * nw_ref[...]).astype(BF16)
        h_ref[...] = hv
        for j, (c0, c1) in enumerate(chunks):
            wj = w_ref[j] if blocked else w_ref[:, c0:c1]
            o_ref[:, c0:c1] = _nn(hv, wj).astype(o_ref.dtype)

    return pl.pallas_call(
        body, grid=(t_len // tm,),
        in_specs=[pl.BlockSpec((tm, D), lambda i: (i, 0)), _resident((1, D)), _resident(w.shape)],
        out_specs=[pl.BlockSpec((tm, n_len), lambda i: (i, 0)),
                   pl.BlockSpec((tm, D), lambda i: (i, 0))],
        out_shape=[SDS((t_len, n_len), BF16), SDS((t_len, D), BF16)],
        compiler_params=_cparams(("parallel",)), name=name)(x, nw, w)


def matmul_residual(a, w, res, relu2, name, after=None):
    t_len, k_len = a.shape
    tm = min(512, t_len)

    def body(a_ref, w_ref, res_ref, *rest):
        o_ref = rest[-1]
        av = a_ref[...]
        if relu2:
            af = jnp.maximum(av.astype(F32), 0.0)
            av = (af * af).astype(BF16)
        o_ref[...] = res_ref[...] + _nn(av, w_ref[...])

    extra = [] if after is None else [after]
    return pl.pallas_call(
        body, grid=(t_len // tm,),
        in_specs=[pl.BlockSpec((tm, k_len), lambda i: (i, 0)),
                  _resident((k_len, D)),
                  pl.BlockSpec((tm, D), lambda i: (i, 0))] + [pl.BlockSpec(memory_space=pl.ANY)] * len(extra),
        out_specs=pl.BlockSpec((tm, D), lambda i: (i, 0)),
        out_shape=SDS((t_len, D), F32),
        compiler_params=_cparams(("parallel",)), name=name)(a, w, res, *extra)


def matmul_nt_act(dy, w, u, name):
    t_len = dy.shape[0]
    n_len = w.shape[0]
    tm = min(512, t_len)
    chunks = _col_chunks(n_len, 1024)

    def body(dy_ref, w_ref, *rest):
        if u is None:
            (o_ref,) = rest
        else:
            u_ref, o_ref = rest
        dyv = dy_ref[...]
        for c0, c1 in chunks:
            p = _nt(dyv, w_ref[c0:c1, :])
            if u is not None:
                p = p * (2.0 * jnp.maximum(u_ref[:, c0:c1].astype(F32), 0.0))
            o_ref[:, c0:c1] = p.astype(o_ref.dtype)

    in_specs = [pl.BlockSpec((tm, D), lambda i: (i, 0)), _resident((n_len, D))]
    args = [dy, w]
    if u is not None:
        in_specs.append(pl.BlockSpec((tm, n_len), lambda i: (i, 0)))
        args.append(u)
    return pl.pallas_call(
        body, grid=(t_len // tm,),
        in_specs=in_specs,
        out_specs=pl.BlockSpec((tm, n_len), lambda i: (i, 0)),
        out_shape=SDS((t_len, n_len), BF16),
        compiler_params=_cparams(("parallel",)), name=name)(*args)


def matmul_tn(a, b, a_spec, b_spec, o_spec, o_shape, n_out, relu2, name, tt_max=2048):
    t_len = a.shape[0]
    tt = min(tt_max, t_len)
    nt = t_len // tt

    def body(a_ref, b_ref, o_ref, acc):
        t = pl.program_id(1)
        av = a_ref[...]
        if relu2:
            af = jnp.maximum(av.astype(F32), 0.0)
            av = (af * af).astype(BF16)
        p = _tn(av, b_ref[...])

        @pl.when(t == 0)
        def _():
            acc[...] = p

        @pl.when(t > 0)
        def _():
            acc[...] += p

        @pl.when(t == nt - 1)
        def _():
            if len(blk) == 3:
                for j in range(blk[0]):
                    o_ref[j] = acc[:, j * blk[2]:(j + 1) * blk[2]].astype(o_ref.dtype)
            else:
                o_ref[...] = acc[...].astype(o_ref.dtype)

    blk = tuple(o_spec.block_shape)
    acc_shape = (blk[1], blk[0] * blk[2]) if len(blk) == 3 else blk
    return pl.pallas_call(
        body, grid=(n_out, nt),
        in_specs=[a_spec(tt), b_spec(tt)],
        out_specs=o_spec, out_shape=o_shape,
        scratch_shapes=[pltpu.VMEM(acc_shape, F32)],
        compiler_params=_cparams(("parallel", "arbitrary")), name=name)(a, b)


def matmul_nt_norm_bwd(dy, w, x, nw, dres, name):
    t_len = x.shape[0]
    blocked = w.ndim == 3
    k_len = dy.shape[1]
    kb = k_len // N_DEV
    tm = min(512, t_len)

    def body(dy_ref, w_ref, x_ref, nw_ref, dres_ref, dx_ref, dxb_ref, dnw_ref):
        @pl.when(pl.program_id(0) == 0)
        def _():
            dnw_ref[...] = jnp.zeros_like(dnw_ref)

        if blocked:
            dh = _nt(dy_ref[:, 0:kb], w_ref[0])
            for j in range(1, N_DEV):
                dh = dh + _nt(dy_ref[:, j * kb:(j + 1) * kb], w_ref[j])
        else:
            dh = _nt(dy_ref[...], w_ref[...])
        xv = x_ref[...]
        r = lax.rsqrt(jnp.mean(xv * xv, axis=-1, keepdims=True) + EPS)
        xh = xv * r
        dnw_ref[0:1, :] += jnp.sum(dh * xh, axis=0, keepdims=True)
        g = dh * nw_ref[...]
        dx = dres_ref[...] + r * (g - xh * jnp.mean(g * xh, axis=-1, keepdims=True))
        dx_ref[...] = dx
        dxb_ref[...] = dx.astype(BF16)

    return pl.pallas_call(
        body, grid=(t_len // tm,),
        in_specs=[pl.BlockSpec((tm, k_len), lambda i: (i, 0)),
                  _resident(w.shape),
                  pl.BlockSpec((tm, D), lambda i: (i, 0)),
                  _resident((1, D)),
                  pl.BlockSpec((tm, D), lambda i: (i, 0))],
        out_specs=[pl.BlockSpec((tm, D), lambda i: (i, 0)),
                   pl.BlockSpec((tm, D), lambda i: (i, 0)),
                   pl.BlockSpec((8, D), lambda i: (0, 0))],
        out_shape=[SDS((t_len, D), F32), SDS((t_len, D), BF16), SDS((8, D), F32)],
        compiler_params=_cparams(("arbitrary",)), name=name)(dy, w, x, nw, dres)


def loss_head(x, fw, tgt):
    t_len = x.shape[0]
    tm = min(512, t_len)

    def body(x_ref, fw_ref, t_ref, loss_ref, dx_ref, dxb_ref, dfw_ref):
        @pl.when(pl.program_id(0) == 0)
        def _():
            loss_ref[...] = jnp.zeros_like(loss_ref)
            dfw_ref[...] = jnp.zeros_like(dfw_ref)
        xv = x_ref[...]
        r = lax.rsqrt(jnp.mean(xv * xv, axis=-1, keepdims=True) + EPS)
        xh = xv * r
        w = fw_ref[...]
        e = xh * w - t_ref[...]
        row = jnp.sum(e * e, axis=-1, keepdims=True) * (1.0 / D)
        loss_ref[...] += 0.5 * jnp.sum(row, axis=0, keepdims=True)
        dyf = e * (1.0 / D)
        dfw_ref[0:1, :] += jnp.sum(dyf * xh, axis=0, keepdims=True)
        g = dyf * w
        dx = r * (g - xh * jnp.mean(g * xh, axis=-1, keepdims=True))
        dx_ref[...] = dx
        dxb_ref[...] = dx.astype(BF16)

    return pl.pallas_call(
        body, grid=(t_len // tm,),
        in_specs=[pl.BlockSpec((tm, D), lambda i: (i, 0)),
                  pl.BlockSpec((1, D), lambda i: (0, 0)),
                  pl.BlockSpec((tm, D), lambda i: (i, 0))],
        out_specs=[pl.BlockSpec((8, LANES), lambda i: (0, 0)),
                   pl.BlockSpec((tm, D), lambda i: (i, 0)),
                   pl.BlockSpec((tm, D), lambda i: (i, 0)),
                   pl.BlockSpec((8, D), lambda i: (0, 0))],
        out_shape=[SDS((8, LANES), F32), SDS((t_len, D), F32), SDS((t_len, D), BF16), SDS((8, D), F32)],
        compiler_params=_cparams(("arbitrary",)), name="loss_head")(x, fw, tgt)


TAP_SHIFTS = (3, 2, 1)


def _shift_matrix(n, up):
    r = lax.broadcasted_iota(jnp.int32, (n, n), 0)
    c = lax.broadcasted_iota(jnp.int32, (n, n), 1)
    return jnp.concatenate([jnp.where(c == (r + j if up else r - j), 1.0, 0.0).astype(BF16) for j in TAP_SHIFTS],
                           axis=0)


def _shifts_dn(xb, halo, sm, n_shifts):
    n = xb.shape[0]
    first = len(TAP_SHIFTS) - n_shifts
    moved = _nn(sm[first * n:], xb)
    row = lax.broadcasted_iota(jnp.int32, halo.shape, 0)
    outs = []
    for k in range(n_shifts):
        j = TAP_SHIFTS[first + k]
        o = moved[k * n:(k + 1) * n]
        top = jnp.where(row < j, pltpu.roll(halo, j, 0), o[0:8])
        outs.append(jnp.concatenate([top, o[8:]], axis=0))
    return outs


def _shifts_up(xb, nxt, sm, n_shifts):
    n = xb.shape[0]
    first = len(TAP_SHIFTS) - n_shifts
    moved = _nn(sm[first * n:], xb)
    row = lax.broadcasted_iota(jnp.int32, nxt.shape, 0)
    outs = []
    for k in range(n_shifts):
        j = TAP_SHIFTS[first + k]
        o = moved[k * n:(k + 1) * n]
        bot = jnp.where(row >= 8 - j, pltpu.roll(nxt, 8 - j, 0), o[n - 8:n])
        outs.append(jnp.concatenate([o[:n - 8], bot], axis=0))
    return outs


def _conv_fwd(x, xb, halo, w_ref, kw, sm):
    shifted = _shifts_dn(xb, halo, sm, kw - 1)
    acc = w_ref[kw - 1:kw, :] * x
    for k in range(kw - 1):
        acc = acc + w_ref[k:k + 1, :] * shifted[k]
    return acc


def _chunk_cumsum(a, pos):
    for sh in (1, 2, 4, 8, 16, 32):
        a = a + jnp.where(pos >= sh, pltpu.roll(a, sh, 0), 0.0)
    return a


def _chunk_rcumsum(a, pos):
    n = a.shape[0]
    for sh in (1, 2, 4, 8, 16, 32):
        a = a + jnp.where(pos < CHUNK - sh, pltpu.roll(a, n - sh, 0), 0.0)
    return a


def _softplus(v):
    return jnp.maximum(v, 0.0) + jnp.log(1.0 + jnp.exp(-jnp.abs(v)))


def _silu(v):
    return v * _sigmoid(v)


def _dsilu(v):
    s = _sigmoid(v)
    return s * (1.0 + v * (1.0 - s))


def _lane_masks(width=D):
    lane = lax.broadcasted_iota(jnp.int32, (CHUNK, width), 1) & (HDIM - 1)
    row = lax.broadcasted_iota(jnp.int32, (CHUNK, width), 0)
    return lane == row, lane <= row


def _rep_matrix():
    lane = lax.broadcasted_iota(jnp.int32, (CHUNK, 512), 1) & (HDIM - 1)
    row = lax.broadcasted_iota(jnp.int32, (CHUNK, 512), 0)
    return jnp.where(lane == row, 1.0, 0.0).astype(BF16)


def _blockdiag(xp):
    lane = lax.broadcasted_iota(jnp.int32, xp.shape, 1)
    zero = jnp.zeros_like(xp)
    return jnp.concatenate([jnp.where(lane < HDIM, xp, zero), jnp.where(lane >= HDIM, xp, zero)], axis=0)


def _mixer_views(tt):
    r8 = tt // 8

    def main(width, col):
        return pl.BlockSpec((tt, width), lambda i, c=col // width: (i, c))

    def halo(width, col):
        return pl.BlockSpec((8, width), lambda i, c=col // width: (jnp.maximum(i * r8 - 1, 0), c))

    return main, halo


def mixer_fwd(proj, prm, tt):
    t_len = proj.shape[0]
    nblk = t_len // tt
    nc = tt // CHUNK
    main, halo = _mixer_views(tt)

    def body(ub_ref, uc_ref, uh_ref, z_ref, xr_ref, bcr_ref, dtr_ref, uch_ref, uhh_ref, xrh_ref, bcrh_ref,
             scw_ref, cwx_ref, cwbc_ref, cbx_ref, cbbc_ref, dtb_ref, alog_ref, dsk_ref, nrm_ref, eh_ref,
             y_ref, st_ref, aux_ref, hs, xs_s, bc_s, dtx_s, cumx_s, yssd_s):
        i = pl.program_id(0)
        first = i == 0

        @pl.when(first)
        def _():
            hs[...] = jnp.zeros_like(hs)

        keep = jnp.where(first, 0.0, 1.0)
        sm = _shift_matrix(tt, False)
        v = uc_ref[...].astype(F32) * uh_ref[...].astype(F32)
        vh = uch_ref[...].astype(F32) * uhh_ref[...].astype(F32) * keep
        cv = _conv_fwd(v, v.astype(BF16), vh, scw_ref, 3, sm)
        aux_ref[:, A_CV:A_CV + D] = cv.astype(BF16)
        y_ref[:, 0:D] = (ub_ref[...].astype(F32) * cv).astype(BF16)

        xrb = xr_ref[...]
        pre_x = _conv_fwd(xrb.astype(F32), xrb, xrh_ref[...].astype(F32) * keep, cwx_ref, 4, sm) + cbx_ref[...]
        aux_ref[:, A_PX:A_PX + D] = pre_x.astype(BF16)
        xs_s[...] = _silu(pre_x)
        bcrb = bcr_ref[...]
        pre_bc = _conv_fwd(bcrb.astype(F32), bcrb, bcrh_ref[...].astype(F32) * keep, cwbc_ref, 4, sm) + cbbc_ref[...]
        aux_ref[:, A_PBC:A_PBC + 512] = pre_bc.astype(BF16)
        bc_s[...] = _silu(pre_bc)
        dt = _softplus(dtr_ref[...].astype(F32) + dtb_ref[...])
        a_neg = -jnp.exp(alog_ref[...])
        pos = lax.broadcasted_iota(jnp.int32, (tt, LANES), 0) & (CHUNK - 1)
        cum = _chunk_cumsum(dt * a_neg, pos)
        eh = eh_ref[...]
        dtx_s[...] = _expand(dt, eh)
        cumx_s[...] = _expand(cum, eh)
        irep, causal = _lane_masks()
        rep = _rep_matrix()

        def chunk(c, carry):
            r0 = pl.multiple_of(c * CHUNK, CHUNK)
            rows = pl.ds(r0, CHUNK)
            cumx = cumx_s[rows, :]
            cum_l = cumx[CHUNK - 1:CHUNK, :]
            xd = xs_s[rows, :] * dtx_s[rows, :]
            xf = xd * jnp.exp(cum_l - cumx)
            ex = jnp.exp(cumx)
            e_l = jnp.exp(cum_l)
            rvec = jnp.sum(jnp.where(irep, cumx, 0.0), axis=0, keepdims=True)
            lam = jnp.exp(jnp.where(causal, cumx - rvec, NEG_BIG))
            bc = bc_s[rows, :]
            for g in range(2):
                gs = slice(g * 512, (g + 1) * 512)
                bg = bc[:, g * NSTATE:(g + 1) * NSTATE].astype(BF16)
                cg = bc[:, 256 + g * NSTATE:256 + (g + 1) * NSTATE].astype(BF16)
                s_rep = _nn(_nt(cg, bg).astype(BF16), rep)
                m_g = (s_rep * lam[:, gs]).astype(BF16)
                h_g = hs[:, gs]
                h_b = h_g.astype(BF16)
                st_ref[c, :, gs] = h_b
                yo = _nn(cg, h_b) * ex[:, gs]
                xd_b = xd[:, gs].astype(BF16)
                for hp in range(4):
                    ps = slice(hp * LANES, (hp + 1) * LANES)
                    yd = _nn(m_g[:, ps], _blockdiag(xd_b[:, ps]))
                    yssd_s[rows, g * 512 + hp * LANES:g * 512 + (hp + 1) * LANES] = yd + yo[:, ps]
                hs[:, gs] = h_g * e_l[:, gs] + _tn(bg, xf[:, gs].astype(BF16))
            return carry

        lax.fori_loop(0, nc, chunk, 0, unroll=True)

        ys = yssd_s[...] + dsk_ref[...] * xs_s[...]
        aux_ref[:, A_YS:A_YS + D] = ys.astype(BF16)
        gt = ys * _silu(z_ref[...].astype(F32))
        for g in range(2):
            gs = slice(g * 512, (g + 1) * 512)
            gg = gt[:, gs]
            rn = lax.rsqrt(jnp.mean(gg * gg, axis=-1, keepdims=True) + EPS)
            y_ref[:, D + g * 512:D + (g + 1) * 512] = (gg * rn * nrm_ref[:, gs]).astype(BF16)

    params = [prm[k] for k in ("scw", "cwx", "cwbc", "cbx", "cbbc", "dtb", "alog", "dskx", "nrm", "eh")]
    in_specs = [main(D, C_UB), main(D, C_UC), main(D, C_UH), main(D, C_Z), main(D, C_XS), main(512, C_BC),
                main(LANES, C_DT), halo(D, C_UC), halo(D, C_UH), halo(D, C_XS), halo(512, C_BC)]
    in_specs += [_param_spec(a, prm["layer"]) for a in params]
    return pl.pallas_call(
        body, grid=(nblk,),
        in_specs=in_specs,
        out_specs=[pl.BlockSpec((tt, MIX), lambda i: (i, 0)),
                   pl.BlockSpec((nc, NSTATE, D), lambda i: (i, 0, 0)),
                   pl.BlockSpec((tt, AUX_W), lambda i: (i, 0))],
        out_shape=[SDS((t_len, MIX), BF16), SDS((t_len // CHUNK, NSTATE, D), BF16), SDS((t_len, AUX_W), BF16)],
        scratch_shapes=[pltpu.VMEM((NSTATE, D), F32), pltpu.VMEM((tt, D), F32), pltpu.VMEM((tt, 512), F32),
                        pltpu.VMEM((tt, D), F32), pltpu.VMEM((tt, D), F32), pltpu.VMEM((tt, D), F32)],
        compiler_params=_cparams(("arbitrary",)), name="mixer_fwd")(*([proj] * 11), *params)


def mixer_bwd(proj, dy, states, aux, prm, tt):
    t_len = proj.shape[0]
    nblk = t_len // tt
    nc = tt // CHUNK

    def rev(i):
        return nblk - 1 - i

    def main(width, col):
        return pl.BlockSpec((tt, width), lambda i, c=col // width: (rev(i), c))

    def body(ub_ref, uc_ref, uh_ref, z_ref, xr_ref, bcr_ref, dtr_ref, dy_ref, st_ref, aux_ref,
             scw_ref, cwx_ref, cwbc_ref, cbx_ref, cbbc_ref, dtb_ref, alog_ref, dsk_ref, nrm_ref, eh_ref, eht_ref,
             dp_ref, gscw_ref, gcwx_ref, gcwbc_ref, gvec_ref, gdt_ref,
             dhs, xs_s, bc_s, dtx_s, cumx_s, dys_s, dxs_s, dbc_s, red_s, ddtx_s, nx_cv, nx_px, nx_pbc, sgx_s, sgbc_s):
        i = pl.program_id(0)

        @pl.when(i == 0)
        def _():
            dhs[...] = jnp.zeros_like(dhs)
            nx_cv[...] = jnp.zeros_like(nx_cv)
            nx_px[...] = jnp.zeros_like(nx_px)
            nx_pbc[...] = jnp.zeros_like(nx_pbc)
            gscw_ref[...] = jnp.zeros_like(gscw_ref)
            gcwx_ref[...] = jnp.zeros_like(gcwx_ref)
            gcwbc_ref[...] = jnp.zeros_like(gcwbc_ref)
            gvec_ref[...] = jnp.zeros_like(gvec_ref)
            gdt_ref[...] = jnp.zeros_like(gdt_ref)

        uc = uc_ref[...].astype(F32)
        uh = uh_ref[...].astype(F32)
        v = uc * uh
        dya = dy_ref[:, 0:D].astype(F32)
        dp_ref[:, C_UB:C_UB + D] = (dya * aux_ref[:, A_CV:A_CV + D].astype(F32)).astype(BF16)
        dcv = dya * ub_ref[...].astype(F32)
        sm = _shift_matrix(tt, True)
        ups = _shifts_up(dcv.astype(BF16), nx_cv[...], sm, 2) + [dcv]
        dv = None
        for k in range(3):
            gscw_ref[k:k + 1, :] += jnp.sum(v * ups[k], axis=0, keepdims=True)
            term = scw_ref[k:k + 1, :] * ups[k]
            dv = term if dv is None else dv + term
        nx_cv[...] = dcv[0:8]
        dp_ref[:, C_UC:C_UC + D] = (dv * uh).astype(BF16)
        dp_ref[:, C_UH:C_UH + D] = (dv * uc).astype(BF16)

        pre_x = aux_ref[:, A_PX:A_PX + D].astype(F32)
        pre_bc = aux_ref[:, A_PBC:A_PBC + 512].astype(F32)
        sg_x = _sigmoid(pre_x)
        sg_bc = _sigmoid(pre_bc)
        sgx_s[...] = sg_x
        sgbc_s[...] = sg_bc
        xs = pre_x * sg_x
        xs_s[...] = xs
        bc_s[...] = pre_bc * sg_bc
        dt_pre = dtr_ref[...].astype(F32) + dtb_ref[...]
        dt = _softplus(dt_pre)
        a_neg = -jnp.exp(alog_ref[...])
        pos = lax.broadcasted_iota(jnp.int32, (tt, LANES), 0) & (CHUNK - 1)
        cum = _chunk_cumsum(dt * a_neg, pos)
        eh = eh_ref[...]
        eht = eht_ref[...]
        dtx_s[...] = _expand(dt, eh)
        cumx_s[...] = _expand(cum, eh)

        irep, causal = _lane_masks()
        irep_g, _ = _lane_masks(512)
        rep = _rep_matrix()
        row8 = lax.broadcasted_iota(jnp.int32, (8, 512), 0)
        lane128 = lax.broadcasted_iota(jnp.int32, (CHUNK, LANES), 1)

        z = z_ref[...].astype(F32)
        sg_z = _sigmoid(z)
        sz = z * sg_z
        dsz = sg_z * (1.0 + z * (1.0 - sg_z))
        ys = aux_ref[:, A_YS:A_YS + D].astype(F32)
        gt = ys * sz
        dyb = dy_ref[:, D:MIX].astype(F32)
        for g in range(2):
            gs = slice(g * 512, (g + 1) * 512)
            gg = gt[:, gs]
            rn = lax.rsqrt(jnp.mean(gg * gg, axis=-1, keepdims=True) + EPS)
            gvec_ref[0:1, gs] += jnp.sum(dyb[:, gs] * gg * rn, axis=0, keepdims=True)
            dgn = dyb[:, gs] * nrm_ref[:, gs]
            dgt = rn * (dgn - gg * (rn * rn) * jnp.mean(dgn * gg, axis=-1, keepdims=True))
            dys = dgt * sz[:, gs]
            dys_s[:, gs] = dys
            dp_ref[:, C_Z + g * 512:C_Z + (g + 1) * 512] = (dgt * ys[:, gs] * dsz[:, gs]).astype(BF16)
        dys_all = dys_s[...]
        gvec_ref[1:2, :] += jnp.sum(dys_all * xs, axis=0, keepdims=True)

        def bwd_chunk(cc, carry):
            c = nc - 1 - cc
            r0 = pl.multiple_of(c * CHUNK, CHUNK)
            rows = pl.ds(r0, CHUNK)
            cumx = cumx_s[rows, :]
            cum_l = cumx[CHUNK - 1:CHUNK, :]
            xs_c = xs_s[rows, :]
            dtx = dtx_s[rows, :]
            xd = xs_c * dtx
            f = jnp.exp(cum_l - cumx)
            xf = xd * f
            ex = jnp.exp(cumx)
            e_l = jnp.exp(cum_l)
            rvec = jnp.sum(jnp.where(irep, cumx, 0.0), axis=0, keepdims=True)
            lam = jnp.exp(jnp.where(causal, cumx - rvec, NEG_BIG))
            bc = bc_s[rows, :]
            dyc = dys_s[rows, :]
            for g in range(2):
                gs = slice(g * 512, (g + 1) * 512)
                bg = bc[:, g * NSTATE:(g + 1) * NSTATE].astype(BF16)
                cg = bc[:, 256 + g * NSTATE:256 + (g + 1) * NSTATE].astype(BF16)
                h0 = st_ref[c, :, gs]
                dh = dhs[:, gs]
                dh_b = dh.astype(BF16)
                xf_g = xf[:, gs]
                dxf = _nn(bg, dh_b)
                db = _nt(xf_g.astype(BF16), dh_b)
                s_rep = _nn(_nt(cg, bg).astype(BF16), rep)
                lam_g = lam[:, gs]
                m_g = s_rep * lam_g
                m_b = m_g.astype(BF16)
                ex_g = ex[:, gs]
                dy_g = dyc[:, gs]
                yo = _nn(cg, h0) * ex_g
                dg_b = (dy_g * ex_g).astype(BF16)
                dc = _nt(dg_b, h0)
                el_g = e_l[:, gs]
                dee = jnp.sum(dh * h0.astype(F32), axis=0, keepdims=True) * el_g
                dhs[:, gs] = dh * el_g + _tn(cg, dg_b)
                xd_b = xd[:, gs].astype(BF16)
                dy_b = dy_g.astype(BF16)
                dm_parts, dxd_parts = [], []
                for hp in range(4):
                    ps = slice(hp * LANES, (hp + 1) * LANES)
                    bd = _blockdiag(xd_b[:, ps])
                    dm_parts.append(_nt(dy_b[:, ps], bd))
                    t2 = _tn(m_b[:, ps], dy_b[:, ps])
                    dxd_parts.append(jnp.where(lane128 < HDIM, t2[0:CHUNK], t2[CHUNK:2 * CHUNK]))
                dm = jnp.concatenate(dm_parts, axis=1)
                dxd = jnp.concatenate(dxd_parts, axis=1) + dxf * f[:, gs]
                dseg = dm * m_g
                ds_b = _nt((dm * lam_g).astype(BF16), rep).astype(BF16)
                dc = dc + _nn(ds_b, bg)
                db = db + _tn(ds_b, cg)
                colsum = jnp.sum(dseg, axis=0, keepdims=True)
                dxfxf = dxf * xf_g
                red = dseg - jnp.where(irep_g, colsum, 0.0) + dy_g * yo - dxfxf
                last = jnp.sum(dxfxf, axis=0, keepdims=True) + dee
                red_s[rows, gs] = red
                tail = pl.ds(pl.multiple_of(r0 + CHUNK - 8, 8), 8)
                red_s[tail, gs] += jnp.where(row8 == 7, last, 0.0)
                ddtx_s[rows, gs] = dxd * xs_c[:, gs]
                dxs_s[rows, gs] = dxd * dtx[:, gs] + dsk_ref[:, gs] * dy_g
                dbc_s[rows, g * NSTATE:(g + 1) * NSTATE] = db
                dbc_s[rows, 256 + g * NSTATE:256 + (g + 1) * NSTATE] = dc
            return carry

        lax.fori_loop(0, nc, bwd_chunk, 0, unroll=True)

        dcum = _head_reduce(red_s[...], eht)
        da = _chunk_rcumsum(dcum, pos)
        ddt = _head_reduce(ddtx_s[...], eht) + da * a_neg
        gdt_ref[1:2, :] += jnp.sum(da * dt, axis=0, keepdims=True) * a_neg
        ddt_raw = ddt * _sigmoid(dt_pre)
        lane_t = lax.broadcasted_iota(jnp.int32, (tt, LANES), 1)
        ddt_raw = jnp.where(lane_t < NHEAD, ddt_raw, 0.0)
        gdt_ref[0:1, :] += jnp.sum(ddt_raw, axis=0, keepdims=True)
        dp_ref[:, C_DT:C_DT + LANES] = ddt_raw.astype(BF16)

        sg_x = sgx_s[...]
        sg_bc = sgbc_s[...]
        pre_x = aux_ref[:, A_PX:A_PX + D].astype(F32)
        pre_bc = aux_ref[:, A_PBC:A_PBC + 512].astype(F32)
        dpx = dxs_s[...] * (sg_x * (1.0 + pre_x * (1.0 - sg_x)))
        dpbc = dbc_s[...] * (sg_bc * (1.0 + pre_bc * (1.0 - sg_bc)))
        gvec_ref[2:3, :] += jnp.sum(dpx, axis=0, keepdims=True)
        gcwbc_ref[4:5, :] += jnp.sum(dpbc, axis=0, keepdims=True)
        xraw = xr_ref[...].astype(F32)
        bcraw = bcr_ref[...].astype(F32)
        ups_x = _shifts_up(dpx.astype(BF16), nx_px[...], sm, 3) + [dpx]
        ups_bc = _shifts_up(dpbc.astype(BF16), nx_pbc[...], sm, 3) + [dpbc]
        dxr, dbcr = None, None
        for k in range(4):
            up_x = ups_x[k]
            up_bc = ups_bc[k]
            gcwx_ref[k:k + 1, :] += jnp.sum(xraw * up_x, axis=0, keepdims=True)
            gcwbc_ref[k:k + 1, :] += jnp.sum(bcraw * up_bc, axis=0, keepdims=True)
            tx = cwx_ref[k:k + 1, :] * up_x
            tb = cwbc_ref[k:k + 1, :] * up_bc
            dxr = tx if dxr is None else dxr + tx
            dbcr = tb if dbcr is None else dbcr + tb
        nx_px[...] = dpx[0:8]
        nx_pbc[...] = dpbc[0:8]
        dp_ref[:, C_XS:C_XS + D] = dxr.astype(BF16)
        dp_ref[:, C_BC:C_BC + 512] = dbcr.astype(BF16)

        @pl.when(i == nblk - 1)
        def _():
            gdt_ref[2:3, :] = _head_reduce(gvec_ref[1:2, :] * jnp.ones((8, 1), F32), eht)[0:1, :]

    def const(shape):
        return pl.BlockSpec(shape, lambda i: (0, 0))

    params = [prm[k] for k in ("scw", "cwx", "cwbc", "cbx", "cbbc", "dtb", "alog", "dskx", "nrm", "eh", "eht")]
    in_specs = [main(D, C_UB), main(D, C_UC), main(D, C_UH), main(D, C_Z), main(D, C_XS), main(512, C_BC),
                main(LANES, C_DT),
                pl.BlockSpec((tt, MIX), lambda i: (rev(i), 0)),
                pl.BlockSpec((nc, NSTATE, D), lambda i: (rev(i), 0, 0)),
                pl.BlockSpec((tt, AUX_W), lambda i: (rev(i), 0))]
    in_specs += [_param_spec(a, prm["layer"]) for a in params]
    return pl.pallas_call(
        body, grid=(nblk,),
        in_specs=in_specs,
        out_specs=[pl.BlockSpec((tt, NINP), lambda i: (rev(i), 0)),
                   const((8, D)), const((8, D)), const((8, 512)), const((8, D)), const((8, LANES))],
        out_shape=[SDS((t_len, NINP), BF16), SDS((8, D), F32), SDS((8, D), F32), SDS((8, 512), F32),
                   SDS((8, D), F32), SDS((8, LANES), F32)],
        scratch_shapes=[pltpu.VMEM((NSTATE, D), F32),
                        pltpu.VMEM((tt, D), F32), pltpu.VMEM((tt, 512), F32),
                        pltpu.VMEM((tt, D), F32), pltpu.VMEM((tt, D), F32),
                        pltpu.VMEM((tt, D), F32), pltpu.VMEM((tt, D), F32),
                        pltpu.VMEM((tt, 512), F32),
                        pltpu.VMEM((tt, D), F32), pltpu.VMEM((tt, D), F32),
                        pltpu.VMEM((8, D), F32), pltpu.VMEM((8, D), F32), pltpu.VMEM((8, 512), F32),
                        pltpu.VMEM((tt, D), F32), pltpu.VMEM((tt, 512), F32)],
        compiler_params=_cparams(("arbitrary",)), name="mixer_bwd")(
            *([proj] * 7), dy, states, aux, *params)


TN_IN = 1152
DW_TOKENS = 4096


def layer_fwd_mix(x, lw, prm, tt):
    proj, h1 = norm_matmul(x, lw["nw1"], lw["win"], "in_proj")
    y, st, aux = mixer_fwd(proj, prm, tt)
    return h1, proj, (st, aux), y


def layer_fwd_mlp(x, mixed, lw, between=None):
    h1, proj, st, y = mixed
    x1 = matmul_residual(y, lw["wout"], x, False, "out_proj")
    u, h2 = norm_matmul(x1, lw["nw2"], lw["wup"], "up_proj")
    x2 = matmul_residual(u, lw["wdn"], x1, True, "down_proj", None if between is None else between(u))
    return x2, (x, h1, proj, st, y, x1, h2, u)


def layer_fwd(x, lw, prm, tt):
    return layer_fwd_mlp(x, layer_fwd_mix(x, lw, prm, tt), lw)


def _dw(a, b, a_cols, b_cols, relu2, name, tt_max=2048):
    m_len, n_len = a.shape[1], b.shape[1]
    n_a, n_b = m_len // a_cols, n_len // b_cols
    assert n_a == 1 or n_b == 1
    if n_b == 1:
        return matmul_tn(
            a, b,
            lambda t_: pl.BlockSpec((t_, a_cols), lambda n, t: (t, n)),
            lambda t_: pl.BlockSpec((t_, n_len), lambda n, t: (t, 0)),
            pl.BlockSpec((a_cols, n_len), lambda n, t: (n, 0)), SDS((m_len, n_len), BF16), n_a, relu2, name, tt_max)
    return matmul_tn(
        a, b,
        lambda t_: pl.BlockSpec((t_, m_len), lambda n, t: (t, 0)),
        lambda t_: pl.BlockSpec((t_, b_cols), lambda n, t: (t, n)),
        pl.BlockSpec((m_len, b_cols), lambda n, t: (0, n)), SDS((m_len, n_len), BF16), n_b, relu2, name, tt_max)


def layer_bwd_mlp(dx2, dx2b, lw, saved):
    _, _, _, _, y, x1, h2, u = saved
    du = matmul_nt_act(dx2b, lw["wdn"], u, "mlp_bwd_du")
    g_wdn = _dw(u, dx2b, 1024, D, True, "dw_down")
    dx1, dx1b, g_nw2 = matmul_nt_norm_bwd(du, lw["wup"], x1, lw["nw2"], dx2, "mlp_bwd_dx")
    cb = DFF // N_DEV
    g_wup = matmul_tn(
        h2, du,
        lambda t_: pl.BlockSpec((t_, D), lambda n, t: (t, 0)),
        lambda t_: pl.BlockSpec((t_, 2 * cb), lambda n, t: (t, n)),
        pl.BlockSpec((2, D, cb), lambda n, t: (n, 0, 0)), SDS((N_DEV, D, cb), BF16), N_DEV // 2, False, "dw_up",
        DW_TOKENS)
    dy = matmul_nt_act(dx1b, lw["wout"], None, "out_bwd_dy")
    g_wout = _dw(y, dx1b, 1024, D, False, "dw_out", DW_TOKENS)
    return dx1, dx1b, dy, {"wout": g_wout, "wup": g_wup, "wdn": g_wdn, "nw2": g_nw2[0]}


def layer_bwd_mix(dx1, dy, lw, prm, saved, tt):
    x, h1, proj, st = saved[:4]
    dproj, gscw, gcwx, gcwbc, gvec, gdt = mixer_bwd(proj, dy, st[0], st[1], prm, tt)
    dx0, dx0b, g_nw1 = matmul_nt_norm_bwd(dproj, lw["win"], x, lw["nw1"], dx1, "in_bwd_dx")
    g_win = _dw(h1, dproj, D, TN_IN, False, "dw_in", DW_TOKENS)
    grads = {
        "win": g_win, "scw": gscw[0:3], "cw": jnp.concatenate([gcwx[0:4], gcwbc[0:4]], axis=1),
        "cb": jnp.concatenate([gvec[2], gcwbc[4]], axis=0),
        "dtb": gdt[0, :NHEAD], "alog": gdt[1, :NHEAD], "dsk": gdt[2, :NHEAD],
        "nrm": gvec[0], "nw1": g_nw1[0],
    }
    return dx0, dx0b, grads


def layer_bwd(dx2, dx2b, lw, prm, saved, tt):
    dx1, dx1b, dy, g_mlp = layer_bwd_mlp(dx2, dx2b, lw, saved)
    dx0, dx0b, g_mix = layer_bwd_mix(dx1, dy, lw, prm, saved, tt)
    return dx0, dx0b, {**g_mlp, **g_mix}


def stacked_params(conv_b, dt_bias, a_log, d_skip, ssd_norm_w):
    def lanes128(a):
        return jnp.pad(a, ((0, 0), (0, LANES - a.shape[1])))[:, None, :]

    return {"cbx": conv_b[:, None, :D], "cbbc": conv_b[:, None, D:], "dtb": lanes128(dt_bias),
            "alog": lanes128(a_log), "dskx": jnp.repeat(d_skip, HDIM, axis=1)[:, None, :],
            "nrm": ssd_norm_w[:, None, :]}


def layer_params(layer, win, scw, cw, nw1, nw2, stacked, eh, eht):
    def rows8(a):
        return jnp.pad(a, ((0, 8 - a.shape[0]), (0, 0)))

    lw = {"win": win, "nw1": nw1[None, :], "nw2": nw2[None, :]}
    prm = dict(stacked, layer=layer, scw=rows8(scw), cwx=rows8(cw[:, :D]), cwbc=rows8(cw[:, D:]), eh=eh, eht=eht)
    return lw, prm


def _param_spec(arr, layer):
    if arr.ndim == 3:
        return pl.BlockSpec((None,) + arr.shape[1:], lambda i: (layer, 0, 0))
    return pl.BlockSpec(arr.shape, lambda i: (0, 0))


def _flip(v, bit):
    return 1 - v if bit else v


def all_gather(arrs, name):
    n = len(arrs)

    def body(*refs):
        ins, outs = refs[:n], refs[n:2 * n]
        send_sems, recv_sems, local_sems = refs[2 * n:]
        x, y, c = lax.axis_index("x"), lax.axis_index("y"), lax.axis_index("c")
        sibling = (x, y, 1 - c)
        chips = [(1 - x, y), (x, 1 - y), (1 - x, 1 - y)]

        def idx(px, py, pc):
            return 4 * px + 2 * py + pc

        def copy(a, k, block, to, src=None):
            dst = outs[a].at[idx(*block)]
            return pltpu.make_async_remote_copy(
                src_ref=dst if src is None else src, dst_ref=dst,
                send_sem=send_sems.at[a, k], recv_sem=recv_sems.at[a, k], device_id=to, device_id_type=MESH)

        me = (x, y, c)
        mine = [pltpu.make_async_copy(ins[a], outs[a].at[idx(*me)], local_sems.at[a]) for a in range(n)]
        for cp in mine:
            cp.start()
        first = []
        for a in range(n):
            first.append(copy(a, 0, me, sibling, src=ins[a]))
            first += [copy(a, 1 + j, me, (*chip, c), src=ins[a]) for j, chip in enumerate(chips)]
        for cp in first:
            cp.start()
        passed = []
        for j, chip in enumerate(chips):
            for a in range(n):
                copy(a, 1 + j, (*chip, c), me).wait_recv()
                cp = copy(a, 4 + j, (*chip, c), sibling)
                cp.start()
                passed.append(cp)
        for a in range(n):
            copy(a, 0, sibling, me).wait_recv()
            for j, chip in enumerate(chips):
                copy(a, 4 + j, (*chip, 1 - c), me).wait_recv()
        for cp in first + passed:
            cp.wait_send()
        for cp in mine:
            cp.wait()

    any_spec = pl.BlockSpec(memory_space=pl.ANY)
    return pl.pallas_call(
        body, in_specs=[any_spec] * n, out_specs=[any_spec] * n,
        out_shape=[SDS((N_DEV,) + a.shape, a.dtype) for a in arrs],
        scratch_shapes=[pltpu.SemaphoreType.DMA((n, 7)), pltpu.SemaphoreType.DMA((n, 7)),
                        pltpu.SemaphoreType.DMA((n,))],
        name=name)(*arrs)


HBM_SPEC = pl.BlockSpec(memory_space=pltpu.HBM)
SEM_SPEC = pl.BlockSpec(memory_space=pltpu.SEMAPHORE)
SIDE_EFFECT = pltpu.SideEffectType.DATAFLOW_SIDE_EFFECTING
N_PEER = N_DEV - 1


def _peer(mask):
    x, y, c = lax.axis_index("x"), lax.axis_index("y"), lax.axis_index("c")
    return _flip(x, mask & 4), _flip(y, mask & 2), _flip(c, mask & 1)


ALL_PEERS = tuple(range(1, N_DEV))
SIBLING_AND_CHIPS = (1, 2, 4, 6)


def exchange_start(srcs, per_peer, name, after=None, masks=ALL_PEERS):
    n = len(srcs)
    npeer = len(masks)
    lands = [SDS((N_DEV,) + (a.shape[1:] if per_peer else a.shape), a.dtype) for a in srcs]
    n_in = 2 * n + (after is not None)

    def body(*refs):
        src_refs, land_refs = refs[:n], refs[n:2 * n]
        send_sems, recv_sems = refs[n_in], refs[n_in + 1]
        token = refs[-1]
        x, y, c = lax.axis_index("x"), lax.axis_index("y"), lax.axis_index("c")
        me = 4 * x + 2 * y + c
        for a in range(n):
            for k, mask in enumerate(masks):
                px, py, pc = _peer(mask)
                part = src_refs[a].at[4 * px + 2 * py + pc] if per_peer else src_refs[a]
                pltpu.make_async_remote_copy(
                    src_ref=part, dst_ref=land_refs[a].at[me], send_sem=send_sems.at[a * npeer + k],
                    recv_sem=recv_sems.at[a * npeer + k], device_id=(px, py, pc), device_id_type=MESH).start()
        token[...] = jnp.zeros_like(token)

    out = pl.pallas_call(
        body, name=name,
        out_shape=(pltpu.SemaphoreType.DMA((n * npeer,)), pltpu.SemaphoreType.DMA((n * npeer,)),
                   *[pltpu.HBM(a.shape, a.dtype) for a in srcs], *[pltpu.HBM(l.shape, l.dtype) for l in lands],
                   SDS((8, LANES), F32)),
        in_specs=(HBM_SPEC,) * (2 * n) + ((pl.BlockSpec(memory_space=pl.ANY),) if after is not None else ()),
        out_specs=(SEM_SPEC, SEM_SPEC) + (HBM_SPEC,) * (2 * n) + (pl.BlockSpec(memory_space=pltpu.VMEM),),
        input_output_aliases={k: 2 + k for k in range(2 * n)},
        compiler_params=pltpu.CompilerParams(has_side_effects=SIDE_EFFECT),
    )(*[pltpu.with_memory_space_constraint(a, pltpu.HBM) for a in srcs],
      *[pltpu.with_memory_space_constraint(lax.empty(l.shape, l.dtype), pltpu.HBM) for l in lands],
      *([after] if after is not None else []))
    return out[0], out[1], list(out[2:2 + n]), list(out[2 + n:2 + 2 * n]), out[-1]


def exchange_wait(started, after, per_peer, name, masks=ALL_PEERS):
    send_sems, recv_sems, srcs, lands, _ = started
    n = len(srcs)
    npeer = len(masks)

    def body(*refs):
        src_refs, land_refs = refs[:n], refs[n:2 * n]
        send_sems, recv_sems = refs[2 * n], refs[2 * n + 1]
        for k, mask in enumerate(masks):
            for a in range(n):
                copy = pltpu.make_async_remote_copy(
                    src_ref=src_refs[a].at[0] if per_peer else src_refs[a], dst_ref=land_refs[a].at[0],
                    send_sem=send_sems.at[a * npeer + k], recv_sem=recv_sems.at[a * npeer + k],
                    device_id=_peer(mask), device_id_type=MESH)
                copy.wait_send()
                copy.wait_recv()

    out = pl.pallas_call(
        body, name=name,
        out_shape=tuple(pltpu.HBM(a.shape, a.dtype) for a in srcs + lands),
        in_specs=(HBM_SPEC,) * (2 * n) + (SEM_SPEC, SEM_SPEC, pl.BlockSpec(memory_space=pl.ANY)),
        out_specs=(HBM_SPEC,) * (2 * n), input_output_aliases={k: k for k in range(2 * n)},
        compiler_params=pltpu.CompilerParams(has_side_effects=SIDE_EFFECT),
    )(*srcs, *lands, send_sems, recv_sems, after)
    return list(out[:n]), list(out[n:])


def relay_to_sibling(lands, name):
    n = len(lands)
    chips = (2, 4, 6)

    def body(*refs):
        land_refs = refs[n:2 * n]
        send_sems, recv_sems = refs[2 * n], refs[2 * n + 1]
        x, y, c = lax.axis_index("x"), lax.axis_index("y"), lax.axis_index("c")
        copies = []
        for a in range(n):
            for k, mask in enumerate(chips):
                px, py, _ = _peer(mask)
                block = land_refs[a].at[4 * px + 2 * py + c]
                cp = pltpu.make_async_remote_copy(
                    src_ref=block, dst_ref=block, send_sem=send_sems.at[a * 3 + k], recv_sem=recv_sems.at[a * 3 + k],
                    device_id=(x, y, 1 - c), device_id_type=MESH)
                cp.start()
                copies.append((cp, a, k, land_refs[a].at[4 * px + 2 * py + 1 - c]))
        for cp, a, k, arriving in copies:
            cp.wait_send()
            pltpu.make_async_remote_copy(
                src_ref=arriving, dst_ref=arriving, send_sem=send_sems.at[a * 3 + k], recv_sem=recv_sems.at[a * 3 + k],
                device_id=(x, y, 1 - c), device_id_type=MESH).wait_recv()

    any_spec = pl.BlockSpec(memory_space=pl.ANY)
    return list(pl.pallas_call(
        body, in_specs=[any_spec] * n, out_specs=[any_spec] * n,
        out_shape=[SDS(a.shape, a.dtype) for a in lands],
        input_output_aliases={k: k for k in range(n)},
        scratch_shapes=[pltpu.SemaphoreType.DMA((n * 3,)), pltpu.SemaphoreType.DMA((n * 3,))],
        name=name)(*lands))


def relay_start(lands, name):
    n = len(lands)

    def body(*refs):
        land_refs = refs[:n]
        send_sems, recv_sems = refs[n], refs[n + 1]
        token = refs[-1]
        x, y, c = lax.axis_index("x"), lax.axis_index("y"), lax.axis_index("c")
        for a in range(n):
            for k, mask in enumerate((2, 4, 6)):
                px, py, _ = _peer(mask)
                block = land_refs[a].at[4 * px + 2 * py + c]
                pltpu.make_async_remote_copy(
                    src_ref=block, dst_ref=block, send_sem=send_sems.at[a * 3 + k], recv_sem=recv_sems.at[a * 3 + k],
                    device_id=(x, y, 1 - c), device_id_type=MESH).start()
        token[...] = jnp.zeros_like(token)

    out = pl.pallas_call(
        body, name=name,
        out_shape=(pltpu.SemaphoreType.DMA((n * 3,)), pltpu.SemaphoreType.DMA((n * 3,)),
                   *[pltpu.HBM(a.shape, a.dtype) for a in lands], SDS((8, LANES), F32)),
        in_specs=(HBM_SPEC,) * n,
        out_specs=(SEM_SPEC, SEM_SPEC) + (HBM_SPEC,) * n + (pl.BlockSpec(memory_space=pltpu.VMEM),),
        input_output_aliases={k: 2 + k for k in range(n)},
        compiler_params=pltpu.CompilerParams(has_side_effects=SIDE_EFFECT),
    )(*lands)
    return out[0], out[1], list(out[2:2 + n]), out[-1]


def relay_wait(started, after, name):
    send_sems, recv_sems, lands, _ = started
    n = len(lands)

    def body(*refs):
        land_refs = refs[:n]
        send_sems, recv_sems = refs[n], refs[n + 1]
        x, y, c = lax.axis_index("x"), lax.axis_index("y"), lax.axis_index("c")
        for a in range(n):
            for k in range(3):
                copy = pltpu.make_async_remote_copy(
                    src_ref=land_refs[a].at[0], dst_ref=land_refs[a].at[0], send_sem=send_sems.at[a * 3 + k],
                    recv_sem=recv_sems.at[a * 3 + k], device_id=(x, y, 1 - c), device_id_type=MESH)
                copy.wait_send()
                copy.wait_recv()

    return list(pl.pallas_call(
        body, name=name,
        out_shape=tuple(pltpu.HBM(a.shape, a.dtype) for a in lands),
        in_specs=(HBM_SPEC,) * n + (SEM_SPEC, SEM_SPEC, pl.BlockSpec(memory_space=pl.ANY)),
        out_specs=(HBM_SPEC,) * n, input_output_aliases={k: k for k in range(n)},
        compiler_params=pltpu.CompilerParams(has_side_effects=SIDE_EFFECT),
    )(*lands, send_sems, recv_sems, after))


IN_SHARD = NIN // N_DEV
SLOT_W = 768


def _slot_window(j):
    return (IN_SHARD * j // LANES) * LANES, -(-(IN_SHARD * (j + 1)) // LANES) * LANES


def _placement(j):
    a, b = _slot_window(j)
    r = lax.broadcasted_iota(jnp.int32, (SLOT_W, b - a), 0)
    c = lax.broadcasted_iota(jnp.int32, (SLOT_W, b - a), 1)
    return jnp.where(jnp.logical_and(c == r + (IN_SHARD * j - a), r < IN_SHARD), 1.0, 0.0).astype(BF16)


def assemble_w_in(land):
    tm = 256

    def body(l_ref, o_ref, acc):
        acc[...] = jnp.zeros_like(acc)
        for j in range(N_DEV):
            a, b = _slot_window(j)
            acc[:, a:b] += _nn(l_ref[j], _placement(j))
        o_ref[...] = acc[...].astype(BF16)

    return pl.pallas_call(
        body, grid=(D // tm,),
        in_specs=[pl.BlockSpec((N_DEV, tm, SLOT_W), lambda i: (0, i, 0))],
        out_specs=pl.BlockSpec((tm, NINP), lambda i: (i, 0)),
        out_shape=SDS((D, NINP), BF16),
        scratch_shapes=[pltpu.VMEM((tm, NINP), F32)],
        compiler_params=_cparams(("parallel",)), name="assemble_w_in")(land)


def scatter_w_in(dw):
    tm = 256

    def body(d_ref, o_ref):
        for j in range(N_DEV):
            a, b = _slot_window(j)
            o_ref[j] = _nt(d_ref[:, a:b], _placement(j)).astype(BF16)

    return pl.pallas_call(
        body, grid=(D // tm,),
        in_specs=[pl.BlockSpec((tm, NINP), lambda i: (i, 0))],
        out_specs=pl.BlockSpec((N_DEV, tm, SLOT_W), lambda i: (0, i, 0)),
        out_shape=SDS((N_DEV, D, SLOT_W), BF16),
        compiler_params=_cparams(("parallel",)), name="scatter_w_in")(dw)


def _adamw_math(g, w_ref, m_ref, v_ref, g_ref, d_ref, nm_ref, nv_ref):
    mn = ADAM_B1 * m_ref[...] + (1.0 - ADAM_B1) * g
    vn = ADAM_B2 * v_ref[...] + (1.0 - ADAM_B2) * jnp.square(g)
    m_hat = mn / (1.0 - ADAM_B1 ** ADAM_STEP)
    v_hat = vn / (1.0 - ADAM_B2 ** ADAM_STEP)
    g_ref[...] = g
    d_ref[...] = -ADAM_LR * (m_hat / (jnp.sqrt(v_hat) + ADAM_EPS) + ADAM_WD * w_ref[...])
    nm_ref[...] = mn
    nv_ref[...] = vn


def adamw_layers(w, slots, m, v, name):
    depth, r_len, c_len = w.shape
    cs = slots[0].shape[2]
    br = min(128, r_len)
    assert r_len % br == 0

    def body(w_ref, *rest):
        s_refs, (m_ref, v_ref, g_ref, d_ref, nm_ref, nv_ref) = rest[:depth], rest[depth:]
        layer = pl.program_id(0)
        for k in range(depth):
            @pl.when(layer == k)
            def _(k=k):
                g = s_refs[k][0, :, 0:c_len].astype(F32)
                for j in range(1, N_DEV):
                    g = g + s_refs[k][j, :, 0:c_len].astype(F32)
                _adamw_math(g, w_ref, m_ref, v_ref, g_ref, d_ref, nm_ref, nv_ref)

    spec = pl.BlockSpec((None, br, c_len), lambda l, i: (l, i, 0))
    s_specs = [pl.BlockSpec((N_DEV, br, cs), lambda l, i, k=k: (0, jnp.where(l == k, i, 0), 0))
               for k in range(depth)]
    return pl.pallas_call(
        body, grid=(depth, r_len // br),
        in_specs=[spec] + s_specs + [spec, spec],
        out_specs=[spec] * 4, out_shape=[SDS(w.shape, F32)] * 4,
        compiler_params=_cparams(("arbitrary", "arbitrary")), name=name)(w, *slots, m, v)


def adamw(w, slots, m, v, name):
    r_len, c_len = w.shape
    br = r_len if r_len <= 512 else 512
    assert r_len % br == 0

    def body(w_ref, s_ref, m_ref, v_ref, g_ref, d_ref, nm_ref, nv_ref):
        g = s_ref[0].astype(F32)
        for k in range(1, N_DEV):
            g = g + s_ref[k].astype(F32)
        _adamw_math(g, w_ref, m_ref, v_ref, g_ref, d_ref, nm_ref, nv_ref)

    spec = pl.BlockSpec((br, c_len), lambda i: (i, 0))
    return pl.pallas_call(
        body, grid=(r_len // br,),
        in_specs=[spec, pl.BlockSpec((N_DEV, br, c_len), lambda i: (0, i, 0)), spec, spec],
        out_specs=[spec] * 4, out_shape=[SDS((r_len, c_len), F32)] * 4,
        compiler_params=_cparams(("parallel",)), name=name)(w, slots, m, v)


def _adamw_nd(w, slots, m, v, name):
    shp = w.shape
    r = int(np.prod(shp[:-1]))
    outs = adamw(w.reshape(r, shp[-1]), slots.reshape(N_DEV, r, shp[-1]), m.reshape(r, shp[-1]),
                 v.reshape(r, shp[-1]), name)
    return [o.reshape(shp) for o in outs]


SMALL = [("norm_mix_w", DEPTH * D), ("ssd_conv_b", DEPTH * XBC), ("dt_bias", DEPTH * NHEAD),
         ("a_log", DEPTH * NHEAD), ("d_skip", DEPTH * NHEAD), ("ssd_norm_w", DEPTH * D),
         ("norm_mlp_w", DEPTH * D), ("final_norm_w", D)]
SMALL_LEN = sum(s for _, s in SMALL)
SMALL_ROWS = -(-SMALL_LEN // LANES)


def _pack_small(parts):
    flat = jnp.concatenate([parts[k].reshape(-1) for k, _ in SMALL])
    return jnp.pad(flat, (0, SMALL_ROWS * LANES - SMALL_LEN)).reshape(SMALL_ROWS, LANES)


def _unpack_small(packed, shapes):
    flat = packed.reshape(-1)
    out, off = {}, 0
    for k, s in SMALL:
        out[k] = flat[off:off + s].reshape(shapes[k])
        off += s
    return out


def kernel(x, norm_mix_w, w_in, short_conv_w, ssd_conv_w, ssd_conv_b, dt_bias, a_log, d_skip, ssd_norm_w, w_out, norm_mlp_w, w_up, w_down, final_norm_w, loss_target, m_norm_mix_w, m_w_in, m_short_conv_w, m_ssd_conv_w, m_ssd_conv_b, m_dt_bias, m_a_log, m_d_skip, m_ssd_norm_w, m_w_out, m_norm_mlp_w, m_w_up, m_w_down, m_final_norm_w, v_norm_mix_w, v_w_in, v_short_conv_w, v_ssd_conv_w, v_ssd_conv_b, v_dt_bias, v_a_log, v_d_skip, v_ssd_norm_w, v_w_out, v_norm_mlp_w, v_w_up, v_w_down, v_final_norm_w):
    xs = x[0]
    t_len = xs.shape[0]
    tt = min(256, t_len)
    eh, eht = _head_matrices()
    stacked = stacked_params(ssd_conv_b, dt_bias, a_log, d_skip, ssd_norm_w)
    me = 4 * lax.axis_index("x") + 2 * lax.axis_index("y") + lax.axis_index("c")

    def start_first(i, after):
        return exchange_start(
            [jnp.pad(w_in[i].astype(BF16), ((0, 0), (0, SLOT_W - IN_SHARD))), short_conv_w[i], ssd_conv_w[i]],
            False, "w_in_start_%d" % i, after, SIBLING_AND_CHIPS)

    def start_rest(i, after):
        return exchange_start([w_out[i].astype(BF16), w_up[i].astype(BF16), w_down[i].astype(BF16)], False,
                              "w_rest_start_%d" % i, after, SIBLING_AND_CHIPS)

    def fill_own(srcs, lands, per_peer):
        own = [lax.dynamic_index_in_dim(s_, me, 0, keepdims=False) for s_ in srcs] if per_peer else srcs
        return [lax.dynamic_update_index_in_dim(l_, o_, me, 0) for l_, o_ in zip(lands, own)]

    def finish_weights(started, after, name):
        srcs, lands = exchange_wait(started, after, False, name + "_wait", SIBLING_AND_CHIPS)
        return fill_own(srcs, relay_to_sibling(lands, name + "_relay"), False)

    act = xs
    saved, layers = [], []
    first = start_first(0, None)
    rest = start_rest(0, first[4])
    token = first[4][0, 0] + rest[4][0, 0]
    pending = []
    for i in range(DEPTH):
        if i == 0:
            g_in, g_sc, g_cw = finish_weights(first, act, "w_in_0")
        else:
            srcs_f, relay_f = pending.pop()
            g_in, g_sc, g_cw = fill_own(srcs_f, relay_wait(relay_f, act, "w_in_%d_relay_wait" % i), False)
        if i + 1 < DEPTH:
            first = start_first(i + 1, g_in)
            token = token + first[4][0, 0]
        lw, prm = layer_params(
            i, assemble_w_in(g_in), g_sc.transpose(1, 0, 2).reshape(3, D), g_cw.transpose(1, 0, 2).reshape(4, XBC),
            norm_mix_w[i], norm_mlp_w[i], stacked, eh, eht)
        lw["nw1"] = lw["nw1"] + token
        proj, h1 = norm_matmul(act, lw["nw1"], lw["win"], "in_proj")
        srcs_r, lands_r = exchange_wait(rest, proj, False, "w_rest_%d_wait" % i, SIBLING_AND_CHIPS)
        relay = relay_start(lands_r, "w_rest_%d_relay_start" % i)
        y, st, aux = mixer_fwd(proj, dict(prm, nrm=prm["nrm"] + relay[3][0, 0]), tt)
        mixed = (h1, proj, (st, aux), y)
        g_out, g_up, g_dn = fill_own(srcs_r, relay_wait(relay, y, "w_rest_%d_relay_wait" % i), False)
        lw.update(wout=g_out.reshape(MIX, D), wup=g_up, wdn=g_dn.reshape(DFF, D))
        between = None
        if i + 1 < DEPTH:
            rest = start_rest(i + 1, g_dn)
            token = rest[4][0, 0]
            lw["nw2"] = lw["nw2"] + token

            def between(u, first=first, nxt=i + 1):
                srcs_n, lands_n = exchange_wait(first, u, False, "w_in_%d_wait" % nxt, SIBLING_AND_CHIPS)
                started = relay_start(lands_n, "w_in_%d_relay_start" % nxt)
                pending.append((srcs_n, started))
                return started[3]
        layers.append((lw, prm))
        act, sv = layer_fwd_mlp(act, mixed, lw, between)
        saved.append(sv)
    loss_acc, dx, dxb, g_fw = loss_head(act, final_norm_w[None, :], loss_target[0])

    grads = [None] * DEPTH
    sent_rest, sent_in = [None] * DEPTH, [None] * DEPTH
    token = None
    for i in reversed(range(DEPTH)):
        lw, prm = layers[i]
        if token is not None:
            lw = dict(lw, nw2=lw["nw2"] + token)
        dx1, _, dy, g_mlp = layer_bwd_mlp(dx, dxb, lw, saved[i])
        sent_rest[i] = exchange_start(
            [g_mlp["wout"].reshape(N_DEV, MIX // N_DEV, D), g_mlp["wup"], g_mlp["wdn"].reshape(N_DEV, DFF // N_DEV, D)],
            True, "g_rest_start_%d" % i)
        dx, dxb, g_mix = layer_bwd_mix(dx1, dy, lw, dict(prm, nrm=prm["nrm"] + sent_rest[i][4][0, 0]), saved[i], tt)
        grads[i] = {**g_mlp, **g_mix}
        if i > 0:
            sent_in[i] = exchange_start([scatter_w_in(g_mix["win"])], True, "g_in_start_%d" % i)
            token = sent_in[i][4][0, 0]

    def stack(k):
        return jnp.stack([g[k] for g in grads])

    small = _pack_small({"norm_mix_w": stack("nw1"), "ssd_conv_b": stack("cb"), "dt_bias": stack("dtb"),
                         "a_log": stack("alog"), "d_skip": stack("dsk"), "ssd_norm_w": stack("nrm"),
                         "norm_mlp_w": stack("nw2"), "final_norm_w": g_fw[0]})
    r_small, r_sc, r_cw = all_gather([small, stack("scw"), stack("cw")], "gather_small_grads")
    r_sc = lax.dynamic_slice_in_dim(r_sc, me * (D // N_DEV), D // N_DEV, axis=3)
    r_cw = lax.dynamic_slice_in_dim(r_cw, me * (XBC // N_DEV), XBC // N_DEV, axis=3)
    sent_in[0] = exchange_start([scatter_w_in(grads[0]["win"])], True, "g_in_start_0", after=r_small)

    after = sent_in[0][4]
    recv = [fill_own(*exchange_wait(sent_rest[i], after, True, "g_rest_wait_%d" % i), True) for i in range(DEPTH)]
    res = {}
    res["w_out"] = adamw_layers(w_out, [r[0] for r in recv], m_w_out, v_w_out, "adamw_w_out")
    res["w_up"] = adamw_layers(w_up, [r[1] for r in recv], m_w_up, v_w_up, "adamw_w_up")
    res["w_down"] = adamw_layers(w_down, [r[2] for r in recv], m_w_down, v_w_down, "adamw_w_down")
    after = res["w_down"][1]
    recv_in = [fill_own(*exchange_wait(sent_in[i], after, True, "g_in_wait_%d" % i), True)[0] for i in range(DEPTH)]
    res["w_in"] = adamw_layers(w_in, recv_in, m_w_in, v_w_in, "adamw_w_in")
    res["short_conv_w"] = _adamw_nd(short_conv_w, r_sc, m_short_conv_w, v_short_conv_w, "adamw_short_conv")
    res["ssd_conv_w"] = _adamw_nd(ssd_conv_w, r_cw, m_ssd_conv_w, v_ssd_conv_w, "adamw_ssd_conv")
    small_w = {"norm_mix_w": norm_mix_w, "ssd_conv_b": ssd_conv_b, "dt_bias": dt_bias, "a_log": a_log,
               "d_skip": d_skip, "ssd_norm_w": ssd_norm_w, "norm_mlp_w": norm_mlp_w, "final_norm_w": final_norm_w}
    small_m = {"norm_mix_w": m_norm_mix_w, "ssd_conv_b": m_ssd_conv_b, "dt_bias": m_dt_bias, "a_log": m_a_log,
               "d_skip": m_d_skip, "ssd_norm_w": m_ssd_norm_w, "norm_mlp_w": m_norm_mlp_w,
               "final_norm_w": m_final_norm_w}
    small_v = {"norm_mix_w": v_norm_mix_w, "ssd_conv_b": v_ssd_conv_b, "dt_bias": v_dt_bias, "a_log": v_a_log,
               "d_skip": v_d_skip, "ssd_norm_w": v_ssd_norm_w, "norm_mlp_w": v_norm_mlp_w,
               "final_norm_w": v_final_norm_w}
    shapes = {k: a.shape for k, a in small_w.items()}
    packed = adamw(_pack_small(small_w), r_small, _pack_small(small_m), _pack_small(small_v), "adamw_small")
    unpacked = [_unpack_small(p, shapes) for p in packed]
    for k in small_w:
        res[k] = [u[k] for u in unpacked]

    loss = lax.psum(loss_acc[0, 0], ("x", "y", "c"))
    order = ["norm_mix_w", "w_in", "short_conv_w", "ssd_conv_w", "ssd_conv_b", "dt_bias", "a_log", "d_skip",
             "ssd_norm_w", "w_out", "norm_mlp_w", "w_up", "w_down", "final_norm_w"]
    out = [loss, dx[None]]
    for part in range(4):
        out += [res[k][part] for k in order]
    return tuple(out)
```

```python
import functools

import numpy as np
import jax
import jax.numpy as jnp
from jax import lax
from jax.experimental import pallas as pl
from jax.experimental.pallas import tpu as pltpu

F32 = jnp.float32
BF16 = jnp.bfloat16
SDS = jax.ShapeDtypeStruct

N_DEV = 8
DEPTH = 4
D = 1024
NIN = 5648
NINP = 5760
DFF = 4096
MIX = 2048
NHEAD = 16
HDIM = 64
NSTATE = 128
CHUNK = 64
XBC = 1536
EPS = 1e-5
LANES = 128
NEG_BIG = -1e30

C_UB, C_UC, C_UH, C_Z, C_XS, C_BC, C_DT = 0, 1024, 2048, 3072, 4096, 5120, 5632
A_CV, A_YS, A_PX, A_PBC, AUX_W = 0, 1024, 2048, 3072, 3584

ADAM_LR = 0.001
ADAM_B1 = 0.9
ADAM_B2 = 0.999
ADAM_EPS = 1e-08
ADAM_WD = 0.01
ADAM_STEP = 10

VMEM_LIMIT = 56 * 1024 * 1024
MESH = pl.DeviceIdType.MESH


def _cparams(sem):
    return pltpu.CompilerParams(dimension_semantics=sem, vmem_limit_bytes=VMEM_LIMIT)


def _nt(a, b):
    return lax.dot_general(a, b, (((1,), (1,)), ((), ())), preferred_element_type=F32)


def _tn(a, b):
    return lax.dot_general(a, b, (((0,), (0,)), ((), ())), preferred_element_type=F32)


def _nn(a, b):
    return jnp.dot(a, b, preferred_element_type=F32)


def _sigmoid(v):
    return 0.5 * jnp.tanh(0.5 * v) + 0.5


def _split3(v):
    v1 = v.astype(BF16)
    r1 = v - v1.astype(F32)
    v2 = r1.astype(BF16)
    v3 = (r1 - v2.astype(F32)).astype(BF16)
    return v1, v2, v3


def _expand(v, eh):
    v1, v2, v3 = _split3(v)
    return _nn(v1, eh) + _nn(v2, eh) + _nn(v3, eh)


def _head_reduce(v, eht):
    v1 = v.astype(BF16)
    v2 = (v - v1.astype(F32)).astype(BF16)
    return _nn(v1, eht) + _nn(v2, eht)


def _head_matrices():
    eh = np.zeros((LANES, D), np.float32)
    for h in range(NHEAD):
        eh[h, h * HDIM:(h + 1) * HDIM] = 1.0
    return jnp.asarray(eh, BF16), jnp.asarray(eh.T.copy(), BF16)


def _resident(shape):
    return pl.BlockSpec(shape, lambda *_: (0,) * len(shape), pipeline_mode=pl.Buffered(1))


def _col_chunks(n, step):
    return [(c, min(c + step, n)) for c in range(0, n, step)]


def norm_matmul(x, nw, w, name):
    t_len = x.shape[0]
    blocked = w.ndim == 3
    n_len = w.shape[0] * w.shape[2] if blocked else w.shape[1]
    tm = min(512, t_len)
    chunks = _col_chunks(n_len, n_len // N_DEV if blocked else 1536)

    def body(x_ref, nw_ref, w_ref, o_ref, h_ref):
        xv = x_ref[...]
        r = lax.rsqrt(jnp.mean(xv * xv, axis=-1, keepdims=True) + EPS)
        hv = (xv * r * nw_ref[...]).astype(BF16)
        h_ref[...] = hv
        for j, (c0, c1) in enumerate(chunks):
            wj = w_ref[j] if blocked else w_ref[:, c0:c1]
            o_ref[:, c0:c1] = _nn(hv, wj).astype(o_ref.dtype)

    return pl.pallas_call(
        body, grid=(t_len // tm,),
        in_specs=[pl.BlockSpec((tm, D), lambda i: (i, 0)), _resident((1, D)), _resident(w.shape)],
        out_specs=[pl.BlockSpec((tm, n_len), lambda i: (i, 0)),
                   pl.BlockSpec((tm, D), lambda i: (i, 0))],
        out_shape=[SDS((t_len, n_len), BF16), SDS((t_len, D), BF16)],
        compiler_params=_cparams(("parallel",)), name=name)(x, nw, w)


def matmul_residual(a, w, res, relu2, name, after=None):
    t_len, k_len = a.shape
    tm = min(512, t_len)

    def body(a_ref, w_ref, res_ref, *rest):
        o_ref = rest[-1]
        av = a_ref[...]
        if relu2:
            af = jnp.maximum(av.astype(F32), 0.0)
            av = (af * af).astype(BF16)
        o_ref[...] = res_ref[...] + _nn(av, w_ref[...])

    extra = [] if after is None else [after]
    return pl.pallas_call(
        body, grid=(t_len // tm,),
        in_specs=[pl.BlockSpec((tm, k_len), lambda i: (i, 0)),
                  _resident((k_len, D)),
                  pl.BlockSpec((tm, D), lambda i: (i, 0))] + [pl.BlockSpec(memory_space=pl.ANY)] * len(extra),
        out_specs=pl.BlockSpec((tm, D), lambda i: (i, 0)),
        out_shape=SDS((t_len, D), F32),
        compiler_params=_cparams(("parallel",)), name=name)(a, w, res, *extra)


def matmul_nt_act(dy, w, u, name):
    t_len = dy.shape[0]
    n_len = w.shape[0]
    tm = min(512, t_len)
    chunks = _col_chunks(n_len, 1024)

    def body(dy_ref, w_ref, *rest):
        if u is None:
            (o_ref,) = rest
        else:
            u_ref, o_ref = rest
        dyv = dy_ref[...]
        for c0, c1 in chunks:
            p = _nt(dyv, w_ref[c0:c1, :])
            if u is not None:
                p = p * (2.0 * jnp.maximum(u_ref[:, c0:c1].astype(F32), 0.0))
            o_ref[:, c0:c1] = p.astype(o_ref.dtype)

    in_specs = [pl.BlockSpec((tm, D), lambda i: (i, 0)), _resident((n_len, D))]
    args = [dy, w]
    if u is not None:
        in_specs.append(pl.BlockSpec((tm, n_len), lambda i: (i, 0)))
        args.append(u)
    return pl.pallas_call(
        body, grid=(t_len // tm,),
        in_specs=in_specs,
        out_specs=pl.BlockSpec((tm, n_len), lambda i: (i, 0)),
        out_shape=SDS((t_len, n_len), BF16),
        compiler_params=_cparams(("parallel",)), name=name)(*args)


def matmul_tn(a, b, a_spec, b_spec, o_spec, o_shape, n_out, relu2, name, tt_max=2048):
    t_len = a.shape[0]
    tt = min(tt_max, t_len)
    nt = t_len // tt

    def body(a_ref, b_ref, o_ref, acc):
        t = pl.program_id(1)
        av = a_ref[...]
        if relu2:
            af = jnp.maximum(av.astype(F32), 0.0)
            av = (af * af).astype(BF16)
        p = _tn(av, b_ref[...])

        @pl.when(t == 0)
        def _():
            acc[...] = p

        @pl.when(t > 0)
        def _():
            acc[...] += p

        @pl.when(t == nt - 1)
        def _():
            if len(blk) == 3:
                for j in range(blk[0]):
                    o_ref[j] = acc[:, j * blk[2]:(j + 1) * blk[2]].astype(o_ref.dtype)
            else:
                o_ref[...] = acc[...].astype(o_ref.dtype)

    blk = tuple(o_spec.block_shape)
    acc_shape = (blk[1], blk[0] * blk[2]) if len(blk) == 3 else blk
    return pl.pallas_call(
        body, grid=(n_out, nt),
        in_specs=[a_spec(tt), b_spec(tt)],
        out_specs=o_spec, out_shape=o_shape,
        scratch_shapes=[pltpu.VMEM(acc_shape, F32)],
        compiler_params=_cparams(("parallel", "arbitrary")), name=name)(a, b)


def matmul_nt_norm_bwd(dy, w, x, nw, dres, name):
    t_len = x.shape[0]
    blocked = w.ndim == 3
    k_len = dy.shape[1]
    kb = k_len // N_DEV
    tm = min(512, t_len)

    def body(dy_ref, w_ref, x_ref, nw_ref, dres_ref, dx_ref, dxb_ref, dnw_ref):
        @pl.when(pl.program_id(0) == 0)
        def _():
            dnw_ref[...] = jnp.zeros_like(dnw_ref)

        if blocked:
            dh = _nt(dy_ref[:, 0:kb], w_ref[0])
            for j in range(1, N_DEV):
                dh = dh + _nt(dy_ref[:, j * kb:(j + 1) * kb], w_ref[j])
        else:
            dh = _nt(dy_ref[...], w_ref[...])
        xv = x_ref[...]
        r = lax.rsqrt(jnp.mean(xv * xv, axis=-1, keepdims=True) + EPS)
        xh = xv * r
        dnw_ref[0:1, :] += jnp.sum(dh * xh, axis=0, keepdims=True)
        g = dh * nw_ref[...]
        dx = dres_ref[...] + r * (g - xh * jnp.mean(g * xh, axis=-1, keepdims=True))
        dx_ref[...] = dx
        dxb_ref[...] = dx.astype(BF16)

    return pl.pallas_call(
        body, grid=(t_len // tm,),
        in_specs=[pl.BlockSpec((tm, k_len), lambda i: (i, 0)),
                  _resident(w.shape),
                  pl.BlockSpec((tm, D), lambda i: (i, 0)),
                  _resident((1, D)),
                  pl.BlockSpec((tm, D), lambda i: (i, 0))],
        out_specs=[pl.BlockSpec((tm, D), lambda i: (i, 0)),
                   pl.BlockSpec((tm, D), lambda i: (i, 0)),
                   pl.BlockSpec((8, D), lambda i: (0, 0))],
        out_shape=[SDS((t_len, D), F32), SDS((t_len, D), BF16), SDS((8, D), F32)],
        compiler_params=_cparams(("arbitrary",)), name=name)(dy, w, x, nw, dres)


def loss_head(x, fw, tgt):
    t_len = x.shape[0]
    tm = min(512, t_len)

    def body(x_ref, fw_ref, t_ref, loss_ref, dx_ref, dxb_ref, dfw_ref):
        @pl.when(pl.program_id(0) == 0)
        def _():
            loss_ref[...] = jnp.zeros_like(loss_ref)
            dfw_ref[...] = jnp.zeros_like(dfw_ref)
        xv = x_ref[...]
        r = lax.rsqrt(jnp.mean(xv * xv, axis=-1, keepdims=True) + EPS)
        xh = xv * r
        w = fw_ref[...]
        e = xh * w - t_ref[...]
        row = jnp.sum(e * e, axis=-1, keepdims=True) * (1.0 / D)
        loss_ref[...] += 0.5 * jnp.sum(row, axis=0, keepdims=True)
        dyf = e * (1.0 / D)
        dfw_ref[0:1, :] += jnp.sum(dyf * xh, axis=0, keepdims=True)
        g = dyf * w
        dx = r * (g - xh * jnp.mean(g * xh, axis=-1, keepdims=True))
        dx_ref[...] = dx
        dxb_ref[...] = dx.astype(BF16)

    return pl.pallas_call(
        body, grid=(t_len // tm,),
        in_specs=[pl.BlockSpec((tm, D), lambda i: (i, 0)),
                  pl.BlockSpec((1, D), lambda i: (0, 0)),
                  pl.BlockSpec((tm, D), lambda i: (i, 0))],
        out_specs=[pl.BlockSpec((8, LANES), lambda i: (0, 0)),
                   pl.BlockSpec((tm, D), lambda i: (i, 0)),
                   pl.BlockSpec((tm, D), lambda i: (i, 0)),
                   pl.BlockSpec((8, D), lambda i: (0, 0))],
        out_shape=[SDS((8, LANES), F32), SDS((t_len, D), F32), SDS((t_len, D), BF16), SDS((8, D), F32)],
        compiler_params=_cparams(("arbitrary",)), name="loss_head")(x, fw, tgt)


TAP_SHIFTS = (3, 2, 1)


def _shift_matrix(n, up):
    r = lax.broadcasted_iota(jnp.int32, (n, n), 0)
    c = lax.broadcasted_iota(jnp.int32, (n, n), 1)
    return jnp.concatenate([jnp.where(c == (r + j if up else r - j), 1.0, 0.0).astype(BF16) for j in TAP_SHIFTS],
                           axis=0)


def _shifts_dn(xb, halo, sm, n_shifts):
    n = xb.shape[0]
    first = len(TAP_SHIFTS) - n_shifts
    moved = _nn(sm[first * n:], xb)
    row = lax.broadcasted_iota(jnp.int32, halo.shape, 0)
    outs = []
    for k in range(n_shifts):
        j = TAP_SHIFTS[first + k]
        o = moved[k * n:(k + 1) * n]
        top = jnp.where(row < j, pltpu.roll(halo, j, 0), o[0:8])
        outs.append(jnp.concatenate([top, o[8:]], axis=0))
    return outs


def _shifts_up(xb, nxt, sm, n_shifts):
    n = xb.shape[0]
    first = len(TAP_SHIFTS) - n_shifts
    moved = _nn(sm[first * n:], xb)
    row = lax.broadcasted_iota(jnp.int32, nxt.shape, 0)
    outs = []
    for k in range(n_shifts):
        j = TAP_SHIFTS[first + k]
        o = moved[k * n:(k + 1) * n]
        bot = jnp.where(row >= 8 - j, pltpu.roll(nxt, 8 - j, 0), o[n - 8:n])
        outs.append(jnp.concatenate([o[:n - 8], bot], axis=0))
    return outs


def _conv_fwd(x, xb, halo, w_ref, kw, sm):
    shifted = _shifts_dn(xb, halo, sm, kw - 1)
    acc = w_ref[kw - 1:kw, :] * x
    for k in range(kw - 1):
        acc = acc + w_ref[k:k + 1, :] * shifted[k]
    return acc


def _chunk_cumsum(a, pos):
    for sh in (1, 2, 4, 8, 16, 32):
        a = a + jnp.where(pos >= sh, pltpu.roll(a, sh, 0), 0.0)
    return a


def _chunk_rcumsum(a, pos):
    n = a.shape[0]
    for sh in (1, 2, 4, 8, 16, 32):
        a = a + jnp.where(pos < CHUNK - sh, pltpu.roll(a, n - sh, 0), 0.0)
    return a


def _softplus(v):
    return jnp.maximum(v, 0.0) + jnp.log(1.0 + jnp.exp(-jnp.abs(v)))


def _silu(v):
    return v * _sigmoid(v)


def _dsilu(v):
    s = _sigmoid(v)
    return s * (1.0 + v * (1.0 - s))


def _lane_masks(width=D):
    lane = lax.broadcasted_iota(jnp.int32, (CHUNK, width), 1) & (HDIM - 1)
    row = lax.broadcasted_iota(jnp.int32, (CHUNK, width), 0)
    return lane == row, lane <= row


def _rep_matrix():
    lane = lax.broadcasted_iota(jnp.int32, (CHUNK, 512), 1) & (HDIM - 1)
    row = lax.broadcasted_iota(jnp.int32, (CHUNK, 512), 0)
    return jnp.where(lane == row, 1.0, 0.0).astype(BF16)


def _blockdiag(xp):
    lane = lax.broadcasted_iota(jnp.int32, xp.shape, 1)
    zero = jnp.zeros_like(xp)
    return jnp.concatenate([jnp.where(lane < HDIM, xp, zero), jnp.where(lane >= HDIM, xp, zero)], axis=0)


def _mixer_views(tt):
    r8 = tt // 8

    def main(width, col):
        return pl.BlockSpec((tt, width), lambda i, c=col // width: (i, c))

    def halo(width, col):
        return pl.BlockSpec((8, width), lambda i, c=col // width: (jnp.maximum(i * r8 - 1, 0), c))

    return main, halo


def mixer_fwd(proj, prm, tt):
    t_len = proj.shape[0]
    nblk = t_len // tt
    nc = tt // CHUNK
    main, halo = _mixer_views(tt)

    def body(ub_ref, uc_ref, uh_ref, z_ref, xr_ref, bcr_ref, dtr_ref, uch_ref, uhh_ref, xrh_ref, bcrh_ref,
             scw_ref, cwx_ref, cwbc_ref, cbx_ref, cbbc_ref, dtb_ref, alog_ref, dsk_ref, nrm_ref, eh_ref,
             y_ref, st_ref, aux_ref, hs, xs_s, bc_s, dtx_s, cumx_s, yssd_s):
        i = pl.program_id(0)
        first = i == 0

        @pl.when(first)
        def _():
            hs[...] = jnp.zeros_like(hs)

        keep = jnp.where(first, 0.0, 1.0)
        sm = _shift_matrix(tt, False)
        v = uc_ref[...].astype(F32) * uh_ref[...].astype(F32)
        vh = uch_ref[...].astype(F32) * uhh_ref[...].astype(F32) * keep
        cv = _conv_fwd(v, v.astype(BF16), vh, scw_ref, 3, sm)
        aux_ref[:, A_CV:A_CV + D] = cv.astype(BF16)
        y_ref[:, 0:D] = (ub_ref[...].astype(F32) * cv).astype(BF16)

        xrb = xr_ref[...]
        pre_x = _conv_fwd(xrb.astype(F32), xrb, xrh_ref[...].astype(F32) * keep, cwx_ref, 4, sm) + cbx_ref[...]
        aux_ref[:, A_PX:A_PX + D] = pre_x.astype(BF16)
        xs_s[...] = _silu(pre_x)
        bcrb = bcr_ref[...]
        pre_bc = _conv_fwd(bcrb.astype(F32), bcrb, bcrh_ref[...].astype(F32) * keep, cwbc_ref, 4, sm) + cbbc_ref[...]
        aux_ref[:, A_PBC:A_PBC + 512] = pre_bc.astype(BF16)
        bc_s[...] = _silu(pre_bc)
        dt = _softplus(dtr_ref[...].astype(F32) + dtb_ref[...])
        a_neg = -jnp.exp(alog_ref[...])
        pos = lax.broadcasted_iota(jnp.int32, (tt, LANES), 0) & (CHUNK - 1)
        cum = _chunk_cumsum(dt * a_neg, pos)
        eh = eh_ref[...]
        dtx_s[...] = _expand(dt, eh)
        cumx_s[...] = _expand(cum, eh)
        irep, causal = _lane_masks()
        rep = _rep_matrix()

        def chunk(c, carry):
            r0 = pl.multiple_of(c * CHUNK, CHUNK)
            rows = pl.ds(r0, CHUNK)
            cumx = cumx_s[rows, :]
            cum_l = cumx[CHUNK - 1:CHUNK, :]
            xd = xs_s[rows, :] * dtx_s[rows, :]
            xf = xd * jnp.exp(cum_l - cumx)
            ex = jnp.exp(cumx)
            e_l = jnp.exp(cum_l)
            rvec = jnp.sum(jnp.where(irep, cumx, 0.0), axis=0, keepdims=True)
            lam = jnp.exp(jnp.where(causal, cumx - rvec, NEG_BIG))
            bc = bc_s[rows, :]
            for g in range(2):
                gs = slice(g * 512, (g + 1) * 512)
                bg = bc[:, g * NSTATE:(g + 1) * NSTATE].astype(BF16)
                cg = bc[:, 256 + g * NSTATE:256 + (g + 1) * NSTATE].astype(BF16)
                s_rep = _nn(_nt(cg, bg).astype(BF16), rep)
                m_g = (s_rep * lam[:, gs]).astype(BF16)
                h_g = hs[:, gs]
                h_b = h_g.astype(BF16)
                st_ref[c, :, gs] = h_b
                yo = _nn(cg, h_b) * ex[:, gs]
                xd_b = xd[:, gs].astype(BF16)
                for hp in range(4):
                    ps = slice(hp * LANES, (hp + 1) * LANES)
                    yd = _nn(m_g[:, ps], _blockdiag(xd_b[:, ps]))
                    yssd_s[rows, g * 512 + hp * LANES:g * 512 + (hp + 1) * LANES] = yd + yo[:, ps]
                hs[:, gs] = h_g * e_l[:, gs] + _tn(bg, xf[:, gs].astype(BF16))
            return carry

        lax.fori_loop(0, nc, chunk, 0, unroll=True)

        ys = yssd_s[...] + dsk_ref[...] * xs_s[...]
        aux_ref[:, A_YS:A_YS + D] = ys.astype(BF16)
        gt = ys * _silu(z_ref[...].astype(F32))
        for g in range(2):
            gs = slice(g * 512, (g + 1) * 512)
            gg = gt[:, gs]
            rn = lax.rsqrt(jnp.mean(gg * gg, axis=-1, keepdims=True) + EPS)
            y_ref[:, D + g * 512:D + (g + 1) * 512] = (gg * rn * nrm_ref[:, gs]).astype(BF16)

    params = [prm[k] for k in ("scw", "cwx", "cwbc", "cbx", "cbbc", "dtb", "alog", "dskx", "nrm", "eh")]
    in_specs = [main(D, C_UB), main(D, C_UC), main(D, C_UH), main(D, C_Z), main(D, C_XS), main(512, C_BC),
                main(LANES, C_DT), halo(D, C_UC), halo(D, C_UH), halo(D, C_XS), halo(512, C_BC)]
    in_specs += [_param_spec(a, prm["layer"]) for a in params]
    return pl.pallas_call(
        body, grid=(nblk,),
        in_specs=in_specs,
        out_specs=[pl.BlockSpec((tt, MIX), lambda i: (i, 0)),
                   pl.BlockSpec((nc, NSTATE, D), lambda i: (i, 0, 0)),
                   pl.BlockSpec((tt, AUX_W), lambda i: (i, 0))],
        out_shape=[SDS((t_len, MIX), BF16), SDS((t_len // CHUNK, NSTATE, D), BF16), SDS((t_len, AUX_W), BF16)],
        scratch_shapes=[pltpu.VMEM((NSTATE, D), F32), pltpu.VMEM((tt, D), F32), pltpu.VMEM((tt, 512), F32),
                        pltpu.VMEM((tt, D), F32), pltpu.VMEM((tt, D), F32), pltpu.VMEM((tt, D), F32)],
        compiler_params=_cparams(("arbitrary",)), name="mixer_fwd")(*([proj] * 11), *params)


def mixer_bwd(proj, dy, states, aux, prm, tt):
    t_len = proj.shape[0]
    nblk = t_len // tt
    nc = tt // CHUNK

    def rev(i):
        return nblk - 1 - i

    def main(width, col):
        return pl.BlockSpec((tt, width), lambda i, c=col // width: (rev(i), c))

    def body(ub_ref, uc_ref, uh_ref, z_ref, xr_ref, bcr_ref, dtr_ref, dy_ref, st_ref, aux_ref,
             scw_ref, cwx_ref, cwbc_ref, cbx_ref, cbbc_ref, dtb_ref, alog_ref, dsk_ref, nrm_ref, eh_ref, eht_ref,
             dp_ref, gscw_ref, gcwx_ref, gcwbc_ref, gvec_ref, gdt_ref,
             dhs, xs_s, bc_s, dtx_s, cumx_s, dys_s, dxs_s, dbc_s, red_s, ddtx_s, nx_cv, nx_px, nx_pbc, sgx_s, sgbc_s):
        i = pl.program_id(0)

        @pl.when(i == 0)
        def _():
            dhs[...] = jnp.zeros_like(dhs)
            nx_cv[...] = jnp.zeros_like(nx_cv)
            nx_px[...] = jnp.zeros_like(nx_px)
            nx_pbc[...] = jnp.zeros_like(nx_pbc)
            gscw_ref[...] = jnp.zeros_like(gscw_ref)
            gcwx_ref[...] = jnp.zeros_like(gcwx_ref)
            gcwbc_ref[...] = jnp.zeros_like(gcwbc_ref)
            gvec_ref[...] = jnp.zeros_like(gvec_ref)
            gdt_ref[...] = jnp.zeros_like(gdt_ref)

        uc = uc_ref[...].astype(F32)
        uh = uh_ref[...].astype(F32)
        v = uc * uh
        dya = dy_ref[:, 0:D].astype(F32)
        dp_ref[:, C_UB:C_UB + D] = (dya * aux_ref[:, A_CV:A_CV + D].astype(F32)).astype(BF16)
        dcv = dya * ub_ref[...].astype(F32)
        sm = _shift_matrix(tt, True)
        ups = _shifts_up(dcv.astype(BF16), nx_cv[...], sm, 2) + [dcv]
        dv = None
        for k in range(3):
            gscw_ref[k:k + 1, :] += jnp.sum(v * ups[k], axis=0, keepdims=True)
            term = scw_ref[k:k + 1, :] * ups[k]
            dv = term if dv is None else dv + term
        nx_cv[...] = dcv[0:8]
        dp_ref[:, C_UC:C_UC + D] = (dv * uh).astype(BF16)
        dp_ref[:, C_UH:C_UH + D] = (dv * uc).astype(BF16)

        pre_x = aux_ref[:, A_PX:A_PX + D].astype(F32)
        pre_bc = aux_ref[:, A_PBC:A_PBC + 512].astype(F32)
        sg_x = _sigmoid(pre_x)
        sg_bc = _sigmoid(pre_bc)
        sgx_s[...] = sg_x
        sgbc_s[...] = sg_bc
        xs = pre_x * sg_x
        xs_s[...] = xs
        bc_s[...] = pre_bc * sg_bc
        dt_pre = dtr_ref[...].astype(F32) + dtb_ref[...]
        dt = _softplus(dt_pre)
        a_neg = -jnp.exp(alog_ref[...])
        pos = lax.broadcasted_iota(jnp.int32, (tt, LANES), 0) & (CHUNK - 1)
        cum = _chunk_cumsum(dt * a_neg, pos)
        eh = eh_ref[...]
        eht = eht_ref[...]
        dtx_s[...] = _expand(dt, eh)
        cumx_s[...] = _expand(cum, eh)

        irep, causal = _lane_masks()
        irep_g, _ = _lane_masks(512)
        rep = _rep_matrix()
        row8 = lax.broadcasted_iota(jnp.int32, (8, 512), 0)
        lane128 = lax.broadcasted_iota(jnp.int32, (CHUNK, LANES), 1)

        z = z_ref[...].astype(F32)
        sg_z = _sigmoid(z)
        sz = z * sg_z
        dsz = sg_z * (1.0 + z * (1.0 - sg_z))
        ys = aux_ref[:, A_YS:A_YS + D].astype(F32)
        gt = ys * sz
        dyb = dy_ref[:, D:MIX].astype(F32)
        for g in range(2):
            gs = slice(g * 512, (g + 1) * 512)
            gg = gt[:, gs]
            rn = lax.rsqrt(jnp.mean(gg * gg, axis=-1, keepdims=True) + EPS)
            gvec_ref[0:1, gs] += jnp.sum(dyb[:, gs] * gg * rn, axis=0, keepdims=True)
            dgn = dyb[:, gs] * nrm_ref[:, gs]
            dgt = rn * (dgn - gg * (rn * rn) * jnp.mean(dgn * gg, axis=-1, keepdims=True))
            dys = dgt * sz[:, gs]
            dys_s[:, gs] = dys
            dp_ref[:, C_Z + g * 512:C_Z + (g + 1) * 512] = (dgt * ys[:, gs] * dsz[:, gs]).astype(BF16)
        dys_all = dys_s[...]
        gvec_ref[1:2, :] += jnp.sum(dys_all * xs, axis=0, keepdims=True)

        def bwd_chunk(cc, carry):
            c = nc - 1 - cc
            r0 = pl.multiple_of(c * CHUNK, CHUNK)
            rows = pl.ds(r0, CHUNK)
            cumx = cumx_s[rows, :]
            cum_l = cumx[CHUNK - 1:CHUNK, :]
            xs_c = xs_s[rows, :]
            dtx = dtx_s[rows, :]
            xd = xs_c * dtx
            f = jnp.exp(cum_l - cumx)
            xf = xd * f
            ex = jnp.exp(cumx)
            e_l = jnp.exp(cum_l)
            rvec = jnp.sum(jnp.where(irep, cumx, 0.0), axis=0, keepdims=True)
            lam = jnp.exp(jnp.where(causal, cumx - rvec, NEG_BIG))
            bc = bc_s[rows, :]
            dyc = dys_s[rows, :]
            for g in range(2):
                gs = slice(g * 512, (g + 1) * 512)
                bg = bc[:, g * NSTATE:(g + 1) * NSTATE].astype(BF16)
                cg = bc[:, 256 + g * NSTATE:256 + (g + 1) * NSTATE].astype(BF16)
                h0 = st_ref[c, :, gs]
                dh = dhs[:, gs]
                dh_b = dh.astype(BF16)
                xf_g = xf[:, gs]
                dxf = _nn(bg, dh_b)
                db = _nt(xf_g.astype(BF16), dh_b)
                s_rep = _nn(_nt(cg, bg).astype(BF16), rep)
                lam_g = lam[:, gs]
                m_g = s_rep * lam_g
                m_b = m_g.astype(BF16)
                ex_g = ex[:, gs]
                dy_g = dyc[:, gs]
                yo = _nn(cg, h0) * ex_g
                dg_b = (dy_g * ex_g).astype(BF16)
                dc = _nt(dg_b, h0)
                el_g = e_l[:, gs]
                dee = jnp.sum(dh * h0.astype(F32), axis=0, keepdims=True) * el_g
                dhs[:, gs] = dh * el_g + _tn(cg, dg_b)
                xd_b = xd[:, gs].astype(BF16)
                dy_b = dy_g.astype(BF16)
                dm_parts, dxd_parts = [], []
                for hp in range(4):
                    ps = slice(hp * LANES, (hp + 1) * LANES)
                    bd = _blockdiag(xd_b[:, ps])
                    dm_parts.append(_nt(dy_b[:, ps], bd))
                    t2 = _tn(m_b[:, ps], dy_b[:, ps])
                    dxd_parts.append(jnp.where(lane128 < HDIM, t2[0:CHUNK], t2[CHUNK:2 * CHUNK]))
                dm = jnp.concatenate(dm_parts, axis=1)
                dxd = jnp.concatenate(dxd_parts, axis=1) + dxf * f[:, gs]
                dseg = dm * m_g
                ds_b = _nt((dm * lam_g).astype(BF16), rep).astype(BF16)
                dc = dc + _nn(ds_b, bg)
                db = db + _tn(ds_b, cg)
                colsum = jnp.sum(dseg, axis=0, keepdims=True)
                dxfxf = dxf * xf_g
                red = dseg - jnp.where(irep_g, colsum, 0.0) + dy_g * yo - dxfxf
                last = jnp.sum(dxfxf, axis=0, keepdims=True) + dee
                red_s[rows, gs] = red
                tail = pl.ds(pl.multiple_of(r0 + CHUNK - 8, 8), 8)
                red_s[tail, gs] += jnp.where(row8 == 7, last, 0.0)
                ddtx_s[rows, gs] = dxd * xs_c[:, gs]
                dxs_s[rows, gs] = dxd * dtx[:, gs] + dsk_ref[:, gs] * dy_g
                dbc_s[rows, g * NSTATE:(g + 1) * NSTATE] = db
                dbc_s[rows, 256 + g * NSTATE:256 + (g + 1) * NSTATE] = dc
            return carry

        lax.fori_loop(0, nc, bwd_chunk, 0, unroll=True)

        dcum = _head_reduce(red_s[...], eht)
        da = _chunk_rcumsum(dcum, pos)
        ddt = _head_reduce(ddtx_s[...], eht) + da * a_neg
        gdt_ref[1:2, :] += jnp.sum(da * dt, axis=0, keepdims=True) * a_neg
        ddt_raw = ddt * _sigmoid(dt_pre)
        lane_t = lax.broadcasted_iota(jnp.int32, (tt, LANES), 1)
        ddt_raw = jnp.where(lane_t < NHEAD, ddt_raw, 0.0)
        gdt_ref[0:1, :] += jnp.sum(ddt_raw, axis=0, keepdims=True)
        dp_ref[:, C_DT:C_DT + LANES] = ddt_raw.astype(BF16)

        sg_x = sgx_s[...]
        sg_bc = sgbc_s[...]
        pre_x = aux_ref[:, A_PX:A_PX + D].astype(F32)
        pre_bc = aux_ref[:, A_PBC:A_PBC + 512].astype(F32)
        dpx = dxs_s[...] * (sg_x * (1.0 + pre_x * (1.0 - sg_x)))
        dpbc = dbc_s[...] * (sg_bc * (1.0 + pre_bc * (1.0 - sg_bc)))
        gvec_ref[2:3, :] += jnp.sum(dpx, axis=0, keepdims=True)
        gcwbc_ref[4:5, :] += jnp.sum(dpbc, axis=0, keepdims=True)
        xraw = xr_ref[...].astype(F32)
        bcraw = bcr_ref[...].astype(F32)
        ups_x = _shifts_up(dpx.astype(BF16), nx_px[...], sm, 3) + [dpx]
        ups_bc = _shifts_up(dpbc.astype(BF16), nx_pbc[...], sm, 3) + [dpbc]
        dxr, dbcr = None, None
        for k in range(4):
            up_x = ups_x[k]
            up_bc = ups_bc[k]
            gcwx_ref[k:k + 1, :] += jnp.sum(xraw * up_x, axis=0, keepdims=True)
            gcwbc_ref[k:k + 1, :] += jnp.sum(bcraw * up_bc, axis=0, keepdims=True)
            tx = cwx_ref[k:k + 1, :] * up_x
            tb = cwbc_ref[k:k + 1, :] * up_bc
            dxr = tx if dxr is None else dxr + tx
            dbcr = tb if dbcr is None else dbcr + tb
        nx_px[...] = dpx[0:8]
        nx_pbc[...] = dpbc[0:8]
        dp_ref[:, C_XS:C_XS + D] = dxr.astype(BF16)
        dp_ref[:, C_BC:C_BC + 512] = dbcr.astype(BF16)

        @pl.when(i == nblk - 1)
        def _():
            gdt_ref[2:3, :] = _head_reduce(gvec_ref[1:2, :] * jnp.ones((8, 1), F32), eht)[0:1, :]

    def const(shape):
        return pl.BlockSpec(shape, lambda i: (0, 0))

    params = [prm[k] for k in ("scw", "cwx", "cwbc", "cbx", "cbbc", "dtb", "alog", "dskx", "nrm", "eh", "eht")]
    in_specs = [main(D, C_UB), main(D, C_UC), main(D, C_UH), main(D, C_Z), main(D, C_XS), main(512, C_BC),
                main(LANES, C_DT),
                pl.BlockSpec((tt, MIX), lambda i: (rev(i), 0)),
                pl.BlockSpec((nc, NSTATE, D), lambda i: (rev(i), 0, 0)),
                pl.BlockSpec((tt, AUX_W), lambda i: (rev(i), 0))]
    in_specs += [_param_spec(a, prm["layer"]) for a in params]
    return pl.pallas_call(
        body, grid=(nblk,),
        in_specs=in_specs,
        out_specs=[pl.BlockSpec((tt, NINP), lambda i: (rev(i), 0)),
                   const((8, D)), const((8, D)), const((8, 512)), const((8, D)), const((8, LANES))],
        out_shape=[SDS((t_len, NINP), BF16), SDS((8, D), F32), SDS((8, D), F32), SDS((8, 512), F32),
                   SDS((8, D), F32), SDS((8, LANES), F32)],
        scratch_shapes=[pltpu.VMEM((NSTATE, D), F32),
                        pltpu.VMEM((tt, D), F32), pltpu.VMEM((tt, 512), F32),
                        pltpu.VMEM((tt, D), F32), pltpu.VMEM((tt, D), F32),
                        pltpu.VMEM((tt, D), F32), pltpu.VMEM((tt, D), F32),
                        pltpu.VMEM((tt, 512), F32),
                        pltpu.VMEM((tt, D), F32), pltpu.VMEM((tt, D), F32),
                        pltpu.VMEM((8, D), F32), pltpu.VMEM((8, D), F32), pltpu.VMEM((8, 512), F32),
                        pltpu.VMEM((tt, D), F32), pltpu.VMEM((tt, 512), F32)],
        compiler_params=_cparams(("arbitrary",)), name="mixer_bwd")(
            *([proj] * 7), dy, states, aux, *params)


TN_IN = 1152
DW_TOKENS = 4096


def layer_fwd_mix(x, lw, prm, tt):
    proj, h1 = norm_matmul(x, lw["nw1"], lw["win"], "in_proj")
    y, st, aux = mixer_fwd(proj, prm, tt)
    return h1, proj, (st, aux), y


def layer_fwd_mlp(x, mixed, lw, between=None):
    h1, proj, st, y = mixed
    x1 = matmul_residual(y, lw["wout"], x, False, "out_proj")
    u, h2 = norm_matmul(x1, lw["nw2"], lw["wup"], "up_proj")
    x2 = matmul_residual(u, lw["wdn"], x1, True, "down_proj", None if between is None else between(u))
    return x2, (x, h1, proj, st, y, x1, h2, u)


def layer_fwd(x, lw, prm, tt):
    return layer_fwd_mlp(x, layer_fwd_mix(x, lw, prm, tt), lw)


def _dw(a, b, a_cols, b_cols, relu2, name, tt_max=2048):
    m_len, n_len = a.shape[1], b.shape[1]
    n_a, n_b = m_len // a_cols, n_len // b_cols
    assert n_a == 1 or n_b == 1
    if n_b == 1:
        return matmul_tn(
            a, b,
            lambda t_: pl.BlockSpec((t_, a_cols), lambda n, t: (t, n)),
            lambda t_: pl.BlockSpec((t_, n_len), lambda n, t: (t, 0)),
            pl.BlockSpec((a_cols, n_len), lambda n, t: (n, 0)), SDS((m_len, n_len), BF16), n_a, relu2, name, tt_max)
    return matmul_tn(
        a, b,
        lambda t_: pl.BlockSpec((t_, m_len), lambda n, t: (t, 0)),
        lambda t_: pl.BlockSpec((t_, b_cols), lambda n, t: (t, n)),
        pl.BlockSpec((m_len, b_cols), lambda n, t: (0, n)), SDS((m_len, n_len), BF16), n_b, relu2, name, tt_max)


def layer_bwd_mlp(dx2, dx2b, lw, saved):
    _, _, _, _, y, x1, h2, u = saved
    du = matmul_nt_act(dx2b, lw["wdn"], u, "mlp_bwd_du")
    g_wdn = _dw(u, dx2b, 1024, D, True, "dw_down")
    dx1, dx1b, g_nw2 = matmul_nt_norm_bwd(du, lw["wup"], x1, lw["nw2"], dx2, "mlp_bwd_dx")
    cb = DFF // N_DEV
    g_wup = matmul_tn(
        h2, du,
        lambda t_: pl.BlockSpec((t_, D), lambda n, t: (t, 0)),
        lambda t_: pl.BlockSpec((t_, 2 * cb), lambda n, t: (t, n)),
        pl.BlockSpec((2, D, cb), lambda n, t: (n, 0, 0)), SDS((N_DEV, D, cb), BF16), N_DEV // 2, False, "dw_up",
        DW_TOKENS)
    dy = matmul_nt_act(dx1b, lw["wout"], None, "out_bwd_dy")
    g_wout = _dw(y, dx1b, 1024, D, False, "dw_out", DW_TOKENS)
    return dx1, dx1b, dy, {"wout": g_wout, "wup": g_wup, "wdn": g_wdn, "nw2": g_nw2[0]}


def layer_bwd_mix(dx1, dy, lw, prm, saved, tt):
    x, h1, proj, st = saved[:4]
    dproj, gscw, gcwx, gcwbc, gvec, gdt = mixer_bwd(proj, dy, st[0], st[1], prm, tt)
    dx0, dx0b, g_nw1 = matmul_nt_norm_bwd(dproj, lw["win"], x, lw["nw1"], dx1, "in_bwd_dx")
    g_win = _dw(h1, dproj, D, TN_IN, False, "dw_in", DW_TOKENS)
    grads = {
        "win": g_win, "scw": gscw[0:3], "cw": jnp.concatenate([gcwx[0:4], gcwbc[0:4]], axis=1),
        "cb": jnp.concatenate([gvec[2], gcwbc[4]], axis=0),
        "dtb": gdt[0, :NHEAD], "alog": gdt[1, :NHEAD], "dsk": gdt[2, :NHEAD],
        "nrm": gvec[0], "nw1": g_nw1[0],
    }
    return dx0, dx0b, grads


def layer_bwd(dx2, dx2b, lw, prm, saved, tt):
    dx1, dx1b, dy, g_mlp = layer_bwd_mlp(dx2, dx2b, lw, saved)
    dx0, dx0b, g_mix = layer_bwd_mix(dx1, dy, lw, prm, saved, tt)
    return dx0, dx0b, {**g_mlp, **g_mix}


def stacked_params(conv_b, dt_bias, a_log, d_skip, ssd_norm_w):
    def lanes128(a):
        return jnp.pad(a, ((0, 0), (0, LANES - a.shape[1])))[:, None, :]

    return {"cbx": conv_b[:, None, :D], "cbbc": conv_b[:, None, D:], "dtb": lanes128(dt_bias),
            "alog": lanes128(a_log), "dskx": jnp.repeat(d_skip, HDIM, axis=1)[:, None, :],
            "nrm": ssd_norm_w[:, None, :]}


def layer_params(layer, win, scw, cw, nw1, nw2, stacked, eh, eht):
    def rows8(a):
        return jnp.pad(a, ((0, 8 - a.shape[0]), (0, 0)))

    lw = {"win": win, "nw1": nw1[None, :], "nw2": nw2[None, :]}
    prm = dict(stacked, layer=layer, scw=rows8(scw), cwx=rows8(cw[:, :D]), cwbc=rows8(cw[:, D:]), eh=eh, eht=eht)
    return lw, prm


def _param_spec(arr, layer):
    if arr.ndim == 3:
        return pl.BlockSpec((None,) + arr.shape[1:], lambda i: (layer, 0, 0))
    return pl.BlockSpec(arr.shape, lambda i: (0, 0))


def _flip(v, bit):
    return 1 - v if bit else v


def all_gather(arrs, name):
    n = len(arrs)

    def body(*refs):
        ins, outs = refs[:n], refs[n:2 * n]
        send_sems, recv_sems, local_sems = refs[2 * n:]
        x, y, c = lax.axis_index("x"), lax.axis_index("y"), lax.axis_index("c")
        sibling = (x, y, 1 - c)
        chips = [(1 - x, y), (x, 1 - y), (1 - x, 1 - y)]

        def idx(px, py, pc):
            return 4 * px + 2 * py + pc

        def copy(a, k, block, to, src=None):
            dst = outs[a].at[idx(*block)]
            return pltpu.make_async_remote_copy(
                src_ref=dst if src is None else src, dst_ref=dst,
                send_sem=send_sems.at[a, k], recv_sem=recv_sems.at[a, k], device_id=to, device_id_type=MESH)

        me = (x, y, c)
        mine = [pltpu.make_async_copy(ins[a], outs[a].at[idx(*me)], local_sems.at[a]) for a in range(n)]
        for cp in mine:
            cp.start()
        first = []
        for a in range(n):
            first.append(copy(a, 0, me, sibling, src=ins[a]))
            first += [copy(a, 1 + j, me, (*chip, c), src=ins[a]) for j, chip in enumerate(chips)]
        for cp in first:
            cp.start()
        passed = []
        for j, chip in enumerate(chips):
            for a in range(n):
                copy(a, 1 + j, (*chip, c), me).wait_recv()
                cp = copy(a, 4 + j, (*chip, c), sibling)
                cp.start()
                passed.append(cp)
        for a in range(n):
            copy(a, 0, sibling, me).wait_recv()
            for j, chip in enumerate(chips):
                copy(a, 4 + j, (*chip, 1 - c), me).wait_recv()
        for cp in first + passed:
            cp.wait_send()
        for cp in mine:
            cp.wait()

    any_spec = pl.BlockSpec(memory_space=pl.ANY)
    return pl.pallas_call(
        body, in_specs=[any_spec] * n, out_specs=[any_spec] * n,
        out_shape=[SDS((N_DEV,) + a.shape, a.dtype) for a in arrs],
        scratch_shapes=[pltpu.SemaphoreType.DMA((n, 7)), pltpu.SemaphoreType.DMA((n, 7)),
                        pltpu.SemaphoreType.DMA((n,))],
        name=name)(*arrs)


HBM_SPEC = pl.BlockSpec(memory_space=pltpu.HBM)
SEM_SPEC = pl.BlockSpec(memory_space=pltpu.SEMAPHORE)
SIDE_EFFECT = pltpu.SideEffectType.DATAFLOW_SIDE_EFFECTING
N_PEER = N_DEV - 1


def _peer(mask):
    x, y, c = lax.axis_index("x"), lax.axis_index("y"), lax.axis_index("c")
    return _flip(x, mask & 4), _flip(y, mask & 2), _flip(c, mask & 1)


ALL_PEERS = tuple(range(1, N_DEV))
SIBLING_AND_CHIPS = (1, 2, 4, 6)


def exchange_start(srcs, per_peer, name, after=None, masks=ALL_PEERS):
    n = len(srcs)
    npeer = len(masks)
    lands = [SDS((N_DEV,) + (a.shape[1:] if per_peer else a.shape), a.dtype) for a in srcs]
    n_in = 2 * n + (after is not None)

    def body(*refs):
        src_refs, land_refs = refs[:n], refs[n:2 * n]
        send_sems, recv_sems = refs[n_in], refs[n_in + 1]
        token = refs[-1]
        x, y, c = lax.axis_index("x"), lax.axis_index("y"), lax.axis_index("c")
        me = 4 * x + 2 * y + c
        for a in range(n):
            for k, mask in enumerate(masks):
                px, py, pc = _peer(mask)
                part = src_refs[a].at[4 * px + 2 * py + pc] if per_peer else src_refs[a]
                pltpu.make_async_remote_copy(
                    src_ref=part, dst_ref=land_refs[a].at[me], send_sem=send_sems.at[a * npeer + k],
                    recv_sem=recv_sems.at[a * npeer + k], device_id=(px, py, pc), device_id_type=MESH).start()
        token[...] = jnp.zeros_like(token)

    out = pl.pallas_call(
        body, name=name,
        out_shape=(pltpu.SemaphoreType.DMA((n * npeer,)), pltpu.SemaphoreType.DMA((n * npeer,)),
                   *[pltpu.HBM(a.shape, a.dtype) for a in srcs], *[pltpu.HBM(l.shape, l.dtype) for l in lands],
                   SDS((8, LANES), F32)),
        in_specs=(HBM_SPEC,) * (2 * n) + ((pl.BlockSpec(memory_space=pl.ANY),) if after is not None else ()),
        out_specs=(SEM_SPEC, SEM_SPEC) + (HBM_SPEC,) * (2 * n) + (pl.BlockSpec(memory_space=pltpu.VMEM),),
        input_output_aliases={k: 2 + k for k in range(2 * n)},
        compiler_params=pltpu.CompilerParams(has_side_effects=SIDE_EFFECT),
    )(*[pltpu.with_memory_space_constraint(a, pltpu.HBM) for a in srcs],
      *[pltpu.with_memory_space_constraint(lax.empty(l.shape, l.dtype), pltpu.HBM) for l in lands],
      *([after] if after is not None else []))
    return out[0], out[1], list(out[2:2 + n]), list(out[2 + n:2 + 2 * n]), out[-1]


def exchange_wait(started, after, per_peer, name, masks=ALL_PEERS):
    send_sems, recv_sems, srcs, lands, _ = started
    n = len(srcs)
    npeer = len(masks)

    def body(*refs):
        src_refs, land_refs = refs[:n], refs[n:2 * n]
        send_sems, recv_sems = refs[2 * n], refs[2 * n + 1]
        for k, mask in enumerate(masks):
            for a in range(n):
                copy = pltpu.make_async_remote_copy(
                    src_ref=src_refs[a].at[0] if per_peer else src_refs[a], dst_ref=land_refs[a].at[0],
                    send_sem=send_sems.at[a * npeer + k], recv_sem=recv_sems.at[a * npeer + k],
                    device_id=_peer(mask), device_id_type=MESH)
                copy.wait_send()
                copy.wait_recv()

    out = pl.pallas_call(
        body, name=name,
        out_shape=tuple(pltpu.HBM(a.shape, a.dtype) for a in srcs + lands),
        in_specs=(HBM_SPEC,) * (2 * n) + (SEM_SPEC, SEM_SPEC, pl.BlockSpec(memory_space=pl.ANY)),
        out_specs=(HBM_SPEC,) * (2 * n), input_output_aliases={k: k for k in range(2 * n)},
        compiler_params=pltpu.CompilerParams(has_side_effects=SIDE_EFFECT),
    )(*srcs, *lands, send_sems, recv_sems, after)
    return list(out[:n]), list(out[n:])


def relay_to_sibling(lands, name):
    n = len(lands)
    chips = (2, 4, 6)

    def body(*refs):
        land_refs = refs[n:2 * n]
        send_sems, recv_sems = refs[2 * n], refs[2 * n + 1]
        x, y, c = lax.axis_index("x"), lax.axis_index("y"), lax.axis_index("c")
        copies = []
        for a in range(n):
            for k, mask in enumerate(chips):
                px, py, _ = _peer(mask)
                block = land_refs[a].at[4 * px + 2 * py + c]
                cp = pltpu.make_async_remote_copy(
                    src_ref=block, dst_ref=block, send_sem=send_sems.at[a * 3 + k], recv_sem=recv_sems.at[a * 3 + k],
                    device_id=(x, y, 1 - c), device_id_type=MESH)
                cp.start()
                copies.append((cp, a, k, land_refs[a].at[4 * px + 2 * py + 1 - c]))
        for cp, a, k, arriving in copies:
            cp.wait_send()
            pltpu.make_async_remote_copy(
                src_ref=arriving, dst_ref=arriving, send_sem=send_sems.at[a * 3 + k], recv_sem=recv_sems.at[a * 3 + k],
                device_id=(x, y, 1 - c), device_id_type=MESH).wait_recv()

    any_spec = pl.BlockSpec(memory_space=pl.ANY)
    return list(pl.pallas_call(
        body, in_specs=[any_spec] * n, out_specs=[any_spec] * n,
        out_shape=[SDS(a.shape, a.dtype) for a in lands],
        input_output_aliases={k: k for k in range(n)},
        scratch_shapes=[pltpu.SemaphoreType.DMA((n * 3,)), pltpu.SemaphoreType.DMA((n * 3,))],
        name=name)(*lands))


def relay_start(lands, name):
    n = len(lands)

    def body(*refs):
        land_refs = refs[:n]
        send_sems, recv_sems = refs[n], refs[n + 1]
        token = refs[-1]
        x, y, c = lax.axis_index("x"), lax.axis_index("y"), lax.axis_index("c")
        for a in range(n):
            for k, mask in enumerate((2, 4, 6)):
                px, py, _ = _peer(mask)
                block = land_refs[a].at[4 * px + 2 * py + c]
                pltpu.make_async_remote_copy(
                    src_ref=block, dst_ref=block, send_sem=send_sems.at[a * 3 + k], recv_sem=recv_sems.at[a * 3 + k],
                    device_id=(x, y, 1 - c), device_id_type=MESH).start()
        token[...] = jnp.zeros_like(token)

    out = pl.pallas_call(
        body, name=name,
        out_shape=(pltpu.SemaphoreType.DMA((n * 3,)), pltpu.SemaphoreType.DMA((n * 3,)),
                   *[pltpu.HBM(a.shape, a.dtype) for a in lands], SDS((8, LANES), F32)),
        in_specs=(HBM_SPEC,) * n,
        out_specs=(SEM_SPEC, SEM_SPEC) + (HBM_SPEC,) * n + (pl.BlockSpec(memory_space=pltpu.VMEM),),
        input_output_aliases={k: 2 + k for k in range(n)},
        compiler_params=pltpu.CompilerParams(has_side_effects=SIDE_EFFECT),
    )(*lands)
    return out[0], out[1], list(out[2:2 + n]), out[-1]


def relay_wait(started, after, name):
    send_sems, recv_sems, lands, _ = started
    n = len(lands)

    def body(*refs):
        land_refs = refs[:n]
        send_sems, recv_sems = refs[n], refs[n + 1]
        x, y, c = lax.axis_index("x"), lax.axis_index("y"), lax.axis_index("c")
        for a in range(n):
            for k in range(3):
                copy = pltpu.make_async_remote_copy(
                    src_ref=land_refs[a].at[0], dst_ref=land_refs[a].at[0], send_sem=send_sems.at[a * 3 + k],
                    recv_sem=recv_sems.at[a * 3 + k], device_id=(x, y, 1 - c), device_id_type=MESH)
                copy.wait_send()
                copy.wait_recv()

    return list(pl.pallas_call(
        body, name=name,
        out_shape=tuple(pltpu.HBM(a.shape, a.dtype) for a in lands),
        in_specs=(HBM_SPEC,) * n + (SEM_SPEC, SEM_SPEC, pl.BlockSpec(memory_space=pl.ANY)),
        out_specs=(HBM_SPEC,) * n, input_output_aliases={k: k for k in range(n)},
        compiler_params=pltpu.CompilerParams(has_side_effects=SIDE_EFFECT),
    )(*lands, send_sems, recv_sems, after))


IN_SHARD = NIN // N_DEV
SLOT_W = 768


def _slot_window(j):
    return (IN_SHARD * j // LANES) * LANES, -(-(IN_SHARD * (j + 1)) // LANES) * LANES


def _placement(j):
    a, b = _slot_window(j)
    r = lax.broadcasted_iota(jnp.int32, (SLOT_W, b - a), 0)
    c = lax.broadcasted_iota(jnp.int32, (SLOT_W, b - a), 1)
    return jnp.where(jnp.logical_and(c == r + (IN_SHARD * j - a), r < IN_SHARD), 1.0, 0.0).astype(BF16)


def assemble_w_in(land):
    tm = 256

    def body(l_ref, o_ref, acc):
        acc[...] = jnp.zeros_like(acc)
        for j in range(N_DEV):
            a, b = _slot_window(j)
            acc[:, a:b] += _nn(l_ref[j], _placement(j))
        o_ref[...] = acc[...].astype(BF16)

    return pl.pallas_call(
        body, grid=(D // tm,),
        in_specs=[pl.BlockSpec((N_DEV, tm, SLOT_W), lambda i: (0, i, 0))],
        out_specs=pl.BlockSpec((tm, NINP), lambda i: (i, 0)),
        out_shape=SDS((D, NINP), BF16),
        scratch_shapes=[pltpu.VMEM((tm, NINP), F32)],
        compiler_params=_cparams(("parallel",)), name="assemble_w_in")(land)


def scatter_w_in(dw):
    tm = 256

    def body(d_ref, o_ref):
        for j in range(N_DEV):
            a, b = _slot_window(j)
            o_ref[j] = _nt(d_ref[:, a:b], _placement(j)).astype(BF16)

    return pl.pallas_call(
        body, grid=(D // tm,),
        in_specs=[pl.BlockSpec((tm, NINP), lambda i: (i, 0))],
        out_specs=pl.BlockSpec((N_DEV, tm, SLOT_W), lambda i: (0, i, 0)),
        out_shape=SDS((N_DEV, D, SLOT_W), BF16),
        compiler_params=_cparams(("parallel",)), name="scatter_w_in")(dw)


def _adamw_math(g, w_ref, m_ref, v_ref, g_ref, d_ref, nm_ref, nv_ref):
    mn = ADAM_B1 * m_ref[...] + (1.0 - ADAM_B1) * g
    vn = ADAM_B2 * v_ref[...] + (1.0 - ADAM_B2) * jnp.square(g)
    m_hat = mn / (1.0 - ADAM_B1 ** ADAM_STEP)
    v_hat = vn / (1.0 - ADAM_B2 ** ADAM_STEP)
    g_ref[...] = g
    d_ref[...] = -ADAM_LR * (m_hat / (jnp.sqrt(v_hat) + ADAM_EPS) + ADAM_WD * w_ref[...])
    nm_ref[...] = mn
    nv_ref[...] = vn


def adamw_layers(w, slots, m, v, name):
    depth, r_len, c_len = w.shape
    cs = slots[0].shape[2]
    br = min(128, r_len)
    assert r_len % br == 0

    def body(w_ref, *rest):
        s_refs, (m_ref, v_ref, g_ref, d_ref, nm_ref, nv_ref) = rest[:depth], rest[depth:]
        layer = pl.program_id(0)
        for k in range(depth):
            @pl.when(layer == k)
            def _(k=k):
                g = s_refs[k][0, :, 0:c_len].astype(F32)
                for j in range(1, N_DEV):
                    g = g + s_refs[k][j, :, 0:c_len].astype(F32)
                _adamw_math(g, w_ref, m_ref, v_ref, g_ref, d_ref, nm_ref, nv_ref)

    spec = pl.BlockSpec((None, br, c_len), lambda l, i: (l, i, 0))
    s_specs = [pl.BlockSpec((N_DEV, br, cs), lambda l, i, k=k: (0, jnp.where(l == k, i, 0), 0))
               for k in range(depth)]
    return pl.pallas_call(
        body, grid=(depth, r_len // br),
        in_specs=[spec] + s_specs + [spec, spec],
        out_specs=[spec] * 4, out_shape=[SDS(w.shape, F32)] * 4,
        compiler_params=_cparams(("arbitrary", "arbitrary")), name=name)(w, *slots, m, v)


def adamw(w, slots, m, v, name):
    r_len, c_len = w.shape
    br = r_len if r_len <= 512 else 512
    assert r_len % br == 0

    def body(w_ref, s_ref, m_ref, v_ref, g_ref, d_ref, nm_ref, nv_ref):
        g = s_ref[0].astype(F32)
        for k in range(1, N_DEV):
            g = g + s_ref[k].astype(F32)
        _adamw_math(g, w_ref, m_ref, v_ref, g_ref, d_ref, nm_ref, nv_ref)

    spec = pl.BlockSpec((br, c_len), lambda i: (i, 0))
    return pl.pallas_call(
        body, grid=(r_len // br,),
        in_specs=[spec, pl.BlockSpec((N_DEV, br, c_len), lambda i: (0, i, 0)), spec, spec],
        out_specs=[spec] * 4, out_shape=[SDS((r_len, c_len), F32)] * 4,
        compiler_params=_cparams(("parallel",)), name=name)(w, slots, m, v)


def _adamw_nd(w, slots, m, v, name):
    shp = w.shape
    r = int(np.prod(shp[:-1]))
    outs = adamw(w.reshape(r, shp[-1]), slots.reshape(N_DEV, r, shp[-1]), m.reshape(r, shp[-1]),
                 v.reshape(r, shp[-1]), name)
    return [o.reshape(shp) for o in outs]


SMALL = [("norm_mix_w", DEPTH * D), ("ssd_conv_b", DEPTH * XBC), ("dt_bias", DEPTH * NHEAD),
         ("a_log", DEPTH * NHEAD), ("d_skip", DEPTH * NHEAD), ("ssd_norm_w", DEPTH * D),
         ("norm_mlp_w", DEPTH * D), ("final_norm_w", D)]
SMALL_LEN = sum(s for _, s in SMALL)
SMALL_ROWS = -(-SMALL_LEN // LANES)


def _pack_small(parts):
    flat = jnp.concatenate([parts[k].reshape(-1) for k, _ in SMALL])
    return jnp.pad(flat, (0, SMALL_ROWS * LANES - SMALL_LEN)).reshape(SMALL_ROWS, LANES)


def _unpack_small(packed, shapes):
    flat = packed.reshape(-1)
    out, off = {}, 0
    for k, s in SMALL:
        out[k] = flat[off:off + s].reshape(shapes[k])
        off += s
    return out


def kernel(x, norm_mix_w, w_in, short_conv_w, ssd_conv_w, ssd_conv_b, dt_bias, a_log, d_skip, ssd_norm_w, w_out, norm_mlp_w, w_up, w_down, final_norm_w, loss_target, m_norm_mix_w, m_w_in, m_short_conv_w, m_ssd_conv_w, m_ssd_conv_b, m_dt_bias, m_a_log, m_d_skip, m_ssd_norm_w, m_w_out, m_norm_mlp_w, m_w_up, m_w_down, m_final_norm_w, v_norm_mix_w, v_w_in, v_short_conv_w, v_ssd_conv_w, v_ssd_conv_b, v_dt_bias, v_a_log, v_d_skip, v_ssd_norm_w, v_w_out, v_norm_mlp_w, v_w_up, v_w_down, v_final_norm_w):
    xs = x[0]
    t_len = xs.shape[0]
    tt = min(256, t_len)
    eh, eht = _head_matrices()
    stacked = stacked_params(ssd_conv_b, dt_bias, a_log, d_skip, ssd_norm_w)
    me = 4 * lax.axis_index("x") + 2 * lax.axis_index("y") + lax.axis_index("c")

    def start_first(i, after):
        return exchange_start(
            [jnp.pad(w_in[i].astype(BF16), ((0, 0), (0, SLOT_W - IN_SHARD))), short_conv_w[i], ssd_conv_w[i]],
            False, "w_in_start_%d" % i, after, SIBLING_AND_CHIPS)

    def start_rest(i, after):
        return exchange_start([w_out[i].astype(BF16), w_up[i].astype(BF16), w_down[i].astype(BF16)], False,
                              "w_rest_start_%d" % i, after, SIBLING_AND_CHIPS)

    def fill_own(srcs, lands, per_peer):
        own = [lax.dynamic_index_in_dim(s_, me, 0, keepdims=False) for s_ in srcs] if per_peer else srcs
        return [lax.dynamic_update_index_in_dim(l_, o_, me, 0) for l_, o_ in zip(lands, own)]

    def finish_weights(started, after, name):
        srcs, lands = exchange_wait(started, after, False, name + "_wait", SIBLING_AND_CHIPS)
        return fill_own(srcs, relay_to_sibling(lands, name + "_relay"), False)

    act = xs
    saved, layers = [], []
    first = start_first(0, None)
    rest = start_rest(0, first[4])
    token = first[4][0, 0] + rest[4][0, 0]
    pending = []
    for i in range(DEPTH):
        if i == 0:
            g_in, g_sc, g_cw = finish_weights(first, act, "w_in_0")
        else:
            srcs_f, relay_f = pending.pop()
            g_in, g_sc, g_cw = fill_own(srcs_f, relay_wait(relay_f, act, "w_in_%d_relay_wait" % i), False)
        if 0 < i < DEPTH - 1:
            first = start_first(i + 1, g_in)
            token = token + first[4][0, 0]
        lw, prm = layer_params(
            i, assemble_w_in(g_in), g_sc.transpose(1, 0, 2).reshape(3, D), g_cw.transpose(1, 0, 2).reshape(4, XBC),
            norm_mix_w[i], norm_mlp_w[i], stacked, eh, eht)
        lw["nw1"] = lw["nw1"] + token
        proj, h1 = norm_matmul(act, lw["nw1"], lw["win"], "in_proj")
        srcs_r, lands_r = exchange_wait(rest, proj, False, "w_rest_%d_wait" % i, SIBLING_AND_CHIPS)
        relay = relay_start(lands_r, "w_rest_%d_relay_start" % i)
        order = relay[3][0, 0]
        if i == 0:
            first = start_first(1, relay[3])
            order = order + first[4][0, 0]
        y, st, aux = mixer_fwd(proj, dict(prm, nrm=prm["nrm"] + order), tt)
        mixed = (h1, proj, (st, aux), y)
        g_out, g_up, g_dn = fill_own(srcs_r, relay_wait(relay, y, "w_rest_%d_relay_wait" % i), False)
        lw.update(wout=g_out.reshape(MIX, D), wup=g_up, wdn=g_dn.reshape(DFF, D))
        between = None
        if i + 1 < DEPTH:
            rest = start_rest(i + 1, g_dn)
            token = rest[4][0, 0]
            lw["nw2"] = lw["nw2"] + token

            def between(u, first=first, nxt=i + 1):
                srcs_n, lands_n = exchange_wait(first, u, False, "w_in_%d_wait" % nxt, SIBLING_AND_CHIPS)
                started = relay_start(lands_n, "w_in_%d_relay_start" % nxt)
                pending.append((srcs_n, started))
                return started[3]
        layers.append((lw, prm))
        act, sv = layer_fwd_mlp(act, mixed, lw, between)
        saved.append(sv)
    loss_acc, dx, dxb, g_fw = loss_head(act, final_norm_w[None, :], loss_target[0])

    grads = [None] * DEPTH
    sent_rest, sent_in = [None] * DEPTH, [None] * DEPTH
    token = None
    for i in reversed(range(DEPTH)):
        lw, prm = layers[i]
        if token is not None:
            lw = dict(lw, nw2=lw["nw2"] + token)
        dx1, _, dy, g_mlp = layer_bwd_mlp(dx, dxb, lw, saved[i])
        sent_rest[i] = exchange_start(
            [g_mlp["wout"].reshape(N_DEV, MIX // N_DEV, D), g_mlp["wup"], g_mlp["wdn"].reshape(N_DEV, DFF // N_DEV, D)],
            True, "g_rest_start_%d" % i)
        dx, dxb, g_mix = layer_bwd_mix(dx1, dy, lw, dict(prm, nrm=prm["nrm"] + sent_rest[i][4][0, 0]), saved[i], tt)
        grads[i] = {**g_mlp, **g_mix}
        if i > 0:
            sent_in[i] = exchange_start([scatter_w_in(g_mix["win"])], True, "g_in_start_%d" % i)
            token = sent_in[i][4][0, 0]

    def stack(k):
        return jnp.stack([g[k] for g in grads])

    small = _pack_small({"norm_mix_w": stack("nw1"), "ssd_conv_b": stack("cb"), "dt_bias": stack("dtb"),
                         "a_log": stack("alog"), "d_skip": stack("dsk"), "ssd_norm_w": stack("nrm"),
                         "norm_mlp_w": stack("nw2"), "final_norm_w": g_fw[0]})
    r_small, r_sc, r_cw = all_gather([small, stack("scw"), stack("cw")], "gather_small_grads")
    r_sc = lax.dynamic_slice_in_dim(r_sc, me * (D // N_DEV), D // N_DEV, axis=3)
    r_cw = lax.dynamic_slice_in_dim(r_cw, me * (XBC // N_DEV), XBC // N_DEV, axis=3)
    sent_in[0] = exchange_start([scatter_w_in(grads[0]["win"])], True, "g_in_start_0", after=r_small)

    after = sent_in[0][4]
    recv = [fill_own(*exchange_wait(sent_rest[i], after, True, "g_rest_wait_%d" % i), True) for i in range(DEPTH)]
    res = {}
    res["w_out"] = adamw_layers(w_out, [r[0] for r in recv], m_w_out, v_w_out, "adamw_w_out")
    res["w_up"] = adamw_layers(w_up, [r[1] for r in recv], m_w_up, v_w_up, "adamw_w_up")
    res["w_down"] = adamw_layers(w_down, [r[2] for r in recv], m_w_down, v_w_down, "adamw_w_down")
    after = res["w_down"][1]
    recv_in = [fill_own(*exchange_wait(sent_in[i], after, True, "g_in_wait_%d" % i), True)[0] for i in range(DEPTH)]
    res["w_in"] = adamw_layers(w_in, recv_in, m_w_in, v_w_in, "adamw_w_in")
    res["short_conv_w"] = _adamw_nd(short_conv_w, r_sc, m_short_conv_w, v_short_conv_w, "adamw_short_conv")
    res["ssd_conv_w"] = _adamw_nd(ssd_conv_w, r_cw, m_ssd_conv_w, v_ssd_conv_w, "adamw_ssd_conv")
    small_w = {"norm_mix_w": norm_mix_w, "ssd_conv_b": ssd_conv_b, "dt_bias": dt_bias, "a_log": a_log,
               "d_skip": d_skip, "ssd_norm_w": ssd_norm_w, "norm_mlp_w": norm_mlp_w, "final_norm_w": final_norm_w}
    small_m = {"norm_mix_w": m_norm_mix_w, "ssd_conv_b": m_ssd_conv_b, "dt_bias": m_dt_bias, "a_log": m_a_log,
               "d_skip": m_d_skip, "ssd_norm_w": m_ssd_norm_w, "norm_mlp_w": m_norm_mlp_w,
               "final_norm_w": m_final_norm_w}
    small_v = {"norm_mix_w": v_norm_mix_w, "ssd_conv_b": v_ssd_conv_b, "dt_bias": v_dt_bias, "a_log": v_a_log,
               "d_skip": v_d_skip, "ssd_norm_w": v_ssd_norm_w, "norm_mlp_w": v_norm_mlp_w,
               "final_norm_w": v_final_norm_w}
    shapes = {k: a.shape for k, a in small_w.items()}
    packed = adamw(_pack_small(small_w), r_small, _pack_small(small_m), _pack_small(small_v), "adamw_small")
    unpacked = [_unpack_small(p, shapes) for p in packed]
    for k in small_w:
        res[k] = [u[k] for u in unpacked]

    loss = lax.psum(loss_acc[0, 0], ("x", "y", "c"))
    order = ["norm_mix_w", "w_in", "short_conv_w", "ssd_conv_w", "ssd_conv_b", "dt_bias", "a_log", "d_skip",
             "ssd_norm_w", "w_out", "norm_mlp_w", "w_up", "w_down", "final_norm_w"]
    out = [loss, dx[None]]
    for part in range(4):
        out += [res[k][part] for k in order]
    return tuple(out)
```

```python
import functools

import numpy as np
import jax
import jax.numpy as jnp
from jax import lax
from jax.experimental import pallas as pl
from jax.experimental.pallas import tpu as pltpu

F32 = jnp.float32
BF16 = jnp.bfloat16
SDS = jax.ShapeDtypeStruct

N_DEV = 8
DEPTH = 4
D = 1024
NIN = 5648
NINP = 5760
DFF = 4096
MIX = 2048
NHEAD = 16
HDIM = 64
NSTATE = 128
CHUNK = 64
XBC = 1536
EPS = 1e-5
LANES = 128
NEG_BIG = -1e30

C_UB, C_UC, C_UH, C_Z, C_XS, C_BC, C_DT = 0, 1024, 2048, 3072, 4096, 5120, 5632
A_CV, A_YS, A_PX, A_PBC, AUX_W = 0, 1024, 2048, 3072, 3584

ADAM_LR = 0.001
ADAM_B1 = 0.9
ADAM_B2 = 0.999
ADAM_EPS = 1e-08
ADAM_WD = 0.01
ADAM_STEP = 10

VMEM_LIMIT = 56 * 1024 * 1024
MESH = pl.DeviceIdType.MESH


def _cparams(sem):
    return pltpu.CompilerParams(dimension_semantics=sem, vmem_limit_bytes=VMEM_LIMIT)


def _nt(a, b):
    return lax.dot_general(a, b, (((1,), (1,)), ((), ())), preferred_element_type=F32)


def _tn(a, b):
    return lax.dot_general(a, b, (((0,), (0,)), ((), ())), preferred_element_type=F32)


def _nn(a, b):
    return jnp.dot(a, b, preferred_element_type=F32)


def _sigmoid(v):
    return 0.5 * jnp.tanh(0.5 * v) + 0.5


def _split3(v):
    v1 = v.astype(BF16)
    r1 = v - v1.astype(F32)
    v2 = r1.astype(BF16)
    v3 = (r1 - v2.astype(F32)).astype(BF16)
    return v1, v2, v3


def _expand(v, eh):
    v1, v2, v3 = _split3(v)
    return _nn(v1, eh) + _nn(v2, eh) + _nn(v3, eh)


def _head_reduce(v, eht):
    v1 = v.astype(BF16)
    v2 = (v - v1.astype(F32)).astype(BF16)
    return _nn(v1, eht) + _nn(v2, eht)


def _head_matrices():
    eh = np.zeros((LANES, D), np.float32)
    for h in range(NHEAD):
        eh[h, h * HDIM:(h + 1) * HDIM] = 1.0
    return jnp.asarray(eh, BF16), jnp.asarray(eh.T.copy(), BF16)


def _resident(shape):
    return pl.BlockSpec(shape, lambda *_: (0,) * len(shape), pipeline_mode=pl.Buffered(1))


def _col_chunks(n, step):
    return [(c, min(c + step, n)) for c in range(0, n, step)]


def norm_matmul(x, nw, w, name):
    t_len = x.shape[0]
    blocked = w.ndim == 3
    n_len = w.shape[0] * w.shape[2] if blocked else w.shape[1]
    tm = min(512, t_len)
    chunks = _col_chunks(n_len, n_len // N_DEV if blocked else 1536)

    def body(x_ref, nw_ref, w_ref, o_ref, h_ref):
        xv = x_ref[...]
        r = lax.rsqrt(jnp.mean(xv * xv, axis=-1, keepdims=True) + EPS)
        hv = (xv * r * nw_ref[...]).astype(BF16)
        h_ref[...] = hv
        for j, (c0, c1) in enumerate(chunks):
            wj = w_ref[j] if blocked else w_ref[:, c0:c1]
            o_ref[:, c0:c1] = _nn(hv, wj).astype(o_ref.dtype)

    return pl.pallas_call(
        body, grid=(t_len // tm,),
        in_specs=[pl.BlockSpec((tm, D), lambda i: (i, 0)), _resident((1, D)), _resident(w.shape)],
        out_specs=[pl.BlockSpec((tm, n_len), lambda i: (i, 0)),
                   pl.BlockSpec((tm, D), lambda i: (i, 0))],
        out_shape=[SDS((t_len, n_len), BF16), SDS((t_len, D), BF16)],
        compiler_params=_cparams(("parallel",)), name=name)(x, nw, w)


def matmul_residual(a, w, res, relu2, name):
    t_len, k_len = a.shape
    tm = min(512, t_len)

    def body(a_ref, w_ref, res_ref, o_ref):
        av = a_ref[...]
        if relu2:
            af = jnp.maximum(av.astype(F32), 0.0)
            av = (af * af).astype(BF16)
        o_ref[...] = res_ref[...] + _nn(av, w_ref[...])

    return pl.pallas_call(
        body, grid=(t_len // tm,),
        in_specs=[pl.BlockSpec((tm, k_len), lambda i: (i, 0)),
                  _resident((k_len, D)),
                  pl.BlockSpec((tm, D), lambda i: (i, 0))],
        out_specs=pl.BlockSpec((tm, D), lambda i: (i, 0)),
        out_shape=SDS((t_len, D), F32),
        compiler_params=_cparams(("parallel",)), name=name)(a, w, res)


def matmul_nt_act(dy, w, u, name):
    t_len = dy.shape[0]
    n_len = w.shape[0]
    tm = min(512, t_len)
    chunks = _col_chunks(n_len, 1024)

    def body(dy_ref, w_ref, *rest):
        if u is None:
            (o_ref,) = rest
        else:
            u_ref, o_ref = rest
        dyv = dy_ref[...]
        for c0, c1 in chunks:
            p = _nt(dyv, w_ref[c0:c1, :])
            if u is not None:
                p = p * (2.0 * jnp.maximum(u_ref[:, c0:c1].astype(F32), 0.0))
            o_ref[:, c0:c1] = p.astype(o_ref.dtype)

    in_specs = [pl.BlockSpec((tm, D), lambda i: (i, 0)), _resident((n_len, D))]
    args = [dy, w]
    if u is not None:
        in_specs.append(pl.BlockSpec((tm, n_len), lambda i: (i, 0)))
        args.append(u)
    return pl.pallas_call(
        body, grid=(t_len // tm,),
        in_specs=in_specs,
        out_specs=pl.BlockSpec((tm, n_len), lambda i: (i, 0)),
        out_shape=SDS((t_len, n_len), BF16),
        compiler_params=_cparams(("parallel",)), name=name)(*args)


def matmul_tn(a, b, a_spec, b_spec, o_spec, o_shape, n_out, relu2, name, tt_max=2048):
    t_len = a.shape[0]
    tt = min(tt_max, t_len)
    nt = t_len // tt

    def body(a_ref, b_ref, o_ref, acc):
        t = pl.program_id(1)
        av = a_ref[...]
        if relu2:
            af = jnp.maximum(av.astype(F32), 0.0)
            av = (af * af).astype(BF16)
        p = _tn(av, b_ref[...])

        @pl.when(t == 0)
        def _():
            acc[...] = p

        @pl.when(t > 0)
        def _():
            acc[...] += p

        @pl.when(t == nt - 1)
        def _():
            if len(blk) == 3:
                for j in range(blk[0]):
                    o_ref[j] = acc[:, j * blk[2]:(j + 1) * blk[2]].astype(o_ref.dtype)
            else:
                o_ref[...] = acc[...].astype(o_ref.dtype)

    blk = tuple(o_spec.block_shape)
    acc_shape = (blk[1], blk[0] * blk[2]) if len(blk) == 3 else blk
    return pl.pallas_call(
        body, grid=(n_out, nt),
        in_specs=[a_spec(tt), b_spec(tt)],
        out_specs=o_spec, out_shape=o_shape,
        scratch_shapes=[pltpu.VMEM(acc_shape, F32)],
        compiler_params=_cparams(("parallel", "arbitrary")), name=name)(a, b)


def matmul_nt_norm_bwd(dy, w, x, nw, dres, name):
    t_len = x.shape[0]
    blocked = w.ndim == 3
    k_len = dy.shape[1]
    kb = k_len // N_DEV
    tm = min(512, t_len)

    def body(dy_ref, w_ref, x_ref, nw_ref, dres_ref, dx_ref, dxb_ref, dnw_ref):
        @pl.when(pl.program_id(0) == 0)
        def _():
            dnw_ref[...] = jnp.zeros_like(dnw_ref)

        if blocked:
            dh = _nt(dy_ref[:, 0:kb], w_ref[0])
            for j in range(1, N_DEV):
                dh = dh + _nt(dy_ref[:, j * kb:(j + 1) * kb], w_ref[j])
        else:
            dh = _nt(dy_ref[...], w_ref[...])
        xv = x_ref[...]
        r = lax.rsqrt(jnp.mean(xv * xv, axis=-1, keepdims=True) + EPS)
        xh = xv * r
        dnw_ref[0:1, :] += jnp.sum(dh * xh, axis=0, keepdims=True)
        g = dh * nw_ref[...]
        dx = dres_ref[...] + r * (g - xh * jnp.mean(g * xh, axis=-1, keepdims=True))
        dx_ref[...] = dx
        dxb_ref[...] = dx.astype(BF16)

    return pl.pallas_call(
        body, grid=(t_len // tm,),
        in_specs=[pl.BlockSpec((tm, k_len), lambda i: (i, 0)),
                  _resident(w.shape),
                  pl.BlockSpec((tm, D), lambda i: (i, 0)),
                  _resident((1, D)),
                  pl.BlockSpec((tm, D), lambda i: (i, 0))],
        out_specs=[pl.BlockSpec((tm, D), lambda i: (i, 0)),
                   pl.BlockSpec((tm, D), lambda i: (i, 0)),
                   pl.BlockSpec((8, D), lambda i: (0, 0))],
        out_shape=[SDS((t_len, D), F32), SDS((t_len, D), BF16), SDS((8, D), F32)],
        compiler_params=_cparams(("arbitrary",)), name=name)(dy, w, x, nw, dres)


def loss_head(x, fw, tgt):
    t_len = x.shape[0]
    tm = min(512, t_len)

    def body(x_ref, fw_ref, t_ref, loss_ref, dx_ref, dxb_ref, dfw_ref):
        @pl.when(pl.program_id(0) == 0)
        def _():
            loss_ref[...] = jnp.zeros_like(loss_ref)
            dfw_ref[...] = jnp.zeros_like(dfw_ref)
        xv = x_ref[...]
        r = lax.rsqrt(jnp.mean(xv * xv, axis=-1, keepdims=True) + EPS)
        xh = xv * r
        w = fw_ref[...]
        e = xh * w - t_ref[...]
        row = jnp.sum(e * e, axis=-1, keepdims=True) * (1.0 / D)
        loss_ref[...] += 0.5 * jnp.sum(row, axis=0, keepdims=True)
        dyf = e * (1.0 / D)
        dfw_ref[0:1, :] += jnp.sum(dyf * xh, axis=0, keepdims=True)
        g = dyf * w
        dx = r * (g - xh * jnp.mean(g * xh, axis=-1, keepdims=True))
        dx_ref[...] = dx
        dxb_ref[...] = dx.astype(BF16)

    return pl.pallas_call(
        body, grid=(t_len // tm,),
        in_specs=[pl.BlockSpec((tm, D), lambda i: (i, 0)),
                  pl.BlockSpec((1, D), lambda i: (0, 0)),
                  pl.BlockSpec((tm, D), lambda i: (i, 0))],
        out_specs=[pl.BlockSpec((8, LANES), lambda i: (0, 0)),
                   pl.BlockSpec((tm, D), lambda i: (i, 0)),
                   pl.BlockSpec((tm, D), lambda i: (i, 0)),
                   pl.BlockSpec((8, D), lambda i: (0, 0))],
        out_shape=[SDS((8, LANES), F32), SDS((t_len, D), F32), SDS((t_len, D), BF16), SDS((8, D), F32)],
        compiler_params=_cparams(("arbitrary",)), name="loss_head")(x, fw, tgt)


TAP_SHIFTS = (3, 2, 1)


def _shift_matrix(n, up):
    r = lax.broadcasted_iota(jnp.int32, (n, n), 0)
    c = lax.broadcasted_iota(jnp.int32, (n, n), 1)
    return jnp.concatenate([jnp.where(c == (r + j if up else r - j), 1.0, 0.0).astype(BF16) for j in TAP_SHIFTS],
                           axis=0)


def _shifts_dn(xb, halo, sm, n_shifts):
    n = xb.shape[0]
    first = len(TAP_SHIFTS) - n_shifts
    moved = _nn(sm[first * n:], xb)
    row = lax.broadcasted_iota(jnp.int32, halo.shape, 0)
    outs = []
    for k in range(n_shifts):
        j = TAP_SHIFTS[first + k]
        o = moved[k * n:(k + 1) * n]
        top = jnp.where(row < j, pltpu.roll(halo, j, 0), o[0:8])
        outs.append(jnp.concatenate([top, o[8:]], axis=0))
    return outs


def _shifts_up(xb, nxt, sm, n_shifts):
    n = xb.shape[0]
    first = len(TAP_SHIFTS) - n_shifts
    moved = _nn(sm[first * n:], xb)
    row = lax.broadcasted_iota(jnp.int32, nxt.shape, 0)
    outs = []
    for k in range(n_shifts):
        j = TAP_SHIFTS[first + k]
        o = moved[k * n:(k + 1) * n]
        bot = jnp.where(row >= 8 - j, pltpu.roll(nxt, 8 - j, 0), o[n - 8:n])
        outs.append(jnp.concatenate([o[:n - 8], bot], axis=0))
    return outs


def _conv_fwd(x, xb, halo, w_ref, kw, sm):
    shifted = _shifts_dn(xb, halo, sm, kw - 1)
    acc = w_ref[kw - 1:kw, :] * x
    for k in range(kw - 1):
        acc = acc + w_ref[k:k + 1, :] * shifted[k]
    return acc


def _chunk_cumsum(a, pos):
    for sh in (1, 2, 4, 8, 16, 32):
        a = a + jnp.where(pos >= sh, pltpu.roll(a, sh, 0), 0.0)
    return a


def _chunk_rcumsum(a, pos):
    n = a.shape[0]
    for sh in (1, 2, 4, 8, 16, 32):
        a = a + jnp.where(pos < CHUNK - sh, pltpu.roll(a, n - sh, 0), 0.0)
    return a


def _softplus(v):
    return jnp.maximum(v, 0.0) + jnp.log(1.0 + jnp.exp(-jnp.abs(v)))


def _silu(v):
    return v * _sigmoid(v)


def _dsilu(v):
    s = _sigmoid(v)
    return s * (1.0 + v * (1.0 - s))


def _lane_masks(width=D):
    lane = lax.broadcasted_iota(jnp.int32, (CHUNK, width), 1) & (HDIM - 1)
    row = lax.broadcasted_iota(jnp.int32, (CHUNK, width), 0)
    return lane == row, lane <= row


def _rep_matrix():
    lane = lax.broadcasted_iota(jnp.int32, (CHUNK, 512), 1) & (HDIM - 1)
    row = lax.broadcasted_iota(jnp.int32, (CHUNK, 512), 0)
    return jnp.where(lane == row, 1.0, 0.0).astype(BF16)


def _blockdiag(xp):
    lane = lax.broadcasted_iota(jnp.int32, xp.shape, 1)
    zero = jnp.zeros_like(xp)
    return jnp.concatenate([jnp.where(lane < HDIM, xp, zero), jnp.where(lane >= HDIM, xp, zero)], axis=0)


def _mixer_views(tt):
    r8 = tt // 8

    def main(width, col):
        return pl.BlockSpec((tt, width), lambda i, c=col // width: (i, c))

    def halo(width, col):
        return pl.BlockSpec((8, width), lambda i, c=col // width: (jnp.maximum(i * r8 - 1, 0), c))

    return main, halo


def mixer_fwd(proj, prm, tt):
    t_len = proj.shape[0]
    nblk = t_len // tt
    nc = tt // CHUNK
    main, halo = _mixer_views(tt)

    def body(ub_ref, uc_ref, uh_ref, z_ref, xr_ref, bcr_ref, dtr_ref, uch_ref, uhh_ref, xrh_ref, bcrh_ref,
             scw_ref, cwx_ref, cwbc_ref, cbx_ref, cbbc_ref, dtb_ref, alog_ref, dsk_ref, nrm_ref, eh_ref,
             y_ref, st_ref, aux_ref, hs, xs_s, bc_s, dtx_s, cumx_s, yssd_s):
        i = pl.program_id(0)
        first = i == 0

        @pl.when(first)
        def _():
            hs[...] = jnp.zeros_like(hs)

        keep = jnp.where(first, 0.0, 1.0)
        sm = _shift_matrix(tt, False)
        v = uc_ref[...].astype(F32) * uh_ref[...].astype(F32)
        vh = uch_ref[...].astype(F32) * uhh_ref[...].astype(F32) * keep
        cv = _conv_fwd(v, v.astype(BF16), vh, scw_ref, 3, sm)
        aux_ref[:, A_CV:A_CV + D] = cv.astype(BF16)
        y_ref[:, 0:D] = (ub_ref[...].astype(F32) * cv).astype(BF16)

        xrb = xr_ref[...]
        pre_x = _conv_fwd(xrb.astype(F32), xrb, xrh_ref[...].astype(F32) * keep, cwx_ref, 4, sm) + cbx_ref[...]
        aux_ref[:, A_PX:A_PX + D] = pre_x.astype(BF16)
        xs_s[...] = _silu(pre_x)
        bcrb = bcr_ref[...]
        pre_bc = _conv_fwd(bcrb.astype(F32), bcrb, bcrh_ref[...].astype(F32) * keep, cwbc_ref, 4, sm) + cbbc_ref[...]
        aux_ref[:, A_PBC:A_PBC + 512] = pre_bc.astype(BF16)
        bc_s[...] = _silu(pre_bc)
        dt = _softplus(dtr_ref[...].astype(F32) + dtb_ref[...])
        a_neg = -jnp.exp(alog_ref[...])
        pos = lax.broadcasted_iota(jnp.int32, (tt, LANES), 0) & (CHUNK - 1)
        cum = _chunk_cumsum(dt * a_neg, pos)
        eh = eh_ref[...]
        dtx_s[...] = _expand(dt, eh)
        cumx_s[...] = _expand(cum, eh)
        irep, causal = _lane_masks()
        rep = _rep_matrix()

        def chunk(c, carry):
            r0 = pl.multiple_of(c * CHUNK, CHUNK)
            rows = pl.ds(r0, CHUNK)
            cumx = cumx_s[rows, :]
            cum_l = cumx[CHUNK - 1:CHUNK, :]
            xd = xs_s[rows, :] * dtx_s[rows, :]
            xf = xd * jnp.exp(cum_l - cumx)
            ex = jnp.exp(cumx)
            e_l = jnp.exp(cum_l)
            rvec = jnp.sum(jnp.where(irep, cumx, 0.0), axis=0, keepdims=True)
            lam = jnp.exp(jnp.where(causal, cumx - rvec, NEG_BIG))
            bc = bc_s[rows, :]
            for g in range(2):
                gs = slice(g * 512, (g + 1) * 512)
                bg = bc[:, g * NSTATE:(g + 1) * NSTATE].astype(BF16)
                cg = bc[:, 256 + g * NSTATE:256 + (g + 1) * NSTATE].astype(BF16)
                s_rep = _nn(_nt(cg, bg).astype(BF16), rep)
                m_g = (s_rep * lam[:, gs]).astype(BF16)
                h_g = hs[:, gs]
                h_b = h_g.astype(BF16)
                st_ref[c, :, gs] = h_b
                yo = _nn(cg, h_b) * ex[:, gs]
                xd_b = xd[:, gs].astype(BF16)
                for hp in range(4):
                    ps = slice(hp * LANES, (hp + 1) * LANES)
                    yd = _nn(m_g[:, ps], _blockdiag(xd_b[:, ps]))
                    yssd_s[rows, g * 512 + hp * LANES:g * 512 + (hp + 1) * LANES] = yd + yo[:, ps]
                hs[:, gs] = h_g * e_l[:, gs] + _tn(bg, xf[:, gs].astype(BF16))
            return carry

        lax.fori_loop(0, nc, chunk, 0, unroll=True)

        for r0 in range(0, tt, CHUNK):
            rows = slice(r0, r0 + CHUNK)
            ys = yssd_s[rows, :] + dsk_ref[...] * xs_s[rows, :]
            aux_ref[rows, A_YS:A_YS + D] = ys.astype(BF16)
            gt = ys * _silu(z_ref[rows, :].astype(F32))
            for g in range(2):
                gs = slice(g * 512, (g + 1) * 512)
                gg = gt[:, gs]
                rn = lax.rsqrt(jnp.mean(gg * gg, axis=-1, keepdims=True) + EPS)
                y_ref[rows, D + g * 512:D + (g + 1) * 512] = (gg * rn * nrm_ref[:, gs]).astype(BF16)

    params = [prm[k] for k in ("scw", "cwx", "cwbc", "cbx", "cbbc", "dtb", "alog", "dskx", "nrm", "eh")]
    in_specs = [main(D, C_UB), main(D, C_UC), main(D, C_UH), main(D, C_Z), main(D, C_XS), main(512, C_BC),
                main(LANES, C_DT), halo(D, C_UC), halo(D, C_UH), halo(D, C_XS), halo(512, C_BC)]
    in_specs += [_param_spec(a, prm["layer"]) for a in params]
    return pl.pallas_call(
        body, grid=(nblk,),
        in_specs=in_specs,
        out_specs=[pl.BlockSpec((tt, MIX), lambda i: (i, 0)),
                   pl.BlockSpec((nc, NSTATE, D), lambda i: (i, 0, 0)),
                   pl.BlockSpec((tt, AUX_W), lambda i: (i, 0))],
        out_shape=[SDS((t_len, MIX), BF16), SDS((t_len // CHUNK, NSTATE, D), BF16), SDS((t_len, AUX_W), BF16)],
        scratch_shapes=[pltpu.VMEM((NSTATE, D), F32), pltpu.VMEM((tt, D), F32), pltpu.VMEM((tt, 512), F32),
                        pltpu.VMEM((tt, D), F32), pltpu.VMEM((tt, D), F32), pltpu.VMEM((tt, D), F32)],
        compiler_params=_cparams(("arbitrary",)), name="mixer_fwd")(*([proj] * 11), *params)


def mixer_bwd(proj, dy, states, aux, prm, tt):
    t_len = proj.shape[0]
    nblk = t_len // tt
    nc = tt // CHUNK

    def rev(i):
        return nblk - 1 - i

    def main(width, col):
        return pl.BlockSpec((tt, width), lambda i, c=col // width: (rev(i), c))

    def body(ub_ref, uc_ref, uh_ref, z_ref, xr_ref, bcr_ref, dtr_ref, dy_ref, st_ref, aux_ref,
             scw_ref, cwx_ref, cwbc_ref, cbx_ref, cbbc_ref, dtb_ref, alog_ref, dsk_ref, nrm_ref, eh_ref, eht_ref,
             dp_ref, gscw_ref, gcwx_ref, gcwbc_ref, gvec_ref, gdt_ref,
             dhs, xs_s, bc_s, dtx_s, cumx_s, dys_s, dxs_s, dbc_s, red_s, ddtx_s, nx_cv, nx_px, nx_pbc, sgx_s, sgbc_s):
        i = pl.program_id(0)

        @pl.when(i == 0)
        def _():
            dhs[...] = jnp.zeros_like(dhs)
            nx_cv[...] = jnp.zeros_like(nx_cv)
            nx_px[...] = jnp.zeros_like(nx_px)
            nx_pbc[...] = jnp.zeros_like(nx_pbc)
            gscw_ref[...] = jnp.zeros_like(gscw_ref)
            gcwx_ref[...] = jnp.zeros_like(gcwx_ref)
            gcwbc_ref[...] = jnp.zeros_like(gcwbc_ref)
            gvec_ref[...] = jnp.zeros_like(gvec_ref)
            gdt_ref[...] = jnp.zeros_like(gdt_ref)

        uc = uc_ref[...].astype(F32)
        uh = uh_ref[...].astype(F32)
        v = uc * uh
        dya = dy_ref[:, 0:D].astype(F32)
        dp_ref[:, C_UB:C_UB + D] = (dya * aux_ref[:, A_CV:A_CV + D].astype(F32)).astype(BF16)
        dcv = dya * ub_ref[...].astype(F32)
        sm = _shift_matrix(tt, True)
        ups = _shifts_up(dcv.astype(BF16), nx_cv[...], sm, 2) + [dcv]
        dv = None
        for k in range(3):
            gscw_ref[k:k + 1, :] += jnp.sum(v * ups[k], axis=0, keepdims=True)
            term = scw_ref[k:k + 1, :] * ups[k]
            dv = term if dv is None else dv + term
        nx_cv[...] = dcv[0:8]
        dp_ref[:, C_UC:C_UC + D] = (dv * uh).astype(BF16)
        dp_ref[:, C_UH:C_UH + D] = (dv * uc).astype(BF16)

        pre_x = aux_ref[:, A_PX:A_PX + D].astype(F32)
        pre_bc = aux_ref[:, A_PBC:A_PBC + 512].astype(F32)
        sg_x = _sigmoid(pre_x)
        sg_bc = _sigmoid(pre_bc)
        sgx_s[...] = sg_x
        sgbc_s[...] = sg_bc
        xs = pre_x * sg_x
        xs_s[...] = xs
        bc_s[...] = pre_bc * sg_bc
        dt_pre = dtr_ref[...].astype(F32) + dtb_ref[...]
        dt = _softplus(dt_pre)
        a_neg = -jnp.exp(alog_ref[...])
        pos = lax.broadcasted_iota(jnp.int32, (tt, LANES), 0) & (CHUNK - 1)
        cum = _chunk_cumsum(dt * a_neg, pos)
        eh = eh_ref[...]
        eht = eht_ref[...]
        dtx_s[...] = _expand(dt, eh)
        cumx_s[...] = _expand(cum, eh)

        irep, causal = _lane_masks()
        irep_g, _ = _lane_masks(512)
        rep = _rep_matrix()
        row8 = lax.broadcasted_iota(jnp.int32, (8, 512), 0)
        lane128 = lax.broadcasted_iota(jnp.int32, (CHUNK, LANES), 1)

        z = z_ref[...].astype(F32)
        sg_z = _sigmoid(z)
        sz = z * sg_z
        dsz = sg_z * (1.0 + z * (1.0 - sg_z))
        ys = aux_ref[:, A_YS:A_YS + D].astype(F32)
        gt = ys * sz
        dyb = dy_ref[:, D:MIX].astype(F32)
        for g in range(2):
            gs = slice(g * 512, (g + 1) * 512)
            gg = gt[:, gs]
            rn = lax.rsqrt(jnp.mean(gg * gg, axis=-1, keepdims=True) + EPS)
            gvec_ref[0:1, gs] += jnp.sum(dyb[:, gs] * gg * rn, axis=0, keepdims=True)
            dgn = dyb[:, gs] * nrm_ref[:, gs]
            dgt = rn * (dgn - gg * (rn * rn) * jnp.mean(dgn * gg, axis=-1, keepdims=True))
            dys = dgt * sz[:, gs]
            dys_s[:, gs] = dys
            dp_ref[:, C_Z + g * 512:C_Z + (g + 1) * 512] = (dgt * ys[:, gs] * dsz[:, gs]).astype(BF16)
        dys_all = dys_s[...]
        gvec_ref[1:2, :] += jnp.sum(dys_all * xs, axis=0, keepdims=True)

        def bwd_chunk(cc, carry):
            c = nc - 1 - cc
            r0 = pl.multiple_of(c * CHUNK, CHUNK)
            rows = pl.ds(r0, CHUNK)
            cumx = cumx_s[rows, :]
            cum_l = cumx[CHUNK - 1:CHUNK, :]
            xs_c = xs_s[rows, :]
            dtx = dtx_s[rows, :]
            xd = xs_c * dtx
            f = jnp.exp(cum_l - cumx)
            xf = xd * f
            ex = jnp.exp(cumx)
            e_l = jnp.exp(cum_l)
            rvec = jnp.sum(jnp.where(irep, cumx, 0.0), axis=0, keepdims=True)
            lam = jnp.exp(jnp.where(causal, cumx - rvec, NEG_BIG))
            bc = bc_s[rows, :]
            dyc = dys_s[rows, :]
            for g in range(2):
                gs = slice(g * 512, (g + 1) * 512)
                bg = bc[:, g * NSTATE:(g + 1) * NSTATE].astype(BF16)
                cg = bc[:, 256 + g * NSTATE:256 + (g + 1) * NSTATE].astype(BF16)
                h0 = st_ref[c, :, gs]
                dh = dhs[:, gs]
                dh_b = dh.astype(BF16)
                xf_g = xf[:, gs]
                dxf = _nn(bg, dh_b)
                db = _nt(xf_g.astype(BF16), dh_b)
                s_rep = _nn(_nt(cg, bg).astype(BF16), rep)
                lam_g = lam[:, gs]
                m_g = s_rep * lam_g
                m_b = m_g.astype(BF16)
                ex_g = ex[:, gs]
                dy_g = dyc[:, gs]
                yo = _nn(cg, h0) * ex_g
                dg_b = (dy_g * ex_g).astype(BF16)
                dc = _nt(dg_b, h0)
                el_g = e_l[:, gs]
                dee = jnp.sum(dh * h0.astype(F32), axis=0, keepdims=True) * el_g
                dhs[:, gs] = dh * el_g + _tn(cg, dg_b)
                xd_b = xd[:, gs].astype(BF16)
                dy_b = dy_g.astype(BF16)
                dm_parts, dxd_parts = [], []
                for hp in range(4):
                    ps = slice(hp * LANES, (hp + 1) * LANES)
                    bd = _blockdiag(xd_b[:, ps])
                    dm_parts.append(_nt(dy_b[:, ps], bd))
                    t2 = _tn(m_b[:, ps], dy_b[:, ps])
                    dxd_parts.append(jnp.where(lane128 < HDIM, t2[0:CHUNK], t2[CHUNK:2 * CHUNK]))
                dm = jnp.concatenate(dm_parts, axis=1)
                dxd = jnp.concatenate(dxd_parts, axis=1) + dxf * f[:, gs]
                dseg = dm * m_g
                ds_b = _nt((dm * lam_g).astype(BF16), rep).astype(BF16)
                dc = dc + _nn(ds_b, bg)
                db = db + _tn(ds_b, cg)
                colsum = jnp.sum(dseg, axis=0, keepdims=True)
                dxfxf = dxf * xf_g
                red = dseg - jnp.where(irep_g, colsum, 0.0) + dy_g * yo - dxfxf
                last = jnp.sum(dxfxf, axis=0, keepdims=True) + dee
                red_s[rows, gs] = red
                tail = pl.ds(pl.multiple_of(r0 + CHUNK - 8, 8), 8)
                red_s[tail, gs] += jnp.where(row8 == 7, last, 0.0)
                ddtx_s[rows, gs] = dxd * xs_c[:, gs]
                dxs_s[rows, gs] = dxd * dtx[:, gs] + dsk_ref[:, gs] * dy_g
                dbc_s[rows, g * NSTATE:(g + 1) * NSTATE] = db
                dbc_s[rows, 256 + g * NSTATE:256 + (g + 1) * NSTATE] = dc
            return carry

        lax.fori_loop(0, nc, bwd_chunk, 0, unroll=True)

        dcum = _head_reduce(red_s[...], eht)
        da = _chunk_rcumsum(dcum, pos)
        ddt = _head_reduce(ddtx_s[...], eht) + da * a_neg
        gdt_ref[1:2, :] += jnp.sum(da * dt, axis=0, keepdims=True) * a_neg
        ddt_raw = ddt * _sigmoid(dt_pre)
        lane_t = lax.broadcasted_iota(jnp.int32, (tt, LANES), 1)
        ddt_raw = jnp.where(lane_t < NHEAD, ddt_raw, 0.0)
        gdt_ref[0:1, :] += jnp.sum(ddt_raw, axis=0, keepdims=True)
        dp_ref[:, C_DT:C_DT + LANES] = ddt_raw.astype(BF16)

        sg_x = sgx_s[...]
        sg_bc = sgbc_s[...]
        pre_x = aux_ref[:, A_PX:A_PX + D].astype(F32)
        pre_bc = aux_ref[:, A_PBC:A_PBC + 512].astype(F32)
        dpx = dxs_s[...] * (sg_x * (1.0 + pre_x * (1.0 - sg_x)))
        dpbc = dbc_s[...] * (sg_bc * (1.0 + pre_bc * (1.0 - sg_bc)))
        gvec_ref[2:3, :] += jnp.sum(dpx, axis=0, keepdims=True)
        gcwbc_ref[4:5, :] += jnp.sum(dpbc, axis=0, keepdims=True)
        xraw = xr_ref[...].astype(F32)
        bcraw = bcr_ref[...].astype(F32)
        ups_x = _shifts_up(dpx.astype(BF16), nx_px[...], sm, 3) + [dpx]
        ups_bc = _shifts_up(dpbc.astype(BF16), nx_pbc[...], sm, 3) + [dpbc]
        dxr, dbcr = None, None
        for k in range(4):
            up_x = ups_x[k]
            up_bc = ups_bc[k]
            gcwx_ref[k:k + 1, :] += jnp.sum(xraw * up_x, axis=0, keepdims=True)
            gcwbc_ref[k:k + 1, :] += jnp.sum(bcraw * up_bc, axis=0, keepdims=True)
            tx = cwx_ref[k:k + 1, :] * up_x
            tb = cwbc_ref[k:k + 1, :] * up_bc
            dxr = tx if dxr is None else dxr + tx
            dbcr = tb if dbcr is None else dbcr + tb
        nx_px[...] = dpx[0:8]
        nx_pbc[...] = dpbc[0:8]
        dp_ref[:, C_XS:C_XS + D] = dxr.astype(BF16)
        dp_ref[:, C_BC:C_BC + 512] = dbcr.astype(BF16)

        @pl.when(i == nblk - 1)
        def _():
            gdt_ref[2:3, :] = _head_reduce(gvec_ref[1:2, :] * jnp.ones((8, 1), F32), eht)[0:1, :]

    def const(shape):
        return pl.BlockSpec(shape, lambda i: (0, 0))

    params = [prm[k] for k in ("scw", "cwx", "cwbc", "cbx", "cbbc", "dtb", "alog", "dskx", "nrm", "eh", "eht")]
    in_specs = [main(D, C_UB), main(D, C_UC), main(D, C_UH), main(D, C_Z), main(D, C_XS), main(512, C_BC),
                main(LANES, C_DT),
                pl.BlockSpec((tt, MIX), lambda i: (rev(i), 0)),
                pl.BlockSpec((nc, NSTATE, D), lambda i: (rev(i), 0, 0)),
                pl.BlockSpec((tt, AUX_W), lambda i: (rev(i), 0))]
    in_specs += [_param_spec(a, prm["layer"]) for a in params]
    return pl.pallas_call(
        body, grid=(nblk,),
        in_specs=in_specs,
        out_specs=[pl.BlockSpec((tt, NINP), lambda i: (rev(i), 0)),
                   const((8, D)), const((8, D)), const((8, 512)), const((8, D)), const((8, LANES))],
        out_shape=[SDS((t_len, NINP), BF16), SDS((8, D), F32), SDS((8, D), F32), SDS((8, 512), F32),
                   SDS((8, D), F32), SDS((8, LANES), F32)],
        scratch_shapes=[pltpu.VMEM((NSTATE, D), F32),
                        pltpu.VMEM((tt, D), F32), pltpu.VMEM((tt, 512), F32),
                        pltpu.VMEM((tt, D), F32), pltpu.VMEM((tt, D), F32),
                        pltpu.VMEM((tt, D), F32), pltpu.VMEM((tt, D), F32),
                        pltpu.VMEM((tt, 512), F32),
                        pltpu.VMEM((tt, D), F32), pltpu.VMEM((tt, D), F32),
                        pltpu.VMEM((8, D), F32), pltpu.VMEM((8, D), F32), pltpu.VMEM((8, 512), F32),
                        pltpu.VMEM((tt, D), F32), pltpu.VMEM((tt, 512), F32)],
        compiler_params=_cparams(("arbitrary",)), name="mixer_bwd")(
            *([proj] * 7), dy, states, aux, *params)


TN_IN = 1152
DW_TOKENS = 4096


def layer_fwd_mix(x, lw, prm, tt):
    proj, h1 = norm_matmul(x, lw["nw1"], lw["win"], "in_proj")
    y, st, aux = mixer_fwd(proj, prm, tt)
    return h1, proj, (st, aux), y


def layer_fwd_mlp(x, mixed, lw):
    h1, proj, st, y = mixed
    x1 = matmul_residual(y, lw["wout"], x, False, "out_proj")
    u, h2 = norm_matmul(x1, lw["nw2"], lw["wup"], "up_proj")
    x2 = matmul_residual(u, lw["wdn"], x1, True, "down_proj")
    return x2, (x, h1, proj, st, y, x1, h2, u)


def layer_fwd(x, lw, prm, tt):
    return layer_fwd_mlp(x, layer_fwd_mix(x, lw, prm, tt), lw)


def _dw(a, b, a_cols, b_cols, relu2, name, tt_max=2048):
    m_len, n_len = a.shape[1], b.shape[1]
    n_a, n_b = m_len // a_cols, n_len // b_cols
    assert n_a == 1 or n_b == 1
    if n_b == 1:
        return matmul_tn(
            a, b,
            lambda t_: pl.BlockSpec((t_, a_cols), lambda n, t: (t, n)),
            lambda t_: pl.BlockSpec((t_, n_len), lambda n, t: (t, 0)),
            pl.BlockSpec((a_cols, n_len), lambda n, t: (n, 0)), SDS((m_len, n_len), BF16), n_a, relu2, name, tt_max)
    return matmul_tn(
        a, b,
        lambda t_: pl.BlockSpec((t_, m_len), lambda n, t: (t, 0)),
        lambda t_: pl.BlockSpec((t_, b_cols), lambda n, t: (t, n)),
        pl.BlockSpec((m_len, b_cols), lambda n, t: (0, n)), SDS((m_len, n_len), BF16), n_b, relu2, name, tt_max)


def layer_bwd_mlp(dx2, dx2b, lw, saved):
    _, _, _, _, y, x1, h2, u = saved
    du = matmul_nt_act(dx2b, lw["wdn"], u, "mlp_bwd_du")
    g_wdn = _dw(u, dx2b, 1024, D, True, "dw_down")
    dx1, dx1b, g_nw2 = matmul_nt_norm_bwd(du, lw["wup"], x1, lw["nw2"], dx2, "mlp_bwd_dx")
    cb = DFF // N_DEV
    g_wup = matmul_tn(
        h2, du,
        lambda t_: pl.BlockSpec((t_, D), lambda n, t: (t, 0)),
        lambda t_: pl.BlockSpec((t_, 2 * cb), lambda n, t: (t, n)),
        pl.BlockSpec((2, D, cb), lambda n, t: (n, 0, 0)), SDS((N_DEV, D, cb), BF16), N_DEV // 2, False, "dw_up",
        DW_TOKENS)
    dy = matmul_nt_act(dx1b, lw["wout"], None, "out_bwd_dy")
    g_wout = _dw(y, dx1b, 1024, D, False, "dw_out", DW_TOKENS)
    return dx1, dx1b, dy, {"wout": g_wout, "wup": g_wup, "wdn": g_wdn, "nw2": g_nw2[0]}


def layer_bwd_mix(dx1, dy, lw, prm, saved, tt):
    x, h1, proj, st = saved[:4]
    dproj, gscw, gcwx, gcwbc, gvec, gdt = mixer_bwd(proj, dy, st[0], st[1], prm, tt)
    dx0, dx0b, g_nw1 = matmul_nt_norm_bwd(dproj, lw["win"], x, lw["nw1"], dx1, "in_bwd_dx")
    g_win = _dw(h1, dproj, D, TN_IN, False, "dw_in", DW_TOKENS)
    grads = {
        "win": g_win, "scw": gscw[0:3], "cw": jnp.concatenate([gcwx[0:4], gcwbc[0:4]], axis=1),
        "cb": jnp.concatenate([gvec[2], gcwbc[4]], axis=0),
        "dtb": gdt[0, :NHEAD], "alog": gdt[1, :NHEAD], "dsk": gdt[2, :NHEAD],
        "nrm": gvec[0], "nw1": g_nw1[0],
    }
    return dx0, dx0b, grads


def layer_bwd(dx2, dx2b, lw, prm, saved, tt):
    dx1, dx1b, dy, g_mlp = layer_bwd_mlp(dx2, dx2b, lw, saved)
    dx0, dx0b, g_mix = layer_bwd_mix(dx1, dy, lw, prm, saved, tt)
    return dx0, dx0b, {**g_mlp, **g_mix}


def stacked_params(conv_b, dt_bias, a_log, d_skip, ssd_norm_w):
    def lanes128(a):
        return jnp.pad(a, ((0, 0), (0, LANES - a.shape[1])))[:, None, :]

    return {"cbx": conv_b[:, None, :D], "cbbc": conv_b[:, None, D:], "dtb": lanes128(dt_bias),
            "alog": lanes128(a_log), "dskx": jnp.repeat(d_skip, HDIM, axis=1)[:, None, :],
            "nrm": ssd_norm_w[:, None, :]}


def layer_params(layer, win, scw, cw, nw1, nw2, stacked, eh, eht):
    def rows8(a):
        return jnp.pad(a, ((0, 8 - a.shape[0]), (0, 0)))

    lw = {"win": win, "nw1": nw1[None, :], "nw2": nw2[None, :]}
    prm = dict(stacked, layer=layer, scw=rows8(scw), cwx=rows8(cw[:, :D]), cwbc=rows8(cw[:, D:]), eh=eh, eht=eht)
    return lw, prm


def _param_spec(arr, layer):
    if arr.ndim == 3:
        return pl.BlockSpec((None,) + arr.shape[1:], lambda i: (layer, 0, 0))
    return pl.BlockSpec(arr.shape, lambda i: (0, 0))


def _flip(v, bit):
    return 1 - v if bit else v


def all_gather(arrs, name):
    n = len(arrs)

    def body(*refs):
        ins, outs = refs[:n], refs[n:2 * n]
        send_sems, recv_sems, local_sems = refs[2 * n:]
        x, y, c = lax.axis_index("x"), lax.axis_index("y"), lax.axis_index("c")
        sibling = (x, y, 1 - c)
        chips = [(1 - x, y), (x, 1 - y), (1 - x, 1 - y)]

        def idx(px, py, pc):
            return 4 * px + 2 * py + pc

        def copy(a, k, block, to, src=None):
            dst = outs[a].at[idx(*block)]
            return pltpu.make_async_remote_copy(
                src_ref=dst if src is None else src, dst_ref=dst,
                send_sem=send_sems.at[a, k], recv_sem=recv_sems.at[a, k], device_id=to, device_id_type=MESH)

        me = (x, y, c)
        mine = [pltpu.make_async_copy(ins[a], outs[a].at[idx(*me)], local_sems.at[a]) for a in range(n)]
        for cp in mine:
            cp.start()
        first = []
        for a in range(n):
            first.append(copy(a, 0, me, sibling, src=ins[a]))
            first += [copy(a, 1 + j, me, (*chip, c), src=ins[a]) for j, chip in enumerate(chips)]
        for cp in first:
            cp.start()
        passed = []
        for j, chip in enumerate(chips):
            for a in range(n):
                copy(a, 1 + j, (*chip, c), me).wait_recv()
                cp = copy(a, 4 + j, (*chip, c), sibling)
                cp.start()
                passed.append(cp)
        for a in range(n):
            copy(a, 0, sibling, me).wait_recv()
            for j, chip in enumerate(chips):
                copy(a, 4 + j, (*chip, 1 - c), me).wait_recv()
        for cp in first + passed:
            cp.wait_send()
        for cp in mine:
            cp.wait()

    any_spec = pl.BlockSpec(memory_space=pl.ANY)
    return pl.pallas_call(
        body, in_specs=[any_spec] * n, out_specs=[any_spec] * n,
        out_shape=[SDS((N_DEV,) + a.shape, a.dtype) for a in arrs],
        scratch_shapes=[pltpu.SemaphoreType.DMA((n, 7)), pltpu.SemaphoreType.DMA((n, 7)),
                        pltpu.SemaphoreType.DMA((n,))],
        name=name)(*arrs)


HBM_SPEC = pl.BlockSpec(memory_space=pltpu.HBM)
SEM_SPEC = pl.BlockSpec(memory_space=pltpu.SEMAPHORE)
SIDE_EFFECT = pltpu.SideEffectType.DATAFLOW_SIDE_EFFECTING
N_PEER = N_DEV - 1


def _peer(mask):
    x, y, c = lax.axis_index("x"), lax.axis_index("y"), lax.axis_index("c")
    return _flip(x, mask & 4), _flip(y, mask & 2), _flip(c, mask & 1)


ALL_PEERS = tuple(range(1, N_DEV))
SIBLING_AND_CHIPS = (1, 2, 4, 6)


def exchange_start(srcs, per_peer, name, after=None, masks=ALL_PEERS):
    n = len(srcs)
    npeer = len(masks)
    lands = [SDS((N_DEV,) + (a.shape[1:] if per_peer else a.shape), a.dtype) for a in srcs]
    n_in = 2 * n + (after is not None)

    def body(*refs):
        src_refs, land_refs = refs[:n], refs[n:2 * n]
        send_sems, recv_sems = refs[n_in], refs[n_in + 1]
        token = refs[-1]
        x, y, c = lax.axis_index("x"), lax.axis_index("y"), lax.axis_index("c")
        me = 4 * x + 2 * y + c
        for a in range(n):
            for k, mask in enumerate(masks):
                px, py, pc = _peer(mask)
                part = src_refs[a].at[4 * px + 2 * py + pc] if per_peer else src_refs[a]
                pltpu.make_async_remote_copy(
                    src_ref=part, dst_ref=land_refs[a].at[me], send_sem=send_sems.at[a * npeer + k],
                    recv_sem=recv_sems.at[a * npeer + k], device_id=(px, py, pc), device_id_type=MESH).start()
        token[...] = jnp.zeros_like(token)

    out = pl.pallas_call(
        body, name=name,
        out_shape=(pltpu.SemaphoreType.DMA((n * npeer,)), pltpu.SemaphoreType.DMA((n * npeer,)),
                   *[pltpu.HBM(a.shape, a.dtype) for a in srcs], *[pltpu.HBM(l.shape, l.dtype) for l in lands],
                   SDS((8, LANES), F32)),
        in_specs=(HBM_SPEC,) * (2 * n) + ((pl.BlockSpec(memory_space=pl.ANY),) if after is not None else ()),
        out_specs=(SEM_SPEC, SEM_SPEC) + (HBM_SPEC,) * (2 * n) + (pl.BlockSpec(memory_space=pltpu.VMEM),),
        input_output_aliases={k: 2 + k for k in range(2 * n)},
        compiler_params=pltpu.CompilerParams(has_side_effects=SIDE_EFFECT),
    )(*[pltpu.with_memory_space_constraint(a, pltpu.HBM) for a in srcs],
      *[pltpu.with_memory_space_constraint(lax.empty(l.shape, l.dtype), pltpu.HBM) for l in lands],
      *([after] if after is not None else []))
    return out[0], out[1], list(out[2:2 + n]), list(out[2 + n:2 + 2 * n]), out[-1]


def exchange_wait(started, after, per_peer, name, masks=ALL_PEERS):
    send_sems, recv_sems, srcs, lands, _ = started
    n = len(srcs)
    npeer = len(masks)

    def body(*refs):
        src_refs, land_refs = refs[:n], refs[n:2 * n]
        send_sems, recv_sems = refs[2 * n], refs[2 * n + 1]
        for k, mask in enumerate(masks):
            for a in range(n):
                copy = pltpu.make_async_remote_copy(
                    src_ref=src_refs[a].at[0] if per_peer else src_refs[a], dst_ref=land_refs[a].at[0],
                    send_sem=send_sems.at[a * npeer + k], recv_sem=recv_sems.at[a * npeer + k],
                    device_id=_peer(mask), device_id_type=MESH)
                copy.wait_send()
                copy.wait_recv()

    out = pl.pallas_call(
        body, name=name,
        out_shape=tuple(pltpu.HBM(a.shape, a.dtype) for a in srcs + lands),
        in_specs=(HBM_SPEC,) * (2 * n) + (SEM_SPEC, SEM_SPEC, pl.BlockSpec(memory_space=pl.ANY)),
        out_specs=(HBM_SPEC,) * (2 * n), input_output_aliases={k: k for k in range(2 * n)},
        compiler_params=pltpu.CompilerParams(has_side_effects=SIDE_EFFECT),
    )(*srcs, *lands, send_sems, recv_sems, after)
    return list(out[:n]), list(out[n:])


def relay_to_sibling(lands, name):
    n = len(lands)
    chips = (2, 4, 6)

    def body(*refs):
        land_refs = refs[n:2 * n]
        send_sems, recv_sems = refs[2 * n], refs[2 * n + 1]
        x, y, c = lax.axis_index("x"), lax.axis_index("y"), lax.axis_index("c")
        copies = []
        for a in range(n):
            for k, mask in enumerate(chips):
                px, py, _ = _peer(mask)
                block = land_refs[a].at[4 * px + 2 * py + c]
                cp = pltpu.make_async_remote_copy(
                    src_ref=block, dst_ref=block, send_sem=send_sems.at[a * 3 + k], recv_sem=recv_sems.at[a * 3 + k],
                    device_id=(x, y, 1 - c), device_id_type=MESH)
                cp.start()
                copies.append((cp, a, k, land_refs[a].at[4 * px + 2 * py + 1 - c]))
        for cp, a, k, arriving in copies:
            cp.wait_send()
            pltpu.make_async_remote_copy(
                src_ref=arriving, dst_ref=arriving, send_sem=send_sems.at[a * 3 + k], recv_sem=recv_sems.at[a * 3 + k],
                device_id=(x, y, 1 - c), device_id_type=MESH).wait_recv()

    any_spec = pl.BlockSpec(memory_space=pl.ANY)
    return list(pl.pallas_call(
        body, in_specs=[any_spec] * n, out_specs=[any_spec] * n,
        out_shape=[SDS(a.shape, a.dtype) for a in lands],
        input_output_aliases={k: k for k in range(n)},
        scratch_shapes=[pltpu.SemaphoreType.DMA((n * 3,)), pltpu.SemaphoreType.DMA((n * 3,))],
        name=name)(*lands))


def relay_start(lands, name):
    n = len(lands)

    def body(*refs):
        land_refs = refs[:n]
        send_sems, recv_sems = refs[n], refs[n + 1]
        token = refs[-1]
        x, y, c = lax.axis_index("x"), lax.axis_index("y"), lax.axis_index("c")
        for a in range(n):
            for k, mask in enumerate((2, 4, 6)):
                px, py, _ = _peer(mask)
                block = land_refs[a].at[4 * px + 2 * py + c]
                pltpu.make_async_remote_copy(
                    src_ref=block, dst_ref=block, send_sem=send_sems.at[a * 3 + k], recv_sem=recv_sems.at[a * 3 + k],
                    device_id=(x, y, 1 - c), device_id_type=MESH).start()
        token[...] = jnp.zeros_like(token)

    out = pl.pallas_call(
        body, name=name,
        out_shape=(pltpu.SemaphoreType.DMA((n * 3,)), pltpu.SemaphoreType.DMA((n * 3,)),
                   *[pltpu.HBM(a.shape, a.dtype) for a in lands], SDS((8, LANES), F32)),
        in_specs=(HBM_SPEC,) * n,
        out_specs=(SEM_SPEC, SEM_SPEC) + (HBM_SPEC,) * n + (pl.BlockSpec(memory_space=pltpu.VMEM),),
        input_output_aliases={k: 2 + k for k in range(n)},
        compiler_params=pltpu.CompilerParams(has_side_effects=SIDE_EFFECT),
    )(*lands)
    return out[0], out[1], list(out[2:2 + n]), out[-1]


def relay_wait(started, after, name):
    send_sems, recv_sems, lands, _ = started
    n = len(lands)

    def body(*refs):
        land_refs = refs[:n]
        send_sems, recv_sems = refs[n], refs[n + 1]
        x, y, c = lax.axis_index("x"), lax.axis_index("y"), lax.axis_index("c")
        for a in range(n):
            for k in range(3):
                copy = pltpu.make_async_remote_copy(
                    src_ref=land_refs[a].at[0], dst_ref=land_refs[a].at[0], send_sem=send_sems.at[a * 3 + k],
                    recv_sem=recv_sems.at[a * 3 + k], device_id=(x, y, 1 - c), device_id_type=MESH)
                copy.wait_send()
                copy.wait_recv()

    return list(pl.pallas_call(
        body, name=name,
        out_shape=tuple(pltpu.HBM(a.shape, a.dtype) for a in lands),
        in_specs=(HBM_SPEC,) * n + (SEM_SPEC, SEM_SPEC, pl.BlockSpec(memory_space=pl.ANY)),
        out_specs=(HBM_SPEC,) * n, input_output_aliases={k: k for k in range(n)},
        compiler_params=pltpu.CompilerParams(has_side_effects=SIDE_EFFECT),
    )(*lands, send_sems, recv_sems, after))


IN_SHARD = NIN // N_DEV
SLOT_W = 768


def _slot_window(j):
    return (IN_SHARD * j // LANES) * LANES, -(-(IN_SHARD * (j + 1)) // LANES) * LANES


def _placement(j):
    a, b = _slot_window(j)
    r = lax.broadcasted_iota(jnp.int32, (SLOT_W, b - a), 0)
    c = lax.broadcasted_iota(jnp.int32, (SLOT_W, b - a), 1)
    return jnp.where(jnp.logical_and(c == r + (IN_SHARD * j - a), r < IN_SHARD), 1.0, 0.0).astype(BF16)


def assemble_w_in(land):
    tm = 256

    def body(l_ref, o_ref, acc):
        acc[...] = jnp.zeros_like(acc)
        for j in range(N_DEV):
            a, b = _slot_window(j)
            acc[:, a:b] += _nn(l_ref[j], _placement(j))
        o_ref[...] = acc[...].astype(BF16)

    return pl.pallas_call(
        body, grid=(D // tm,),
        in_specs=[pl.BlockSpec((N_DEV, tm, SLOT_W), lambda i: (0, i, 0))],
        out_specs=pl.BlockSpec((tm, NINP), lambda i: (i, 0)),
        out_shape=SDS((D, NINP), BF16),
        scratch_shapes=[pltpu.VMEM((tm, NINP), F32)],
        compiler_params=_cparams(("parallel",)), name="assemble_w_in")(land)


def scatter_w_in(dw):
    tm = 256

    def body(d_ref, o_ref):
        for j in range(N_DEV):
            a, b = _slot_window(j)
            o_ref[j] = _nt(d_ref[:, a:b], _placement(j)).astype(BF16)

    return pl.pallas_call(
        body, grid=(D // tm,),
        in_specs=[pl.BlockSpec((tm, NINP), lambda i: (i, 0))],
        out_specs=pl.BlockSpec((N_DEV, tm, SLOT_W), lambda i: (0, i, 0)),
        out_shape=SDS((N_DEV, D, SLOT_W), BF16),
        compiler_params=_cparams(("parallel",)), name="scatter_w_in")(dw)


def _adamw_math(g, w_ref, m_ref, v_ref, g_ref, d_ref, nm_ref, nv_ref):
    mn = ADAM_B1 * m_ref[...] + (1.0 - ADAM_B1) * g
    vn = ADAM_B2 * v_ref[...] + (1.0 - ADAM_B2) * jnp.square(g)
    m_hat = mn / (1.0 - ADAM_B1 ** ADAM_STEP)
    v_hat = vn / (1.0 - ADAM_B2 ** ADAM_STEP)
    g_ref[...] = g
    d_ref[...] = -ADAM_LR * (m_hat / (jnp.sqrt(v_hat) + ADAM_EPS) + ADAM_WD * w_ref[...])
    nm_ref[...] = mn
    nv_ref[...] = vn


def adamw_layers(w, slots, m, v, name):
    depth, r_len, c_len = w.shape
    cs = slots[0].shape[2]
    br = min(128, r_len)
    assert r_len % br == 0

    def body(w_ref, *rest):
        s_refs, (m_ref, v_ref, g_ref, d_ref, nm_ref, nv_ref) = rest[:depth], rest[depth:]
        layer = pl.program_id(0)
        for k in range(depth):
            @pl.when(layer == k)
            def _(k=k):
                g = s_refs[k][0, :, 0:c_len].astype(F32)
                for j in range(1, N_DEV):
                    g = g + s_refs[k][j, :, 0:c_len].astype(F32)
                _adamw_math(g, w_ref, m_ref, v_ref, g_ref, d_ref, nm_ref, nv_ref)

    spec = pl.BlockSpec((None, br, c_len), lambda l, i: (l, i, 0))
    s_specs = [pl.BlockSpec((N_DEV, br, cs), lambda l, i, k=k: (0, jnp.where(l == k, i, 0), 0))
               for k in range(depth)]
    return pl.pallas_call(
        body, grid=(depth, r_len // br),
        in_specs=[spec] + s_specs + [spec, spec],
        out_specs=[spec] * 4, out_shape=[SDS(w.shape, F32)] * 4,
        compiler_params=_cparams(("arbitrary", "arbitrary")), name=name)(w, *slots, m, v)


def adamw(w, slots, m, v, name):
    r_len, c_len = w.shape
    br = r_len if r_len <= 512 else 512
    assert r_len % br == 0

    def body(w_ref, s_ref, m_ref, v_ref, g_ref, d_ref, nm_ref, nv_ref):
        g = s_ref[0].astype(F32)
        for k in range(1, N_DEV):
            g = g + s_ref[k].astype(F32)
        _adamw_math(g, w_ref, m_ref, v_ref, g_ref, d_ref, nm_ref, nv_ref)

    spec = pl.BlockSpec((br, c_len), lambda i: (i, 0))
    return pl.pallas_call(
        body, grid=(r_len // br,),
        in_specs=[spec, pl.BlockSpec((N_DEV, br, c_len), lambda i: (0, i, 0)), spec, spec],
        out_specs=[spec] * 4, out_shape=[SDS((r_len, c_len), F32)] * 4,
        compiler_params=_cparams(("parallel",)), name=name)(w, slots, m, v)


def _adamw_nd(w, slots, m, v, name):
    shp = w.shape
    r = int(np.prod(shp[:-1]))
    outs = adamw(w.reshape(r, shp[-1]), slots.reshape(N_DEV, r, shp[-1]), m.reshape(r, shp[-1]),
                 v.reshape(r, shp[-1]), name)
    return [o.reshape(shp) for o in outs]


SMALL = [("norm_mix_w", DEPTH * D), ("ssd_conv_b", DEPTH * XBC), ("dt_bias", DEPTH * NHEAD),
         ("a_log", DEPTH * NHEAD), ("d_skip", DEPTH * NHEAD), ("ssd_norm_w", DEPTH * D),
         ("norm_mlp_w", DEPTH * D), ("final_norm_w", D)]
SMALL_LEN = sum(s for _, s in SMALL)
SMALL_ROWS = -(-SMALL_LEN // LANES)


def _pack_small(parts):
    flat = jnp.concatenate([parts[k].reshape(-1) for k, _ in SMALL])
    return jnp.pad(flat, (0, SMALL_ROWS * LANES - SMALL_LEN)).reshape(SMALL_ROWS, LANES)


def _unpack_small(packed, shapes):
    flat = packed.reshape(-1)
    out, off = {}, 0
    for k, s in SMALL:
        out[k] = flat[off:off + s].reshape(shapes[k])
        off += s
    return out


def kernel(x, norm_mix_w, w_in, short_conv_w, ssd_conv_w, ssd_conv_b, dt_bias, a_log, d_skip, ssd_norm_w, w_out, norm_mlp_w, w_up, w_down, final_norm_w, loss_target, m_norm_mix_w, m_w_in, m_short_conv_w, m_ssd_conv_w, m_ssd_conv_b, m_dt_bias, m_a_log, m_d_skip, m_ssd_norm_w, m_w_out, m_norm_mlp_w, m_w_up, m_w_down, m_final_norm_w, v_norm_mix_w, v_w_in, v_short_conv_w, v_ssd_conv_w, v_ssd_conv_b, v_dt_bias, v_a_log, v_d_skip, v_ssd_norm_w, v_w_out, v_norm_mlp_w, v_w_up, v_w_down, v_final_norm_w):
    xs = x[0]
    t_len = xs.shape[0]
    tt = min(256, t_len)
    eh, eht = _head_matrices()
    stacked = stacked_params(ssd_conv_b, dt_bias, a_log, d_skip, ssd_norm_w)
    me = 4 * lax.axis_index("x") + 2 * lax.axis_index("y") + lax.axis_index("c")

    def start_weights(i, after):
        first = exchange_start(
            [jnp.pad(w_in[i].astype(BF16), ((0, 0), (0, SLOT_W - IN_SHARD))), short_conv_w[i], ssd_conv_w[i]],
            False, "w_in_start_%d" % i, after, SIBLING_AND_CHIPS)
        rest = exchange_start([w_out[i].astype(BF16), w_up[i].astype(BF16), w_down[i].astype(BF16)], False,
                              "w_rest_start_%d" % i, first[4] if after is None else after, SIBLING_AND_CHIPS)
        return first, rest

    def fill_own(srcs, lands, per_peer):
        own = [lax.dynamic_index_in_dim(s_, me, 0, keepdims=False) for s_ in srcs] if per_peer else srcs
        return [lax.dynamic_update_index_in_dim(l_, o_, me, 0) for l_, o_ in zip(lands, own)]

    def finish_weights(started, after, name):
        srcs, lands = exchange_wait(started, after, False, name + "_wait", SIBLING_AND_CHIPS)
        return fill_own(srcs, relay_to_sibling(lands, name + "_relay"), False)

    act = xs
    saved, layers = [], []
    first, rest = start_weights(0, None)
    token = first[4][0, 0] + rest[4][0, 0]
    for i in range(DEPTH):
        g_in, g_sc, g_cw = finish_weights(first, act, "w_in_%d" % i)
        lw, prm = layer_params(
            i, assemble_w_in(g_in), g_sc.transpose(1, 0, 2).reshape(3, D), g_cw.transpose(1, 0, 2).reshape(4, XBC),
            norm_mix_w[i], norm_mlp_w[i], stacked, eh, eht)
        lw["nw1"] = lw["nw1"] + token
        proj, h1 = norm_matmul(act, lw["nw1"], lw["win"], "in_proj")
        srcs_r, lands_r = exchange_wait(rest, proj, False, "w_rest_%d_wait" % i, SIBLING_AND_CHIPS)
        relay = relay_start(lands_r, "w_rest_%d_relay_start" % i)
        y, st, aux = mixer_fwd(proj, dict(prm, nrm=prm["nrm"] + relay[3][0, 0]), tt)
        mixed = (h1, proj, (st, aux), y)
        g_out, g_up, g_dn = fill_own(srcs_r, relay_wait(relay, y, "w_rest_%d_relay_wait" % i), False)
        lw.update(wout=g_out.reshape(MIX, D), wup=g_up, wdn=g_dn.reshape(DFF, D))
        if i + 1 < DEPTH:
            first, rest = start_weights(i + 1, g_dn)
            token = first[4][0, 0] + rest[4][0, 0]
            lw["nw2"] = lw["nw2"] + token
        layers.append((lw, prm))
        act, sv = layer_fwd_mlp(act, mixed, lw)
        saved.append(sv)
    loss_acc, dx, dxb, g_fw = loss_head(act, final_norm_w[None, :], loss_target[0])

    grads = [None] * DEPTH
    sent_rest, sent_in = [None] * DEPTH, [None] * DEPTH
    token = None
    for i in reversed(range(DEPTH)):
        lw, prm = layers[i]
        if token is not None:
            lw = dict(lw, nw2=lw["nw2"] + token)
        dx1, _, dy, g_mlp = layer_bwd_mlp(dx, dxb, lw, saved[i])
        sent_rest[i] = exchange_start(
            [g_mlp["wout"].reshape(N_DEV, MIX // N_DEV, D), g_mlp["wup"], g_mlp["wdn"].reshape(N_DEV, DFF // N_DEV, D)],
            True, "g_rest_start_%d" % i)
        dx, dxb, g_mix = layer_bwd_mix(dx1, dy, lw, dict(prm, nrm=prm["nrm"] + sent_rest[i][4][0, 0]), saved[i], tt)
        grads[i] = {**g_mlp, **g_mix}
        if i > 0:
            sent_in[i] = exchange_start([scatter_w_in(g_mix["win"])], True, "g_in_start_%d" % i)
            token = sent_in[i][4][0, 0]

    def stack(k):
        return jnp.stack([g[k] for g in grads])

    small = _pack_small({"norm_mix_w": stack("nw1"), "ssd_conv_b": stack("cb"), "dt_bias": stack("dtb"),
                         "a_log": stack("alog"), "d_skip": stack("dsk"), "ssd_norm_w": stack("nrm"),
                         "norm_mlp_w": stack("nw2"), "final_norm_w": g_fw[0]})
    r_small, r_sc, r_cw = all_gather([small, stack("scw"), stack("cw")], "gather_small_grads")
    r_sc = lax.dynamic_slice_in_dim(r_sc, me * (D // N_DEV), D // N_DEV, axis=3)
    r_cw = lax.dynamic_slice_in_dim(r_cw, me * (XBC // N_DEV), XBC // N_DEV, axis=3)
    sent_in[0] = exchange_start([scatter_w_in(grads[0]["win"])], True, "g_in_start_0", after=r_small)

    after = sent_in[0][4]
    recv = [fill_own(*exchange_wait(sent_rest[i], after, True, "g_rest_wait_%d" % i), True) for i in range(DEPTH)]
    res = {}
    res["w_out"] = adamw_layers(w_out, [r[0] for r in recv], m_w_out, v_w_out, "adamw_w_out")
    res["w_up"] = adamw_layers(w_up, [r[1] for r in recv], m_w_up, v_w_up, "adamw_w_up")
    res["w_down"] = adamw_layers(w_down, [r[2] for r in recv], m_w_down, v_w_down, "adamw_w_down")
    after = res["w_down"][1]
    recv_in = [fill_own(*exchange_wait(sent_in[i], after, True, "g_in_wait_%d" % i), True)[0] for i in range(DEPTH)]
    res["w_in"] = adamw_layers(w_in, recv_in, m_w_in, v_w_in, "adamw_w_in")
    res["short_conv_w"] = _adamw_nd(short_conv_w, r_sc, m_short_conv_w, v_short_conv_w, "adamw_short_conv")
    res["ssd_conv_w"] = _adamw_nd(ssd_conv_w, r_cw, m_ssd_conv_w, v_ssd_conv_w, "adamw_ssd_conv")
    small_w = {"norm_mix_w": norm_mix_w, "ssd_conv_b": ssd_conv_b, "dt_bias": dt_bias, "a_log": a_log,
               "d_skip": d_skip, "ssd_norm_w": ssd_norm_w, "norm_mlp_w": norm_mlp_w, "final_norm_w": final_norm_w}
    small_m = {"norm_mix_w": m_norm_mix_w, "ssd_conv_b": m_ssd_conv_b, "dt_bias": m_dt_bias, "a_log": m_a_log,
               "d_skip": m_d_skip, "ssd_norm_w": m_ssd_norm_w, "norm_mlp_w": m_norm_mlp_w,
               "final_norm_w": m_final_norm_w}
    small_v = {"norm_mix_w": v_norm_mix_w, "ssd_conv_b": v_ssd_conv_b, "dt_bias": v_dt_bias, "a_log": v_a_log,
               "d_skip": v_d_skip, "ssd_norm_w": v_ssd_norm_w, "norm_mlp_w": v_norm_mlp_w,
               "final_norm_w": v_final_norm_w}
    shapes = {k: a.shape for k, a in small_w.items()}
    packed = adamw(_pack_small(small_w), r_small, _pack_small(small_m), _pack_small(small_v), "adamw_small")
    unpacked = [_unpack_small(p, shapes) for p in packed]
    for k in small_w:
        res[k] = [u[k] for u in unpacked]

    loss = lax.psum(loss_acc[0, 0], ("x", "y", "c"))
    order = ["norm_mix_w", "w_in", "short_conv_w", "ssd_conv_w", "ssd_conv_b", "dt_bias", "a_log", "d_skip",
             "ssd_norm_w", "w_out", "norm_mlp_w", "w_up", "w_down", "final_norm_w"]
    out = [loss, dx[None]]
    for part in range(4):
        out += [res[k][part] for k in order]
    return tuple(out)
```
